```python
import jax, jax.numpy as jnp
from jax import lax
import numpy as np

D_MODEL = 1024
BATCH = 8
SEQ = 2048
DEPTH = 1

ATTN_HEADS = 8
QK_NOPE_DIM = 64
QK_ROPE_DIM = 32
QK_HEAD_DIM = QK_NOPE_DIM + QK_ROPE_DIM
V_HEAD_DIM = 64
Q_LORA_RANK = D_MODEL // 4
KV_LORA_RANK = D_MODEL // 8
ROPE_THETA = 10000.0
Q_BLOCK = 128
ATTN_WIDTH = ATTN_HEADS * V_HEAD_DIM

SSM_HEADS = 8
SSM_HEAD_DIM = 64
SSM_INNER = SSM_HEADS * SSM_HEAD_DIM
SSM_GROUPS = 2
SSM_STATE = 128
SSM_CONV = 5
SSM_CHUNK = 128
SSM_CONV_CH = SSM_INNER + 2 * SSM_GROUPS * SSM_STATE

D_MIX = ATTN_WIDTH + SSM_INNER
D_FF = 4 * D_MODEL
EPS = 1e-6

IN_SPLITS = (Q_LORA_RANK, KV_LORA_RANK, QK_ROPE_DIM, SSM_INNER, SSM_CONV_CH, 2 * SSM_HEADS)
IN_WIDTH = Q_LORA_RANK + KV_LORA_RANK + QK_ROPE_DIM + SSM_INNER + SSM_CONV_CH + 2 * SSM_HEADS

kernel_name = 'hybrid_mla_mamba2_sqrelu_encoder'


def rms_norm(x, g):
    xf = x.astype(jnp.float32)
    y = xf * lax.rsqrt(jnp.mean(xf * xf, axis=-1, keepdims=True) + EPS)
    return (y * g.astype(jnp.float32)).astype(x.dtype)


def rope_cos_sin(positions, dtype):
    inv_freq = 1.0 / (ROPE_THETA ** (jnp.arange(0, QK_ROPE_DIM, 2, dtype=jnp.float32) / QK_ROPE_DIM))
    ang = positions.astype(jnp.float32)[..., None] * inv_freq
    ang = jnp.concatenate([ang, ang], axis=-1)[:, :, None, :]
    return jnp.cos(ang).astype(dtype), jnp.sin(ang).astype(dtype)


def apply_rope(x, cos, sin):
    x1, x2 = jnp.split(x, 2, axis=-1)
    return x * cos + jnp.concatenate([-x2, x1], axis=-1) * sin


def mla_attention(c_q, c_kv, k_pe, cos, sin, q_a_norm_g, w_uq, kv_a_norm_g, w_ukv, q_norm_g, k_norm_g):
    b, s, _ = c_q.shape
    q = (rms_norm(c_q, q_a_norm_g) @ w_uq).reshape(b, s, ATTN_HEADS, QK_HEAD_DIM)
    kv = (rms_norm(c_kv, kv_a_norm_g) @ w_ukv).reshape(b, s, ATTN_HEADS, QK_NOPE_DIM + V_HEAD_DIM)
    k_nope, v = jnp.split(kv, [QK_NOPE_DIM], axis=-1)
    k_pe_h = jnp.broadcast_to(k_pe[:, :, None, :], (b, s, ATTN_HEADS, QK_ROPE_DIM))
    k = jnp.concatenate([k_nope, k_pe_h], axis=-1)
    q = rms_norm(q, q_norm_g)
    k = rms_norm(k, k_norm_g)
    q = jnp.concatenate([q[..., :QK_NOPE_DIM], apply_rope(q[..., QK_NOPE_DIM:], cos, sin)], axis=-1)
    k = jnp.concatenate([k[..., :QK_NOPE_DIM], apply_rope(k[..., QK_NOPE_DIM:], cos, sin)], axis=-1)
    scale = QK_HEAD_DIM ** -0.5
    n_blk = s // Q_BLOCK
    q_blocks = jnp.moveaxis(q.reshape(b, n_blk, Q_BLOCK, ATTN_HEADS, QK_HEAD_DIM), 1, 0)

    def attend(qb):
        logits = jnp.einsum('bqhd,bkhd->bhqk', qb, k).astype(jnp.float32) * scale
        p = jax.nn.softmax(logits, axis=-1).astype(v.dtype)
        return jnp.einsum('bhqk,bkhd->bqhd', p, v)

    o = lax.map(attend, q_blocks)
    return jnp.moveaxis(o, 0, 1).reshape(b, s, ATTN_WIDTH)


def ssd_scan(x, dt, a, bmat, cmat):
    b, s, h, p = x.shape
    g, n = bmat.shape[-2], bmat.shape[-1]
    r = h // g
    c = s // SSM_CHUNK
    L = SSM_CHUNK
    xdt = (x * dt[..., None]).reshape(b, c, L, g, r, p)
    da = jnp.moveaxis((dt * a).reshape(b, c, L, g, r), 2, -1)
    a_cs = jnp.cumsum(da, axis=-1)
    bm = bmat.reshape(b, c, L, g, n)
    cm = cmat.reshape(b, c, L, g, n)
    seg = a_cs[..., :, None] - a_cs[..., None, :]
    lower = jnp.tril(jnp.ones((L, L), dtype=bool))
    decay_in = jnp.exp(jnp.where(lower, seg, -jnp.inf))
    cb = jnp.einsum('bclgn,bcsgn->bcgls', cm, bm)
    y_diag = jnp.einsum('bcgls,bcgrls,bcsgrp->bclgrp', cb, decay_in, xdt)
    decay_to_end = jnp.exp(a_cs[..., -1:] - a_cs)
    chunk_states = jnp.einsum('bclgn,bcgrl,bclgrp->bcgrpn', bm, decay_to_end, xdt)
    chunk_decay = jnp.exp(a_cs[..., -1])

    def step(state, inp):
        dec, new = inp
        return state * dec[..., None, None] + new, state

    init = jnp.zeros((b, g, r, p, n), dtype=chunk_states.dtype)
    _, prev = lax.scan(step, init, (jnp.moveaxis(chunk_decay, 1, 0), jnp.moveaxis(chunk_states, 1, 0)))
    prev = jnp.moveaxis(prev, 0, 1)
    y_off = jnp.einsum('bclgn,bcgrpn,bcgrl->bclgrp', cm, prev, jnp.exp(a_cs))
    return (y_diag + y_off).reshape(b, s, h, p)


def mamba2_mixer(z, xbc, dt_raw, conv_w, conv_b, a_log_fwd, a_log_bwd, dt_bias_fwd, dt_bias_bwd, d_skip, ssm_norm_g):
    b, s, _ = z.shape
    xbc = lax.conv_general_dilated(
        xbc, conv_w, window_strides=(1,), padding=[(SSM_CONV // 2, SSM_CONV // 2)],
        dimension_numbers=('NWC', 'WIO', 'NWC'), feature_group_count=SSM_CONV_CH) + conv_b
    xbc = jax.nn.silu(xbc)
    xs, bm, cm = jnp.split(xbc, [SSM_INNER, SSM_INNER + SSM_GROUPS * SSM_STATE], axis=-1)
    xs = xs.reshape(b, s, SSM_HEADS, SSM_HEAD_DIM)
    bm = bm.reshape(b, s, SSM_GROUPS, SSM_STATE)
    cm = cm.reshape(b, s, SSM_GROUPS, SSM_STATE)
    dt_f, dt_b = jnp.split(dt_raw, 2, axis=-1)
    dt_f = jax.nn.softplus(dt_f + dt_bias_fwd)
    dt_b = jax.nn.softplus(dt_b + dt_bias_bwd)
    y_f = ssd_scan(xs, dt_f, -jnp.exp(a_log_fwd), bm, cm)
    flip = lambda t: jnp.flip(t, axis=1)
    y_b = flip(ssd_scan(flip(xs), flip(dt_b), -jnp.exp(a_log_bwd), flip(bm), flip(cm)))
    y = y_f + y_b + d_skip[:, None] * xs
    y = y.reshape(b, s, SSM_INNER) * jax.nn.silu(z)
    y = rms_norm(y.reshape(b, s, SSM_GROUPS, SSM_INNER // SSM_GROUPS),
                 ssm_norm_g.reshape(SSM_GROUPS, SSM_INNER // SSM_GROUPS))
    return y.reshape(b, s, SSM_INNER)


def hybrid_layer(x, cos, sin, ln_mix_g, w_in, q_a_norm_g, w_uq, kv_a_norm_g, w_ukv, q_norm_g, k_norm_g,
                 attn_out_norm_g, conv_w, conv_b, a_log_fwd, a_log_bwd, dt_bias_fwd, dt_bias_bwd, d_skip,
                 ssm_norm_g, w_out, ln_mlp_g, w_mlp_up, w_mlp_down):
    h = rms_norm(x, ln_mix_g)
    proj = h @ w_in
    split_idx = np.cumsum(IN_SPLITS)[:-1].tolist()
    c_q, c_kv, k_pe, z, xbc, dt_raw = jnp.split(proj, split_idx, axis=-1)
    attn = mla_attention(c_q, c_kv, k_pe, cos, sin, q_a_norm_g, w_uq, kv_a_norm_g, w_ukv, q_norm_g, k_norm_g)
    attn = rms_norm(attn, attn_out_norm_g)
    ssm = mamba2_mixer(z, xbc, dt_raw, conv_w, conv_b, a_log_fwd, a_log_bwd, dt_bias_fwd, dt_bias_bwd,
                       d_skip, ssm_norm_g)
    x = x + jnp.concatenate([attn, ssm], axis=-1) @ w_out
    hm = rms_norm(x, ln_mlp_g)
    x = x + jnp.square(jax.nn.relu(hm @ w_mlp_up)) @ w_mlp_down
    return x


def setup_inputs(seed: int = 0) -> dict:
    key = jax.random.key(seed)
    ks = jax.random.split(key, 24)
    f32 = jnp.float32

    def nrm(k, shape, fan_in):
        return jax.random.normal(k, shape, f32) * (fan_in ** -0.5)

    def gain(k, shape):
        return 1.0 + 0.02 * jax.random.normal(k, shape, f32)

    x = jax.random.normal(ks[0], (BATCH, SEQ, D_MODEL), f32)
    positions = (jnp.arange(SEQ, dtype=jnp.int32)[None, :]
                 + jax.random.randint(ks[1], (BATCH, 1), 0, 4096, dtype=jnp.int32))
    a_log_fwd = jnp.log(jax.random.uniform(ks[13], (DEPTH, SSM_HEADS), f32, 1.0, 16.0))
    a_log_bwd = jnp.log(jax.random.uniform(ks[14], (DEPTH, SSM_HEADS), f32, 1.0, 16.0))

    def dt_bias(k):
        dt = jnp.exp(jax.random.uniform(k, (DEPTH, SSM_HEADS), f32, np.log(1e-3), np.log(1e-1)))
        return dt + jnp.log(-jnp.expm1(-dt))

    return {
        'x': x,
        'positions': positions,
        'ln_mix_g': gain(ks[2], (DEPTH, D_MODEL)),
        'w_in': nrm(ks[3], (DEPTH, D_MODEL, IN_WIDTH), D_MODEL),
        'q_a_norm_g': gain(ks[4], (DEPTH, Q_LORA_RANK)),
        'w_uq': nrm(ks[5], (DEPTH, Q_LORA_RANK, ATTN_HEADS * QK_HEAD_DIM), Q_LORA_RANK),
        'kv_a_norm_g': gain(ks[6], (DEPTH, KV_LORA_RANK)),
        'w_ukv': nrm(ks[7], (DEPTH, KV_LORA_RANK, ATTN_HEADS * (QK_NOPE_DIM + V_HEAD_DIM)), KV_LORA_RANK),
        'q_norm_g': gain(ks[8], (DEPTH, QK_HEAD_DIM)),
        'k_norm_g': gain(ks[9], (DEPTH, QK_HEAD_DIM)),
        'attn_out_norm_g': gain(ks[10], (DEPTH, ATTN_WIDTH)),
        'conv_w': nrm(ks[11], (DEPTH, SSM_CONV, 1, SSM_CONV_CH), SSM_CONV),
        'conv_b': 0.02 * jax.random.normal(ks[12], (DEPTH, SSM_CONV_CH), f32),
        'a_log_fwd': a_log_fwd,
        'a_log_bwd': a_log_bwd,
        'dt_bias_fwd': dt_bias(ks[15]),
        'dt_bias_bwd': dt_bias(ks[16]),
        'd_skip': gain(ks[17], (DEPTH, SSM_HEADS)),
        'ssm_norm_g': gain(ks[18], (DEPTH, SSM_INNER)),
        'w_out': nrm(ks[19], (DEPTH, D_MIX, D_MODEL), D_MIX),
        'ln_mlp_g': gain(ks[20], (DEPTH, D_MODEL)),
        'w_mlp_up': nrm(ks[21], (DEPTH, D_MODEL, D_FF), D_MODEL),
        'w_mlp_down': nrm(ks[22], (DEPTH, D_FF, D_MODEL), D_FF),
    }


def reference(x, positions, ln_mix_g, w_in, q_a_norm_g, w_uq, kv_a_norm_g, w_ukv, q_norm_g, k_norm_g,
              attn_out_norm_g, conv_w, conv_b, a_log_fwd, a_log_bwd, dt_bias_fwd, dt_bias_bwd, d_skip,
              ssm_norm_g, w_out, ln_mlp_g, w_mlp_up, w_mlp_down):
    cos, sin = rope_cos_sin(positions, x.dtype)
    for l in range(DEPTH):
        x = hybrid_layer(x, cos, sin, ln_mix_g[l], w_in[l], q_a_norm_g[l], w_uq[l], kv_a_norm_g[l], w_ukv[l],
                         q_norm_g[l], k_norm_g[l], attn_out_norm_g[l], conv_w[l], conv_b[l], a_log_fwd[l],
                         a_log_bwd[l], dt_bias_fwd[l], dt_bias_bwd[l], d_skip[l], ssm_norm_g[l], w_out[l],
                         ln_mlp_g[l], w_mlp_up[l], w_mlp_down[l])
    return x
```

```python
import functools

import numpy as np
import jax
import jax.numpy as jnp
from jax import lax
from jax.experimental import pallas as pl
from jax.experimental.pallas import tpu as pltpu

F32 = jnp.float32
BF16 = jnp.bfloat16

D_MODEL = 1024
ATTN_HEADS = 8
QK_NOPE_DIM = 64
QK_ROPE_DIM = 32
QK_HEAD_DIM = QK_NOPE_DIM + QK_ROPE_DIM
V_HEAD_DIM = 64
Q_LORA_RANK = D_MODEL // 4
KV_LORA_RANK = D_MODEL // 8
ROPE_THETA = 10000.0
ATTN_WIDTH = ATTN_HEADS * V_HEAD_DIM
SSM_HEADS = 8
SSM_HEAD_DIM = 64
SSM_INNER = SSM_HEADS * SSM_HEAD_DIM
SSM_GROUPS = 2
SSM_STATE = 128
SSM_CONV = 5
SSM_CHUNK = 128
SSM_CONV_CH = SSM_INNER + 2 * SSM_GROUPS * SSM_STATE
D_MIX = ATTN_WIDTH + SSM_INNER
D_FF = 4 * D_MODEL
EPS = 1e-6

LANES = 128
HEAD_PAD = LANES

HEADS_PER_GROUP = SSM_HEADS // SSM_GROUPS
GROUP_INNER = SSM_INNER // SSM_GROUPS
GROUP_ROWS = GROUP_INNER + 2 * SSM_STATE
DT_ROWS = 2 * HEADS_PER_GROUP
SSM_T_ROWS = SSM_INNER + SSM_CONV_CH + 2 * SSM_HEADS

TM_IN = 512
TQ = 512
TM_MLP = 512
FF_CHUNK = 1024
CONV_ROWS = 64
VMEM_LIMIT = 56 * 1024 * 1024


def _rms(x):
    return x * lax.rsqrt(jnp.mean(x * x, axis=-1, keepdims=True) + EPS)


def _dot(a, b):
    return jnp.dot(a, b, preferred_element_type=F32)


def _dot_nt(a, b):
    return lax.dot_general(a, b, (((1,), (1,)), ((), ())), preferred_element_type=F32)


def _inproj_kernel(x_ref, cos_ref, sin_ref, g_ref, wtok_ref, wssm_ref, gqa_ref, wuq_ref,
                   gkva_ref, wuk_ref, wuv_ref, gq_ref, gk_ref,
                   q_ref, k_ref, v_ref, z_ref, xbc_ref, dt_ref):
    tm = x_ref.shape[0]
    h = (_rms(x_ref[...]) * g_ref[...]).astype(BF16)

    ssm = _dot_nt(wssm_ref[...], h)
    for c in range(tm // SSM_CHUNK):
        cols = slice(c * SSM_CHUNK, (c + 1) * SSM_CHUNK)
        z_ref[c] = ssm[0:SSM_INNER, cols]
        xbc_ref[c] = ssm[SSM_INNER:SSM_INNER + SSM_CONV_CH, cols]
        dt_ref[c] = ssm[SSM_INNER + SSM_CONV_CH:SSM_T_ROWS, cols]

    tok = _dot(h, wtok_ref[...])
    cq = tok[:, 0:Q_LORA_RANK]
    ckv = tok[:, Q_LORA_RANK:Q_LORA_RANK + KV_LORA_RANK]
    kpe = tok[:, Q_LORA_RANK + KV_LORA_RANK:]
    cqn = (_rms(cq) * gqa_ref[...]).astype(BF16)
    ckvn = (_rms(ckv) * gkva_ref[...]).astype(BF16)
    q_pre = _dot(cqn, wuq_ref[...])
    k_pre = _dot(ckvn, wuk_ref[...])
    v_ref[...] = _dot(ckvn, wuv_ref[...]).astype(BF16)

    cosp = cos_ref[...]
    sinp = sin_ref[...]
    lane = lax.broadcasted_iota(jnp.int32, (tm, HEAD_PAD), 1)
    first_half = lane < QK_NOPE_DIM + QK_ROPE_DIM // 2
    half = QK_ROPE_DIM // 2

    def norm_rope(xh, g):
        ssq = jnp.sum(xh * xh, axis=-1, keepdims=True)
        xn = xh * lax.rsqrt(ssq * (1.0 / QK_HEAD_DIM) + EPS) * g
        rot = jnp.where(first_half, -pltpu.roll(xn, HEAD_PAD - half, 1), pltpu.roll(xn, half, 1))
        return xn * cosp + rot * sinp

    scale = QK_HEAD_DIM ** -0.5
    gq = gq_ref[...]
    gk = gk_ref[...]
    for hh in range(ATTN_HEADS):
        sl = slice(hh * HEAD_PAD, (hh + 1) * HEAD_PAD)
        q_ref[:, sl] = (norm_rope(q_pre[:, sl], gq) * scale).astype(BF16)
        k_ref[:, sl] = norm_rope(k_pre[:, sl] + kpe, gk).astype(BF16)


def _inproj_call(x2, cosp, sinp, g, wtok, wssm, gqa, wuq, gkva, wuk, wuv, gq, gk):
    t = x2.shape[0]
    n_tiles = t // TM_IN
    cpt = TM_IN // SSM_CHUNK
    nchunks = t // SSM_CHUNK
    full = lambda a: pl.BlockSpec(a.shape, lambda i: (0,) * a.ndim)
    row = lambda w: pl.BlockSpec((TM_IN, w), lambda i: (i, 0))
    chunked = lambda r: pl.BlockSpec((cpt, r, SSM_CHUNK), lambda i: (i, 0, 0))
    return pl.pallas_call(
        _inproj_kernel,
        grid=(n_tiles,),
        in_specs=[row(D_MODEL), row(HEAD_PAD), row(HEAD_PAD), full(g), full(wtok), full(wssm),
                  full(gqa), full(wuq), full(gkva), full(wuk), full(wuv), full(gq), full(gk)],
        out_specs=[row(ATTN_HEADS * HEAD_PAD), row(ATTN_HEADS * HEAD_PAD), row(ATTN_WIDTH),
                   chunked(SSM_INNER), chunked(SSM_CONV_CH), chunked(2 * SSM_HEADS)],
        out_shape=[jax.ShapeDtypeStruct((t, ATTN_HEADS * HEAD_PAD), BF16),
                   jax.ShapeDtypeStruct((t, ATTN_HEADS * HEAD_PAD), BF16),
                   jax.ShapeDtypeStruct((t, ATTN_WIDTH), BF16),
                   jax.ShapeDtypeStruct((nchunks, SSM_INNER, SSM_CHUNK), F32),
                   jax.ShapeDtypeStruct((nchunks, SSM_CONV_CH, SSM_CHUNK), F32),
                   jax.ShapeDtypeStruct((nchunks, 2 * SSM_HEADS, SSM_CHUNK), F32)],
        compiler_params=pltpu.CompilerParams(dimension_semantics=("arbitrary",),
                                             vmem_limit_bytes=VMEM_LIMIT),
        name="inproj",
    )(x2, cosp, sinp, g, wtok, wssm, gqa, wuq, gkva, wuk, wuv, gq, gk)


def _attn_kernel(q_ref, k_ref, v_ref, o_ref):
    outs = []
    for j in range(2):
        q = q_ref[:, j * HEAD_PAD:(j + 1) * HEAD_PAD]
        k = k_ref[:, j * HEAD_PAD:(j + 1) * HEAD_PAD]
        s = _dot_nt(q, k)
        m = jnp.max(s, axis=-1, keepdims=True)
        p = jnp.exp(s - m)
        l = jnp.sum(p, axis=-1, keepdims=True)
        o = _dot(p.astype(BF16), v_ref[:, j * V_HEAD_DIM:(j + 1) * V_HEAD_DIM])
        outs.append(o / l)
    o_ref[...] = jnp.concatenate(outs, axis=1)


def _attn_call(q, k, v, batch, seq):
    n_q = seq // TQ
    return pl.pallas_call(
        _attn_kernel,
        grid=(batch, ATTN_HEADS // 2, n_q),
        in_specs=[pl.BlockSpec((TQ, 2 * HEAD_PAD), lambda b, hp, i: (b * n_q + i, hp)),
                  pl.BlockSpec((seq, 2 * HEAD_PAD), lambda b, hp, i: (b, hp)),
                  pl.BlockSpec((seq, 2 * V_HEAD_DIM), lambda b, hp, i: (b, hp))],
        out_specs=pl.BlockSpec((TQ, 2 * V_HEAD_DIM), lambda b, hp, i: (b * n_q + i, hp)),
        out_shape=jax.ShapeDtypeStruct((batch * seq, ATTN_WIDTH), F32),
        compiler_params=pltpu.CompilerParams(
            dimension_semantics=("arbitrary", "arbitrary", "arbitrary"),
            vmem_limit_bytes=VMEM_LIMIT),
        name="attention",
    )(q, k, v)


def _split3(x):
    hi = x.astype(BF16)
    r1 = x - hi.astype(F32)
    mid = r1.astype(BF16)
    lo = (r1 - mid.astype(F32)).astype(BF16)
    return hi, mid, lo


def _ssd_kernel(z_ref, xbc_ref, dt_ref, cw_ref, cbias_ref, alog_ref, dtbias_ref, dskip_ref, gn_ref,
                o_ref, xc_scr, yb_scr, cs_scr, dtv_scr, colcs_scr, coldt_scr, stf_scr, stb_scr):
    nc = xbc_ref.shape[0]
    L = SSM_CHUNK
    hpg = HEADS_PER_GROUP
    P = SSM_HEAD_DIM

    lane_c = lax.broadcasted_iota(jnp.int32, (CONV_ROWS, L), 1)
    zeros_c = jnp.zeros((CONV_ROWS, L), F32)
    for j in range(nc):
        def conv_body(r, carry, j=j):
            rs = pl.ds(pl.multiple_of(r * CONV_ROWS, CONV_ROWS), CONV_ROWS)
            cur = xbc_ref[j, rs, :]
            prev = xbc_ref[j - 1, rs, :] if j > 0 else zeros_c
            nxt = xbc_ref[j + 1, rs, :] if j < nc - 1 else zeros_c
            xm2 = jnp.where(lane_c < 2, pltpu.roll(prev, 2, 1), pltpu.roll(cur, 2, 1))
            xm1 = jnp.where(lane_c < 1, pltpu.roll(prev, 1, 1), pltpu.roll(cur, 1, 1))
            xp1 = jnp.where(lane_c >= L - 1, pltpu.roll(nxt, L - 1, 1), pltpu.roll(cur, L - 1, 1))
            xp2 = jnp.where(lane_c >= L - 2, pltpu.roll(nxt, L - 2, 1), pltpu.roll(cur, L - 2, 1))
            acc = (cbias_ref[rs, :] + cw_ref[0, rs, :] * xm2 + cw_ref[1, rs, :] * xm1
                   + cw_ref[2, rs, :] * cur + cw_ref[3, rs, :] * xp1 + cw_ref[4, rs, :] * xp2)
            xc_scr[j, rs, :] = acc * jax.nn.sigmoid(acc)
            return carry
        lax.fori_loop(0, GROUP_ROWS // CONV_ROWS, conv_body, 0)

    a_neg = -jnp.exp(alog_ref[...])
    dtv = jax.nn.softplus(dt_ref[...] + dtbias_ref[...][None])
    dtv_scr[...] = dtv
    da2 = (dtv * a_neg[None]).reshape(nc * DT_ROWS, L)
    ri = lax.broadcasted_iota(jnp.int32, (L, L), 0)
    ci = lax.broadcasted_iota(jnp.int32, (L, L), 1)
    upper = (ri <= ci).astype(BF16)
    lower = (ri >= ci).astype(BF16)
    ident = (ri == ci).astype(BF16)
    dparts = _split3(da2)
    cs_f = sum(_dot(p, upper) for p in dparts)
    cs_b = sum(_dot(p, lower) for p in dparts)
    cs_scr[...] = jnp.where((ri & hpg) == 0, cs_f, cs_b).reshape(nc, DT_ROWS, L)
    col_f = sum(_dot_nt(lower, p) for p in dparts)
    col_b = sum(_dot_nt(upper, p) for p in dparts)
    colcs_scr[...] = jnp.where((ci & hpg) == 0, col_f, col_b)
    coldt_scr[...] = sum(_dot_nt(ident, p) for p in _split3(dtv.reshape(nc * DT_ROWS, L)))

    head_rows = [slice(h * P, (h + 1) * P) for h in range(hpg)]
    x_rows = slice(0, GROUP_INNER)
    b_rows = slice(GROUP_INNER, GROUP_INNER + SSM_STATE)
    c_rows = slice(GROUP_INNER + SSM_STATE, GROUP_ROWS)

    def lane_bcast(col):
        return jnp.broadcast_to(col, (col.shape[0], L))

    def scan_step(c, st_scr, cs_d, dt_d, tot):
        xs = xc_scr[c, x_rows, :]
        bt = xc_scr[c, b_rows, :].astype(BF16)
        ct = xc_scr[c, c_rows, :].astype(BF16)
        st = st_scr[...]
        yoff = _dot(st.astype(BF16), ct)
        scale = jnp.exp(cs_d)
        w = dt_d * jnp.exp(tot - cs_d)
        dec = jnp.exp(tot)
        xw = jnp.concatenate([xs[head_rows[h]] * w[h:h + 1, :] for h in range(hpg)], axis=0)
        new = _dot_nt(xw.astype(BF16), bt)
        for h in range(hpg):
            st_scr[head_rows[h], :] = st[head_rows[h]] * dec[h:h + 1, :] + new[head_rows[h]]
        return jnp.concatenate([yoff[head_rows[h]] * scale[h:h + 1, :] for h in range(hpg)], axis=0)

    stb_scr[...] = jnp.zeros_like(stb_scr)

    def bwd_body(i, carry):
        c = nc - 1 - i
        cs_d = cs_scr[c, hpg:DT_ROWS, :]
        dt_d = dtv_scr[c, hpg:DT_ROWS, :]
        tot = lane_bcast(cs_d[:, 0:1])
        yb_scr[c] = scan_step(c, stb_scr, cs_d, dt_d, tot)
        return carry
    lax.fori_loop(0, nc, bwd_body, 0)

    stf_scr[...] = jnp.zeros_like(stf_scr)
    mask_f = ci >= ri
    mask_b = ci <= ri
    neg_inf = jnp.float32(-jnp.inf)

    def fwd_body(c, carry):
        cs_f_r = cs_scr[c, 0:hpg, :]
        dt_f_r = dtv_scr[c, 0:hpg, :]
        cs_b_r = cs_scr[c, hpg:DT_ROWS, :]
        tot = lane_bcast(cs_f_r[:, L - 1:L])
        y = scan_step(c, stf_scr, cs_f_r, dt_f_r, tot) + yb_scr[c]
        xs = xc_scr[c, x_rows, :]
        bm = xc_scr[c, b_rows, :].T.astype(BF16)
        ct = xc_scr[c, c_rows, :].astype(BF16)
        cbt = _dot(bm, ct)
        shift = (L - c * DT_ROWS) % L
        colc = pltpu.roll(colcs_scr[...], shift, 1)
        cold = pltpu.roll(coldt_scr[...], shift, 1)
        ys = []
        for h in range(hpg):
            ef = jnp.exp(jnp.where(mask_f, cs_f_r[h:h + 1, :] - colc[:, h:h + 1], neg_inf))
            eb = jnp.exp(jnp.where(mask_b, cs_b_r[h:h + 1, :] - colc[:, hpg + h:hpg + h + 1], neg_inf))
            wt = cbt * (ef * cold[:, h:h + 1] + eb * cold[:, hpg + h:hpg + h + 1])
            ys.append(_dot(xs[head_rows[h]].astype(BF16), wt.astype(BF16)))
        y = y + jnp.concatenate(ys, axis=0) + dskip_ref[...] * xs
        zc = z_ref[c]
        y = y * (zc * jax.nn.sigmoid(zc))
        yn = y * lax.rsqrt(jnp.mean(y * y, axis=0, keepdims=True) + EPS) * gn_ref[...]
        o_ref[pl.ds(pl.multiple_of(c * L, L), L), :] = yn.T
        return carry
    lax.fori_loop(0, nc, fwd_body, 0)


def _ssd_call(zc, xbcc, dtc, cw, cbias, alog, dtbias, dskip, gn, batch, seq):
    nc = seq // SSM_CHUNK
    L = SSM_CHUNK
    grp = lambda r: pl.BlockSpec((r, L), lambda b, g: (g, 0))
    return pl.pallas_call(
        _ssd_kernel,
        grid=(batch, SSM_GROUPS),
        in_specs=[pl.BlockSpec((nc, GROUP_INNER, L), lambda b, g: (b, g, 0)),
                  pl.BlockSpec((nc, GROUP_ROWS, L), lambda b, g: (b, g, 0)),
                  pl.BlockSpec((nc, DT_ROWS, L), lambda b, g: (b, g, 0)),
                  pl.BlockSpec((SSM_CONV, GROUP_ROWS, L), lambda b, g: (0, g, 0)),
                  grp(GROUP_ROWS), grp(DT_ROWS), grp(DT_ROWS), grp(GROUP_INNER), grp(GROUP_INNER)],
        out_specs=pl.BlockSpec((seq, GROUP_INNER), lambda b, g: (b, g)),
        out_shape=jax.ShapeDtypeStruct((batch * seq, SSM_INNER), F32),
        scratch_shapes=[pltpu.VMEM((nc, GROUP_ROWS, L), F32),
                        pltpu.VMEM((nc, GROUP_INNER, L), F32),
                        pltpu.VMEM((nc, DT_ROWS, L), F32),
                        pltpu.VMEM((nc, DT_ROWS, L), F32),
                        pltpu.VMEM((L, nc * DT_ROWS), F32),
                        pltpu.VMEM((L, nc * DT_ROWS), F32),
                        pltpu.VMEM((GROUP_INNER, SSM_STATE), F32),
                        pltpu.VMEM((GROUP_INNER, SSM_STATE), F32)],
        compiler_params=pltpu.CompilerParams(dimension_semantics=("arbitrary", "arbitrary"),
                                             vmem_limit_bytes=VMEM_LIMIT),
        name="ssd",
    )(zc, xbcc, dtc, cw, cbias, alog, dtbias, dskip, gn)


def _mlp_kernel(x_ref, attn_ref, ssm_ref, gat_ref, wo_ref, gm_ref, wup_ref, wdn_ref, o_ref):
    an = (_rms(attn_ref[...]) * gat_ref[...]).astype(BF16)
    mix = _dot(an, wo_ref[0:ATTN_WIDTH, :]) + _dot(ssm_ref[...].astype(BF16), wo_ref[ATTN_WIDTH:D_MIX, :])
    x1 = x_ref[...] + mix
    hm = (_rms(x1) * gm_ref[...]).astype(BF16)
    acc = jnp.zeros_like(x1)
    for c in range(D_FF // FF_CHUNK):
        cols = slice(c * FF_CHUNK, (c + 1) * FF_CHUNK)
        u = _dot(hm, wup_ref[:, cols])
        acc = acc + _dot(jnp.square(jnp.maximum(u, 0.0)).astype(BF16), wdn_ref[cols, :])
    o_ref[...] = x1 + acc


def _mlp_call(x2, attn, ssm, gat, wo, gm, wup, wdn):
    t = x2.shape[0]
    row = lambda w: pl.BlockSpec((TM_MLP, w), lambda i: (i, 0))
    full = lambda a: pl.BlockSpec(a.shape, lambda i: (0, 0), pipeline_mode=pl.Buffered(1))
    return pl.pallas_call(
        _mlp_kernel,
        grid=(t // TM_MLP,),
        in_specs=[row(D_MODEL), row(ATTN_WIDTH), row(SSM_INNER), full(gat), full(wo), full(gm),
                  full(wup), full(wdn)],
        out_specs=row(D_MODEL),
        out_shape=jax.ShapeDtypeStruct((t, D_MODEL), F32),
        compiler_params=pltpu.CompilerParams(dimension_semantics=("arbitrary",),
                                             vmem_limit_bytes=VMEM_LIMIT),
        name="outproj_mlp",
    )(x2, attn, ssm, gat, wo, gm, wup, wdn)


def _ssm_row_perm():
    xbc, dt = [], []
    for g in range(SSM_GROUPS):
        xbc += list(range(g * GROUP_INNER, (g + 1) * GROUP_INNER))
        xbc += list(range(SSM_INNER + g * SSM_STATE, SSM_INNER + (g + 1) * SSM_STATE))
        xbc += list(range(SSM_INNER + SSM_GROUPS * SSM_STATE + g * SSM_STATE,
                          SSM_INNER + SSM_GROUPS * SSM_STATE + (g + 1) * SSM_STATE))
        dt += list(range(g * HEADS_PER_GROUP, (g + 1) * HEADS_PER_GROUP))
        dt += list(range(SSM_HEADS + g * HEADS_PER_GROUP, SSM_HEADS + (g + 1) * HEADS_PER_GROUP))
    return np.array(xbc), np.array(dt)


def _lane_bcast(v):
    return jnp.broadcast_to(v[..., None], v.shape + (LANES,)).astype(F32)


def _layer(x2, cosp, sinp, batch, seq, ln_mix_g, w_in, q_a_norm_g, w_uq, kv_a_norm_g, w_ukv, q_norm_g,
           k_norm_g, attn_out_norm_g, conv_w, conv_b, a_log_fwd, a_log_bwd, dt_bias_fwd, dt_bias_bwd,
           d_skip, ssm_norm_g, w_out, ln_mlp_g, w_mlp_up, w_mlp_down):
    xbc_perm, dt_perm = _ssm_row_perm()
    o_cq, o_ckv, o_kpe = 0, Q_LORA_RANK, Q_LORA_RANK + KV_LORA_RANK
    o_z = o_kpe + QK_ROPE_DIM
    o_xbc = o_z + SSM_INNER
    o_dt = o_xbc + SSM_CONV_CH
    pad_rope = HEAD_PAD - QK_HEAD_DIM

    kpe_pad = jnp.pad(w_in[:, o_kpe:o_z], ((0, 0), (QK_NOPE_DIM, pad_rope)))
    wtok = jnp.concatenate([w_in[:, o_cq:o_kpe], kpe_pad], axis=1).astype(BF16)
    wssm = jnp.concatenate([w_in[:, o_z:o_xbc], w_in[:, o_xbc:o_dt][:, xbc_perm],
                            w_in[:, o_dt:][:, dt_perm]], axis=1).T.astype(BF16)
    wuq = jnp.pad(w_uq.reshape(Q_LORA_RANK, ATTN_HEADS, QK_HEAD_DIM),
                  ((0, 0), (0, 0), (0, pad_rope))).reshape(Q_LORA_RANK, -1).astype(BF16)
    wukv = w_ukv.reshape(KV_LORA_RANK, ATTN_HEADS, QK_NOPE_DIM + V_HEAD_DIM)
    wuk = jnp.pad(wukv[:, :, :QK_NOPE_DIM],
                  ((0, 0), (0, 0), (0, HEAD_PAD - QK_NOPE_DIM))).reshape(KV_LORA_RANK, -1).astype(BF16)
    wuv = wukv[:, :, QK_NOPE_DIM:].reshape(KV_LORA_RANK, -1).astype(BF16)
    gq = jnp.pad(q_norm_g, (0, pad_rope))[None, :]
    gk = jnp.pad(k_norm_g, (0, pad_rope))[None, :]

    q, k, v, zc, xbcc, dtc = _inproj_call(
        x2, cosp, sinp, ln_mix_g[None, :], wtok, wssm, q_a_norm_g[None, :], wuq,
        kv_a_norm_g[None, :], wuk, wuv, gq, gk)

    attn = _attn_call(q, k, v, batch, seq)

    cw = _lane_bcast(conv_w[:, 0, :][:, xbc_perm])
    cbias = _lane_bcast(conv_b[xbc_perm])
    alog = _lane_bcast(jnp.concatenate([a_log_fwd, a_log_bwd])[dt_perm])
    dtbias = _lane_bcast(jnp.concatenate([dt_bias_fwd, dt_bias_bwd])[dt_perm])
    dskip = _lane_bcast(jnp.repeat(d_skip, SSM_HEAD_DIM))
    gn = _lane_bcast(ssm_norm_g)
    ssm = _ssd_call(zc, xbcc, dtc, cw, cbias, alog, dtbias, dskip, gn, batch, seq)

    return _mlp_call(x2, attn, ssm, attn_out_norm_g[None, :], w_out.astype(BF16), ln_mlp_g[None, :],
                     w_mlp_up.astype(BF16), w_mlp_down.astype(BF16))


def _rope_tables(positions):
    inv_freq = 1.0 / (ROPE_THETA ** (jnp.arange(0, QK_ROPE_DIM, 2, dtype=F32) / QK_ROPE_DIM))
    ang = positions.astype(F32).reshape(-1, 1) * inv_freq
    ang = jnp.concatenate([ang, ang], axis=-1)
    t = ang.shape[0]
    pad = HEAD_PAD - QK_HEAD_DIM
    cosp = jnp.concatenate([jnp.ones((t, QK_NOPE_DIM), F32), jnp.cos(ang), jnp.zeros((t, pad), F32)], axis=1)
    sinp = jnp.concatenate([jnp.zeros((t, QK_NOPE_DIM), F32), jnp.sin(ang), jnp.zeros((t, pad), F32)], axis=1)
    return cosp, sinp


def kernel(x, positions, ln_mix_g, w_in, q_a_norm_g, w_uq, kv_a_norm_g, w_ukv, q_norm_g, k_norm_g,
           attn_out_norm_g, conv_w, conv_b, a_log_fwd, a_log_bwd, dt_bias_fwd, dt_bias_bwd, d_skip,
           ssm_norm_g, w_out, ln_mlp_g, w_mlp_up, w_mlp_down):
    batch, seq, d = x.shape
    assert d == D_MODEL and seq % TQ == 0 and (batch * seq) % TM_IN == 0 and (batch * seq) % TM_MLP == 0
    cosp, sinp = _rope_tables(positions)
    x2 = x.reshape(batch * seq, d)
    for l in range(ln_mix_g.shape[0]):
        x2 = _layer(x2, cosp, sinp, batch, seq, ln_mix_g[l], w_in[l], q_a_norm_g[l], w_uq[l], kv_a_norm_g[l],
                    w_ukv[l], q_norm_g[l], k_norm_g[l], attn_out_norm_g[l], conv_w[l], conv_b[l],
                    a_log_fwd[l], a_log_bwd[l], dt_bias_fwd[l], dt_bias_bwd[l], d_skip[l], ssm_norm_g[l],
                    w_out[l], ln_mlp_g[l], w_mlp_up[l], w_mlp_down[l])
    return x2.reshape(batch, seq, d)
```

```python
import numpy as np
import jax
import jax.numpy as jnp
from jax import lax
from jax.experimental import pallas as pl
from jax.experimental.pallas import tpu as pltpu

F32 = jnp.float32
BF16 = jnp.bfloat16

D_MODEL = 1024
ATTN_HEADS = 8
QK_NOPE_DIM = 64
QK_ROPE_DIM = 32
QK_HEAD_DIM = QK_NOPE_DIM + QK_ROPE_DIM
V_HEAD_DIM = 64
Q_LORA_RANK = D_MODEL // 4
KV_LORA_RANK = D_MODEL // 8
ROPE_THETA = 10000.0
ATTN_WIDTH = ATTN_HEADS * V_HEAD_DIM
SSM_HEADS = 8
SSM_HEAD_DIM = 64
SSM_INNER = SSM_HEADS * SSM_HEAD_DIM
SSM_GROUPS = 2
SSM_STATE = 128
SSM_CONV = 5
SSM_CHUNK = 128
SSM_CONV_CH = SSM_INNER + 2 * SSM_GROUPS * SSM_STATE
D_MIX = ATTN_WIDTH + SSM_INNER
D_FF = 4 * D_MODEL
EPS = 1e-6

LANES = 128
SUBLANES = 8
HEAD_PAD = LANES

HEADS_PER_GROUP = SSM_HEADS // SSM_GROUPS
GROUP_INNER = SSM_INNER // SSM_GROUPS
GROUP_COLS = GROUP_INNER + 2 * SSM_STATE
DT_ROWS = 2 * HEADS_PER_GROUP
CONV_WIN = 2 * SSM_CHUNK
CONV_WIN_LEAD = SSM_CHUNK // 2

COL_CKV = Q_LORA_RANK
COL_MISC = COL_CKV + KV_LORA_RANK
COL_Z = COL_MISC + LANES
COL_XBC = COL_Z + SSM_INNER
IN_COLS = COL_XBC + SSM_CONV_CH

TM_IN = 512
TQ = 512
TM_MLP = 512
FF_CHUNK = 1024
VMEM_LIMIT = 56 * 1024 * 1024


def _inv_rms(x):
    n = x.shape[-1]
    x2 = x * x
    acc = x2[:, 0:LANES]
    for i in range(1, n // LANES):
        acc = acc + x2[:, i * LANES:(i + 1) * LANES]
    return lax.rsqrt(jnp.sum(acc, axis=-1, keepdims=True) * (1.0 / n) + EPS)


def _dot(a, b):
    return jnp.dot(a, b, preferred_element_type=F32)


def _dot_nt(a, b):
    return lax.dot_general(a, b, (((1,), (1,)), ((), ())), preferred_element_type=F32)


def _inproj_kernel(x_ref, cs_ref, g_ref, win_ref, gqa_ref, wuq_ref, gkva_ref, wuk_ref, wuv_ref,
                   gq_ref, gkn_ref, gkr_ref,
                   q_ref, k_ref, v_ref, z_ref, xbc_ref, dt_ref):
    tm = x_ref.shape[0]
    x = x_ref[...]
    h = (x * _inv_rms(x) * g_ref[...]).astype(BF16)
    big = _dot(h, win_ref[...])
    z_ref[...] = big[:, COL_Z:COL_XBC].astype(BF16)
    xbc_ref[...] = big[:, COL_XBC:IN_COLS].astype(BF16)
    misc = big[:, COL_MISC:COL_Z]
    for c in range(tm // SSM_CHUNK):
        dt_ref[c] = misc[c * SSM_CHUNK:(c + 1) * SSM_CHUNK, :].T[0:2 * SSM_HEADS, :]

    cq = big[:, 0:COL_CKV]
    ckv = big[:, COL_CKV:COL_MISC]
    cqn = (cq * _inv_rms(cq) * gqa_ref[...]).astype(BF16)
    ckvn = (ckv * _inv_rms(ckv) * gkva_ref[...]).astype(BF16)
    q_pre = _dot(cqn, wuq_ref[...])
    k_pre = _dot(ckvn, wuk_ref[...])
    v_ref[...] = _dot(ckvn, wuv_ref[...]).astype(BF16)

    cs = cs_ref[...]
    lane = lax.broadcasted_iota(jnp.int32, (1, HEAD_PAD), 1)
    in_head = (lane < QK_HEAD_DIM).astype(F32)
    is_rope = ((lane >= QK_NOPE_DIM) & (lane < QK_HEAD_DIM)).astype(F32)
    inv_d = 1.0 / QK_HEAD_DIM

    gcq = gq_ref[...] * cs
    for hh in range(ATTN_HEADS):
        sl = slice(hh * HEAD_PAD, (hh + 1) * HEAD_PAD)
        qh = q_pre[:, sl]
        ssq = jnp.sum(qh * qh * in_head, axis=-1, keepdims=True)
        q_ref[:, sl] = (qh * lax.rsqrt(ssq * inv_d + EPS) * gcq).astype(BF16)

    ab = misc * (gkr_ref[...] * cs)
    lane2 = lax.broadcasted_iota(jnp.int32, (tm, HEAD_PAD), 1)
    swapped = jnp.where(lane2 < QK_HEAD_DIM, pltpu.roll(ab, HEAD_PAD - QK_ROPE_DIM, 1),
                        pltpu.roll(ab, QK_ROPE_DIM, 1))
    s_both = jnp.where(lane2 >= QK_NOPE_DIM, ab + swapped, 0.0)
    ssq_pe = jnp.sum(misc * misc * is_rope, axis=-1, keepdims=True)
    gkn = gkn_ref[...]
    for hh in range(ATTN_HEADS):
        sl = slice(hh * HEAD_PAD, (hh + 1) * HEAD_PAD)
        kh = k_pre[:, sl]
        ssq = jnp.sum(kh * kh, axis=-1, keepdims=True) + ssq_pe
        k_ref[:, sl] = ((kh * gkn + s_both) * lax.rsqrt(ssq * inv_d + EPS)).astype(BF16)


def _inproj_call(x2, cs, g, win, gqa, wuq, gkva, wuk, wuv, gq, gkn, gkr):
    t = x2.shape[0]
    cpt = TM_IN // SSM_CHUNK
    full = lambda a: pl.BlockSpec(a.shape, lambda i: (0,) * a.ndim)
    row = lambda w: pl.BlockSpec((TM_IN, w), lambda i: (i, 0))
    return pl.pallas_call(
        _inproj_kernel,
        grid=(t // TM_IN,),
        in_specs=[row(D_MODEL), row(HEAD_PAD), full(g), full(win), full(gqa), full(wuq), full(gkva),
                  full(wuk), full(wuv), full(gq), full(gkn), full(gkr)],
        out_specs=[row(ATTN_HEADS * HEAD_PAD), row(ATTN_HEADS * HEAD_PAD), row(ATTN_WIDTH),
                   row(SSM_INNER), row(SSM_CONV_CH),
                   pl.BlockSpec((cpt, 2 * SSM_HEADS, SSM_CHUNK), lambda i: (i, 0, 0))],
        out_shape=[jax.ShapeDtypeStruct((t, ATTN_HEADS * HEAD_PAD), BF16),
                   jax.ShapeDtypeStruct((t, ATTN_HEADS * HEAD_PAD), BF16),
                   jax.ShapeDtypeStruct((t, ATTN_WIDTH), BF16),
                   jax.ShapeDtypeStruct((t, SSM_INNER), BF16),
                   jax.ShapeDtypeStruct((t, SSM_CONV_CH), BF16),
                   jax.ShapeDtypeStruct((t // SSM_CHUNK, 2 * SSM_HEADS, SSM_CHUNK), F32)],
        compiler_params=pltpu.CompilerParams(dimension_semantics=("arbitrary",),
                                             vmem_limit_bytes=VMEM_LIMIT),
        name="inproj",
    )(x2, cs, g, win, gqa, wuq, gkva, wuk, wuv, gq, gkn, gkr)


def _attn_kernel(q_ref, k_ref, v_ref, o_ref):
    outs = []
    for j in range(2):
        q = q_ref[:, j * HEAD_PAD:(j + 1) * HEAD_PAD]
        k = k_ref[:, j * HEAD_PAD:(j + 1) * HEAD_PAD]
        s = _dot_nt(q, k)
        m = jnp.max(s, axis=-1, keepdims=True)
        p = jnp.exp(s - m)
        l = jnp.sum(p, axis=-1, keepdims=True)
        o = _dot(p.astype(BF16), v_ref[:, j * V_HEAD_DIM:(j + 1) * V_HEAD_DIM])
        outs.append(o / l)
    o_ref[...] = jnp.concatenate(outs, axis=1)


def _attn_call(q, k, v, batch, seq):
    n_q = seq // TQ
    return pl.pallas_call(
        _attn_kernel,
        grid=(batch, ATTN_HEADS // 2, n_q),
        in_specs=[pl.BlockSpec((TQ, 2 * HEAD_PAD), lambda b, hp, i: (b * n_q + i, hp)),
                  pl.BlockSpec((seq, 2 * HEAD_PAD), lambda b, hp, i: (b, hp)),
                  pl.BlockSpec((seq, 2 * V_HEAD_DIM), lambda b, hp, i: (b, hp))],
        out_specs=pl.BlockSpec((TQ, 2 * V_HEAD_DIM), lambda b, hp, i: (b * n_q + i, hp)),
        out_shape=jax.ShapeDtypeStruct((batch * seq, ATTN_WIDTH), F32),
        compiler_params=pltpu.CompilerParams(
            dimension_semantics=("arbitrary", "arbitrary", "arbitrary"),
            vmem_limit_bytes=VMEM_LIMIT),
        name="attention",
    )(q, k, v)


def _split3(x):
    hi = x.astype(BF16)
    r1 = x - hi.astype(F32)
    mid = r1.astype(BF16)
    lo = (r1 - mid.astype(F32)).astype(BF16)
    return hi, mid, lo


def _ssd_kernel(xbc_ref, z_ref, dt_ref, sh_ref, cw_ref, cbias_ref, alog_ref, dtbias_ref, dskip_ref, gn_ref,
                o_ref, xc_scr, xm_scr, nsf_scr, nsb_scr, cs_scr, dtv_scr, w_scr, dec_scr, colcs_scr):
    nc = dt_ref.shape[0]
    seq = xbc_ref.shape[0]
    L = SSM_CHUNK
    hpg = HEADS_PER_GROUP
    npair = hpg // 2
    P = SSM_HEAD_DIM
    x_cols = slice(0, GROUP_INNER)
    b_cols = slice(GROUP_INNER, GROUP_INNER + SSM_STATE)
    c_cols = slice(GROUP_INNER + SSM_STATE, GROUP_COLS)
    pair_cols = [slice(j * L, (j + 1) * L) for j in range(npair)]

    a_neg = -jnp.exp(alog_ref[...])
    dtv = jax.nn.softplus(dt_ref[...] + dtbias_ref[...][None])
    dtv_scr[...] = dtv
    da2 = (dtv * a_neg[None]).reshape(nc * DT_ROWS, L)
    ri = lax.broadcasted_iota(jnp.int32, (L, L), 0)
    ci = lax.broadcasted_iota(jnp.int32, (L, L), 1)
    upper = (ri <= ci).astype(BF16)
    lower = (ri >= ci).astype(BF16)
    dparts = _split3(da2)
    cs_f = sum(_dot(p, upper) for p in dparts)
    cs_b = sum(_dot(p, lower) for p in dparts)
    rowsel = (ri & hpg) == 0
    cs2 = jnp.where(rowsel, cs_f, cs_b)
    cs_scr[...] = cs2.reshape(nc, DT_ROWS, L)
    col_f = sum(_dot_nt(lower, p) for p in dparts)
    col_b = sum(_dot_nt(upper, p) for p in dparts)
    colcs_scr[...] = jnp.where((ci & hpg) == 0, col_f, col_b)

    def lane_bcast(col):
        return jnp.broadcast_to(col, (col.shape[0], L))

    tot2 = jnp.where(rowsel, lane_bcast(cs2[:, L - 1:L]), lane_bcast(cs2[:, 0:1]))
    dec_scr[...] = jnp.exp(tot2).reshape(nc, DT_ROWS, L)
    w_scr[...] = (dtv.reshape(nc * DT_ROWS, L) * jnp.exp(tot2 - cs2)).reshape(nc, DT_ROWS, L)

    lane_t = lax.broadcasted_iota(jnp.int32, (L, L), 1)
    low_half = lane_t < P

    def pass_a(c, carry):
        ws = pl.multiple_of(jnp.clip(c * L - CONV_WIN_LEAD, 0, seq - CONV_WIN), CONV_WIN_LEAD)
        variant = jnp.where(c == 0, 0, jnp.where(c == nc - 1, 2, 1))
        shifted = _dot(sh_ref[variant], xbc_ref[pl.ds(ws, CONV_WIN), :])
        acc = cbias_ref[...]
        for kk in range(SSM_CONV):
            acc = acc + cw_ref[kk:kk + 1, :] * shifted[kk * L:(kk + 1) * L, :]
        xc = acc * jax.nn.sigmoid(acc)
        xc_scr[c] = xc
        for j in range(npair):
            xp = xc[:, pair_cols[j]]
            xm_scr[c, j] = jnp.concatenate([jnp.where(low_half, xp, 0.0), jnp.where(low_half, 0.0, xp)],
                                           axis=0).astype(BF16)
        bt = xc[:, b_cols].T
        for d, ns_scr in ((0, nsf_scr), (1, nsb_scr)):
            w = w_scr[c, d * hpg:(d + 1) * hpg, :]
            ns_scr[c] = jnp.concatenate(
                [_dot(jnp.concatenate([(bt * w[2 * j:2 * j + 1, :]).astype(BF16),
                                       (bt * w[2 * j + 1:2 * j + 2, :]).astype(BF16)], axis=1), xm_scr[c, j])
                 for j in range(npair)], axis=1)
        return carry
    lax.fori_loop(0, nc, pass_a, 0)

    lane_r = lax.broadcasted_iota(jnp.int32, (1, L), 1)

    def decay_row(c, d):
        dec = dec_scr[c, d * hpg:(d + 1) * hpg, :]
        return jnp.concatenate([jnp.where(lane_r < P, dec[2 * j:2 * j + 1, :], dec[2 * j + 1:2 * j + 2, :])
                                for j in range(npair)], axis=1)

    def pass_b(d, ns_scr):
        def body(i, st):
            c = i if d == 0 else nc - 1 - i
            new = ns_scr[c]
            ns_scr[c] = st
            return st * decay_row(c, d) + new
        lax.fori_loop(0, nc, body, jnp.zeros((SSM_STATE, GROUP_INNER), F32))
    pass_b(0, nsf_scr)
    pass_b(1, nsb_scr)

    mask_f = ci <= ri
    mask_b = ci >= ri
    neg_inf = jnp.float32(-jnp.inf)

    def pass_c(c, carry):
        base = pl.multiple_of(c * L, L)
        xc = xc_scr[c]
        bm = xc[:, b_cols].astype(BF16)
        cm = xc[:, c_cols].astype(BF16)
        cb = _dot_nt(cm, bm)
        off_f = _dot(cm, nsf_scr[c].astype(BF16))
        off_b = _dot(cm, nsb_scr[c].astype(BF16))
        colc = pltpu.roll(colcs_scr[...], (L - c * DT_ROWS) % L, 1)
        cs_f_r = cs_scr[c, 0:hpg, :]
        cs_b_r = cs_scr[c, hpg:DT_ROWS, :]
        dt_f_r = dtv_scr[c, 0:hpg, :]
        dt_b_r = dtv_scr[c, hpg:DT_ROWS, :]
        ys = []
        for j in range(npair):
            ws_, sf_, sb_ = [], [], []
            for h in (2 * j, 2 * j + 1):
                bc_f = lane_bcast(colc[:, h:h + 1])
                bc_b = lane_bcast(colc[:, hpg + h:hpg + h + 1])
                ef = jnp.exp(jnp.where(mask_f, bc_f - cs_f_r[h:h + 1, :], neg_inf))
                eb = jnp.exp(jnp.where(mask_b, bc_b - cs_b_r[h:h + 1, :], neg_inf))
                ws_.append((cb * (ef * dt_f_r[h:h + 1, :] + eb * dt_b_r[h:h + 1, :])).astype(BF16))
                sf_.append(jnp.exp(bc_f))
                sb_.append(jnp.exp(bc_b))
            ys.append(_dot(jnp.concatenate(ws_, axis=1), xm_scr[c, j])
                      + jnp.where(low_half, sf_[0], sf_[1]) * off_f[:, pair_cols[j]]
                      + jnp.where(low_half, sb_[0], sb_[1]) * off_b[:, pair_cols[j]])
        y = jnp.concatenate(ys, axis=1) + dskip_ref[...] * xc[:, x_cols]
        zc = z_ref[pl.ds(base, L), :].astype(F32)
        y = y * (zc * jax.nn.sigmoid(zc))
        o_ref[pl.ds(base, L), :] = y * _inv_rms(y) * gn_ref[...]
        return carry
    lax.fori_loop(0, nc, pass_c, 0)


def _conv_shift_matrices():
    out = np.zeros((3, SSM_CONV * SSM_CHUNK, CONV_WIN), np.float32)
    for v, lead in enumerate((0, CONV_WIN_LEAD, CONV_WIN - SSM_CHUNK)):
        for kk in range(SSM_CONV):
            for t in range(SSM_CHUNK):
                j = t + kk - SSM_CONV // 2 + lead
                if 0 <= j < CONV_WIN:
                    out[v, kk * SSM_CHUNK + t, j] = 1.0
    return out


def _ssd_call(xbc, z, dtc, cw, cbias, alog, dtbias, dskip, gn, batch, seq):
    nc = seq // SSM_CHUNK
    L = SSM_CHUNK
    assert nc >= 3 and seq >= CONV_WIN
    sh = jnp.asarray(_conv_shift_matrices(), BF16)
    return pl.pallas_call(
        _ssd_kernel,
        grid=(batch, SSM_GROUPS),
        in_specs=[pl.BlockSpec((seq, GROUP_COLS), lambda b, g: (b, g)),
                  pl.BlockSpec((seq, GROUP_INNER), lambda b, g: (b, g)),
                  pl.BlockSpec((nc, DT_ROWS, L), lambda b, g: (b, g, 0)),
                  pl.BlockSpec(sh.shape, lambda b, g: (0, 0, 0)),
                  pl.BlockSpec((SUBLANES, GROUP_COLS), lambda b, g: (0, g)),
                  pl.BlockSpec((1, GROUP_COLS), lambda b, g: (0, g)),
                  pl.BlockSpec((DT_ROWS, L), lambda b, g: (g, 0)),
                  pl.BlockSpec((DT_ROWS, L), lambda b, g: (g, 0)),
                  pl.BlockSpec((1, GROUP_INNER), lambda b, g: (0, g)),
                  pl.BlockSpec((1, GROUP_INNER), lambda b, g: (0, g))],
        out_specs=pl.BlockSpec((seq, GROUP_INNER), lambda b, g: (b, g)),
        out_shape=jax.ShapeDtypeStruct((batch * seq, SSM_INNER), F32),
        scratch_shapes=[pltpu.VMEM((nc, L, GROUP_COLS), F32),
                        pltpu.VMEM((nc, HEADS_PER_GROUP // 2, 2 * L, L), BF16),
                        pltpu.VMEM((nc, SSM_STATE, GROUP_INNER), F32),
                        pltpu.VMEM((nc, SSM_STATE, GROUP_INNER), F32),
                        pltpu.VMEM((nc, DT_ROWS, L), F32),
                        pltpu.VMEM((nc, DT_ROWS, L), F32),
                        pltpu.VMEM((nc, DT_ROWS, L), F32),
                        pltpu.VMEM((nc, DT_ROWS, L), F32),
                        pltpu.VMEM((L, nc * DT_ROWS), F32)],
        compiler_params=pltpu.CompilerParams(dimension_semantics=("arbitrary", "arbitrary"),
                                             vmem_limit_bytes=VMEM_LIMIT),
        name="ssd",
    )(xbc, z, dtc, sh, cw, cbias, alog, dtbias, dskip, gn)


def _mlp_kernel(x_ref, attn_ref, ssm_ref, gat_ref, wo_ref, gm_ref, wup_ref, wdn_ref, o_ref):
    a = attn_ref[...]
    an = (a * _inv_rms(a) * gat_ref[...]).astype(BF16)
    mix = _dot(an, wo_ref[0:ATTN_WIDTH, :]) + _dot(ssm_ref[...].astype(BF16), wo_ref[ATTN_WIDTH:D_MIX, :])
    x1 = x_ref[...] + mix
    hm = (x1 * _inv_rms(x1) * gm_ref[...]).astype(BF16)
    acc = jnp.zeros_like(x1)
    for c in range(D_FF // FF_CHUNK):
        cols = slice(c * FF_CHUNK, (c + 1) * FF_CHUNK)
        u = _dot(hm, wup_ref[:, cols])
        acc = acc + _dot(jnp.square(jnp.maximum(u, 0.0)).astype(BF16), wdn_ref[cols, :])
    o_ref[...] = x1 + acc


def _mlp_call(x2, attn, ssm, gat, wo, gm, wup, wdn):
    t = x2.shape[0]
    row = lambda w: pl.BlockSpec((TM_MLP, w), lambda i: (i, 0))
    full = lambda a: pl.BlockSpec(a.shape, lambda i: (0, 0), pipeline_mode=pl.Buffered(1))
    return pl.pallas_call(
        _mlp_kernel,
        grid=(t // TM_MLP,),
        in_specs=[row(D_MODEL), row(ATTN_WIDTH), row(SSM_INNER), full(gat), full(wo), full(gm),
                  full(wup), full(wdn)],
        out_specs=row(D_MODEL),
        out_shape=jax.ShapeDtypeStruct((t, D_MODEL), F32),
        compiler_params=pltpu.CompilerParams(dimension_semantics=("arbitrary",),
                                             vmem_limit_bytes=VMEM_LIMIT),
        name="outproj_mlp",
    )(x2, attn, ssm, gat, wo, gm, wup, wdn)


def _ssm_perm():
    xbc, dt = [], []
    for g in range(SSM_GROUPS):
        xbc += list(range(g * GROUP_INNER, (g + 1) * GROUP_INNER))
        xbc += list(range(SSM_INNER + g * SSM_STATE, SSM_INNER + (g + 1) * SSM_STATE))
        xbc += list(range(SSM_INNER + SSM_GROUPS * SSM_STATE + g * SSM_STATE,
                          SSM_INNER + SSM_GROUPS * SSM_STATE + (g + 1) * SSM_STATE))
        dt += list(range(g * HEADS_PER_GROUP, (g + 1) * HEADS_PER_GROUP))
        dt += list(range(SSM_HEADS + g * HEADS_PER_GROUP, SSM_HEADS + (g + 1) * HEADS_PER_GROUP))
    return np.array(xbc), np.array(dt)


def _rotate_half(w):
    half = QK_ROPE_DIM // 2
    return jnp.concatenate([-w[..., half:], w[..., :half]], axis=-1)


def _swap_halves(g):
    half = QK_ROPE_DIM // 2
    return jnp.concatenate([g[..., half:], g[..., :half]], axis=-1)


def _lane_bcast(v):
    return jnp.broadcast_to(v[..., None], v.shape + (LANES,)).astype(F32)


def _layer(x2, cs, batch, seq, ln_mix_g, w_in, q_a_norm_g, w_uq, kv_a_norm_g, w_ukv, q_norm_g,
           k_norm_g, attn_out_norm_g, conv_w, conv_b, a_log_fwd, a_log_bwd, dt_bias_fwd, dt_bias_bwd,
           d_skip, ssm_norm_g, w_out, ln_mlp_g, w_mlp_up, w_mlp_down):
    xbc_perm, dt_perm = _ssm_perm()
    o_kpe = Q_LORA_RANK + KV_LORA_RANK
    o_z = o_kpe + QK_ROPE_DIM
    o_xbc = o_z + SSM_INNER
    o_dt = o_xbc + SSM_CONV_CH
    d = w_in.shape[0]

    w_kpe = w_in[:, o_kpe:o_z]
    misc = jnp.concatenate([w_in[:, o_dt:][:, dt_perm],
                            jnp.zeros((d, QK_NOPE_DIM - 2 * SSM_HEADS), w_in.dtype),
                            w_kpe, _rotate_half(w_kpe)], axis=1)
    win = jnp.concatenate([w_in[:, 0:o_kpe], misc, w_in[:, o_z:o_xbc],
                           w_in[:, o_xbc:o_dt][:, xbc_perm]], axis=1).astype(BF16)
    wq = w_uq.reshape(Q_LORA_RANK, ATTN_HEADS, QK_HEAD_DIM)
    wuq = jnp.concatenate([wq, _rotate_half(wq[..., QK_NOPE_DIM:])], axis=-1)
    wuq = wuq.reshape(Q_LORA_RANK, ATTN_HEADS * HEAD_PAD).astype(BF16)
    wukv = w_ukv.reshape(KV_LORA_RANK, ATTN_HEADS, QK_NOPE_DIM + V_HEAD_DIM)
    wuk = jnp.pad(wukv[:, :, :QK_NOPE_DIM],
                  ((0, 0), (0, 0), (0, HEAD_PAD - QK_NOPE_DIM))).reshape(KV_LORA_RANK, -1).astype(BF16)
    wuv = wukv[:, :, QK_NOPE_DIM:].reshape(KV_LORA_RANK, -1).astype(BF16)
    scale = QK_HEAD_DIM ** -0.5
    gq = (jnp.concatenate([q_norm_g, _swap_halves(q_norm_g[QK_NOPE_DIM:])]) * scale)[None, :]
    zeros_nope = jnp.zeros((QK_NOPE_DIM,), k_norm_g.dtype)
    gkn = jnp.concatenate([k_norm_g[:QK_NOPE_DIM], zeros_nope])[None, :]
    gkr = jnp.concatenate([zeros_nope, k_norm_g[QK_NOPE_DIM:], _swap_halves(k_norm_g[QK_NOPE_DIM:])])[None, :]

    q, k, v, z, xbc, dtc = _inproj_call(
        x2, cs, ln_mix_g[None, :], win, q_a_norm_g[None, :], wuq, kv_a_norm_g[None, :], wuk, wuv,
        gq, gkn, gkr)

    attn = _attn_call(q, k, v, batch, seq)

    cw = jnp.pad(conv_w[:, 0, :][:, xbc_perm], ((0, SUBLANES - SSM_CONV), (0, 0)))
    cbias = conv_b[xbc_perm][None, :]
    alog = _lane_bcast(jnp.concatenate([a_log_fwd, a_log_bwd])[dt_perm])
    dtbias = _lane_bcast(jnp.concatenate([dt_bias_fwd, dt_bias_bwd])[dt_perm])
    dskip = jnp.repeat(d_skip, SSM_HEAD_DIM)[None, :]
    ssm = _ssd_call(xbc, z, dtc, cw, cbias, alog, dtbias, dskip, ssm_norm_g[None, :], batch, seq)

    return _mlp_call(x2, attn, ssm, attn_out_norm_g[None, :], w_out.astype(BF16), ln_mlp_g[None, :],
                     w_mlp_up.astype(BF16), w_mlp_down.astype(BF16))


def _rope_table(positions):
    inv_freq = 1.0 / (ROPE_THETA ** (jnp.arange(0, QK_ROPE_DIM, 2, dtype=F32) / QK_ROPE_DIM))
    ang = positions.astype(F32).reshape(-1, 1) * inv_freq
    ang = jnp.concatenate([ang, ang], axis=-1)
    return jnp.concatenate([jnp.ones((ang.shape[0], QK_NOPE_DIM), F32), jnp.cos(ang), jnp.sin(ang)], axis=1)


def kernel(x, positions, ln_mix_g, w_in, q_a_norm_g, w_uq, kv_a_norm_g, w_ukv, q_norm_g, k_norm_g,
           attn_out_norm_g, conv_w, conv_b, a_log_fwd, a_log_bwd, dt_bias_fwd, dt_bias_bwd, d_skip,
           ssm_norm_g, w_out, ln_mlp_g, w_mlp_up, w_mlp_down):
    batch, seq, d = x.shape
    assert d == D_MODEL and seq % TQ == 0 and (batch * seq) % TM_IN == 0 and (batch * seq) % TM_MLP == 0
    cs = _rope_table(positions)
    x2 = x.reshape(batch * seq, d)
    for l in range(ln_mix_g.shape[0]):
        x2 = _layer(x2, cs, batch, seq, ln_mix_g[l], w_in[l], q_a_norm_g[l], w_uq[l], kv_a_norm_g[l],
                    w_ukv[l], q_norm_g[l], k_norm_g[l], attn_out_norm_g[l], conv_w[l], conv_b[l],
                    a_log_fwd[l], a_log_bwd[l], dt_bias_fwd[l], dt_bias_bwd[l], d_skip[l], ssm_norm_g[l],
                    w_out[l], ln_mlp_g[l], w_mlp_up[l], w_mlp_down[l])
    return x2.reshape(batch, seq, d)
```

```python
import numpy as np
import jax
import jax.numpy as jnp
from jax import lax
from jax.experimental import pallas as pl
from jax.experimental.pallas import tpu as pltpu

F32 = jnp.float32
BF16 = jnp.bfloat16

D_MODEL = 1024
ATTN_HEADS = 8
QK_NOPE_DIM = 64
QK_ROPE_DIM = 32
QK_HEAD_DIM = QK_NOPE_DIM + QK_ROPE_DIM
V_HEAD_DIM = 64
Q_LORA_RANK = D_MODEL // 4
KV_LORA_RANK = D_MODEL // 8
ROPE_THETA = 10000.0
ATTN_WIDTH = ATTN_HEADS * V_HEAD_DIM
SSM_HEADS = 8
SSM_HEAD_DIM = 64
SSM_INNER = SSM_HEADS * SSM_HEAD_DIM
SSM_GROUPS = 2
SSM_STATE = 128
SSM_CONV = 5
SSM_CHUNK = 128
SSM_CONV_CH = SSM_INNER + 2 * SSM_GROUPS * SSM_STATE
D_MIX = ATTN_WIDTH + SSM_INNER
D_FF = 4 * D_MODEL
EPS = 1e-6

LANES = 128
SUBLANES = 8
HEAD_PAD = LANES

HEADS_PER_GROUP = SSM_HEADS // SSM_GROUPS
GROUP_INNER = SSM_INNER // SSM_GROUPS
GROUP_COLS = GROUP_INNER + 2 * SSM_STATE
DT_ROWS = 2 * HEADS_PER_GROUP
CONV_WIN = 2 * SSM_CHUNK
CONV_WIN_LEAD = SSM_CHUNK // 2

COL_CKV = Q_LORA_RANK
COL_MISC = COL_CKV + KV_LORA_RANK
COL_Z = COL_MISC + LANES
COL_XBC = COL_Z + SSM_INNER
IN_COLS = COL_XBC + SSM_CONV_CH

TM_IN = 512
TQ = 512
ATTN_HEADS_PER_STEP = 8
TM_MLP = 512
FF_CHUNK = 1024
VMEM_LIMIT = 56 * 1024 * 1024


def _inv_rms(x):
    n = x.shape[-1]
    x2 = x * x
    acc = x2[:, 0:LANES]
    for i in range(1, n // LANES):
        acc = acc + x2[:, i * LANES:(i + 1) * LANES]
    return lax.rsqrt(jnp.sum(acc, axis=-1, keepdims=True) * (1.0 / n) + EPS)


def _dot(a, b):
    return jnp.dot(a, b, preferred_element_type=F32)


def _dot_nt(a, b):
    return lax.dot_general(a, b, (((1,), (1,)), ((), ())), preferred_element_type=F32)


def _inproj_kernel(x_ref, cs_ref, g_ref, win_ref, gqa_ref, wuq_ref, gkva_ref, wuk_ref, wuv_ref, vone_ref,
                   gq_ref, gkn_ref, gkr_ref,
                   q_ref, k_ref, v_ref, z_ref, xbc_ref, dt_ref):
    tm = x_ref.shape[0]
    x = x_ref[...]
    h = (x * _inv_rms(x) * g_ref[...]).astype(BF16)
    big = _dot(h, win_ref[...])
    z_ref[...] = big[:, COL_Z:COL_XBC].astype(BF16)
    xbc_ref[...] = big[:, COL_XBC:IN_COLS].astype(BF16)
    misc = big[:, COL_MISC:COL_Z]
    for c in range(tm // SSM_CHUNK):
        dt_ref[c] = misc[c * SSM_CHUNK:(c + 1) * SSM_CHUNK, :].T[0:2 * SSM_HEADS, :]

    cq = big[:, 0:COL_CKV]
    ckv = big[:, COL_CKV:COL_MISC]
    cqn = (cq * _inv_rms(cq) * gqa_ref[...]).astype(BF16)
    ckvn = (ckv * _inv_rms(ckv) * gkva_ref[...]).astype(BF16)
    q_pre = _dot(cqn, wuq_ref[...])
    k_pre = _dot(ckvn, wuk_ref[...])
    v_ref[...] = (_dot(ckvn, wuv_ref[...]) + vone_ref[...]).astype(BF16)

    cs = cs_ref[...]
    lane = lax.broadcasted_iota(jnp.int32, (1, HEAD_PAD), 1)
    in_head = (lane < QK_HEAD_DIM).astype(F32)
    is_rope = ((lane >= QK_NOPE_DIM) & (lane < QK_HEAD_DIM)).astype(F32)
    inv_d = 1.0 / QK_HEAD_DIM

    gcq = gq_ref[...] * cs
    for hh in range(ATTN_HEADS):
        sl = slice(hh * HEAD_PAD, (hh + 1) * HEAD_PAD)
        qh = q_pre[:, sl]
        ssq = jnp.sum(qh * qh * in_head, axis=-1, keepdims=True)
        q_ref[:, sl] = (qh * lax.rsqrt(ssq * inv_d + EPS) * gcq).astype(BF16)

    ab = misc * (gkr_ref[...] * cs)
    lane2 = lax.broadcasted_iota(jnp.int32, (tm, HEAD_PAD), 1)
    swapped = jnp.where(lane2 < QK_HEAD_DIM, pltpu.roll(ab, HEAD_PAD - QK_ROPE_DIM, 1),
                        pltpu.roll(ab, QK_ROPE_DIM, 1))
    s_both = jnp.where(lane2 >= QK_NOPE_DIM, ab + swapped, 0.0)
    ssq_pe = jnp.sum(misc * misc * is_rope, axis=-1, keepdims=True)
    gkn = gkn_ref[...]
    for hh in range(ATTN_HEADS):
        sl = slice(hh * HEAD_PAD, (hh + 1) * HEAD_PAD)
        kh = k_pre[:, sl]
        ssq = jnp.sum(kh * kh, axis=-1, keepdims=True) + ssq_pe
        k_ref[:, sl] = ((kh * gkn + s_both) * lax.rsqrt(ssq * inv_d + EPS)).astype(BF16)


def _inproj_call(x2, cs, g, win, gqa, wuq, gkva, wuk, wuv, vone, gq, gkn, gkr):
    t = x2.shape[0]
    cpt = TM_IN // SSM_CHUNK
    full = lambda a: pl.BlockSpec(a.shape, lambda i: (0,) * a.ndim)
    row = lambda w: pl.BlockSpec((TM_IN, w), lambda i: (i, 0))
    return pl.pallas_call(
        _inproj_kernel,
        grid=(t // TM_IN,),
        in_specs=[row(D_MODEL), row(HEAD_PAD), full(g), full(win), full(gqa), full(wuq), full(gkva),
                  full(wuk), full(wuv), full(vone), full(gq), full(gkn), full(gkr)],
        out_specs=[row(ATTN_HEADS * HEAD_PAD), row(ATTN_HEADS * HEAD_PAD), row(ATTN_HEADS * HEAD_PAD),
                   row(SSM_INNER), row(SSM_CONV_CH),
                   pl.BlockSpec((cpt, 2 * SSM_HEADS, SSM_CHUNK), lambda i: (i, 0, 0))],
        out_shape=[jax.ShapeDtypeStruct((t, ATTN_HEADS * HEAD_PAD), BF16),
                   jax.ShapeDtypeStruct((t, ATTN_HEADS * HEAD_PAD), BF16),
                   jax.ShapeDtypeStruct((t, ATTN_HEADS * HEAD_PAD), BF16),
                   jax.ShapeDtypeStruct((t, SSM_INNER), BF16),
                   jax.ShapeDtypeStruct((t, SSM_CONV_CH), BF16),
                   jax.ShapeDtypeStruct((t // SSM_CHUNK, 2 * SSM_HEADS, SSM_CHUNK), F32)],
        compiler_params=pltpu.CompilerParams(dimension_semantics=("arbitrary",),
                                             vmem_limit_bytes=VMEM_LIMIT),
        name="inproj",
    )(x2, cs, g, win, gqa, wuq, gkva, wuk, wuv, vone, gq, gkn, gkr)


def _attn_kernel(q_ref, k_ref, v_ref, o_ref):
    res = []
    for j in range(ATTN_HEADS_PER_STEP):
        sl = slice(j * HEAD_PAD, (j + 1) * HEAD_PAD)
        s = _dot_nt(q_ref[:, sl], k_ref[:, sl])
        m = jnp.max(s, axis=-1, keepdims=True)
        acc = _dot(jnp.exp2(s - m).astype(BF16), v_ref[:, sl])
        res.append(acc / pltpu.roll(acc, V_HEAD_DIM, 1))
    lane = lax.broadcasted_iota(jnp.int32, res[0].shape, 1)
    for jp in range(ATTN_HEADS_PER_STEP // 2):
        o_ref[:, jp * HEAD_PAD:(jp + 1) * HEAD_PAD] = jnp.where(lane < V_HEAD_DIM, res[2 * jp], res[2 * jp + 1])


def _attn_call(q, k, v, batch, seq):
    n_q = seq // TQ
    hps = ATTN_HEADS_PER_STEP
    return pl.pallas_call(
        _attn_kernel,
        grid=(batch, ATTN_HEADS // hps, n_q),
        in_specs=[pl.BlockSpec((TQ, hps * HEAD_PAD), lambda b, hp, i: (b * n_q + i, hp)),
                  pl.BlockSpec((seq, hps * HEAD_PAD), lambda b, hp, i: (b, hp)),
                  pl.BlockSpec((seq, hps * HEAD_PAD), lambda b, hp, i: (b, hp))],
        out_specs=pl.BlockSpec((TQ, hps * V_HEAD_DIM), lambda b, hp, i: (b * n_q + i, hp)),
        out_shape=jax.ShapeDtypeStruct((batch * seq, ATTN_WIDTH), F32),
        compiler_params=pltpu.CompilerParams(
            dimension_semantics=("arbitrary", "arbitrary", "arbitrary"),
            vmem_limit_bytes=VMEM_LIMIT),
        name="attention",
    )(q, k, v)


def _split3(x):
    hi = x.astype(BF16)
    r1 = x - hi.astype(F32)
    mid = r1.astype(BF16)
    lo = (r1 - mid.astype(F32)).astype(BF16)
    return hi, mid, lo


def _ssd_kernel(xbc_ref, z_ref, dt_ref, sh_ref, cw_ref, cbias_ref, alog_ref, dtbias_ref, dskip_ref, gn_ref,
                o_ref, xc_scr, xm_scr, nsf_scr, nsb_scr, cs_scr, dtv_scr, w_scr, dec_scr, colcs_scr):
    nc = dt_ref.shape[0]
    seq = xbc_ref.shape[0]
    L = SSM_CHUNK
    hpg = HEADS_PER_GROUP
    npair = hpg // 2
    P = SSM_HEAD_DIM
    x_cols = slice(0, GROUP_INNER)
    b_cols = slice(GROUP_INNER, GROUP_INNER + SSM_STATE)
    c_cols = slice(GROUP_INNER + SSM_STATE, GROUP_COLS)
    pair_cols = [slice(j * L, (j + 1) * L) for j in range(npair)]

    a_neg = -jnp.exp(alog_ref[...])
    dtv = jax.nn.softplus(dt_ref[...] + dtbias_ref[...][None])
    dtv_scr[...] = dtv
    da2 = (dtv * a_neg[None]).reshape(nc * DT_ROWS, L)
    ri = lax.broadcasted_iota(jnp.int32, (L, L), 0)
    ci = lax.broadcasted_iota(jnp.int32, (L, L), 1)
    upper = (ri <= ci).astype(BF16)
    lower = (ri >= ci).astype(BF16)
    dparts = _split3(da2)
    cs_f = sum(_dot(p, upper) for p in dparts)
    cs_b = sum(_dot(p, lower) for p in dparts)
    rowsel = (ri & hpg) == 0
    cs2 = jnp.where(rowsel, cs_f, cs_b)
    cs_scr[...] = cs2.reshape(nc, DT_ROWS, L)
    col_f = sum(_dot_nt(lower, p) for p in dparts)
    col_b = sum(_dot_nt(upper, p) for p in dparts)
    colcs_scr[...] = jnp.where((ci & hpg) == 0, col_f, col_b)

    def lane_bcast(col):
        return jnp.broadcast_to(col, (col.shape[0], L))

    tot2 = jnp.where(rowsel, lane_bcast(cs2[:, L - 1:L]), lane_bcast(cs2[:, 0:1]))
    dec_scr[...] = jnp.exp(tot2).reshape(nc, DT_ROWS, L)
    w_scr[...] = (dtv.reshape(nc * DT_ROWS, L) * jnp.exp(tot2 - cs2)).reshape(nc, DT_ROWS, L)

    lane_t = lax.broadcasted_iota(jnp.int32, (L, L), 1)
    low_half = lane_t < P

    def pass_a(c, carry):
        ws = pl.multiple_of(jnp.clip(c * L - CONV_WIN_LEAD, 0, seq - CONV_WIN), CONV_WIN_LEAD)
        variant = jnp.where(c == 0, 0, jnp.where(c == nc - 1, 2, 1))
        shifted = _dot(sh_ref[variant], xbc_ref[pl.ds(ws, CONV_WIN), :])
        acc = cbias_ref[...]
        for kk in range(SSM_CONV):
            acc = acc + cw_ref[kk:kk + 1, :] * shifted[kk * L:(kk + 1) * L, :]
        xc = acc * jax.nn.sigmoid(acc)
        xc_scr[c] = xc
        for j in range(npair):
            xp = xc[:, pair_cols[j]]
            xm_scr[c, j] = jnp.concatenate([jnp.where(low_half, xp, 0.0), jnp.where(low_half, 0.0, xp)],
                                           axis=0).astype(BF16)
        bt = xc[:, b_cols].T
        for d, ns_scr in ((0, nsf_scr), (1, nsb_scr)):
            w = w_scr[c, d * hpg:(d + 1) * hpg, :]
            ns_scr[c] = jnp.concatenate(
                [_dot(jnp.concatenate([(bt * w[2 * j:2 * j + 1, :]).astype(BF16),
                                       (bt * w[2 * j + 1:2 * j + 2, :]).astype(BF16)], axis=1), xm_scr[c, j])
                 for j in range(npair)], axis=1)
        return carry
    lax.fori_loop(0, nc, pass_a, 0)

    lane_r = lax.broadcasted_iota(jnp.int32, (1, L), 1)

    def decay_row(c, d):
        dec = dec_scr[c, d * hpg:(d + 1) * hpg, :]
        return jnp.concatenate([jnp.where(lane_r < P, dec[2 * j:2 * j + 1, :], dec[2 * j + 1:2 * j + 2, :])
                                for j in range(npair)], axis=1)

    def pass_b(d, ns_scr):
        def body(i, st):
            c = i if d == 0 else nc - 1 - i
            new = ns_scr[c]
            ns_scr[c] = st
            return st * decay_row(c, d) + new
        lax.fori_loop(0, nc, body, jnp.zeros((SSM_STATE, GROUP_INNER), F32))
    pass_b(0, nsf_scr)
    pass_b(1, nsb_scr)

    mask_f = ci <= ri
    mask_b = ci >= ri
    neg_inf = jnp.float32(-jnp.inf)

    def pass_c(c, carry):
        base = pl.multiple_of(c * L, L)
        xc = xc_scr[c]
        bm = xc[:, b_cols].astype(BF16)
        cm = xc[:, c_cols].astype(BF16)
        cb = _dot_nt(cm, bm)
        off_f = _dot(cm, nsf_scr[c].astype(BF16))
        off_b = _dot(cm, nsb_scr[c].astype(BF16))
        colc = pltpu.roll(colcs_scr[...], (L - c * DT_ROWS) % L, 1)
        cs_f_r = cs_scr[c, 0:hpg, :]
        cs_b_r = cs_scr[c, hpg:DT_ROWS, :]
        dt_f_r = dtv_scr[c, 0:hpg, :]
        dt_b_r = dtv_scr[c, hpg:DT_ROWS, :]
        ys = []
        for j in range(npair):
            ws_, sf_, sb_ = [], [], []
            for h in (2 * j, 2 * j + 1):
                bc_f = lane_bcast(colc[:, h:h + 1])
                bc_b = lane_bcast(colc[:, hpg + h:hpg + h + 1])
                ef = jnp.exp(jnp.where(mask_f, bc_f - cs_f_r[h:h + 1, :], neg_inf))
                eb = jnp.exp(jnp.where(mask_b, bc_b - cs_b_r[h:h + 1, :], neg_inf))
                ws_.append((cb * (ef * dt_f_r[h:h + 1, :] + eb * dt_b_r[h:h + 1, :])).astype(BF16))
                sf_.append(jnp.exp(bc_f))
                sb_.append(jnp.exp(bc_b))
            ys.append(_dot(jnp.concatenate(ws_, axis=1), xm_scr[c, j])
                      + jnp.where(low_half, sf_[0], sf_[1]) * off_f[:, pair_cols[j]]
                      + jnp.where(low_half, sb_[0], sb_[1]) * off_b[:, pair_cols[j]])
        y = jnp.concatenate(ys, axis=1) + dskip_ref[...] * xc[:, x_cols]
        zc = z_ref[pl.ds(base, L), :].astype(F32)
        y = y * (zc * jax.nn.sigmoid(zc))
        o_ref[pl.ds(base, L), :] = y * _inv_rms(y) * gn_ref[...]
        return carry
    lax.fori_loop(0, nc, pass_c, 0)


def _conv_shift_matrices():
    out = np.zeros((3, SSM_CONV * SSM_CHUNK, CONV_WIN), np.float32)
    for v, lead in enumerate((0, CONV_WIN_LEAD, CONV_WIN - SSM_CHUNK)):
        for kk in range(SSM_CONV):
            for t in range(SSM_CHUNK):
                j = t + kk - SSM_CONV // 2 + lead
                if 0 <= j < CONV_WIN:
                    out[v, kk * SSM_CHUNK + t, j] = 1.0
    return out


def _ssd_call(xbc, z, dtc, cw, cbias, alog, dtbias, dskip, gn, batch, seq):
    nc = seq // SSM_CHUNK
    L = SSM_CHUNK
    assert nc >= 3 and seq >= CONV_WIN
    sh = jnp.asarray(_conv_shift_matrices(), BF16)
    return pl.pallas_call(
        _ssd_kernel,
        grid=(batch, SSM_GROUPS),
        in_specs=[pl.BlockSpec((seq, GROUP_COLS), lambda b, g: (b, g)),
                  pl.BlockSpec((seq, GROUP_INNER), lambda b, g: (b, g)),
                  pl.BlockSpec((nc, DT_ROWS, L), lambda b, g: (b, g, 0)),
                  pl.BlockSpec(sh.shape, lambda b, g: (0, 0, 0)),
                  pl.BlockSpec((SUBLANES, GROUP_COLS), lambda b, g: (0, g)),
                  pl.BlockSpec((1, GROUP_COLS), lambda b, g: (0, g)),
                  pl.BlockSpec((DT_ROWS, L), lambda b, g: (g, 0)),
                  pl.BlockSpec((DT_ROWS, L), lambda b, g: (g, 0)),
                  pl.BlockSpec((1, GROUP_INNER), lambda b, g: (0, g)),
                  pl.BlockSpec((1, GROUP_INNER), lambda b, g: (0, g))],
        out_specs=pl.BlockSpec((seq, GROUP_INNER), lambda b, g: (b, g)),
        out_shape=jax.ShapeDtypeStruct((batch * seq, SSM_INNER), F32),
        scratch_shapes=[pltpu.VMEM((nc, L, GROUP_COLS), F32),
                        pltpu.VMEM((nc, HEADS_PER_GROUP // 2, 2 * L, L), BF16),
                        pltpu.VMEM((nc, SSM_STATE, GROUP_INNER), F32),
                        pltpu.VMEM((nc, SSM_STATE, GROUP_INNER), F32),
                        pltpu.VMEM((nc, DT_ROWS, L), F32),
                        pltpu.VMEM((nc, DT_ROWS, L), F32),
                        pltpu.VMEM((nc, DT_ROWS, L), F32),
                        pltpu.VMEM((nc, DT_ROWS, L), F32),
                        pltpu.VMEM((L, nc * DT_ROWS), F32)],
        compiler_params=pltpu.CompilerParams(dimension_semantics=("arbitrary", "arbitrary"),
                                             vmem_limit_bytes=VMEM_LIMIT),
        name="ssd",
    )(xbc, z, dtc, sh, cw, cbias, alog, dtbias, dskip, gn)


def _mlp_kernel(x_ref, attn_ref, ssm_ref, gat_ref, wo_ref, gm_ref, wup_ref, wdn_ref, o_ref):
    a = attn_ref[...]
    an = (a * _inv_rms(a) * gat_ref[...]).astype(BF16)
    mix = _dot(an, wo_ref[0:ATTN_WIDTH, :]) + _dot(ssm_ref[...].astype(BF16), wo_ref[ATTN_WIDTH:D_MIX, :])
    x1 = x_ref[...] + mix
    hm = (x1 * _inv_rms(x1) * gm_ref[...]).astype(BF16)
    acc = jnp.zeros_like(x1)
    for c in range(D_FF // FF_CHUNK):
        cols = slice(c * FF_CHUNK, (c + 1) * FF_CHUNK)
        u = _dot(hm, wup_ref[:, cols])
        acc = acc + _dot(jnp.square(jnp.maximum(u, 0.0)).astype(BF16), wdn_ref[cols, :])
    o_ref[...] = x1 + acc


def _mlp_call(x2, attn, ssm, gat, wo, gm, wup, wdn):
    t = x2.shape[0]
    row = lambda w: pl.BlockSpec((TM_MLP, w), lambda i: (i, 0))
    full = lambda a: pl.BlockSpec(a.shape, lambda i: (0, 0), pipeline_mode=pl.Buffered(1))
    return pl.pallas_call(
        _mlp_kernel,
        grid=(t // TM_MLP,),
        in_specs=[row(D_MODEL), row(ATTN_WIDTH), row(SSM_INNER), full(gat), full(wo), full(gm),
                  full(wup), full(wdn)],
        out_specs=row(D_MODEL),
        out_shape=jax.ShapeDtypeStruct((t, D_MODEL), F32),
        compiler_params=pltpu.CompilerParams(dimension_semantics=("arbitrary",),
                                             vmem_limit_bytes=VMEM_LIMIT),
        name="outproj_mlp",
    )(x2, attn, ssm, gat, wo, gm, wup, wdn)


def _ssm_perm():
    xbc, dt = [], []
    for g in range(SSM_GROUPS):
        xbc += list(range(g * GROUP_INNER, (g + 1) * GROUP_INNER))
        xbc += list(range(SSM_INNER + g * SSM_STATE, SSM_INNER + (g + 1) * SSM_STATE))
        xbc += list(range(SSM_INNER + SSM_GROUPS * SSM_STATE + g * SSM_STATE,
                          SSM_INNER + SSM_GROUPS * SSM_STATE + (g + 1) * SSM_STATE))
        dt += list(range(g * HEADS_PER_GROUP, (g + 1) * HEADS_PER_GROUP))
        dt += list(range(SSM_HEADS + g * HEADS_PER_GROUP, SSM_HEADS + (g + 1) * HEADS_PER_GROUP))
    return np.array(xbc), np.array(dt)


def _rotate_half(w):
    half = QK_ROPE_DIM // 2
    return jnp.concatenate([-w[..., half:], w[..., :half]], axis=-1)


def _swap_halves(g):
    half = QK_ROPE_DIM // 2
    return jnp.concatenate([g[..., half:], g[..., :half]], axis=-1)


def _lane_bcast(v):
    return jnp.broadcast_to(v[..., None], v.shape + (LANES,)).astype(F32)


def _layer(x2, cs, batch, seq, ln_mix_g, w_in, q_a_norm_g, w_uq, kv_a_norm_g, w_ukv, q_norm_g,
           k_norm_g, attn_out_norm_g, conv_w, conv_b, a_log_fwd, a_log_bwd, dt_bias_fwd, dt_bias_bwd,
           d_skip, ssm_norm_g, w_out, ln_mlp_g, w_mlp_up, w_mlp_down):
    xbc_perm, dt_perm = _ssm_perm()
    o_kpe = Q_LORA_RANK + KV_LORA_RANK
    o_z = o_kpe + QK_ROPE_DIM
    o_xbc = o_z + SSM_INNER
    o_dt = o_xbc + SSM_CONV_CH
    d = w_in.shape[0]

    w_kpe = w_in[:, o_kpe:o_z]
    misc = jnp.concatenate([w_in[:, o_dt:][:, dt_perm],
                            jnp.zeros((d, QK_NOPE_DIM - 2 * SSM_HEADS), w_in.dtype),
                            w_kpe, _rotate_half(w_kpe)], axis=1)
    win = jnp.concatenate([w_in[:, 0:o_kpe], misc, w_in[:, o_z:o_xbc],
                           w_in[:, o_xbc:o_dt][:, xbc_perm]], axis=1).astype(BF16)
    wq = w_uq.reshape(Q_LORA_RANK, ATTN_HEADS, QK_HEAD_DIM)
    wuq = jnp.concatenate([wq, _rotate_half(wq[..., QK_NOPE_DIM:])], axis=-1)
    wuq = wuq.reshape(Q_LORA_RANK, ATTN_HEADS * HEAD_PAD).astype(BF16)
    wukv = w_ukv.reshape(KV_LORA_RANK, ATTN_HEADS, QK_NOPE_DIM + V_HEAD_DIM)
    wuk = jnp.pad(wukv[:, :, :QK_NOPE_DIM],
                  ((0, 0), (0, 0), (0, HEAD_PAD - QK_NOPE_DIM))).reshape(KV_LORA_RANK, -1).astype(BF16)
    wv = wukv[:, :, QK_NOPE_DIM:].reshape(KV_LORA_RANK, ATTN_HEADS // 2, 2, V_HEAD_DIM)
    zv = jnp.zeros_like(wv[:, :, 0])
    wuv = jnp.stack([jnp.concatenate([wv[:, :, 0], zv], axis=-1),
                     jnp.concatenate([zv, wv[:, :, 1]], axis=-1)], axis=2).reshape(KV_LORA_RANK, -1).astype(BF16)
    ones_v = jnp.ones((ATTN_HEADS // 2, V_HEAD_DIM), F32)
    vone = jnp.stack([jnp.concatenate([0 * ones_v, ones_v], axis=-1),
                      jnp.concatenate([ones_v, 0 * ones_v], axis=-1)], axis=1).reshape(1, -1)
    scale = QK_HEAD_DIM ** -0.5 * np.log2(np.e)
    gq = (jnp.concatenate([q_norm_g, _swap_halves(q_norm_g[QK_NOPE_DIM:])]) * scale)[None, :]
    zeros_nope = jnp.zeros((QK_NOPE_DIM,), k_norm_g.dtype)
    gkn = jnp.concatenate([k_norm_g[:QK_NOPE_DIM], zeros_nope])[None, :]
    gkr = jnp.concatenate([zeros_nope, k_norm_g[QK_NOPE_DIM:], _swap_halves(k_norm_g[QK_NOPE_DIM:])])[None, :]

    q, k, v, z, xbc, dtc = _inproj_call(
        x2, cs, ln_mix_g[None, :], win, q_a_norm_g[None, :], wuq, kv_a_norm_g[None, :], wuk, wuv, vone,
        gq, gkn, gkr)

    attn = _attn_call(q, k, v, batch, seq)

    cw = jnp.pad(conv_w[:, 0, :][:, xbc_perm], ((0, SUBLANES - SSM_CONV), (0, 0)))
    cbias = conv_b[xbc_perm][None, :]
    alog = _lane_bcast(jnp.concatenate([a_log_fwd, a_log_bwd])[dt_perm])
    dtbias = _lane_bcast(jnp.concatenate([dt_bias_fwd, dt_bias_bwd])[dt_perm])
    dskip = jnp.repeat(d_skip, SSM_HEAD_DIM)[None, :]
    ssm = _ssd_call(xbc, z, dtc, cw, cbias, alog, dtbias, dskip, ssm_norm_g[None, :], batch, seq)

    return _mlp_call(x2, attn, ssm, attn_out_norm_g[None, :], w_out.astype(BF16), ln_mlp_g[None, :],
                     w_mlp_up.astype(BF16), w_mlp_down.astype(BF16))


def _rope_table(positions):
    inv_freq = 1.0 / (ROPE_THETA ** (jnp.arange(0, QK_ROPE_DIM, 2, dtype=F32) / QK_ROPE_DIM))
    ang = positions.astype(F32).reshape(-1, 1) * inv_freq
    ang = jnp.concatenate([ang, ang], axis=-1)
    return jnp.concatenate([jnp.ones((ang.shape[0], QK_NOPE_DIM), F32), jnp.cos(ang), jnp.sin(ang)], axis=1)


def kernel(x, positions, ln_mix_g, w_in, q_a_norm_g, w_uq, kv_a_norm_g, w_ukv, q_norm_g, k_norm_g,
           attn_out_norm_g, conv_w, conv_b, a_log_fwd, a_log_bwd, dt_bias_fwd, dt_bias_bwd, d_skip,
           ssm_norm_g, w_out, ln_mlp_g, w_mlp_up, w_mlp_down):
    batch, seq, d = x.shape
    assert d == D_MODEL and seq % TQ == 0 and (batch * seq) % TM_IN == 0 and (batch * seq) % TM_MLP == 0
    cs = _rope_table(positions)
    x2 = x.reshape(batch * seq, d)
    for l in range(ln_mix_g.shape[0]):
        x2 = _layer(x2, cs, batch, seq, ln_mix_g[l], w_in[l], q_a_norm_g[l], w_uq[l], kv_a_norm_g[l],
                    w_ukv[l], q_norm_g[l], k_norm_g[l], attn_out_norm_g[l], conv_w[l], conv_b[l],
                    a_log_fwd[l], a_log_bwd[l], dt_bias_fwd[l], dt_bias_bwd[l], d_skip[l], ssm_norm_g[l],
                    w_out[l], ln_mlp_g[l], w_mlp_up[l], w_mlp_down[l])
    return x2.reshape(batch, seq, d)
```

```python
import numpy as np
import jax
import jax.numpy as jnp
from jax import lax
from jax.experimental import pallas as pl
from jax.experimental.pallas import tpu as pltpu

F32 = jnp.float32
BF16 = jnp.bfloat16

D_MODEL = 1024
ATTN_HEADS = 8
QK_NOPE_DIM = 64
QK_ROPE_DIM = 32
QK_HEAD_DIM = QK_NOPE_DIM + QK_ROPE_DIM
V_HEAD_DIM = 64
Q_LORA_RANK = D_MODEL // 4
KV_LORA_RANK = D_MODEL // 8
ROPE_THETA = 10000.0
ATTN_WIDTH = ATTN_HEADS * V_HEAD_DIM
SSM_HEADS = 8
SSM_HEAD_DIM = 64
SSM_INNER = SSM_HEADS * SSM_HEAD_DIM
SSM_GROUPS = 2
SSM_STATE = 128
SSM_CONV = 5
SSM_CHUNK = 128
SSM_CONV_CH = SSM_INNER + 2 * SSM_GROUPS * SSM_STATE
D_MIX = ATTN_WIDTH + SSM_INNER
D_FF = 4 * D_MODEL
EPS = 1e-6

LANES = 128
SUBLANES = 8
HEAD_PAD = LANES

HEADS_PER_GROUP = SSM_HEADS // SSM_GROUPS
GROUP_INNER = SSM_INNER // SSM_GROUPS
GROUP_COLS = GROUP_INNER + 2 * SSM_STATE
DT_ROWS = 2 * HEADS_PER_GROUP
SEG_K_PER_ROW = 16
SEG_BLOCKS = DT_ROWS + HEADS_PER_GROUP
LOG2E = float(np.log2(np.e))
CONV_WIN = 2 * SSM_CHUNK
CONV_WIN_LEAD = SSM_CHUNK // 2
CONV_SHIFTED_TAPS = tuple(k for k in range(SSM_CONV) if k != SSM_CONV // 2)

COL_CKV = Q_LORA_RANK
COL_MISC = COL_CKV + KV_LORA_RANK
COL_Z = COL_MISC + LANES
COL_XBC = COL_Z + SSM_INNER
IN_COLS = COL_XBC + SSM_CONV_CH

TM_IN = 512
TQ = 512
ATTN_HEADS_PER_STEP = 8
TM_MLP = 512
FF_CHUNK = 1024
SSD_UNROLL = 4
PASS_A_UNROLL = 3
VMEM_LIMIT = 56 * 1024 * 1024


def _inv_rms(x):
    n = x.shape[-1]
    x2 = x * x
    acc = x2[:, 0:LANES]
    for i in range(1, n // LANES):
        acc = acc + x2[:, i * LANES:(i + 1) * LANES]
    return lax.rsqrt(jnp.sum(acc, axis=-1, keepdims=True) * (1.0 / n) + EPS)


def _dot(a, b):
    return jnp.dot(a, b, preferred_element_type=F32)


def _dot_nt(a, b):
    return lax.dot_general(a, b, (((1,), (1,)), ((), ())), preferred_element_type=F32)


def _inproj_kernel(x_ref, cs_ref, g_ref, win_ref, gqa_ref, wuq_ref, gkva_ref, wuk_ref, wuv_ref, vone_ref,
                   gq_ref, gkn_ref, gkr_ref,
                   q_ref, k_ref, v_ref, z_ref, xbc_ref, dt_ref):
    tm = x_ref.shape[0]
    x = x_ref[...]
    h = (x * _inv_rms(x) * g_ref[...]).astype(BF16)
    big = _dot(h, win_ref[...])
    z_ref[...] = big[:, COL_Z:COL_XBC].astype(BF16)
    xbc_ref[...] = big[:, COL_XBC:IN_COLS].astype(BF16)
    misc = big[:, COL_MISC:COL_Z]
    for c in range(tm // SSM_CHUNK):
        dt_ref[c] = misc[c * SSM_CHUNK:(c + 1) * SSM_CHUNK, :].T[0:2 * SSM_HEADS, :]

    cq = big[:, 0:COL_CKV]
    ckv = big[:, COL_CKV:COL_MISC]
    cqn = (cq * _inv_rms(cq) * gqa_ref[...]).astype(BF16)
    ckvn = (ckv * _inv_rms(ckv) * gkva_ref[...]).astype(BF16)
    q_pre = _dot(cqn, wuq_ref[...])
    k_pre = _dot(ckvn, wuk_ref[...])
    v_ref[...] = (_dot(ckvn, wuv_ref[...]) + vone_ref[...]).astype(BF16)

    cs = cs_ref[...]
    lane = lax.broadcasted_iota(jnp.int32, (1, HEAD_PAD), 1)
    in_head = (lane < QK_HEAD_DIM).astype(F32)
    is_rope = ((lane >= QK_NOPE_DIM) & (lane < QK_HEAD_DIM)).astype(F32)
    inv_d = 1.0 / QK_HEAD_DIM

    gcq = gq_ref[...] * cs
    for hh in range(ATTN_HEADS):
        sl = slice(hh * HEAD_PAD, (hh + 1) * HEAD_PAD)
        qh = q_pre[:, sl]
        ssq = jnp.sum(qh * qh * in_head, axis=-1, keepdims=True)
        q_ref[:, sl] = (qh * lax.rsqrt(ssq * inv_d + EPS) * gcq).astype(BF16)

    ab = misc * (gkr_ref[...] * cs)
    lane2 = lax.broadcasted_iota(jnp.int32, (tm, HEAD_PAD), 1)
    swapped = jnp.where(lane2 < QK_HEAD_DIM, pltpu.roll(ab, HEAD_PAD - QK_ROPE_DIM, 1),
                        pltpu.roll(ab, QK_ROPE_DIM, 1))
    s_both = jnp.where(lane2 >= QK_NOPE_DIM, ab + swapped, 0.0)
    ssq_pe = jnp.sum(misc * misc * is_rope, axis=-1, keepdims=True)
    gkn = gkn_ref[...]
    for hh in range(ATTN_HEADS):
        sl = slice(hh * HEAD_PAD, (hh + 1) * HEAD_PAD)
        kh = k_pre[:, sl]
        ssq = jnp.sum(kh * kh, axis=-1, keepdims=True) + ssq_pe
        k_ref[:, sl] = ((kh * gkn + s_both) * lax.rsqrt(ssq * inv_d + EPS)).astype(BF16)


def _inproj_call(x2, cs, g, win, gqa, wuq, gkva, wuk, wuv, vone, gq, gkn, gkr):
    t = x2.shape[0]
    cpt = TM_IN // SSM_CHUNK
    full = lambda a: pl.BlockSpec(a.shape, lambda i: (0,) * a.ndim)
    row = lambda w: pl.BlockSpec((TM_IN, w), lambda i: (i, 0))
    return pl.pallas_call(
        _inproj_kernel,
        grid=(t // TM_IN,),
        in_specs=[row(D_MODEL), row(HEAD_PAD), full(g), full(win), full(gqa), full(wuq), full(gkva),
                  full(wuk), full(wuv), full(vone), full(gq), full(gkn), full(gkr)],
        out_specs=[row(ATTN_HEADS * HEAD_PAD), row(ATTN_HEADS * HEAD_PAD), row(ATTN_HEADS * HEAD_PAD),
                   row(SSM_INNER), row(SSM_CONV_CH),
                   pl.BlockSpec((cpt, 2 * SSM_HEADS, SSM_CHUNK), lambda i: (i, 0, 0))],
        out_shape=[jax.ShapeDtypeStruct((t, ATTN_HEADS * HEAD_PAD), BF16),
                   jax.ShapeDtypeStruct((t, ATTN_HEADS * HEAD_PAD), BF16),
                   jax.ShapeDtypeStruct((t, ATTN_HEADS * HEAD_PAD), BF16),
                   jax.ShapeDtypeStruct((t, SSM_INNER), BF16),
                   jax.ShapeDtypeStruct((t, SSM_CONV_CH), BF16),
                   jax.ShapeDtypeStruct((t // SSM_CHUNK, 2 * SSM_HEADS, SSM_CHUNK), F32)],
        compiler_params=pltpu.CompilerParams(dimension_semantics=("arbitrary",),
                                             vmem_limit_bytes=VMEM_LIMIT),
        name="inproj",
    )(x2, cs, g, win, gqa, wuq, gkva, wuk, wuv, vone, gq, gkn, gkr)


def _attn_kernel(q_ref, k_ref, v_ref, o_ref):
    res = []
    for j in range(ATTN_HEADS_PER_STEP):
        sl = slice(j * HEAD_PAD, (j + 1) * HEAD_PAD)
        s = _dot_nt(q_ref[:, sl], k_ref[:, sl])
        m = jnp.max(s, axis=-1, keepdims=True)
        acc = _dot(jnp.exp2(s - m).astype(BF16), v_ref[:, sl])
        res.append(acc / pltpu.roll(acc, V_HEAD_DIM, 1))
    lane = lax.broadcasted_iota(jnp.int32, res[0].shape, 1)
    for jp in range(ATTN_HEADS_PER_STEP // 2):
        o_ref[:, jp * HEAD_PAD:(jp + 1) * HEAD_PAD] = jnp.where(lane < V_HEAD_DIM, res[2 * jp], res[2 * jp + 1])


def _attn_call(q, k, v, batch, seq):
    n_q = seq // TQ
    hps = ATTN_HEADS_PER_STEP
    return pl.pallas_call(
        _attn_kernel,
        grid=(batch, ATTN_HEADS // hps, n_q),
        in_specs=[pl.BlockSpec((TQ, hps * HEAD_PAD), lambda b, hp, i: (b * n_q + i, hp)),
                  pl.BlockSpec((seq, hps * HEAD_PAD), lambda b, hp, i: (b, hp)),
                  pl.BlockSpec((seq, hps * HEAD_PAD), lambda b, hp, i: (b, hp))],
        out_specs=pl.BlockSpec((TQ, hps * V_HEAD_DIM), lambda b, hp, i: (b * n_q + i, hp)),
        out_shape=jax.ShapeDtypeStruct((batch * seq, ATTN_WIDTH), F32),
        compiler_params=pltpu.CompilerParams(
            dimension_semantics=("arbitrary", "arbitrary", "arbitrary"),
            vmem_limit_bytes=VMEM_LIMIT),
        name="attention",
    )(q, k, v)


def _split3(x):
    hi = x.astype(BF16)
    r1 = x - hi.astype(F32)
    mid = r1.astype(BF16)
    lo = (r1 - mid.astype(F32)).astype(BF16)
    return hi, mid, lo


def _ssd_kernel(xbc_ref, z_ref, dt_ref, sh_ref, place_ref, lhs_ones_ref, rhs_const_ref, cw_ref, cbias_ref,
                alog_ref, dtbias_ref, dskip_ref, gn_ref,
                o_ref, xc_scr, xm_scr, nsf_scr, nsb_scr, dtv_scr, w_scr, dec_scr, lhs_scr, rhs_scr, rpart_scr):
    nc = dt_ref.shape[0]
    seq = xbc_ref.shape[0]
    L = SSM_CHUNK
    hpg = HEADS_PER_GROUP
    npair = hpg // 2
    P = SSM_HEAD_DIM
    x_cols = slice(0, GROUP_INNER)
    b_cols = slice(GROUP_INNER, GROUP_INNER + SSM_STATE)
    c_cols = slice(GROUP_INNER + SSM_STATE, GROUP_COLS)
    pair_cols = [slice(j * L, (j + 1) * L) for j in range(npair)]

    a_neg = -jnp.exp(alog_ref[...]) * LOG2E
    dtv = jax.nn.softplus(dt_ref[...] + dtbias_ref[...][None])
    dtv_scr[...] = dtv
    da2 = (dtv * a_neg[None]).reshape(nc * DT_ROWS, L)
    ri = lax.broadcasted_iota(jnp.int32, (L, L), 0)
    ci = lax.broadcasted_iota(jnp.int32, (L, L), 1)
    upper = (ri <= ci).astype(BF16)
    lower = (ri >= ci).astype(BF16)
    dparts = _split3(da2)
    cs_f = sum(_dot(p, upper) for p in dparts)
    cs_b = sum(_dot(p, lower) for p in dparts)
    rowsel = (ri & hpg) == 0
    cs2 = jnp.where(rowsel, cs_f, cs_b)
    col_f = sum(_dot_nt(lower, p) for p in dparts)
    col_b = sum(_dot_nt(upper, p) for p in dparts)
    colcs = jnp.where((ci & hpg) == 0, col_f, col_b)

    lhs_all = _dot(jnp.concatenate(_split3(colcs), axis=1), place_ref[...]) + lhs_ones_ref[...]
    for c in range(nc):
        lhs_scr[c] = lhs_all[:, c * L:(c + 1) * L].astype(BF16)
    for i, part in enumerate(_split3(-cs2)):
        rpart_scr[i] = part.astype(F32).reshape(nc, DT_ROWS, L)
    for u in range(SSD_UNROLL):
        rhs_scr[u] = rhs_const_ref[...]

    def lane_bcast(col):
        return jnp.broadcast_to(col, (col.shape[0], L))

    tot2 = jnp.where(rowsel, lane_bcast(cs2[:, L - 1:L]), lane_bcast(cs2[:, 0:1]))
    dec_scr[...] = jnp.exp2(tot2).reshape(nc, DT_ROWS, L)
    w_scr[...] = (dtv.reshape(nc * DT_ROWS, L) * jnp.exp2(tot2 - cs2)).reshape(nc, DT_ROWS, L)

    lane_t = lax.broadcasted_iota(jnp.int32, (L, L), 1)
    low_half = lane_t < P

    half_cols = GROUP_COLS // 2

    def conv_stage(c):
        base = pl.multiple_of(c * L, L)
        ws = pl.multiple_of(jnp.clip(c * L - CONV_WIN_LEAD, 0, seq - CONV_WIN), CONV_WIN_LEAD)
        variant = jnp.where(c == 0, 0, jnp.where(c == nc - 1, 2, 1))
        sh = sh_ref[variant]
        halves = []
        for hf in range(2):
            cols = slice(hf * half_cols, (hf + 1) * half_cols)
            shifted = _dot(sh, xbc_ref[pl.ds(ws, CONV_WIN), cols])
            acc = cbias_ref[:, cols] + (cw_ref[SSM_CONV // 2:SSM_CONV // 2 + 1, cols]
                                        * xbc_ref[pl.ds(base, L), cols].astype(F32))
            for t, kk in enumerate(CONV_SHIFTED_TAPS):
                acc = acc + cw_ref[kk:kk + 1, cols] * shifted[t * L:(t + 1) * L, :]
            halves.append(acc * jax.nn.sigmoid(acc))
        xc = jnp.concatenate(halves, axis=1)
        xc_scr[c] = xc
        for j in range(npair):
            xp = xc[:, pair_cols[j]]
            xm_scr[c, j] = jnp.concatenate([jnp.where(low_half, xp, 0.0), jnp.where(low_half, 0.0, xp)],
                                           axis=0).astype(BF16)

    def state_loads(c):
        return xc_scr[c, :, b_cols], [xm_scr[c, j] for j in range(npair)]

    def state_stage(c, loaded):
        b_tok, xm = loaded
        bt = b_tok.T
        for d, ns_scr in ((0, nsf_scr), (1, nsb_scr)):
            w = w_scr[c, d * hpg:(d + 1) * hpg, :]
            ns_scr[c] = jnp.concatenate(
                [_dot(jnp.concatenate([(bt * w[2 * j:2 * j + 1, :]).astype(BF16),
                                       (bt * w[2 * j + 1:2 * j + 2, :]).astype(BF16)], axis=1), xm[j])
                 for j in range(npair)], axis=1)

    conv_stage(0)

    def pass_a(c, carry):
        loaded = state_loads(c)
        conv_stage(c + 1)
        state_stage(c, loaded)
        return carry
    lax.fori_loop(0, nc - 1, pass_a, 0, unroll=PASS_A_UNROLL)
    state_stage(nc - 1, state_loads(nc - 1))

    lane_r = lax.broadcasted_iota(jnp.int32, (1, L), 1)

    def decay_row(c, d):
        dec = dec_scr[c, d * hpg:(d + 1) * hpg, :]
        return jnp.concatenate([jnp.where(lane_r < P, dec[2 * j:2 * j + 1, :], dec[2 * j + 1:2 * j + 2, :])
                                for j in range(npair)], axis=1)

    def pass_b(d, ns_scr):
        def body(i, st):
            c = i if d == 0 else nc - 1 - i
            new = ns_scr[c]
            ns_scr[c] = st
            return st * decay_row(c, d) + new
        lax.fori_loop(0, nc, body, jnp.zeros((SSM_STATE, GROUP_INNER), F32))
    pass_b(0, nsf_scr)
    pass_b(1, nsb_scr)

    mask_f = ci <= ri
    mask_b = ci >= ri
    neg_inf = jnp.float32(-jnp.inf)
    sub16 = lax.broadcasted_iota(jnp.int32, (SEG_K_PER_ROW, L), 0)

    def pass_c(c, carry):
        base = pl.multiple_of(c * L, L)
        xc = xc_scr[c]
        bm = xc[:, b_cols].astype(BF16)
        cm = xc[:, c_cols].astype(BF16)
        cb = _dot_nt(cm, bm)
        off_f = _dot(cm, nsf_scr[c].astype(BF16))
        off_b = _dot(cm, nsb_scr[c].astype(BF16))
        slot = c % SSD_UNROLL
        for hd in range(DT_ROWS):
            blk = jnp.where(sub16 < 3, 1.0, 0.0)
            for i in range(3):
                blk = jnp.where(sub16 == 3 + i, rpart_scr[i, c, hd:hd + 1, :], blk)
            rhs_scr[slot, hd * SEG_K_PER_ROW:(hd + 1) * SEG_K_PER_ROW, hd * L:(hd + 1) * L] = blk.astype(BF16)
        seg = _dot(lhs_scr[c], rhs_scr[slot])
        dt_f_r = dtv_scr[c, 0:hpg, :]
        dt_b_r = dtv_scr[c, hpg:DT_ROWS, :]
        ys = []
        for j in range(npair):
            ws_ = []
            for h in (2 * j, 2 * j + 1):
                ef = jnp.exp2(jnp.where(mask_f, seg[:, h * L:(h + 1) * L], neg_inf))
                eb = jnp.exp2(jnp.where(mask_b, seg[:, (hpg + h) * L:(hpg + h + 1) * L], neg_inf))
                ws_.append((cb * (ef * dt_f_r[h:h + 1, :] + eb * dt_b_r[h:h + 1, :])).astype(BF16))
            scale_f = jnp.exp2(seg[:, (DT_ROWS + j) * L:(DT_ROWS + j + 1) * L])
            scale_b = jnp.exp2(seg[:, (DT_ROWS + npair + j) * L:(DT_ROWS + npair + j + 1) * L])
            ys.append(_dot(jnp.concatenate(ws_, axis=1), xm_scr[c, j])
                      + scale_f * off_f[:, pair_cols[j]] + scale_b * off_b[:, pair_cols[j]])
        y = jnp.concatenate(ys, axis=1) + dskip_ref[...] * xc[:, x_cols]
        zc = z_ref[pl.ds(base, L), :].astype(F32)
        y = y * (zc * jax.nn.sigmoid(zc))
        o_ref[pl.ds(base, L), :] = y * _inv_rms(y) * gn_ref[...]
        return carry
    lax.fori_loop(0, nc, pass_c, 0, unroll=SSD_UNROLL)


def _conv_shift_matrices():
    out = np.zeros((3, len(CONV_SHIFTED_TAPS) * SSM_CHUNK, CONV_WIN), np.float32)
    for v, lead in enumerate((0, CONV_WIN_LEAD, CONV_WIN - SSM_CHUNK)):
        for i, kk in enumerate(CONV_SHIFTED_TAPS):
            for t in range(SSM_CHUNK):
                j = t + kk - SSM_CONV // 2 + lead
                if 0 <= j < CONV_WIN:
                    out[v, i * SSM_CHUNK + t, j] = 1.0
    return out


def _seg_matmul_constants(nc):
    L, K, half = SSM_CHUNK, SEG_K_PER_ROW, SSM_HEAD_DIM
    place = np.zeros((3 * L, nc * L), np.float32)
    lhs_ones = np.zeros((1, nc * L), np.float32)
    for c in range(nc):
        for hd in range(DT_ROWS):
            for i in range(3):
                place[i * L + c * DT_ROWS + hd, c * L + hd * K + i] = 1.0
                lhs_ones[0, c * L + hd * K + 3 + i] = 1.0
    rhs = np.zeros((L, SEG_BLOCKS * L), np.float32)
    for hd in range(DT_ROWS):
        d, h = divmod(hd, HEADS_PER_GROUP)
        pair_block = DT_ROWS + d * (HEADS_PER_GROUP // 2) + h // 2
        lanes = slice(0, half) if h % 2 == 0 else slice(half, L)
        rhs[hd * K:hd * K + 3, hd * L:(hd + 1) * L] = 1.0
        rhs[hd * K:hd * K + 3, pair_block * L:(pair_block + 1) * L][:, lanes] = 1.0
    return place, lhs_ones, rhs


def _ssd_call(xbc, z, dtc, cw, cbias, alog, dtbias, dskip, gn, batch, seq):
    nc = seq // SSM_CHUNK
    L = SSM_CHUNK
    assert nc >= 3 and seq >= CONV_WIN and nc * DT_ROWS == L
    sh = jnp.asarray(_conv_shift_matrices(), BF16)
    place, lhs_ones, rhs_const = _seg_matmul_constants(nc)
    place = jnp.asarray(place, BF16)
    lhs_ones = jnp.asarray(lhs_ones, F32)
    rhs_const = jnp.asarray(rhs_const, BF16)
    const = lambda a: pl.BlockSpec(a.shape, lambda b, g: (0,) * a.ndim)
    return pl.pallas_call(
        _ssd_kernel,
        grid=(batch, SSM_GROUPS),
        in_specs=[pl.BlockSpec((seq, GROUP_COLS), lambda b, g: (b, g)),
                  pl.BlockSpec((seq, GROUP_INNER), lambda b, g: (b, g)),
                  pl.BlockSpec((nc, DT_ROWS, L), lambda b, g: (b, g, 0)),
                  const(sh), const(place), const(lhs_ones), const(rhs_const),
                  pl.BlockSpec((SUBLANES, GROUP_COLS), lambda b, g: (0, g)),
                  pl.BlockSpec((1, GROUP_COLS), lambda b, g: (0, g)),
                  pl.BlockSpec((DT_ROWS, L), lambda b, g: (g, 0)),
                  pl.BlockSpec((DT_ROWS, L), lambda b, g: (g, 0)),
                  pl.BlockSpec((1, GROUP_INNER), lambda b, g: (0, g)),
                  pl.BlockSpec((1, GROUP_INNER), lambda b, g: (0, g))],
        out_specs=pl.BlockSpec((seq, GROUP_INNER), lambda b, g: (b, g)),
        out_shape=jax.ShapeDtypeStruct((batch * seq, SSM_INNER), F32),
        scratch_shapes=[pltpu.VMEM((nc, L, GROUP_COLS), F32),
                        pltpu.VMEM((nc, HEADS_PER_GROUP // 2, 2 * L, L), BF16),
                        pltpu.VMEM((nc, SSM_STATE, GROUP_INNER), F32),
                        pltpu.VMEM((nc, SSM_STATE, GROUP_INNER), F32),
                        pltpu.VMEM((nc, DT_ROWS, L), F32),
                        pltpu.VMEM((nc, DT_ROWS, L), F32),
                        pltpu.VMEM((nc, DT_ROWS, L), F32),
                        pltpu.VMEM((nc, L, L), BF16),
                        pltpu.VMEM((SSD_UNROLL, L, SEG_BLOCKS * L), BF16),
                        pltpu.VMEM((3, nc, DT_ROWS, L), F32)],
        compiler_params=pltpu.CompilerParams(dimension_semantics=("arbitrary", "arbitrary"),
                                             vmem_limit_bytes=VMEM_LIMIT),
        name="ssd",
    )(xbc, z, dtc, sh, place, lhs_ones, rhs_const, cw, cbias, alog, dtbias, dskip, gn)


def _mlp_kernel(x_ref, attn_ref, ssm_ref, gat_ref, wo_ref, gm_ref, wup_ref, wdn_ref, o_ref):
    a = attn_ref[...]
    an = (a * _inv_rms(a) * gat_ref[...]).astype(BF16)
    mix = _dot(an, wo_ref[0:ATTN_WIDTH, :]) + _dot(ssm_ref[...].astype(BF16), wo_ref[ATTN_WIDTH:D_MIX, :])
    x1 = x_ref[...] + mix
    hm = (x1 * _inv_rms(x1) * gm_ref[...]).astype(BF16)
    acc = jnp.zeros_like(x1)
    for c in range(D_FF // FF_CHUNK):
        cols = slice(c * FF_CHUNK, (c + 1) * FF_CHUNK)
        u = _dot(hm, wup_ref[:, cols])
        acc = acc + _dot(jnp.square(jnp.maximum(u, 0.0)).astype(BF16), wdn_ref[cols, :])
    o_ref[...] = x1 + acc


def _mlp_call(x2, attn, ssm, gat, wo, gm, wup, wdn):
    t = x2.shape[0]
    row = lambda w: pl.BlockSpec((TM_MLP, w), lambda i: (i, 0))
    full = lambda a: pl.BlockSpec(a.shape, lambda i: (0, 0), pipeline_mode=pl.Buffered(1))
    return pl.pallas_call(
        _mlp_kernel,
        grid=(t // TM_MLP,),
        in_specs=[row(D_MODEL), row(ATTN_WIDTH), row(SSM_INNER), full(gat), full(wo), full(gm),
                  full(wup), full(wdn)],
        out_specs=row(D_MODEL),
        out_shape=jax.ShapeDtypeStruct((t, D_MODEL), F32),
        compiler_params=pltpu.CompilerParams(dimension_semantics=("arbitrary",),
                                             vmem_limit_bytes=VMEM_LIMIT),
        name="outproj_mlp",
    )(x2, attn, ssm, gat, wo, gm, wup, wdn)


def _ssm_perm():
    xbc, dt = [], []
    for g in range(SSM_GROUPS):
        xbc += list(range(g * GROUP_INNER, (g + 1) * GROUP_INNER))
        xbc += list(range(SSM_INNER + g * SSM_STATE, SSM_INNER + (g + 1) * SSM_STATE))
        xbc += list(range(SSM_INNER + SSM_GROUPS * SSM_STATE + g * SSM_STATE,
                          SSM_INNER + SSM_GROUPS * SSM_STATE + (g + 1) * SSM_STATE))
        dt += list(range(g * HEADS_PER_GROUP, (g + 1) * HEADS_PER_GROUP))
        dt += list(range(SSM_HEADS + g * HEADS_PER_GROUP, SSM_HEADS + (g + 1) * HEADS_PER_GROUP))
    return np.array(xbc), np.array(dt)


def _rotate_half(w):
    half = QK_ROPE_DIM // 2
    return jnp.concatenate([-w[..., half:], w[..., :half]], axis=-1)


def _swap_halves(g):
    half = QK_ROPE_DIM // 2
    return jnp.concatenate([g[..., half:], g[..., :half]], axis=-1)


def _lane_bcast(v):
    return jnp.broadcast_to(v[..., None], v.shape + (LANES,)).astype(F32)


def _layer(x2, cs, batch, seq, ln_mix_g, w_in, q_a_norm_g, w_uq, kv_a_norm_g, w_ukv, q_norm_g,
           k_norm_g, attn_out_norm_g, conv_w, conv_b, a_log_fwd, a_log_bwd, dt_bias_fwd, dt_bias_bwd,
           d_skip, ssm_norm_g, w_out, ln_mlp_g, w_mlp_up, w_mlp_down):
    xbc_perm, dt_perm = _ssm_perm()
    o_kpe = Q_LORA_RANK + KV_LORA_RANK
    o_z = o_kpe + QK_ROPE_DIM
    o_xbc = o_z + SSM_INNER
    o_dt = o_xbc + SSM_CONV_CH
    d = w_in.shape[0]

    w_kpe = w_in[:, o_kpe:o_z]
    misc = jnp.concatenate([w_in[:, o_dt:][:, dt_perm],
                            jnp.zeros((d, QK_NOPE_DIM - 2 * SSM_HEADS), w_in.dtype),
                            w_kpe, _rotate_half(w_kpe)], axis=1)
    win = jnp.concatenate([w_in[:, 0:o_kpe], misc, w_in[:, o_z:o_xbc],
                           w_in[:, o_xbc:o_dt][:, xbc_perm]], axis=1).astype(BF16)
    wq = w_uq.reshape(Q_LORA_RANK, ATTN_HEADS, QK_HEAD_DIM)
    wuq = jnp.concatenate([wq, _rotate_half(wq[..., QK_NOPE_DIM:])], axis=-1)
    wuq = wuq.reshape(Q_LORA_RANK, ATTN_HEADS * HEAD_PAD).astype(BF16)
    wukv = w_ukv.reshape(KV_LORA_RANK, ATTN_HEADS, QK_NOPE_DIM + V_HEAD_DIM)
    wuk = jnp.pad(wukv[:, :, :QK_NOPE_DIM],
                  ((0, 0), (0, 0), (0, HEAD_PAD - QK_NOPE_DIM))).reshape(KV_LORA_RANK, -1).astype(BF16)
    wv = wukv[:, :, QK_NOPE_DIM:].reshape(KV_LORA_RANK, ATTN_HEADS // 2, 2, V_HEAD_DIM)
    zv = jnp.zeros_like(wv[:, :, 0])
    wuv = jnp.stack([jnp.concatenate([wv[:, :, 0], zv], axis=-1),
                     jnp.concatenate([zv, wv[:, :, 1]], axis=-1)], axis=2).reshape(KV_LORA_RANK, -1).astype(BF16)
    ones_v = jnp.ones((ATTN_HEADS // 2, V_HEAD_DIM), F32)
    vone = jnp.stack([jnp.concatenate([0 * ones_v, ones_v], axis=-1),
                      jnp.concatenate([ones_v, 0 * ones_v], axis=-1)], axis=1).reshape(1, -1)
    scale = QK_HEAD_DIM ** -0.5 * np.log2(np.e)
    gq = (jnp.concatenate([q_norm_g, _swap_halves(q_norm_g[QK_NOPE_DIM:])]) * scale)[None, :]
    zeros_nope = jnp.zeros((QK_NOPE_DIM,), k_norm_g.dtype)
    gkn = jnp.concatenate([k_norm_g[:QK_NOPE_DIM], zeros_nope])[None, :]
    gkr = jnp.concatenate([zeros_nope, k_norm_g[QK_NOPE_DIM:], _swap_halves(k_norm_g[QK_NOPE_DIM:])])[None, :]

    q, k, v, z, xbc, dtc = _inproj_call(
        x2, cs, ln_mix_g[None, :], win, q_a_norm_g[None, :], wuq, kv_a_norm_g[None, :], wuk, wuv, vone,
        gq, gkn, gkr)

    attn = _attn_call(q, k, v, batch, seq)

    cw = jnp.pad(conv_w[:, 0, :][:, xbc_perm], ((0, SUBLANES - SSM_CONV), (0, 0)))
    cbias = conv_b[xbc_perm][None, :]
    alog = _lane_bcast(jnp.concatenate([a_log_fwd, a_log_bwd])[dt_perm])
    dtbias = _lane_bcast(jnp.concatenate([dt_bias_fwd, dt_bias_bwd])[dt_perm])
    dskip = jnp.repeat(d_skip, SSM_HEAD_DIM)[None, :]
    ssm = _ssd_call(xbc, z, dtc, cw, cbias, alog, dtbias, dskip, ssm_norm_g[None, :], batch, seq)

    return _mlp_call(x2, attn, ssm, attn_out_norm_g[None, :], w_out.astype(BF16), ln_mlp_g[None, :],
                     w_mlp_up.astype(BF16), w_mlp_down.astype(BF16))


def _rope_table(positions):
    inv_freq = 1.0 / (ROPE_THETA ** (jnp.arange(0, QK_ROPE_DIM, 2, dtype=F32) / QK_ROPE_DIM))
    ang = positions.astype(F32).reshape(-1, 1) * inv_freq
    ang = jnp.concatenate([ang, ang], axis=-1)
    return jnp.concatenate([jnp.ones((ang.shape[0], QK_NOPE_DIM), F32), jnp.cos(ang), jnp.sin(ang)], axis=1)


def kernel(x, positions, ln_mix_g, w_in, q_a_norm_g, w_uq, kv_a_norm_g, w_ukv, q_norm_g, k_norm_g,
           attn_out_norm_g, conv_w, conv_b, a_log_fwd, a_log_bwd, dt_bias_fwd, dt_bias_bwd, d_skip,
           ssm_norm_g, w_out, ln_mlp_g, w_mlp_up, w_mlp_down):
    batch, seq, d = x.shape
    assert d == D_MODEL and seq % TQ == 0 and (batch * seq) % TM_IN == 0 and (batch * seq) % TM_MLP == 0
    cs = _rope_table(positions)
    x2 = x.reshape(batch * seq, d)
    for l in range(ln_mix_g.shape[0]):
        x2 = _layer(x2, cs, batch, seq, ln_mix_g[l], w_in[l], q_a_norm_g[l], w_uq[l], kv_a_norm_g[l],
                    w_ukv[l], q_norm_g[l], k_norm_g[l], attn_out_norm_g[l], conv_w[l], conv_b[l],
                    a_log_fwd[l], a_log_bwd[l], dt_bias_fwd[l], dt_bias_bwd[l], d_skip[l], ssm_norm_g[l],
                    w_out[l], ln_mlp_g[l], w_mlp_up[l], w_mlp_down[l])
    return x2.reshape(batch, seq, d)
```

```python
import numpy as np
import jax
import jax.numpy as jnp
from jax import lax
from jax.experimental import pallas as pl
from jax.experimental.pallas import tpu as pltpu

F32 = jnp.float32
BF16 = jnp.bfloat16

D_MODEL = 1024
ATTN_HEADS = 8
QK_NOPE_DIM = 64
QK_ROPE_DIM = 32
QK_HEAD_DIM = QK_NOPE_DIM + QK_ROPE_DIM
V_HEAD_DIM = 64
Q_LORA_RANK = D_MODEL // 4
KV_LORA_RANK = D_MODEL // 8
ROPE_THETA = 10000.0
ATTN_WIDTH = ATTN_HEADS * V_HEAD_DIM
SSM_HEADS = 8
SSM_HEAD_DIM = 64
SSM_INNER = SSM_HEADS * SSM_HEAD_DIM
SSM_GROUPS = 2
SSM_STATE = 128
SSM_CONV = 5
SSM_CHUNK = 128
SSM_CONV_CH = SSM_INNER + 2 * SSM_GROUPS * SSM_STATE
D_MIX = ATTN_WIDTH + SSM_INNER
D_FF = 4 * D_MODEL
EPS = 1e-6

LANES = 128
SUBLANES = 8
HEAD_PAD = LANES

HEADS_PER_GROUP = SSM_HEADS // SSM_GROUPS
GROUP_INNER = SSM_INNER // SSM_GROUPS
GROUP_COLS = GROUP_INNER + 2 * SSM_STATE
DT_ROWS = 2 * HEADS_PER_GROUP
SEG_K_PER_ROW = 16
SEG_BLOCKS = DT_ROWS + HEADS_PER_GROUP
LOG2E = float(np.log2(np.e))
CONV_WIN = 2 * SSM_CHUNK
CONV_WIN_LEAD = SSM_CHUNK // 2
CONV_SHIFTED_TAPS = tuple(k for k in range(SSM_CONV) if k != SSM_CONV // 2)

COL_CKV = Q_LORA_RANK
COL_MISC = COL_CKV + KV_LORA_RANK
COL_Z = COL_MISC + LANES
COL_XBC = COL_Z + SSM_INNER
IN_COLS = COL_XBC + SSM_CONV_CH

TM_IN = 512
TQ = 512
ATTN_HEADS_PER_STEP = 8
TM_MLP = 512
FF_CHUNK = 1024
SSD_UNROLL = 4
PASS_A_UNROLL = 3
VMEM_LIMIT = 56 * 1024 * 1024


def _inv_rms(x):
    n = x.shape[-1]
    x2 = x * x
    acc = x2[:, 0:LANES]
    for i in range(1, n // LANES):
        acc = acc + x2[:, i * LANES:(i + 1) * LANES]
    return lax.rsqrt(jnp.sum(acc, axis=-1, keepdims=True) * (1.0 / n) + EPS)


def _dot(a, b):
    return jnp.dot(a, b, preferred_element_type=F32)


def _dot_nt(a, b):
    return lax.dot_general(a, b, (((1,), (1,)), ((), ())), preferred_element_type=F32)


def _inproj_kernel(x_ref, cs_ref, g_ref, win_ref, gqa_ref, wuq_ref, gkva_ref, wuk_ref, wuv_ref, vone_ref,
                   gq_ref, gkn_ref, gkr_ref,
                   q_ref, k_ref, v_ref, z_ref, xbc_ref, dt_ref):
    tm = x_ref.shape[0]
    x = x_ref[...]
    h = (x * _inv_rms(x) * g_ref[...]).astype(BF16)
    big = _dot(h, win_ref[...])
    z_ref[...] = big[:, COL_Z:COL_XBC].astype(BF16)
    xbc_ref[...] = big[:, COL_XBC:IN_COLS].astype(BF16)
    misc = big[:, COL_MISC:COL_Z]
    for c in range(tm // SSM_CHUNK):
        dt_ref[c] = misc[c * SSM_CHUNK:(c + 1) * SSM_CHUNK, :].T[0:2 * SSM_HEADS, :]

    cq = big[:, 0:COL_CKV]
    ckv = big[:, COL_CKV:COL_MISC]
    cqn = (cq * _inv_rms(cq) * gqa_ref[...]).astype(BF16)
    ckvn = (ckv * _inv_rms(ckv) * gkva_ref[...]).astype(BF16)
    q_pre = _dot(cqn, wuq_ref[...])
    k_pre = _dot(ckvn, wuk_ref[...])
    v_ref[...] = (_dot(ckvn, wuv_ref[...]) + vone_ref[...]).astype(BF16)

    cs = cs_ref[...]
    lane = lax.broadcasted_iota(jnp.int32, (1, HEAD_PAD), 1)
    in_head = (lane < QK_HEAD_DIM).astype(F32)
    is_rope = ((lane >= QK_NOPE_DIM) & (lane < QK_HEAD_DIM)).astype(F32)
    inv_d = 1.0 / QK_HEAD_DIM

    gcq = gq_ref[...] * cs
    for hh in range(ATTN_HEADS):
        sl = slice(hh * HEAD_PAD, (hh + 1) * HEAD_PAD)
        qh = q_pre[:, sl]
        ssq = jnp.sum(qh * qh * in_head, axis=-1, keepdims=True)
        q_ref[:, sl] = (qh * lax.rsqrt(ssq * inv_d + EPS) * gcq).astype(BF16)

    ab = misc * (gkr_ref[...] * cs)
    lane2 = lax.broadcasted_iota(jnp.int32, (tm, HEAD_PAD), 1)
    swapped = jnp.where(lane2 < QK_HEAD_DIM, pltpu.roll(ab, HEAD_PAD - QK_ROPE_DIM, 1),
                        pltpu.roll(ab, QK_ROPE_DIM, 1))
    s_both = jnp.where(lane2 >= QK_NOPE_DIM, ab + swapped, 0.0)
    ssq_pe = jnp.sum(misc * misc * is_rope, axis=-1, keepdims=True)
    gkn = gkn_ref[...]
    for hh in range(ATTN_HEADS):
        sl = slice(hh * HEAD_PAD, (hh + 1) * HEAD_PAD)
        kh = k_pre[:, sl]
        ssq = jnp.sum(kh * kh, axis=-1, keepdims=True) + ssq_pe
        k_ref[:, sl] = ((kh * gkn + s_both) * lax.rsqrt(ssq * inv_d + EPS)).astype(BF16)


def _inproj_call(x2, cs, g, win, gqa, wuq, gkva, wuk, wuv, vone, gq, gkn, gkr):
    t = x2.shape[0]
    cpt = TM_IN // SSM_CHUNK
    full = lambda a: pl.BlockSpec(a.shape, lambda i: (0,) * a.ndim)
    row = lambda w: pl.BlockSpec((TM_IN, w), lambda i: (i, 0))
    return pl.pallas_call(
        _inproj_kernel,
        grid=(t // TM_IN,),
        in_specs=[row(D_MODEL), row(HEAD_PAD), full(g), full(win), full(gqa), full(wuq), full(gkva),
                  full(wuk), full(wuv), full(vone), full(gq), full(gkn), full(gkr)],
        out_specs=[row(ATTN_HEADS * HEAD_PAD), row(ATTN_HEADS * HEAD_PAD), row(ATTN_HEADS * HEAD_PAD),
                   row(SSM_INNER), row(SSM_CONV_CH),
                   pl.BlockSpec((cpt, 2 * SSM_HEADS, SSM_CHUNK), lambda i: (i, 0, 0))],
        out_shape=[jax.ShapeDtypeStruct((t, ATTN_HEADS * HEAD_PAD), BF16),
                   jax.ShapeDtypeStruct((t, ATTN_HEADS * HEAD_PAD), BF16),
                   jax.ShapeDtypeStruct((t, ATTN_HEADS * HEAD_PAD), BF16),
                   jax.ShapeDtypeStruct((t, SSM_INNER), BF16),
                   jax.ShapeDtypeStruct((t, SSM_CONV_CH), BF16),
                   jax.ShapeDtypeStruct((t // SSM_CHUNK, 2 * SSM_HEADS, SSM_CHUNK), F32)],
        compiler_params=pltpu.CompilerParams(dimension_semantics=("arbitrary",),
                                             vmem_limit_bytes=VMEM_LIMIT),
        name="inproj",
    )(x2, cs, g, win, gqa, wuq, gkva, wuk, wuv, vone, gq, gkn, gkr)


def _attn_kernel(q_ref, k_ref, v_ref, o_ref):
    res = []
    for j in range(ATTN_HEADS_PER_STEP):
        sl = slice(j * HEAD_PAD, (j + 1) * HEAD_PAD)
        s = _dot_nt(q_ref[:, sl], k_ref[:, sl])
        m = jnp.max(s, axis=-1, keepdims=True)
        acc = _dot(jnp.exp2(s - m).astype(BF16), v_ref[:, sl])
        res.append(acc / pltpu.roll(acc, V_HEAD_DIM, 1))
    lane = lax.broadcasted_iota(jnp.int32, res[0].shape, 1)
    for jp in range(ATTN_HEADS_PER_STEP // 2):
        o_ref[:, jp * HEAD_PAD:(jp + 1) * HEAD_PAD] = jnp.where(lane < V_HEAD_DIM, res[2 * jp], res[2 * jp + 1])


def _attn_call(q, k, v, batch, seq):
    n_q = seq // TQ
    hps = ATTN_HEADS_PER_STEP
    return pl.pallas_call(
        _attn_kernel,
        grid=(batch, ATTN_HEADS // hps, n_q),
        in_specs=[pl.BlockSpec((TQ, hps * HEAD_PAD), lambda b, hp, i: (b * n_q + i, hp)),
                  pl.BlockSpec((seq, hps * HEAD_PAD), lambda b, hp, i: (b, hp)),
                  pl.BlockSpec((seq, hps * HEAD_PAD), lambda b, hp, i: (b, hp))],
        out_specs=pl.BlockSpec((TQ, hps * V_HEAD_DIM), lambda b, hp, i: (b * n_q + i, hp)),
        out_shape=jax.ShapeDtypeStruct((batch * seq, ATTN_WIDTH), F32),
        compiler_params=pltpu.CompilerParams(
            dimension_semantics=("arbitrary", "arbitrary", "arbitrary"),
            vmem_limit_bytes=VMEM_LIMIT),
        name="attention",
    )(q, k, v)


def _split3(x):
    hi = x.astype(BF16)
    r1 = x - hi.astype(F32)
    mid = r1.astype(BF16)
    lo = (r1 - mid.astype(F32)).astype(BF16)
    return hi, mid, lo


def _ssd_kernel(x_ref, b_ref, c_ref, z_ref, dt_ref, sh_ref, place_ref, lhs_ones_ref, rhs_const_ref,
                cwx_ref, cwb_ref, cwc_ref, cbx_ref, cbb_ref, cbc_ref, alog_ref, dtbias_ref, dskip_ref, gn_ref,
                o_ref, xc_scr, xm_scr, nsf_scr, nsb_scr, dtv_scr, w_scr, dec_scr, lhs_scr, rhs_scr, rpart_scr):
    nc = dt_ref.shape[0]
    seq = x_ref.shape[0]
    L = SSM_CHUNK
    hpg = HEADS_PER_GROUP
    npair = hpg // 2
    P = SSM_HEAD_DIM
    x_cols = slice(0, GROUP_INNER)
    b_cols = slice(GROUP_INNER, GROUP_INNER + SSM_STATE)
    c_cols = slice(GROUP_INNER + SSM_STATE, GROUP_COLS)
    pair_cols = [slice(j * L, (j + 1) * L) for j in range(npair)]

    a_neg = -jnp.exp(alog_ref[...]) * LOG2E
    dtv = jax.nn.softplus(dt_ref[...] + dtbias_ref[...][None])
    dtv_scr[...] = dtv
    da2 = (dtv * a_neg[None]).reshape(nc * DT_ROWS, L)
    ri = lax.broadcasted_iota(jnp.int32, (L, L), 0)
    ci = lax.broadcasted_iota(jnp.int32, (L, L), 1)
    upper = (ri <= ci).astype(BF16)
    lower = (ri >= ci).astype(BF16)
    dparts = _split3(da2)
    cs_f = sum(_dot(p, upper) for p in dparts)
    cs_b = sum(_dot(p, lower) for p in dparts)
    rowsel = (ri & hpg) == 0
    cs2 = jnp.where(rowsel, cs_f, cs_b)
    col_f = sum(_dot_nt(lower, p) for p in dparts)
    col_b = sum(_dot_nt(upper, p) for p in dparts)
    colcs = jnp.where((ci & hpg) == 0, col_f, col_b)

    lhs_all = _dot(jnp.concatenate(_split3(colcs), axis=1), place_ref[...]) + lhs_ones_ref[...]
    for c in range(nc):
        lhs_scr[c] = lhs_all[:, c * L:(c + 1) * L].astype(BF16)
    for i, part in enumerate(_split3(-cs2)):
        rpart_scr[i] = part.astype(F32).reshape(nc, DT_ROWS, L)
    for u in range(SSD_UNROLL):
        rhs_scr[u] = rhs_const_ref[...]

    def lane_bcast(col):
        return jnp.broadcast_to(col, (col.shape[0], L))

    tot2 = jnp.where(rowsel, lane_bcast(cs2[:, L - 1:L]), lane_bcast(cs2[:, 0:1]))
    dec_scr[...] = jnp.exp2(tot2).reshape(nc, DT_ROWS, L)
    w_scr[...] = (dtv.reshape(nc * DT_ROWS, L) * jnp.exp2(tot2 - cs2)).reshape(nc, DT_ROWS, L)

    lane_t = lax.broadcasted_iota(jnp.int32, (L, L), 1)
    low_half = lane_t < P

    cw_halves = (cwx_ref[...], jnp.concatenate([cwb_ref[...], cwc_ref[...]], axis=1))
    cb_halves = (cbx_ref[...], jnp.concatenate([cbb_ref[...], cbc_ref[...]], axis=1))

    def rows(hf, start, size):
        if hf == 0:
            return x_ref[pl.ds(start, size), :]
        return jnp.concatenate([b_ref[pl.ds(start, size), :], c_ref[pl.ds(start, size), :]], axis=1)

    def conv_stage(c):
        base = pl.multiple_of(c * L, L)
        ws = pl.multiple_of(jnp.clip(c * L - CONV_WIN_LEAD, 0, seq - CONV_WIN), CONV_WIN_LEAD)
        variant = jnp.where(c == 0, 0, jnp.where(c == nc - 1, 2, 1))
        sh = sh_ref[variant]
        halves = []
        for hf in range(2):
            cw = cw_halves[hf]
            shifted = _dot(sh, rows(hf, ws, CONV_WIN))
            acc = cb_halves[hf] + cw[SSM_CONV // 2:SSM_CONV // 2 + 1, :] * rows(hf, base, L).astype(F32)
            for t, kk in enumerate(CONV_SHIFTED_TAPS):
                acc = acc + cw[kk:kk + 1, :] * shifted[t * L:(t + 1) * L, :]
            halves.append(acc * jax.nn.sigmoid(acc))
        xc = jnp.concatenate(halves, axis=1)
        xc_scr[c] = xc
        for j in range(npair):
            xp = xc[:, pair_cols[j]]
            xm_scr[c, j] = jnp.concatenate([jnp.where(low_half, xp, 0.0), jnp.where(low_half, 0.0, xp)],
                                           axis=0).astype(BF16)

    def state_loads(c):
        return xc_scr[c, :, b_cols], [xm_scr[c, j] for j in range(npair)]

    def state_stage(c, loaded):
        b_tok, xm = loaded
        bt = b_tok.T
        for d, ns_scr in ((0, nsf_scr), (1, nsb_scr)):
            w = w_scr[c, d * hpg:(d + 1) * hpg, :]
            ns_scr[c] = jnp.concatenate(
                [_dot(jnp.concatenate([(bt * w[2 * j:2 * j + 1, :]).astype(BF16),
                                       (bt * w[2 * j + 1:2 * j + 2, :]).astype(BF16)], axis=1), xm[j])
                 for j in range(npair)], axis=1)

    conv_stage(0)

    def pass_a(c, carry):
        loaded = state_loads(c)
        conv_stage(c + 1)
        state_stage(c, loaded)
        return carry
    lax.fori_loop(0, nc - 1, pass_a, 0, unroll=PASS_A_UNROLL)
    state_stage(nc - 1, state_loads(nc - 1))

    lane_r = lax.broadcasted_iota(jnp.int32, (1, L), 1)

    def decay_row(c, d):
        dec = dec_scr[c, d * hpg:(d + 1) * hpg, :]
        return jnp.concatenate([jnp.where(lane_r < P, dec[2 * j:2 * j + 1, :], dec[2 * j + 1:2 * j + 2, :])
                                for j in range(npair)], axis=1)

    def pass_b(d, ns_scr):
        def body(i, st):
            c = i if d == 0 else nc - 1 - i
            new = ns_scr[c]
            ns_scr[c] = st
            return st * decay_row(c, d) + new
        lax.fori_loop(0, nc, body, jnp.zeros((SSM_STATE, GROUP_INNER), F32))
    pass_b(0, nsf_scr)
    pass_b(1, nsb_scr)

    mask_f = ci <= ri
    mask_b = ci >= ri
    neg_inf = jnp.float32(-jnp.inf)
    sub16 = lax.broadcasted_iota(jnp.int32, (SEG_K_PER_ROW, L), 0)

    def pass_c(c, carry):
        base = pl.multiple_of(c * L, L)
        xc = xc_scr[c]
        bm = xc[:, b_cols].astype(BF16)
        cm = xc[:, c_cols].astype(BF16)
        cb = _dot_nt(cm, bm)
        off_f = _dot(cm, nsf_scr[c].astype(BF16))
        off_b = _dot(cm, nsb_scr[c].astype(BF16))
        slot = c % SSD_UNROLL
        for hd in range(DT_ROWS):
            blk = jnp.where(sub16 < 3, 1.0, 0.0)
            for i in range(3):
                blk = jnp.where(sub16 == 3 + i, rpart_scr[i, c, hd:hd + 1, :], blk)
            rhs_scr[slot, hd * SEG_K_PER_ROW:(hd + 1) * SEG_K_PER_ROW, hd * L:(hd + 1) * L] = blk.astype(BF16)
        seg = _dot(lhs_scr[c], rhs_scr[slot])
        dt_f_r = dtv_scr[c, 0:hpg, :]
        dt_b_r = dtv_scr[c, hpg:DT_ROWS, :]
        ys = []
        for j in range(npair):
            ws_ = []
            for h in (2 * j, 2 * j + 1):
                ef = jnp.exp2(jnp.where(mask_f, seg[:, h * L:(h + 1) * L], neg_inf))
                eb = jnp.exp2(jnp.where(mask_b, seg[:, (hpg + h) * L:(hpg + h + 1) * L], neg_inf))
                ws_.append((cb * (ef * dt_f_r[h:h + 1, :] + eb * dt_b_r[h:h + 1, :])).astype(BF16))
            scale_f = jnp.exp2(seg[:, (DT_ROWS + j) * L:(DT_ROWS + j + 1) * L])
            scale_b = jnp.exp2(seg[:, (DT_ROWS + npair + j) * L:(DT_ROWS + npair + j + 1) * L])
            ys.append(_dot(jnp.concatenate(ws_, axis=1), xm_scr[c, j])
                      + scale_f * off_f[:, pair_cols[j]] + scale_b * off_b[:, pair_cols[j]])
        y = jnp.concatenate(ys, axis=1) + dskip_ref[...] * xc[:, x_cols]
        zc = z_ref[pl.ds(base, L), :].astype(F32)
        y = y * (zc * jax.nn.sigmoid(zc))
        o_ref[pl.ds(base, L), :] = y * _inv_rms(y) * gn_ref[...]
        return carry
    lax.fori_loop(0, nc, pass_c, 0, unroll=SSD_UNROLL)


def _conv_shift_matrices():
    out = np.zeros((3, len(CONV_SHIFTED_TAPS) * SSM_CHUNK, CONV_WIN), np.float32)
    for v, lead in enumerate((0, CONV_WIN_LEAD, CONV_WIN - SSM_CHUNK)):
        for i, kk in enumerate(CONV_SHIFTED_TAPS):
            for t in range(SSM_CHUNK):
                j = t + kk - SSM_CONV // 2 + lead
                if 0 <= j < CONV_WIN:
                    out[v, i * SSM_CHUNK + t, j] = 1.0
    return out


def _seg_matmul_constants(nc):
    L, K, half = SSM_CHUNK, SEG_K_PER_ROW, SSM_HEAD_DIM
    place = np.zeros((3 * L, nc * L), np.float32)
    lhs_ones = np.zeros((1, nc * L), np.float32)
    for c in range(nc):
        for hd in range(DT_ROWS):
            for i in range(3):
                place[i * L + c * DT_ROWS + hd, c * L + hd * K + i] = 1.0
                lhs_ones[0, c * L + hd * K + 3 + i] = 1.0
    rhs = np.zeros((L, SEG_BLOCKS * L), np.float32)
    for hd in range(DT_ROWS):
        d, h = divmod(hd, HEADS_PER_GROUP)
        pair_block = DT_ROWS + d * (HEADS_PER_GROUP // 2) + h // 2
        lanes = slice(0, half) if h % 2 == 0 else slice(half, L)
        rhs[hd * K:hd * K + 3, hd * L:(hd + 1) * L] = 1.0
        rhs[hd * K:hd * K + 3, pair_block * L:(pair_block + 1) * L][:, lanes] = 1.0
    return place, lhs_ones, rhs


def _ssd_call(xbc, z, dtc, cw, cbias, alog, dtbias, dskip, gn, batch, seq):
    nc = seq // SSM_CHUNK
    L = SSM_CHUNK
    assert nc >= 3 and seq >= CONV_WIN and nc * DT_ROWS == L
    sh = jnp.asarray(_conv_shift_matrices(), BF16)
    place, lhs_ones, rhs_const = _seg_matmul_constants(nc)
    place = jnp.asarray(place, BF16)
    lhs_ones = jnp.asarray(lhs_ones, F32)
    rhs_const = jnp.asarray(rhs_const, BF16)
    const = lambda a: pl.BlockSpec(a.shape, lambda b, g: (0,) * a.ndim)
    b_blk0 = SSM_INNER // SSM_STATE
    c_blk0 = b_blk0 + SSM_GROUPS

    def xbc_views(rows_, batched):
        lead = (lambda b: b) if batched else (lambda b: 0)
        return [pl.BlockSpec((rows_, GROUP_INNER), lambda b, g: (lead(b), g)),
                pl.BlockSpec((rows_, SSM_STATE), lambda b, g: (lead(b), b_blk0 + g)),
                pl.BlockSpec((rows_, SSM_STATE), lambda b, g: (lead(b), c_blk0 + g))]
    return pl.pallas_call(
        _ssd_kernel,
        grid=(batch, SSM_GROUPS),
        in_specs=xbc_views(seq, True) + [
                  pl.BlockSpec((seq, GROUP_INNER), lambda b, g: (b, g)),
                  pl.BlockSpec((nc, DT_ROWS, L), lambda b, g: (b, g, 0)),
                  const(sh), const(place), const(lhs_ones), const(rhs_const)]
                 + xbc_views(SUBLANES, False) + xbc_views(1, False) + [
                  pl.BlockSpec((DT_ROWS, L), lambda b, g: (g, 0)),
                  pl.BlockSpec((DT_ROWS, L), lambda b, g: (g, 0)),
                  pl.BlockSpec((1, GROUP_INNER), lambda b, g: (0, g)),
                  pl.BlockSpec((1, GROUP_INNER), lambda b, g: (0, g))],
        out_specs=pl.BlockSpec((seq, GROUP_INNER), lambda b, g: (b, g)),
        out_shape=jax.ShapeDtypeStruct((batch * seq, SSM_INNER), F32),
        scratch_shapes=[pltpu.VMEM((nc, L, GROUP_COLS), F32),
                        pltpu.VMEM((nc, HEADS_PER_GROUP // 2, 2 * L, L), BF16),
                        pltpu.VMEM((nc, SSM_STATE, GROUP_INNER), F32),
                        pltpu.VMEM((nc, SSM_STATE, GROUP_INNER), F32),
                        pltpu.VMEM((nc, DT_ROWS, L), F32),
                        pltpu.VMEM((nc, DT_ROWS, L), F32),
                        pltpu.VMEM((nc, DT_ROWS, L), F32),
                        pltpu.VMEM((nc, L, L), BF16),
                        pltpu.VMEM((SSD_UNROLL, L, SEG_BLOCKS * L), BF16),
                        pltpu.VMEM((3, nc, DT_ROWS, L), F32)],
        compiler_params=pltpu.CompilerParams(dimension_semantics=("arbitrary", "arbitrary"),
                                             vmem_limit_bytes=VMEM_LIMIT),
        name="ssd",
    )(xbc, xbc, xbc, z, dtc, sh, place, lhs_ones, rhs_const, cw, cw, cw, cbias, cbias, cbias,
      alog, dtbias, dskip, gn)


def _mlp_kernel(x_ref, attn_ref, ssm_ref, gat_ref, wo_ref, gm_ref, wup_ref, wdn_ref, o_ref):
    a = attn_ref[...]
    an = (a * _inv_rms(a) * gat_ref[...]).astype(BF16)
    mix = _dot(an, wo_ref[0:ATTN_WIDTH, :]) + _dot(ssm_ref[...].astype(BF16), wo_ref[ATTN_WIDTH:D_MIX, :])
    x1 = x_ref[...] + mix
    hm = (x1 * _inv_rms(x1) * gm_ref[...]).astype(BF16)
    acc = jnp.zeros_like(x1)
    for c in range(D_FF // FF_CHUNK):
        cols = slice(c * FF_CHUNK, (c + 1) * FF_CHUNK)
        u = _dot(hm, wup_ref[:, cols])
        acc = acc + _dot(jnp.square(jnp.maximum(u, 0.0)).astype(BF16), wdn_ref[cols, :])
    o_ref[...] = x1 + acc


def _mlp_call(x2, attn, ssm, gat, wo, gm, wup, wdn):
    t = x2.shape[0]
    row = lambda w: pl.BlockSpec((TM_MLP, w), lambda i: (i, 0))
    full = lambda a: pl.BlockSpec(a.shape, lambda i: (0, 0), pipeline_mode=pl.Buffered(1))
    return pl.pallas_call(
        _mlp_kernel,
        grid=(t // TM_MLP,),
        in_specs=[row(D_MODEL), row(ATTN_WIDTH), row(SSM_INNER), full(gat), full(wo), full(gm),
                  full(wup), full(wdn)],
        out_specs=row(D_MODEL),
        out_shape=jax.ShapeDtypeStruct((t, D_MODEL), F32),
        compiler_params=pltpu.CompilerParams(dimension_semantics=("arbitrary",),
                                             vmem_limit_bytes=VMEM_LIMIT),
        name="outproj_mlp",
    )(x2, attn, ssm, gat, wo, gm, wup, wdn)


def _group_major(fwd, bwd):
    lead = fwd.shape[:-1]
    both = jnp.stack([fwd.reshape(lead + (SSM_GROUPS, HEADS_PER_GROUP)),
                      bwd.reshape(lead + (SSM_GROUPS, HEADS_PER_GROUP))], axis=-2)
    return both.reshape(lead + (2 * SSM_HEADS,))


def _rotate_half(w):
    half = QK_ROPE_DIM // 2
    return jnp.concatenate([-w[..., half:], w[..., :half]], axis=-1)


def _swap_halves(g):
    half = QK_ROPE_DIM // 2
    return jnp.concatenate([g[..., half:], g[..., :half]], axis=-1)


def _lane_bcast(v):
    return jnp.broadcast_to(v[..., None], v.shape + (LANES,)).astype(F32)


def _layer(x2, cs, batch, seq, ln_mix_g, w_in, q_a_norm_g, w_uq, kv_a_norm_g, w_ukv, q_norm_g,
           k_norm_g, attn_out_norm_g, conv_w, conv_b, a_log_fwd, a_log_bwd, dt_bias_fwd, dt_bias_bwd,
           d_skip, ssm_norm_g, w_out, ln_mlp_g, w_mlp_up, w_mlp_down):
    o_kpe = Q_LORA_RANK + KV_LORA_RANK
    o_z = o_kpe + QK_ROPE_DIM
    o_dt = o_z + SSM_INNER + SSM_CONV_CH
    d = w_in.shape[0]

    w_kpe = w_in[:, o_kpe:o_z]
    w_dt = _group_major(w_in[:, o_dt:o_dt + SSM_HEADS], w_in[:, o_dt + SSM_HEADS:])
    win = jnp.concatenate([w_in[:, 0:o_kpe], w_dt, jnp.zeros((d, QK_NOPE_DIM - 2 * SSM_HEADS), w_in.dtype),
                           w_kpe, _rotate_half(w_kpe), w_in[:, o_z:o_dt]], axis=1).astype(BF16)
    wq = w_uq.reshape(Q_LORA_RANK, ATTN_HEADS, QK_HEAD_DIM)
    wuq = jnp.concatenate([wq, _rotate_half(wq[..., QK_NOPE_DIM:])], axis=-1)
    wuq = wuq.reshape(Q_LORA_RANK, ATTN_HEADS * HEAD_PAD).astype(BF16)
    wukv = w_ukv.reshape(KV_LORA_RANK, ATTN_HEADS, QK_NOPE_DIM + V_HEAD_DIM)
    wuk = jnp.pad(wukv[:, :, :QK_NOPE_DIM],
                  ((0, 0), (0, 0), (0, HEAD_PAD - QK_NOPE_DIM))).reshape(KV_LORA_RANK, -1).astype(BF16)
    wv = wukv[:, :, QK_NOPE_DIM:].reshape(KV_LORA_RANK, ATTN_HEADS // 2, 2, V_HEAD_DIM)
    zv = jnp.zeros_like(wv[:, :, 0])
    wuv = jnp.stack([jnp.concatenate([wv[:, :, 0], zv], axis=-1),
                     jnp.concatenate([zv, wv[:, :, 1]], axis=-1)], axis=2).reshape(KV_LORA_RANK, -1).astype(BF16)
    ones_v = np.ones((ATTN_HEADS // 2, V_HEAD_DIM), np.float32)
    vone = jnp.asarray(np.stack([np.concatenate([0 * ones_v, ones_v], axis=-1),
                                 np.concatenate([ones_v, 0 * ones_v], axis=-1)], axis=1).reshape(1, -1))
    scale = QK_HEAD_DIM ** -0.5 * np.log2(np.e)
    gq = (jnp.concatenate([q_norm_g, _swap_halves(q_norm_g[QK_NOPE_DIM:])]) * scale)[None, :]
    zeros_nope = jnp.zeros((QK_NOPE_DIM,), k_norm_g.dtype)
    gkn = jnp.concatenate([k_norm_g[:QK_NOPE_DIM], zeros_nope])[None, :]
    gkr = jnp.concatenate([zeros_nope, k_norm_g[QK_NOPE_DIM:], _swap_halves(k_norm_g[QK_NOPE_DIM:])])[None, :]

    q, k, v, z, xbc, dtc = _inproj_call(
        x2, cs, ln_mix_g[None, :], win, q_a_norm_g[None, :], wuq, kv_a_norm_g[None, :], wuk, wuv, vone,
        gq, gkn, gkr)

    attn = _attn_call(q, k, v, batch, seq)

    cw = jnp.pad(conv_w[:, 0, :], ((0, SUBLANES - SSM_CONV), (0, 0)))
    cbias = conv_b[None, :]
    alog = _lane_bcast(_group_major(a_log_fwd, a_log_bwd))
    dtbias = _lane_bcast(_group_major(dt_bias_fwd, dt_bias_bwd))
    dskip = jnp.repeat(d_skip, SSM_HEAD_DIM)[None, :]
    ssm = _ssd_call(xbc, z, dtc, cw, cbias, alog, dtbias, dskip, ssm_norm_g[None, :], batch, seq)

    return _mlp_call(x2, attn, ssm, attn_out_norm_g[None, :], w_out.astype(BF16), ln_mlp_g[None, :],
                     w_mlp_up.astype(BF16), w_mlp_down.astype(BF16))


def _rope_table(positions):
    inv_freq = 1.0 / (ROPE_THETA ** (jnp.arange(0, QK_ROPE_DIM, 2, dtype=F32) / QK_ROPE_DIM))
    ang = positions.astype(F32).reshape(-1, 1) * inv_freq
    ang = jnp.concatenate([ang, ang], axis=-1)
    return jnp.concatenate([jnp.ones((ang.shape[0], QK_NOPE_DIM), F32), jnp.cos(ang), jnp.sin(ang)], axis=1)


def kernel(x, positions, ln_mix_g, w_in, q_a_norm_g, w_uq, kv_a_norm_g, w_ukv, q_norm_g, k_norm_g,
           attn_out_norm_g, conv_w, conv_b, a_log_fwd, a_log_bwd, dt_bias_fwd, dt_bias_bwd, d_skip,
           ssm_norm_g, w_out, ln_mlp_g, w_mlp_up, w_mlp_down):
    batch, seq, d = x.shape
    assert d == D_MODEL and seq % TQ == 0 and (batch * seq) % TM_IN == 0 and (batch * seq) % TM_MLP == 0
    cs = _rope_table(positions)
    x2 = x.reshape(batch * seq, d)
    for l in range(ln_mix_g.shape[0]):
        x2 = _layer(x2, cs, batch, seq, ln_mix_g[l], w_in[l], q_a_norm_g[l], w_uq[l], kv_a_norm_g[l],
                    w_ukv[l], q_norm_g[l], k_norm_g[l], attn_out_norm_g[l], conv_w[l], conv_b[l],
                    a_log_fwd[l], a_log_bwd[l], dt_bias_fwd[l], dt_bias_bwd[l], d_skip[l], ssm_norm_g[l],
                    w_out[l], ln_mlp_g[l], w_mlp_up[l], w_mlp_down[l])
    return x2.reshape(batch, seq, d)
```

```python
import numpy as np
import jax
import jax.numpy as jnp
from jax import lax
from jax.experimental import pallas as pl
from jax.experimental.pallas import tpu as pltpu

F32 = jnp.float32
BF16 = jnp.bfloat16

D_MODEL = 1024
ATTN_HEADS = 8
QK_NOPE_DIM = 64
QK_ROPE_DIM = 32
QK_HEAD_DIM = QK_NOPE_DIM + QK_ROPE_DIM
V_HEAD_DIM = 64
Q_LORA_RANK = D_MODEL // 4
KV_LORA_RANK = D_MODEL // 8
ROPE_THETA = 10000.0
ATTN_WIDTH = ATTN_HEADS * V_HEAD_DIM
SSM_HEADS = 8
SSM_HEAD_DIM = 64
SSM_INNER = SSM_HEADS * SSM_HEAD_DIM
SSM_GROUPS = 2
SSM_STATE = 128
SSM_CONV = 5
SSM_CHUNK = 128
SSM_CONV_CH = SSM_INNER + 2 * SSM_GROUPS * SSM_STATE
D_MIX = ATTN_WIDTH + SSM_INNER
D_FF = 4 * D_MODEL
EPS = 1e-6

LANES = 128
SUBLANES = 8
HEAD_PAD = LANES

HEADS_PER_GROUP = SSM_HEADS // SSM_GROUPS
GROUP_INNER = SSM_INNER // SSM_GROUPS
GROUP_COLS = GROUP_INNER + 2 * SSM_STATE
DT_ROWS = 2 * HEADS_PER_GROUP
SEG_K_PER_ROW = 16
SEG_BLOCKS = DT_ROWS + HEADS_PER_GROUP
LOG2E = float(np.log2(np.e))
CONV_WIN = 2 * SSM_CHUNK
CONV_WIN_LEAD = SSM_CHUNK // 2
CONV_SHIFTED_TAPS = tuple(k for k in range(SSM_CONV) if k != SSM_CONV // 2)

COL_CKV = Q_LORA_RANK
COL_MISC = COL_CKV + KV_LORA_RANK
COL_Z = COL_MISC + LANES
COL_XBC = COL_Z + SSM_INNER
IN_COLS = COL_XBC + SSM_CONV_CH

TM_IN = 512
TQ = 512
ATTN_HEADS_PER_STEP = 8
TM_MLP = 512
FF_CHUNK = 1024
SSD_UNROLL = 4
PASS_A_UNROLL = 3
VMEM_LIMIT = 56 * 1024 * 1024


def _inv_rms(x):
    n = x.shape[-1]
    x2 = x * x
    acc = x2[:, 0:LANES]
    for i in range(1, n // LANES):
        acc = acc + x2[:, i * LANES:(i + 1) * LANES]
    return lax.rsqrt(jnp.sum(acc, axis=-1, keepdims=True) * (1.0 / n) + EPS)


def _dot(a, b):
    return jnp.dot(a, b, preferred_element_type=F32)


def _dot_nt(a, b):
    return lax.dot_general(a, b, (((1,), (1,)), ((), ())), preferred_element_type=F32)


def _inproj_kernel(x_ref, cs_ref, g_ref, win_ref, gqa_ref, wuq_ref, gkva_ref, wuk_ref, wuv_ref, vone_ref,
                   gq_ref, gkn_ref, gkr_ref,
                   q_ref, k_ref, v_ref, z_ref, xbc_ref, dt_ref):
    tm = x_ref.shape[0]
    x = x_ref[...]
    h = (x * _inv_rms(x) * g_ref[...]).astype(BF16)
    big = _dot(h, win_ref[...])
    z_ref[...] = big[:, COL_Z:COL_XBC].astype(BF16)
    xbc_ref[...] = big[:, COL_XBC:IN_COLS].astype(BF16)
    misc = big[:, COL_MISC:COL_Z]
    for c in range(tm // SSM_CHUNK):
        dt_ref[c] = misc[c * SSM_CHUNK:(c + 1) * SSM_CHUNK, :].T[0:2 * SSM_HEADS, :]

    cq = big[:, 0:COL_CKV]
    ckv = big[:, COL_CKV:COL_MISC]
    cqn = (cq * _inv_rms(cq) * gqa_ref[...]).astype(BF16)
    ckvn = (ckv * _inv_rms(ckv) * gkva_ref[...]).astype(BF16)
    q_pre = _dot(cqn, wuq_ref[...])
    k_pre = _dot(ckvn, wuk_ref[...])
    v_ref[...] = (_dot(ckvn, wuv_ref[...]) + vone_ref[...]).astype(BF16)

    half = QK_ROPE_DIM // 2
    zpad = jnp.zeros((HEAD_PAD - QK_ROPE_DIM, SSM_CHUNK), F32)
    tt = jnp.concatenate([jnp.concatenate([cs_ref[:, c * SSM_CHUNK:(c + 1) * SSM_CHUNK], zpad], axis=0).T
                          for c in range(tm // SSM_CHUNK)], axis=0)
    lane_t = lax.broadcasted_iota(jnp.int32, (tm, HEAD_PAD), 1)
    cs = jnp.where(lane_t < QK_NOPE_DIM, 1.0,
                   jnp.where(lane_t < QK_NOPE_DIM + half, pltpu.roll(tt, QK_NOPE_DIM, 1),
                             jnp.where(lane_t < QK_HEAD_DIM + half, pltpu.roll(tt, QK_NOPE_DIM + half, 1),
                                       pltpu.roll(tt, QK_HEAD_DIM, 1))))
    lane = lax.broadcasted_iota(jnp.int32, (1, HEAD_PAD), 1)
    in_head = (lane < QK_HEAD_DIM).astype(F32)
    is_rope = ((lane >= QK_NOPE_DIM) & (lane < QK_HEAD_DIM)).astype(F32)
    inv_d = 1.0 / QK_HEAD_DIM

    gcq = gq_ref[...] * cs
    for hh in range(ATTN_HEADS):
        sl = slice(hh * HEAD_PAD, (hh + 1) * HEAD_PAD)
        qh = q_pre[:, sl]
        ssq = jnp.sum(qh * qh * in_head, axis=-1, keepdims=True)
        q_ref[:, sl] = (qh * lax.rsqrt(ssq * inv_d + EPS) * gcq).astype(BF16)

    ab = misc * (gkr_ref[...] * cs)
    lane2 = lax.broadcasted_iota(jnp.int32, (tm, HEAD_PAD), 1)
    swapped = jnp.where(lane2 < QK_HEAD_DIM, pltpu.roll(ab, HEAD_PAD - QK_ROPE_DIM, 1),
                        pltpu.roll(ab, QK_ROPE_DIM, 1))
    s_both = jnp.where(lane2 >= QK_NOPE_DIM, ab + swapped, 0.0)
    ssq_pe = jnp.sum(misc * misc * is_rope, axis=-1, keepdims=True)
    gkn = gkn_ref[...]
    for hh in range(ATTN_HEADS):
        sl = slice(hh * HEAD_PAD, (hh + 1) * HEAD_PAD)
        kh = k_pre[:, sl]
        ssq = jnp.sum(kh * kh, axis=-1, keepdims=True) + ssq_pe
        k_ref[:, sl] = ((kh * gkn + s_both) * lax.rsqrt(ssq * inv_d + EPS)).astype(BF16)


def _win_prep_kernel(w_ref, o_ref):
    w = w_ref[0]
    o_kpe = Q_LORA_RANK + KV_LORA_RANK
    o_z = o_kpe + QK_ROPE_DIM
    o_dt = o_z + SSM_INNER + SSM_CONV_CH
    half = QK_ROPE_DIM // 2
    hpg = HEADS_PER_GROUP
    dt_f = w[:, o_dt:o_dt + SSM_HEADS]
    dt_b = w[:, o_dt + SSM_HEADS:o_dt + 2 * SSM_HEADS]
    kpe = w[:, o_kpe:o_z]
    pieces = [w[:, 0:o_kpe]]
    for g in range(SSM_GROUPS):
        pieces += [dt_f[:, g * hpg:(g + 1) * hpg], dt_b[:, g * hpg:(g + 1) * hpg]]
    pieces += [jnp.zeros((w.shape[0], QK_NOPE_DIM - 2 * SSM_HEADS), F32), kpe, -kpe[:, half:], kpe[:, :half],
               w[:, o_z:o_dt]]
    o_ref[...] = jnp.concatenate(pieces, axis=1).astype(BF16)


def _win_prep_call(w_in3, layer):
    d, n = w_in3.shape[1], w_in3.shape[2]
    rows = d // 4
    return pl.pallas_call(
        _win_prep_kernel,
        grid=(d // rows,),
        in_specs=[pl.BlockSpec((1, rows, n), lambda i: (layer, i, 0))],
        out_specs=pl.BlockSpec((rows, IN_COLS), lambda i: (i, 0)),
        out_shape=jax.ShapeDtypeStruct((d, IN_COLS), BF16),
        compiler_params=pltpu.CompilerParams(dimension_semantics=("arbitrary",), vmem_limit_bytes=VMEM_LIMIT),
        name="win_prep",
    )(w_in3)


def _inproj_call(x2, cs, g, win, gqa, wuq, gkva, wuk, wuv, vone, gq, gkn, gkr):
    t = x2.shape[0]
    cpt = TM_IN // SSM_CHUNK
    full = lambda a: pl.BlockSpec(a.shape, lambda i: (0,) * a.ndim)
    row = lambda w: pl.BlockSpec((TM_IN, w), lambda i: (i, 0))
    return pl.pallas_call(
        _inproj_kernel,
        grid=(t // TM_IN,),
        in_specs=[row(D_MODEL), pl.BlockSpec((QK_ROPE_DIM, TM_IN), lambda i: (0, i)), full(g), full(win), full(gqa), full(wuq), full(gkva),
                  full(wuk), full(wuv), full(vone), full(gq), full(gkn), full(gkr)],
        out_specs=[row(ATTN_HEADS * HEAD_PAD), row(ATTN_HEADS * HEAD_PAD), row(ATTN_HEADS * HEAD_PAD),
                   row(SSM_INNER), row(SSM_CONV_CH),
                   pl.BlockSpec((cpt, 2 * SSM_HEADS, SSM_CHUNK), lambda i: (i, 0, 0))],
        out_shape=[jax.ShapeDtypeStruct((t, ATTN_HEADS * HEAD_PAD), BF16),
                   jax.ShapeDtypeStruct((t, ATTN_HEADS * HEAD_PAD), BF16),
                   jax.ShapeDtypeStruct((t, ATTN_HEADS * HEAD_PAD), BF16),
                   jax.ShapeDtypeStruct((t, SSM_INNER), BF16),
                   jax.ShapeDtypeStruct((t, SSM_CONV_CH), BF16),
                   jax.ShapeDtypeStruct((t // SSM_CHUNK, 2 * SSM_HEADS, SSM_CHUNK), F32)],
        compiler_params=pltpu.CompilerParams(dimension_semantics=("arbitrary",),
                                             vmem_limit_bytes=VMEM_LIMIT),
        name="inproj",
    )(x2, cs, g, win, gqa, wuq, gkva, wuk, wuv, vone, gq, gkn, gkr)


def _attn_kernel(q_ref, k_ref, v_ref, o_ref):
    res = []
    for j in range(ATTN_HEADS_PER_STEP):
        sl = slice(j * HEAD_PAD, (j + 1) * HEAD_PAD)
        s = _dot_nt(q_ref[:, sl], k_ref[:, sl])
        m = jnp.max(s, axis=-1, keepdims=True)
        acc = _dot(jnp.exp2(s - m).astype(BF16), v_ref[:, sl])
        res.append(acc / pltpu.roll(acc, V_HEAD_DIM, 1))
    lane = lax.broadcasted_iota(jnp.int32, res[0].shape, 1)
    for jp in range(ATTN_HEADS_PER_STEP // 2):
        o_ref[:, jp * HEAD_PAD:(jp + 1) * HEAD_PAD] = jnp.where(lane < V_HEAD_DIM, res[2 * jp], res[2 * jp + 1])


def _attn_call(q, k, v, batch, seq):
    n_q = seq // TQ
    hps = ATTN_HEADS_PER_STEP
    return pl.pallas_call(
        _attn_kernel,
        grid=(batch, ATTN_HEADS // hps, n_q),
        in_specs=[pl.BlockSpec((TQ, hps * HEAD_PAD), lambda b, hp, i: (b * n_q + i, hp)),
                  pl.BlockSpec((seq, hps * HEAD_PAD), lambda b, hp, i: (b, hp)),
                  pl.BlockSpec((seq, hps * HEAD_PAD), lambda b, hp, i: (b, hp))],
        out_specs=pl.BlockSpec((TQ, hps * V_HEAD_DIM), lambda b, hp, i: (b * n_q + i, hp)),
        out_shape=jax.ShapeDtypeStruct((batch * seq, ATTN_WIDTH), F32),
        compiler_params=pltpu.CompilerParams(
            dimension_semantics=("arbitrary", "arbitrary", "arbitrary"),
            vmem_limit_bytes=VMEM_LIMIT),
        name="attention",
    )(q, k, v)


def _split3(x):
    hi = x.astype(BF16)
    r1 = x - hi.astype(F32)
    mid = r1.astype(BF16)
    lo = (r1 - mid.astype(F32)).astype(BF16)
    return hi, mid, lo


def _ssd_kernel(x_ref, b_ref, c_ref, z_ref, dt_ref, sh_ref, place_ref, lhs_ones_ref, rhs_const_ref,
                cwx_ref, cwb_ref, cwc_ref, cbx_ref, cbb_ref, cbc_ref, alog_ref, dtbias_ref, dskip_ref, gn_ref,
                o_ref, xc_scr, xm_scr, nsf_scr, nsb_scr, dtv_scr, w_scr, dec_scr, lhs_scr, rhs_scr, rpart_scr):
    nc = dt_ref.shape[0]
    seq = x_ref.shape[0]
    L = SSM_CHUNK
    hpg = HEADS_PER_GROUP
    npair = hpg // 2
    P = SSM_HEAD_DIM
    x_cols = slice(0, GROUP_INNER)
    b_cols = slice(GROUP_INNER, GROUP_INNER + SSM_STATE)
    c_cols = slice(GROUP_INNER + SSM_STATE, GROUP_COLS)
    pair_cols = [slice(j * L, (j + 1) * L) for j in range(npair)]

    a_neg = -jnp.exp(alog_ref[...]) * LOG2E
    dtv = jax.nn.softplus(dt_ref[...] + dtbias_ref[...][None])
    dtv_scr[...] = dtv
    da2 = (dtv * a_neg[None]).reshape(nc * DT_ROWS, L)
    ri = lax.broadcasted_iota(jnp.int32, (L, L), 0)
    ci = lax.broadcasted_iota(jnp.int32, (L, L), 1)
    upper = (ri <= ci).astype(BF16)
    lower = (ri >= ci).astype(BF16)
    dparts = _split3(da2)
    cs_f = sum(_dot(p, upper) for p in dparts)
    cs_b = sum(_dot(p, lower) for p in dparts)
    rowsel = (ri & hpg) == 0
    cs2 = jnp.where(rowsel, cs_f, cs_b)
    col_f = sum(_dot_nt(lower, p) for p in dparts)
    col_b = sum(_dot_nt(upper, p) for p in dparts)
    colcs = jnp.where((ci & hpg) == 0, col_f, col_b)

    lhs_all = _dot(jnp.concatenate(_split3(colcs), axis=1), place_ref[...]) + lhs_ones_ref[...]
    for c in range(nc):
        lhs_scr[c] = lhs_all[:, c * L:(c + 1) * L].astype(BF16)
    for i, part in enumerate(_split3(-cs2)):
        rpart_scr[i] = part.astype(F32).reshape(nc, DT_ROWS, L)
    for u in range(SSD_UNROLL):
        rhs_scr[u] = rhs_const_ref[...]

    def lane_bcast(col):
        return jnp.broadcast_to(col, (col.shape[0], L))

    tot2 = jnp.where(rowsel, lane_bcast(cs2[:, L - 1:L]), lane_bcast(cs2[:, 0:1]))
    dec_scr[...] = jnp.exp2(tot2).reshape(nc, DT_ROWS, L)
    w_scr[...] = (dtv.reshape(nc * DT_ROWS, L) * jnp.exp2(tot2 - cs2)).reshape(nc, DT_ROWS, L)

    lane_t = lax.broadcasted_iota(jnp.int32, (L, L), 1)
    low_half = lane_t < P

    cw_halves = (cwx_ref[...], jnp.concatenate([cwb_ref[...], cwc_ref[...]], axis=1))
    cb_halves = (cbx_ref[...], jnp.concatenate([cbb_ref[...], cbc_ref[...]], axis=1))

    def rows(hf, start, size):
        if hf == 0:
            return x_ref[pl.ds(start, size), :]
        return jnp.concatenate([b_ref[pl.ds(start, size), :], c_ref[pl.ds(start, size), :]], axis=1)

    def conv_stage(c):
        base = pl.multiple_of(c * L, L)
        ws = pl.multiple_of(jnp.clip(c * L - CONV_WIN_LEAD, 0, seq - CONV_WIN), CONV_WIN_LEAD)
        variant = jnp.where(c == 0, 0, jnp.where(c == nc - 1, 2, 1))
        sh = sh_ref[variant]
        halves = []
        for hf in range(2):
            cw = cw_halves[hf]
            shifted = _dot(sh, rows(hf, ws, CONV_WIN))
            acc = cb_halves[hf] + cw[SSM_CONV // 2:SSM_CONV // 2 + 1, :] * rows(hf, base, L).astype(F32)
            for t, kk in enumerate(CONV_SHIFTED_TAPS):
                acc = acc + cw[kk:kk + 1, :] * shifted[t * L:(t + 1) * L, :]
            halves.append(acc * jax.nn.sigmoid(acc))
        xc = jnp.concatenate(halves, axis=1)
        xc_scr[c] = xc
        for j in range(npair):
            xp = xc[:, pair_cols[j]]
            xm_scr[c, j] = jnp.concatenate([jnp.where(low_half, xp, 0.0), jnp.where(low_half, 0.0, xp)],
                                           axis=0).astype(BF16)

    def state_loads(c):
        return xc_scr[c, :, b_cols], [xm_scr[c, j] for j in range(npair)]

    def state_stage(c, loaded):
        b_tok, xm = loaded
        bt = b_tok.T
        for d, ns_scr in ((0, nsf_scr), (1, nsb_scr)):
            w = w_scr[c, d * hpg:(d + 1) * hpg, :]
            ns_scr[c] = jnp.concatenate(
                [_dot(jnp.concatenate([(bt * w[2 * j:2 * j + 1, :]).astype(BF16),
                                       (bt * w[2 * j + 1:2 * j + 2, :]).astype(BF16)], axis=1), xm[j])
                 for j in range(npair)], axis=1)

    conv_stage(0)

    def pass_a(c, carry):
        loaded = state_loads(c)
        conv_stage(c + 1)
        state_stage(c, loaded)
        return carry
    lax.fori_loop(0, nc - 1, pass_a, 0, unroll=PASS_A_UNROLL)
    state_stage(nc - 1, state_loads(nc - 1))

    lane_r = lax.broadcasted_iota(jnp.int32, (1, L), 1)

    def decay_row(c, d):
        dec = dec_scr[c, d * hpg:(d + 1) * hpg, :]
        return jnp.concatenate([jnp.where(lane_r < P, dec[2 * j:2 * j + 1, :], dec[2 * j + 1:2 * j + 2, :])
                                for j in range(npair)], axis=1)

    def pass_b(d, ns_scr):
        def body(i, st):
            c = i if d == 0 else nc - 1 - i
            new = ns_scr[c]
            ns_scr[c] = st
            return st * decay_row(c, d) + new
        lax.fori_loop(0, nc, body, jnp.zeros((SSM_STATE, GROUP_INNER), F32))
    pass_b(0, nsf_scr)
    pass_b(1, nsb_scr)

    mask_f = ci <= ri
    mask_b = ci >= ri
    neg_inf = jnp.float32(-jnp.inf)
    sub16 = lax.broadcasted_iota(jnp.int32, (SEG_K_PER_ROW, L), 0)

    def pass_c(c, carry):
        base = pl.multiple_of(c * L, L)
        xc = xc_scr[c]
        bm = xc[:, b_cols].astype(BF16)
        cm = xc[:, c_cols].astype(BF16)
        cb = _dot_nt(cm, bm)
        off_f = _dot(cm, nsf_scr[c].astype(BF16))
        off_b = _dot(cm, nsb_scr[c].astype(BF16))
        slot = c % SSD_UNROLL
        for hd in range(DT_ROWS):
            blk = jnp.where(sub16 < 3, 1.0, 0.0)
            for i in range(3):
                blk = jnp.where(sub16 == 3 + i, rpart_scr[i, c, hd:hd + 1, :], blk)
            rhs_scr[slot, hd * SEG_K_PER_ROW:(hd + 1) * SEG_K_PER_ROW, hd * L:(hd + 1) * L] = blk.astype(BF16)
        seg = _dot(lhs_scr[c], rhs_scr[slot])
        dt_f_r = dtv_scr[c, 0:hpg, :]
        dt_b_r = dtv_scr[c, hpg:DT_ROWS, :]
        ys = []
        for j in range(npair):
            ws_ = []
            for h in (2 * j, 2 * j + 1):
                ef = jnp.exp2(jnp.where(mask_f, seg[:, h * L:(h + 1) * L], neg_inf))
                eb = jnp.exp2(jnp.where(mask_b, seg[:, (hpg + h) * L:(hpg + h + 1) * L], neg_inf))
                ws_.append((cb * (ef * dt_f_r[h:h + 1, :] + eb * dt_b_r[h:h + 1, :])).astype(BF16))
            scale_f = jnp.exp2(seg[:, (DT_ROWS + j) * L:(DT_ROWS + j + 1) * L])
            scale_b = jnp.exp2(seg[:, (DT_ROWS + npair + j) * L:(DT_ROWS + npair + j + 1) * L])
            ys.append(_dot(jnp.concatenate(ws_, axis=1), xm_scr[c, j])
                      + scale_f * off_f[:, pair_cols[j]] + scale_b * off_b[:, pair_cols[j]])
        y = jnp.concatenate(ys, axis=1) + dskip_ref[...] * xc[:, x_cols]
        zc = z_ref[pl.ds(base, L), :].astype(F32)
        y = y * (zc * jax.nn.sigmoid(zc))
        o_ref[pl.ds(base, L), :] = y * _inv_rms(y) * gn_ref[...]
        return carry
    lax.fori_loop(0, nc, pass_c, 0, unroll=SSD_UNROLL)


def _conv_shift_matrices():
    out = np.zeros((3, len(CONV_SHIFTED_TAPS) * SSM_CHUNK, CONV_WIN), np.float32)
    for v, lead in enumerate((0, CONV_WIN_LEAD, CONV_WIN - SSM_CHUNK)):
        for i, kk in enumerate(CONV_SHIFTED_TAPS):
            for t in range(SSM_CHUNK):
                j = t + kk - SSM_CONV // 2 + lead
                if 0 <= j < CONV_WIN:
                    out[v, i * SSM_CHUNK + t, j] = 1.0
    return out


def _seg_matmul_constants(nc):
    L, K, half = SSM_CHUNK, SEG_K_PER_ROW, SSM_HEAD_DIM
    place = np.zeros((3 * L, nc * L), np.float32)
    lhs_ones = np.zeros((1, nc * L), np.float32)
    for c in range(nc):
        for hd in range(DT_ROWS):
            for i in range(3):
                place[i * L + c * DT_ROWS + hd, c * L + hd * K + i] = 1.0
                lhs_ones[0, c * L + hd * K + 3 + i] = 1.0
    rhs = np.zeros((L, SEG_BLOCKS * L), np.float32)
    for hd in range(DT_ROWS):
        d, h = divmod(hd, HEADS_PER_GROUP)
        pair_block = DT_ROWS + d * (HEADS_PER_GROUP // 2) + h // 2
        lanes = slice(0, half) if h % 2 == 0 else slice(half, L)
        rhs[hd * K:hd * K + 3, hd * L:(hd + 1) * L] = 1.0
        rhs[hd * K:hd * K + 3, pair_block * L:(pair_block + 1) * L][:, lanes] = 1.0
    return place, lhs_ones, rhs


def _ssd_call(xbc, z, dtc, cw, cbias, alog, dtbias, dskip, gn, batch, seq):
    nc = seq // SSM_CHUNK
    L = SSM_CHUNK
    assert nc >= 3 and seq >= CONV_WIN and nc * DT_ROWS == L
    sh = jnp.asarray(_conv_shift_matrices(), BF16)
    place, lhs_ones, rhs_const = _seg_matmul_constants(nc)
    place = jnp.asarray(place, BF16)
    lhs_ones = jnp.asarray(lhs_ones, F32)
    rhs_const = jnp.asarray(rhs_const, BF16)
    const = lambda a: pl.BlockSpec(a.shape, lambda b, g: (0,) * a.ndim)
    b_blk0 = SSM_INNER // SSM_STATE
    c_blk0 = b_blk0 + SSM_GROUPS

    def xbc_views(rows_, batched):
        lead = (lambda b: b) if batched else (lambda b: 0)
        return [pl.BlockSpec((rows_, GROUP_INNER), lambda b, g: (lead(b), g)),
                pl.BlockSpec((rows_, SSM_STATE), lambda b, g: (lead(b), b_blk0 + g)),
                pl.BlockSpec((rows_, SSM_STATE), lambda b, g: (lead(b), c_blk0 + g))]
    return pl.pallas_call(
        _ssd_kernel,
        grid=(batch, SSM_GROUPS),
        in_specs=xbc_views(seq, True) + [
                  pl.BlockSpec((seq, GROUP_INNER), lambda b, g: (b, g)),
                  pl.BlockSpec((nc, DT_ROWS, L), lambda b, g: (b, g, 0)),
                  const(sh), const(place), const(lhs_ones), const(rhs_const)]
                 + xbc_views(SUBLANES, False) + xbc_views(1, False) + [
                  pl.BlockSpec((DT_ROWS, L), lambda b, g: (g, 0)),
                  pl.BlockSpec((DT_ROWS, L), lambda b, g: (g, 0)),
                  pl.BlockSpec((1, GROUP_INNER), lambda b, g: (0, g)),
                  pl.BlockSpec((1, GROUP_INNER), lambda b, g: (0, g))],
        out_specs=pl.BlockSpec((seq, GROUP_INNER), lambda b, g: (b, g)),
        out_shape=jax.ShapeDtypeStruct((batch * seq, SSM_INNER), F32),
        scratch_shapes=[pltpu.VMEM((nc, L, GROUP_COLS), F32),
                        pltpu.VMEM((nc, HEADS_PER_GROUP // 2, 2 * L, L), BF16),
                        pltpu.VMEM((nc, SSM_STATE, GROUP_INNER), F32),
                        pltpu.VMEM((nc, SSM_STATE, GROUP_INNER), F32),
                        pltpu.VMEM((nc, DT_ROWS, L), F32),
                        pltpu.VMEM((nc, DT_ROWS, L), F32),
                        pltpu.VMEM((nc, DT_ROWS, L), F32),
                        pltpu.VMEM((nc, L, L), BF16),
                        pltpu.VMEM((SSD_UNROLL, L, SEG_BLOCKS * L), BF16),
                        pltpu.VMEM((3, nc, DT_ROWS, L), F32)],
        compiler_params=pltpu.CompilerParams(dimension_semantics=("arbitrary", "arbitrary"),
                                             vmem_limit_bytes=VMEM_LIMIT),
        name="ssd",
    )(xbc, xbc, xbc, z, dtc, sh, place, lhs_ones, rhs_const, cw, cw, cw, cbias, cbias, cbias,
      alog, dtbias, dskip, gn)


def _mlp_kernel(x_ref, attn_ref, ssm_ref, gat_ref, wo_ref, gm_ref, wup_ref, wdn_ref, o_ref):
    a = attn_ref[...]
    an = (a * _inv_rms(a) * gat_ref[...]).astype(BF16)
    mix = _dot(an, wo_ref[0:ATTN_WIDTH, :]) + _dot(ssm_ref[...].astype(BF16), wo_ref[ATTN_WIDTH:D_MIX, :])
    x1 = x_ref[...] + mix
    hm = (x1 * _inv_rms(x1) * gm_ref[...]).astype(BF16)
    acc = jnp.zeros_like(x1)
    for c in range(D_FF // FF_CHUNK):
        cols = slice(c * FF_CHUNK, (c + 1) * FF_CHUNK)
        u = _dot(hm, wup_ref[:, cols])
        acc = acc + _dot(jnp.square(jnp.maximum(u, 0.0)).astype(BF16), wdn_ref[cols, :])
    o_ref[...] = x1 + acc


def _mlp_call(x2, attn, ssm, gat, wo, gm, wup, wdn):
    t = x2.shape[0]
    row = lambda w: pl.BlockSpec((TM_MLP, w), lambda i: (i, 0))
    full = lambda a: pl.BlockSpec(a.shape, lambda i: (0, 0), pipeline_mode=pl.Buffered(1))
    return pl.pallas_call(
        _mlp_kernel,
        grid=(t // TM_MLP,),
        in_specs=[row(D_MODEL), row(ATTN_WIDTH), row(SSM_INNER), full(gat), full(wo), full(gm),
                  full(wup), full(wdn)],
        out_specs=row(D_MODEL),
        out_shape=jax.ShapeDtypeStruct((t, D_MODEL), F32),
        compiler_params=pltpu.CompilerParams(dimension_semantics=("arbitrary",),
                                             vmem_limit_bytes=VMEM_LIMIT),
        name="outproj_mlp",
    )(x2, attn, ssm, gat, wo, gm, wup, wdn)


def _group_major(fwd, bwd):
    lead = fwd.shape[:-1]
    both = jnp.stack([fwd.reshape(lead + (SSM_GROUPS, HEADS_PER_GROUP)),
                      bwd.reshape(lead + (SSM_GROUPS, HEADS_PER_GROUP))], axis=-2)
    return both.reshape(lead + (2 * SSM_HEADS,))


def _rotate_half(w):
    half = QK_ROPE_DIM // 2
    return jnp.concatenate([-w[..., half:], w[..., :half]], axis=-1)


def _swap_halves(g):
    half = QK_ROPE_DIM // 2
    return jnp.concatenate([g[..., half:], g[..., :half]], axis=-1)


def _lane_bcast(v):
    return jnp.broadcast_to(v[..., None], v.shape + (LANES,)).astype(F32)


def _layer(x2, cs, batch, seq, ln_mix_g, w_in, q_a_norm_g, w_uq, kv_a_norm_g, w_ukv, q_norm_g,
           k_norm_g, attn_out_norm_g, conv_w, conv_b, a_log_fwd, a_log_bwd, dt_bias_fwd, dt_bias_bwd,
           d_skip, ssm_norm_g, w_out, ln_mlp_g, w_mlp_up, w_mlp_down):
    win = _win_prep_call(*w_in)
    wq = w_uq.reshape(Q_LORA_RANK, ATTN_HEADS, QK_HEAD_DIM)
    wuq = jnp.concatenate([wq, _rotate_half(wq[..., QK_NOPE_DIM:])], axis=-1)
    wuq = wuq.reshape(Q_LORA_RANK, ATTN_HEADS * HEAD_PAD).astype(BF16)
    wukv = w_ukv.reshape(KV_LORA_RANK, ATTN_HEADS, QK_NOPE_DIM + V_HEAD_DIM)
    wuk = jnp.pad(wukv[:, :, :QK_NOPE_DIM],
                  ((0, 0), (0, 0), (0, HEAD_PAD - QK_NOPE_DIM))).reshape(KV_LORA_RANK, -1).astype(BF16)
    wv = wukv[:, :, QK_NOPE_DIM:].reshape(KV_LORA_RANK, ATTN_HEADS // 2, 2, V_HEAD_DIM)
    zv = jnp.zeros_like(wv[:, :, 0])
    wuv = jnp.stack([jnp.concatenate([wv[:, :, 0], zv], axis=-1),
                     jnp.concatenate([zv, wv[:, :, 1]], axis=-1)], axis=2).reshape(KV_LORA_RANK, -1).astype(BF16)
    ones_v = np.ones((ATTN_HEADS // 2, V_HEAD_DIM), np.float32)
    vone = jnp.asarray(np.stack([np.concatenate([0 * ones_v, ones_v], axis=-1),
                                 np.concatenate([ones_v, 0 * ones_v], axis=-1)], axis=1).reshape(1, -1))
    scale = QK_HEAD_DIM ** -0.5 * np.log2(np.e)
    gq = (jnp.concatenate([q_norm_g, _swap_halves(q_norm_g[QK_NOPE_DIM:])]) * scale)[None, :]
    zeros_nope = jnp.zeros((QK_NOPE_DIM,), k_norm_g.dtype)
    gkn = jnp.concatenate([k_norm_g[:QK_NOPE_DIM], zeros_nope])[None, :]
    gkr = jnp.concatenate([zeros_nope, k_norm_g[QK_NOPE_DIM:], _swap_halves(k_norm_g[QK_NOPE_DIM:])])[None, :]

    q, k, v, z, xbc, dtc = _inproj_call(
        x2, cs, ln_mix_g[None, :], win, q_a_norm_g[None, :], wuq, kv_a_norm_g[None, :], wuk, wuv, vone,
        gq, gkn, gkr)

    attn = _attn_call(q, k, v, batch, seq)

    cw = jnp.pad(conv_w[:, 0, :], ((0, SUBLANES - SSM_CONV), (0, 0)))
    cbias = conv_b[None, :]
    alog = _lane_bcast(_group_major(a_log_fwd, a_log_bwd))
    dtbias = _lane_bcast(_group_major(dt_bias_fwd, dt_bias_bwd))
    dskip = jnp.repeat(d_skip, SSM_HEAD_DIM)[None, :]
    ssm = _ssd_call(xbc, z, dtc, cw, cbias, alog, dtbias, dskip, ssm_norm_g[None, :], batch, seq)

    return _mlp_call(x2, attn, ssm, attn_out_norm_g[None, :], w_out.astype(BF16), ln_mlp_g[None, :],
                     w_mlp_up.astype(BF16), w_mlp_down.astype(BF16))


def _rope_table(positions):
    inv_freq = 1.0 / (ROPE_THETA ** (jnp.arange(0, QK_ROPE_DIM, 2, dtype=F32) / QK_ROPE_DIM))
    ang = inv_freq[:, None] * positions.astype(F32).reshape(1, -1)
    return jnp.concatenate([jnp.cos(ang), jnp.sin(ang)], axis=0)


def kernel(x, positions, ln_mix_g, w_in, q_a_norm_g, w_uq, kv_a_norm_g, w_ukv, q_norm_g, k_norm_g,
           attn_out_norm_g, conv_w, conv_b, a_log_fwd, a_log_bwd, dt_bias_fwd, dt_bias_bwd, d_skip,
           ssm_norm_g, w_out, ln_mlp_g, w_mlp_up, w_mlp_down):
    batch, seq, d = x.shape
    assert d == D_MODEL and seq % TQ == 0 and (batch * seq) % TM_IN == 0 and (batch * seq) % TM_MLP == 0
    cs = _rope_table(positions)
    x2 = x.reshape(batch * seq, d)
    for l in range(ln_mix_g.shape[0]):
        x2 = _layer(x2, cs, batch, seq, ln_mix_g[l], (w_in, l), q_a_norm_g[l], w_uq[l], kv_a_norm_g[l],
                    w_ukv[l], q_norm_g[l], k_norm_g[l], attn_out_norm_g[l], conv_w[l], conv_b[l],
                    a_log_fwd[l], a_log_bwd[l], dt_bias_fwd[l], dt_bias_bwd[l], d_skip[l], ssm_norm_g[l],
                    w_out[l], ln_mlp_g[l], w_mlp_up[l], w_mlp_down[l])
    return x2.reshape(batch, seq, d)
```

```python
import numpy as np
import jax
import jax.numpy as jnp
from jax import lax
from jax.experimental import pallas as pl
from jax.experimental.pallas import tpu as pltpu

F32 = jnp.float32
BF16 = jnp.bfloat16

D_MODEL = 1024
ATTN_HEADS = 8
QK_NOPE_DIM = 64
QK_ROPE_DIM = 32
QK_HEAD_DIM = QK_NOPE_DIM + QK_ROPE_DIM
V_HEAD_DIM = 64
Q_LORA_RANK = D_MODEL // 4
KV_LORA_RANK = D_MODEL // 8
ROPE_THETA = 10000.0
ATTN_WIDTH = ATTN_HEADS * V_HEAD_DIM
SSM_HEADS = 8
SSM_HEAD_DIM = 64
SSM_INNER = SSM_HEADS * SSM_HEAD_DIM
SSM_GROUPS = 2
SSM_STATE = 128
SSM_CONV = 5
SSM_CHUNK = 128
SSM_CONV_CH = SSM_INNER + 2 * SSM_GROUPS * SSM_STATE
D_MIX = ATTN_WIDTH + SSM_INNER
D_FF = 4 * D_MODEL
EPS = 1e-6

LANES = 128
SUBLANES = 8
HEAD_PAD = LANES

HEADS_PER_GROUP = SSM_HEADS // SSM_GROUPS
GROUP_INNER = SSM_INNER // SSM_GROUPS
GROUP_COLS = GROUP_INNER + 2 * SSM_STATE
DT_ROWS = 2 * HEADS_PER_GROUP
SEG_K_PER_ROW = 16
SEG_BLOCKS = DT_ROWS + HEADS_PER_GROUP
LOG2E = float(np.log2(np.e))
CONV_WIN = 2 * SSM_CHUNK
CONV_WIN_LEAD = SSM_CHUNK // 2
CONV_SHIFTED_TAPS = tuple(k for k in range(SSM_CONV) if k != SSM_CONV // 2)

COL_CKV = Q_LORA_RANK
COL_MISC = COL_CKV + KV_LORA_RANK
COL_Z = COL_MISC + LANES
COL_XBC = COL_Z + SSM_INNER
IN_COLS = COL_XBC + SSM_CONV_CH

TM_IN = 1024
TQ = 512
ATTN_HEADS_PER_STEP = 8
TM_MLP = 512
FF_CHUNK = 1024
SSD_UNROLL = 4
PASS_A_UNROLL = 3
WIN_PREP_STEPS = 4
VMEM_LIMIT = 56 * 1024 * 1024


def _inv_rms(x):
    n = x.shape[-1]
    x2 = x * x
    acc = x2[:, 0:LANES]
    for i in range(1, n // LANES):
        acc = acc + x2[:, i * LANES:(i + 1) * LANES]
    return lax.rsqrt(jnp.sum(acc, axis=-1, keepdims=True) * (1.0 / n) + EPS)


def _dot(a, b):
    return jnp.dot(a, b, preferred_element_type=F32)


def _dot_nt(a, b):
    return lax.dot_general(a, b, (((1,), (1,)), ((), ())), preferred_element_type=F32)


def _inproj_kernel(x_ref, cs_ref, g_ref, win_ref, gqa_ref, wuq_ref, gkva_ref, wuk_ref, wuv_ref, vone_ref,
                   gq_ref, gkn_ref, gkr_ref,
                   q_ref, k_ref, v_ref, z_ref, xbc_ref, dt_ref):
    tm = x_ref.shape[0]
    x = x_ref[...]
    h = (x * _inv_rms(x) * g_ref[...]).astype(BF16)
    big = _dot(h, win_ref[...])
    z_ref[...] = big[:, COL_Z:COL_XBC].astype(BF16)
    xbc_ref[...] = big[:, COL_XBC:IN_COLS].astype(BF16)
    misc = big[:, COL_MISC:COL_Z]
    for c in range(tm // SSM_CHUNK):
        dt_ref[c] = misc[c * SSM_CHUNK:(c + 1) * SSM_CHUNK, :].T[0:2 * SSM_HEADS, :]

    cq = big[:, 0:COL_CKV]
    ckv = big[:, COL_CKV:COL_MISC]
    cqn = (cq * _inv_rms(cq) * gqa_ref[...]).astype(BF16)
    ckvn = (ckv * _inv_rms(ckv) * gkva_ref[...]).astype(BF16)
    q_pre = _dot(cqn, wuq_ref[...])
    k_pre = _dot(ckvn, wuk_ref[...])
    v_ref[...] = (_dot(ckvn, wuv_ref[...]) + vone_ref[...]).astype(BF16)

    half = QK_ROPE_DIM // 2
    zpad = jnp.zeros((HEAD_PAD - QK_ROPE_DIM, SSM_CHUNK), F32)
    tt = jnp.concatenate([jnp.concatenate([cs_ref[:, c * SSM_CHUNK:(c + 1) * SSM_CHUNK], zpad], axis=0).T
                          for c in range(tm // SSM_CHUNK)], axis=0)
    lane_t = lax.broadcasted_iota(jnp.int32, (tm, HEAD_PAD), 1)
    cs = jnp.where(lane_t < QK_NOPE_DIM, 1.0,
                   jnp.where(lane_t < QK_NOPE_DIM + half, pltpu.roll(tt, QK_NOPE_DIM, 1),
                             jnp.where(lane_t < QK_HEAD_DIM + half, pltpu.roll(tt, QK_NOPE_DIM + half, 1),
                                       pltpu.roll(tt, QK_HEAD_DIM, 1))))
    lane = lax.broadcasted_iota(jnp.int32, (1, HEAD_PAD), 1)
    in_head = (lane < QK_HEAD_DIM).astype(F32)
    is_rope = ((lane >= QK_NOPE_DIM) & (lane < QK_HEAD_DIM)).astype(F32)
    inv_d = 1.0 / QK_HEAD_DIM

    gcq = gq_ref[...] * cs
    for hh in range(ATTN_HEADS):
        sl = slice(hh * HEAD_PAD, (hh + 1) * HEAD_PAD)
        qh = q_pre[:, sl]
        ssq = jnp.sum(qh * qh * in_head, axis=-1, keepdims=True)
        q_ref[:, sl] = (qh * lax.rsqrt(ssq * inv_d + EPS) * gcq).astype(BF16)

    ab = misc * (gkr_ref[...] * cs)
    lane2 = lax.broadcasted_iota(jnp.int32, (tm, HEAD_PAD), 1)
    swapped = jnp.where(lane2 < QK_HEAD_DIM, pltpu.roll(ab, HEAD_PAD - QK_ROPE_DIM, 1),
                        pltpu.roll(ab, QK_ROPE_DIM, 1))
    s_both = jnp.where(lane2 >= QK_NOPE_DIM, ab + swapped, 0.0)
    ssq_pe = jnp.sum(misc * misc * is_rope, axis=-1, keepdims=True)
    gkn = gkn_ref[...]
    for hh in range(ATTN_HEADS):
        sl = slice(hh * HEAD_PAD, (hh + 1) * HEAD_PAD)
        kh = k_pre[:, sl]
        ssq = jnp.sum(kh * kh, axis=-1, keepdims=True) + ssq_pe
        k_ref[:, sl] = ((kh * gkn + s_both) * lax.rsqrt(ssq * inv_d + EPS)).astype(BF16)


def _win_prep_kernel(w_ref, o_ref):
    w = w_ref[...]
    o_kpe = Q_LORA_RANK + KV_LORA_RANK
    o_z = o_kpe + QK_ROPE_DIM
    o_dt = o_z + SSM_INNER + SSM_CONV_CH
    half = QK_ROPE_DIM // 2
    hpg = HEADS_PER_GROUP
    dt_f = w[:, o_dt:o_dt + SSM_HEADS]
    dt_b = w[:, o_dt + SSM_HEADS:o_dt + 2 * SSM_HEADS]
    kpe = w[:, o_kpe:o_z]
    pieces = [w[:, 0:o_kpe]]
    for g in range(SSM_GROUPS):
        pieces += [dt_f[:, g * hpg:(g + 1) * hpg], dt_b[:, g * hpg:(g + 1) * hpg]]
    pieces += [jnp.zeros((w.shape[0], QK_NOPE_DIM - 2 * SSM_HEADS), F32), kpe, -kpe[:, half:], kpe[:, :half],
               w[:, o_z:o_dt]]
    o_ref[...] = jnp.concatenate(pieces, axis=1).astype(BF16)


def _win_prep_call(w_in3, layer):
    depth, d, n = w_in3.shape
    rows = d // WIN_PREP_STEPS
    return pl.pallas_call(
        _win_prep_kernel,
        grid=(WIN_PREP_STEPS,),
        in_specs=[pl.BlockSpec((rows, n), lambda i: (layer * WIN_PREP_STEPS + i, 0))],
        out_specs=pl.BlockSpec((rows, IN_COLS), lambda i: (i, 0)),
        out_shape=jax.ShapeDtypeStruct((d, IN_COLS), BF16),
        compiler_params=pltpu.CompilerParams(dimension_semantics=("arbitrary",), vmem_limit_bytes=VMEM_LIMIT),
        name="win_prep",
    )(w_in3.reshape(depth * d, n))


def _inproj_call(x2, cs, g, win, gqa, wuq, gkva, wuk, wuv, vone, gq, gkn, gkr):
    t = x2.shape[0]
    cpt = TM_IN // SSM_CHUNK
    full = lambda a: pl.BlockSpec(a.shape, lambda i: (0,) * a.ndim, pipeline_mode=pl.Buffered(1))
    row = lambda w: pl.BlockSpec((TM_IN, w), lambda i: (i, 0))
    return pl.pallas_call(
        _inproj_kernel,
        grid=(t // TM_IN,),
        in_specs=[row(D_MODEL), pl.BlockSpec((QK_ROPE_DIM, TM_IN), lambda i: (0, i)), full(g), full(win), full(gqa), full(wuq), full(gkva),
                  full(wuk), full(wuv), full(vone), full(gq), full(gkn), full(gkr)],
        out_specs=[row(ATTN_HEADS * HEAD_PAD), row(ATTN_HEADS * HEAD_PAD), row(ATTN_HEADS * HEAD_PAD),
                   row(SSM_INNER), row(SSM_CONV_CH),
                   pl.BlockSpec((cpt, 2 * SSM_HEADS, SSM_CHUNK), lambda i: (i, 0, 0))],
        out_shape=[jax.ShapeDtypeStruct((t, ATTN_HEADS * HEAD_PAD), BF16),
                   jax.ShapeDtypeStruct((t, ATTN_HEADS * HEAD_PAD), BF16),
                   jax.ShapeDtypeStruct((t, ATTN_HEADS * HEAD_PAD), BF16),
                   jax.ShapeDtypeStruct((t, SSM_INNER), BF16),
                   jax.ShapeDtypeStruct((t, SSM_CONV_CH), BF16),
                   jax.ShapeDtypeStruct((t // SSM_CHUNK, 2 * SSM_HEADS, SSM_CHUNK), F32)],
        compiler_params=pltpu.CompilerParams(dimension_semantics=("arbitrary",),
                                             vmem_limit_bytes=VMEM_LIMIT),
        name="inproj",
    )(x2, cs, g, win, gqa, wuq, gkva, wuk, wuv, vone, gq, gkn, gkr)


def _attn_kernel(q_ref, k_ref, v_ref, o_ref):
    res = []
    for j in range(ATTN_HEADS_PER_STEP):
        sl = slice(j * HEAD_PAD, (j + 1) * HEAD_PAD)
        s = _dot_nt(q_ref[:, sl], k_ref[:, sl])
        m = jnp.max(s, axis=-1, keepdims=True)
        acc = _dot(jnp.exp2(s - m).astype(BF16), v_ref[:, sl])
        res.append(acc / pltpu.roll(acc, V_HEAD_DIM, 1))
    lane = lax.broadcasted_iota(jnp.int32, res[0].shape, 1)
    for jp in range(ATTN_HEADS_PER_STEP // 2):
        o_ref[:, jp * HEAD_PAD:(jp + 1) * HEAD_PAD] = jnp.where(lane < V_HEAD_DIM, res[2 * jp], res[2 * jp + 1])


def _attn_call(q, k, v, batch, seq):
    n_q = seq // TQ
    hps = ATTN_HEADS_PER_STEP
    return pl.pallas_call(
        _attn_kernel,
        grid=(batch, ATTN_HEADS // hps, n_q),
        in_specs=[pl.BlockSpec((TQ, hps * HEAD_PAD), lambda b, hp, i: (b * n_q + i, hp)),
                  pl.BlockSpec((seq, hps * HEAD_PAD), lambda b, hp, i: (b, hp)),
                  pl.BlockSpec((seq, hps * HEAD_PAD), lambda b, hp, i: (b, hp))],
        out_specs=pl.BlockSpec((TQ, hps * V_HEAD_DIM), lambda b, hp, i: (b * n_q + i, hp)),
        out_shape=jax.ShapeDtypeStruct((batch * seq, ATTN_WIDTH), F32),
        compiler_params=pltpu.CompilerParams(
            dimension_semantics=("arbitrary", "arbitrary", "arbitrary"),
            vmem_limit_bytes=VMEM_LIMIT),
        name="attention",
    )(q, k, v)


def _split3(x):
    hi = x.astype(BF16)
    r1 = x - hi.astype(F32)
    mid = r1.astype(BF16)
    lo = (r1 - mid.astype(F32)).astype(BF16)
    return hi, mid, lo


def _ssd_kernel(x_ref, b_ref, c_ref, dt_ref, sh_ref, place_ref, lhs_ones_ref, rhs_const_ref,
                cwx_ref, cwb_ref, cwc_ref, cbx_ref, cbb_ref, cbc_ref, alog_ref, dtbias_ref, dskip_ref,
                o_ref, xc_scr, xm_scr, nsf_scr, nsb_scr, dtv_scr, w_scr, dec_scr, lhs_scr, rhs_scr, rpart_scr):
    nc = dt_ref.shape[0]
    seq = x_ref.shape[0]
    L = SSM_CHUNK
    hpg = HEADS_PER_GROUP
    npair = hpg // 2
    P = SSM_HEAD_DIM
    x_cols = slice(0, GROUP_INNER)
    b_cols = slice(GROUP_INNER, GROUP_INNER + SSM_STATE)
    c_cols = slice(GROUP_INNER + SSM_STATE, GROUP_COLS)
    pair_cols = [slice(j * L, (j + 1) * L) for j in range(npair)]

    a_neg = -jnp.exp(alog_ref[...]) * LOG2E
    dtv = jax.nn.softplus(dt_ref[...] + dtbias_ref[...][None])
    dtv_scr[...] = dtv
    da2 = (dtv * a_neg[None]).reshape(nc * DT_ROWS, L)
    ri = lax.broadcasted_iota(jnp.int32, (L, L), 0)
    ci = lax.broadcasted_iota(jnp.int32, (L, L), 1)
    upper = (ri <= ci).astype(BF16)
    lower = (ri >= ci).astype(BF16)
    dparts = _split3(da2)
    cs_f = sum(_dot(p, upper) for p in dparts)
    cs_b = sum(_dot(p, lower) for p in dparts)
    rowsel = (ri & hpg) == 0
    cs2 = jnp.where(rowsel, cs_f, cs_b)
    col_f = sum(_dot_nt(lower, p) for p in dparts)
    col_b = sum(_dot_nt(upper, p) for p in dparts)
    colcs = jnp.where((ci & hpg) == 0, col_f, col_b)

    lhs_all = _dot(jnp.concatenate(_split3(colcs), axis=1), place_ref[...]) + lhs_ones_ref[...]
    for c in range(nc):
        lhs_scr[c] = lhs_all[:, c * L:(c + 1) * L].astype(BF16)
    for i, part in enumerate(_split3(-cs2)):
        rpart_scr[i] = part.astype(F32).reshape(nc, DT_ROWS, L)
    for u in range(SSD_UNROLL):
        rhs_scr[u] = rhs_const_ref[...]

    def lane_bcast(col):
        return jnp.broadcast_to(col, (col.shape[0], L))

    tot2 = jnp.where(rowsel, lane_bcast(cs2[:, L - 1:L]), lane_bcast(cs2[:, 0:1]))
    dec_scr[...] = jnp.exp2(tot2).reshape(nc, DT_ROWS, L)
    w_scr[...] = (dtv.reshape(nc * DT_ROWS, L) * jnp.exp2(tot2 - cs2)).reshape(nc, DT_ROWS, L)

    lane_t = lax.broadcasted_iota(jnp.int32, (L, L), 1)
    low_half = lane_t < P

    cw_halves = (cwx_ref[...], jnp.concatenate([cwb_ref[...], cwc_ref[...]], axis=1))
    cb_halves = (cbx_ref[...], jnp.concatenate([cbb_ref[...], cbc_ref[...]], axis=1))

    def rows(hf, start, size):
        if hf == 0:
            return x_ref[pl.ds(start, size), :]
        return jnp.concatenate([b_ref[pl.ds(start, size), :], c_ref[pl.ds(start, size), :]], axis=1)

    def conv_stage(c):
        base = pl.multiple_of(c * L, L)
        ws = pl.multiple_of(jnp.clip(c * L - CONV_WIN_LEAD, 0, seq - CONV_WIN), CONV_WIN_LEAD)
        variant = jnp.where(c == 0, 0, jnp.where(c == nc - 1, 2, 1))
        sh = sh_ref[variant]
        halves = []
        for hf in range(2):
            cw = cw_halves[hf]
            shifted = _dot(sh, rows(hf, ws, CONV_WIN))
            acc = cb_halves[hf] + cw[SSM_CONV // 2:SSM_CONV // 2 + 1, :] * rows(hf, base, L).astype(F32)
            for t, kk in enumerate(CONV_SHIFTED_TAPS):
                acc = acc + cw[kk:kk + 1, :] * shifted[t * L:(t + 1) * L, :]
            halves.append(acc * jax.nn.sigmoid(acc))
        xc = jnp.concatenate(halves, axis=1)
        xc_scr[c] = xc
        for j in range(npair):
            xp = xc[:, pair_cols[j]]
            xm_scr[c, j] = jnp.concatenate([jnp.where(low_half, xp, 0.0), jnp.where(low_half, 0.0, xp)],
                                           axis=0).astype(BF16)

    def state_loads(c):
        return xc_scr[c, :, b_cols], [xm_scr[c, j] for j in range(npair)]

    def state_stage(c, loaded):
        b_tok, xm = loaded
        bt = b_tok.T
        for d, ns_scr in ((0, nsf_scr), (1, nsb_scr)):
            w = w_scr[c, d * hpg:(d + 1) * hpg, :]
            ns_scr[c] = jnp.concatenate(
                [_dot(jnp.concatenate([(bt * w[2 * j:2 * j + 1, :]).astype(BF16),
                                       (bt * w[2 * j + 1:2 * j + 2, :]).astype(BF16)], axis=1), xm[j])
                 for j in range(npair)], axis=1)

    conv_stage(0)

    def pass_a(c, carry):
        loaded = state_loads(c)
        conv_stage(c + 1)
        state_stage(c, loaded)
        return carry
    lax.fori_loop(0, nc - 1, pass_a, 0, unroll=PASS_A_UNROLL)
    state_stage(nc - 1, state_loads(nc - 1))

    lane_r = lax.broadcasted_iota(jnp.int32, (1, L), 1)

    def decay_row(c, d):
        dec = dec_scr[c, d * hpg:(d + 1) * hpg, :]
        return jnp.concatenate([jnp.where(lane_r < P, dec[2 * j:2 * j + 1, :], dec[2 * j + 1:2 * j + 2, :])
                                for j in range(npair)], axis=1)

    def pass_b(d, ns_scr):
        def body(i, st):
            c = i if d == 0 else nc - 1 - i
            new = ns_scr[c]
            ns_scr[c] = st
            return st * decay_row(c, d) + new
        lax.fori_loop(0, nc, body, jnp.zeros((SSM_STATE, GROUP_INNER), F32))
    pass_b(0, nsf_scr)
    pass_b(1, nsb_scr)

    mask_f = ci <= ri
    mask_b = ci >= ri
    neg_inf = jnp.float32(-jnp.inf)
    sub16 = lax.broadcasted_iota(jnp.int32, (SEG_K_PER_ROW, L), 0)

    def pass_c(c, carry):
        base = pl.multiple_of(c * L, L)
        xc = xc_scr[c]
        bm = xc[:, b_cols].astype(BF16)
        cm = xc[:, c_cols].astype(BF16)
        cb = _dot_nt(cm, bm)
        off_f = _dot(cm, nsf_scr[c].astype(BF16))
        off_b = _dot(cm, nsb_scr[c].astype(BF16))
        slot = c % SSD_UNROLL
        for hd in range(DT_ROWS):
            blk = jnp.where(sub16 < 3, 1.0, 0.0)
            for i in range(3):
                blk = jnp.where(sub16 == 3 + i, rpart_scr[i, c, hd:hd + 1, :], blk)
            rhs_scr[slot, hd * SEG_K_PER_ROW:(hd + 1) * SEG_K_PER_ROW, hd * L:(hd + 1) * L] = blk.astype(BF16)
        seg = _dot(lhs_scr[c], rhs_scr[slot])
        dt_f_r = dtv_scr[c, 0:hpg, :]
        dt_b_r = dtv_scr[c, hpg:DT_ROWS, :]
        ys = []
        for j in range(npair):
            ws_ = []
            for h in (2 * j, 2 * j + 1):
                ef = jnp.exp2(jnp.where(mask_f, seg[:, h * L:(h + 1) * L], neg_inf))
                eb = jnp.exp2(jnp.where(mask_b, seg[:, (hpg + h) * L:(hpg + h + 1) * L], neg_inf))
                ws_.append((cb * (ef * dt_f_r[h:h + 1, :] + eb * dt_b_r[h:h + 1, :])).astype(BF16))
            scale_f = jnp.exp2(seg[:, (DT_ROWS + j) * L:(DT_ROWS + j + 1) * L])
            scale_b = jnp.exp2(seg[:, (DT_ROWS + npair + j) * L:(DT_ROWS + npair + j + 1) * L])
            ys.append(_dot(jnp.concatenate(ws_, axis=1), xm_scr[c, j])
                      + scale_f * off_f[:, pair_cols[j]] + scale_b * off_b[:, pair_cols[j]])
        y = jnp.concatenate(ys, axis=1) + dskip_ref[...] * xc[:, x_cols]
        o_ref[pl.ds(base, L), :] = y
        return carry
    lax.fori_loop(0, nc, pass_c, 0, unroll=SSD_UNROLL)


def _conv_shift_matrices():
    out = np.zeros((3, len(CONV_SHIFTED_TAPS) * SSM_CHUNK, CONV_WIN), np.float32)
    for v, lead in enumerate((0, CONV_WIN_LEAD, CONV_WIN - SSM_CHUNK)):
        for i, kk in enumerate(CONV_SHIFTED_TAPS):
            for t in range(SSM_CHUNK):
                j = t + kk - SSM_CONV // 2 + lead
                if 0 <= j < CONV_WIN:
                    out[v, i * SSM_CHUNK + t, j] = 1.0
    return out


def _seg_matmul_constants(nc):
    L, K, half = SSM_CHUNK, SEG_K_PER_ROW, SSM_HEAD_DIM
    place = np.zeros((3 * L, nc * L), np.float32)
    lhs_ones = np.zeros((1, nc * L), np.float32)
    for c in range(nc):
        for hd in range(DT_ROWS):
            for i in range(3):
                place[i * L + c * DT_ROWS + hd, c * L + hd * K + i] = 1.0
                lhs_ones[0, c * L + hd * K + 3 + i] = 1.0
    rhs = np.zeros((L, SEG_BLOCKS * L), np.float32)
    for hd in range(DT_ROWS):
        d, h = divmod(hd, HEADS_PER_GROUP)
        pair_block = DT_ROWS + d * (HEADS_PER_GROUP // 2) + h // 2
        lanes = slice(0, half) if h % 2 == 0 else slice(half, L)
        rhs[hd * K:hd * K + 3, hd * L:(hd + 1) * L] = 1.0
        rhs[hd * K:hd * K + 3, pair_block * L:(pair_block + 1) * L][:, lanes] = 1.0
    return place, lhs_ones, rhs


def _ssd_call(xbc, dtc, cw, cbias, alog, dtbias, dskip, batch, seq):
    nc = seq // SSM_CHUNK
    L = SSM_CHUNK
    assert nc >= 3 and seq >= CONV_WIN and nc * DT_ROWS == L
    sh = jnp.asarray(_conv_shift_matrices(), BF16)
    place, lhs_ones, rhs_const = _seg_matmul_constants(nc)
    place = jnp.asarray(place, BF16)
    lhs_ones = jnp.asarray(lhs_ones, F32)
    rhs_const = jnp.asarray(rhs_const, BF16)
    const = lambda a: pl.BlockSpec(a.shape, lambda b, g: (0,) * a.ndim)
    b_blk0 = SSM_INNER // SSM_STATE
    c_blk0 = b_blk0 + SSM_GROUPS

    def xbc_views(rows_, batched):
        lead = (lambda b: b) if batched else (lambda b: 0)
        return [pl.BlockSpec((rows_, GROUP_INNER), lambda b, g: (lead(b), g)),
                pl.BlockSpec((rows_, SSM_STATE), lambda b, g: (lead(b), b_blk0 + g)),
                pl.BlockSpec((rows_, SSM_STATE), lambda b, g: (lead(b), c_blk0 + g))]
    return pl.pallas_call(
        _ssd_kernel,
        grid=(batch, SSM_GROUPS),
        in_specs=xbc_views(seq, True) + [
                  pl.BlockSpec((nc, DT_ROWS, L), lambda b, g: (b, g, 0)),
                  const(sh), const(place), const(lhs_ones), const(rhs_const)]
                 + xbc_views(SUBLANES, False) + xbc_views(1, False) + [
                  pl.BlockSpec((DT_ROWS, L), lambda b, g: (g, 0)),
                  pl.BlockSpec((DT_ROWS, L), lambda b, g: (g, 0)),
                  pl.BlockSpec((1, GROUP_INNER), lambda b, g: (0, g))],
        out_specs=pl.BlockSpec((seq, GROUP_INNER), lambda b, g: (b, g)),
        out_shape=jax.ShapeDtypeStruct((batch * seq, SSM_INNER), F32),
        scratch_shapes=[pltpu.VMEM((nc, L, GROUP_COLS), F32),
                        pltpu.VMEM((nc, HEADS_PER_GROUP // 2, 2 * L, L), BF16),
                        pltpu.VMEM((nc, SSM_STATE, GROUP_INNER), F32),
                        pltpu.VMEM((nc, SSM_STATE, GROUP_INNER), F32),
                        pltpu.VMEM((nc, DT_ROWS, L), F32),
                        pltpu.VMEM((nc, DT_ROWS, L), F32),
                        pltpu.VMEM((nc, DT_ROWS, L), F32),
                        pltpu.VMEM((nc, L, L), BF16),
                        pltpu.VMEM((SSD_UNROLL, L, SEG_BLOCKS * L), BF16),
                        pltpu.VMEM((3, nc, DT_ROWS, L), F32)],
        compiler_params=pltpu.CompilerParams(dimension_semantics=("arbitrary", "arbitrary"),
                                             vmem_limit_bytes=VMEM_LIMIT),
        name="ssd",
    )(xbc, xbc, xbc, dtc, sh, place, lhs_ones, rhs_const, cw, cw, cw, cbias, cbias, cbias,
      alog, dtbias, dskip)


def _mlp_kernel(x_ref, attn_ref, ssm_ref, z_ref, gat_ref, gn_ref, wo_ref, gm_ref, wup_ref, wdn_ref, o_ref):
    a = attn_ref[...]
    an = (a * _inv_rms(a) * gat_ref[...]).astype(BF16)
    zc = z_ref[...].astype(F32)
    y = ssm_ref[...] * (zc * jax.nn.sigmoid(zc))
    yn = jnp.concatenate([y[:, g * GROUP_INNER:(g + 1) * GROUP_INNER]
                          * _inv_rms(y[:, g * GROUP_INNER:(g + 1) * GROUP_INNER]) for g in range(SSM_GROUPS)],
                         axis=1) * gn_ref[...]
    mix = _dot(an, wo_ref[0:ATTN_WIDTH, :]) + _dot(yn.astype(BF16), wo_ref[ATTN_WIDTH:D_MIX, :])
    x1 = x_ref[...] + mix
    hm = (x1 * _inv_rms(x1) * gm_ref[...]).astype(BF16)
    acc = jnp.zeros_like(x1)
    for c in range(D_FF // FF_CHUNK):
        cols = slice(c * FF_CHUNK, (c + 1) * FF_CHUNK)
        u = _dot(hm, wup_ref[:, cols])
        acc = acc + _dot(jnp.square(jnp.maximum(u, 0.0)).astype(BF16), wdn_ref[cols, :])
    o_ref[...] = x1 + acc


def _mlp_call(x2, attn, ssm, z, gat, gn, wo, gm, wup, wdn):
    t = x2.shape[0]
    row = lambda w: pl.BlockSpec((TM_MLP, w), lambda i: (i, 0))
    full = lambda a: pl.BlockSpec(a.shape, lambda i: (0, 0), pipeline_mode=pl.Buffered(1))
    return pl.pallas_call(
        _mlp_kernel,
        grid=(t // TM_MLP,),
        in_specs=[row(D_MODEL), row(ATTN_WIDTH), row(SSM_INNER), row(SSM_INNER), full(gat), full(gn), full(wo),
                  full(gm), full(wup), full(wdn)],
        out_specs=row(D_MODEL),
        out_shape=jax.ShapeDtypeStruct((t, D_MODEL), F32),
        compiler_params=pltpu.CompilerParams(dimension_semantics=("arbitrary",),
                                             vmem_limit_bytes=VMEM_LIMIT),
        name="outproj_mlp",
    )(x2, attn, ssm, z, gat, gn, wo, gm, wup, wdn)


def _group_major(fwd, bwd):
    lead = fwd.shape[:-1]
    both = jnp.stack([fwd.reshape(lead + (SSM_GROUPS, HEADS_PER_GROUP)),
                      bwd.reshape(lead + (SSM_GROUPS, HEADS_PER_GROUP))], axis=-2)
    return both.reshape(lead + (2 * SSM_HEADS,))


def _rotate_half(w):
    half = QK_ROPE_DIM // 2
    return jnp.concatenate([-w[..., half:], w[..., :half]], axis=-1)


def _swap_halves(g):
    half = QK_ROPE_DIM // 2
    return jnp.concatenate([g[..., half:], g[..., :half]], axis=-1)


def _lane_bcast(v):
    return jnp.broadcast_to(v[..., None], v.shape + (LANES,)).astype(F32)


def _layer(x2, cs, batch, seq, ln_mix_g, w_in, q_a_norm_g, w_uq, kv_a_norm_g, w_ukv, q_norm_g,
           k_norm_g, attn_out_norm_g, conv_w, conv_b, a_log_fwd, a_log_bwd, dt_bias_fwd, dt_bias_bwd,
           d_skip, ssm_norm_g, w_out, ln_mlp_g, w_mlp_up, w_mlp_down):
    win = _win_prep_call(*w_in)
    wq = w_uq.reshape(Q_LORA_RANK, ATTN_HEADS, QK_HEAD_DIM)
    wuq = jnp.concatenate([wq, _rotate_half(wq[..., QK_NOPE_DIM:])], axis=-1)
    wuq = wuq.reshape(Q_LORA_RANK, ATTN_HEADS * HEAD_PAD).astype(BF16)
    wukv = w_ukv.reshape(KV_LORA_RANK, ATTN_HEADS, QK_NOPE_DIM + V_HEAD_DIM)
    wuk = jnp.pad(wukv[:, :, :QK_NOPE_DIM],
                  ((0, 0), (0, 0), (0, HEAD_PAD - QK_NOPE_DIM))).reshape(KV_LORA_RANK, -1).astype(BF16)
    wv = wukv[:, :, QK_NOPE_DIM:].reshape(KV_LORA_RANK, ATTN_HEADS // 2, 2, V_HEAD_DIM)
    zv = jnp.zeros_like(wv[:, :, 0])
    wuv = jnp.stack([jnp.concatenate([wv[:, :, 0], zv], axis=-1),
                     jnp.concatenate([zv, wv[:, :, 1]], axis=-1)], axis=2).reshape(KV_LORA_RANK, -1).astype(BF16)
    ones_v = np.ones((ATTN_HEADS // 2, V_HEAD_DIM), np.float32)
    vone = jnp.asarray(np.stack([np.concatenate([0 * ones_v, ones_v], axis=-1),
                                 np.concatenate([ones_v, 0 * ones_v], axis=-1)], axis=1).reshape(1, -1))
    scale = QK_HEAD_DIM ** -0.5 * np.log2(np.e)
    gq = (jnp.concatenate([q_norm_g, _swap_halves(q_norm_g[QK_NOPE_DIM:])]) * scale)[None, :]
    zeros_nope = jnp.zeros((QK_NOPE_DIM,), k_norm_g.dtype)
    gkn = jnp.concatenate([k_norm_g[:QK_NOPE_DIM], zeros_nope])[None, :]
    gkr = jnp.concatenate([zeros_nope, k_norm_g[QK_NOPE_DIM:], _swap_halves(k_norm_g[QK_NOPE_DIM:])])[None, :]

    q, k, v, z, xbc, dtc = _inproj_call(
        x2, cs, ln_mix_g[None, :], win, q_a_norm_g[None, :], wuq, kv_a_norm_g[None, :], wuk, wuv, vone,
        gq, gkn, gkr)

    attn = _attn_call(q, k, v, batch, seq)

    cw = jnp.pad(conv_w[:, 0, :], ((0, SUBLANES - SSM_CONV), (0, 0)))
    cbias = conv_b[None, :]
    alog = _lane_bcast(_group_major(a_log_fwd, a_log_bwd))
    dtbias = _lane_bcast(_group_major(dt_bias_fwd, dt_bias_bwd))
    dskip = jnp.repeat(d_skip, SSM_HEAD_DIM)[None, :]
    ssm = _ssd_call(xbc, dtc, cw, cbias, alog, dtbias, dskip, batch, seq)

    return _mlp_call(x2, attn, ssm, z, attn_out_norm_g[None, :], ssm_norm_g[None, :], w_out.astype(BF16),
                     ln_mlp_g[None, :], w_mlp_up.astype(BF16), w_mlp_down.astype(BF16))


def _rope_table(positions):
    inv_freq = 1.0 / (ROPE_THETA ** (jnp.arange(0, QK_ROPE_DIM, 2, dtype=F32) / QK_ROPE_DIM))
    ang = inv_freq[:, None] * positions.astype(F32).reshape(1, -1)
    return jnp.concatenate([jnp.cos(ang), jnp.sin(ang)], axis=0)


def kernel(x, positions, ln_mix_g, w_in, q_a_norm_g, w_uq, kv_a_norm_g, w_ukv, q_norm_g, k_norm_g,
           attn_out_norm_g, conv_w, conv_b, a_log_fwd, a_log_bwd, dt_bias_fwd, dt_bias_bwd, d_skip,
           ssm_norm_g, w_out, ln_mlp_g, w_mlp_up, w_mlp_down):
    batch, seq, d = x.shape
    assert d == D_MODEL and seq % TQ == 0 and (batch * seq) % TM_IN == 0 and (batch * seq) % TM_MLP == 0
    cs = _rope_table(positions)
    x2 = x.reshape(batch * seq, d)
    for l in range(ln_mix_g.shape[0]):
        x2 = _layer(x2, cs, batch, seq, ln_mix_g[l], (w_in, l), q_a_norm_g[l], w_uq[l], kv_a_norm_g[l],
                    w_ukv[l], q_norm_g[l], k_norm_g[l], attn_out_norm_g[l], conv_w[l], conv_b[l],
                    a_log_fwd[l], a_log_bwd[l], dt_bias_fwd[l], dt_bias_bwd[l], d_skip[l], ssm_norm_g[l],
                    w_out[l], ln_mlp_g[l], w_mlp_up[l], w_mlp_down[l])
    return x2.reshape(batch, seq, d)
```

```python
import numpy as np
import jax
import jax.numpy as jnp
from jax import lax
from jax.experimental import pallas as pl
from jax.experimental.pallas import tpu as pltpu

F32 = jnp.float32
BF16 = jnp.bfloat16

D_MODEL = 1024
ATTN_HEADS = 8
QK_NOPE_DIM = 64
QK_ROPE_DIM = 32
QK_HEAD_DIM = QK_NOPE_DIM + QK_ROPE_DIM
V_HEAD_DIM = 64
Q_LORA_RANK = D_MODEL // 4
KV_LORA_RANK = D_MODEL // 8
ROPE_THETA = 10000.0
ATTN_WIDTH = ATTN_HEADS * V_HEAD_DIM
SSM_HEADS = 8
SSM_HEAD_DIM = 64
SSM_INNER = SSM_HEADS * SSM_HEAD_DIM
SSM_GROUPS = 2
SSM_STATE = 128
SSM_CONV = 5
SSM_CHUNK = 128
SSM_CONV_CH = SSM_INNER + 2 * SSM_GROUPS * SSM_STATE
D_MIX = ATTN_WIDTH + SSM_INNER
D_FF = 4 * D_MODEL
EPS = 1e-6

LANES = 128
SUBLANES = 8
HEAD_PAD = LANES

HEADS_PER_GROUP = SSM_HEADS // SSM_GROUPS
GROUP_INNER = SSM_INNER // SSM_GROUPS
GROUP_COLS = GROUP_INNER + 2 * SSM_STATE
DT_ROWS = 2 * HEADS_PER_GROUP
SEG_K_PER_ROW = 16
SEG_BLOCKS = DT_ROWS + HEADS_PER_GROUP
LOG2E = float(np.log2(np.e))
CONV_WIN = 2 * SSM_CHUNK
CONV_WIN_LEAD = SSM_CHUNK // 2
CONV_SHIFTED_TAPS = tuple(k for k in range(SSM_CONV) if k != SSM_CONV // 2)

COL_CKV = Q_LORA_RANK
COL_MISC = COL_CKV + KV_LORA_RANK
COL_Z = COL_MISC + LANES
COL_XBC = COL_Z + SSM_INNER
IN_COLS = COL_XBC + SSM_CONV_CH

TM_IN = 1024
IN_SUBTILE = 512
TQ = 512
ATTN_HEADS_PER_STEP = 8
TM_MLP = 512
FF_CHUNK = 1024
SSD_UNROLL = 4
PASS_A_UNROLL = 3
WIN_PREP_STEPS = 4
VMEM_LIMIT = 56 * 1024 * 1024


def _inv_rms(x):
    n = x.shape[-1]
    x2 = x * x
    acc = x2[:, 0:LANES]
    for i in range(1, n // LANES):
        acc = acc + x2[:, i * LANES:(i + 1) * LANES]
    return lax.rsqrt(jnp.sum(acc, axis=-1, keepdims=True) * (1.0 / n) + EPS)


def _dot(a, b):
    return jnp.dot(a, b, preferred_element_type=F32)


def _dot_nt(a, b):
    return lax.dot_general(a, b, (((1,), (1,)), ((), ())), preferred_element_type=F32)


def _inproj_kernel(x_ref, cs_ref, g_ref, win_ref, gqa_ref, wuq_ref, gkva_ref, wuk_ref, wuv_ref, vone_ref,
                   gq_ref, gkn_ref, gkr_ref,
                   q_ref, k_ref, v_ref, z_ref, xbc_ref, dt_ref):
    for sub in range(x_ref.shape[0] // IN_SUBTILE):
        _inproj_rows(sub * IN_SUBTILE, x_ref, cs_ref, g_ref, win_ref, gqa_ref, wuq_ref, gkva_ref, wuk_ref,
                     wuv_ref, vone_ref, gq_ref, gkn_ref, gkr_ref, q_ref, k_ref, v_ref, z_ref, xbc_ref, dt_ref)


def _inproj_rows(r0, x_ref, cs_ref, g_ref, win_ref, gqa_ref, wuq_ref, gkva_ref, wuk_ref, wuv_ref, vone_ref,
                 gq_ref, gkn_ref, gkr_ref, q_ref, k_ref, v_ref, z_ref, xbc_ref, dt_ref):
    tm = IN_SUBTILE
    rs = slice(r0, r0 + tm)
    x = x_ref[rs, :]
    h = (x * _inv_rms(x) * g_ref[...]).astype(BF16)
    big = _dot(h, win_ref[...])
    z_ref[rs, :] = big[:, COL_Z:COL_XBC].astype(BF16)
    xbc_ref[rs, :] = big[:, COL_XBC:IN_COLS].astype(BF16)
    misc = big[:, COL_MISC:COL_Z]
    for c in range(tm // SSM_CHUNK):
        dt_ref[r0 // SSM_CHUNK + c] = misc[c * SSM_CHUNK:(c + 1) * SSM_CHUNK, :].T[0:2 * SSM_HEADS, :]

    cq = big[:, 0:COL_CKV]
    ckv = big[:, COL_CKV:COL_MISC]
    cqn = (cq * _inv_rms(cq) * gqa_ref[...]).astype(BF16)
    ckvn = (ckv * _inv_rms(ckv) * gkva_ref[...]).astype(BF16)
    q_pre = _dot(cqn, wuq_ref[...])
    k_pre = _dot(ckvn, wuk_ref[...])
    v_ref[rs, :] = (_dot(ckvn, wuv_ref[...]) + vone_ref[...]).astype(BF16)

    half = QK_ROPE_DIM // 2
    zpad = jnp.zeros((HEAD_PAD - QK_ROPE_DIM, SSM_CHUNK), F32)
    tt = jnp.concatenate(
        [jnp.concatenate([cs_ref[:, r0 + c * SSM_CHUNK:r0 + (c + 1) * SSM_CHUNK], zpad], axis=0).T
         for c in range(tm // SSM_CHUNK)], axis=0)
    lane_t = lax.broadcasted_iota(jnp.int32, (tm, HEAD_PAD), 1)
    cs = jnp.where(lane_t < QK_NOPE_DIM, 1.0,
                   jnp.where(lane_t < QK_NOPE_DIM + half, pltpu.roll(tt, QK_NOPE_DIM, 1),
                             jnp.where(lane_t < QK_HEAD_DIM + half, pltpu.roll(tt, QK_NOPE_DIM + half, 1),
                                       pltpu.roll(tt, QK_HEAD_DIM, 1))))
    lane = lax.broadcasted_iota(jnp.int32, (1, HEAD_PAD), 1)
    in_head = (lane < QK_HEAD_DIM).astype(F32)
    is_rope = ((lane >= QK_NOPE_DIM) & (lane < QK_HEAD_DIM)).astype(F32)
    inv_d = 1.0 / QK_HEAD_DIM

    gcq = gq_ref[...] * cs
    for hh in range(ATTN_HEADS):
        sl = slice(hh * HEAD_PAD, (hh + 1) * HEAD_PAD)
        qh = q_pre[:, sl]
        ssq = jnp.sum(qh * qh * in_head, axis=-1, keepdims=True)
        q_ref[rs, sl] = (qh * lax.rsqrt(ssq * inv_d + EPS) * gcq).astype(BF16)

    ab = misc * (gkr_ref[...] * cs)
    lane2 = lax.broadcasted_iota(jnp.int32, (tm, HEAD_PAD), 1)
    swapped = jnp.where(lane2 < QK_HEAD_DIM, pltpu.roll(ab, HEAD_PAD - QK_ROPE_DIM, 1),
                        pltpu.roll(ab, QK_ROPE_DIM, 1))
    s_both = jnp.where(lane2 >= QK_NOPE_DIM, ab + swapped, 0.0)
    ssq_pe = jnp.sum(misc * misc * is_rope, axis=-1, keepdims=True)
    gkn = gkn_ref[...]
    for hh in range(ATTN_HEADS):
        sl = slice(hh * HEAD_PAD, (hh + 1) * HEAD_PAD)
        kh = k_pre[:, sl]
        ssq = jnp.sum(kh * kh, axis=-1, keepdims=True) + ssq_pe
        k_ref[rs, sl] = ((kh * gkn + s_both) * lax.rsqrt(ssq * inv_d + EPS)).astype(BF16)


def _win_prep_kernel(w_ref, o_ref):
    w = w_ref[...]
    o_kpe = Q_LORA_RANK + KV_LORA_RANK
    o_z = o_kpe + QK_ROPE_DIM
    o_dt = o_z + SSM_INNER + SSM_CONV_CH
    half = QK_ROPE_DIM // 2
    hpg = HEADS_PER_GROUP
    dt_f = w[:, o_dt:o_dt + SSM_HEADS]
    dt_b = w[:, o_dt + SSM_HEADS:o_dt + 2 * SSM_HEADS]
    kpe = w[:, o_kpe:o_z]
    pieces = [w[:, 0:o_kpe]]
    for g in range(SSM_GROUPS):
        pieces += [dt_f[:, g * hpg:(g + 1) * hpg], dt_b[:, g * hpg:(g + 1) * hpg]]
    pieces += [jnp.zeros((w.shape[0], QK_NOPE_DIM - 2 * SSM_HEADS), F32), kpe, -kpe[:, half:], kpe[:, :half],
               w[:, o_z:o_dt]]
    o_ref[...] = jnp.concatenate(pieces, axis=1).astype(BF16)


def _win_prep_call(w_in3, layer):
    depth, d, n = w_in3.shape
    rows = d // WIN_PREP_STEPS
    return pl.pallas_call(
        _win_prep_kernel,
        grid=(WIN_PREP_STEPS,),
        in_specs=[pl.BlockSpec((rows, n), lambda i: (layer * WIN_PREP_STEPS + i, 0))],
        out_specs=pl.BlockSpec((rows, IN_COLS), lambda i: (i, 0)),
        out_shape=jax.ShapeDtypeStruct((d, IN_COLS), BF16),
        compiler_params=pltpu.CompilerParams(dimension_semantics=("arbitrary",), vmem_limit_bytes=VMEM_LIMIT),
        name="win_prep",
    )(w_in3.reshape(depth * d, n))


def _inproj_call(x2, cs, g, win, gqa, wuq, gkva, wuk, wuv, vone, gq, gkn, gkr):
    t = x2.shape[0]
    cpt = TM_IN // SSM_CHUNK
    full = lambda a: pl.BlockSpec(a.shape, lambda i: (0,) * a.ndim, pipeline_mode=pl.Buffered(1))
    row = lambda w: pl.BlockSpec((TM_IN, w), lambda i: (i, 0))
    return pl.pallas_call(
        _inproj_kernel,
        grid=(t // TM_IN,),
        in_specs=[row(D_MODEL), pl.BlockSpec((QK_ROPE_DIM, TM_IN), lambda i: (0, i)), full(g), full(win), full(gqa), full(wuq), full(gkva),
                  full(wuk), full(wuv), full(vone), full(gq), full(gkn), full(gkr)],
        out_specs=[row(ATTN_HEADS * HEAD_PAD), row(ATTN_HEADS * HEAD_PAD), row(ATTN_HEADS * HEAD_PAD),
                   row(SSM_INNER), row(SSM_CONV_CH),
                   pl.BlockSpec((cpt, 2 * SSM_HEADS, SSM_CHUNK), lambda i: (i, 0, 0))],
        out_shape=[jax.ShapeDtypeStruct((t, ATTN_HEADS * HEAD_PAD), BF16),
                   jax.ShapeDtypeStruct((t, ATTN_HEADS * HEAD_PAD), BF16),
                   jax.ShapeDtypeStruct((t, ATTN_HEADS * HEAD_PAD), BF16),
                   jax.ShapeDtypeStruct((t, SSM_INNER), BF16),
                   jax.ShapeDtypeStruct((t, SSM_CONV_CH), BF16),
                   jax.ShapeDtypeStruct((t // SSM_CHUNK, 2 * SSM_HEADS, SSM_CHUNK), F32)],
        compiler_params=pltpu.CompilerParams(dimension_semantics=("arbitrary",),
                                             vmem_limit_bytes=VMEM_LIMIT),
        name="inproj",
    )(x2, cs, g, win, gqa, wuq, gkva, wuk, wuv, vone, gq, gkn, gkr)


def _attn_kernel(q_ref, k_ref, v_ref, o_ref):
    res = []
    for j in range(ATTN_HEADS_PER_STEP):
        sl = slice(j * HEAD_PAD, (j + 1) * HEAD_PAD)
        s = _dot_nt(q_ref[:, sl], k_ref[:, sl])
        m = jnp.max(s, axis=-1, keepdims=True)
        acc = _dot(jnp.exp2(s - m).astype(BF16), v_ref[:, sl])
        res.append(acc / pltpu.roll(acc, V_HEAD_DIM, 1))
    lane = lax.broadcasted_iota(jnp.int32, res[0].shape, 1)
    for jp in range(ATTN_HEADS_PER_STEP // 2):
        o_ref[:, jp * HEAD_PAD:(jp + 1) * HEAD_PAD] = jnp.where(lane < V_HEAD_DIM, res[2 * jp], res[2 * jp + 1])


def _attn_call(q, k, v, batch, seq):
    n_q = seq // TQ
    hps = ATTN_HEADS_PER_STEP
    return pl.pallas_call(
        _attn_kernel,
        grid=(batch, ATTN_HEADS // hps, n_q),
        in_specs=[pl.BlockSpec((TQ, hps * HEAD_PAD), lambda b, hp, i: (b * n_q + i, hp)),
                  pl.BlockSpec((seq, hps * HEAD_PAD), lambda b, hp, i: (b, hp)),
                  pl.BlockSpec((seq, hps * HEAD_PAD), lambda b, hp, i: (b, hp))],
        out_specs=pl.BlockSpec((TQ, hps * V_HEAD_DIM), lambda b, hp, i: (b * n_q + i, hp)),
        out_shape=jax.ShapeDtypeStruct((batch * seq, ATTN_WIDTH), F32),
        compiler_params=pltpu.CompilerParams(
            dimension_semantics=("arbitrary", "arbitrary", "arbitrary"),
            vmem_limit_bytes=VMEM_LIMIT),
        name="attention",
    )(q, k, v)


def _split3(x):
    hi = x.astype(BF16)
    r1 = x - hi.astype(F32)
    mid = r1.astype(BF16)
    lo = (r1 - mid.astype(F32)).astype(BF16)
    return hi, mid, lo


def _ssd_kernel(x_ref, b_ref, c_ref, dt_ref, sh_ref, place_ref, lhs_ones_ref, rhs_const_ref,
                cwx_ref, cwb_ref, cwc_ref, cbx_ref, cbb_ref, cbc_ref, alog_ref, dtbias_ref, dskip_ref,
                o_ref, xc_scr, xm_scr, nsf_scr, nsb_scr, dtv_scr, w_scr, dec_scr, lhs_scr, rhs_scr, rpart_scr):
    nc = dt_ref.shape[0]
    seq = x_ref.shape[0]
    L = SSM_CHUNK
    hpg = HEADS_PER_GROUP
    npair = hpg // 2
    P = SSM_HEAD_DIM
    x_cols = slice(0, GROUP_INNER)
    b_cols = slice(GROUP_INNER, GROUP_INNER + SSM_STATE)
    c_cols = slice(GROUP_INNER + SSM_STATE, GROUP_COLS)
    pair_cols = [slice(j * L, (j + 1) * L) for j in range(npair)]

    a_neg = -jnp.exp(alog_ref[...]) * LOG2E
    dtv = jax.nn.softplus(dt_ref[...] + dtbias_ref[...][None])
    dtv_scr[...] = dtv
    da2 = (dtv * a_neg[None]).reshape(nc * DT_ROWS, L)
    ri = lax.broadcasted_iota(jnp.int32, (L, L), 0)
    ci = lax.broadcasted_iota(jnp.int32, (L, L), 1)
    upper = (ri <= ci).astype(BF16)
    lower = (ri >= ci).astype(BF16)
    dparts = _split3(da2)
    cs_f = sum(_dot(p, upper) for p in dparts)
    cs_b = sum(_dot(p, lower) for p in dparts)
    rowsel = (ri & hpg) == 0
    cs2 = jnp.where(rowsel, cs_f, cs_b)
    col_f = sum(_dot_nt(lower, p) for p in dparts)
    col_b = sum(_dot_nt(upper, p) for p in dparts)
    colcs = jnp.where((ci & hpg) == 0, col_f, col_b)

    lhs_all = _dot(jnp.concatenate(_split3(colcs), axis=1), place_ref[...]) + lhs_ones_ref[...]
    for c in range(nc):
        lhs_scr[c] = lhs_all[:, c * L:(c + 1) * L].astype(BF16)
    for i, part in enumerate(_split3(-cs2)):
        rpart_scr[i] = part.astype(F32).reshape(nc, DT_ROWS, L)
    for u in range(SSD_UNROLL):
        rhs_scr[u] = rhs_const_ref[...]

    def lane_bcast(col):
        return jnp.broadcast_to(col, (col.shape[0], L))

    tot2 = jnp.where(rowsel, lane_bcast(cs2[:, L - 1:L]), lane_bcast(cs2[:, 0:1]))
    dec_scr[...] = jnp.exp2(tot2).reshape(nc, DT_ROWS, L)
    w_scr[...] = (dtv.reshape(nc * DT_ROWS, L) * jnp.exp2(tot2 - cs2)).reshape(nc, DT_ROWS, L)

    lane_t = lax.broadcasted_iota(jnp.int32, (L, L), 1)
    low_half = lane_t < P

    cw_halves = (cwx_ref[...], jnp.concatenate([cwb_ref[...], cwc_ref[...]], axis=1))
    cb_halves = (cbx_ref[...], jnp.concatenate([cbb_ref[...], cbc_ref[...]], axis=1))

    def rows(hf, start, size):
        if hf == 0:
            return x_ref[pl.ds(start, size), :]
        return jnp.concatenate([b_ref[pl.ds(start, size), :], c_ref[pl.ds(start, size), :]], axis=1)

    def conv_stage(c):
        base = pl.multiple_of(c * L, L)
        ws = pl.multiple_of(jnp.clip(c * L - CONV_WIN_LEAD, 0, seq - CONV_WIN), CONV_WIN_LEAD)
        variant = jnp.where(c == 0, 0, jnp.where(c == nc - 1, 2, 1))
        sh = sh_ref[variant]
        halves = []
        for hf in range(2):
            cw = cw_halves[hf]
            shifted = _dot(sh, rows(hf, ws, CONV_WIN))
            acc = cb_halves[hf] + cw[SSM_CONV // 2:SSM_CONV // 2 + 1, :] * rows(hf, base, L).astype(F32)
            for t, kk in enumerate(CONV_SHIFTED_TAPS):
                acc = acc + cw[kk:kk + 1, :] * shifted[t * L:(t + 1) * L, :]
            halves.append(acc * jax.nn.sigmoid(acc))
        xc = jnp.concatenate(halves, axis=1)
        xc_scr[c] = xc
        for j in range(npair):
            xp = xc[:, pair_cols[j]]
            xm_scr[c, j] = jnp.concatenate([jnp.where(low_half, xp, 0.0), jnp.where(low_half, 0.0, xp)],
                                           axis=0).astype(BF16)

    def state_loads(c):
        return xc_scr[c, :, b_cols], [xm_scr[c, j] for j in range(npair)]

    def state_stage(c, loaded):
        b_tok, xm = loaded
        bt = b_tok.T
        for d, ns_scr in ((0, nsf_scr), (1, nsb_scr)):
            w = w_scr[c, d * hpg:(d + 1) * hpg, :]
            ns_scr[c] = jnp.concatenate(
                [_dot(jnp.concatenate([(bt * w[2 * j:2 * j + 1, :]).astype(BF16),
                                       (bt * w[2 * j + 1:2 * j + 2, :]).astype(BF16)], axis=1), xm[j])
                 for j in range(npair)], axis=1)

    conv_stage(0)

    def pass_a(c, carry):
        loaded = state_loads(c)
        conv_stage(c + 1)
        state_stage(c, loaded)
        return carry
    lax.fori_loop(0, nc - 1, pass_a, 0, unroll=PASS_A_UNROLL)
    state_stage(nc - 1, state_loads(nc - 1))

    lane_r = lax.broadcasted_iota(jnp.int32, (1, L), 1)

    def decay_row(c, d):
        dec = dec_scr[c, d * hpg:(d + 1) * hpg, :]
        return jnp.concatenate([jnp.where(lane_r < P, dec[2 * j:2 * j + 1, :], dec[2 * j + 1:2 * j + 2, :])
                                for j in range(npair)], axis=1)

    def pass_b(d, ns_scr):
        def body(i, st):
            c = i if d == 0 else nc - 1 - i
            new = ns_scr[c]
            ns_scr[c] = st
            return st * decay_row(c, d) + new
        lax.fori_loop(0, nc, body, jnp.zeros((SSM_STATE, GROUP_INNER), F32))
    pass_b(0, nsf_scr)
    pass_b(1, nsb_scr)

    mask_f = ci <= ri
    mask_b = ci >= ri
    neg_inf = jnp.float32(-jnp.inf)
    sub16 = lax.broadcasted_iota(jnp.int32, (SEG_K_PER_ROW, L), 0)

    def pass_c(c, carry):
        base = pl.multiple_of(c * L, L)
        xc = xc_scr[c]
        bm = xc[:, b_cols].astype(BF16)
        cm = xc[:, c_cols].astype(BF16)
        cb = _dot_nt(cm, bm)
        off_f = _dot(cm, nsf_scr[c].astype(BF16))
        off_b = _dot(cm, nsb_scr[c].astype(BF16))
        slot = c % SSD_UNROLL
        for hd in range(DT_ROWS):
            blk = jnp.where(sub16 < 3, 1.0, 0.0)
            for i in range(3):
                blk = jnp.where(sub16 == 3 + i, rpart_scr[i, c, hd:hd + 1, :], blk)
            rhs_scr[slot, hd * SEG_K_PER_ROW:(hd + 1) * SEG_K_PER_ROW, hd * L:(hd + 1) * L] = blk.astype(BF16)
        seg = _dot(lhs_scr[c], rhs_scr[slot])
        dt_f_r = dtv_scr[c, 0:hpg, :]
        dt_b_r = dtv_scr[c, hpg:DT_ROWS, :]
        ys = []
        for j in range(npair):
            ws_ = []
            for h in (2 * j, 2 * j + 1):
                ef = jnp.exp2(jnp.where(mask_f, seg[:, h * L:(h + 1) * L], neg_inf))
                eb = jnp.exp2(jnp.where(mask_b, seg[:, (hpg + h) * L:(hpg + h + 1) * L], neg_inf))
                ws_.append((cb * (ef * dt_f_r[h:h + 1, :] + eb * dt_b_r[h:h + 1, :])).astype(BF16))
            scale_f = jnp.exp2(seg[:, (DT_ROWS + j) * L:(DT_ROWS + j + 1) * L])
            scale_b = jnp.exp2(seg[:, (DT_ROWS + npair + j) * L:(DT_ROWS + npair + j + 1) * L])
            ys.append(_dot(jnp.concatenate(ws_, axis=1), xm_scr[c, j])
                      + scale_f * off_f[:, pair_cols[j]] + scale_b * off_b[:, pair_cols[j]])
        y = jnp.concatenate(ys, axis=1) + dskip_ref[...] * xc[:, x_cols]
        o_ref[pl.ds(base, L), :] = y
        return carry
    lax.fori_loop(0, nc, pass_c, 0, unroll=SSD_UNROLL)


def _conv_shift_matrices():
    out = np.zeros((3, len(CONV_SHIFTED_TAPS) * SSM_CHUNK, CONV_WIN), np.float32)
    for v, lead in enumerate((0, CONV_WIN_LEAD, CONV_WIN - SSM_CHUNK)):
        for i, kk in enumerate(CONV_SHIFTED_TAPS):
            for t in range(SSM_CHUNK):
                j = t + kk - SSM_CONV // 2 + lead
                if 0 <= j < CONV_WIN:
                    out[v, i * SSM_CHUNK + t, j] = 1.0
    return out


def _seg_matmul_constants(nc):
    L, K, half = SSM_CHUNK, SEG_K_PER_ROW, SSM_HEAD_DIM
    place = np.zeros((3 * L, nc * L), np.float32)
    lhs_ones = np.zeros((1, nc * L), np.float32)
    for c in range(nc):
        for hd in range(DT_ROWS):
            for i in range(3):
                place[i * L + c * DT_ROWS + hd, c * L + hd * K + i] = 1.0
                lhs_ones[0, c * L + hd * K + 3 + i] = 1.0
    rhs = np.zeros((L, SEG_BLOCKS * L), np.float32)
    for hd in range(DT_ROWS):
        d, h = divmod(hd, HEADS_PER_GROUP)
        pair_block = DT_ROWS + d * (HEADS_PER_GROUP // 2) + h // 2
        lanes = slice(0, half) if h % 2 == 0 else slice(half, L)
        rhs[hd * K:hd * K + 3, hd * L:(hd + 1) * L] = 1.0
        rhs[hd * K:hd * K + 3, pair_block * L:(pair_block + 1) * L][:, lanes] = 1.0
    return place, lhs_ones, rhs


def _ssd_call(xbc, dtc, cw, cbias, alog, dtbias, dskip, batch, seq):
    nc = seq // SSM_CHUNK
    L = SSM_CHUNK
    assert nc >= 3 and seq >= CONV_WIN and nc * DT_ROWS == L
    sh = jnp.asarray(_conv_shift_matrices(), BF16)
    place, lhs_ones, rhs_const = _seg_matmul_constants(nc)
    place = jnp.asarray(place, BF16)
    lhs_ones = jnp.asarray(lhs_ones, F32)
    rhs_const = jnp.asarray(rhs_const, BF16)
    const = lambda a: pl.BlockSpec(a.shape, lambda b, g: (0,) * a.ndim)
    b_blk0 = SSM_INNER // SSM_STATE
    c_blk0 = b_blk0 + SSM_GROUPS

    def xbc_views(rows_, batched):
        lead = (lambda b: b) if batched else (lambda b: 0)
        return [pl.BlockSpec((rows_, GROUP_INNER), lambda b, g: (lead(b), g)),
                pl.BlockSpec((rows_, SSM_STATE), lambda b, g: (lead(b), b_blk0 + g)),
                pl.BlockSpec((rows_, SSM_STATE), lambda b, g: (lead(b), c_blk0 + g))]
    return pl.pallas_call(
        _ssd_kernel,
        grid=(batch, SSM_GROUPS),
        in_specs=xbc_views(seq, True) + [
                  pl.BlockSpec((nc, DT_ROWS, L), lambda b, g: (b, g, 0)),
                  const(sh), const(place), const(lhs_ones), const(rhs_const)]
                 + xbc_views(SUBLANES, False) + xbc_views(1, False) + [
                  pl.BlockSpec((DT_ROWS, L), lambda b, g: (g, 0)),
                  pl.BlockSpec((DT_ROWS, L), lambda b, g: (g, 0)),
                  pl.BlockSpec((1, GROUP_INNER), lambda b, g: (0, g))],
        out_specs=pl.BlockSpec((seq, GROUP_INNER), lambda b, g: (b, g)),
        out_shape=jax.ShapeDtypeStruct((batch * seq, SSM_INNER), F32),
        scratch_shapes=[pltpu.VMEM((nc, L, GROUP_COLS), F32),
                        pltpu.VMEM((nc, HEADS_PER_GROUP // 2, 2 * L, L), BF16),
                        pltpu.VMEM((nc, SSM_STATE, GROUP_INNER), F32),
                        pltpu.VMEM((nc, SSM_STATE, GROUP_INNER), F32),
                        pltpu.VMEM((nc, DT_ROWS, L), F32),
                        pltpu.VMEM((nc, DT_ROWS, L), F32),
                        pltpu.VMEM((nc, DT_ROWS, L), F32),
                        pltpu.VMEM((nc, L, L), BF16),
                        pltpu.VMEM((SSD_UNROLL, L, SEG_BLOCKS * L), BF16),
                        pltpu.VMEM((3, nc, DT_ROWS, L), F32)],
        compiler_params=pltpu.CompilerParams(dimension_semantics=("arbitrary", "arbitrary"),
                                             vmem_limit_bytes=VMEM_LIMIT),
        name="ssd",
    )(xbc, xbc, xbc, dtc, sh, place, lhs_ones, rhs_const, cw, cw, cw, cbias, cbias, cbias,
      alog, dtbias, dskip)


def _mlp_kernel(x_ref, attn_ref, ssm_ref, z_ref, gat_ref, gn_ref, wo_ref, gm_ref, wup_ref, wdn_ref, o_ref):
    a = attn_ref[...]
    an = (a * _inv_rms(a) * gat_ref[...]).astype(BF16)
    zc = z_ref[...].astype(F32)
    y = ssm_ref[...] * (zc * jax.nn.sigmoid(zc))
    yn = jnp.concatenate([y[:, g * GROUP_INNER:(g + 1) * GROUP_INNER]
                          * _inv_rms(y[:, g * GROUP_INNER:(g + 1) * GROUP_INNER]) for g in range(SSM_GROUPS)],
                         axis=1) * gn_ref[...]
    mix = _dot(an, wo_ref[0:ATTN_WIDTH, :]) + _dot(yn.astype(BF16), wo_ref[ATTN_WIDTH:D_MIX, :])
    x1 = x_ref[...] + mix
    hm = (x1 * _inv_rms(x1) * gm_ref[...]).astype(BF16)
    acc = jnp.zeros_like(x1)
    for c in range(D_FF // FF_CHUNK):
        cols = slice(c * FF_CHUNK, (c + 1) * FF_CHUNK)
        u = _dot(hm, wup_ref[:, cols])
        acc = acc + _dot(jnp.square(jnp.maximum(u, 0.0)).astype(BF16), wdn_ref[cols, :])
    o_ref[...] = x1 + acc


def _mlp_call(x2, attn, ssm, z, gat, gn, wo, gm, wup, wdn):
    t = x2.shape[0]
    row = lambda w: pl.BlockSpec((TM_MLP, w), lambda i: (i, 0))
    full = lambda a: pl.BlockSpec(a.shape, lambda i: (0, 0), pipeline_mode=pl.Buffered(1))
    return pl.pallas_call(
        _mlp_kernel,
        grid=(t // TM_MLP,),
        in_specs=[row(D_MODEL), row(ATTN_WIDTH), row(SSM_INNER), row(SSM_INNER), full(gat), full(gn), full(wo),
                  full(gm), full(wup), full(wdn)],
        out_specs=row(D_MODEL),
        out_shape=jax.ShapeDtypeStruct((t, D_MODEL), F32),
        compiler_params=pltpu.CompilerParams(dimension_semantics=("arbitrary",),
                                             vmem_limit_bytes=VMEM_LIMIT),
        name="outproj_mlp",
    )(x2, attn, ssm, z, gat, gn, wo, gm, wup, wdn)


def _group_major(fwd, bwd):
    lead = fwd.shape[:-1]
    both = jnp.stack([fwd.reshape(lead + (SSM_GROUPS, HEADS_PER_GROUP)),
                      bwd.reshape(lead + (SSM_GROUPS, HEADS_PER_GROUP))], axis=-2)
    return both.reshape(lead + (2 * SSM_HEADS,))


def _rotate_half(w):
    half = QK_ROPE_DIM // 2
    return jnp.concatenate([-w[..., half:], w[..., :half]], axis=-1)


def _swap_halves(g):
    half = QK_ROPE_DIM // 2
    return jnp.concatenate([g[..., half:], g[..., :half]], axis=-1)


def _lane_bcast(v):
    return jnp.broadcast_to(v[..., None], v.shape + (LANES,)).astype(F32)


def _layer(x2, cs, batch, seq, ln_mix_g, w_in, q_a_norm_g, w_uq, kv_a_norm_g, w_ukv, q_norm_g,
           k_norm_g, attn_out_norm_g, conv_w, conv_b, a_log_fwd, a_log_bwd, dt_bias_fwd, dt_bias_bwd,
           d_skip, ssm_norm_g, w_out, ln_mlp_g, w_mlp_up, w_mlp_down):
    win = _win_prep_call(*w_in)
    wq = w_uq.reshape(Q_LORA_RANK, ATTN_HEADS, QK_HEAD_DIM)
    wuq = jnp.concatenate([wq, _rotate_half(wq[..., QK_NOPE_DIM:])], axis=-1)
    wuq = wuq.reshape(Q_LORA_RANK, ATTN_HEADS * HEAD_PAD).astype(BF16)
    wukv = w_ukv.reshape(KV_LORA_RANK, ATTN_HEADS, QK_NOPE_DIM + V_HEAD_DIM)
    wuk = jnp.pad(wukv[:, :, :QK_NOPE_DIM],
                  ((0, 0), (0, 0), (0, HEAD_PAD - QK_NOPE_DIM))).reshape(KV_LORA_RANK, -1).astype(BF16)
    wv = wukv[:, :, QK_NOPE_DIM:].reshape(KV_LORA_RANK, ATTN_HEADS // 2, 2, V_HEAD_DIM)
    zv = jnp.zeros_like(wv[:, :, 0])
    wuv = jnp.stack([jnp.concatenate([wv[:, :, 0], zv], axis=-1),
                     jnp.concatenate([zv, wv[:, :, 1]], axis=-1)], axis=2).reshape(KV_LORA_RANK, -1).astype(BF16)
    ones_v = np.ones((ATTN_HEADS // 2, V_HEAD_DIM), np.float32)
    vone = jnp.asarray(np.stack([np.concatenate([0 * ones_v, ones_v], axis=-1),
                                 np.concatenate([ones_v, 0 * ones_v], axis=-1)], axis=1).reshape(1, -1))
    scale = QK_HEAD_DIM ** -0.5 * np.log2(np.e)
    gq = (jnp.concatenate([q_norm_g, _swap_halves(q_norm_g[QK_NOPE_DIM:])]) * scale)[None, :]
    zeros_nope = jnp.zeros((QK_NOPE_DIM,), k_norm_g.dtype)
    gkn = jnp.concatenate([k_norm_g[:QK_NOPE_DIM], zeros_nope])[None, :]
    gkr = jnp.concatenate([zeros_nope, k_norm_g[QK_NOPE_DIM:], _swap_halves(k_norm_g[QK_NOPE_DIM:])])[None, :]

    q, k, v, z, xbc, dtc = _inproj_call(
        x2, cs, ln_mix_g[None, :], win, q_a_norm_g[None, :], wuq, kv_a_norm_g[None, :], wuk, wuv, vone,
        gq, gkn, gkr)

    attn = _attn_call(q, k, v, batch, seq)

    cw = jnp.pad(conv_w[:, 0, :], ((0, SUBLANES - SSM_CONV), (0, 0)))
    cbias = conv_b[None, :]
    alog = _lane_bcast(_group_major(a_log_fwd, a_log_bwd))
    dtbias = _lane_bcast(_group_major(dt_bias_fwd, dt_bias_bwd))
    dskip = jnp.repeat(d_skip, SSM_HEAD_DIM)[None, :]
    ssm = _ssd_call(xbc, dtc, cw, cbias, alog, dtbias, dskip, batch, seq)

    return _mlp_call(x2, attn, ssm, z, attn_out_norm_g[None, :], ssm_norm_g[None, :], w_out.astype(BF16),
                     ln_mlp_g[None, :], w_mlp_up.astype(BF16), w_mlp_down.astype(BF16))


def _rope_table(positions):
    inv_freq = 1.0 / (ROPE_THETA ** (jnp.arange(0, QK_ROPE_DIM, 2, dtype=F32) / QK_ROPE_DIM))
    ang = inv_freq[:, None] * positions.astype(F32).reshape(1, -1)
    return jnp.concatenate([jnp.cos(ang), jnp.sin(ang)], axis=0)


def kernel(x, positions, ln_mix_g, w_in, q_a_norm_g, w_uq, kv_a_norm_g, w_ukv, q_norm_g, k_norm_g,
           attn_out_norm_g, conv_w, conv_b, a_log_fwd, a_log_bwd, dt_bias_fwd, dt_bias_bwd, d_skip,
           ssm_norm_g, w_out, ln_mlp_g, w_mlp_up, w_mlp_down):
    batch, seq, d = x.shape
    assert d == D_MODEL and seq % TQ == 0 and (batch * seq) % TM_IN == 0 and (batch * seq) % TM_MLP == 0
    cs = _rope_table(positions)
    x2 = x.reshape(batch * seq, d)
    for l in range(ln_mix_g.shape[0]):
        x2 = _layer(x2, cs, batch, seq, ln_mix_g[l], (w_in, l), q_a_norm_g[l], w_uq[l], kv_a_norm_g[l],
                    w_ukv[l], q_norm_g[l], k_norm_g[l], attn_out_norm_g[l], conv_w[l], conv_b[l],
                    a_log_fwd[l], a_log_bwd[l], dt_bias_fwd[l], dt_bias_bwd[l], d_skip[l], ssm_norm_g[l],
                    w_out[l], ln_mlp_g[l], w_mlp_up[l], w_mlp_down[l])
    return x2.reshape(batch, seq, d)
```

```python
import numpy as np
import jax
import jax.numpy as jnp
from jax import lax
from jax.experimental import pallas as pl
from jax.experimental.pallas import tpu as pltpu

F32 = jnp.float32
BF16 = jnp.bfloat16

D_MODEL = 1024
ATTN_HEADS = 8
QK_NOPE_DIM = 64
QK_ROPE_DIM = 32
QK_HEAD_DIM = QK_NOPE_DIM + QK_ROPE_DIM
V_HEAD_DIM = 64
Q_LORA_RANK = D_MODEL // 4
KV_LORA_RANK = D_MODEL // 8
ROPE_THETA = 10000.0
ATTN_WIDTH = ATTN_HEADS * V_HEAD_DIM
SSM_HEADS = 8
SSM_HEAD_DIM = 64
SSM_INNER = SSM_HEADS * SSM_HEAD_DIM
SSM_GROUPS = 2
SSM_STATE = 128
SSM_CONV = 5
SSM_CHUNK = 128
SSM_CONV_CH = SSM_INNER + 2 * SSM_GROUPS * SSM_STATE
D_MIX = ATTN_WIDTH + SSM_INNER
D_FF = 4 * D_MODEL
EPS = 1e-6

LANES = 128
SUBLANES = 8
HEAD_PAD = LANES

HEADS_PER_GROUP = SSM_HEADS // SSM_GROUPS
GROUP_INNER = SSM_INNER // SSM_GROUPS
GROUP_COLS = GROUP_INNER + 2 * SSM_STATE
DT_ROWS = 2 * HEADS_PER_GROUP
SEG_K_PER_ROW = 16
SEG_BLOCKS = DT_ROWS + HEADS_PER_GROUP
LOG2E = float(np.log2(np.e))
CONV_WIN = 2 * SSM_CHUNK
CONV_WIN_LEAD = SSM_CHUNK // 2
CONV_SHIFTED_TAPS = tuple(k for k in range(SSM_CONV) if k != SSM_CONV // 2)

COL_CKV = Q_LORA_RANK
COL_MISC = COL_CKV + KV_LORA_RANK
COL_Z = COL_MISC + LANES
COL_XBC = COL_Z + SSM_INNER
IN_COLS = COL_XBC + SSM_CONV_CH

TM_IN = 1024
IN_SUBTILE = 512
TQ = 512
ATTN_HEADS_PER_STEP = 8
TM_MLP = 512
FF_CHUNK = 1024
SSD_UNROLL = 8
PASS_A_UNROLL = 5
WIN_PREP_STEPS = 4
VMEM_LIMIT = 56 * 1024 * 1024


def _inv_rms(x):
    n = x.shape[-1]
    x2 = x * x
    acc = x2[:, 0:LANES]
    for i in range(1, n // LANES):
        acc = acc + x2[:, i * LANES:(i + 1) * LANES]
    return lax.rsqrt(jnp.sum(acc, axis=-1, keepdims=True) * (1.0 / n) + EPS)


def _dot(a, b):
    return jnp.dot(a, b, preferred_element_type=F32)


def _dot_nt(a, b):
    return lax.dot_general(a, b, (((1,), (1,)), ((), ())), preferred_element_type=F32)


def _inproj_kernel(x_ref, cs_ref, g_ref, win_ref, gqa_ref, wuq_ref, gkva_ref, wuk_ref, wuv_ref, vone_ref,
                   gq_ref, gkn_ref, gkr_ref,
                   q_ref, k_ref, v_ref, z_ref, xbc_ref, dt_ref):
    for sub in range(x_ref.shape[0] // IN_SUBTILE):
        _inproj_rows(sub * IN_SUBTILE, x_ref, cs_ref, g_ref, win_ref, gqa_ref, wuq_ref, gkva_ref, wuk_ref,
                     wuv_ref, vone_ref, gq_ref, gkn_ref, gkr_ref, q_ref, k_ref, v_ref, z_ref, xbc_ref, dt_ref)


def _inproj_rows(r0, x_ref, cs_ref, g_ref, win_ref, gqa_ref, wuq_ref, gkva_ref, wuk_ref, wuv_ref, vone_ref,
                 gq_ref, gkn_ref, gkr_ref, q_ref, k_ref, v_ref, z_ref, xbc_ref, dt_ref):
    tm = IN_SUBTILE
    rs = slice(r0, r0 + tm)
    x = x_ref[rs, :]
    h = (x * _inv_rms(x) * g_ref[...]).astype(BF16)
    big = _dot(h, win_ref[...])
    z_ref[rs, :] = big[:, COL_Z:COL_XBC].astype(BF16)
    xbc_ref[rs, :] = big[:, COL_XBC:IN_COLS].astype(BF16)
    misc = big[:, COL_MISC:COL_Z]
    for c in range(tm // SSM_CHUNK):
        dt_ref[r0 // SSM_CHUNK + c] = misc[c * SSM_CHUNK:(c + 1) * SSM_CHUNK, :].T[0:2 * SSM_HEADS, :]

    cq = big[:, 0:COL_CKV]
    ckv = big[:, COL_CKV:COL_MISC]
    cqn = (cq * _inv_rms(cq) * gqa_ref[...]).astype(BF16)
    ckvn = (ckv * _inv_rms(ckv) * gkva_ref[...]).astype(BF16)
    q_pre = _dot(cqn, wuq_ref[...])
    k_pre = _dot(ckvn, wuk_ref[...])
    v_ref[rs, :] = (_dot(ckvn, wuv_ref[...]) + vone_ref[...]).astype(BF16)

    half = QK_ROPE_DIM // 2
    zpad = jnp.zeros((HEAD_PAD - QK_ROPE_DIM, SSM_CHUNK), F32)
    tt = jnp.concatenate(
        [jnp.concatenate([cs_ref[:, r0 + c * SSM_CHUNK:r0 + (c + 1) * SSM_CHUNK], zpad], axis=0).T
         for c in range(tm // SSM_CHUNK)], axis=0)
    lane_t = lax.broadcasted_iota(jnp.int32, (tm, HEAD_PAD), 1)
    cs = jnp.where(lane_t < QK_NOPE_DIM, 1.0,
                   jnp.where(lane_t < QK_NOPE_DIM + half, pltpu.roll(tt, QK_NOPE_DIM, 1),
                             jnp.where(lane_t < QK_HEAD_DIM + half, pltpu.roll(tt, QK_NOPE_DIM + half, 1),
                                       pltpu.roll(tt, QK_HEAD_DIM, 1))))
    lane = lax.broadcasted_iota(jnp.int32, (1, HEAD_PAD), 1)
    in_head = (lane < QK_HEAD_DIM).astype(F32)
    is_rope = ((lane >= QK_NOPE_DIM) & (lane < QK_HEAD_DIM)).astype(F32)
    inv_d = 1.0 / QK_HEAD_DIM

    gcq = gq_ref[...] * cs
    for hh in range(ATTN_HEADS):
        sl = slice(hh * HEAD_PAD, (hh + 1) * HEAD_PAD)
        qh = q_pre[:, sl]
        ssq = jnp.sum(qh * qh * in_head, axis=-1, keepdims=True)
        q_ref[rs, sl] = (qh * lax.rsqrt(ssq * inv_d + EPS) * gcq).astype(BF16)

    ab = misc * (gkr_ref[...] * cs)
    lane2 = lax.broadcasted_iota(jnp.int32, (tm, HEAD_PAD), 1)
    swapped = jnp.where(lane2 < QK_HEAD_DIM, pltpu.roll(ab, HEAD_PAD - QK_ROPE_DIM, 1),
                        pltpu.roll(ab, QK_ROPE_DIM, 1))
    s_both = jnp.where(lane2 >= QK_NOPE_DIM, ab + swapped, 0.0)
    ssq_pe = jnp.sum(misc * misc * is_rope, axis=-1, keepdims=True)
    gkn = gkn_ref[...]
    for hh in range(ATTN_HEADS):
        sl = slice(hh * HEAD_PAD, (hh + 1) * HEAD_PAD)
        kh = k_pre[:, sl]
        ssq = jnp.sum(kh * kh, axis=-1, keepdims=True) + ssq_pe
        k_ref[rs, sl] = ((kh * gkn + s_both) * lax.rsqrt(ssq * inv_d + EPS)).astype(BF16)


def _win_prep_kernel(w_ref, o_ref):
    w = w_ref[...]
    o_kpe = Q_LORA_RANK + KV_LORA_RANK
    o_z = o_kpe + QK_ROPE_DIM
    o_dt = o_z + SSM_INNER + SSM_CONV_CH
    half = QK_ROPE_DIM // 2
    hpg = HEADS_PER_GROUP
    dt_f = w[:, o_dt:o_dt + SSM_HEADS]
    dt_b = w[:, o_dt + SSM_HEADS:o_dt + 2 * SSM_HEADS]
    kpe = w[:, o_kpe:o_z]
    pieces = [w[:, 0:o_kpe]]
    for g in range(SSM_GROUPS):
        pieces += [dt_f[:, g * hpg:(g + 1) * hpg], dt_b[:, g * hpg:(g + 1) * hpg]]
    pieces += [jnp.zeros((w.shape[0], QK_NOPE_DIM - 2 * SSM_HEADS), F32), kpe, -kpe[:, half:], kpe[:, :half],
               w[:, o_z:o_dt]]
    o_ref[...] = jnp.concatenate(pieces, axis=1).astype(BF16)


def _win_prep_call(w_in3, layer):
    depth, d, n = w_in3.shape
    rows = d // WIN_PREP_STEPS
    return pl.pallas_call(
        _win_prep_kernel,
        grid=(WIN_PREP_STEPS,),
        in_specs=[pl.BlockSpec((rows, n), lambda i: (layer * WIN_PREP_STEPS + i, 0))],
        out_specs=pl.BlockSpec((rows, IN_COLS), lambda i: (i, 0)),
        out_shape=jax.ShapeDtypeStruct((d, IN_COLS), BF16),
        compiler_params=pltpu.CompilerParams(dimension_semantics=("arbitrary",), vmem_limit_bytes=VMEM_LIMIT),
        name="win_prep",
    )(w_in3.reshape(depth * d, n))


def _inproj_call(x2, cs, g, win, gqa, wuq, gkva, wuk, wuv, vone, gq, gkn, gkr):
    t = x2.shape[0]
    cpt = TM_IN // SSM_CHUNK
    full = lambda a: pl.BlockSpec(a.shape, lambda i: (0,) * a.ndim, pipeline_mode=pl.Buffered(1))
    row = lambda w: pl.BlockSpec((TM_IN, w), lambda i: (i, 0))
    return pl.pallas_call(
        _inproj_kernel,
        grid=(t // TM_IN,),
        in_specs=[row(D_MODEL), pl.BlockSpec((QK_ROPE_DIM, TM_IN), lambda i: (0, i)), full(g), full(win), full(gqa), full(wuq), full(gkva),
                  full(wuk), full(wuv), full(vone), full(gq), full(gkn), full(gkr)],
        out_specs=[row(ATTN_HEADS * HEAD_PAD), row(ATTN_HEADS * HEAD_PAD), row(ATTN_HEADS * HEAD_PAD),
                   row(SSM_INNER), row(SSM_CONV_CH),
                   pl.BlockSpec((cpt, 2 * SSM_HEADS, SSM_CHUNK), lambda i: (i, 0, 0))],
        out_shape=[jax.ShapeDtypeStruct((t, ATTN_HEADS * HEAD_PAD), BF16),
                   jax.ShapeDtypeStruct((t, ATTN_HEADS * HEAD_PAD), BF16),
                   jax.ShapeDtypeStruct((t, ATTN_HEADS * HEAD_PAD), BF16),
                   jax.ShapeDtypeStruct((t, SSM_INNER), BF16),
                   jax.ShapeDtypeStruct((t, SSM_CONV_CH), BF16),
                   jax.ShapeDtypeStruct((t // SSM_CHUNK, 2 * SSM_HEADS, SSM_CHUNK), F32)],
        compiler_params=pltpu.CompilerParams(dimension_semantics=("arbitrary",),
                                             vmem_limit_bytes=VMEM_LIMIT),
        name="inproj",
    )(x2, cs, g, win, gqa, wuq, gkva, wuk, wuv, vone, gq, gkn, gkr)


def _attn_kernel(q_ref, k_ref, v_ref, o_ref):
    res = []
    for j in range(ATTN_HEADS_PER_STEP):
        sl = slice(j * HEAD_PAD, (j + 1) * HEAD_PAD)
        s = _dot_nt(q_ref[:, sl], k_ref[:, sl])
        m = jnp.max(s, axis=-1, keepdims=True)
        acc = _dot(jnp.exp2(s - m).astype(BF16), v_ref[:, sl])
        res.append(acc / pltpu.roll(acc, V_HEAD_DIM, 1))
    lane = lax.broadcasted_iota(jnp.int32, res[0].shape, 1)
    for jp in range(ATTN_HEADS_PER_STEP // 2):
        o_ref[:, jp * HEAD_PAD:(jp + 1) * HEAD_PAD] = jnp.where(lane < V_HEAD_DIM, res[2 * jp], res[2 * jp + 1])


def _attn_call(q, k, v, batch, seq):
    n_q = seq // TQ
    hps = ATTN_HEADS_PER_STEP
    return pl.pallas_call(
        _attn_kernel,
        grid=(batch, ATTN_HEADS // hps, n_q),
        in_specs=[pl.BlockSpec((TQ, hps * HEAD_PAD), lambda b, hp, i: (b * n_q + i, hp)),
                  pl.BlockSpec((seq, hps * HEAD_PAD), lambda b, hp, i: (b, hp)),
                  pl.BlockSpec((seq, hps * HEAD_PAD), lambda b, hp, i: (b, hp))],
        out_specs=pl.BlockSpec((TQ, hps * V_HEAD_DIM), lambda b, hp, i: (b * n_q + i, hp)),
        out_shape=jax.ShapeDtypeStruct((batch * seq, ATTN_WIDTH), F32),
        compiler_params=pltpu.CompilerParams(
            dimension_semantics=("arbitrary", "arbitrary", "arbitrary"),
            vmem_limit_bytes=VMEM_LIMIT),
        name="attention",
    )(q, k, v)


def _split3(x):
    hi = x.astype(BF16)
    r1 = x - hi.astype(F32)
    mid = r1.astype(BF16)
    lo = (r1 - mid.astype(F32)).astype(BF16)
    return hi, mid, lo


def _ssd_kernel(x_ref, b_ref, c_ref, dt_ref, sh_ref, place_ref, lhs_ones_ref, rhs_const_ref,
                cwx_ref, cwb_ref, cwc_ref, cbx_ref, cbb_ref, cbc_ref, alog_ref, dtbias_ref, dskip_ref,
                o_ref, xc_scr, xm_scr, nsf_scr, nsb_scr, dtv_scr, w_scr, dec_scr, lhs_scr, rhs_scr, rpart_scr):
    nc = dt_ref.shape[0]
    seq = x_ref.shape[0]
    L = SSM_CHUNK
    hpg = HEADS_PER_GROUP
    npair = hpg // 2
    P = SSM_HEAD_DIM
    x_cols = slice(0, GROUP_INNER)
    b_cols = slice(GROUP_INNER, GROUP_INNER + SSM_STATE)
    c_cols = slice(GROUP_INNER + SSM_STATE, GROUP_COLS)
    pair_cols = [slice(j * L, (j + 1) * L) for j in range(npair)]

    a_neg = -jnp.exp(alog_ref[...]) * LOG2E
    dtv = jax.nn.softplus(dt_ref[...] + dtbias_ref[...][None])
    dtv_scr[...] = dtv
    da2 = (dtv * a_neg[None]).reshape(nc * DT_ROWS, L)
    ri = lax.broadcasted_iota(jnp.int32, (L, L), 0)
    ci = lax.broadcasted_iota(jnp.int32, (L, L), 1)
    upper = (ri <= ci).astype(BF16)
    lower = (ri >= ci).astype(BF16)
    tri = jnp.concatenate([upper, lower], axis=1)
    cs_fb = sum(_dot(p, tri) for p in _split3(da2))
    rowsel = (ri & hpg) == 0
    cs2 = jnp.where(rowsel, cs_fb[:, 0:L], cs_fb[:, L:2 * L])
    colcs = cs2.T

    col_parts = jnp.concatenate(_split3(colcs), axis=1)
    half_n = place_ref.shape[1] // 2
    for hf in range(2):
        cols = slice(hf * half_n, (hf + 1) * half_n)
        lhs_half = _dot(col_parts, place_ref[:, cols]) + lhs_ones_ref[:, cols]
        for c in range(nc // 2):
            lhs_scr[hf * (nc // 2) + c] = lhs_half[:, c * L:(c + 1) * L].astype(BF16)
    for i, part in enumerate(_split3(-cs2)):
        rpart_scr[i] = part.astype(F32).reshape(nc, DT_ROWS, L)
    for u in range(SSD_UNROLL):
        rhs_scr[u] = rhs_const_ref[...]

    def lane_bcast(col):
        return jnp.broadcast_to(col, (col.shape[0], L))

    tot2 = jnp.where(rowsel, lane_bcast(cs2[:, L - 1:L]), lane_bcast(cs2[:, 0:1]))
    dec_scr[...] = jnp.exp2(tot2).reshape(nc, DT_ROWS, L)
    w_scr[...] = (dtv.reshape(nc * DT_ROWS, L) * jnp.exp2(tot2 - cs2)).reshape(nc, DT_ROWS, L)

    lane_t = lax.broadcasted_iota(jnp.int32, (L, L), 1)
    low_half = lane_t < P

    cw_halves = (cwx_ref[...], jnp.concatenate([cwb_ref[...], cwc_ref[...]], axis=1))
    cb_halves = (cbx_ref[...], jnp.concatenate([cbb_ref[...], cbc_ref[...]], axis=1))

    def rows(hf, start, size):
        if hf == 0:
            return x_ref[pl.ds(start, size), :]
        return jnp.concatenate([b_ref[pl.ds(start, size), :], c_ref[pl.ds(start, size), :]], axis=1)

    def conv_stage(c):
        base = pl.multiple_of(c * L, L)
        ws = pl.multiple_of(jnp.clip(c * L - CONV_WIN_LEAD, 0, seq - CONV_WIN), CONV_WIN_LEAD)
        variant = jnp.where(c == 0, 0, jnp.where(c == nc - 1, 2, 1))
        sh = sh_ref[variant]
        halves = []
        for hf in range(2):
            cw = cw_halves[hf]
            shifted = _dot(sh, rows(hf, ws, CONV_WIN))
            acc = cb_halves[hf] + cw[SSM_CONV // 2:SSM_CONV // 2 + 1, :] * rows(hf, base, L).astype(F32)
            for t, kk in enumerate(CONV_SHIFTED_TAPS):
                acc = acc + cw[kk:kk + 1, :] * shifted[t * L:(t + 1) * L, :]
            halves.append(acc * jax.nn.sigmoid(acc))
        xc = jnp.concatenate(halves, axis=1)
        xc_scr[c] = xc
        for j in range(npair):
            xp = xc[:, pair_cols[j]]
            xm_scr[c, j] = jnp.concatenate([jnp.where(low_half, xp, 0.0), jnp.where(low_half, 0.0, xp)],
                                           axis=0).astype(BF16)

    def state_loads(c):
        return xc_scr[c, :, b_cols], [xm_scr[c, j] for j in range(npair)]

    def state_stage(c, loaded):
        b_tok, xm = loaded
        bt = b_tok.T
        for d, ns_scr in ((0, nsf_scr), (1, nsb_scr)):
            w = w_scr[c, d * hpg:(d + 1) * hpg, :]
            ns_scr[c] = jnp.concatenate(
                [_dot(jnp.concatenate([(bt * w[2 * j:2 * j + 1, :]).astype(BF16),
                                       (bt * w[2 * j + 1:2 * j + 2, :]).astype(BF16)], axis=1), xm[j])
                 for j in range(npair)], axis=1)

    conv_stage(0)

    def pass_a(c, carry):
        loaded = state_loads(c)
        conv_stage(c + 1)
        state_stage(c, loaded)
        return carry
    lax.fori_loop(0, nc - 1, pass_a, 0, unroll=PASS_A_UNROLL)
    state_stage(nc - 1, state_loads(nc - 1))

    lane_r = lax.broadcasted_iota(jnp.int32, (1, L), 1)

    def decay_row(c, d):
        dec = dec_scr[c, d * hpg:(d + 1) * hpg, :]
        return jnp.concatenate([jnp.where(lane_r < P, dec[2 * j:2 * j + 1, :], dec[2 * j + 1:2 * j + 2, :])
                                for j in range(npair)], axis=1)

    def pass_b(d, ns_scr):
        def body(i, st):
            c = i if d == 0 else nc - 1 - i
            new = ns_scr[c]
            ns_scr[c] = st
            return st * decay_row(c, d) + new
        lax.fori_loop(0, nc, body, jnp.zeros((SSM_STATE, GROUP_INNER), F32))
    pass_b(0, nsf_scr)
    pass_b(1, nsb_scr)

    mask_f = ci <= ri
    mask_b = ci >= ri
    neg_inf = jnp.float32(-jnp.inf)
    sub16 = lax.broadcasted_iota(jnp.int32, (SEG_K_PER_ROW, L), 0)

    def pass_c(c, carry):
        base = pl.multiple_of(c * L, L)
        xc = xc_scr[c]
        bm = xc[:, b_cols].astype(BF16)
        cm = xc[:, c_cols].astype(BF16)
        cb = _dot_nt(cm, bm)
        off_f = _dot(cm, nsf_scr[c].astype(BF16))
        off_b = _dot(cm, nsb_scr[c].astype(BF16))
        slot = c % SSD_UNROLL
        for hd in range(DT_ROWS):
            blk = jnp.where(sub16 < 3, 1.0, 0.0)
            for i in range(3):
                blk = jnp.where(sub16 == 3 + i, rpart_scr[i, c, hd:hd + 1, :], blk)
            rhs_scr[slot, hd * SEG_K_PER_ROW:(hd + 1) * SEG_K_PER_ROW, hd * L:(hd + 1) * L] = blk.astype(BF16)
        seg = _dot(lhs_scr[c], rhs_scr[slot])
        dt_f_r = dtv_scr[c, 0:hpg, :]
        dt_b_r = dtv_scr[c, hpg:DT_ROWS, :]
        ys = []
        for j in range(npair):
            ws_ = []
            for h in (2 * j, 2 * j + 1):
                ef = jnp.exp2(jnp.where(mask_f, seg[:, h * L:(h + 1) * L], neg_inf))
                eb = jnp.exp2(jnp.where(mask_b, seg[:, (hpg + h) * L:(hpg + h + 1) * L], neg_inf))
                ws_.append((cb * (ef * dt_f_r[h:h + 1, :] + eb * dt_b_r[h:h + 1, :])).astype(BF16))
            scale_f = jnp.exp2(seg[:, (DT_ROWS + j) * L:(DT_ROWS + j + 1) * L])
            scale_b = jnp.exp2(seg[:, (DT_ROWS + npair + j) * L:(DT_ROWS + npair + j + 1) * L])
            ys.append(_dot(jnp.concatenate(ws_, axis=1), xm_scr[c, j])
                      + scale_f * off_f[:, pair_cols[j]] + scale_b * off_b[:, pair_cols[j]])
        y = jnp.concatenate(ys, axis=1) + dskip_ref[...] * xc[:, x_cols]
        o_ref[pl.ds(base, L), :] = y
        return carry
    lax.fori_loop(0, nc, pass_c, 0, unroll=SSD_UNROLL)


def _conv_shift_matrices():
    out = np.zeros((3, len(CONV_SHIFTED_TAPS) * SSM_CHUNK, CONV_WIN), np.float32)
    for v, lead in enumerate((0, CONV_WIN_LEAD, CONV_WIN - SSM_CHUNK)):
        for i, kk in enumerate(CONV_SHIFTED_TAPS):
            for t in range(SSM_CHUNK):
                j = t + kk - SSM_CONV // 2 + lead
                if 0 <= j < CONV_WIN:
                    out[v, i * SSM_CHUNK + t, j] = 1.0
    return out


def _seg_matmul_constants(nc):
    L, K, half = SSM_CHUNK, SEG_K_PER_ROW, SSM_HEAD_DIM
    place = np.zeros((3 * L, nc * L), np.float32)
    lhs_ones = np.zeros((1, nc * L), np.float32)
    for c in range(nc):
        for hd in range(DT_ROWS):
            for i in range(3):
                place[i * L + c * DT_ROWS + hd, c * L + hd * K + i] = 1.0
                lhs_ones[0, c * L + hd * K + 3 + i] = 1.0
    rhs = np.zeros((L, SEG_BLOCKS * L), np.float32)
    for hd in range(DT_ROWS):
        d, h = divmod(hd, HEADS_PER_GROUP)
        pair_block = DT_ROWS + d * (HEADS_PER_GROUP // 2) + h // 2
        lanes = slice(0, half) if h % 2 == 0 else slice(half, L)
        rhs[hd * K:hd * K + 3, hd * L:(hd + 1) * L] = 1.0
        rhs[hd * K:hd * K + 3, pair_block * L:(pair_block + 1) * L][:, lanes] = 1.0
    return place, lhs_ones, rhs


def _ssd_call(xbc, dtc, cw, cbias, alog, dtbias, dskip, batch, seq):
    nc = seq // SSM_CHUNK
    L = SSM_CHUNK
    assert nc >= 3 and seq >= CONV_WIN and nc * DT_ROWS == L
    sh = jnp.asarray(_conv_shift_matrices(), BF16)
    place, lhs_ones, rhs_const = _seg_matmul_constants(nc)
    place = jnp.asarray(place, BF16)
    lhs_ones = jnp.asarray(lhs_ones, F32)
    rhs_const = jnp.asarray(rhs_const, BF16)
    const = lambda a: pl.BlockSpec(a.shape, lambda b, g: (0,) * a.ndim)
    b_blk0 = SSM_INNER // SSM_STATE
    c_blk0 = b_blk0 + SSM_GROUPS

    def xbc_views(rows_, batched):
        lead = (lambda b: b) if batched else (lambda b: 0)
        return [pl.BlockSpec((rows_, GROUP_INNER), lambda b, g: (lead(b), g)),
                pl.BlockSpec((rows_, SSM_STATE), lambda b, g: (lead(b), b_blk0 + g)),
                pl.BlockSpec((rows_, SSM_STATE), lambda b, g: (lead(b), c_blk0 + g))]
    return pl.pallas_call(
        _ssd_kernel,
        grid=(batch, SSM_GROUPS),
        in_specs=xbc_views(seq, True) + [
                  pl.BlockSpec((nc, DT_ROWS, L), lambda b, g: (b, g, 0)),
                  const(sh), const(place), const(lhs_ones), const(rhs_const)]
                 + xbc_views(SUBLANES, False) + xbc_views(1, False) + [
                  pl.BlockSpec((DT_ROWS, L), lambda b, g: (g, 0)),
                  pl.BlockSpec((DT_ROWS, L), lambda b, g: (g, 0)),
                  pl.BlockSpec((1, GROUP_INNER), lambda b, g: (0, g))],
        out_specs=pl.BlockSpec((seq, GROUP_INNER), lambda b, g: (b, g)),
        out_shape=jax.ShapeDtypeStruct((batch * seq, SSM_INNER), F32),
        scratch_shapes=[pltpu.VMEM((nc, L, GROUP_COLS), F32),
                        pltpu.VMEM((nc, HEADS_PER_GROUP // 2, 2 * L, L), BF16),
                        pltpu.VMEM((nc, SSM_STATE, GROUP_INNER), F32),
                        pltpu.VMEM((nc, SSM_STATE, GROUP_INNER), F32),
                        pltpu.VMEM((nc, DT_ROWS, L), F32),
                        pltpu.VMEM((nc, DT_ROWS, L), F32),
                        pltpu.VMEM((nc, DT_ROWS, L), F32),
                        pltpu.VMEM((nc, L, L), BF16),
                        pltpu.VMEM((SSD_UNROLL, L, SEG_BLOCKS * L), BF16),
                        pltpu.VMEM((3, nc, DT_ROWS, L), F32)],
        compiler_params=pltpu.CompilerParams(dimension_semantics=("arbitrary", "arbitrary"),
                                             vmem_limit_bytes=VMEM_LIMIT),
        name="ssd",
    )(xbc, xbc, xbc, dtc, sh, place, lhs_ones, rhs_const, cw, cw, cw, cbias, cbias, cbias,
      alog, dtbias, dskip)


def _mlp_kernel(x_ref, attn_ref, ssm_ref, z_ref, gat_ref, gn_ref, wo_ref, gm_ref, wup_ref, wdn_ref, o_ref):
    a = attn_ref[...]
    an = (a * _inv_rms(a) * gat_ref[...]).astype(BF16)
    zc = z_ref[...].astype(F32)
    y = ssm_ref[...] * (zc * jax.nn.sigmoid(zc))
    yn = jnp.concatenate([y[:, g * GROUP_INNER:(g + 1) * GROUP_INNER]
                          * _inv_rms(y[:, g * GROUP_INNER:(g + 1) * GROUP_INNER]) for g in range(SSM_GROUPS)],
                         axis=1) * gn_ref[...]
    mix = _dot(an, wo_ref[0:ATTN_WIDTH, :]) + _dot(yn.astype(BF16), wo_ref[ATTN_WIDTH:D_MIX, :])
    x1 = x_ref[...] + mix
    hm = (x1 * _inv_rms(x1) * gm_ref[...]).astype(BF16)
    acc = jnp.zeros_like(x1)
    for c in range(D_FF // FF_CHUNK):
        cols = slice(c * FF_CHUNK, (c + 1) * FF_CHUNK)
        u = _dot(hm, wup_ref[:, cols])
        acc = acc + _dot(jnp.square(jnp.maximum(u, 0.0)).astype(BF16), wdn_ref[cols, :])
    o_ref[...] = x1 + acc


def _mlp_call(x2, attn, ssm, z, gat, gn, wo, gm, wup, wdn):
    t = x2.shape[0]
    row = lambda w: pl.BlockSpec((TM_MLP, w), lambda i: (i, 0))
    full = lambda a: pl.BlockSpec(a.shape, lambda i: (0, 0), pipeline_mode=pl.Buffered(1))
    return pl.pallas_call(
        _mlp_kernel,
        grid=(t // TM_MLP,),
        in_specs=[row(D_MODEL), row(ATTN_WIDTH), row(SSM_INNER), row(SSM_INNER), full(gat), full(gn), full(wo),
                  full(gm), full(wup), full(wdn)],
        out_specs=row(D_MODEL),
        out_shape=jax.ShapeDtypeStruct((t, D_MODEL), F32),
        compiler_params=pltpu.CompilerParams(dimension_semantics=("arbitrary",),
                                             vmem_limit_bytes=VMEM_LIMIT),
        name="outproj_mlp",
    )(x2, attn, ssm, z, gat, gn, wo, gm, wup, wdn)


def _group_major(fwd, bwd):
    lead = fwd.shape[:-1]
    both = jnp.stack([fwd.reshape(lead + (SSM_GROUPS, HEADS_PER_GROUP)),
                      bwd.reshape(lead + (SSM_GROUPS, HEADS_PER_GROUP))], axis=-2)
    return both.reshape(lead + (2 * SSM_HEADS,))


def _rotate_half(w):
    half = QK_ROPE_DIM // 2
    return jnp.concatenate([-w[..., half:], w[..., :half]], axis=-1)


def _swap_halves(g):
    half = QK_ROPE_DIM // 2
    return jnp.concatenate([g[..., half:], g[..., :half]], axis=-1)


def _lane_bcast(v):
    return jnp.broadcast_to(v[..., None], v.shape + (LANES,)).astype(F32)


def _layer(x2, cs, batch, seq, ln_mix_g, w_in, q_a_norm_g, w_uq, kv_a_norm_g, w_ukv, q_norm_g,
           k_norm_g, attn_out_norm_g, conv_w, conv_b, a_log_fwd, a_log_bwd, dt_bias_fwd, dt_bias_bwd,
           d_skip, ssm_norm_g, w_out, ln_mlp_g, w_mlp_up, w_mlp_down):
    win = _win_prep_call(*w_in)
    wq = w_uq.reshape(Q_LORA_RANK, ATTN_HEADS, QK_HEAD_DIM)
    wuq = jnp.concatenate([wq, _rotate_half(wq[..., QK_NOPE_DIM:])], axis=-1)
    wuq = wuq.reshape(Q_LORA_RANK, ATTN_HEADS * HEAD_PAD).astype(BF16)
    wukv = w_ukv.reshape(KV_LORA_RANK, ATTN_HEADS, QK_NOPE_DIM + V_HEAD_DIM)
    wuk = jnp.pad(wukv[:, :, :QK_NOPE_DIM],
                  ((0, 0), (0, 0), (0, HEAD_PAD - QK_NOPE_DIM))).reshape(KV_LORA_RANK, -1).astype(BF16)
    wv = wukv[:, :, QK_NOPE_DIM:].reshape(KV_LORA_RANK, ATTN_HEADS // 2, 2, V_HEAD_DIM)
    zv = jnp.zeros_like(wv[:, :, 0])
    wuv = jnp.stack([jnp.concatenate([wv[:, :, 0], zv], axis=-1),
                     jnp.concatenate([zv, wv[:, :, 1]], axis=-1)], axis=2).reshape(KV_LORA_RANK, -1).astype(BF16)
    ones_v = np.ones((ATTN_HEADS // 2, V_HEAD_DIM), np.float32)
    vone = jnp.asarray(np.stack([np.concatenate([0 * ones_v, ones_v], axis=-1),
                                 np.concatenate([ones_v, 0 * ones_v], axis=-1)], axis=1).reshape(1, -1))
    scale = QK_HEAD_DIM ** -0.5 * np.log2(np.e)
    gq = (jnp.concatenate([q_norm_g, _swap_halves(q_norm_g[QK_NOPE_DIM:])]) * scale)[None, :]
    zeros_nope = jnp.zeros((QK_NOPE_DIM,), k_norm_g.dtype)
    gkn = jnp.concatenate([k_norm_g[:QK_NOPE_DIM], zeros_nope])[None, :]
    gkr = jnp.concatenate([zeros_nope, k_norm_g[QK_NOPE_DIM:], _swap_halves(k_norm_g[QK_NOPE_DIM:])])[None, :]

    q, k, v, z, xbc, dtc = _inproj_call(
        x2, cs, ln_mix_g[None, :], win, q_a_norm_g[None, :], wuq, kv_a_norm_g[None, :], wuk, wuv, vone,
        gq, gkn, gkr)

    attn = _attn_call(q, k, v, batch, seq)

    cw = jnp.pad(conv_w[:, 0, :], ((0, SUBLANES - SSM_CONV), (0, 0)))
    cbias = conv_b[None, :]
    alog = _lane_bcast(_group_major(a_log_fwd, a_log_bwd))
    dtbias = _lane_bcast(_group_major(dt_bias_fwd, dt_bias_bwd))
    dskip = jnp.repeat(d_skip, SSM_HEAD_DIM)[None, :]
    ssm = _ssd_call(xbc, dtc, cw, cbias, alog, dtbias, dskip, batch, seq)

    return _mlp_call(x2, attn, ssm, z, attn_out_norm_g[None, :], ssm_norm_g[None, :], w_out.astype(BF16),
                     ln_mlp_g[None, :], w_mlp_up.astype(BF16), w_mlp_down.astype(BF16))


def _rope_table(positions):
    inv_freq = 1.0 / (ROPE_THETA ** (jnp.arange(0, QK_ROPE_DIM, 2, dtype=F32) / QK_ROPE_DIM))
    ang = inv_freq[:, None] * positions.astype(F32).reshape(1, -1)
    return jnp.concatenate([jnp.cos(ang), jnp.sin(ang)], axis=0)


def kernel(x, positions, ln_mix_g, w_in, q_a_norm_g, w_uq, kv_a_norm_g, w_ukv, q_norm_g, k_norm_g,
           attn_out_norm_g, conv_w, conv_b, a_log_fwd, a_log_bwd, dt_bias_fwd, dt_bias_bwd, d_skip,
           ssm_norm_g, w_out, ln_mlp_g, w_mlp_up, w_mlp_down):
    batch, seq, d = x.shape
    assert d == D_MODEL and seq % TQ == 0 and (batch * seq) % TM_IN == 0 and (batch * seq) % TM_MLP == 0
    cs = _rope_table(positions)
    x2 = x.reshape(batch * seq, d)
    for l in range(ln_mix_g.shape[0]):
        x2 = _layer(x2, cs, batch, seq, ln_mix_g[l], (w_in, l), q_a_norm_g[l], w_uq[l], kv_a_norm_g[l],
                    w_ukv[l], q_norm_g[l], k_norm_g[l], attn_out_norm_g[l], conv_w[l], conv_b[l],
                    a_log_fwd[l], a_log_bwd[l], dt_bias_fwd[l], dt_bias_bwd[l], d_skip[l], ssm_norm_g[l],
                    w_out[l], ln_mlp_g[l], w_mlp_up[l], w_mlp_down[l])
    return x2.reshape(batch, seq, d)
```

```python
import numpy as np
import jax
import jax.numpy as jnp
from jax import lax
from jax.experimental import pallas as pl
from jax.experimental.pallas import tpu as pltpu

F32 = jnp.float32
BF16 = jnp.bfloat16

D_MODEL = 1024
ATTN_HEADS = 8
QK_NOPE_DIM = 64
QK_ROPE_DIM = 32
QK_HEAD_DIM = QK_NOPE_DIM + QK_ROPE_DIM
V_HEAD_DIM = 64
Q_LORA_RANK = D_MODEL // 4
KV_LORA_RANK = D_MODEL // 8
ROPE_THETA = 10000.0
ATTN_WIDTH = ATTN_HEADS * V_HEAD_DIM
SSM_HEADS = 8
SSM_HEAD_DIM = 64
SSM_INNER = SSM_HEADS * SSM_HEAD_DIM
SSM_GROUPS = 2
SSM_STATE = 128
SSM_CONV = 5
SSM_CHUNK = 128
SSM_CONV_CH = SSM_INNER + 2 * SSM_GROUPS * SSM_STATE
D_MIX = ATTN_WIDTH + SSM_INNER
D_FF = 4 * D_MODEL
EPS = 1e-6

LANES = 128
SUBLANES = 8
HEAD_PAD = LANES

HEADS_PER_GROUP = SSM_HEADS // SSM_GROUPS
GROUP_INNER = SSM_INNER // SSM_GROUPS
GROUP_COLS = GROUP_INNER + 2 * SSM_STATE
DT_ROWS = 2 * HEADS_PER_GROUP
SEG_K_PER_ROW = 16
SEG_BLOCKS = DT_ROWS + HEADS_PER_GROUP
LOG2E = float(np.log2(np.e))
CONV_WIN = 2 * SSM_CHUNK
CONV_WIN_LEAD = SSM_CHUNK // 2
CONV_SHIFTED_TAPS = tuple(k for k in range(SSM_CONV) if k != SSM_CONV // 2)

COL_CKV = Q_LORA_RANK
COL_MISC = COL_CKV + KV_LORA_RANK
COL_Z = COL_MISC + LANES
COL_XBC = COL_Z + SSM_INNER
IN_COLS = COL_XBC + SSM_CONV_CH

TM_IN = 1024
IN_SUBTILE = 512
TQ = 512
ATTN_HEADS_PER_STEP = 8
TM_MLP = 512
FF_CHUNK = 1024
SSD_UNROLL = 8
PASS_A_UNROLL = 5
WIN_PREP_STEPS = 4
VMEM_LIMIT = 56 * 1024 * 1024


def _inv_rms(x):
    n = x.shape[-1]
    x2 = x * x
    acc = x2[:, 0:LANES]
    for i in range(1, n // LANES):
        acc = acc + x2[:, i * LANES:(i + 1) * LANES]
    return lax.rsqrt(jnp.sum(acc, axis=-1, keepdims=True) * (1.0 / n) + EPS)


def _dot(a, b):
    return jnp.dot(a, b, preferred_element_type=F32)


def _dot_nt(a, b):
    return lax.dot_general(a, b, (((1,), (1,)), ((), ())), preferred_element_type=F32)


def _inproj_kernel(x_ref, cs_ref, g_ref, win_ref, gqa_ref, wuq_ref, gkva_ref, wuk_ref, wuv_ref, vone_ref,
                   gq_ref, gkn_ref, gkr_ref,
                   q_ref, k_ref, v_ref, z_ref, xbc_ref, dt_ref):
    for sub in range(x_ref.shape[0] // IN_SUBTILE):
        _inproj_rows(sub * IN_SUBTILE, x_ref, cs_ref, g_ref, win_ref, gqa_ref, wuq_ref, gkva_ref, wuk_ref,
                     wuv_ref, vone_ref, gq_ref, gkn_ref, gkr_ref, q_ref, k_ref, v_ref, z_ref, xbc_ref, dt_ref)


def _inproj_rows(r0, x_ref, cs_ref, g_ref, win_ref, gqa_ref, wuq_ref, gkva_ref, wuk_ref, wuv_ref, vone_ref,
                 gq_ref, gkn_ref, gkr_ref, q_ref, k_ref, v_ref, z_ref, xbc_ref, dt_ref):
    tm = IN_SUBTILE
    rs = slice(r0, r0 + tm)
    x = x_ref[rs, :]
    h = (x * _inv_rms(x) * g_ref[...]).astype(BF16)
    big = _dot(h, win_ref[...])
    z_ref[rs, :] = big[:, COL_Z:COL_XBC].astype(BF16)
    xbc_ref[rs, :] = big[:, COL_XBC:IN_COLS].astype(BF16)
    misc = big[:, COL_MISC:COL_Z]
    for c in range(tm // SSM_CHUNK):
        dt_ref[r0 // SSM_CHUNK + c] = misc[c * SSM_CHUNK:(c + 1) * SSM_CHUNK, :].T[0:2 * SSM_HEADS, :]

    cq = big[:, 0:COL_CKV]
    ckv = big[:, COL_CKV:COL_MISC]
    cqn = (cq * _inv_rms(cq) * gqa_ref[...]).astype(BF16)
    ckvn = (ckv * _inv_rms(ckv) * gkva_ref[...]).astype(BF16)
    q_pre = _dot(cqn, wuq_ref[...])
    k_pre = _dot(ckvn, wuk_ref[...])
    v_ref[rs, :] = (_dot(ckvn, wuv_ref[...]) + vone_ref[...]).astype(BF16)

    half = QK_ROPE_DIM // 2
    zpad = jnp.zeros((HEAD_PAD - QK_ROPE_DIM, SSM_CHUNK), F32)
    tt = jnp.concatenate(
        [jnp.concatenate([cs_ref[:, r0 + c * SSM_CHUNK:r0 + (c + 1) * SSM_CHUNK], zpad], axis=0).T
         for c in range(tm // SSM_CHUNK)], axis=0)
    lane_t = lax.broadcasted_iota(jnp.int32, (tm, HEAD_PAD), 1)
    cs = jnp.where(lane_t < QK_NOPE_DIM, 1.0,
                   jnp.where(lane_t < QK_NOPE_DIM + half, pltpu.roll(tt, QK_NOPE_DIM, 1),
                             jnp.where(lane_t < QK_HEAD_DIM + half, pltpu.roll(tt, QK_NOPE_DIM + half, 1),
                                       pltpu.roll(tt, QK_HEAD_DIM, 1))))
    lane = lax.broadcasted_iota(jnp.int32, (1, HEAD_PAD), 1)
    in_head = (lane < QK_HEAD_DIM).astype(F32)
    is_rope = ((lane >= QK_NOPE_DIM) & (lane < QK_HEAD_DIM)).astype(F32)
    inv_d = 1.0 / QK_HEAD_DIM

    gcq = gq_ref[...] * cs
    for hh in range(ATTN_HEADS):
        sl = slice(hh * HEAD_PAD, (hh + 1) * HEAD_PAD)
        qh = q_pre[:, sl]
        ssq = jnp.sum(qh * qh * in_head, axis=-1, keepdims=True)
        q_ref[rs, sl] = (qh * lax.rsqrt(ssq * inv_d + EPS) * gcq).astype(BF16)

    ab = misc * (gkr_ref[...] * cs)
    lane2 = lax.broadcasted_iota(jnp.int32, (tm, HEAD_PAD), 1)
    swapped = jnp.where(lane2 < QK_HEAD_DIM, pltpu.roll(ab, HEAD_PAD - QK_ROPE_DIM, 1),
                        pltpu.roll(ab, QK_ROPE_DIM, 1))
    s_both = jnp.where(lane2 >= QK_NOPE_DIM, ab + swapped, 0.0)
    ssq_pe = jnp.sum(misc * misc * is_rope, axis=-1, keepdims=True)
    gkn = gkn_ref[...]
    for hh in range(ATTN_HEADS):
        sl = slice(hh * HEAD_PAD, (hh + 1) * HEAD_PAD)
        kh = k_pre[:, sl]
        ssq = jnp.sum(kh * kh, axis=-1, keepdims=True) + ssq_pe
        k_ref[rs, sl] = ((kh * gkn + s_both) * lax.rsqrt(ssq * inv_d + EPS)).astype(BF16)


def _win_prep_kernel(w_ref, o_ref):
    w = w_ref[...]
    o_kpe = Q_LORA_RANK + KV_LORA_RANK
    o_z = o_kpe + QK_ROPE_DIM
    o_dt = o_z + SSM_INNER + SSM_CONV_CH
    half = QK_ROPE_DIM // 2
    hpg = HEADS_PER_GROUP
    dt_f = w[:, o_dt:o_dt + SSM_HEADS]
    dt_b = w[:, o_dt + SSM_HEADS:o_dt + 2 * SSM_HEADS]
    kpe = w[:, o_kpe:o_z]
    pieces = [w[:, 0:o_kpe]]
    for g in range(SSM_GROUPS):
        pieces += [dt_f[:, g * hpg:(g + 1) * hpg], dt_b[:, g * hpg:(g + 1) * hpg]]
    pieces += [jnp.zeros((w.shape[0], QK_NOPE_DIM - 2 * SSM_HEADS), F32), kpe, -kpe[:, half:], kpe[:, :half],
               w[:, o_z:o_dt]]
    o_ref[...] = jnp.concatenate(pieces, axis=1).astype(BF16)


def _win_prep_call(w_in3, layer):
    depth, d, n = w_in3.shape
    rows = d // WIN_PREP_STEPS
    return pl.pallas_call(
        _win_prep_kernel,
        grid=(WIN_PREP_STEPS,),
        in_specs=[pl.BlockSpec((rows, n), lambda i: (layer * WIN_PREP_STEPS + i, 0))],
        out_specs=pl.BlockSpec((rows, IN_COLS), lambda i: (i, 0)),
        out_shape=jax.ShapeDtypeStruct((d, IN_COLS), BF16),
        compiler_params=pltpu.CompilerParams(dimension_semantics=("arbitrary",), vmem_limit_bytes=VMEM_LIMIT),
        name="win_prep",
    )(w_in3.reshape(depth * d, n))


def _inproj_call(x2, cs, g, win, gqa, wuq, gkva, wuk, wuv, vone, gq, gkn, gkr):
    t = x2.shape[0]
    cpt = TM_IN // SSM_CHUNK
    full = lambda a: pl.BlockSpec(a.shape, lambda i: (0,) * a.ndim, pipeline_mode=pl.Buffered(1))
    row = lambda w: pl.BlockSpec((TM_IN, w), lambda i: (i, 0))
    return pl.pallas_call(
        _inproj_kernel,
        grid=(t // TM_IN,),
        in_specs=[row(D_MODEL), pl.BlockSpec((QK_ROPE_DIM, TM_IN), lambda i: (0, i)), full(g), full(win), full(gqa), full(wuq), full(gkva),
                  full(wuk), full(wuv), full(vone), full(gq), full(gkn), full(gkr)],
        out_specs=[row(ATTN_HEADS * HEAD_PAD), row(ATTN_HEADS * HEAD_PAD), row(ATTN_HEADS * HEAD_PAD),
                   row(SSM_INNER), row(SSM_CONV_CH),
                   pl.BlockSpec((cpt, 2 * SSM_HEADS, SSM_CHUNK), lambda i: (i, 0, 0))],
        out_shape=[jax.ShapeDtypeStruct((t, ATTN_HEADS * HEAD_PAD), BF16),
                   jax.ShapeDtypeStruct((t, ATTN_HEADS * HEAD_PAD), BF16),
                   jax.ShapeDtypeStruct((t, ATTN_HEADS * HEAD_PAD), BF16),
                   jax.ShapeDtypeStruct((t, SSM_INNER), BF16),
                   jax.ShapeDtypeStruct((t, SSM_CONV_CH), BF16),
                   jax.ShapeDtypeStruct((t // SSM_CHUNK, 2 * SSM_HEADS, SSM_CHUNK), F32)],
        compiler_params=pltpu.CompilerParams(dimension_semantics=("arbitrary",),
                                             vmem_limit_bytes=VMEM_LIMIT),
        name="inproj",
    )(x2, cs, g, win, gqa, wuq, gkva, wuk, wuv, vone, gq, gkn, gkr)


def _attn_kernel(q_ref, k_ref, v_ref, o_ref):
    lane = lax.broadcasted_iota(jnp.int32, (q_ref.shape[0], HEAD_PAD), 1)
    for jp in range(ATTN_HEADS_PER_STEP // 2):
        sls = [slice(j * HEAD_PAD, (j + 1) * HEAD_PAD) for j in (2 * jp, 2 * jp + 1)]
        ss = [_dot_nt(q_ref[:, sl], k_ref[:, sl]) for sl in sls]
        ps = [jnp.exp2(s - jnp.max(s, axis=-1, keepdims=True)).astype(BF16) for s in ss]
        accs = [_dot(p, v_ref[:, sl]) for p, sl in zip(ps, sls)]
        res = [acc / pltpu.roll(acc, V_HEAD_DIM, 1) for acc in accs]
        o_ref[:, jp * HEAD_PAD:(jp + 1) * HEAD_PAD] = jnp.where(lane < V_HEAD_DIM, res[0], res[1])


def _attn_call(q, k, v, batch, seq):
    n_q = seq // TQ
    hps = ATTN_HEADS_PER_STEP
    return pl.pallas_call(
        _attn_kernel,
        grid=(batch, ATTN_HEADS // hps, n_q),
        in_specs=[pl.BlockSpec((TQ, hps * HEAD_PAD), lambda b, hp, i: (b * n_q + i, hp)),
                  pl.BlockSpec((seq, hps * HEAD_PAD), lambda b, hp, i: (b, hp)),
                  pl.BlockSpec((seq, hps * HEAD_PAD), lambda b, hp, i: (b, hp))],
        out_specs=pl.BlockSpec((TQ, hps * V_HEAD_DIM), lambda b, hp, i: (b * n_q + i, hp)),
        out_shape=jax.ShapeDtypeStruct((batch * seq, ATTN_WIDTH), F32),
        compiler_params=pltpu.CompilerParams(
            dimension_semantics=("arbitrary", "arbitrary", "arbitrary"),
            vmem_limit_bytes=VMEM_LIMIT),
        name="attention",
    )(q, k, v)


def _split3(x):
    hi = x.astype(BF16)
    r1 = x - hi.astype(F32)
    mid = r1.astype(BF16)
    lo = (r1 - mid.astype(F32)).astype(BF16)
    return hi, mid, lo


def _ssd_kernel(x_ref, b_ref, c_ref, dt_ref, sh_ref, place_ref, lhs_ones_ref, rhs_const_ref,
                cwx_ref, cwb_ref, cwc_ref, cbx_ref, cbb_ref, cbc_ref, alog_ref, dtbias_ref, dskip_ref,
                o_ref, xc_scr, xm_scr, nsf_scr, nsb_scr, dtv_scr, w_scr, dec_scr, lhs_scr, rhs_scr, rpart_scr):
    nc = dt_ref.shape[0]
    seq = x_ref.shape[0]
    L = SSM_CHUNK
    hpg = HEADS_PER_GROUP
    npair = hpg // 2
    P = SSM_HEAD_DIM
    x_cols = slice(0, GROUP_INNER)
    b_cols = slice(GROUP_INNER, GROUP_INNER + SSM_STATE)
    c_cols = slice(GROUP_INNER + SSM_STATE, GROUP_COLS)
    pair_cols = [slice(j * L, (j + 1) * L) for j in range(npair)]

    a_neg = -jnp.exp(alog_ref[...]) * LOG2E
    dtv = jax.nn.softplus(dt_ref[...] + dtbias_ref[...][None])
    dtv_scr[...] = dtv
    da2 = (dtv * a_neg[None]).reshape(nc * DT_ROWS, L)
    ri = lax.broadcasted_iota(jnp.int32, (L, L), 0)
    ci = lax.broadcasted_iota(jnp.int32, (L, L), 1)
    upper = (ri <= ci).astype(BF16)
    lower = (ri >= ci).astype(BF16)
    tri = jnp.concatenate([upper, lower], axis=1)
    cs_fb = sum(_dot(p, tri) for p in _split3(da2))
    rowsel = (ri & hpg) == 0
    cs2 = jnp.where(rowsel, cs_fb[:, 0:L], cs_fb[:, L:2 * L])
    colcs = cs2.T

    col_parts = jnp.concatenate(_split3(colcs), axis=1)
    half_n = place_ref.shape[1] // 2
    for hf in range(2):
        cols = slice(hf * half_n, (hf + 1) * half_n)
        lhs_half = _dot(col_parts, place_ref[:, cols]) + lhs_ones_ref[:, cols]
        for c in range(nc // 2):
            lhs_scr[hf * (nc // 2) + c] = lhs_half[:, c * L:(c + 1) * L].astype(BF16)
    for i, part in enumerate(_split3(-cs2)):
        rpart_scr[i] = part.astype(F32).reshape(nc, DT_ROWS, L)
    for u in range(SSD_UNROLL):
        rhs_scr[u] = rhs_const_ref[...]

    def lane_bcast(col):
        return jnp.broadcast_to(col, (col.shape[0], L))

    tot2 = jnp.where(rowsel, lane_bcast(cs2[:, L - 1:L]), lane_bcast(cs2[:, 0:1]))
    dec_scr[...] = jnp.exp2(tot2).reshape(nc, DT_ROWS, L)
    w_scr[...] = (dtv.reshape(nc * DT_ROWS, L) * jnp.exp2(tot2 - cs2)).reshape(nc, DT_ROWS, L)

    lane_t = lax.broadcasted_iota(jnp.int32, (L, L), 1)
    low_half = lane_t < P

    cw_halves = (cwx_ref[...], jnp.concatenate([cwb_ref[...], cwc_ref[...]], axis=1))
    cb_halves = (cbx_ref[...], jnp.concatenate([cbb_ref[...], cbc_ref[...]], axis=1))

    def rows(hf, start, size):
        if hf == 0:
            return x_ref[pl.ds(start, size), :]
        return jnp.concatenate([b_ref[pl.ds(start, size), :], c_ref[pl.ds(start, size), :]], axis=1)

    def conv_stage(c):
        base = pl.multiple_of(c * L, L)
        ws = pl.multiple_of(jnp.clip(c * L - CONV_WIN_LEAD, 0, seq - CONV_WIN), CONV_WIN_LEAD)
        variant = jnp.where(c == 0, 0, jnp.where(c == nc - 1, 2, 1))
        sh = sh_ref[variant]
        halves = []
        for hf in range(2):
            cw = cw_halves[hf]
            shifted = _dot(sh, rows(hf, ws, CONV_WIN))
            acc = cb_halves[hf] + cw[SSM_CONV // 2:SSM_CONV // 2 + 1, :] * rows(hf, base, L).astype(F32)
            for t, kk in enumerate(CONV_SHIFTED_TAPS):
                acc = acc + cw[kk:kk + 1, :] * shifted[t * L:(t + 1) * L, :]
            halves.append(acc * jax.nn.sigmoid(acc))
        xc = jnp.concatenate(halves, axis=1)
        xc_scr[c] = xc
        for j in range(npair):
            xp = xc[:, pair_cols[j]]
            xm_scr[c, j] = jnp.concatenate([jnp.where(low_half, xp, 0.0), jnp.where(low_half, 0.0, xp)],
                                           axis=0).astype(BF16)

    def state_loads(c):
        return xc_scr[c, :, b_cols], [xm_scr[c, j] for j in range(npair)]

    def state_stage(c, loaded):
        b_tok, xm = loaded
        bt = b_tok.T
        for d, ns_scr in ((0, nsf_scr), (1, nsb_scr)):
            w = w_scr[c, d * hpg:(d + 1) * hpg, :]
            ns_scr[c] = jnp.concatenate(
                [_dot(jnp.concatenate([(bt * w[2 * j:2 * j + 1, :]).astype(BF16),
                                       (bt * w[2 * j + 1:2 * j + 2, :]).astype(BF16)], axis=1), xm[j])
                 for j in range(npair)], axis=1)

    conv_stage(0)

    def pass_a(c, carry):
        loaded = state_loads(c)
        conv_stage(c + 1)
        state_stage(c, loaded)
        return carry
    lax.fori_loop(0, nc - 1, pass_a, 0, unroll=PASS_A_UNROLL)
    state_stage(nc - 1, state_loads(nc - 1))

    lane_r = lax.broadcasted_iota(jnp.int32, (1, L), 1)

    def decay_row(c, d):
        dec = dec_scr[c, d * hpg:(d + 1) * hpg, :]
        return jnp.concatenate([jnp.where(lane_r < P, dec[2 * j:2 * j + 1, :], dec[2 * j + 1:2 * j + 2, :])
                                for j in range(npair)], axis=1)

    def pass_b(d, ns_scr):
        def body(i, st):
            c = i if d == 0 else nc - 1 - i
            new = ns_scr[c]
            ns_scr[c] = st
            return st * decay_row(c, d) + new
        lax.fori_loop(0, nc, body, jnp.zeros((SSM_STATE, GROUP_INNER), F32))
    pass_b(0, nsf_scr)
    pass_b(1, nsb_scr)

    mask_f = ci <= ri
    mask_b = ci >= ri
    neg_inf = jnp.float32(-jnp.inf)
    sub16 = lax.broadcasted_iota(jnp.int32, (SEG_K_PER_ROW, L), 0)

    def pass_c(c, carry):
        base = pl.multiple_of(c * L, L)
        xc = xc_scr[c]
        bm = xc[:, b_cols].astype(BF16)
        cm = xc[:, c_cols].astype(BF16)
        cb = _dot_nt(cm, bm)
        off_f = _dot(cm, nsf_scr[c].astype(BF16))
        off_b = _dot(cm, nsb_scr[c].astype(BF16))
        slot = c % SSD_UNROLL
        for hd in range(DT_ROWS):
            blk = jnp.where(sub16 < 3, 1.0, 0.0)
            for i in range(3):
                blk = jnp.where(sub16 == 3 + i, rpart_scr[i, c, hd:hd + 1, :], blk)
            rhs_scr[slot, hd * SEG_K_PER_ROW:(hd + 1) * SEG_K_PER_ROW, hd * L:(hd + 1) * L] = blk.astype(BF16)
        seg = _dot(lhs_scr[c], rhs_scr[slot])
        dt_f_r = dtv_scr[c, 0:hpg, :]
        dt_b_r = dtv_scr[c, hpg:DT_ROWS, :]
        ys = []
        for j in range(npair):
            ws_ = []
            for h in (2 * j, 2 * j + 1):
                ef = jnp.exp2(jnp.where(mask_f, seg[:, h * L:(h + 1) * L], neg_inf))
                eb = jnp.exp2(jnp.where(mask_b, seg[:, (hpg + h) * L:(hpg + h + 1) * L], neg_inf))
                ws_.append((cb * (ef * dt_f_r[h:h + 1, :] + eb * dt_b_r[h:h + 1, :])).astype(BF16))
            scale_f = jnp.exp2(seg[:, (DT_ROWS + j) * L:(DT_ROWS + j + 1) * L])
            scale_b = jnp.exp2(seg[:, (DT_ROWS + npair + j) * L:(DT_ROWS + npair + j + 1) * L])
            ys.append(_dot(jnp.concatenate(ws_, axis=1), xm_scr[c, j])
                      + scale_f * off_f[:, pair_cols[j]] + scale_b * off_b[:, pair_cols[j]])
        y = jnp.concatenate(ys, axis=1) + dskip_ref[...] * xc[:, x_cols]
        o_ref[pl.ds(base, L), :] = y
        return carry
    lax.fori_loop(0, nc, pass_c, 0, unroll=SSD_UNROLL)


def _conv_shift_matrices():
    out = np.zeros((3, len(CONV_SHIFTED_TAPS) * SSM_CHUNK, CONV_WIN), np.float32)
    for v, lead in enumerate((0, CONV_WIN_LEAD, CONV_WIN - SSM_CHUNK)):
        for i, kk in enumerate(CONV_SHIFTED_TAPS):
            for t in range(SSM_CHUNK):
                j = t + kk - SSM_CONV // 2 + lead
                if 0 <= j < CONV_WIN:
                    out[v, i * SSM_CHUNK + t, j] = 1.0
    return out


def _seg_matmul_constants(nc):
    L, K, half = SSM_CHUNK, SEG_K_PER_ROW, SSM_HEAD_DIM
    place = np.zeros((3 * L, nc * L), np.float32)
    lhs_ones = np.zeros((1, nc * L), np.float32)
    for c in range(nc):
        for hd in range(DT_ROWS):
            for i in range(3):
                place[i * L + c * DT_ROWS + hd, c * L + hd * K + i] = 1.0
                lhs_ones[0, c * L + hd * K + 3 + i] = 1.0
    rhs = np.zeros((L, SEG_BLOCKS * L), np.float32)
    for hd in range(DT_ROWS):
        d, h = divmod(hd, HEADS_PER_GROUP)
        pair_block = DT_ROWS + d * (HEADS_PER_GROUP // 2) + h // 2
        lanes = slice(0, half) if h % 2 == 0 else slice(half, L)
        rhs[hd * K:hd * K + 3, hd * L:(hd + 1) * L] = 1.0
        rhs[hd * K:hd * K + 3, pair_block * L:(pair_block + 1) * L][:, lanes] = 1.0
    return place, lhs_ones, rhs


def _ssd_call(xbc, dtc, cw, cbias, alog, dtbias, dskip, batch, seq):
    nc = seq // SSM_CHUNK
    L = SSM_CHUNK
    assert nc >= 3 and seq >= CONV_WIN and nc * DT_ROWS == L
    sh = jnp.asarray(_conv_shift_matrices(), BF16)
    place, lhs_ones, rhs_const = _seg_matmul_constants(nc)
    place = jnp.asarray(place, BF16)
    lhs_ones = jnp.asarray(lhs_ones, F32)
    rhs_const = jnp.asarray(rhs_const, BF16)
    const = lambda a: pl.BlockSpec(a.shape, lambda b, g: (0,) * a.ndim)
    b_blk0 = SSM_INNER // SSM_STATE
    c_blk0 = b_blk0 + SSM_GROUPS

    def xbc_views(rows_, batched):
        lead = (lambda b: b) if batched else (lambda b: 0)
        return [pl.BlockSpec((rows_, GROUP_INNER), lambda b, g: (lead(b), g)),
                pl.BlockSpec((rows_, SSM_STATE), lambda b, g: (lead(b), b_blk0 + g)),
                pl.BlockSpec((rows_, SSM_STATE), lambda b, g: (lead(b), c_blk0 + g))]
    return pl.pallas_call(
        _ssd_kernel,
        grid=(batch, SSM_GROUPS),
        in_specs=xbc_views(seq, True) + [
                  pl.BlockSpec((nc, DT_ROWS, L), lambda b, g: (b, g, 0)),
                  const(sh), const(place), const(lhs_ones), const(rhs_const)]
                 + xbc_views(SUBLANES, False) + xbc_views(1, False) + [
                  pl.BlockSpec((DT_ROWS, L), lambda b, g: (g, 0)),
                  pl.BlockSpec((DT_ROWS, L), lambda b, g: (g, 0)),
                  pl.BlockSpec((1, GROUP_INNER), lambda b, g: (0, g))],
        out_specs=pl.BlockSpec((seq, GROUP_INNER), lambda b, g: (b, g)),
        out_shape=jax.ShapeDtypeStruct((batch * seq, SSM_INNER), F32),
        scratch_shapes=[pltpu.VMEM((nc, L, GROUP_COLS), F32),
                        pltpu.VMEM((nc, HEADS_PER_GROUP // 2, 2 * L, L), BF16),
                        pltpu.VMEM((nc, SSM_STATE, GROUP_INNER), F32),
                        pltpu.VMEM((nc, SSM_STATE, GROUP_INNER), F32),
                        pltpu.VMEM((nc, DT_ROWS, L), F32),
                        pltpu.VMEM((nc, DT_ROWS, L), F32),
                        pltpu.VMEM((nc, DT_ROWS, L), F32),
                        pltpu.VMEM((nc, L, L), BF16),
                        pltpu.VMEM((SSD_UNROLL, L, SEG_BLOCKS * L), BF16),
                        pltpu.VMEM((3, nc, DT_ROWS, L), F32)],
        compiler_params=pltpu.CompilerParams(dimension_semantics=("arbitrary", "arbitrary"),
                                             vmem_limit_bytes=VMEM_LIMIT),
        name="ssd",
    )(xbc, xbc, xbc, dtc, sh, place, lhs_ones, rhs_const, cw, cw, cw, cbias, cbias, cbias,
      alog, dtbias, dskip)


def _mlp_kernel(x_ref, attn_ref, ssm_ref, z_ref, gat_ref, gn_ref, wo_ref, gm_ref, wup_ref, wdn_ref, o_ref):
    a = attn_ref[...]
    an = (a * _inv_rms(a) * gat_ref[...]).astype(BF16)
    zc = z_ref[...].astype(F32)
    y = ssm_ref[...] * (zc * jax.nn.sigmoid(zc))
    yn = jnp.concatenate([y[:, g * GROUP_INNER:(g + 1) * GROUP_INNER]
                          * _inv_rms(y[:, g * GROUP_INNER:(g + 1) * GROUP_INNER]) for g in range(SSM_GROUPS)],
                         axis=1) * gn_ref[...]
    mix = _dot(an, wo_ref[0:ATTN_WIDTH, :]) + _dot(yn.astype(BF16), wo_ref[ATTN_WIDTH:D_MIX, :])
    x1 = x_ref[...] + mix
    hm = (x1 * _inv_rms(x1) * gm_ref[...]).astype(BF16)
    acc = jnp.zeros_like(x1)
    for c in range(D_FF // FF_CHUNK):
        cols = slice(c * FF_CHUNK, (c + 1) * FF_CHUNK)
        u = _dot(hm, wup_ref[:, cols])
        acc = acc + _dot(jnp.square(jnp.maximum(u, 0.0)).astype(BF16), wdn_ref[cols, :])
    o_ref[...] = x1 + acc


def _mlp_call(x2, attn, ssm, z, gat, gn, wo, gm, wup, wdn):
    t = x2.shape[0]
    row = lambda w: pl.BlockSpec((TM_MLP, w), lambda i: (i, 0))
    full = lambda a: pl.BlockSpec(a.shape, lambda i: (0, 0), pipeline_mode=pl.Buffered(1))
    return pl.pallas_call(
        _mlp_kernel,
        grid=(t // TM_MLP,),
        in_specs=[row(D_MODEL), row(ATTN_WIDTH), row(SSM_INNER), row(SSM_INNER), full(gat), full(gn), full(wo),
                  full(gm), full(wup), full(wdn)],
        out_specs=row(D_MODEL),
        out_shape=jax.ShapeDtypeStruct((t, D_MODEL), F32),
        compiler_params=pltpu.CompilerParams(dimension_semantics=("arbitrary",),
                                             vmem_limit_bytes=VMEM_LIMIT),
        name="outproj_mlp",
    )(x2, attn, ssm, z, gat, gn, wo, gm, wup, wdn)


def _group_major(fwd, bwd):
    lead = fwd.shape[:-1]
    both = jnp.stack([fwd.reshape(lead + (SSM_GROUPS, HEADS_PER_GROUP)),
                      bwd.reshape(lead + (SSM_GROUPS, HEADS_PER_GROUP))], axis=-2)
    return both.reshape(lead + (2 * SSM_HEADS,))


def _rotate_half(w):
    half = QK_ROPE_DIM // 2
    return jnp.concatenate([-w[..., half:], w[..., :half]], axis=-1)


def _swap_halves(g):
    half = QK_ROPE_DIM // 2
    return jnp.concatenate([g[..., half:], g[..., :half]], axis=-1)


def _lane_bcast(v):
    return jnp.broadcast_to(v[..., None], v.shape + (LANES,)).astype(F32)


def _layer(x2, cs, batch, seq, ln_mix_g, w_in, q_a_norm_g, w_uq, kv_a_norm_g, w_ukv, q_norm_g,
           k_norm_g, attn_out_norm_g, conv_w, conv_b, a_log_fwd, a_log_bwd, dt_bias_fwd, dt_bias_bwd,
           d_skip, ssm_norm_g, w_out, ln_mlp_g, w_mlp_up, w_mlp_down):
    win = _win_prep_call(*w_in)
    wq = w_uq.reshape(Q_LORA_RANK, ATTN_HEADS, QK_HEAD_DIM)
    wuq = jnp.concatenate([wq, _rotate_half(wq[..., QK_NOPE_DIM:])], axis=-1)
    wuq = wuq.reshape(Q_LORA_RANK, ATTN_HEADS * HEAD_PAD).astype(BF16)
    wukv = w_ukv.reshape(KV_LORA_RANK, ATTN_HEADS, QK_NOPE_DIM + V_HEAD_DIM)
    wuk = jnp.pad(wukv[:, :, :QK_NOPE_DIM],
                  ((0, 0), (0, 0), (0, HEAD_PAD - QK_NOPE_DIM))).reshape(KV_LORA_RANK, -1).astype(BF16)
    wv = wukv[:, :, QK_NOPE_DIM:].reshape(KV_LORA_RANK, ATTN_HEADS // 2, 2, V_HEAD_DIM)
    zv = jnp.zeros_like(wv[:, :, 0])
    wuv = jnp.stack([jnp.concatenate([wv[:, :, 0], zv], axis=-1),
                     jnp.concatenate([zv, wv[:, :, 1]], axis=-1)], axis=2).reshape(KV_LORA_RANK, -1).astype(BF16)
    ones_v = np.ones((ATTN_HEADS // 2, V_HEAD_DIM), np.float32)
    vone = jnp.asarray(np.stack([np.concatenate([0 * ones_v, ones_v], axis=-1),
                                 np.concatenate([ones_v, 0 * ones_v], axis=-1)], axis=1).reshape(1, -1))
    scale = QK_HEAD_DIM ** -0.5 * np.log2(np.e)
    gq = (jnp.concatenate([q_norm_g, _swap_halves(q_norm_g[QK_NOPE_DIM:])]) * scale)[None, :]
    zeros_nope = jnp.zeros((QK_NOPE_DIM,), k_norm_g.dtype)
    gkn = jnp.concatenate([k_norm_g[:QK_NOPE_DIM], zeros_nope])[None, :]
    gkr = jnp.concatenate([zeros_nope, k_norm_g[QK_NOPE_DIM:], _swap_halves(k_norm_g[QK_NOPE_DIM:])])[None, :]

    q, k, v, z, xbc, dtc = _inproj_call(
        x2, cs, ln_mix_g[None, :], win, q_a_norm_g[None, :], wuq, kv_a_norm_g[None, :], wuk, wuv, vone,
        gq, gkn, gkr)

    attn = _attn_call(q, k, v, batch, seq)

    cw = jnp.pad(conv_w[:, 0, :], ((0, SUBLANES - SSM_CONV), (0, 0)))
    cbias = conv_b[None, :]
    alog = _lane_bcast(_group_major(a_log_fwd, a_log_bwd))
    dtbias = _lane_bcast(_group_major(dt_bias_fwd, dt_bias_bwd))
    dskip = jnp.repeat(d_skip, SSM_HEAD_DIM)[None, :]
    ssm = _ssd_call(xbc, dtc, cw, cbias, alog, dtbias, dskip, batch, seq)

    return _mlp_call(x2, attn, ssm, z, attn_out_norm_g[None, :], ssm_norm_g[None, :], w_out.astype(BF16),
                     ln_mlp_g[None, :], w_mlp_up.astype(BF16), w_mlp_down.astype(BF16))


def _rope_table(positions):
    inv_freq = 1.0 / (ROPE_THETA ** (jnp.arange(0, QK_ROPE_DIM, 2, dtype=F32) / QK_ROPE_DIM))
    ang = inv_freq[:, None] * positions.astype(F32).reshape(1, -1)
    return jnp.concatenate([jnp.cos(ang), jnp.sin(ang)], axis=0)


def kernel(x, positions, ln_mix_g, w_in, q_a_norm_g, w_uq, kv_a_norm_g, w_ukv, q_norm_g, k_norm_g,
           attn_out_norm_g, conv_w, conv_b, a_log_fwd, a_log_bwd, dt_bias_fwd, dt_bias_bwd, d_skip,
           ssm_norm_g, w_out, ln_mlp_g, w_mlp_up, w_mlp_down):
    batch, seq, d = x.shape
    assert d == D_MODEL and seq % TQ == 0 and (batch * seq) % TM_IN == 0 and (batch * seq) % TM_MLP == 0
    cs = _rope_table(positions)
    x2 = x.reshape(batch * seq, d)
    for l in range(ln_mix_g.shape[0]):
        x2 = _layer(x2, cs, batch, seq, ln_mix_g[l], (w_in, l), q_a_norm_g[l], w_uq[l], kv_a_norm_g[l],
                    w_ukv[l], q_norm_g[l], k_norm_g[l], attn_out_norm_g[l], conv_w[l], conv_b[l],
                    a_log_fwd[l], a_log_bwd[l], dt_bias_fwd[l], dt_bias_bwd[l], d_skip[l], ssm_norm_g[l],
                    w_out[l], ln_mlp_g[l], w_mlp_up[l], w_mlp_down[l])
    return x2.reshape(batch, seq, d)
```

```python
import numpy as np
import jax
import jax.numpy as jnp
from jax import lax
from jax.experimental import pallas as pl
from jax.experimental.pallas import tpu as pltpu

F32 = jnp.float32
BF16 = jnp.bfloat16

D_MODEL = 1024
ATTN_HEADS = 8
QK_NOPE_DIM = 64
QK_ROPE_DIM = 32
QK_HEAD_DIM = QK_NOPE_DIM + QK_ROPE_DIM
V_HEAD_DIM = 64
Q_LORA_RANK = D_MODEL // 4
KV_LORA_RANK = D_MODEL // 8
ROPE_THETA = 10000.0
ATTN_WIDTH = ATTN_HEADS * V_HEAD_DIM
SSM_HEADS = 8
SSM_HEAD_DIM = 64
SSM_INNER = SSM_HEADS * SSM_HEAD_DIM
SSM_GROUPS = 2
SSM_STATE = 128
SSM_CONV = 5
SSM_CHUNK = 128
SSM_CONV_CH = SSM_INNER + 2 * SSM_GROUPS * SSM_STATE
D_MIX = ATTN_WIDTH + SSM_INNER
D_FF = 4 * D_MODEL
EPS = 1e-6

LANES = 128
SUBLANES = 8
HEAD_PAD = LANES

HEADS_PER_GROUP = SSM_HEADS // SSM_GROUPS
GROUP_INNER = SSM_INNER // SSM_GROUPS
GROUP_COLS = GROUP_INNER + 2 * SSM_STATE
DT_ROWS = 2 * HEADS_PER_GROUP
SEG_K_PER_ROW = 16
SEG_BLOCKS = DT_ROWS + HEADS_PER_GROUP
LOG2E = float(np.log2(np.e))
CONV_WIN = 2 * SSM_CHUNK
CONV_WIN_LEAD = SSM_CHUNK // 2
CONV_SHIFTED_TAPS = tuple(k for k in range(SSM_CONV) if k != SSM_CONV // 2)

COL_CKV = Q_LORA_RANK
COL_MISC = COL_CKV + KV_LORA_RANK
COL_Z = COL_MISC + LANES
COL_XBC = COL_Z + SSM_INNER
IN_COLS = COL_XBC + SSM_CONV_CH

TM_IN = 1024
IN_SUBTILE = 512
TQ = 512
ATTN_HEADS_PER_STEP = 8
TM_MLP = 512
FF_CHUNK = 1024
SSD_UNROLL = 8
CONV_AHEAD = 4
PASS_A_UNROLL = 4
WIN_PREP_STEPS = 4
VMEM_LIMIT = 56 * 1024 * 1024


def _inv_rms(x):
    n = x.shape[-1]
    x2 = x * x
    acc = x2[:, 0:LANES]
    for i in range(1, n // LANES):
        acc = acc + x2[:, i * LANES:(i + 1) * LANES]
    return lax.rsqrt(jnp.sum(acc, axis=-1, keepdims=True) * (1.0 / n) + EPS)


def _dot(a, b):
    return jnp.dot(a, b, preferred_element_type=F32)


def _dot_nt(a, b):
    return lax.dot_general(a, b, (((1,), (1,)), ((), ())), preferred_element_type=F32)


def _inproj_kernel(x_ref, cs_ref, g_ref, win_ref, gqa_ref, wuq_ref, gkva_ref, wuk_ref, wuv_ref, vone_ref,
                   gq_ref, gkn_ref, gkr_ref,
                   q_ref, k_ref, v_ref, z_ref, xbc_ref, dt_ref):
    for sub in range(x_ref.shape[0] // IN_SUBTILE):
        _inproj_rows(sub * IN_SUBTILE, x_ref, cs_ref, g_ref, win_ref, gqa_ref, wuq_ref, gkva_ref, wuk_ref,
                     wuv_ref, vone_ref, gq_ref, gkn_ref, gkr_ref, q_ref, k_ref, v_ref, z_ref, xbc_ref, dt_ref)


def _inproj_rows(r0, x_ref, cs_ref, g_ref, win_ref, gqa_ref, wuq_ref, gkva_ref, wuk_ref, wuv_ref, vone_ref,
                 gq_ref, gkn_ref, gkr_ref, q_ref, k_ref, v_ref, z_ref, xbc_ref, dt_ref):
    tm = IN_SUBTILE
    rs = slice(r0, r0 + tm)
    x = x_ref[rs, :]
    h = (x * _inv_rms(x) * g_ref[...]).astype(BF16)
    big = _dot(h, win_ref[...])
    z_ref[rs, :] = big[:, COL_Z:COL_XBC].astype(BF16)
    xbc_ref[rs, :] = big[:, COL_XBC:IN_COLS].astype(BF16)
    misc = big[:, COL_MISC:COL_Z]
    for c in range(tm // SSM_CHUNK):
        dt_ref[r0 // SSM_CHUNK + c] = misc[c * SSM_CHUNK:(c + 1) * SSM_CHUNK, :].T[0:2 * SSM_HEADS, :]

    cq = big[:, 0:COL_CKV]
    ckv = big[:, COL_CKV:COL_MISC]
    cqn = (cq * _inv_rms(cq) * gqa_ref[...]).astype(BF16)
    ckvn = (ckv * _inv_rms(ckv) * gkva_ref[...]).astype(BF16)
    q_pre = _dot(cqn, wuq_ref[...])
    k_pre = _dot(ckvn, wuk_ref[...])
    v_ref[rs, :] = (_dot(ckvn, wuv_ref[...]) + vone_ref[...]).astype(BF16)

    half = QK_ROPE_DIM // 2
    zpad = jnp.zeros((HEAD_PAD - QK_ROPE_DIM, SSM_CHUNK), F32)
    tt = jnp.concatenate(
        [jnp.concatenate([cs_ref[:, r0 + c * SSM_CHUNK:r0 + (c + 1) * SSM_CHUNK], zpad], axis=0).T
         for c in range(tm // SSM_CHUNK)], axis=0)
    lane_t = lax.broadcasted_iota(jnp.int32, (tm, HEAD_PAD), 1)
    cs = jnp.where(lane_t < QK_NOPE_DIM, 1.0,
                   jnp.where(lane_t < QK_NOPE_DIM + half, pltpu.roll(tt, QK_NOPE_DIM, 1),
                             jnp.where(lane_t < QK_HEAD_DIM + half, pltpu.roll(tt, QK_NOPE_DIM + half, 1),
                                       pltpu.roll(tt, QK_HEAD_DIM, 1))))
    lane = lax.broadcasted_iota(jnp.int32, (1, HEAD_PAD), 1)
    in_head = (lane < QK_HEAD_DIM).astype(F32)
    is_rope = ((lane >= QK_NOPE_DIM) & (lane < QK_HEAD_DIM)).astype(F32)
    inv_d = 1.0 / QK_HEAD_DIM

    gcq = gq_ref[...] * cs
    for hh in range(ATTN_HEADS):
        sl = slice(hh * HEAD_PAD, (hh + 1) * HEAD_PAD)
        qh = q_pre[:, sl]
        ssq = jnp.sum(qh * qh * in_head, axis=-1, keepdims=True)
        q_ref[rs, sl] = (qh * lax.rsqrt(ssq * inv_d + EPS) * gcq).astype(BF16)

    ab = misc * (gkr_ref[...] * cs)
    lane2 = lax.broadcasted_iota(jnp.int32, (tm, HEAD_PAD), 1)
    swapped = jnp.where(lane2 < QK_HEAD_DIM, pltpu.roll(ab, HEAD_PAD - QK_ROPE_DIM, 1),
                        pltpu.roll(ab, QK_ROPE_DIM, 1))
    s_both = jnp.where(lane2 >= QK_NOPE_DIM, ab + swapped, 0.0)
    ssq_pe = jnp.sum(misc * misc * is_rope, axis=-1, keepdims=True)
    gkn = gkn_ref[...]
    for hh in range(ATTN_HEADS):
        sl = slice(hh * HEAD_PAD, (hh + 1) * HEAD_PAD)
        kh = k_pre[:, sl]
        ssq = jnp.sum(kh * kh, axis=-1, keepdims=True) + ssq_pe
        k_ref[rs, sl] = ((kh * gkn + s_both) * lax.rsqrt(ssq * inv_d + EPS)).astype(BF16)


def _win_prep_kernel(w_ref, o_ref):
    w = w_ref[...]
    o_kpe = Q_LORA_RANK + KV_LORA_RANK
    o_z = o_kpe + QK_ROPE_DIM
    o_dt = o_z + SSM_INNER + SSM_CONV_CH
    half = QK_ROPE_DIM // 2
    hpg = HEADS_PER_GROUP
    dt_f = w[:, o_dt:o_dt + SSM_HEADS]
    dt_b = w[:, o_dt + SSM_HEADS:o_dt + 2 * SSM_HEADS]
    kpe = w[:, o_kpe:o_z]
    pieces = [w[:, 0:o_kpe]]
    for g in range(SSM_GROUPS):
        pieces += [dt_f[:, g * hpg:(g + 1) * hpg], dt_b[:, g * hpg:(g + 1) * hpg]]
    pieces += [jnp.zeros((w.shape[0], QK_NOPE_DIM - 2 * SSM_HEADS), F32), kpe, -kpe[:, half:], kpe[:, :half],
               w[:, o_z:o_dt]]
    o_ref[...] = jnp.concatenate(pieces, axis=1).astype(BF16)


def _win_prep_call(w_in3, layer):
    depth, d, n = w_in3.shape
    rows = d // WIN_PREP_STEPS
    return pl.pallas_call(
        _win_prep_kernel,
        grid=(WIN_PREP_STEPS,),
        in_specs=[pl.BlockSpec((rows, n), lambda i: (layer * WIN_PREP_STEPS + i, 0))],
        out_specs=pl.BlockSpec((rows, IN_COLS), lambda i: (i, 0)),
        out_shape=jax.ShapeDtypeStruct((d, IN_COLS), BF16),
        compiler_params=pltpu.CompilerParams(dimension_semantics=("arbitrary",), vmem_limit_bytes=VMEM_LIMIT),
        name="win_prep",
    )(w_in3.reshape(depth * d, n))


def _inproj_call(x2, cs, g, win, gqa, wuq, gkva, wuk, wuv, vone, gq, gkn, gkr):
    t = x2.shape[0]
    cpt = TM_IN // SSM_CHUNK
    full = lambda a: pl.BlockSpec(a.shape, lambda i: (0,) * a.ndim, pipeline_mode=pl.Buffered(1))
    row = lambda w: pl.BlockSpec((TM_IN, w), lambda i: (i, 0))
    return pl.pallas_call(
        _inproj_kernel,
        grid=(t // TM_IN,),
        in_specs=[row(D_MODEL), pl.BlockSpec((QK_ROPE_DIM, TM_IN), lambda i: (0, i)), full(g), full(win), full(gqa), full(wuq), full(gkva),
                  full(wuk), full(wuv), full(vone), full(gq), full(gkn), full(gkr)],
        out_specs=[row(ATTN_HEADS * HEAD_PAD), row(ATTN_HEADS * HEAD_PAD), row(ATTN_HEADS * HEAD_PAD),
                   row(SSM_INNER), row(SSM_CONV_CH),
                   pl.BlockSpec((cpt, 2 * SSM_HEADS, SSM_CHUNK), lambda i: (i, 0, 0))],
        out_shape=[jax.ShapeDtypeStruct((t, ATTN_HEADS * HEAD_PAD), BF16),
                   jax.ShapeDtypeStruct((t, ATTN_HEADS * HEAD_PAD), BF16),
                   jax.ShapeDtypeStruct((t, ATTN_HEADS * HEAD_PAD), BF16),
                   jax.ShapeDtypeStruct((t, SSM_INNER), BF16),
                   jax.ShapeDtypeStruct((t, SSM_CONV_CH), BF16),
                   jax.ShapeDtypeStruct((t // SSM_CHUNK, 2 * SSM_HEADS, SSM_CHUNK), F32)],
        compiler_params=pltpu.CompilerParams(dimension_semantics=("arbitrary",),
                                             vmem_limit_bytes=VMEM_LIMIT),
        name="inproj",
    )(x2, cs, g, win, gqa, wuq, gkva, wuk, wuv, vone, gq, gkn, gkr)


def _attn_kernel(q_ref, k_ref, v_ref, o_ref):
    lane = lax.broadcasted_iota(jnp.int32, (q_ref.shape[0], HEAD_PAD), 1)
    for jp in range(ATTN_HEADS_PER_STEP // 2):
        sls = [slice(j * HEAD_PAD, (j + 1) * HEAD_PAD) for j in (2 * jp, 2 * jp + 1)]
        ss = [_dot_nt(q_ref[:, sl], k_ref[:, sl]) for sl in sls]
        ps = [jnp.exp2(s - jnp.max(s, axis=-1, keepdims=True)).astype(BF16) for s in ss]
        accs = [_dot(p, v_ref[:, sl]) for p, sl in zip(ps, sls)]
        res = [acc / pltpu.roll(acc, V_HEAD_DIM, 1) for acc in accs]
        o_ref[:, jp * HEAD_PAD:(jp + 1) * HEAD_PAD] = jnp.where(lane < V_HEAD_DIM, res[0], res[1])


def _attn_call(q, k, v, batch, seq):
    n_q = seq // TQ
    hps = ATTN_HEADS_PER_STEP
    return pl.pallas_call(
        _attn_kernel,
        grid=(batch, ATTN_HEADS // hps, n_q),
        in_specs=[pl.BlockSpec((TQ, hps * HEAD_PAD), lambda b, hp, i: (b * n_q + i, hp)),
                  pl.BlockSpec((seq, hps * HEAD_PAD), lambda b, hp, i: (b, hp)),
                  pl.BlockSpec((seq, hps * HEAD_PAD), lambda b, hp, i: (b, hp))],
        out_specs=pl.BlockSpec((TQ, hps * V_HEAD_DIM), lambda b, hp, i: (b * n_q + i, hp)),
        out_shape=jax.ShapeDtypeStruct((batch * seq, ATTN_WIDTH), F32),
        compiler_params=pltpu.CompilerParams(
            dimension_semantics=("arbitrary", "arbitrary", "arbitrary"),
            vmem_limit_bytes=VMEM_LIMIT),
        name="attention",
    )(q, k, v)


def _split3(x):
    hi = x.astype(BF16)
    r1 = x - hi.astype(F32)
    mid = r1.astype(BF16)
    lo = (r1 - mid.astype(F32)).astype(BF16)
    return hi, mid, lo


def _ssd_kernel(x_ref, b_ref, c_ref, dt_ref, sh_ref, place_ref, lhs_ones_ref, rhs_const_ref,
                cwx_ref, cwb_ref, cwc_ref, cbx_ref, cbb_ref, cbc_ref, alog_ref, dtbias_ref, dskip_ref,
                o_ref, xc_scr, xm_scr, nsf_scr, nsb_scr, dtv_scr, w_scr, dec_scr, lhs_scr, rhs_scr, rpart_scr):
    nc = dt_ref.shape[0]
    seq = x_ref.shape[0]
    L = SSM_CHUNK
    hpg = HEADS_PER_GROUP
    npair = hpg // 2
    P = SSM_HEAD_DIM
    x_cols = slice(0, GROUP_INNER)
    b_cols = slice(GROUP_INNER, GROUP_INNER + SSM_STATE)
    c_cols = slice(GROUP_INNER + SSM_STATE, GROUP_COLS)
    pair_cols = [slice(j * L, (j + 1) * L) for j in range(npair)]

    @pl.when((pl.program_id(0) == 0) & (pl.program_id(1) == 0))
    def _init_rhs():
        for u in range(SSD_UNROLL):
            rhs_scr[u] = rhs_const_ref[...]

    a_neg = -jnp.exp(alog_ref[...]) * LOG2E
    dtv = jax.nn.softplus(dt_ref[...] + dtbias_ref[...][None])
    dtv_scr[...] = dtv
    da2 = (dtv * a_neg[None]).reshape(nc * DT_ROWS, L)
    ri = lax.broadcasted_iota(jnp.int32, (L, L), 0)
    ci = lax.broadcasted_iota(jnp.int32, (L, L), 1)
    upper = (ri <= ci).astype(BF16)
    lower = (ri >= ci).astype(BF16)
    tri = jnp.concatenate([upper, lower], axis=1)
    cs_fb = sum(_dot(p, tri) for p in _split3(da2))
    rowsel = (ri & hpg) == 0
    cs2 = jnp.where(rowsel, cs_fb[:, 0:L], cs_fb[:, L:2 * L])
    colcs = cs2.T

    col_parts = jnp.concatenate(_split3(colcs), axis=1)
    half_n = place_ref.shape[1] // 2
    for hf in range(2):
        cols = slice(hf * half_n, (hf + 1) * half_n)
        lhs_half = _dot(col_parts, place_ref[:, cols]) + lhs_ones_ref[:, cols]
        for c in range(nc // 2):
            lhs_scr[hf * (nc // 2) + c] = lhs_half[:, c * L:(c + 1) * L].astype(BF16)
    for i, part in enumerate(_split3(-cs2)):
        rpart_scr[i] = part.astype(F32).reshape(nc, DT_ROWS, L)

    def lane_bcast(col):
        return jnp.broadcast_to(col, (col.shape[0], L))

    tot2 = jnp.where(rowsel, lane_bcast(cs2[:, L - 1:L]), lane_bcast(cs2[:, 0:1]))
    dec_scr[...] = jnp.exp2(tot2).reshape(nc, DT_ROWS, L)
    w_scr[...] = (dtv.reshape(nc * DT_ROWS, L) * jnp.exp2(tot2 - cs2)).reshape(nc, DT_ROWS, L)

    lane_t = lax.broadcasted_iota(jnp.int32, (L, L), 1)
    low_half = lane_t < P

    cw_halves = (cwx_ref[...], jnp.concatenate([cwb_ref[...], cwc_ref[...]], axis=1))
    cb_halves = (cbx_ref[...], jnp.concatenate([cbb_ref[...], cbc_ref[...]], axis=1))

    def rows(hf, start, size):
        if hf == 0:
            return x_ref[pl.ds(start, size), :]
        return jnp.concatenate([b_ref[pl.ds(start, size), :], c_ref[pl.ds(start, size), :]], axis=1)

    def conv_stage(c):
        base = pl.multiple_of(c * L, L)
        ws = pl.multiple_of(jnp.clip(c * L - CONV_WIN_LEAD, 0, seq - CONV_WIN), CONV_WIN_LEAD)
        variant = jnp.where(c == 0, 0, jnp.where(c == nc - 1, 2, 1))
        sh = sh_ref[variant]
        halves = []
        for hf in range(2):
            cw = cw_halves[hf]
            shifted = _dot(sh, rows(hf, ws, CONV_WIN))
            acc = cb_halves[hf] + cw[SSM_CONV // 2:SSM_CONV // 2 + 1, :] * rows(hf, base, L).astype(F32)
            for t, kk in enumerate(CONV_SHIFTED_TAPS):
                acc = acc + cw[kk:kk + 1, :] * shifted[t * L:(t + 1) * L, :]
            halves.append(acc * jax.nn.sigmoid(acc))
        xc = jnp.concatenate(halves, axis=1)
        xc_scr[c] = xc
        for j in range(npair):
            xp = xc[:, pair_cols[j]]
            xm_scr[c, j] = jnp.concatenate([jnp.where(low_half, xp, 0.0), jnp.where(low_half, 0.0, xp)],
                                           axis=0).astype(BF16)

    def state_loads(c):
        return xc_scr[c, :, b_cols], [xm_scr[c, j] for j in range(npair)]

    def state_stage(c, loaded):
        b_tok, xm = loaded
        bt = b_tok.T
        for d, ns_scr in ((0, nsf_scr), (1, nsb_scr)):
            w = w_scr[c, d * hpg:(d + 1) * hpg, :]
            ns_scr[c] = jnp.concatenate(
                [_dot(jnp.concatenate([(bt * w[2 * j:2 * j + 1, :]).astype(BF16),
                                       (bt * w[2 * j + 1:2 * j + 2, :]).astype(BF16)], axis=1), xm[j])
                 for j in range(npair)], axis=1)

    for c in range(CONV_AHEAD):
        conv_stage(c)

    def pass_a(c, carry):
        loaded = state_loads(c)
        conv_stage(c + CONV_AHEAD)
        state_stage(c, loaded)
        return carry
    lax.fori_loop(0, nc - CONV_AHEAD, pass_a, 0, unroll=PASS_A_UNROLL)
    for c in range(nc - CONV_AHEAD, nc):
        state_stage(c, state_loads(c))

    lane_r = lax.broadcasted_iota(jnp.int32, (1, L), 1)

    def decay_row(c, d):
        dec = dec_scr[c, d * hpg:(d + 1) * hpg, :]
        return jnp.concatenate([jnp.where(lane_r < P, dec[2 * j:2 * j + 1, :], dec[2 * j + 1:2 * j + 2, :])
                                for j in range(npair)], axis=1)

    def pass_b(d, ns_scr):
        def body(i, st):
            c = i if d == 0 else nc - 1 - i
            new = ns_scr[c]
            ns_scr[c] = st
            return st * decay_row(c, d) + new
        lax.fori_loop(0, nc, body, jnp.zeros((SSM_STATE, GROUP_INNER), F32))
    pass_b(0, nsf_scr)
    pass_b(1, nsb_scr)

    mask_f = ci <= ri
    mask_b = ci >= ri
    neg_inf = jnp.float32(-jnp.inf)
    sub16 = lax.broadcasted_iota(jnp.int32, (SEG_K_PER_ROW, L), 0)

    def pass_c(c, carry):
        base = pl.multiple_of(c * L, L)
        xc = xc_scr[c]
        bm = xc[:, b_cols].astype(BF16)
        cm = xc[:, c_cols].astype(BF16)
        cb = _dot_nt(cm, bm)
        off_f = _dot(cm, nsf_scr[c].astype(BF16))
        off_b = _dot(cm, nsb_scr[c].astype(BF16))
        slot = c % SSD_UNROLL
        for hd in range(DT_ROWS):
            blk = jnp.where(sub16 < 3, 1.0, 0.0)
            for i in range(3):
                blk = jnp.where(sub16 == 3 + i, rpart_scr[i, c, hd:hd + 1, :], blk)
            rhs_scr[slot, hd * SEG_K_PER_ROW:(hd + 1) * SEG_K_PER_ROW, hd * L:(hd + 1) * L] = blk.astype(BF16)
        seg = _dot(lhs_scr[c], rhs_scr[slot])
        dt_f_r = dtv_scr[c, 0:hpg, :]
        dt_b_r = dtv_scr[c, hpg:DT_ROWS, :]
        ys = []
        for j in range(npair):
            ws_ = []
            for h in (2 * j, 2 * j + 1):
                ef = jnp.exp2(jnp.where(mask_f, seg[:, h * L:(h + 1) * L], neg_inf))
                eb = jnp.exp2(jnp.where(mask_b, seg[:, (hpg + h) * L:(hpg + h + 1) * L], neg_inf))
                ws_.append((cb * (ef * dt_f_r[h:h + 1, :] + eb * dt_b_r[h:h + 1, :])).astype(BF16))
            scale_f = jnp.exp2(seg[:, (DT_ROWS + j) * L:(DT_ROWS + j + 1) * L])
            scale_b = jnp.exp2(seg[:, (DT_ROWS + npair + j) * L:(DT_ROWS + npair + j + 1) * L])
            ys.append(_dot(jnp.concatenate(ws_, axis=1), xm_scr[c, j])
                      + scale_f * off_f[:, pair_cols[j]] + scale_b * off_b[:, pair_cols[j]])
        y = jnp.concatenate(ys, axis=1) + dskip_ref[...] * xc[:, x_cols]
        o_ref[pl.ds(base, L), :] = y
        return carry
    lax.fori_loop(0, nc, pass_c, 0, unroll=SSD_UNROLL)


def _conv_shift_matrices():
    out = np.zeros((3, len(CONV_SHIFTED_TAPS) * SSM_CHUNK, CONV_WIN), np.float32)
    for v, lead in enumerate((0, CONV_WIN_LEAD, CONV_WIN - SSM_CHUNK)):
        for i, kk in enumerate(CONV_SHIFTED_TAPS):
            for t in range(SSM_CHUNK):
                j = t + kk - SSM_CONV // 2 + lead
                if 0 <= j < CONV_WIN:
                    out[v, i * SSM_CHUNK + t, j] = 1.0
    return out


def _seg_matmul_constants(nc):
    L, K, half = SSM_CHUNK, SEG_K_PER_ROW, SSM_HEAD_DIM
    place = np.zeros((3 * L, nc * L), np.float32)
    lhs_ones = np.zeros((1, nc * L), np.float32)
    for c in range(nc):
        for hd in range(DT_ROWS):
            for i in range(3):
                place[i * L + c * DT_ROWS + hd, c * L + hd * K + i] = 1.0
                lhs_ones[0, c * L + hd * K + 3 + i] = 1.0
    rhs = np.zeros((L, SEG_BLOCKS * L), np.float32)
    for hd in range(DT_ROWS):
        d, h = divmod(hd, HEADS_PER_GROUP)
        pair_block = DT_ROWS + d * (HEADS_PER_GROUP // 2) + h // 2
        lanes = slice(0, half) if h % 2 == 0 else slice(half, L)
        rhs[hd * K:hd * K + 3, hd * L:(hd + 1) * L] = 1.0
        rhs[hd * K:hd * K + 3, pair_block * L:(pair_block + 1) * L][:, lanes] = 1.0
    return place, lhs_ones, rhs


def _ssd_call(xbc, dtc, cw, cbias, alog, dtbias, dskip, batch, seq):
    nc = seq // SSM_CHUNK
    L = SSM_CHUNK
    assert nc >= 3 and seq >= CONV_WIN and nc * DT_ROWS == L
    sh = jnp.asarray(_conv_shift_matrices(), BF16)
    place, lhs_ones, rhs_const = _seg_matmul_constants(nc)
    place = jnp.asarray(place, BF16)
    lhs_ones = jnp.asarray(lhs_ones, F32)
    rhs_const = jnp.asarray(rhs_const, BF16)
    const = lambda a: pl.BlockSpec(a.shape, lambda b, g: (0,) * a.ndim)
    b_blk0 = SSM_INNER // SSM_STATE
    c_blk0 = b_blk0 + SSM_GROUPS

    def xbc_views(rows_, batched):
        lead = (lambda b: b) if batched else (lambda b: 0)
        return [pl.BlockSpec((rows_, GROUP_INNER), lambda b, g: (lead(b), g)),
                pl.BlockSpec((rows_, SSM_STATE), lambda b, g: (lead(b), b_blk0 + g)),
                pl.BlockSpec((rows_, SSM_STATE), lambda b, g: (lead(b), c_blk0 + g))]
    return pl.pallas_call(
        _ssd_kernel,
        grid=(batch, SSM_GROUPS),
        in_specs=xbc_views(seq, True) + [
                  pl.BlockSpec((nc, DT_ROWS, L), lambda b, g: (b, g, 0)),
                  const(sh), const(place), const(lhs_ones), const(rhs_const)]
                 + xbc_views(SUBLANES, False) + xbc_views(1, False) + [
                  pl.BlockSpec((DT_ROWS, L), lambda b, g: (g, 0)),
                  pl.BlockSpec((DT_ROWS, L), lambda b, g: (g, 0)),
                  pl.BlockSpec((1, GROUP_INNER), lambda b, g: (0, g))],
        out_specs=pl.BlockSpec((seq, GROUP_INNER), lambda b, g: (b, g)),
        out_shape=jax.ShapeDtypeStruct((batch * seq, SSM_INNER), F32),
        scratch_shapes=[pltpu.VMEM((nc, L, GROUP_COLS), F32),
                        pltpu.VMEM((nc, HEADS_PER_GROUP // 2, 2 * L, L), BF16),
                        pltpu.VMEM((nc, SSM_STATE, GROUP_INNER), F32),
                        pltpu.VMEM((nc, SSM_STATE, GROUP_INNER), F32),
                        pltpu.VMEM((nc, DT_ROWS, L), F32),
                        pltpu.VMEM((nc, DT_ROWS, L), F32),
                        pltpu.VMEM((nc, DT_ROWS, L), F32),
                        pltpu.VMEM((nc, L, L), BF16),
                        pltpu.VMEM((SSD_UNROLL, L, SEG_BLOCKS * L), BF16),
                        pltpu.VMEM((3, nc, DT_ROWS, L), F32)],
        compiler_params=pltpu.CompilerParams(dimension_semantics=("arbitrary", "arbitrary"),
                                             vmem_limit_bytes=VMEM_LIMIT),
        name="ssd",
    )(xbc, xbc, xbc, dtc, sh, place, lhs_ones, rhs_const, cw, cw, cw, cbias, cbias, cbias,
      alog, dtbias, dskip)


def _mlp_kernel(x_ref, attn_ref, ssm_ref, z_ref, gat_ref, gn_ref, wo_ref, gm_ref, wup_ref, wdn_ref, o_ref):
    a = attn_ref[...]
    an = (a * _inv_rms(a) * gat_ref[...]).astype(BF16)
    zc = z_ref[...].astype(F32)
    y = ssm_ref[...] * (zc * jax.nn.sigmoid(zc))
    yn = jnp.concatenate([y[:, g * GROUP_INNER:(g + 1) * GROUP_INNER]
                          * _inv_rms(y[:, g * GROUP_INNER:(g + 1) * GROUP_INNER]) for g in range(SSM_GROUPS)],
                         axis=1) * gn_ref[...]
    mix = _dot(an, wo_ref[0:ATTN_WIDTH, :]) + _dot(yn.astype(BF16), wo_ref[ATTN_WIDTH:D_MIX, :])
    x1 = x_ref[...] + mix
    hm = (x1 * _inv_rms(x1) * gm_ref[...]).astype(BF16)
    acc = jnp.zeros_like(x1)
    for c in range(D_FF // FF_CHUNK):
        cols = slice(c * FF_CHUNK, (c + 1) * FF_CHUNK)
        u = _dot(hm, wup_ref[:, cols])
        acc = acc + _dot(jnp.square(jnp.maximum(u, 0.0)).astype(BF16), wdn_ref[cols, :])
    o_ref[...] = x1 + acc


def _mlp_call(x2, attn, ssm, z, gat, gn, wo, gm, wup, wdn):
    t = x2.shape[0]
    row = lambda w: pl.BlockSpec((TM_MLP, w), lambda i: (i, 0))
    full = lambda a: pl.BlockSpec(a.shape, lambda i: (0, 0), pipeline_mode=pl.Buffered(1))
    return pl.pallas_call(
        _mlp_kernel,
        grid=(t // TM_MLP,),
        in_specs=[row(D_MODEL), row(ATTN_WIDTH), row(SSM_INNER), row(SSM_INNER), full(gat), full(gn), full(wo),
                  full(gm), full(wup), full(wdn)],
        out_specs=row(D_MODEL),
        out_shape=jax.ShapeDtypeStruct((t, D_MODEL), F32),
        compiler_params=pltpu.CompilerParams(dimension_semantics=("arbitrary",),
                                             vmem_limit_bytes=VMEM_LIMIT),
        name="outproj_mlp",
    )(x2, attn, ssm, z, gat, gn, wo, gm, wup, wdn)


def _group_major(fwd, bwd):
    lead = fwd.shape[:-1]
    both = jnp.stack([fwd.reshape(lead + (SSM_GROUPS, HEADS_PER_GROUP)),
                      bwd.reshape(lead + (SSM_GROUPS, HEADS_PER_GROUP))], axis=-2)
    return both.reshape(lead + (2 * SSM_HEADS,))


def _rotate_half(w):
    half = QK_ROPE_DIM // 2
    return jnp.concatenate([-w[..., half:], w[..., :half]], axis=-1)


def _swap_halves(g):
    half = QK_ROPE_DIM // 2
    return jnp.concatenate([g[..., half:], g[..., :half]], axis=-1)


def _lane_bcast(v):
    return jnp.broadcast_to(v[..., None], v.shape + (LANES,)).astype(F32)


def _layer(x2, cs, batch, seq, ln_mix_g, w_in, q_a_norm_g, w_uq, kv_a_norm_g, w_ukv, q_norm_g,
           k_norm_g, attn_out_norm_g, conv_w, conv_b, a_log_fwd, a_log_bwd, dt_bias_fwd, dt_bias_bwd,
           d_skip, ssm_norm_g, w_out, ln_mlp_g, w_mlp_up, w_mlp_down):
    win = _win_prep_call(*w_in)
    wq = w_uq.reshape(Q_LORA_RANK, ATTN_HEADS, QK_HEAD_DIM)
    wuq = jnp.concatenate([wq, _rotate_half(wq[..., QK_NOPE_DIM:])], axis=-1)
    wuq = wuq.reshape(Q_LORA_RANK, ATTN_HEADS * HEAD_PAD).astype(BF16)
    wukv = w_ukv.reshape(KV_LORA_RANK, ATTN_HEADS, QK_NOPE_DIM + V_HEAD_DIM)
    wuk = jnp.pad(wukv[:, :, :QK_NOPE_DIM],
                  ((0, 0), (0, 0), (0, HEAD_PAD - QK_NOPE_DIM))).reshape(KV_LORA_RANK, -1).astype(BF16)
    wv = wukv[:, :, QK_NOPE_DIM:].reshape(KV_LORA_RANK, ATTN_HEADS // 2, 2, V_HEAD_DIM)
    zv = jnp.zeros_like(wv[:, :, 0])
    wuv = jnp.stack([jnp.concatenate([wv[:, :, 0], zv], axis=-1),
                     jnp.concatenate([zv, wv[:, :, 1]], axis=-1)], axis=2).reshape(KV_LORA_RANK, -1).astype(BF16)
    ones_v = np.ones((ATTN_HEADS // 2, V_HEAD_DIM), np.float32)
    vone = jnp.asarray(np.stack([np.concatenate([0 * ones_v, ones_v], axis=-1),
                                 np.concatenate([ones_v, 0 * ones_v], axis=-1)], axis=1).reshape(1, -1))
    scale = QK_HEAD_DIM ** -0.5 * np.log2(np.e)
    gq = (jnp.concatenate([q_norm_g, _swap_halves(q_norm_g[QK_NOPE_DIM:])]) * scale)[None, :]
    zeros_nope = jnp.zeros((QK_NOPE_DIM,), k_norm_g.dtype)
    gkn = jnp.concatenate([k_norm_g[:QK_NOPE_DIM], zeros_nope])[None, :]
    gkr = jnp.concatenate([zeros_nope, k_norm_g[QK_NOPE_DIM:], _swap_halves(k_norm_g[QK_NOPE_DIM:])])[None, :]

    q, k, v, z, xbc, dtc = _inproj_call(
        x2, cs, ln_mix_g[None, :], win, q_a_norm_g[None, :], wuq, kv_a_norm_g[None, :], wuk, wuv, vone,
        gq, gkn, gkr)

    attn = _attn_call(q, k, v, batch, seq)

    cw = jnp.pad(conv_w[:, 0, :], ((0, SUBLANES - SSM_CONV), (0, 0)))
    cbias = conv_b[None, :]
    alog = _lane_bcast(_group_major(a_log_fwd, a_log_bwd))
    dtbias = _lane_bcast(_group_major(dt_bias_fwd, dt_bias_bwd))
    dskip = jnp.repeat(d_skip, SSM_HEAD_DIM)[None, :]
    ssm = _ssd_call(xbc, dtc, cw, cbias, alog, dtbias, dskip, batch, seq)

    return _mlp_call(x2, attn, ssm, z, attn_out_norm_g[None, :], ssm_norm_g[None, :], w_out.astype(BF16),
                     ln_mlp_g[None, :], w_mlp_up.astype(BF16), w_mlp_down.astype(BF16))


def _rope_table(positions):
    inv_freq = 1.0 / (ROPE_THETA ** (jnp.arange(0, QK_ROPE_DIM, 2, dtype=F32) / QK_ROPE_DIM))
    ang = inv_freq[:, None] * positions.astype(F32).reshape(1, -1)
    return jnp.concatenate([jnp.cos(ang), jnp.sin(ang)], axis=0)


def kernel(x, positions, ln_mix_g, w_in, q_a_norm_g, w_uq, kv_a_norm_g, w_ukv, q_norm_g, k_norm_g,
           attn_out_norm_g, conv_w, conv_b, a_log_fwd, a_log_bwd, dt_bias_fwd, dt_bias_bwd, d_skip,
           ssm_norm_g, w_out, ln_mlp_g, w_mlp_up, w_mlp_down):
    batch, seq, d = x.shape
    assert d == D_MODEL and seq % TQ == 0 and (batch * seq) % TM_IN == 0 and (batch * seq) % TM_MLP == 0
    cs = _rope_table(positions)
    x2 = x.reshape(batch * seq, d)
    for l in range(ln_mix_g.shape[0]):
        x2 = _layer(x2, cs, batch, seq, ln_mix_g[l], (w_in, l), q_a_norm_g[l], w_uq[l], kv_a_norm_g[l],
                    w_ukv[l], q_norm_g[l], k_norm_g[l], attn_out_norm_g[l], conv_w[l], conv_b[l],
                    a_log_fwd[l], a_log_bwd[l], dt_bias_fwd[l], dt_bias_bwd[l], d_skip[l], ssm_norm_g[l],
                    w_out[l], ln_mlp_g[l], w_mlp_up[l], w_mlp_down[l])
    return x2.reshape(batch, seq, d)
```

```python
import numpy as np
import jax
import jax.numpy as jnp
from jax import lax
from jax.experimental import pallas as pl
from jax.experimental.pallas import tpu as pltpu

F32 = jnp.float32
BF16 = jnp.bfloat16

D_MODEL = 1024
ATTN_HEADS = 8
QK_NOPE_DIM = 64
QK_ROPE_DIM = 32
QK_HEAD_DIM = QK_NOPE_DIM + QK_ROPE_DIM
V_HEAD_DIM = 64
Q_LORA_RANK = D_MODEL // 4
KV_LORA_RANK = D_MODEL // 8
ROPE_THETA = 10000.0
ATTN_WIDTH = ATTN_HEADS * V_HEAD_DIM
SSM_HEADS = 8
SSM_HEAD_DIM = 64
SSM_INNER = SSM_HEADS * SSM_HEAD_DIM
SSM_GROUPS = 2
SSM_STATE = 128
SSM_CONV = 5
SSM_CHUNK = 128
SSM_CONV_CH = SSM_INNER + 2 * SSM_GROUPS * SSM_STATE
D_MIX = ATTN_WIDTH + SSM_INNER
D_FF = 4 * D_MODEL
EPS = 1e-6

LANES = 128
SUBLANES = 8
HEAD_PAD = LANES

HEADS_PER_GROUP = SSM_HEADS // SSM_GROUPS
GROUP_INNER = SSM_INNER // SSM_GROUPS
GROUP_COLS = GROUP_INNER + 2 * SSM_STATE
DT_ROWS = 2 * HEADS_PER_GROUP
BF16_ROWS = 16
SEG_K_PER_ROW = 8
SEG_K = DT_ROWS * SEG_K_PER_ROW
SEG_BLOCKS = DT_ROWS + HEADS_PER_GROUP
LOG2E = float(np.log2(np.e))
CONV_WIN = 2 * SSM_CHUNK
CONV_WIN_LEAD = SSM_CHUNK // 2
CONV_SHIFTED_TAPS = tuple(k for k in range(SSM_CONV) if k != SSM_CONV // 2)

COL_CKV = Q_LORA_RANK
COL_MISC = COL_CKV + KV_LORA_RANK
COL_Z = COL_MISC + LANES
COL_XBC = COL_Z + SSM_INNER
IN_COLS = COL_XBC + SSM_CONV_CH

TM_IN = 1024
IN_SUBTILE = 512
TQ = 512
ATTN_HEADS_PER_STEP = 8
TM_MLP = 512
FF_CHUNK = 1024
SSD_UNROLL = 8
CONV_AHEAD = 4
PASS_A_UNROLL = 4
WIN_PREP_STEPS = 4
VMEM_LIMIT = 56 * 1024 * 1024


def _inv_rms(x):
    n = x.shape[-1]
    x2 = x * x
    acc = x2[:, 0:LANES]
    for i in range(1, n // LANES):
        acc = acc + x2[:, i * LANES:(i + 1) * LANES]
    return lax.rsqrt(jnp.sum(acc, axis=-1, keepdims=True) * (1.0 / n) + EPS)


def _dot(a, b):
    return jnp.dot(a, b, preferred_element_type=F32)


def _dot_nt(a, b):
    return lax.dot_general(a, b, (((1,), (1,)), ((), ())), preferred_element_type=F32)


def _inproj_kernel(x_ref, cs_ref, g_ref, win_ref, gqa_ref, wuq_ref, gkva_ref, wuk_ref, wuv_ref, vone_ref,
                   gq_ref, gkn_ref, gkr_ref,
                   q_ref, k_ref, v_ref, z_ref, xbc_ref, dt_ref):
    for sub in range(x_ref.shape[0] // IN_SUBTILE):
        _inproj_rows(sub * IN_SUBTILE, x_ref, cs_ref, g_ref, win_ref, gqa_ref, wuq_ref, gkva_ref, wuk_ref,
                     wuv_ref, vone_ref, gq_ref, gkn_ref, gkr_ref, q_ref, k_ref, v_ref, z_ref, xbc_ref, dt_ref)


def _inproj_rows(r0, x_ref, cs_ref, g_ref, win_ref, gqa_ref, wuq_ref, gkva_ref, wuk_ref, wuv_ref, vone_ref,
                 gq_ref, gkn_ref, gkr_ref, q_ref, k_ref, v_ref, z_ref, xbc_ref, dt_ref):
    tm = IN_SUBTILE
    rs = slice(r0, r0 + tm)
    x = x_ref[rs, :]
    h = (x * _inv_rms(x) * g_ref[...]).astype(BF16)
    big = _dot(h, win_ref[...])
    z_ref[rs, :] = big[:, COL_Z:COL_XBC].astype(BF16)
    xbc_ref[rs, :] = big[:, COL_XBC:IN_COLS].astype(BF16)
    misc = big[:, COL_MISC:COL_Z]
    for c in range(tm // SSM_CHUNK):
        dt_ref[r0 // SSM_CHUNK + c] = misc[c * SSM_CHUNK:(c + 1) * SSM_CHUNK, :].T[0:2 * SSM_HEADS, :]

    cq = big[:, 0:COL_CKV]
    ckv = big[:, COL_CKV:COL_MISC]
    cqn = (cq * _inv_rms(cq) * gqa_ref[...]).astype(BF16)
    ckvn = (ckv * _inv_rms(ckv) * gkva_ref[...]).astype(BF16)
    q_pre = _dot(cqn, wuq_ref[...])
    k_pre = _dot(ckvn, wuk_ref[...])
    v_ref[rs, :] = (_dot(ckvn, wuv_ref[...]) + vone_ref[...]).astype(BF16)

    half = QK_ROPE_DIM // 2
    zpad = jnp.zeros((HEAD_PAD - QK_ROPE_DIM, SSM_CHUNK), F32)
    tt = jnp.concatenate(
        [jnp.concatenate([cs_ref[:, r0 + c * SSM_CHUNK:r0 + (c + 1) * SSM_CHUNK], zpad], axis=0).T
         for c in range(tm // SSM_CHUNK)], axis=0)
    lane_t = lax.broadcasted_iota(jnp.int32, (tm, HEAD_PAD), 1)
    cs = jnp.where(lane_t < QK_NOPE_DIM, 1.0,
                   jnp.where(lane_t < QK_NOPE_DIM + half, pltpu.roll(tt, QK_NOPE_DIM, 1),
                             jnp.where(lane_t < QK_HEAD_DIM + half, pltpu.roll(tt, QK_NOPE_DIM + half, 1),
                                       pltpu.roll(tt, QK_HEAD_DIM, 1))))
    lane = lax.broadcasted_iota(jnp.int32, (1, HEAD_PAD), 1)
    in_head = (lane < QK_HEAD_DIM).astype(F32)
    is_rope = ((lane >= QK_NOPE_DIM) & (lane < QK_HEAD_DIM)).astype(F32)
    inv_d = 1.0 / QK_HEAD_DIM

    gcq = gq_ref[...] * cs
    for hh in range(ATTN_HEADS):
        sl = slice(hh * HEAD_PAD, (hh + 1) * HEAD_PAD)
        qh = q_pre[:, sl]
        ssq = jnp.sum(qh * qh * in_head, axis=-1, keepdims=True)
        q_ref[rs, sl] = (qh * lax.rsqrt(ssq * inv_d + EPS) * gcq).astype(BF16)

    ab = misc * (gkr_ref[...] * cs)
    lane2 = lax.broadcasted_iota(jnp.int32, (tm, HEAD_PAD), 1)
    swapped = jnp.where(lane2 < QK_HEAD_DIM, pltpu.roll(ab, HEAD_PAD - QK_ROPE_DIM, 1),
                        pltpu.roll(ab, QK_ROPE_DIM, 1))
    s_both = jnp.where(lane2 >= QK_NOPE_DIM, ab + swapped, 0.0)
    ssq_pe = jnp.sum(misc * misc * is_rope, axis=-1, keepdims=True)
    gkn = gkn_ref[...]
    for hh in range(ATTN_HEADS):
        sl = slice(hh * HEAD_PAD, (hh + 1) * HEAD_PAD)
        kh = k_pre[:, sl]
        ssq = jnp.sum(kh * kh, axis=-1, keepdims=True) + ssq_pe
        k_ref[rs, sl] = ((kh * gkn + s_both) * lax.rsqrt(ssq * inv_d + EPS)).astype(BF16)


def _win_prep_kernel(w_ref, o_ref):
    w = w_ref[...]
    o_kpe = Q_LORA_RANK + KV_LORA_RANK
    o_z = o_kpe + QK_ROPE_DIM
    o_dt = o_z + SSM_INNER + SSM_CONV_CH
    half = QK_ROPE_DIM // 2
    hpg = HEADS_PER_GROUP
    dt_f = w[:, o_dt:o_dt + SSM_HEADS]
    dt_b = w[:, o_dt + SSM_HEADS:o_dt + 2 * SSM_HEADS]
    kpe = w[:, o_kpe:o_z]
    pieces = [w[:, 0:o_kpe]]
    for g in range(SSM_GROUPS):
        pieces += [dt_f[:, g * hpg:(g + 1) * hpg], dt_b[:, g * hpg:(g + 1) * hpg]]
    pieces += [jnp.zeros((w.shape[0], QK_NOPE_DIM - 2 * SSM_HEADS), F32), kpe, -kpe[:, half:], kpe[:, :half],
               w[:, o_z:o_dt]]
    o_ref[...] = jnp.concatenate(pieces, axis=1).astype(BF16)


def _win_prep_call(w_in3, layer):
    depth, d, n = w_in3.shape
    rows = d // WIN_PREP_STEPS
    return pl.pallas_call(
        _win_prep_kernel,
        grid=(WIN_PREP_STEPS,),
        in_specs=[pl.BlockSpec((rows, n), lambda i: (layer * WIN_PREP_STEPS + i, 0))],
        out_specs=pl.BlockSpec((rows, IN_COLS), lambda i: (i, 0)),
        out_shape=jax.ShapeDtypeStruct((d, IN_COLS), BF16),
        compiler_params=pltpu.CompilerParams(dimension_semantics=("arbitrary",), vmem_limit_bytes=VMEM_LIMIT),
        name="win_prep",
    )(w_in3.reshape(depth * d, n))


def _inproj_call(x2, cs, g, win, gqa, wuq, gkva, wuk, wuv, vone, gq, gkn, gkr):
    t = x2.shape[0]
    cpt = TM_IN // SSM_CHUNK
    full = lambda a: pl.BlockSpec(a.shape, lambda i: (0,) * a.ndim, pipeline_mode=pl.Buffered(1))
    row = lambda w: pl.BlockSpec((TM_IN, w), lambda i: (i, 0))
    return pl.pallas_call(
        _inproj_kernel,
        grid=(t // TM_IN,),
        in_specs=[row(D_MODEL), pl.BlockSpec((QK_ROPE_DIM, TM_IN), lambda i: (0, i)), full(g), full(win), full(gqa), full(wuq), full(gkva),
                  full(wuk), full(wuv), full(vone), full(gq), full(gkn), full(gkr)],
        out_specs=[row(ATTN_HEADS * HEAD_PAD), row(ATTN_HEADS * HEAD_PAD), row(ATTN_HEADS * HEAD_PAD),
                   row(SSM_INNER), row(SSM_CONV_CH),
                   pl.BlockSpec((cpt, 2 * SSM_HEADS, SSM_CHUNK), lambda i: (i, 0, 0))],
        out_shape=[jax.ShapeDtypeStruct((t, ATTN_HEADS * HEAD_PAD), BF16),
                   jax.ShapeDtypeStruct((t, ATTN_HEADS * HEAD_PAD), BF16),
                   jax.ShapeDtypeStruct((t, ATTN_HEADS * HEAD_PAD), BF16),
                   jax.ShapeDtypeStruct((t, SSM_INNER), BF16),
                   jax.ShapeDtypeStruct((t, SSM_CONV_CH), BF16),
                   jax.ShapeDtypeStruct((t // SSM_CHUNK, 2 * SSM_HEADS, SSM_CHUNK), F32)],
        compiler_params=pltpu.CompilerParams(dimension_semantics=("arbitrary",),
                                             vmem_limit_bytes=VMEM_LIMIT),
        name="inproj",
    )(x2, cs, g, win, gqa, wuq, gkva, wuk, wuv, vone, gq, gkn, gkr)


def _attn_kernel(q_ref, k_ref, v_ref, o_ref):
    lane = lax.broadcasted_iota(jnp.int32, (q_ref.shape[0], HEAD_PAD), 1)
    for jp in range(ATTN_HEADS_PER_STEP // 2):
        sls = [slice(j * HEAD_PAD, (j + 1) * HEAD_PAD) for j in (2 * jp, 2 * jp + 1)]
        ss = [_dot_nt(q_ref[:, sl], k_ref[:, sl]) for sl in sls]
        ps = [jnp.exp2(s - jnp.max(s, axis=-1, keepdims=True)).astype(BF16) for s in ss]
        accs = [_dot(p, v_ref[:, sl]) for p, sl in zip(ps, sls)]
        res = [acc / pltpu.roll(acc, V_HEAD_DIM, 1) for acc in accs]
        o_ref[:, jp * HEAD_PAD:(jp + 1) * HEAD_PAD] = jnp.where(lane < V_HEAD_DIM, res[0], res[1])


def _attn_call(q, k, v, batch, seq):
    n_q = seq // TQ
    hps = ATTN_HEADS_PER_STEP
    return pl.pallas_call(
        _attn_kernel,
        grid=(batch, ATTN_HEADS // hps, n_q),
        in_specs=[pl.BlockSpec((TQ, hps * HEAD_PAD), lambda b, hp, i: (b * n_q + i, hp)),
                  pl.BlockSpec((seq, hps * HEAD_PAD), lambda b, hp, i: (b, hp)),
                  pl.BlockSpec((seq, hps * HEAD_PAD), lambda b, hp, i: (b, hp))],
        out_specs=pl.BlockSpec((TQ, hps * V_HEAD_DIM), lambda b, hp, i: (b * n_q + i, hp)),
        out_shape=jax.ShapeDtypeStruct((batch * seq, ATTN_WIDTH), F32),
        compiler_params=pltpu.CompilerParams(
            dimension_semantics=("arbitrary", "arbitrary", "arbitrary"),
            vmem_limit_bytes=VMEM_LIMIT),
        name="attention",
    )(q, k, v)


def _split3(x):
    hi = x.astype(BF16)
    r1 = x - hi.astype(F32)
    mid = r1.astype(BF16)
    lo = (r1 - mid.astype(F32)).astype(BF16)
    return hi, mid, lo


def _ssd_kernel(x_ref, b_ref, c_ref, dt_ref, sh_ref, place_ref, lhs_ones_ref, rhs_const_ref,
                cwx_ref, cwb_ref, cwc_ref, cbx_ref, cbb_ref, cbc_ref, alog_ref, dtbias_ref, dskip_ref,
                o_ref, xc_scr, xm_scr, nsf_scr, nsb_scr, dtv_scr, w_scr, dec_scr, lhs_scr, rhs_scr, rpart_scr):
    nc = dt_ref.shape[0]
    seq = x_ref.shape[0]
    L = SSM_CHUNK
    hpg = HEADS_PER_GROUP
    npair = hpg // 2
    P = SSM_HEAD_DIM
    x_cols = slice(0, GROUP_INNER)
    b_cols = slice(GROUP_INNER, GROUP_INNER + SSM_STATE)
    c_cols = slice(GROUP_INNER + SSM_STATE, GROUP_COLS)
    pair_cols = [slice(j * L, (j + 1) * L) for j in range(npair)]

    @pl.when((pl.program_id(0) == 0) & (pl.program_id(1) == 0))
    def _init_rhs():
        for u in range(SSD_UNROLL):
            rhs_scr[u] = rhs_const_ref[...]

    a_neg = -jnp.exp(alog_ref[...]) * LOG2E
    dtv = jax.nn.softplus(dt_ref[...] + dtbias_ref[...][None])
    dtv_scr[...] = dtv
    da2 = (dtv * a_neg[None]).reshape(nc * DT_ROWS, L)
    ri = lax.broadcasted_iota(jnp.int32, (L, L), 0)
    ci = lax.broadcasted_iota(jnp.int32, (L, L), 1)
    upper = (ri <= ci).astype(BF16)
    lower = (ri >= ci).astype(BF16)
    tri = jnp.concatenate([upper, lower], axis=1)
    cs_fb = sum(_dot(p, tri) for p in _split3(da2))
    rowsel = (ri & hpg) == 0
    cs2 = jnp.where(rowsel, cs_fb[:, 0:L], cs_fb[:, L:2 * L])
    colcs = cs2.T

    col_parts = jnp.concatenate(_split3(colcs), axis=1)
    half_n = place_ref.shape[1] // 2
    for hf in range(2):
        cols = slice(hf * half_n, (hf + 1) * half_n)
        lhs_half = _dot(col_parts, place_ref[:, cols]) + lhs_ones_ref[:, cols]
        for c in range(nc // 2):
            lhs_scr[hf * (nc // 2) + c] = lhs_half[:, c * L:c * L + SEG_K].astype(BF16)
    for i, part in enumerate(_split3(-cs2)):
        rpart_scr[i] = part.astype(F32).reshape(nc, DT_ROWS, L)

    def lane_bcast(col):
        return jnp.broadcast_to(col, (col.shape[0], L))

    tot2 = jnp.where(rowsel, lane_bcast(cs2[:, L - 1:L]), lane_bcast(cs2[:, 0:1]))
    dec_scr[...] = jnp.exp2(tot2).reshape(nc, DT_ROWS, L)
    w_scr[...] = (dtv.reshape(nc * DT_ROWS, L) * jnp.exp2(tot2 - cs2)).reshape(nc, DT_ROWS, L)

    lane_t = lax.broadcasted_iota(jnp.int32, (L, L), 1)
    low_half = lane_t < P

    cw_halves = (cwx_ref[...], jnp.concatenate([cwb_ref[...], cwc_ref[...]], axis=1))
    cb_halves = (cbx_ref[...], jnp.concatenate([cbb_ref[...], cbc_ref[...]], axis=1))

    def rows(hf, start, size):
        if hf == 0:
            return x_ref[pl.ds(start, size), :]
        return jnp.concatenate([b_ref[pl.ds(start, size), :], c_ref[pl.ds(start, size), :]], axis=1)

    def conv_stage(c):
        base = pl.multiple_of(c * L, L)
        ws = pl.multiple_of(jnp.clip(c * L - CONV_WIN_LEAD, 0, seq - CONV_WIN), CONV_WIN_LEAD)
        variant = jnp.where(c == 0, 0, jnp.where(c == nc - 1, 2, 1))
        sh = sh_ref[variant]
        halves = []
        for hf in range(2):
            cw = cw_halves[hf]
            shifted = _dot(sh, rows(hf, ws, CONV_WIN))
            acc = cb_halves[hf] + cw[SSM_CONV // 2:SSM_CONV // 2 + 1, :] * rows(hf, base, L).astype(F32)
            for t, kk in enumerate(CONV_SHIFTED_TAPS):
                acc = acc + cw[kk:kk + 1, :] * shifted[t * L:(t + 1) * L, :]
            halves.append(acc * jax.nn.sigmoid(acc))
        xc = jnp.concatenate(halves, axis=1)
        xc_scr[c] = xc
        for j in range(npair):
            xp = xc[:, pair_cols[j]]
            xm_scr[c, j] = jnp.concatenate([jnp.where(low_half, xp, 0.0), jnp.where(low_half, 0.0, xp)],
                                           axis=0).astype(BF16)

    def state_loads(c):
        return xc_scr[c, :, b_cols], [xm_scr[c, j] for j in range(npair)]

    def state_stage(c, loaded):
        b_tok, xm = loaded
        bt = b_tok.T
        for d, ns_scr in ((0, nsf_scr), (1, nsb_scr)):
            w = w_scr[c, d * hpg:(d + 1) * hpg, :]
            ns_scr[c] = jnp.concatenate(
                [_dot(jnp.concatenate([(bt * w[2 * j:2 * j + 1, :]).astype(BF16),
                                       (bt * w[2 * j + 1:2 * j + 2, :]).astype(BF16)], axis=1), xm[j])
                 for j in range(npair)], axis=1)

    for c in range(CONV_AHEAD):
        conv_stage(c)

    def pass_a(c, carry):
        loaded = state_loads(c)
        conv_stage(c + CONV_AHEAD)
        state_stage(c, loaded)
        return carry
    lax.fori_loop(0, nc - CONV_AHEAD, pass_a, 0, unroll=PASS_A_UNROLL)
    for c in range(nc - CONV_AHEAD, nc):
        state_stage(c, state_loads(c))

    lane_r = lax.broadcasted_iota(jnp.int32, (1, L), 1)

    def decay_row(c, d):
        dec = dec_scr[c, d * hpg:(d + 1) * hpg, :]
        return jnp.concatenate([jnp.where(lane_r < P, dec[2 * j:2 * j + 1, :], dec[2 * j + 1:2 * j + 2, :])
                                for j in range(npair)], axis=1)

    def pass_b(d, ns_scr):
        def body(i, st):
            c = i if d == 0 else nc - 1 - i
            new = ns_scr[c]
            ns_scr[c] = st
            return st * decay_row(c, d) + new
        lax.fori_loop(0, nc, body, jnp.zeros((SSM_STATE, GROUP_INNER), F32))
    pass_b(0, nsf_scr)
    pass_b(1, nsb_scr)

    mask_f = ci <= ri
    mask_b = ci >= ri
    neg_inf = jnp.float32(-jnp.inf)
    sub16 = lax.broadcasted_iota(jnp.int32, (BF16_ROWS, L), 0)

    def pass_c(c, carry):
        base = pl.multiple_of(c * L, L)
        xc = xc_scr[c]
        bm = xc[:, b_cols].astype(BF16)
        cm = xc[:, c_cols].astype(BF16)
        cb = _dot_nt(cm, bm)
        off_f = _dot(cm, nsf_scr[c].astype(BF16))
        off_b = _dot(cm, nsb_scr[c].astype(BF16))
        slot = c % SSD_UNROLL
        for hd in range(DT_ROWS):
            r0 = (hd % 2) * SEG_K_PER_ROW
            blk = jnp.where((sub16 >= r0) & (sub16 < r0 + 3), 1.0, 0.0)
            for i in range(3):
                blk = jnp.where(sub16 == r0 + 3 + i, rpart_scr[i, c, hd:hd + 1, :], blk)
            rhs_scr[slot, (hd // 2) * BF16_ROWS:(hd // 2 + 1) * BF16_ROWS, hd * L:(hd + 1) * L] = blk.astype(BF16)
        seg = _dot(lhs_scr[c], rhs_scr[slot])
        dt_f_r = dtv_scr[c, 0:hpg, :]
        dt_b_r = dtv_scr[c, hpg:DT_ROWS, :]
        ys = []
        for j in range(npair):
            ws_ = []
            for h in (2 * j, 2 * j + 1):
                ef = jnp.exp2(jnp.where(mask_f, seg[:, h * L:(h + 1) * L], neg_inf))
                eb = jnp.exp2(jnp.where(mask_b, seg[:, (hpg + h) * L:(hpg + h + 1) * L], neg_inf))
                ws_.append((cb * (ef * dt_f_r[h:h + 1, :] + eb * dt_b_r[h:h + 1, :])).astype(BF16))
            scale_f = jnp.exp2(seg[:, (DT_ROWS + j) * L:(DT_ROWS + j + 1) * L])
            scale_b = jnp.exp2(seg[:, (DT_ROWS + npair + j) * L:(DT_ROWS + npair + j + 1) * L])
            ys.append(_dot(jnp.concatenate(ws_, axis=1), xm_scr[c, j])
                      + scale_f * off_f[:, pair_cols[j]] + scale_b * off_b[:, pair_cols[j]])
        y = jnp.concatenate(ys, axis=1) + dskip_ref[...] * xc[:, x_cols]
        o_ref[pl.ds(base, L), :] = y
        return carry
    lax.fori_loop(0, nc, pass_c, 0, unroll=SSD_UNROLL)


def _conv_shift_matrices():
    out = np.zeros((3, len(CONV_SHIFTED_TAPS) * SSM_CHUNK, CONV_WIN), np.float32)
    for v, lead in enumerate((0, CONV_WIN_LEAD, CONV_WIN - SSM_CHUNK)):
        for i, kk in enumerate(CONV_SHIFTED_TAPS):
            for t in range(SSM_CHUNK):
                j = t + kk - SSM_CONV // 2 + lead
                if 0 <= j < CONV_WIN:
                    out[v, i * SSM_CHUNK + t, j] = 1.0
    return out


def _seg_matmul_constants(nc):
    L, K, half = SSM_CHUNK, SEG_K_PER_ROW, SSM_HEAD_DIM
    place = np.zeros((3 * L, nc * L), np.float32)
    lhs_ones = np.zeros((1, nc * L), np.float32)
    for c in range(nc):
        for hd in range(DT_ROWS):
            for i in range(3):
                place[i * L + c * DT_ROWS + hd, c * L + hd * K + i] = 1.0
                lhs_ones[0, c * L + hd * K + 3 + i] = 1.0
    rhs = np.zeros((SEG_K, SEG_BLOCKS * L), np.float32)
    for hd in range(DT_ROWS):
        d, h = divmod(hd, HEADS_PER_GROUP)
        pair_block = DT_ROWS + d * (HEADS_PER_GROUP // 2) + h // 2
        lanes = slice(0, half) if h % 2 == 0 else slice(half, L)
        rhs[hd * K:hd * K + 3, hd * L:(hd + 1) * L] = 1.0
        rhs[hd * K:hd * K + 3, pair_block * L:(pair_block + 1) * L][:, lanes] = 1.0
    return place, lhs_ones, rhs


def _ssd_call(xbc, dtc, cw, cbias, alog, dtbias, dskip, batch, seq):
    nc = seq // SSM_CHUNK
    L = SSM_CHUNK
    assert nc >= 3 and seq >= CONV_WIN and nc * DT_ROWS == L
    sh = jnp.asarray(_conv_shift_matrices(), BF16)
    place, lhs_ones, rhs_const = _seg_matmul_constants(nc)
    place = jnp.asarray(place, BF16)
    lhs_ones = jnp.asarray(lhs_ones, F32)
    rhs_const = jnp.asarray(rhs_const, BF16)
    const = lambda a: pl.BlockSpec(a.shape, lambda b, g: (0,) * a.ndim)
    b_blk0 = SSM_INNER // SSM_STATE
    c_blk0 = b_blk0 + SSM_GROUPS

    def xbc_views(rows_, batched):
        lead = (lambda b: b) if batched else (lambda b: 0)
        return [pl.BlockSpec((rows_, GROUP_INNER), lambda b, g: (lead(b), g)),
                pl.BlockSpec((rows_, SSM_STATE), lambda b, g: (lead(b), b_blk0 + g)),
                pl.BlockSpec((rows_, SSM_STATE), lambda b, g: (lead(b), c_blk0 + g))]
    return pl.pallas_call(
        _ssd_kernel,
        grid=(batch, SSM_GROUPS),
        in_specs=xbc_views(seq, True) + [
                  pl.BlockSpec((nc, DT_ROWS, L), lambda b, g: (b, g, 0)),
                  const(sh), const(place), const(lhs_ones), const(rhs_const)]
                 + xbc_views(SUBLANES, False) + xbc_views(1, False) + [
                  pl.BlockSpec((DT_ROWS, L), lambda b, g: (g, 0)),
                  pl.BlockSpec((DT_ROWS, L), lambda b, g: (g, 0)),
                  pl.BlockSpec((1, GROUP_INNER), lambda b, g: (0, g))],
        out_specs=pl.BlockSpec((seq, GROUP_INNER), lambda b, g: (b, g)),
        out_shape=jax.ShapeDtypeStruct((batch * seq, SSM_INNER), F32),
        scratch_shapes=[pltpu.VMEM((nc, L, GROUP_COLS), F32),
                        pltpu.VMEM((nc, HEADS_PER_GROUP // 2, 2 * L, L), BF16),
                        pltpu.VMEM((nc, SSM_STATE, GROUP_INNER), F32),
                        pltpu.VMEM((nc, SSM_STATE, GROUP_INNER), F32),
                        pltpu.VMEM((nc, DT_ROWS, L), F32),
                        pltpu.VMEM((nc, DT_ROWS, L), F32),
                        pltpu.VMEM((nc, DT_ROWS, L), F32),
                        pltpu.VMEM((nc, L, SEG_K), BF16),
                        pltpu.VMEM((SSD_UNROLL, SEG_K, SEG_BLOCKS * L), BF16),
                        pltpu.VMEM((3, nc, DT_ROWS, L), F32)],
        compiler_params=pltpu.CompilerParams(dimension_semantics=("arbitrary", "arbitrary"),
                                             vmem_limit_bytes=VMEM_LIMIT),
        name="ssd",
    )(xbc, xbc, xbc, dtc, sh, place, lhs_ones, rhs_const, cw, cw, cw, cbias, cbias, cbias,
      alog, dtbias, dskip)


def _mlp_kernel(x_ref, attn_ref, ssm_ref, z_ref, gat_ref, gn_ref, wo_ref, gm_ref, wup_ref, wdn_ref, o_ref):
    a = attn_ref[...]
    an = (a * _inv_rms(a) * gat_ref[...]).astype(BF16)
    zc = z_ref[...].astype(F32)
    y = ssm_ref[...] * (zc * jax.nn.sigmoid(zc))
    yn = jnp.concatenate([y[:, g * GROUP_INNER:(g + 1) * GROUP_INNER]
                          * _inv_rms(y[:, g * GROUP_INNER:(g + 1) * GROUP_INNER]) for g in range(SSM_GROUPS)],
                         axis=1) * gn_ref[...]
    mix = _dot(an, wo_ref[0:ATTN_WIDTH, :]) + _dot(yn.astype(BF16), wo_ref[ATTN_WIDTH:D_MIX, :])
    x1 = x_ref[...] + mix
    hm = (x1 * _inv_rms(x1) * gm_ref[...]).astype(BF16)
    acc = jnp.zeros_like(x1)
    for c in range(D_FF // FF_CHUNK):
        cols = slice(c * FF_CHUNK, (c + 1) * FF_CHUNK)
        u = _dot(hm, wup_ref[:, cols])
        acc = acc + _dot(jnp.square(jnp.maximum(u, 0.0)).astype(BF16), wdn_ref[cols, :])
    o_ref[...] = x1 + acc


def _mlp_call(x2, attn, ssm, z, gat, gn, wo, gm, wup, wdn):
    t = x2.shape[0]
    row = lambda w: pl.BlockSpec((TM_MLP, w), lambda i: (i, 0))
    full = lambda a: pl.BlockSpec(a.shape, lambda i: (0, 0), pipeline_mode=pl.Buffered(1))
    return pl.pallas_call(
        _mlp_kernel,
        grid=(t // TM_MLP,),
        in_specs=[row(D_MODEL), row(ATTN_WIDTH), row(SSM_INNER), row(SSM_INNER), full(gat), full(gn), full(wo),
                  full(gm), full(wup), full(wdn)],
        out_specs=row(D_MODEL),
        out_shape=jax.ShapeDtypeStruct((t, D_MODEL), F32),
        compiler_params=pltpu.CompilerParams(dimension_semantics=("arbitrary",),
                                             vmem_limit_bytes=VMEM_LIMIT),
        name="outproj_mlp",
    )(x2, attn, ssm, z, gat, gn, wo, gm, wup, wdn)


def _group_major(fwd, bwd):
    lead = fwd.shape[:-1]
    both = jnp.stack([fwd.reshape(lead + (SSM_GROUPS, HEADS_PER_GROUP)),
                      bwd.reshape(lead + (SSM_GROUPS, HEADS_PER_GROUP))], axis=-2)
    return both.reshape(lead + (2 * SSM_HEADS,))


def _rotate_half(w):
    half = QK_ROPE_DIM // 2
    return jnp.concatenate([-w[..., half:], w[..., :half]], axis=-1)


def _swap_halves(g):
    half = QK_ROPE_DIM // 2
    return jnp.concatenate([g[..., half:], g[..., :half]], axis=-1)


def _lane_bcast(v):
    return jnp.broadcast_to(v[..., None], v.shape + (LANES,)).astype(F32)


def _layer(x2, cs, batch, seq, ln_mix_g, w_in, q_a_norm_g, w_uq, kv_a_norm_g, w_ukv, q_norm_g,
           k_norm_g, attn_out_norm_g, conv_w, conv_b, a_log_fwd, a_log_bwd, dt_bias_fwd, dt_bias_bwd,
           d_skip, ssm_norm_g, w_out, ln_mlp_g, w_mlp_up, w_mlp_down):
    win = _win_prep_call(*w_in)
    wq = w_uq.reshape(Q_LORA_RANK, ATTN_HEADS, QK_HEAD_DIM)
    wuq = jnp.concatenate([wq, _rotate_half(wq[..., QK_NOPE_DIM:])], axis=-1)
    wuq = wuq.reshape(Q_LORA_RANK, ATTN_HEADS * HEAD_PAD).astype(BF16)
    wukv = w_ukv.reshape(KV_LORA_RANK, ATTN_HEADS, QK_NOPE_DIM + V_HEAD_DIM)
    wuk = jnp.pad(wukv[:, :, :QK_NOPE_DIM],
                  ((0, 0), (0, 0), (0, HEAD_PAD - QK_NOPE_DIM))).reshape(KV_LORA_RANK, -1).astype(BF16)
    wv = wukv[:, :, QK_NOPE_DIM:].reshape(KV_LORA_RANK, ATTN_HEADS // 2, 2, V_HEAD_DIM)
    zv = jnp.zeros_like(wv[:, :, 0])
    wuv = jnp.stack([jnp.concatenate([wv[:, :, 0], zv], axis=-1),
                     jnp.concatenate([zv, wv[:, :, 1]], axis=-1)], axis=2).reshape(KV_LORA_RANK, -1).astype(BF16)
    ones_v = np.ones((ATTN_HEADS // 2, V_HEAD_DIM), np.float32)
    vone = jnp.asarray(np.stack([np.concatenate([0 * ones_v, ones_v], axis=-1),
                                 np.concatenate([ones_v, 0 * ones_v], axis=-1)], axis=1).reshape(1, -1))
    scale = QK_HEAD_DIM ** -0.5 * np.log2(np.e)
    gq = (jnp.concatenate([q_norm_g, _swap_halves(q_norm_g[QK_NOPE_DIM:])]) * scale)[None, :]
    zeros_nope = jnp.zeros((QK_NOPE_DIM,), k_norm_g.dtype)
    gkn = jnp.concatenate([k_norm_g[:QK_NOPE_DIM], zeros_nope])[None, :]
    gkr = jnp.concatenate([zeros_nope, k_norm_g[QK_NOPE_DIM:], _swap_halves(k_norm_g[QK_NOPE_DIM:])])[None, :]

    q, k, v, z, xbc, dtc = _inproj_call(
        x2, cs, ln_mix_g[None, :], win, q_a_norm_g[None, :], wuq, kv_a_norm_g[None, :], wuk, wuv, vone,
        gq, gkn, gkr)

    attn = _attn_call(q, k, v, batch, seq)

    cw = jnp.pad(conv_w[:, 0, :], ((0, SUBLANES - SSM_CONV), (0, 0)))
    cbias = conv_b[None, :]
    alog = _lane_bcast(_group_major(a_log_fwd, a_log_bwd))
    dtbias = _lane_bcast(_group_major(dt_bias_fwd, dt_bias_bwd))
    dskip = jnp.repeat(d_skip, SSM_HEAD_DIM)[None, :]
    ssm = _ssd_call(xbc, dtc, cw, cbias, alog, dtbias, dskip, batch, seq)

    return _mlp_call(x2, attn, ssm, z, attn_out_norm_g[None, :], ssm_norm_g[None, :], w_out.astype(BF16),
                     ln_mlp_g[None, :], w_mlp_up.astype(BF16), w_mlp_down.astype(BF16))


def _rope_table(positions):
    inv_freq = 1.0 / (ROPE_THETA ** (jnp.arange(0, QK_ROPE_DIM, 2, dtype=F32) / QK_ROPE_DIM))
    ang = inv_freq[:, None] * positions.astype(F32).reshape(1, -1)
    return jnp.concatenate([jnp.cos(ang), jnp.sin(ang)], axis=0)


def kernel(x, positions, ln_mix_g, w_in, q_a_norm_g, w_uq, kv_a_norm_g, w_ukv, q_norm_g, k_norm_g,
           attn_out_norm_g, conv_w, conv_b, a_log_fwd, a_log_bwd, dt_bias_fwd, dt_bias_bwd, d_skip,
           ssm_norm_g, w_out, ln_mlp_g, w_mlp_up, w_mlp_down):
    batch, seq, d = x.shape
    assert d == D_MODEL and seq % TQ == 0 and (batch * seq) % TM_IN == 0 and (batch * seq) % TM_MLP == 0
    cs = _rope_table(positions)
    x2 = x.reshape(batch * seq, d)
    for l in range(ln_mix_g.shape[0]):
        x2 = _layer(x2, cs, batch, seq, ln_mix_g[l], (w_in, l), q_a_norm_g[l], w_uq[l], kv_a_norm_g[l],
                    w_ukv[l], q_norm_g[l], k_norm_g[l], attn_out_norm_g[l], conv_w[l], conv_b[l],
                    a_log_fwd[l], a_log_bwd[l], dt_bias_fwd[l], dt_bias_bwd[l], d_skip[l], ssm_norm_g[l],
                    w_out[l], ln_mlp_g[l], w_mlp_up[l], w_mlp_down[l])
    return x2.reshape(batch, seq, d)
```

```python
import numpy as np
import jax
import jax.numpy as jnp
from jax import lax
from jax.experimental import pallas as pl
from jax.experimental.pallas import tpu as pltpu

F32 = jnp.float32
BF16 = jnp.bfloat16

D_MODEL = 1024
ATTN_HEADS = 8
QK_NOPE_DIM = 64
QK_ROPE_DIM = 32
QK_HEAD_DIM = QK_NOPE_DIM + QK_ROPE_DIM
V_HEAD_DIM = 64
Q_LORA_RANK = D_MODEL // 4
KV_LORA_RANK = D_MODEL // 8
ROPE_THETA = 10000.0
ATTN_WIDTH = ATTN_HEADS * V_HEAD_DIM
SSM_HEADS = 8
SSM_HEAD_DIM = 64
SSM_INNER = SSM_HEADS * SSM_HEAD_DIM
SSM_GROUPS = 2
SSM_STATE = 128
SSM_CONV = 5
SSM_CHUNK = 128
SSM_CONV_CH = SSM_INNER + 2 * SSM_GROUPS * SSM_STATE
D_MIX = ATTN_WIDTH + SSM_INNER
D_FF = 4 * D_MODEL
EPS = 1e-6

LANES = 128
SUBLANES = 8
HEAD_PAD = LANES

HEADS_PER_GROUP = SSM_HEADS // SSM_GROUPS
GROUP_INNER = SSM_INNER // SSM_GROUPS
GROUP_COLS = GROUP_INNER + 2 * SSM_STATE
DT_ROWS = 2 * HEADS_PER_GROUP
BF16_ROWS = 16
SEG_K_PER_ROW = 8
SEG_K = DT_ROWS * SEG_K_PER_ROW
SEG_BLOCKS = DT_ROWS + HEADS_PER_GROUP
LOG2E = float(np.log2(np.e))
CONV_WIN = 2 * SSM_CHUNK
CONV_WIN_LEAD = SSM_CHUNK // 2
CONV_SHIFTED_TAPS = tuple(k for k in range(SSM_CONV) if k != SSM_CONV // 2)

COL_CKV = Q_LORA_RANK
COL_MISC = COL_CKV + KV_LORA_RANK
COL_Z = COL_MISC + LANES
COL_XBC = COL_Z + SSM_INNER
IN_COLS = COL_XBC + SSM_CONV_CH

TM_IN = 1024
IN_SUBTILE = 512
TQ = 1024
ATTN_HEADS_PER_STEP = 8
TM_MLP = 512
FF_CHUNK = 1024
SSD_UNROLL = 8
CONV_AHEAD = 4
PASS_A_UNROLL = 4
WIN_PREP_STEPS = 4
VMEM_LIMIT = 56 * 1024 * 1024


def _inv_rms(x):
    n = x.shape[-1]
    x2 = x * x
    acc = x2[:, 0:LANES]
    for i in range(1, n // LANES):
        acc = acc + x2[:, i * LANES:(i + 1) * LANES]
    return lax.rsqrt(jnp.sum(acc, axis=-1, keepdims=True) * (1.0 / n) + EPS)


def _dot(a, b):
    return jnp.dot(a, b, preferred_element_type=F32)


def _dot_nt(a, b):
    return lax.dot_general(a, b, (((1,), (1,)), ((), ())), preferred_element_type=F32)


def _inproj_kernel(x_ref, cs_ref, g_ref, win_ref, gqa_ref, wuq_ref, gkva_ref, wuk_ref, wuv_ref, vone_ref,
                   gq_ref, gkn_ref, gkr_ref,
                   q_ref, k_ref, v_ref, z_ref, xbc_ref, dt_ref):
    for sub in range(x_ref.shape[0] // IN_SUBTILE):
        _inproj_rows(sub * IN_SUBTILE, x_ref, cs_ref, g_ref, win_ref, gqa_ref, wuq_ref, gkva_ref, wuk_ref,
                     wuv_ref, vone_ref, gq_ref, gkn_ref, gkr_ref, q_ref, k_ref, v_ref, z_ref, xbc_ref, dt_ref)


def _inproj_rows(r0, x_ref, cs_ref, g_ref, win_ref, gqa_ref, wuq_ref, gkva_ref, wuk_ref, wuv_ref, vone_ref,
                 gq_ref, gkn_ref, gkr_ref, q_ref, k_ref, v_ref, z_ref, xbc_ref, dt_ref):
    tm = IN_SUBTILE
    rs = slice(r0, r0 + tm)
    x = x_ref[rs, :]
    h = (x * _inv_rms(x) * g_ref[...]).astype(BF16)
    big = _dot(h, win_ref[...])
    z_ref[rs, :] = big[:, COL_Z:COL_XBC].astype(BF16)
    xbc_ref[rs, :] = big[:, COL_XBC:IN_COLS].astype(BF16)
    misc = big[:, COL_MISC:COL_Z]
    for c in range(tm // SSM_CHUNK):
        dt_ref[r0 // SSM_CHUNK + c] = misc[c * SSM_CHUNK:(c + 1) * SSM_CHUNK, :].T[0:2 * SSM_HEADS, :]

    cq = big[:, 0:COL_CKV]
    ckv = big[:, COL_CKV:COL_MISC]
    cqn = (cq * _inv_rms(cq) * gqa_ref[...]).astype(BF16)
    ckvn = (ckv * _inv_rms(ckv) * gkva_ref[...]).astype(BF16)
    q_pre = _dot(cqn, wuq_ref[...])
    k_pre = _dot(ckvn, wuk_ref[...])
    v_ref[rs, :] = (_dot(ckvn, wuv_ref[...]) + vone_ref[...]).astype(BF16)

    half = QK_ROPE_DIM // 2
    zpad = jnp.zeros((HEAD_PAD - QK_ROPE_DIM, SSM_CHUNK), F32)
    tt = jnp.concatenate(
        [jnp.concatenate([cs_ref[:, r0 + c * SSM_CHUNK:r0 + (c + 1) * SSM_CHUNK], zpad], axis=0).T
         for c in range(tm // SSM_CHUNK)], axis=0)
    lane_t = lax.broadcasted_iota(jnp.int32, (tm, HEAD_PAD), 1)
    cs = jnp.where(lane_t < QK_NOPE_DIM, 1.0,
                   jnp.where(lane_t < QK_NOPE_DIM + half, pltpu.roll(tt, QK_NOPE_DIM, 1),
                             jnp.where(lane_t < QK_HEAD_DIM + half, pltpu.roll(tt, QK_NOPE_DIM + half, 1),
                                       pltpu.roll(tt, QK_HEAD_DIM, 1))))
    lane = lax.broadcasted_iota(jnp.int32, (1, HEAD_PAD), 1)
    in_head = (lane < QK_HEAD_DIM).astype(F32)
    is_rope = ((lane >= QK_NOPE_DIM) & (lane < QK_HEAD_DIM)).astype(F32)
    inv_d = 1.0 / QK_HEAD_DIM

    gcq = gq_ref[...] * cs
    for hh in range(ATTN_HEADS):
        sl = slice(hh * HEAD_PAD, (hh + 1) * HEAD_PAD)
        qh = q_pre[:, sl]
        ssq = jnp.sum(qh * qh * in_head, axis=-1, keepdims=True)
        q_ref[rs, sl] = (qh * lax.rsqrt(ssq * inv_d + EPS) * gcq).astype(BF16)

    ab = misc * (gkr_ref[...] * cs)
    lane2 = lax.broadcasted_iota(jnp.int32, (tm, HEAD_PAD), 1)
    swapped = jnp.where(lane2 < QK_HEAD_DIM, pltpu.roll(ab, HEAD_PAD - QK_ROPE_DIM, 1),
                        pltpu.roll(ab, QK_ROPE_DIM, 1))
    s_both = jnp.where(lane2 >= QK_NOPE_DIM, ab + swapped, 0.0)
    ssq_pe = jnp.sum(misc * misc * is_rope, axis=-1, keepdims=True)
    gkn = gkn_ref[...]
    for hh in range(ATTN_HEADS):
        sl = slice(hh * HEAD_PAD, (hh + 1) * HEAD_PAD)
        kh = k_pre[:, sl]
        ssq = jnp.sum(kh * kh, axis=-1, keepdims=True) + ssq_pe
        k_ref[rs, sl] = ((kh * gkn + s_both) * lax.rsqrt(ssq * inv_d + EPS)).astype(BF16)


def _win_prep_kernel(w_ref, o_ref):
    w = w_ref[...]
    o_kpe = Q_LORA_RANK + KV_LORA_RANK
    o_z = o_kpe + QK_ROPE_DIM
    o_dt = o_z + SSM_INNER + SSM_CONV_CH
    half = QK_ROPE_DIM // 2
    hpg = HEADS_PER_GROUP
    dt_f = w[:, o_dt:o_dt + SSM_HEADS]
    dt_b = w[:, o_dt + SSM_HEADS:o_dt + 2 * SSM_HEADS]
    kpe = w[:, o_kpe:o_z]
    pieces = [w[:, 0:o_kpe]]
    for g in range(SSM_GROUPS):
        pieces += [dt_f[:, g * hpg:(g + 1) * hpg], dt_b[:, g * hpg:(g + 1) * hpg]]
    pieces += [jnp.zeros((w.shape[0], QK_NOPE_DIM - 2 * SSM_HEADS), F32), kpe, -kpe[:, half:], kpe[:, :half],
               w[:, o_z:o_dt]]
    o_ref[...] = jnp.concatenate(pieces, axis=1).astype(BF16)


def _win_prep_call(w_in3, layer):
    depth, d, n = w_in3.shape
    rows = d // WIN_PREP_STEPS
    return pl.pallas_call(
        _win_prep_kernel,
        grid=(WIN_PREP_STEPS,),
        in_specs=[pl.BlockSpec((rows, n), lambda i: (layer * WIN_PREP_STEPS + i, 0))],
        out_specs=pl.BlockSpec((rows, IN_COLS), lambda i: (i, 0)),
        out_shape=jax.ShapeDtypeStruct((d, IN_COLS), BF16),
        compiler_params=pltpu.CompilerParams(dimension_semantics=("arbitrary",), vmem_limit_bytes=VMEM_LIMIT),
        name="win_prep",
    )(w_in3.reshape(depth * d, n))


def _inproj_call(x2, cs, g, win, gqa, wuq, gkva, wuk, wuv, vone, gq, gkn, gkr):
    t = x2.shape[0]
    cpt = TM_IN // SSM_CHUNK
    full = lambda a: pl.BlockSpec(a.shape, lambda i: (0,) * a.ndim, pipeline_mode=pl.Buffered(1))
    row = lambda w: pl.BlockSpec((TM_IN, w), lambda i: (i, 0))
    return pl.pallas_call(
        _inproj_kernel,
        grid=(t // TM_IN,),
        in_specs=[row(D_MODEL), pl.BlockSpec((QK_ROPE_DIM, TM_IN), lambda i: (0, i)), full(g), full(win), full(gqa), full(wuq), full(gkva),
                  full(wuk), full(wuv), full(vone), full(gq), full(gkn), full(gkr)],
        out_specs=[row(ATTN_HEADS * HEAD_PAD), row(ATTN_HEADS * HEAD_PAD), row(ATTN_HEADS * HEAD_PAD),
                   row(SSM_INNER), row(SSM_CONV_CH),
                   pl.BlockSpec((cpt, 2 * SSM_HEADS, SSM_CHUNK), lambda i: (i, 0, 0))],
        out_shape=[jax.ShapeDtypeStruct((t, ATTN_HEADS * HEAD_PAD), BF16),
                   jax.ShapeDtypeStruct((t, ATTN_HEADS * HEAD_PAD), BF16),
                   jax.ShapeDtypeStruct((t, ATTN_HEADS * HEAD_PAD), BF16),
                   jax.ShapeDtypeStruct((t, SSM_INNER), BF16),
                   jax.ShapeDtypeStruct((t, SSM_CONV_CH), BF16),
                   jax.ShapeDtypeStruct((t // SSM_CHUNK, 2 * SSM_HEADS, SSM_CHUNK), F32)],
        compiler_params=pltpu.CompilerParams(dimension_semantics=("arbitrary",),
                                             vmem_limit_bytes=VMEM_LIMIT),
        name="inproj",
    )(x2, cs, g, win, gqa, wuq, gkva, wuk, wuv, vone, gq, gkn, gkr)


def _attn_kernel(q_ref, k_ref, v_ref, o_ref):
    lane = lax.broadcasted_iota(jnp.int32, (q_ref.shape[0], HEAD_PAD), 1)
    for jp in range(ATTN_HEADS_PER_STEP // 2):
        sls = [slice(j * HEAD_PAD, (j + 1) * HEAD_PAD) for j in (2 * jp, 2 * jp + 1)]
        ss = [_dot_nt(q_ref[:, sl], k_ref[:, sl]) for sl in sls]
        ps = [jnp.exp2(s - jnp.max(s, axis=-1, keepdims=True)).astype(BF16) for s in ss]
        accs = [_dot(p, v_ref[:, sl]) for p, sl in zip(ps, sls)]
        res = [acc / pltpu.roll(acc, V_HEAD_DIM, 1) for acc in accs]
        o_ref[:, jp * HEAD_PAD:(jp + 1) * HEAD_PAD] = jnp.where(lane < V_HEAD_DIM, res[0], res[1])


def _attn_call(q, k, v, batch, seq):
    n_q = seq // TQ
    hps = ATTN_HEADS_PER_STEP
    return pl.pallas_call(
        _attn_kernel,
        grid=(batch, ATTN_HEADS // hps, n_q),
        in_specs=[pl.BlockSpec((TQ, hps * HEAD_PAD), lambda b, hp, i: (b * n_q + i, hp)),
                  pl.BlockSpec((seq, hps * HEAD_PAD), lambda b, hp, i: (b, hp)),
                  pl.BlockSpec((seq, hps * HEAD_PAD), lambda b, hp, i: (b, hp))],
        out_specs=pl.BlockSpec((TQ, hps * V_HEAD_DIM), lambda b, hp, i: (b * n_q + i, hp)),
        out_shape=jax.ShapeDtypeStruct((batch * seq, ATTN_WIDTH), F32),
        compiler_params=pltpu.CompilerParams(
            dimension_semantics=("arbitrary", "arbitrary", "arbitrary"),
            vmem_limit_bytes=VMEM_LIMIT),
        name="attention",
    )(q, k, v)


def _split3(x):
    hi = x.astype(BF16)
    r1 = x - hi.astype(F32)
    mid = r1.astype(BF16)
    lo = (r1 - mid.astype(F32)).astype(BF16)
    return hi, mid, lo


def _ssd_kernel(x_ref, b_ref, c_ref, dt_ref, sh_ref, place_ref, lhs_ones_ref, rhs_const_ref,
                cwx_ref, cwb_ref, cwc_ref, cbx_ref, cbb_ref, cbc_ref, alog_ref, dtbias_ref, dskip_ref,
                o_ref, xc_scr, xm_scr, nsf_scr, nsb_scr, dtv_scr, w_scr, dec_scr, lhs_scr, rhs_scr, rpart_scr):
    nc = dt_ref.shape[0]
    seq = x_ref.shape[0]
    L = SSM_CHUNK
    hpg = HEADS_PER_GROUP
    npair = hpg // 2
    P = SSM_HEAD_DIM
    x_cols = slice(0, GROUP_INNER)
    b_cols = slice(GROUP_INNER, GROUP_INNER + SSM_STATE)
    c_cols = slice(GROUP_INNER + SSM_STATE, GROUP_COLS)
    pair_cols = [slice(j * L, (j + 1) * L) for j in range(npair)]

    @pl.when((pl.program_id(0) == 0) & (pl.program_id(1) == 0))
    def _init_rhs():
        for u in range(SSD_UNROLL):
            rhs_scr[u] = rhs_const_ref[...]

    a_neg = -jnp.exp(alog_ref[...]) * LOG2E
    dtv = jax.nn.softplus(dt_ref[...] + dtbias_ref[...][None])
    dtv_scr[...] = dtv
    da2 = (dtv * a_neg[None]).reshape(nc * DT_ROWS, L)
    ri = lax.broadcasted_iota(jnp.int32, (L, L), 0)
    ci = lax.broadcasted_iota(jnp.int32, (L, L), 1)
    upper = (ri <= ci).astype(BF16)
    lower = (ri >= ci).astype(BF16)
    tri = jnp.concatenate([upper, lower], axis=1)
    cs_fb = sum(_dot(p, tri) for p in _split3(da2))
    rowsel = (ri & hpg) == 0
    cs2 = jnp.where(rowsel, cs_fb[:, 0:L], cs_fb[:, L:2 * L])
    colcs = cs2.T

    col_parts = jnp.concatenate(_split3(colcs), axis=1)
    half_n = place_ref.shape[1] // 2
    for hf in range(2):
        cols = slice(hf * half_n, (hf + 1) * half_n)
        lhs_half = _dot(col_parts, place_ref[:, cols]) + lhs_ones_ref[:, cols]
        for c in range(nc // 2):
            lhs_scr[hf * (nc // 2) + c] = lhs_half[:, c * L:c * L + SEG_K].astype(BF16)
    for i, part in enumerate(_split3(-cs2)):
        rpart_scr[i] = part.astype(F32).reshape(nc, DT_ROWS, L)

    def lane_bcast(col):
        return jnp.broadcast_to(col, (col.shape[0], L))

    tot2 = jnp.where(rowsel, lane_bcast(cs2[:, L - 1:L]), lane_bcast(cs2[:, 0:1]))
    dec_scr[...] = jnp.exp2(tot2).reshape(nc, DT_ROWS, L)
    w_scr[...] = (dtv.reshape(nc * DT_ROWS, L) * jnp.exp2(tot2 - cs2)).reshape(nc, DT_ROWS, L)

    lane_t = lax.broadcasted_iota(jnp.int32, (L, L), 1)
    low_half = lane_t < P

    cw_halves = (cwx_ref[...], jnp.concatenate([cwb_ref[...], cwc_ref[...]], axis=1))
    cb_halves = (cbx_ref[...], jnp.concatenate([cbb_ref[...], cbc_ref[...]], axis=1))

    def rows(hf, start, size):
        if hf == 0:
            return x_ref[pl.ds(start, size), :]
        return jnp.concatenate([b_ref[pl.ds(start, size), :], c_ref[pl.ds(start, size), :]], axis=1)

    def conv_stage(c):
        base = pl.multiple_of(c * L, L)
        ws = pl.multiple_of(jnp.clip(c * L - CONV_WIN_LEAD, 0, seq - CONV_WIN), CONV_WIN_LEAD)
        variant = jnp.where(c == 0, 0, jnp.where(c == nc - 1, 2, 1))
        sh = sh_ref[variant]
        halves = []
        for hf in range(2):
            cw = cw_halves[hf]
            shifted = _dot(sh, rows(hf, ws, CONV_WIN))
            acc = cb_halves[hf] + cw[SSM_CONV // 2:SSM_CONV // 2 + 1, :] * rows(hf, base, L).astype(F32)
            for t, kk in enumerate(CONV_SHIFTED_TAPS):
                acc = acc + cw[kk:kk + 1, :] * shifted[t * L:(t + 1) * L, :]
            halves.append(acc * jax.nn.sigmoid(acc))
        xc = jnp.concatenate(halves, axis=1)
        xc_scr[c] = xc
        for j in range(npair):
            xp = xc[:, pair_cols[j]]
            xm_scr[c, j] = jnp.concatenate([jnp.where(low_half, xp, 0.0), jnp.where(low_half, 0.0, xp)],
                                           axis=0).astype(BF16)

    def state_loads(c):
        return xc_scr[c, :, b_cols], [xm_scr[c, j] for j in range(npair)]

    def state_stage(c, loaded):
        b_tok, xm = loaded
        bt = b_tok.T
        for d, ns_scr in ((0, nsf_scr), (1, nsb_scr)):
            w = w_scr[c, d * hpg:(d + 1) * hpg, :]
            ns_scr[c] = jnp.concatenate(
                [_dot(jnp.concatenate([(bt * w[2 * j:2 * j + 1, :]).astype(BF16),
                                       (bt * w[2 * j + 1:2 * j + 2, :]).astype(BF16)], axis=1), xm[j])
                 for j in range(npair)], axis=1)

    for c in range(CONV_AHEAD):
        conv_stage(c)

    def pass_a(c, carry):
        loaded = state_loads(c)
        conv_stage(c + CONV_AHEAD)
        state_stage(c, loaded)
        return carry
    lax.fori_loop(0, nc - CONV_AHEAD, pass_a, 0, unroll=PASS_A_UNROLL)
    for c in range(nc - CONV_AHEAD, nc):
        state_stage(c, state_loads(c))

    lane_r = lax.broadcasted_iota(jnp.int32, (1, L), 1)

    def decay_row(c, d):
        dec = dec_scr[c, d * hpg:(d + 1) * hpg, :]
        return jnp.concatenate([jnp.where(lane_r < P, dec[2 * j:2 * j + 1, :], dec[2 * j + 1:2 * j + 2, :])
                                for j in range(npair)], axis=1)

    def pass_b(d, ns_scr):
        def body(i, st):
            c = i if d == 0 else nc - 1 - i
            new = ns_scr[c]
            ns_scr[c] = st
            return st * decay_row(c, d) + new
        lax.fori_loop(0, nc, body, jnp.zeros((SSM_STATE, GROUP_INNER), F32))
    pass_b(0, nsf_scr)
    pass_b(1, nsb_scr)

    mask_f = ci <= ri
    mask_b = ci >= ri
    neg_inf = jnp.float32(-jnp.inf)
    sub16 = lax.broadcasted_iota(jnp.int32, (BF16_ROWS, L), 0)

    def pass_c(c, carry):
        base = pl.multiple_of(c * L, L)
        xc = xc_scr[c]
        bm = xc[:, b_cols].astype(BF16)
        cm = xc[:, c_cols].astype(BF16)
        cb = _dot_nt(cm, bm)
        off_f = _dot(cm, nsf_scr[c].astype(BF16))
        off_b = _dot(cm, nsb_scr[c].astype(BF16))
        slot = c % SSD_UNROLL
        for hd in range(DT_ROWS):
            r0 = (hd % 2) * SEG_K_PER_ROW
            blk = jnp.where((sub16 >= r0) & (sub16 < r0 + 3), 1.0, 0.0)
            for i in range(3):
                blk = jnp.where(sub16 == r0 + 3 + i, rpart_scr[i, c, hd:hd + 1, :], blk)
            rhs_scr[slot, (hd // 2) * BF16_ROWS:(hd // 2 + 1) * BF16_ROWS, hd * L:(hd + 1) * L] = blk.astype(BF16)
        seg = _dot(lhs_scr[c], rhs_scr[slot])
        dt_f_r = dtv_scr[c, 0:hpg, :]
        dt_b_r = dtv_scr[c, hpg:DT_ROWS, :]
        ys = []
        for j in range(npair):
            ws_ = []
            for h in (2 * j, 2 * j + 1):
                ef = jnp.exp2(jnp.where(mask_f, seg[:, h * L:(h + 1) * L], neg_inf))
                eb = jnp.exp2(jnp.where(mask_b, seg[:, (hpg + h) * L:(hpg + h + 1) * L], neg_inf))
                ws_.append((cb * (ef * dt_f_r[h:h + 1, :] + eb * dt_b_r[h:h + 1, :])).astype(BF16))
            scale_f = jnp.exp2(seg[:, (DT_ROWS + j) * L:(DT_ROWS + j + 1) * L])
            scale_b = jnp.exp2(seg[:, (DT_ROWS + npair + j) * L:(DT_ROWS + npair + j + 1) * L])
            ys.append(_dot(jnp.concatenate(ws_, axis=1), xm_scr[c, j])
                      + scale_f * off_f[:, pair_cols[j]] + scale_b * off_b[:, pair_cols[j]])
        y = jnp.concatenate(ys, axis=1) + dskip_ref[...] * xc[:, x_cols]
        o_ref[pl.ds(base, L), :] = y
        return carry
    lax.fori_loop(0, nc, pass_c, 0, unroll=SSD_UNROLL)


def _conv_shift_matrices():
    out = np.zeros((3, len(CONV_SHIFTED_TAPS) * SSM_CHUNK, CONV_WIN), np.float32)
    for v, lead in enumerate((0, CONV_WIN_LEAD, CONV_WIN - SSM_CHUNK)):
        for i, kk in enumerate(CONV_SHIFTED_TAPS):
            for t in range(SSM_CHUNK):
                j = t + kk - SSM_CONV // 2 + lead
                if 0 <= j < CONV_WIN:
                    out[v, i * SSM_CHUNK + t, j] = 1.0
    return out


def _seg_matmul_constants(nc):
    L, K, half = SSM_CHUNK, SEG_K_PER_ROW, SSM_HEAD_DIM
    place = np.zeros((3 * L, nc * L), np.float32)
    lhs_ones = np.zeros((1, nc * L), np.float32)
    for c in range(nc):
        for hd in range(DT_ROWS):
            for i in range(3):
                place[i * L + c * DT_ROWS + hd, c * L + hd * K + i] = 1.0
                lhs_ones[0, c * L + hd * K + 3 + i] = 1.0
    rhs = np.zeros((SEG_K, SEG_BLOCKS * L), np.float32)
    for hd in range(DT_ROWS):
        d, h = divmod(hd, HEADS_PER_GROUP)
        pair_block = DT_ROWS + d * (HEADS_PER_GROUP // 2) + h // 2
        lanes = slice(0, half) if h % 2 == 0 else slice(half, L)
        rhs[hd * K:hd * K + 3, hd * L:(hd + 1) * L] = 1.0
        rhs[hd * K:hd * K + 3, pair_block * L:(pair_block + 1) * L][:, lanes] = 1.0
    return place, lhs_ones, rhs


def _ssd_call(xbc, dtc, cw, cbias, alog, dtbias, dskip, batch, seq):
    nc = seq // SSM_CHUNK
    L = SSM_CHUNK
    assert nc >= 3 and seq >= CONV_WIN and nc * DT_ROWS == L
    sh = jnp.asarray(_conv_shift_matrices(), BF16)
    place, lhs_ones, rhs_const = _seg_matmul_constants(nc)
    place = jnp.asarray(place, BF16)
    lhs_ones = jnp.asarray(lhs_ones, F32)
    rhs_const = jnp.asarray(rhs_const, BF16)
    const = lambda a: pl.BlockSpec(a.shape, lambda b, g: (0,) * a.ndim)
    b_blk0 = SSM_INNER // SSM_STATE
    c_blk0 = b_blk0 + SSM_GROUPS

    def xbc_views(rows_, batched):
        lead = (lambda b: b) if batched else (lambda b: 0)
        return [pl.BlockSpec((rows_, GROUP_INNER), lambda b, g: (lead(b), g)),
                pl.BlockSpec((rows_, SSM_STATE), lambda b, g: (lead(b), b_blk0 + g)),
                pl.BlockSpec((rows_, SSM_STATE), lambda b, g: (lead(b), c_blk0 + g))]
    return pl.pallas_call(
        _ssd_kernel,
        grid=(batch, SSM_GROUPS),
        in_specs=xbc_views(seq, True) + [
                  pl.BlockSpec((nc, DT_ROWS, L), lambda b, g: (b, g, 0)),
                  const(sh), const(place), const(lhs_ones), const(rhs_const)]
                 + xbc_views(SUBLANES, False) + xbc_views(1, False) + [
                  pl.BlockSpec((DT_ROWS, L), lambda b, g: (g, 0)),
                  pl.BlockSpec((DT_ROWS, L), lambda b, g: (g, 0)),
                  pl.BlockSpec((1, GROUP_INNER), lambda b, g: (0, g))],
        out_specs=pl.BlockSpec((seq, GROUP_INNER), lambda b, g: (b, g)),
        out_shape=jax.ShapeDtypeStruct((batch * seq, SSM_INNER), F32),
        scratch_shapes=[pltpu.VMEM((nc, L, GROUP_COLS), F32),
                        pltpu.VMEM((nc, HEADS_PER_GROUP // 2, 2 * L, L), BF16),
                        pltpu.VMEM((nc, SSM_STATE, GROUP_INNER), F32),
                        pltpu.VMEM((nc, SSM_STATE, GROUP_INNER), F32),
                        pltpu.VMEM((nc, DT_ROWS, L), F32),
                        pltpu.VMEM((nc, DT_ROWS, L), F32),
                        pltpu.VMEM((nc, DT_ROWS, L), F32),
                        pltpu.VMEM((nc, L, SEG_K), BF16),
                        pltpu.VMEM((SSD_UNROLL, SEG_K, SEG_BLOCKS * L), BF16),
                        pltpu.VMEM((3, nc, DT_ROWS, L), F32)],
        compiler_params=pltpu.CompilerParams(dimension_semantics=("arbitrary", "arbitrary"),
                                             vmem_limit_bytes=VMEM_LIMIT),
        name="ssd",
    )(xbc, xbc, xbc, dtc, sh, place, lhs_ones, rhs_const, cw, cw, cw, cbias, cbias, cbias,
      alog, dtbias, dskip)


def _mlp_kernel(x_ref, attn_ref, ssm_ref, z_ref, gat_ref, gn_ref, wo_ref, gm_ref, wup_ref, wdn_ref, o_ref):
    a = attn_ref[...]
    an = (a * _inv_rms(a) * gat_ref[...]).astype(BF16)
    zc = z_ref[...].astype(F32)
    y = ssm_ref[...] * (zc * jax.nn.sigmoid(zc))
    yn = jnp.concatenate([y[:, g * GROUP_INNER:(g + 1) * GROUP_INNER]
                          * _inv_rms(y[:, g * GROUP_INNER:(g + 1) * GROUP_INNER]) for g in range(SSM_GROUPS)],
                         axis=1) * gn_ref[...]
    mix = _dot(an, wo_ref[0:ATTN_WIDTH, :]) + _dot(yn.astype(BF16), wo_ref[ATTN_WIDTH:D_MIX, :])
    x1 = x_ref[...] + mix
    hm = (x1 * _inv_rms(x1) * gm_ref[...]).astype(BF16)
    acc = jnp.zeros_like(x1)
    for c in range(D_FF // FF_CHUNK):
        cols = slice(c * FF_CHUNK, (c + 1) * FF_CHUNK)
        u = _dot(hm, wup_ref[:, cols])
        acc = acc + _dot(jnp.square(jnp.maximum(u, 0.0)).astype(BF16), wdn_ref[cols, :])
    o_ref[...] = x1 + acc


def _mlp_call(x2, attn, ssm, z, gat, gn, wo, gm, wup, wdn):
    t = x2.shape[0]
    row = lambda w: pl.BlockSpec((TM_MLP, w), lambda i: (i, 0))
    full = lambda a: pl.BlockSpec(a.shape, lambda i: (0, 0), pipeline_mode=pl.Buffered(1))
    return pl.pallas_call(
        _mlp_kernel,
        grid=(t // TM_MLP,),
        in_specs=[row(D_MODEL), row(ATTN_WIDTH), row(SSM_INNER), row(SSM_INNER), full(gat), full(gn), full(wo),
                  full(gm), full(wup), full(wdn)],
        out_specs=row(D_MODEL),
        out_shape=jax.ShapeDtypeStruct((t, D_MODEL), F32),
        compiler_params=pltpu.CompilerParams(dimension_semantics=("arbitrary",),
                                             vmem_limit_bytes=VMEM_LIMIT),
        name="outproj_mlp",
    )(x2, attn, ssm, z, gat, gn, wo, gm, wup, wdn)


def _group_major(fwd, bwd):
    lead = fwd.shape[:-1]
    both = jnp.stack([fwd.reshape(lead + (SSM_GROUPS, HEADS_PER_GROUP)),
                      bwd.reshape(lead + (SSM_GROUPS, HEADS_PER_GROUP))], axis=-2)
    return both.reshape(lead + (2 * SSM_HEADS,))


def _rotate_half(w):
    half = QK_ROPE_DIM // 2
    return jnp.concatenate([-w[..., half:], w[..., :half]], axis=-1)


def _swap_halves(g):
    half = QK_ROPE_DIM // 2
    return jnp.concatenate([g[..., half:], g[..., :half]], axis=-1)


def _lane_bcast(v):
    return jnp.broadcast_to(v[..., None], v.shape + (LANES,)).astype(F32)


def _layer(x2, cs, batch, seq, ln_mix_g, w_in, q_a_norm_g, w_uq, kv_a_norm_g, w_ukv, q_norm_g,
           k_norm_g, attn_out_norm_g, conv_w, conv_b, a_log_fwd, a_log_bwd, dt_bias_fwd, dt_bias_bwd,
           d_skip, ssm_norm_g, w_out, ln_mlp_g, w_mlp_up, w_mlp_down):
    win = _win_prep_call(*w_in)
    wq = w_uq.reshape(Q_LORA_RANK, ATTN_HEADS, QK_HEAD_DIM)
    wuq = jnp.concatenate([wq, _rotate_half(wq[..., QK_NOPE_DIM:])], axis=-1)
    wuq = wuq.reshape(Q_LORA_RANK, ATTN_HEADS * HEAD_PAD).astype(BF16)
    wukv = w_ukv.reshape(KV_LORA_RANK, ATTN_HEADS, QK_NOPE_DIM + V_HEAD_DIM)
    wuk = jnp.pad(wukv[:, :, :QK_NOPE_DIM],
                  ((0, 0), (0, 0), (0, HEAD_PAD - QK_NOPE_DIM))).reshape(KV_LORA_RANK, -1).astype(BF16)
    wv = wukv[:, :, QK_NOPE_DIM:].reshape(KV_LORA_RANK, ATTN_HEADS // 2, 2, V_HEAD_DIM)
    zv = jnp.zeros_like(wv[:, :, 0])
    wuv = jnp.stack([jnp.concatenate([wv[:, :, 0], zv], axis=-1),
                     jnp.concatenate([zv, wv[:, :, 1]], axis=-1)], axis=2).reshape(KV_LORA_RANK, -1).astype(BF16)
    ones_v = np.ones((ATTN_HEADS // 2, V_HEAD_DIM), np.float32)
    vone = jnp.asarray(np.stack([np.concatenate([0 * ones_v, ones_v], axis=-1),
                                 np.concatenate([ones_v, 0 * ones_v], axis=-1)], axis=1).reshape(1, -1))
    scale = QK_HEAD_DIM ** -0.5 * np.log2(np.e)
    gq = (jnp.concatenate([q_norm_g, _swap_halves(q_norm_g[QK_NOPE_DIM:])]) * scale)[None, :]
    zeros_nope = jnp.zeros((QK_NOPE_DIM,), k_norm_g.dtype)
    gkn = jnp.concatenate([k_norm_g[:QK_NOPE_DIM], zeros_nope])[None, :]
    gkr = jnp.concatenate([zeros_nope, k_norm_g[QK_NOPE_DIM:], _swap_halves(k_norm_g[QK_NOPE_DIM:])])[None, :]

    q, k, v, z, xbc, dtc = _inproj_call(
        x2, cs, ln_mix_g[None, :], win, q_a_norm_g[None, :], wuq, kv_a_norm_g[None, :], wuk, wuv, vone,
        gq, gkn, gkr)

    attn = _attn_call(q, k, v, batch, seq)

    cw = jnp.pad(conv_w[:, 0, :], ((0, SUBLANES - SSM_CONV), (0, 0)))
    cbias = conv_b[None, :]
    alog = _lane_bcast(_group_major(a_log_fwd, a_log_bwd))
    dtbias = _lane_bcast(_group_major(dt_bias_fwd, dt_bias_bwd))
    dskip = jnp.repeat(d_skip, SSM_HEAD_DIM)[None, :]
    ssm = _ssd_call(xbc, dtc, cw, cbias, alog, dtbias, dskip, batch, seq)

    return _mlp_call(x2, attn, ssm, z, attn_out_norm_g[None, :], ssm_norm_g[None, :], w_out.astype(BF16),
                     ln_mlp_g[None, :], w_mlp_up.astype(BF16), w_mlp_down.astype(BF16))


def _rope_table(positions):
    inv_freq = 1.0 / (ROPE_THETA ** (jnp.arange(0, QK_ROPE_DIM, 2, dtype=F32) / QK_ROPE_DIM))
    ang = inv_freq[:, None] * positions.astype(F32).reshape(1, -1)
    return jnp.concatenate([jnp.cos(ang), jnp.sin(ang)], axis=0)


def kernel(x, positions, ln_mix_g, w_in, q_a_norm_g, w_uq, kv_a_norm_g, w_ukv, q_norm_g, k_norm_g,
           attn_out_norm_g, conv_w, conv_b, a_log_fwd, a_log_bwd, dt_bias_fwd, dt_bias_bwd, d_skip,
           ssm_norm_g, w_out, ln_mlp_g, w_mlp_up, w_mlp_down):
    batch, seq, d = x.shape
    assert d == D_MODEL and seq % TQ == 0 and (batch * seq) % TM_IN == 0 and (batch * seq) % TM_MLP == 0
    cs = _rope_table(positions)
    x2 = x.reshape(batch * seq, d)
    for l in range(ln_mix_g.shape[0]):
        x2 = _layer(x2, cs, batch, seq, ln_mix_g[l], (w_in, l), q_a_norm_g[l], w_uq[l], kv_a_norm_g[l],
                    w_ukv[l], q_norm_g[l], k_norm_g[l], attn_out_norm_g[l], conv_w[l], conv_b[l],
                    a_log_fwd[l], a_log_bwd[l], dt_bias_fwd[l], dt_bias_bwd[l], d_skip[l], ssm_norm_g[l],
                    w_out[l], ln_mlp_g[l], w_mlp_up[l], w_mlp_down[l])
    return x2.reshape(batch, seq, d)
```

```python
import numpy as np
import jax
import jax.numpy as jnp
from jax import lax
from jax.experimental import pallas as pl
from jax.experimental.pallas import tpu as pltpu

F32 = jnp.float32
BF16 = jnp.bfloat16

D_MODEL = 1024
ATTN_HEADS = 8
QK_NOPE_DIM = 64
QK_ROPE_DIM = 32
QK_HEAD_DIM = QK_NOPE_DIM + QK_ROPE_DIM
V_HEAD_DIM = 64
Q_LORA_RANK = D_MODEL // 4
KV_LORA_RANK = D_MODEL // 8
ROPE_THETA = 10000.0
ATTN_WIDTH = ATTN_HEADS * V_HEAD_DIM
SSM_HEADS = 8
SSM_HEAD_DIM = 64
SSM_INNER = SSM_HEADS * SSM_HEAD_DIM
SSM_GROUPS = 2
SSM_STATE = 128
SSM_CONV = 5
SSM_CHUNK = 128
SSM_CONV_CH = SSM_INNER + 2 * SSM_GROUPS * SSM_STATE
D_MIX = ATTN_WIDTH + SSM_INNER
D_FF = 4 * D_MODEL
EPS = 1e-6

LANES = 128
SUBLANES = 8
HEAD_PAD = LANES

HEADS_PER_GROUP = SSM_HEADS // SSM_GROUPS
GROUP_INNER = SSM_INNER // SSM_GROUPS
GROUP_COLS = GROUP_INNER + 2 * SSM_STATE
DT_ROWS = 2 * HEADS_PER_GROUP
BF16_ROWS = 16
SEG_K_PER_ROW = 8
SEG_K = DT_ROWS * SEG_K_PER_ROW
SEG_BLOCKS = DT_ROWS + HEADS_PER_GROUP
LOG2E = float(np.log2(np.e))
CONV_WIN = 2 * SSM_CHUNK
CONV_WIN_LEAD = SSM_CHUNK // 2
CONV_SHIFTED_TAPS = tuple(k for k in range(SSM_CONV) if k != SSM_CONV // 2)

COL_CKV = Q_LORA_RANK
COL_MISC = COL_CKV + KV_LORA_RANK
COL_Z = COL_MISC + LANES
COL_XBC = COL_Z + SSM_INNER
IN_COLS = COL_XBC + SSM_CONV_CH

TM_IN = 1024
IN_SUBTILE = 512
TQ = 1024
ATTN_HEADS_PER_STEP = 8
TM_MLP = 512
FF_CHUNK = 1024
SSD_UNROLL = 8
CONV_AHEAD = 4
PASS_A_UNROLL = 4
WIN_PREP_STEPS = 4
VMEM_LIMIT = 56 * 1024 * 1024


def _inv_rms(x):
    n = x.shape[-1]
    x2 = x * x
    acc = x2[:, 0:LANES]
    for i in range(1, n // LANES):
        acc = acc + x2[:, i * LANES:(i + 1) * LANES]
    return lax.rsqrt(jnp.sum(acc, axis=-1, keepdims=True) * (1.0 / n) + EPS)


def _dot(a, b):
    return jnp.dot(a, b, preferred_element_type=F32)


def _dot_nt(a, b):
    return lax.dot_general(a, b, (((1,), (1,)), ((), ())), preferred_element_type=F32)


def _inproj_kernel(x_ref, cs_ref, g_ref, win_ref, gqa_ref, wuq_ref, gkva_ref, wuk_ref, wuv_ref,
                   gq_ref, gkn_ref, gkr_ref,
                   q_ref, k_ref, v_ref, z_ref, xbc_ref, dt_ref):
    for sub in range(x_ref.shape[0] // IN_SUBTILE):
        _inproj_rows(sub * IN_SUBTILE, x_ref, cs_ref, g_ref, win_ref, gqa_ref, wuq_ref, gkva_ref, wuk_ref,
                     wuv_ref, gq_ref, gkn_ref, gkr_ref, q_ref, k_ref, v_ref, z_ref, xbc_ref, dt_ref)


def _inproj_rows(r0, x_ref, cs_ref, g_ref, win_ref, gqa_ref, wuq_ref, gkva_ref, wuk_ref, wuv_ref,
                 gq_ref, gkn_ref, gkr_ref, q_ref, k_ref, v_ref, z_ref, xbc_ref, dt_ref):
    tm = IN_SUBTILE
    rs = slice(r0, r0 + tm)
    x = x_ref[rs, :]
    h = (x * _inv_rms(x) * g_ref[...]).astype(BF16)
    big = _dot(h, win_ref[...])
    z_ref[rs, :] = big[:, COL_Z:COL_XBC].astype(BF16)
    xbc_ref[rs, :] = big[:, COL_XBC:IN_COLS].astype(BF16)
    misc = big[:, COL_MISC:COL_Z]
    for c in range(tm // SSM_CHUNK):
        dt_ref[r0 // SSM_CHUNK + c] = misc[c * SSM_CHUNK:(c + 1) * SSM_CHUNK, :].T[0:2 * SSM_HEADS, :]

    cq = big[:, 0:COL_CKV]
    ckv = big[:, COL_CKV:COL_MISC]
    cqn = (cq * _inv_rms(cq) * gqa_ref[...]).astype(BF16)
    ckvn = (ckv * _inv_rms(ckv) * gkva_ref[...]).astype(BF16)
    q_pre = _dot(cqn, wuq_ref[...])
    k_pre = _dot(ckvn, wuk_ref[...])
    v_ref[rs, :] = _dot(ckvn, wuv_ref[...]).astype(BF16)

    half = QK_ROPE_DIM // 2
    zpad = jnp.zeros((HEAD_PAD - QK_ROPE_DIM, SSM_CHUNK), F32)
    tt = jnp.concatenate(
        [jnp.concatenate([cs_ref[:, r0 + c * SSM_CHUNK:r0 + (c + 1) * SSM_CHUNK], zpad], axis=0).T
         for c in range(tm // SSM_CHUNK)], axis=0)
    lane_t = lax.broadcasted_iota(jnp.int32, (tm, HEAD_PAD), 1)
    cs = jnp.where(lane_t < QK_NOPE_DIM, 1.0,
                   jnp.where(lane_t < QK_NOPE_DIM + half, pltpu.roll(tt, QK_NOPE_DIM, 1),
                             jnp.where(lane_t < QK_HEAD_DIM + half, pltpu.roll(tt, QK_NOPE_DIM + half, 1),
                                       pltpu.roll(tt, QK_HEAD_DIM, 1))))
    lane = lax.broadcasted_iota(jnp.int32, (1, HEAD_PAD), 1)
    in_head = (lane < QK_HEAD_DIM).astype(F32)
    is_rope = ((lane >= QK_NOPE_DIM) & (lane < QK_HEAD_DIM)).astype(F32)
    inv_d = 1.0 / QK_HEAD_DIM

    gcq = gq_ref[...] * cs
    for hh in range(ATTN_HEADS):
        sl = slice(hh * HEAD_PAD, (hh + 1) * HEAD_PAD)
        qh = q_pre[:, sl]
        ssq = jnp.sum(qh * qh * in_head, axis=-1, keepdims=True)
        q_ref[rs, sl] = (qh * lax.rsqrt(ssq * inv_d + EPS) * gcq).astype(BF16)

    ab = misc * (gkr_ref[...] * cs)
    lane2 = lax.broadcasted_iota(jnp.int32, (tm, HEAD_PAD), 1)
    swapped = jnp.where(lane2 < QK_HEAD_DIM, pltpu.roll(ab, HEAD_PAD - QK_ROPE_DIM, 1),
                        pltpu.roll(ab, QK_ROPE_DIM, 1))
    s_both = jnp.where(lane2 >= QK_NOPE_DIM, ab + swapped, 0.0)
    ssq_pe = jnp.sum(misc * misc * is_rope, axis=-1, keepdims=True)
    gkn = gkn_ref[...]
    for hh in range(ATTN_HEADS):
        sl = slice(hh * HEAD_PAD, (hh + 1) * HEAD_PAD)
        kh = k_pre[:, sl]
        ssq = jnp.sum(kh * kh, axis=-1, keepdims=True) + ssq_pe
        k_ref[rs, sl] = ((kh * gkn + s_both) * lax.rsqrt(ssq * inv_d + EPS)).astype(BF16)


def _win_prep_kernel(w_ref, o_ref):
    w = w_ref[...]
    o_kpe = Q_LORA_RANK + KV_LORA_RANK
    o_z = o_kpe + QK_ROPE_DIM
    o_dt = o_z + SSM_INNER + SSM_CONV_CH
    half = QK_ROPE_DIM // 2
    hpg = HEADS_PER_GROUP
    dt_f = w[:, o_dt:o_dt + SSM_HEADS]
    dt_b = w[:, o_dt + SSM_HEADS:o_dt + 2 * SSM_HEADS]
    kpe = w[:, o_kpe:o_z]
    pieces = [w[:, 0:o_kpe]]
    for g in range(SSM_GROUPS):
        pieces += [dt_f[:, g * hpg:(g + 1) * hpg], dt_b[:, g * hpg:(g + 1) * hpg]]
    pieces += [jnp.zeros((w.shape[0], QK_NOPE_DIM - 2 * SSM_HEADS), F32), kpe, -kpe[:, half:], kpe[:, :half],
               w[:, o_z:o_dt]]
    o_ref[...] = jnp.concatenate(pieces, axis=1).astype(BF16)


def _win_prep_call(w_in3, layer):
    depth, d, n = w_in3.shape
    rows = d // WIN_PREP_STEPS
    return pl.pallas_call(
        _win_prep_kernel,
        grid=(WIN_PREP_STEPS,),
        in_specs=[pl.BlockSpec((rows, n), lambda i: (layer * WIN_PREP_STEPS + i, 0))],
        out_specs=pl.BlockSpec((rows, IN_COLS), lambda i: (i, 0)),
        out_shape=jax.ShapeDtypeStruct((d, IN_COLS), BF16),
        compiler_params=pltpu.CompilerParams(dimension_semantics=("arbitrary",), vmem_limit_bytes=VMEM_LIMIT),
        name="win_prep",
    )(w_in3.reshape(depth * d, n))


def _inproj_call(x2, cs, g, win, gqa, wuq, gkva, wuk, wuv, gq, gkn, gkr):
    t = x2.shape[0]
    cpt = TM_IN // SSM_CHUNK
    full = lambda a: pl.BlockSpec(a.shape, lambda i: (0,) * a.ndim, pipeline_mode=pl.Buffered(1))
    row = lambda w: pl.BlockSpec((TM_IN, w), lambda i: (i, 0))
    return pl.pallas_call(
        _inproj_kernel,
        grid=(t // TM_IN,),
        in_specs=[row(D_MODEL), pl.BlockSpec((QK_ROPE_DIM, TM_IN), lambda i: (0, i)), full(g), full(win), full(gqa), full(wuq), full(gkva),
                  full(wuk), full(wuv), full(gq), full(gkn), full(gkr)],
        out_specs=[row(ATTN_HEADS * HEAD_PAD), row(ATTN_HEADS * HEAD_PAD), row(ATTN_WIDTH),
                   row(SSM_INNER), row(SSM_CONV_CH),
                   pl.BlockSpec((cpt, 2 * SSM_HEADS, SSM_CHUNK), lambda i: (i, 0, 0))],
        out_shape=[jax.ShapeDtypeStruct((t, ATTN_HEADS * HEAD_PAD), BF16),
                   jax.ShapeDtypeStruct((t, ATTN_HEADS * HEAD_PAD), BF16),
                   jax.ShapeDtypeStruct((t, ATTN_WIDTH), BF16),
                   jax.ShapeDtypeStruct((t, SSM_INNER), BF16),
                   jax.ShapeDtypeStruct((t, SSM_CONV_CH), BF16),
                   jax.ShapeDtypeStruct((t // SSM_CHUNK, 2 * SSM_HEADS, SSM_CHUNK), F32)],
        compiler_params=pltpu.CompilerParams(dimension_semantics=("arbitrary",),
                                             vmem_limit_bytes=VMEM_LIMIT),
        name="inproj",
    )(x2, cs, g, win, gqa, wuq, gkva, wuk, wuv, gq, gkn, gkr)


def _attn_kernel(q_ref, k_ref, v_ref, o_ref, vext_scr):
    @pl.when(pl.program_id(2) == 0)
    def _build_v_tiles():
        low = (lax.broadcasted_iota(jnp.int32, (1, HEAD_PAD), 1) < V_HEAD_DIM).astype(F32).astype(BF16)
        high = (1.0 - low.astype(F32)).astype(BF16)
        for jp in range(ATTN_HEADS_PER_STEP // 2):
            pair = v_ref[:, jp * HEAD_PAD:(jp + 1) * HEAD_PAD]
            vext_scr[:, (2 * jp) * HEAD_PAD:(2 * jp + 1) * HEAD_PAD] = pair * low + high
            vext_scr[:, (2 * jp + 1) * HEAD_PAD:(2 * jp + 2) * HEAD_PAD] = pair * high + low

    lane = lax.broadcasted_iota(jnp.int32, (q_ref.shape[0], HEAD_PAD), 1)
    for jp in range(ATTN_HEADS_PER_STEP // 2):
        sls = [slice(j * HEAD_PAD, (j + 1) * HEAD_PAD) for j in (2 * jp, 2 * jp + 1)]
        ss = [_dot_nt(q_ref[:, sl], k_ref[:, sl]) for sl in sls]
        ps = [jnp.exp2(s - jnp.max(s, axis=-1, keepdims=True)).astype(BF16) for s in ss]
        accs = [_dot(p, vext_scr[:, sl]) for p, sl in zip(ps, sls)]
        res = [acc / pltpu.roll(acc, V_HEAD_DIM, 1) for acc in accs]
        o_ref[:, jp * HEAD_PAD:(jp + 1) * HEAD_PAD] = jnp.where(lane < V_HEAD_DIM, res[0], res[1])


def _attn_call(q, k, v, batch, seq):
    n_q = seq // TQ
    hps = ATTN_HEADS_PER_STEP
    return pl.pallas_call(
        _attn_kernel,
        grid=(batch, ATTN_HEADS // hps, n_q),
        in_specs=[pl.BlockSpec((TQ, hps * HEAD_PAD), lambda b, hp, i: (b * n_q + i, hp)),
                  pl.BlockSpec((seq, hps * HEAD_PAD), lambda b, hp, i: (b, hp)),
                  pl.BlockSpec((seq, hps * V_HEAD_DIM), lambda b, hp, i: (b, hp))],
        out_specs=pl.BlockSpec((TQ, hps * V_HEAD_DIM), lambda b, hp, i: (b * n_q + i, hp)),
        out_shape=jax.ShapeDtypeStruct((batch * seq, ATTN_WIDTH), F32),
        scratch_shapes=[pltpu.VMEM((seq, hps * HEAD_PAD), BF16)],
        compiler_params=pltpu.CompilerParams(
            dimension_semantics=("arbitrary", "arbitrary", "arbitrary"),
            vmem_limit_bytes=VMEM_LIMIT),
        name="attention",
    )(q, k, v)


def _split3(x):
    hi = x.astype(BF16)
    r1 = x - hi.astype(F32)
    mid = r1.astype(BF16)
    lo = (r1 - mid.astype(F32)).astype(BF16)
    return hi, mid, lo


def _ssd_kernel(x_ref, b_ref, c_ref, dt_ref, sh_ref, place_ref, lhs_ones_ref, rhs_const_ref,
                cwx_ref, cwb_ref, cwc_ref, cbx_ref, cbb_ref, cbc_ref, alog_ref, dtbias_ref, dskip_ref,
                o_ref, xc_scr, xm_scr, nsf_scr, nsb_scr, dtv_scr, w_scr, dec_scr, lhs_scr, rhs_scr, rpart_scr):
    nc = dt_ref.shape[0]
    seq = x_ref.shape[0]
    L = SSM_CHUNK
    hpg = HEADS_PER_GROUP
    npair = hpg // 2
    P = SSM_HEAD_DIM
    x_cols = slice(0, GROUP_INNER)
    b_cols = slice(GROUP_INNER, GROUP_INNER + SSM_STATE)
    c_cols = slice(GROUP_INNER + SSM_STATE, GROUP_COLS)
    pair_cols = [slice(j * L, (j + 1) * L) for j in range(npair)]

    @pl.when((pl.program_id(0) == 0) & (pl.program_id(1) == 0))
    def _init_rhs():
        for u in range(SSD_UNROLL):
            rhs_scr[u] = rhs_const_ref[...]

    a_neg = -jnp.exp(alog_ref[...]) * LOG2E
    dtv = jax.nn.softplus(dt_ref[...] + dtbias_ref[...][None])
    dtv_scr[...] = dtv
    da2 = (dtv * a_neg[None]).reshape(nc * DT_ROWS, L)
    ri = lax.broadcasted_iota(jnp.int32, (L, L), 0)
    ci = lax.broadcasted_iota(jnp.int32, (L, L), 1)
    upper = (ri <= ci).astype(BF16)
    lower = (ri >= ci).astype(BF16)
    tri = jnp.concatenate([upper, lower], axis=1)
    cs_fb = sum(_dot(p, tri) for p in _split3(da2))
    rowsel = (ri & hpg) == 0
    cs2 = jnp.where(rowsel, cs_fb[:, 0:L], cs_fb[:, L:2 * L])
    colcs = cs2.T

    col_parts = jnp.concatenate(_split3(colcs), axis=1)
    half_n = place_ref.shape[1] // 2
    for hf in range(2):
        cols = slice(hf * half_n, (hf + 1) * half_n)
        lhs_half = _dot(col_parts, place_ref[:, cols]) + lhs_ones_ref[:, cols]
        for c in range(nc // 2):
            lhs_scr[hf * (nc // 2) + c] = lhs_half[:, c * L:c * L + SEG_K].astype(BF16)
    for i, part in enumerate(_split3(-cs2)):
        rpart_scr[i] = part.astype(F32).reshape(nc, DT_ROWS, L)

    def lane_bcast(col):
        return jnp.broadcast_to(col, (col.shape[0], L))

    tot2 = jnp.where(rowsel, lane_bcast(cs2[:, L - 1:L]), lane_bcast(cs2[:, 0:1]))
    dec_scr[...] = jnp.exp2(tot2).reshape(nc, DT_ROWS, L)
    w_scr[...] = (dtv.reshape(nc * DT_ROWS, L) * jnp.exp2(tot2 - cs2)).reshape(nc, DT_ROWS, L)

    lane_t = lax.broadcasted_iota(jnp.int32, (L, L), 1)
    low_half = lane_t < P

    cw_halves = (cwx_ref[...], jnp.concatenate([cwb_ref[...], cwc_ref[...]], axis=1))
    cb_halves = (cbx_ref[...], jnp.concatenate([cbb_ref[...], cbc_ref[...]], axis=1))

    def rows(hf, start, size):
        if hf == 0:
            return x_ref[pl.ds(start, size), :]
        return jnp.concatenate([b_ref[pl.ds(start, size), :], c_ref[pl.ds(start, size), :]], axis=1)

    def conv_stage(c):
        base = pl.multiple_of(c * L, L)
        ws = pl.multiple_of(jnp.clip(c * L - CONV_WIN_LEAD, 0, seq - CONV_WIN), CONV_WIN_LEAD)
        variant = jnp.where(c == 0, 0, jnp.where(c == nc - 1, 2, 1))
        sh = sh_ref[variant]
        halves = []
        for hf in range(2):
            cw = cw_halves[hf]
            shifted = _dot(sh, rows(hf, ws, CONV_WIN))
            acc = cb_halves[hf] + cw[SSM_CONV // 2:SSM_CONV // 2 + 1, :] * rows(hf, base, L).astype(F32)
            for t, kk in enumerate(CONV_SHIFTED_TAPS):
                acc = acc + cw[kk:kk + 1, :] * shifted[t * L:(t + 1) * L, :]
            halves.append(acc * jax.nn.sigmoid(acc))
        xc = jnp.concatenate(halves, axis=1)
        xc_scr[c] = xc
        for j in range(npair):
            xp = xc[:, pair_cols[j]]
            xm_scr[c, j] = jnp.concatenate([jnp.where(low_half, xp, 0.0), jnp.where(low_half, 0.0, xp)],
                                           axis=0).astype(BF16)

    def state_loads(c):
        return xc_scr[c, :, b_cols], [xm_scr[c, j] for j in range(npair)]

    def state_stage(c, loaded):
        b_tok, xm = loaded
        bt = b_tok.T
        for d, ns_scr in ((0, nsf_scr), (1, nsb_scr)):
            w = w_scr[c, d * hpg:(d + 1) * hpg, :]
            ns_scr[c] = jnp.concatenate(
                [_dot(jnp.concatenate([(bt * w[2 * j:2 * j + 1, :]).astype(BF16),
                                       (bt * w[2 * j + 1:2 * j + 2, :]).astype(BF16)], axis=1), xm[j])
                 for j in range(npair)], axis=1)

    for c in range(CONV_AHEAD):
        conv_stage(c)

    def pass_a(c, carry):
        loaded = state_loads(c)
        conv_stage(c + CONV_AHEAD)
        state_stage(c, loaded)
        return carry
    lax.fori_loop(0, nc - CONV_AHEAD, pass_a, 0, unroll=PASS_A_UNROLL)
    for c in range(nc - CONV_AHEAD, nc):
        state_stage(c, state_loads(c))

    lane_r = lax.broadcasted_iota(jnp.int32, (1, L), 1)

    def decay_row(c, d):
        dec = dec_scr[c, d * hpg:(d + 1) * hpg, :]
        return jnp.concatenate([jnp.where(lane_r < P, dec[2 * j:2 * j + 1, :], dec[2 * j + 1:2 * j + 2, :])
                                for j in range(npair)], axis=1)

    def pass_b(d, ns_scr):
        def body(i, st):
            c = i if d == 0 else nc - 1 - i
            new = ns_scr[c]
            ns_scr[c] = st
            return st * decay_row(c, d) + new
        lax.fori_loop(0, nc, body, jnp.zeros((SSM_STATE, GROUP_INNER), F32))
    pass_b(0, nsf_scr)
    pass_b(1, nsb_scr)

    mask_f = ci <= ri
    mask_b = ci >= ri
    neg_inf = jnp.float32(-jnp.inf)
    sub16 = lax.broadcasted_iota(jnp.int32, (BF16_ROWS, L), 0)

    def pass_c(c, carry):
        base = pl.multiple_of(c * L, L)
        xc = xc_scr[c]
        bm = xc[:, b_cols].astype(BF16)
        cm = xc[:, c_cols].astype(BF16)
        cb = _dot_nt(cm, bm)
        off_f = _dot(cm, nsf_scr[c].astype(BF16))
        off_b = _dot(cm, nsb_scr[c].astype(BF16))
        slot = c % SSD_UNROLL
        for hd in range(DT_ROWS):
            r0 = (hd % 2) * SEG_K_PER_ROW
            blk = jnp.where((sub16 >= r0) & (sub16 < r0 + 3), 1.0, 0.0)
            for i in range(3):
                blk = jnp.where(sub16 == r0 + 3 + i, rpart_scr[i, c, hd:hd + 1, :], blk)
            rhs_scr[slot, (hd // 2) * BF16_ROWS:(hd // 2 + 1) * BF16_ROWS, hd * L:(hd + 1) * L] = blk.astype(BF16)
        seg = _dot(lhs_scr[c], rhs_scr[slot])
        dt_f_r = dtv_scr[c, 0:hpg, :]
        dt_b_r = dtv_scr[c, hpg:DT_ROWS, :]
        ys = []
        for j in range(npair):
            ws_ = []
            for h in (2 * j, 2 * j + 1):
                ef = jnp.exp2(jnp.where(mask_f, seg[:, h * L:(h + 1) * L], neg_inf))
                eb = jnp.exp2(jnp.where(mask_b, seg[:, (hpg + h) * L:(hpg + h + 1) * L], neg_inf))
                ws_.append((cb * (ef * dt_f_r[h:h + 1, :] + eb * dt_b_r[h:h + 1, :])).astype(BF16))
            scale_f = jnp.exp2(seg[:, (DT_ROWS + j) * L:(DT_ROWS + j + 1) * L])
            scale_b = jnp.exp2(seg[:, (DT_ROWS + npair + j) * L:(DT_ROWS + npair + j + 1) * L])
            ys.append(_dot(jnp.concatenate(ws_, axis=1), xm_scr[c, j])
                      + scale_f * off_f[:, pair_cols[j]] + scale_b * off_b[:, pair_cols[j]])
        y = jnp.concatenate(ys, axis=1) + dskip_ref[...] * xc[:, x_cols]
        o_ref[pl.ds(base, L), :] = y
        return carry
    lax.fori_loop(0, nc, pass_c, 0, unroll=SSD_UNROLL)


def _conv_shift_matrices():
    out = np.zeros((3, len(CONV_SHIFTED_TAPS) * SSM_CHUNK, CONV_WIN), np.float32)
    for v, lead in enumerate((0, CONV_WIN_LEAD, CONV_WIN - SSM_CHUNK)):
        for i, kk in enumerate(CONV_SHIFTED_TAPS):
            for t in range(SSM_CHUNK):
                j = t + kk - SSM_CONV // 2 + lead
                if 0 <= j < CONV_WIN:
                    out[v, i * SSM_CHUNK + t, j] = 1.0
    return out


def _seg_matmul_constants(nc):
    L, K, half = SSM_CHUNK, SEG_K_PER_ROW, SSM_HEAD_DIM
    place = np.zeros((3 * L, nc * L), np.float32)
    lhs_ones = np.zeros((1, nc * L), np.float32)
    for c in range(nc):
        for hd in range(DT_ROWS):
            for i in range(3):
                place[i * L + c * DT_ROWS + hd, c * L + hd * K + i] = 1.0
                lhs_ones[0, c * L + hd * K + 3 + i] = 1.0
    rhs = np.zeros((SEG_K, SEG_BLOCKS * L), np.float32)
    for hd in range(DT_ROWS):
        d, h = divmod(hd, HEADS_PER_GROUP)
        pair_block = DT_ROWS + d * (HEADS_PER_GROUP // 2) + h // 2
        lanes = slice(0, half) if h % 2 == 0 else slice(half, L)
        rhs[hd * K:hd * K + 3, hd * L:(hd + 1) * L] = 1.0
        rhs[hd * K:hd * K + 3, pair_block * L:(pair_block + 1) * L][:, lanes] = 1.0
    return place, lhs_ones, rhs


def _ssd_call(xbc, dtc, cw, cbias, alog, dtbias, dskip, batch, seq):
    nc = seq // SSM_CHUNK
    L = SSM_CHUNK
    assert nc >= 3 and seq >= CONV_WIN and nc * DT_ROWS == L
    sh = jnp.asarray(_conv_shift_matrices(), BF16)
    place, lhs_ones, rhs_const = _seg_matmul_constants(nc)
    place = jnp.asarray(place, BF16)
    lhs_ones = jnp.asarray(lhs_ones, F32)
    rhs_const = jnp.asarray(rhs_const, BF16)
    const = lambda a: pl.BlockSpec(a.shape, lambda b, g: (0,) * a.ndim)
    b_blk0 = SSM_INNER // SSM_STATE
    c_blk0 = b_blk0 + SSM_GROUPS

    def xbc_views(rows_, batched):
        lead = (lambda b: b) if batched else (lambda b: 0)
        return [pl.BlockSpec((rows_, GROUP_INNER), lambda b, g: (lead(b), g)),
                pl.BlockSpec((rows_, SSM_STATE), lambda b, g: (lead(b), b_blk0 + g)),
                pl.BlockSpec((rows_, SSM_STATE), lambda b, g: (lead(b), c_blk0 + g))]
    return pl.pallas_call(
        _ssd_kernel,
        grid=(batch, SSM_GROUPS),
        in_specs=xbc_views(seq, True) + [
                  pl.BlockSpec((nc, DT_ROWS, L), lambda b, g: (b, g, 0)),
                  const(sh), const(place), const(lhs_ones), const(rhs_const)]
                 + xbc_views(SUBLANES, False) + xbc_views(1, False) + [
                  pl.BlockSpec((DT_ROWS, L), lambda b, g: (g, 0)),
                  pl.BlockSpec((DT_ROWS, L), lambda b, g: (g, 0)),
                  pl.BlockSpec((1, GROUP_INNER), lambda b, g: (0, g))],
        out_specs=pl.BlockSpec((seq, GROUP_INNER), lambda b, g: (b, g)),
        out_shape=jax.ShapeDtypeStruct((batch * seq, SSM_INNER), F32),
        scratch_shapes=[pltpu.VMEM((nc, L, GROUP_COLS), F32),
                        pltpu.VMEM((nc, HEADS_PER_GROUP // 2, 2 * L, L), BF16),
                        pltpu.VMEM((nc, SSM_STATE, GROUP_INNER), F32),
                        pltpu.VMEM((nc, SSM_STATE, GROUP_INNER), F32),
                        pltpu.VMEM((nc, DT_ROWS, L), F32),
                        pltpu.VMEM((nc, DT_ROWS, L), F32),
                        pltpu.VMEM((nc, DT_ROWS, L), F32),
                        pltpu.VMEM((nc, L, SEG_K), BF16),
                        pltpu.VMEM((SSD_UNROLL, SEG_K, SEG_BLOCKS * L), BF16),
                        pltpu.VMEM((3, nc, DT_ROWS, L), F32)],
        compiler_params=pltpu.CompilerParams(dimension_semantics=("arbitrary", "arbitrary"),
                                             vmem_limit_bytes=VMEM_LIMIT),
        name="ssd",
    )(xbc, xbc, xbc, dtc, sh, place, lhs_ones, rhs_const, cw, cw, cw, cbias, cbias, cbias,
      alog, dtbias, dskip)


def _mlp_kernel(x_ref, attn_ref, ssm_ref, z_ref, gat_ref, gn_ref, wo_ref, gm_ref, wup_ref, wdn_ref, o_ref):
    a = attn_ref[...]
    an = (a * _inv_rms(a) * gat_ref[...]).astype(BF16)
    zc = z_ref[...].astype(F32)
    y = ssm_ref[...] * (zc * jax.nn.sigmoid(zc))
    yn = jnp.concatenate([y[:, g * GROUP_INNER:(g + 1) * GROUP_INNER]
                          * _inv_rms(y[:, g * GROUP_INNER:(g + 1) * GROUP_INNER]) for g in range(SSM_GROUPS)],
                         axis=1) * gn_ref[...]
    mix = _dot(an, wo_ref[0:ATTN_WIDTH, :]) + _dot(yn.astype(BF16), wo_ref[ATTN_WIDTH:D_MIX, :])
    x1 = x_ref[...] + mix
    hm = (x1 * _inv_rms(x1) * gm_ref[...]).astype(BF16)
    acc = jnp.zeros_like(x1)
    for c in range(D_FF // FF_CHUNK):
        cols = slice(c * FF_CHUNK, (c + 1) * FF_CHUNK)
        u = _dot(hm, wup_ref[:, cols])
        acc = acc + _dot(jnp.square(jnp.maximum(u, 0.0)).astype(BF16), wdn_ref[cols, :])
    o_ref[...] = x1 + acc


def _mlp_call(x2, attn, ssm, z, gat, gn, wo, gm, wup, wdn):
    t = x2.shape[0]
    row = lambda w: pl.BlockSpec((TM_MLP, w), lambda i: (i, 0))
    full = lambda a: pl.BlockSpec(a.shape, lambda i: (0, 0), pipeline_mode=pl.Buffered(1))
    return pl.pallas_call(
        _mlp_kernel,
        grid=(t // TM_MLP,),
        in_specs=[row(D_MODEL), row(ATTN_WIDTH), row(SSM_INNER), row(SSM_INNER), full(gat), full(gn), full(wo),
                  full(gm), full(wup), full(wdn)],
        out_specs=row(D_MODEL),
        out_shape=jax.ShapeDtypeStruct((t, D_MODEL), F32),
        compiler_params=pltpu.CompilerParams(dimension_semantics=("arbitrary",),
                                             vmem_limit_bytes=VMEM_LIMIT),
        name="outproj_mlp",
    )(x2, attn, ssm, z, gat, gn, wo, gm, wup, wdn)


def _group_major(fwd, bwd):
    lead = fwd.shape[:-1]
    both = jnp.stack([fwd.reshape(lead + (SSM_GROUPS, HEADS_PER_GROUP)),
                      bwd.reshape(lead + (SSM_GROUPS, HEADS_PER_GROUP))], axis=-2)
    return both.reshape(lead + (2 * SSM_HEADS,))


def _rotate_half(w):
    half = QK_ROPE_DIM // 2
    return jnp.concatenate([-w[..., half:], w[..., :half]], axis=-1)


def _swap_halves(g):
    half = QK_ROPE_DIM // 2
    return jnp.concatenate([g[..., half:], g[..., :half]], axis=-1)


def _lane_bcast(v):
    return jnp.broadcast_to(v[..., None], v.shape + (LANES,)).astype(F32)


def _layer(x2, cs, batch, seq, ln_mix_g, w_in, q_a_norm_g, w_uq, kv_a_norm_g, w_ukv, q_norm_g,
           k_norm_g, attn_out_norm_g, conv_w, conv_b, a_log_fwd, a_log_bwd, dt_bias_fwd, dt_bias_bwd,
           d_skip, ssm_norm_g, w_out, ln_mlp_g, w_mlp_up, w_mlp_down):
    win = _win_prep_call(*w_in)
    wq = w_uq.reshape(Q_LORA_RANK, ATTN_HEADS, QK_HEAD_DIM)
    wuq = jnp.concatenate([wq, _rotate_half(wq[..., QK_NOPE_DIM:])], axis=-1)
    wuq = wuq.reshape(Q_LORA_RANK, ATTN_HEADS * HEAD_PAD).astype(BF16)
    wukv = w_ukv.reshape(KV_LORA_RANK, ATTN_HEADS, QK_NOPE_DIM + V_HEAD_DIM)
    wuk = jnp.pad(wukv[:, :, :QK_NOPE_DIM],
                  ((0, 0), (0, 0), (0, HEAD_PAD - QK_NOPE_DIM))).reshape(KV_LORA_RANK, -1).astype(BF16)
    wuv = wukv[:, :, QK_NOPE_DIM:].reshape(KV_LORA_RANK, -1).astype(BF16)
    scale = QK_HEAD_DIM ** -0.5 * np.log2(np.e)
    gq = (jnp.concatenate([q_norm_g, _swap_halves(q_norm_g[QK_NOPE_DIM:])]) * scale)[None, :]
    zeros_nope = jnp.zeros((QK_NOPE_DIM,), k_norm_g.dtype)
    gkn = jnp.concatenate([k_norm_g[:QK_NOPE_DIM], zeros_nope])[None, :]
    gkr = jnp.concatenate([zeros_nope, k_norm_g[QK_NOPE_DIM:], _swap_halves(k_norm_g[QK_NOPE_DIM:])])[None, :]

    q, k, v, z, xbc, dtc = _inproj_call(
        x2, cs, ln_mix_g[None, :], win, q_a_norm_g[None, :], wuq, kv_a_norm_g[None, :], wuk, wuv,
        gq, gkn, gkr)

    attn = _attn_call(q, k, v, batch, seq)

    cw = jnp.pad(conv_w[:, 0, :], ((0, SUBLANES - SSM_CONV), (0, 0)))
    cbias = conv_b[None, :]
    alog = _lane_bcast(_group_major(a_log_fwd, a_log_bwd))
    dtbias = _lane_bcast(_group_major(dt_bias_fwd, dt_bias_bwd))
    dskip = jnp.repeat(d_skip, SSM_HEAD_DIM)[None, :]
    ssm = _ssd_call(xbc, dtc, cw, cbias, alog, dtbias, dskip, batch, seq)

    return _mlp_call(x2, attn, ssm, z, attn_out_norm_g[None, :], ssm_norm_g[None, :], w_out.astype(BF16),
                     ln_mlp_g[None, :], w_mlp_up.astype(BF16), w_mlp_down.astype(BF16))


def _rope_table(positions):
    inv_freq = 1.0 / (ROPE_THETA ** (jnp.arange(0, QK_ROPE_DIM, 2, dtype=F32) / QK_ROPE_DIM))
    ang = inv_freq[:, None] * positions.astype(F32).reshape(1, -1)
    return jnp.concatenate([jnp.cos(ang), jnp.sin(ang)], axis=0)


def kernel(x, positions, ln_mix_g, w_in, q_a_norm_g, w_uq, kv_a_norm_g, w_ukv, q_norm_g, k_norm_g,
           attn_out_norm_g, conv_w, conv_b, a_log_fwd, a_log_bwd, dt_bias_fwd, dt_bias_bwd, d_skip,
           ssm_norm_g, w_out, ln_mlp_g, w_mlp_up, w_mlp_down):
    batch, seq, d = x.shape
    assert d == D_MODEL and seq % TQ == 0 and (batch * seq) % TM_IN == 0 and (batch * seq) % TM_MLP == 0
    cs = _rope_table(positions)
    x2 = x.reshape(batch * seq, d)
    for l in range(ln_mix_g.shape[0]):
        x2 = _layer(x2, cs, batch, seq, ln_mix_g[l], (w_in, l), q_a_norm_g[l], w_uq[l], kv_a_norm_g[l],
                    w_ukv[l], q_norm_g[l], k_norm_g[l], attn_out_norm_g[l], conv_w[l], conv_b[l],
                    a_log_fwd[l], a_log_bwd[l], dt_bias_fwd[l], dt_bias_bwd[l], d_skip[l], ssm_norm_g[l],
                    w_out[l], ln_mlp_g[l], w_mlp_up[l], w_mlp_down[l])
    return x2.reshape(batch, seq, d)
```

```python
import numpy as np
import jax
import jax.numpy as jnp
from jax import lax
from jax.experimental import pallas as pl
from jax.experimental.pallas import tpu as pltpu

F32 = jnp.float32
BF16 = jnp.bfloat16

D_MODEL = 1024
ATTN_HEADS = 8
QK_NOPE_DIM = 64
QK_ROPE_DIM = 32
QK_HEAD_DIM = QK_NOPE_DIM + QK_ROPE_DIM
V_HEAD_DIM = 64
Q_LORA_RANK = D_MODEL // 4
KV_LORA_RANK = D_MODEL // 8
ROPE_THETA = 10000.0
ATTN_WIDTH = ATTN_HEADS * V_HEAD_DIM
SSM_HEADS = 8
SSM_HEAD_DIM = 64
SSM_INNER = SSM_HEADS * SSM_HEAD_DIM
SSM_GROUPS = 2
SSM_STATE = 128
SSM_CONV = 5
SSM_CHUNK = 128
SSM_CONV_CH = SSM_INNER + 2 * SSM_GROUPS * SSM_STATE
D_MIX = ATTN_WIDTH + SSM_INNER
D_FF = 4 * D_MODEL
EPS = 1e-6

LANES = 128
SUBLANES = 8
HEAD_PAD = LANES

HEADS_PER_GROUP = SSM_HEADS // SSM_GROUPS
GROUP_INNER = SSM_INNER // SSM_GROUPS
GROUP_COLS = GROUP_INNER + 2 * SSM_STATE
DT_ROWS = 2 * HEADS_PER_GROUP
BF16_ROWS = 16
SEG_K_PER_ROW = 8
SEG_K = DT_ROWS * SEG_K_PER_ROW
SEG_BLOCKS = DT_ROWS + HEADS_PER_GROUP
LOG2E = float(np.log2(np.e))
CONV_WIN = 2 * SSM_CHUNK
CONV_WIN_LEAD = SSM_CHUNK // 2
CONV_SHIFTED_TAPS = tuple(k for k in range(SSM_CONV) if k != SSM_CONV // 2)

COL_CKV = Q_LORA_RANK
COL_MISC = COL_CKV + KV_LORA_RANK
COL_Z = COL_MISC + LANES
COL_XBC = COL_Z + SSM_INNER
IN_COLS = COL_XBC + SSM_CONV_CH

TM_IN = 1024
IN_SUBTILE = 512
TQ = 1024
ATTN_HEADS_PER_STEP = 8
TM_MLP = 512
FF_CHUNK = 1024
SSD_UNROLL = 8
CONV_AHEAD = 4
PASS_A_UNROLL = 4
WIN_PREP_STEPS = 4
VMEM_LIMIT = 56 * 1024 * 1024


def _inv_rms(x):
    n = x.shape[-1]
    x2 = x * x
    acc = x2[:, 0:LANES]
    for i in range(1, n // LANES):
        acc = acc + x2[:, i * LANES:(i + 1) * LANES]
    return lax.rsqrt(jnp.sum(acc, axis=-1, keepdims=True) * (1.0 / n) + EPS)


def _dot(a, b):
    return jnp.dot(a, b, preferred_element_type=F32)


def _dot_nt(a, b):
    return lax.dot_general(a, b, (((1,), (1,)), ((), ())), preferred_element_type=F32)


def _inproj_kernel(x_ref, cs_ref, g_ref, win_ref, gqa_ref, wuq_ref, gkva_ref, wuk_ref, wuv_ref,
                   gq_ref, gkn_ref, gkr_ref,
                   q_ref, k_ref, v_ref, z_ref, xbc_ref, dt_ref):
    for sub in range(x_ref.shape[0] // IN_SUBTILE):
        _inproj_rows(sub * IN_SUBTILE, x_ref, cs_ref, g_ref, win_ref, gqa_ref, wuq_ref, gkva_ref, wuk_ref,
                     wuv_ref, gq_ref, gkn_ref, gkr_ref, q_ref, k_ref, v_ref, z_ref, xbc_ref, dt_ref)


def _inproj_rows(r0, x_ref, cs_ref, g_ref, win_ref, gqa_ref, wuq_ref, gkva_ref, wuk_ref, wuv_ref,
                 gq_ref, gkn_ref, gkr_ref, q_ref, k_ref, v_ref, z_ref, xbc_ref, dt_ref):
    tm = IN_SUBTILE
    rs = slice(r0, r0 + tm)
    x = x_ref[rs, :]
    h = (x * _inv_rms(x) * g_ref[...]).astype(BF16)
    big = _dot(h, win_ref[...])
    z_ref[rs, :] = big[:, COL_Z:COL_XBC].astype(BF16)
    xbc_ref[rs, :] = big[:, COL_XBC:IN_COLS].astype(BF16)
    misc = big[:, COL_MISC:COL_Z]
    for c in range(tm // SSM_CHUNK):
        dt_ref[r0 // SSM_CHUNK + c] = misc[c * SSM_CHUNK:(c + 1) * SSM_CHUNK, :].T[0:2 * SSM_HEADS, :]

    cq = big[:, 0:COL_CKV]
    ckv = big[:, COL_CKV:COL_MISC]
    cqn = (cq * _inv_rms(cq) * gqa_ref[...]).astype(BF16)
    ckvn = (ckv * _inv_rms(ckv) * gkva_ref[...]).astype(BF16)
    q_pre = _dot(cqn, wuq_ref[...])
    k_pre = _dot(ckvn, wuk_ref[...])
    v_ref[rs, :] = _dot(ckvn, wuv_ref[...]).astype(BF16)

    half = QK_ROPE_DIM // 2
    zpad = jnp.zeros((HEAD_PAD - QK_ROPE_DIM, SSM_CHUNK), F32)
    tt = jnp.concatenate(
        [jnp.concatenate([cs_ref[:, r0 + c * SSM_CHUNK:r0 + (c + 1) * SSM_CHUNK], zpad], axis=0).T
         for c in range(tm // SSM_CHUNK)], axis=0)
    lane_t = lax.broadcasted_iota(jnp.int32, (tm, HEAD_PAD), 1)
    cs = jnp.where(lane_t < QK_NOPE_DIM, 1.0,
                   jnp.where(lane_t < QK_NOPE_DIM + half, pltpu.roll(tt, QK_NOPE_DIM, 1),
                             jnp.where(lane_t < QK_HEAD_DIM + half, pltpu.roll(tt, QK_NOPE_DIM + half, 1),
                                       pltpu.roll(tt, QK_HEAD_DIM, 1))))
    lane = lax.broadcasted_iota(jnp.int32, (1, HEAD_PAD), 1)
    in_head = (lane < QK_HEAD_DIM).astype(F32)
    is_rope = ((lane >= QK_NOPE_DIM) & (lane < QK_HEAD_DIM)).astype(F32)
    inv_d = 1.0 / QK_HEAD_DIM

    gcq = gq_ref[...] * cs
    for hh in range(ATTN_HEADS):
        sl = slice(hh * HEAD_PAD, (hh + 1) * HEAD_PAD)
        qh = q_pre[:, sl]
        ssq = jnp.sum(qh * qh * in_head, axis=-1, keepdims=True)
        q_ref[rs, sl] = (qh * lax.rsqrt(ssq * inv_d + EPS) * gcq).astype(BF16)

    ab = misc * (gkr_ref[...] * cs)
    lane2 = lax.broadcasted_iota(jnp.int32, (tm, HEAD_PAD), 1)
    swapped = jnp.where(lane2 < QK_HEAD_DIM, pltpu.roll(ab, HEAD_PAD - QK_ROPE_DIM, 1),
                        pltpu.roll(ab, QK_ROPE_DIM, 1))
    s_both = jnp.where(lane2 >= QK_NOPE_DIM, ab + swapped, 0.0)
    ssq_pe = jnp.sum(misc * misc * is_rope, axis=-1, keepdims=True)
    gkn = gkn_ref[...]
    for hh in range(ATTN_HEADS):
        sl = slice(hh * HEAD_PAD, (hh + 1) * HEAD_PAD)
        kh = k_pre[:, sl]
        ssq = jnp.sum(kh * kh, axis=-1, keepdims=True) + ssq_pe
        k_ref[rs, sl] = ((kh * gkn + s_both) * lax.rsqrt(ssq * inv_d + EPS)).astype(BF16)


def _win_prep_kernel(w_ref, o_ref):
    w = w_ref[...].astype(F32)
    o_kpe = Q_LORA_RANK + KV_LORA_RANK
    o_z = o_kpe + QK_ROPE_DIM
    o_dt = o_z + SSM_INNER + SSM_CONV_CH
    half = QK_ROPE_DIM // 2
    hpg = HEADS_PER_GROUP
    dt_f = w[:, o_dt:o_dt + SSM_HEADS]
    dt_b = w[:, o_dt + SSM_HEADS:o_dt + 2 * SSM_HEADS]
    kpe = w[:, o_kpe:o_z]
    pieces = [w[:, 0:o_kpe]]
    for g in range(SSM_GROUPS):
        pieces += [dt_f[:, g * hpg:(g + 1) * hpg], dt_b[:, g * hpg:(g + 1) * hpg]]
    pieces += [jnp.zeros((w.shape[0], QK_NOPE_DIM - 2 * SSM_HEADS), F32), kpe, -kpe[:, half:], kpe[:, :half],
               w[:, o_z:o_dt]]
    o_ref[...] = jnp.concatenate(pieces, axis=1).astype(BF16)


def _win_prep_call(w_in3, layer):
    depth, d, n = w_in3.shape
    rows = d // WIN_PREP_STEPS
    return pl.pallas_call(
        _win_prep_kernel,
        grid=(WIN_PREP_STEPS,),
        in_specs=[pl.BlockSpec((rows, n), lambda i: (layer * WIN_PREP_STEPS + i, 0))],
        out_specs=pl.BlockSpec((rows, IN_COLS), lambda i: (i, 0)),
        out_shape=jax.ShapeDtypeStruct((d, IN_COLS), BF16),
        compiler_params=pltpu.CompilerParams(dimension_semantics=("arbitrary",), vmem_limit_bytes=VMEM_LIMIT),
        name="win_prep",
    )(w_in3.reshape(depth * d, n).astype(BF16))


def _inproj_call(x2, cs, g, win, gqa, wuq, gkva, wuk, wuv, gq, gkn, gkr):
    t = x2.shape[0]
    cpt = TM_IN // SSM_CHUNK
    full = lambda a: pl.BlockSpec(a.shape, lambda i: (0,) * a.ndim, pipeline_mode=pl.Buffered(1))
    row = lambda w: pl.BlockSpec((TM_IN, w), lambda i: (i, 0))
    return pl.pallas_call(
        _inproj_kernel,
        grid=(t // TM_IN,),
        in_specs=[row(D_MODEL), pl.BlockSpec((QK_ROPE_DIM, TM_IN), lambda i: (0, i)), full(g), full(win), full(gqa), full(wuq), full(gkva),
                  full(wuk), full(wuv), full(gq), full(gkn), full(gkr)],
        out_specs=[row(ATTN_HEADS * HEAD_PAD), row(ATTN_HEADS * HEAD_PAD), row(ATTN_WIDTH),
                   row(SSM_INNER), row(SSM_CONV_CH),
                   pl.BlockSpec((cpt, 2 * SSM_HEADS, SSM_CHUNK), lambda i: (i, 0, 0))],
        out_shape=[jax.ShapeDtypeStruct((t, ATTN_HEADS * HEAD_PAD), BF16),
                   jax.ShapeDtypeStruct((t, ATTN_HEADS * HEAD_PAD), BF16),
                   jax.ShapeDtypeStruct((t, ATTN_WIDTH), BF16),
                   jax.ShapeDtypeStruct((t, SSM_INNER), BF16),
                   jax.ShapeDtypeStruct((t, SSM_CONV_CH), BF16),
                   jax.ShapeDtypeStruct((t // SSM_CHUNK, 2 * SSM_HEADS, SSM_CHUNK), F32)],
        compiler_params=pltpu.CompilerParams(dimension_semantics=("arbitrary",),
                                             vmem_limit_bytes=VMEM_LIMIT),
        name="inproj",
    )(x2, cs, g, win, gqa, wuq, gkva, wuk, wuv, gq, gkn, gkr)


def _attn_kernel(q_ref, k_ref, v_ref, o_ref, vext_scr):
    @pl.when(pl.program_id(2) == 0)
    def _build_v_tiles():
        low = (lax.broadcasted_iota(jnp.int32, (1, HEAD_PAD), 1) < V_HEAD_DIM).astype(F32).astype(BF16)
        high = (1.0 - low.astype(F32)).astype(BF16)
        for jp in range(ATTN_HEADS_PER_STEP // 2):
            pair = v_ref[:, jp * HEAD_PAD:(jp + 1) * HEAD_PAD]
            vext_scr[:, (2 * jp) * HEAD_PAD:(2 * jp + 1) * HEAD_PAD] = pair * low + high
            vext_scr[:, (2 * jp + 1) * HEAD_PAD:(2 * jp + 2) * HEAD_PAD] = pair * high + low

    lane = lax.broadcasted_iota(jnp.int32, (q_ref.shape[0], HEAD_PAD), 1)
    for jp in range(ATTN_HEADS_PER_STEP // 2):
        sls = [slice(j * HEAD_PAD, (j + 1) * HEAD_PAD) for j in (2 * jp, 2 * jp + 1)]
        ss = [_dot_nt(q_ref[:, sl], k_ref[:, sl]) for sl in sls]
        ps = [jnp.exp2(s - jnp.max(s, axis=-1, keepdims=True)).astype(BF16) for s in ss]
        accs = [_dot(p, vext_scr[:, sl]) for p, sl in zip(ps, sls)]
        res = [acc / pltpu.roll(acc, V_HEAD_DIM, 1) for acc in accs]
        o_ref[:, jp * HEAD_PAD:(jp + 1) * HEAD_PAD] = jnp.where(lane < V_HEAD_DIM, res[0], res[1])


def _attn_call(q, k, v, batch, seq):
    n_q = seq // TQ
    hps = ATTN_HEADS_PER_STEP
    return pl.pallas_call(
        _attn_kernel,
        grid=(batch, ATTN_HEADS // hps, n_q),
        in_specs=[pl.BlockSpec((TQ, hps * HEAD_PAD), lambda b, hp, i: (b * n_q + i, hp)),
                  pl.BlockSpec((seq, hps * HEAD_PAD), lambda b, hp, i: (b, hp)),
                  pl.BlockSpec((seq, hps * V_HEAD_DIM), lambda b, hp, i: (b, hp))],
        out_specs=pl.BlockSpec((TQ, hps * V_HEAD_DIM), lambda b, hp, i: (b * n_q + i, hp)),
        out_shape=jax.ShapeDtypeStruct((batch * seq, ATTN_WIDTH), F32),
        scratch_shapes=[pltpu.VMEM((seq, hps * HEAD_PAD), BF16)],
        compiler_params=pltpu.CompilerParams(
            dimension_semantics=("arbitrary", "arbitrary", "arbitrary"),
            vmem_limit_bytes=VMEM_LIMIT),
        name="attention",
    )(q, k, v)


def _split3(x):
    hi = x.astype(BF16)
    r1 = x - hi.astype(F32)
    mid = r1.astype(BF16)
    lo = (r1 - mid.astype(F32)).astype(BF16)
    return hi, mid, lo


def _ssd_kernel(x_ref, b_ref, c_ref, dt_ref, sh_ref, place_ref, lhs_ones_ref, rhs_const_ref,
                cwx_ref, cwb_ref, cwc_ref, cbx_ref, cbb_ref, cbc_ref, alog_ref, dtbias_ref, dskip_ref,
                o_ref, xc_scr, xm_scr, nsf_scr, nsb_scr, dtv_scr, w_scr, dec_scr, lhs_scr, rhs_scr, rpart_scr):
    nc = dt_ref.shape[0]
    seq = x_ref.shape[0]
    L = SSM_CHUNK
    hpg = HEADS_PER_GROUP
    npair = hpg // 2
    P = SSM_HEAD_DIM
    x_cols = slice(0, GROUP_INNER)
    b_cols = slice(GROUP_INNER, GROUP_INNER + SSM_STATE)
    c_cols = slice(GROUP_INNER + SSM_STATE, GROUP_COLS)
    pair_cols = [slice(j * L, (j + 1) * L) for j in range(npair)]

    @pl.when((pl.program_id(0) == 0) & (pl.program_id(1) == 0))
    def _init_rhs():
        for u in range(SSD_UNROLL):
            rhs_scr[u] = rhs_const_ref[...]

    a_neg = -jnp.exp(alog_ref[...]) * LOG2E
    dtv = jax.nn.softplus(dt_ref[...] + dtbias_ref[...][None])
    dtv_scr[...] = dtv
    da2 = (dtv * a_neg[None]).reshape(nc * DT_ROWS, L)
    ri = lax.broadcasted_iota(jnp.int32, (L, L), 0)
    ci = lax.broadcasted_iota(jnp.int32, (L, L), 1)
    upper = (ri <= ci).astype(BF16)
    lower = (ri >= ci).astype(BF16)
    tri = jnp.concatenate([upper, lower], axis=1)
    cs_fb = sum(_dot(p, tri) for p in _split3(da2))
    rowsel = (ri & hpg) == 0
    cs2 = jnp.where(rowsel, cs_fb[:, 0:L], cs_fb[:, L:2 * L])
    colcs = cs2.T

    col_parts = jnp.concatenate(_split3(colcs), axis=1)
    half_n = place_ref.shape[1] // 2
    for hf in range(2):
        cols = slice(hf * half_n, (hf + 1) * half_n)
        lhs_half = _dot(col_parts, place_ref[:, cols]) + lhs_ones_ref[:, cols]
        for c in range(nc // 2):
            lhs_scr[hf * (nc // 2) + c] = lhs_half[:, c * L:c * L + SEG_K].astype(BF16)
    for i, part in enumerate(_split3(-cs2)):
        rpart_scr[i] = part.astype(F32).reshape(nc, DT_ROWS, L)

    def lane_bcast(col):
        return jnp.broadcast_to(col, (col.shape[0], L))

    tot2 = jnp.where(rowsel, lane_bcast(cs2[:, L - 1:L]), lane_bcast(cs2[:, 0:1]))
    dec_scr[...] = jnp.exp2(tot2).reshape(nc, DT_ROWS, L)
    w_scr[...] = (dtv.reshape(nc * DT_ROWS, L) * jnp.exp2(tot2 - cs2)).reshape(nc, DT_ROWS, L)

    lane_t = lax.broadcasted_iota(jnp.int32, (L, L), 1)
    low_half = lane_t < P

    cw_halves = (cwx_ref[...], jnp.concatenate([cwb_ref[...], cwc_ref[...]], axis=1))
    cb_halves = (cbx_ref[...], jnp.concatenate([cbb_ref[...], cbc_ref[...]], axis=1))

    def rows(hf, start, size):
        if hf == 0:
            return x_ref[pl.ds(start, size), :]
        return jnp.concatenate([b_ref[pl.ds(start, size), :], c_ref[pl.ds(start, size), :]], axis=1)

    def conv_stage(c):
        base = pl.multiple_of(c * L, L)
        ws = pl.multiple_of(jnp.clip(c * L - CONV_WIN_LEAD, 0, seq - CONV_WIN), CONV_WIN_LEAD)
        variant = jnp.where(c == 0, 0, jnp.where(c == nc - 1, 2, 1))
        sh = sh_ref[variant]
        halves = []
        for hf in range(2):
            cw = cw_halves[hf]
            shifted = _dot(sh, rows(hf, ws, CONV_WIN))
            acc = cb_halves[hf] + cw[SSM_CONV // 2:SSM_CONV // 2 + 1, :] * rows(hf, base, L).astype(F32)
            for t, kk in enumerate(CONV_SHIFTED_TAPS):
                acc = acc + cw[kk:kk + 1, :] * shifted[t * L:(t + 1) * L, :]
            halves.append(acc * jax.nn.sigmoid(acc))
        xc = jnp.concatenate(halves, axis=1)
        xc_scr[c] = xc
        for j in range(npair):
            xp = xc[:, pair_cols[j]]
            xm_scr[c, j] = jnp.concatenate([jnp.where(low_half, xp, 0.0), jnp.where(low_half, 0.0, xp)],
                                           axis=0).astype(BF16)

    def state_loads(c):
        return xc_scr[c, :, b_cols], [xm_scr[c, j] for j in range(npair)]

    def state_stage(c, loaded):
        b_tok, xm = loaded
        bt = b_tok.T
        for d, ns_scr in ((0, nsf_scr), (1, nsb_scr)):
            w = w_scr[c, d * hpg:(d + 1) * hpg, :]
            ns_scr[c] = jnp.concatenate(
                [_dot(jnp.concatenate([(bt * w[2 * j:2 * j + 1, :]).astype(BF16),
                                       (bt * w[2 * j + 1:2 * j + 2, :]).astype(BF16)], axis=1), xm[j])
                 for j in range(npair)], axis=1)

    for c in range(CONV_AHEAD):
        conv_stage(c)

    def pass_a(c, carry):
        loaded = state_loads(c)
        conv_stage(c + CONV_AHEAD)
        state_stage(c, loaded)
        return carry
    lax.fori_loop(0, nc - CONV_AHEAD, pass_a, 0, unroll=PASS_A_UNROLL)
    for c in range(nc - CONV_AHEAD, nc):
        state_stage(c, state_loads(c))

    lane_r = lax.broadcasted_iota(jnp.int32, (1, L), 1)

    def decay_row(c, d):
        dec = dec_scr[c, d * hpg:(d + 1) * hpg, :]
        return jnp.concatenate([jnp.where(lane_r < P, dec[2 * j:2 * j + 1, :], dec[2 * j + 1:2 * j + 2, :])
                                for j in range(npair)], axis=1)

    def pass_b(d, ns_scr):
        def body(i, st):
            c = i if d == 0 else nc - 1 - i
            new = ns_scr[c]
            ns_scr[c] = st
            return st * decay_row(c, d) + new
        lax.fori_loop(0, nc, body, jnp.zeros((SSM_STATE, GROUP_INNER), F32))
    pass_b(0, nsf_scr)
    pass_b(1, nsb_scr)

    mask_f = ci <= ri
    mask_b = ci >= ri
    neg_inf = jnp.float32(-jnp.inf)
    sub16 = lax.broadcasted_iota(jnp.int32, (BF16_ROWS, L), 0)

    def pass_c(c, carry):
        base = pl.multiple_of(c * L, L)
        xc = xc_scr[c]
        bm = xc[:, b_cols].astype(BF16)
        cm = xc[:, c_cols].astype(BF16)
        cb = _dot_nt(cm, bm)
        off_f = _dot(cm, nsf_scr[c].astype(BF16))
        off_b = _dot(cm, nsb_scr[c].astype(BF16))
        slot = c % SSD_UNROLL
        for hd in range(DT_ROWS):
            r0 = (hd % 2) * SEG_K_PER_ROW
            blk = jnp.where((sub16 >= r0) & (sub16 < r0 + 3), 1.0, 0.0)
            for i in range(3):
                blk = jnp.where(sub16 == r0 + 3 + i, rpart_scr[i, c, hd:hd + 1, :], blk)
            rhs_scr[slot, (hd // 2) * BF16_ROWS:(hd // 2 + 1) * BF16_ROWS, hd * L:(hd + 1) * L] = blk.astype(BF16)
        seg = _dot(lhs_scr[c], rhs_scr[slot])
        dt_f_r = dtv_scr[c, 0:hpg, :]
        dt_b_r = dtv_scr[c, hpg:DT_ROWS, :]
        ys = []
        for j in range(npair):
            ws_ = []
            for h in (2 * j, 2 * j + 1):
                ef = jnp.exp2(jnp.where(mask_f, seg[:, h * L:(h + 1) * L], neg_inf))
                eb = jnp.exp2(jnp.where(mask_b, seg[:, (hpg + h) * L:(hpg + h + 1) * L], neg_inf))
                ws_.append((cb * (ef * dt_f_r[h:h + 1, :] + eb * dt_b_r[h:h + 1, :])).astype(BF16))
            scale_f = jnp.exp2(seg[:, (DT_ROWS + j) * L:(DT_ROWS + j + 1) * L])
            scale_b = jnp.exp2(seg[:, (DT_ROWS + npair + j) * L:(DT_ROWS + npair + j + 1) * L])
            ys.append(_dot(jnp.concatenate(ws_, axis=1), xm_scr[c, j])
                      + scale_f * off_f[:, pair_cols[j]] + scale_b * off_b[:, pair_cols[j]])
        y = jnp.concatenate(ys, axis=1) + dskip_ref[...] * xc[:, x_cols]
        o_ref[pl.ds(base, L), :] = y
        return carry
    lax.fori_loop(0, nc, pass_c, 0, unroll=SSD_UNROLL)


def _conv_shift_matrices():
    out = np.zeros((3, len(CONV_SHIFTED_TAPS) * SSM_CHUNK, CONV_WIN), np.float32)
    for v, lead in enumerate((0, CONV_WIN_LEAD, CONV_WIN - SSM_CHUNK)):
        for i, kk in enumerate(CONV_SHIFTED_TAPS):
            for t in range(SSM_CHUNK):
                j = t + kk - SSM_CONV // 2 + lead
                if 0 <= j < CONV_WIN:
                    out[v, i * SSM_CHUNK + t, j] = 1.0
    return out


def _seg_matmul_constants(nc):
    L, K, half = SSM_CHUNK, SEG_K_PER_ROW, SSM_HEAD_DIM
    place = np.zeros((3 * L, nc * L), np.float32)
    lhs_ones = np.zeros((1, nc * L), np.float32)
    for c in range(nc):
        for hd in range(DT_ROWS):
            for i in range(3):
                place[i * L + c * DT_ROWS + hd, c * L + hd * K + i] = 1.0
                lhs_ones[0, c * L + hd * K + 3 + i] = 1.0
    rhs = np.zeros((SEG_K, SEG_BLOCKS * L), np.float32)
    for hd in range(DT_ROWS):
        d, h = divmod(hd, HEADS_PER_GROUP)
        pair_block = DT_ROWS + d * (HEADS_PER_GROUP // 2) + h // 2
        lanes = slice(0, half) if h % 2 == 0 else slice(half, L)
        rhs[hd * K:hd * K + 3, hd * L:(hd + 1) * L] = 1.0
        rhs[hd * K:hd * K + 3, pair_block * L:(pair_block + 1) * L][:, lanes] = 1.0
    return place, lhs_ones, rhs


def _ssd_call(xbc, dtc, cw, cbias, alog, dtbias, dskip, batch, seq):
    nc = seq // SSM_CHUNK
    L = SSM_CHUNK
    assert nc >= 3 and seq >= CONV_WIN and nc * DT_ROWS == L
    sh = jnp.asarray(_conv_shift_matrices(), BF16)
    place, lhs_ones, rhs_const = _seg_matmul_constants(nc)
    place = jnp.asarray(place, BF16)
    lhs_ones = jnp.asarray(lhs_ones, F32)
    rhs_const = jnp.asarray(rhs_const, BF16)
    const = lambda a: pl.BlockSpec(a.shape, lambda b, g: (0,) * a.ndim)
    b_blk0 = SSM_INNER // SSM_STATE
    c_blk0 = b_blk0 + SSM_GROUPS

    def xbc_views(rows_, batched):
        lead = (lambda b: b) if batched else (lambda b: 0)
        return [pl.BlockSpec((rows_, GROUP_INNER), lambda b, g: (lead(b), g)),
                pl.BlockSpec((rows_, SSM_STATE), lambda b, g: (lead(b), b_blk0 + g)),
                pl.BlockSpec((rows_, SSM_STATE), lambda b, g: (lead(b), c_blk0 + g))]
    return pl.pallas_call(
        _ssd_kernel,
        grid=(batch, SSM_GROUPS),
        in_specs=xbc_views(seq, True) + [
                  pl.BlockSpec((nc, DT_ROWS, L), lambda b, g: (b, g, 0)),
                  const(sh), const(place), const(lhs_ones), const(rhs_const)]
                 + xbc_views(SUBLANES, False) + xbc_views(1, False) + [
                  pl.BlockSpec((DT_ROWS, L), lambda b, g: (g, 0)),
                  pl.BlockSpec((DT_ROWS, L), lambda b, g: (g, 0)),
                  pl.BlockSpec((1, GROUP_INNER), lambda b, g: (0, g))],
        out_specs=pl.BlockSpec((seq, GROUP_INNER), lambda b, g: (b, g)),
        out_shape=jax.ShapeDtypeStruct((batch * seq, SSM_INNER), F32),
        scratch_shapes=[pltpu.VMEM((nc, L, GROUP_COLS), F32),
                        pltpu.VMEM((nc, HEADS_PER_GROUP // 2, 2 * L, L), BF16),
                        pltpu.VMEM((nc, SSM_STATE, GROUP_INNER), F32),
                        pltpu.VMEM((nc, SSM_STATE, GROUP_INNER), F32),
                        pltpu.VMEM((nc, DT_ROWS, L), F32),
                        pltpu.VMEM((nc, DT_ROWS, L), F32),
                        pltpu.VMEM((nc, DT_ROWS, L), F32),
                        pltpu.VMEM((nc, L, SEG_K), BF16),
                        pltpu.VMEM((SSD_UNROLL, SEG_K, SEG_BLOCKS * L), BF16),
                        pltpu.VMEM((3, nc, DT_ROWS, L), F32)],
        compiler_params=pltpu.CompilerParams(dimension_semantics=("arbitrary", "arbitrary"),
                                             vmem_limit_bytes=VMEM_LIMIT),
        name="ssd",
    )(xbc, xbc, xbc, dtc, sh, place, lhs_ones, rhs_const, cw, cw, cw, cbias, cbias, cbias,
      alog, dtbias, dskip)


def _mlp_kernel(x_ref, attn_ref, ssm_ref, z_ref, gat_ref, gn_ref, wo_ref, gm_ref, wup_ref, wdn_ref, o_ref):
    a = attn_ref[...]
    an = (a * _inv_rms(a) * gat_ref[...]).astype(BF16)
    zc = z_ref[...].astype(F32)
    y = ssm_ref[...] * (zc * jax.nn.sigmoid(zc))
    yn = jnp.concatenate([y[:, g * GROUP_INNER:(g + 1) * GROUP_INNER]
                          * _inv_rms(y[:, g * GROUP_INNER:(g + 1) * GROUP_INNER]) for g in range(SSM_GROUPS)],
                         axis=1) * gn_ref[...]
    mix = _dot(an, wo_ref[0:ATTN_WIDTH, :]) + _dot(yn.astype(BF16), wo_ref[ATTN_WIDTH:D_MIX, :])
    x1 = x_ref[...] + mix
    hm = (x1 * _inv_rms(x1) * gm_ref[...]).astype(BF16)
    acc = jnp.zeros_like(x1)
    for c in range(D_FF // FF_CHUNK):
        cols = slice(c * FF_CHUNK, (c + 1) * FF_CHUNK)
        u = _dot(hm, wup_ref[:, cols])
        acc = acc + _dot(jnp.square(jnp.maximum(u, 0.0)).astype(BF16), wdn_ref[cols, :])
    o_ref[...] = x1 + acc


def _mlp_call(x2, attn, ssm, z, gat, gn, wo, gm, wup, wdn):
    t = x2.shape[0]
    row = lambda w: pl.BlockSpec((TM_MLP, w), lambda i: (i, 0))
    full = lambda a: pl.BlockSpec(a.shape, lambda i: (0, 0), pipeline_mode=pl.Buffered(1))
    return pl.pallas_call(
        _mlp_kernel,
        grid=(t // TM_MLP,),
        in_specs=[row(D_MODEL), row(ATTN_WIDTH), row(SSM_INNER), row(SSM_INNER), full(gat), full(gn), full(wo),
                  full(gm), full(wup), full(wdn)],
        out_specs=row(D_MODEL),
        out_shape=jax.ShapeDtypeStruct((t, D_MODEL), F32),
        compiler_params=pltpu.CompilerParams(dimension_semantics=("arbitrary",),
                                             vmem_limit_bytes=VMEM_LIMIT),
        name="outproj_mlp",
    )(x2, attn, ssm, z, gat, gn, wo, gm, wup, wdn)


def _group_major(fwd, bwd):
    lead = fwd.shape[:-1]
    both = jnp.stack([fwd.reshape(lead + (SSM_GROUPS, HEADS_PER_GROUP)),
                      bwd.reshape(lead + (SSM_GROUPS, HEADS_PER_GROUP))], axis=-2)
    return both.reshape(lead + (2 * SSM_HEADS,))


def _rotate_half(w):
    half = QK_ROPE_DIM // 2
    return jnp.concatenate([-w[..., half:], w[..., :half]], axis=-1)


def _swap_halves(g):
    half = QK_ROPE_DIM // 2
    return jnp.concatenate([g[..., half:], g[..., :half]], axis=-1)


def _lane_bcast(v):
    return jnp.broadcast_to(v[..., None], v.shape + (LANES,)).astype(F32)


def _layer(x2, cs, batch, seq, ln_mix_g, w_in, q_a_norm_g, w_uq, kv_a_norm_g, w_ukv, q_norm_g,
           k_norm_g, attn_out_norm_g, conv_w, conv_b, a_log_fwd, a_log_bwd, dt_bias_fwd, dt_bias_bwd,
           d_skip, ssm_norm_g, w_out, ln_mlp_g, w_mlp_up, w_mlp_down):
    win = _win_prep_call(*w_in)
    wq = w_uq.reshape(Q_LORA_RANK, ATTN_HEADS, QK_HEAD_DIM)
    wuq = jnp.concatenate([wq, _rotate_half(wq[..., QK_NOPE_DIM:])], axis=-1)
    wuq = wuq.reshape(Q_LORA_RANK, ATTN_HEADS * HEAD_PAD).astype(BF16)
    wukv = w_ukv.reshape(KV_LORA_RANK, ATTN_HEADS, QK_NOPE_DIM + V_HEAD_DIM)
    wuk = jnp.pad(wukv[:, :, :QK_NOPE_DIM],
                  ((0, 0), (0, 0), (0, HEAD_PAD - QK_NOPE_DIM))).reshape(KV_LORA_RANK, -1).astype(BF16)
    wuv = wukv[:, :, QK_NOPE_DIM:].reshape(KV_LORA_RANK, -1).astype(BF16)
    scale = QK_HEAD_DIM ** -0.5 * np.log2(np.e)
    gq = (jnp.concatenate([q_norm_g, _swap_halves(q_norm_g[QK_NOPE_DIM:])]) * scale)[None, :]
    zeros_nope = jnp.zeros((QK_NOPE_DIM,), k_norm_g.dtype)
    gkn = jnp.concatenate([k_norm_g[:QK_NOPE_DIM], zeros_nope])[None, :]
    gkr = jnp.concatenate([zeros_nope, k_norm_g[QK_NOPE_DIM:], _swap_halves(k_norm_g[QK_NOPE_DIM:])])[None, :]

    q, k, v, z, xbc, dtc = _inproj_call(
        x2, cs, ln_mix_g[None, :], win, q_a_norm_g[None, :], wuq, kv_a_norm_g[None, :], wuk, wuv,
        gq, gkn, gkr)

    attn = _attn_call(q, k, v, batch, seq)

    cw = jnp.pad(conv_w[:, 0, :], ((0, SUBLANES - SSM_CONV), (0, 0)))
    cbias = conv_b[None, :]
    alog = _lane_bcast(_group_major(a_log_fwd, a_log_bwd))
    dtbias = _lane_bcast(_group_major(dt_bias_fwd, dt_bias_bwd))
    dskip = jnp.repeat(d_skip, SSM_HEAD_DIM)[None, :]
    ssm = _ssd_call(xbc, dtc, cw, cbias, alog, dtbias, dskip, batch, seq)

    return _mlp_call(x2, attn, ssm, z, attn_out_norm_g[None, :], ssm_norm_g[None, :], w_out.astype(BF16),
                     ln_mlp_g[None, :], w_mlp_up.astype(BF16), w_mlp_down.astype(BF16))


def _rope_table(positions):
    inv_freq = 1.0 / (ROPE_THETA ** (jnp.arange(0, QK_ROPE_DIM, 2, dtype=F32) / QK_ROPE_DIM))
    ang = inv_freq[:, None] * positions.astype(F32).reshape(1, -1)
    return jnp.concatenate([jnp.cos(ang), jnp.sin(ang)], axis=0)


def kernel(x, positions, ln_mix_g, w_in, q_a_norm_g, w_uq, kv_a_norm_g, w_ukv, q_norm_g, k_norm_g,
           attn_out_norm_g, conv_w, conv_b, a_log_fwd, a_log_bwd, dt_bias_fwd, dt_bias_bwd, d_skip,
           ssm_norm_g, w_out, ln_mlp_g, w_mlp_up, w_mlp_down):
    batch, seq, d = x.shape
    assert d == D_MODEL and seq % TQ == 0 and (batch * seq) % TM_IN == 0 and (batch * seq) % TM_MLP == 0
    cs = _rope_table(positions)
    x2 = x.reshape(batch * seq, d)
    for l in range(ln_mix_g.shape[0]):
        x2 = _layer(x2, cs, batch, seq, ln_mix_g[l], (w_in, l), q_a_norm_g[l], w_uq[l], kv_a_norm_g[l],
                    w_ukv[l], q_norm_g[l], k_norm_g[l], attn_out_norm_g[l], conv_w[l], conv_b[l],
                    a_log_fwd[l], a_log_bwd[l], dt_bias_fwd[l], dt_bias_bwd[l], d_skip[l], ssm_norm_g[l],
                    w_out[l], ln_mlp_g[l], w_mlp_up[l], w_mlp_down[l])
    return x2.reshape(batch, seq, d)
```

```python
import numpy as np
import jax
import jax.numpy as jnp
from jax import lax
from jax.experimental import pallas as pl
from jax.experimental.pallas import tpu as pltpu

F32 = jnp.float32
BF16 = jnp.bfloat16

D_MODEL = 1024
ATTN_HEADS = 8
QK_NOPE_DIM = 64
QK_ROPE_DIM = 32
QK_HEAD_DIM = QK_NOPE_DIM + QK_ROPE_DIM
V_HEAD_DIM = 64
Q_LORA_RANK = D_MODEL // 4
KV_LORA_RANK = D_MODEL // 8
ROPE_THETA = 10000.0
ATTN_WIDTH = ATTN_HEADS * V_HEAD_DIM
SSM_HEADS = 8
SSM_HEAD_DIM = 64
SSM_INNER = SSM_HEADS * SSM_HEAD_DIM
SSM_GROUPS = 2
SSM_STATE = 128
SSM_CONV = 5
SSM_CHUNK = 128
SSM_CONV_CH = SSM_INNER + 2 * SSM_GROUPS * SSM_STATE
D_MIX = ATTN_WIDTH + SSM_INNER
D_FF = 4 * D_MODEL
EPS = 1e-6

LANES = 128
SUBLANES = 8
HEAD_PAD = LANES

HEADS_PER_GROUP = SSM_HEADS // SSM_GROUPS
GROUP_INNER = SSM_INNER // SSM_GROUPS
GROUP_COLS = GROUP_INNER + 2 * SSM_STATE
DT_ROWS = 2 * HEADS_PER_GROUP
BF16_ROWS = 16
SEG_K_PER_ROW = 8
SEG_K = DT_ROWS * SEG_K_PER_ROW
SEG_BLOCKS = DT_ROWS + HEADS_PER_GROUP
LOG2E = float(np.log2(np.e))
CONV_WIN = 2 * SSM_CHUNK
CONV_WIN_LEAD = SSM_CHUNK // 2
CONV_SHIFTED_TAPS = tuple(k for k in range(SSM_CONV) if k != SSM_CONV // 2)

COL_CKV = Q_LORA_RANK
COL_MISC = COL_CKV + KV_LORA_RANK
COL_Z = COL_MISC + LANES
COL_XBC = COL_Z + SSM_INNER
IN_COLS = COL_XBC + SSM_CONV_CH

TM_IN = 1024
IN_SUBTILE = 512
TQ = 1024
ATTN_HEADS_PER_STEP = 8
TM_MLP = 1024
MLP_SUBTILE = 512
FF_CHUNK = 1024
SSD_UNROLL = 8
CONV_AHEAD = 4
PASS_A_UNROLL = 4
WIN_PREP_STEPS = 4
VMEM_LIMIT = 56 * 1024 * 1024


def _inv_rms(x):
    n = x.shape[-1]
    x2 = x * x
    acc = x2[:, 0:LANES]
    for i in range(1, n // LANES):
        acc = acc + x2[:, i * LANES:(i + 1) * LANES]
    return lax.rsqrt(jnp.sum(acc, axis=-1, keepdims=True) * (1.0 / n) + EPS)


def _dot(a, b):
    return jnp.dot(a, b, preferred_element_type=F32)


def _dot_nt(a, b):
    return lax.dot_general(a, b, (((1,), (1,)), ((), ())), preferred_element_type=F32)


def _inproj_kernel(x_ref, cs_ref, g_ref, win_ref, gqa_ref, wuq_ref, gkva_ref, wuk_ref, wuv_ref,
                   gq_ref, gkn_ref, gkr_ref,
                   q_ref, k_ref, v_ref, z_ref, xbc_ref, dt_ref):
    for sub in range(x_ref.shape[0] // IN_SUBTILE):
        _inproj_rows(sub * IN_SUBTILE, x_ref, cs_ref, g_ref, win_ref, gqa_ref, wuq_ref, gkva_ref, wuk_ref,
                     wuv_ref, gq_ref, gkn_ref, gkr_ref, q_ref, k_ref, v_ref, z_ref, xbc_ref, dt_ref)


def _inproj_rows(r0, x_ref, cs_ref, g_ref, win_ref, gqa_ref, wuq_ref, gkva_ref, wuk_ref, wuv_ref,
                 gq_ref, gkn_ref, gkr_ref, q_ref, k_ref, v_ref, z_ref, xbc_ref, dt_ref):
    tm = IN_SUBTILE
    rs = slice(r0, r0 + tm)
    x = x_ref[rs, :]
    h = (x * _inv_rms(x) * g_ref[...]).astype(BF16)
    big = _dot(h, win_ref[...])
    z_ref[rs, :] = big[:, COL_Z:COL_XBC].astype(BF16)
    xbc_ref[rs, :] = big[:, COL_XBC:IN_COLS].astype(BF16)
    misc = big[:, COL_MISC:COL_Z]
    for c in range(tm // SSM_CHUNK):
        dt_ref[r0 // SSM_CHUNK + c] = misc[c * SSM_CHUNK:(c + 1) * SSM_CHUNK, :].T[0:2 * SSM_HEADS, :]

    cq = big[:, 0:COL_CKV]
    ckv = big[:, COL_CKV:COL_MISC]
    cqn = (cq * _inv_rms(cq) * gqa_ref[...]).astype(BF16)
    ckvn = (ckv * _inv_rms(ckv) * gkva_ref[...]).astype(BF16)
    q_pre = _dot(cqn, wuq_ref[...])
    k_pre = _dot(ckvn, wuk_ref[...])
    v_ref[rs, :] = _dot(ckvn, wuv_ref[...]).astype(BF16)

    half = QK_ROPE_DIM // 2
    zpad = jnp.zeros((HEAD_PAD - QK_ROPE_DIM, SSM_CHUNK), F32)
    tt = jnp.concatenate(
        [jnp.concatenate([cs_ref[:, r0 + c * SSM_CHUNK:r0 + (c + 1) * SSM_CHUNK], zpad], axis=0).T
         for c in range(tm // SSM_CHUNK)], axis=0)
    lane_t = lax.broadcasted_iota(jnp.int32, (tm, HEAD_PAD), 1)
    cs = jnp.where(lane_t < QK_NOPE_DIM, 1.0,
                   jnp.where(lane_t < QK_NOPE_DIM + half, pltpu.roll(tt, QK_NOPE_DIM, 1),
                             jnp.where(lane_t < QK_HEAD_DIM + half, pltpu.roll(tt, QK_NOPE_DIM + half, 1),
                                       pltpu.roll(tt, QK_HEAD_DIM, 1))))
    lane = lax.broadcasted_iota(jnp.int32, (1, HEAD_PAD), 1)
    in_head = (lane < QK_HEAD_DIM).astype(F32)
    is_rope = ((lane >= QK_NOPE_DIM) & (lane < QK_HEAD_DIM)).astype(F32)
    inv_d = 1.0 / QK_HEAD_DIM

    gcq = gq_ref[...] * cs
    for hh in range(ATTN_HEADS):
        sl = slice(hh * HEAD_PAD, (hh + 1) * HEAD_PAD)
        qh = q_pre[:, sl]
        ssq = jnp.sum(qh * qh * in_head, axis=-1, keepdims=True)
        q_ref[rs, sl] = (qh * lax.rsqrt(ssq * inv_d + EPS) * gcq).astype(BF16)

    ab = misc * (gkr_ref[...] * cs)
    lane2 = lax.broadcasted_iota(jnp.int32, (tm, HEAD_PAD), 1)
    swapped = jnp.where(lane2 < QK_HEAD_DIM, pltpu.roll(ab, HEAD_PAD - QK_ROPE_DIM, 1),
                        pltpu.roll(ab, QK_ROPE_DIM, 1))
    s_both = jnp.where(lane2 >= QK_NOPE_DIM, ab + swapped, 0.0)
    ssq_pe = jnp.sum(misc * misc * is_rope, axis=-1, keepdims=True)
    gkn = gkn_ref[...]
    for hh in range(ATTN_HEADS):
        sl = slice(hh * HEAD_PAD, (hh + 1) * HEAD_PAD)
        kh = k_pre[:, sl]
        ssq = jnp.sum(kh * kh, axis=-1, keepdims=True) + ssq_pe
        k_ref[rs, sl] = ((kh * gkn + s_both) * lax.rsqrt(ssq * inv_d + EPS)).astype(BF16)


def _win_prep_kernel(w_ref, o_ref):
    w = w_ref[...].astype(F32)
    o_kpe = Q_LORA_RANK + KV_LORA_RANK
    o_z = o_kpe + QK_ROPE_DIM
    o_dt = o_z + SSM_INNER + SSM_CONV_CH
    half = QK_ROPE_DIM // 2
    hpg = HEADS_PER_GROUP
    dt_f = w[:, o_dt:o_dt + SSM_HEADS]
    dt_b = w[:, o_dt + SSM_HEADS:o_dt + 2 * SSM_HEADS]
    kpe = w[:, o_kpe:o_z]
    pieces = [w[:, 0:o_kpe]]
    for g in range(SSM_GROUPS):
        pieces += [dt_f[:, g * hpg:(g + 1) * hpg], dt_b[:, g * hpg:(g + 1) * hpg]]
    pieces += [jnp.zeros((w.shape[0], QK_NOPE_DIM - 2 * SSM_HEADS), F32), kpe, -kpe[:, half:], kpe[:, :half],
               w[:, o_z:o_dt]]
    o_ref[...] = jnp.concatenate(pieces, axis=1).astype(BF16)


def _win_prep_call(w_in3, layer):
    depth, d, n = w_in3.shape
    rows = d // WIN_PREP_STEPS
    return pl.pallas_call(
        _win_prep_kernel,
        grid=(WIN_PREP_STEPS,),
        in_specs=[pl.BlockSpec((rows, n), lambda i: (layer * WIN_PREP_STEPS + i, 0))],
        out_specs=pl.BlockSpec((rows, IN_COLS), lambda i: (i, 0)),
        out_shape=jax.ShapeDtypeStruct((d, IN_COLS), BF16),
        compiler_params=pltpu.CompilerParams(dimension_semantics=("arbitrary",), vmem_limit_bytes=VMEM_LIMIT),
        name="win_prep",
    )(w_in3.reshape(depth * d, n).astype(BF16))


def _inproj_call(x2, cs, g, win, gqa, wuq, gkva, wuk, wuv, gq, gkn, gkr):
    t = x2.shape[0]
    cpt = TM_IN // SSM_CHUNK
    full = lambda a: pl.BlockSpec(a.shape, lambda i: (0,) * a.ndim, pipeline_mode=pl.Buffered(1))
    row = lambda w: pl.BlockSpec((TM_IN, w), lambda i: (i, 0))
    return pl.pallas_call(
        _inproj_kernel,
        grid=(t // TM_IN,),
        in_specs=[row(D_MODEL), pl.BlockSpec((QK_ROPE_DIM, TM_IN), lambda i: (0, i)), full(g), full(win), full(gqa), full(wuq), full(gkva),
                  full(wuk), full(wuv), full(gq), full(gkn), full(gkr)],
        out_specs=[row(ATTN_HEADS * HEAD_PAD), row(ATTN_HEADS * HEAD_PAD), row(ATTN_WIDTH),
                   row(SSM_INNER), row(SSM_CONV_CH),
                   pl.BlockSpec((cpt, 2 * SSM_HEADS, SSM_CHUNK), lambda i: (i, 0, 0))],
        out_shape=[jax.ShapeDtypeStruct((t, ATTN_HEADS * HEAD_PAD), BF16),
                   jax.ShapeDtypeStruct((t, ATTN_HEADS * HEAD_PAD), BF16),
                   jax.ShapeDtypeStruct((t, ATTN_WIDTH), BF16),
                   jax.ShapeDtypeStruct((t, SSM_INNER), BF16),
                   jax.ShapeDtypeStruct((t, SSM_CONV_CH), BF16),
                   jax.ShapeDtypeStruct((t // SSM_CHUNK, 2 * SSM_HEADS, SSM_CHUNK), F32)],
        compiler_params=pltpu.CompilerParams(dimension_semantics=("arbitrary",),
                                             vmem_limit_bytes=VMEM_LIMIT),
        name="inproj",
    )(x2, cs, g, win, gqa, wuq, gkva, wuk, wuv, gq, gkn, gkr)


def _attn_kernel(q_ref, k_ref, v_ref, *refs):
    n_w = (len(refs) - 2) // 2
    w_in_refs, o_ref, w_out_refs, vext_scr = refs[:n_w], refs[n_w], refs[n_w + 1:2 * n_w + 1], refs[-1]
    for w_in_ref, w_out_ref in zip(w_in_refs, w_out_refs):
        w_out_ref[...] = w_in_ref[...].astype(BF16)

    @pl.when(pl.program_id(2) == 0)
    def _build_v_tiles():
        low = (lax.broadcasted_iota(jnp.int32, (1, HEAD_PAD), 1) < V_HEAD_DIM).astype(F32).astype(BF16)
        high = (1.0 - low.astype(F32)).astype(BF16)
        for jp in range(ATTN_HEADS_PER_STEP // 2):
            pair = v_ref[:, jp * HEAD_PAD:(jp + 1) * HEAD_PAD]
            vext_scr[:, (2 * jp) * HEAD_PAD:(2 * jp + 1) * HEAD_PAD] = pair * low + high
            vext_scr[:, (2 * jp + 1) * HEAD_PAD:(2 * jp + 2) * HEAD_PAD] = pair * high + low

    lane = lax.broadcasted_iota(jnp.int32, (q_ref.shape[0], HEAD_PAD), 1)
    for jp in range(ATTN_HEADS_PER_STEP // 2):
        sls = [slice(j * HEAD_PAD, (j + 1) * HEAD_PAD) for j in (2 * jp, 2 * jp + 1)]
        ss = [_dot_nt(q_ref[:, sl], k_ref[:, sl]) for sl in sls]
        ps = [jnp.exp2(s - jnp.max(s, axis=-1, keepdims=True)).astype(BF16) for s in ss]
        accs = [_dot(p, vext_scr[:, sl]) for p, sl in zip(ps, sls)]
        res = [acc / pltpu.roll(acc, V_HEAD_DIM, 1) for acc in accs]
        o_ref[:, jp * HEAD_PAD:(jp + 1) * HEAD_PAD] = jnp.where(lane < V_HEAD_DIM, res[0], res[1])


def _attn_call(q, k, v, later_weights, batch, seq):
    n_q = seq // TQ
    hps = ATTN_HEADS_PER_STEP
    assert ATTN_HEADS == hps
    n_steps = batch * n_q

    def rows_spec(w):
        assert w.shape[0] % (n_steps * BF16_ROWS) == 0
        return pl.BlockSpec((w.shape[0] // n_steps, w.shape[1]), lambda b, hp, i: (b * n_q + i, 0))
    w_specs = [rows_spec(w) for w in later_weights]
    outs = pl.pallas_call(
        _attn_kernel,
        grid=(batch, ATTN_HEADS // hps, n_q),
        in_specs=[pl.BlockSpec((TQ, hps * HEAD_PAD), lambda b, hp, i: (b * n_q + i, hp)),
                  pl.BlockSpec((seq, hps * HEAD_PAD), lambda b, hp, i: (b, hp)),
                  pl.BlockSpec((seq, hps * V_HEAD_DIM), lambda b, hp, i: (b, hp))] + w_specs,
        out_specs=[pl.BlockSpec((TQ, hps * V_HEAD_DIM), lambda b, hp, i: (b * n_q + i, hp))] + w_specs,
        out_shape=[jax.ShapeDtypeStruct((batch * seq, ATTN_WIDTH), F32)]
                  + [jax.ShapeDtypeStruct(w.shape, BF16) for w in later_weights],
        scratch_shapes=[pltpu.VMEM((seq, hps * HEAD_PAD), BF16)],
        compiler_params=pltpu.CompilerParams(
            dimension_semantics=("arbitrary", "arbitrary", "arbitrary"),
            vmem_limit_bytes=VMEM_LIMIT),
        name="attention",
    )(q, k, v, *later_weights)
    return outs[0], outs[1:]


def _split3(x):
    hi = x.astype(BF16)
    r1 = x - hi.astype(F32)
    mid = r1.astype(BF16)
    lo = (r1 - mid.astype(F32)).astype(BF16)
    return hi, mid, lo


def _ssd_kernel(x_ref, b_ref, c_ref, dt_ref, sh_ref, place_ref, lhs_ones_ref, rhs_const_ref,
                cwx_ref, cwb_ref, cwc_ref, cbx_ref, cbb_ref, cbc_ref, alog_ref, dtbias_ref, dskip_ref,
                o_ref, xc_scr, xm_scr, nsf_scr, nsb_scr, dtv_scr, w_scr, dec_scr, lhs_scr, rhs_scr, rpart_scr):
    nc = dt_ref.shape[0]
    seq = x_ref.shape[0]
    L = SSM_CHUNK
    hpg = HEADS_PER_GROUP
    npair = hpg // 2
    P = SSM_HEAD_DIM
    x_cols = slice(0, GROUP_INNER)
    b_cols = slice(GROUP_INNER, GROUP_INNER + SSM_STATE)
    c_cols = slice(GROUP_INNER + SSM_STATE, GROUP_COLS)
    pair_cols = [slice(j * L, (j + 1) * L) for j in range(npair)]

    @pl.when((pl.program_id(0) == 0) & (pl.program_id(1) == 0))
    def _init_rhs():
        for u in range(SSD_UNROLL):
            rhs_scr[u] = rhs_const_ref[...]

    a_neg = -jnp.exp(alog_ref[...]) * LOG2E
    dtv = jax.nn.softplus(dt_ref[...] + dtbias_ref[...][None])
    dtv_scr[...] = dtv
    da2 = (dtv * a_neg[None]).reshape(nc * DT_ROWS, L)
    ri = lax.broadcasted_iota(jnp.int32, (L, L), 0)
    ci = lax.broadcasted_iota(jnp.int32, (L, L), 1)
    upper = (ri <= ci).astype(BF16)
    lower = (ri >= ci).astype(BF16)
    tri = jnp.concatenate([upper, lower], axis=1)
    cs_fb = sum(_dot(p, tri) for p in _split3(da2))
    rowsel = (ri & hpg) == 0
    cs2 = jnp.where(rowsel, cs_fb[:, 0:L], cs_fb[:, L:2 * L])
    colcs = cs2.T

    col_parts = jnp.concatenate(_split3(colcs), axis=1)
    half_n = place_ref.shape[1] // 2
    for hf in range(2):
        cols = slice(hf * half_n, (hf + 1) * half_n)
        lhs_half = _dot(col_parts, place_ref[:, cols]) + lhs_ones_ref[:, cols]
        for c in range(nc // 2):
            lhs_scr[hf * (nc // 2) + c] = lhs_half[:, c * L:c * L + SEG_K].astype(BF16)
    for i, part in enumerate(_split3(-cs2)):
        rpart_scr[i] = part.astype(F32).reshape(nc, DT_ROWS, L)

    def lane_bcast(col):
        return jnp.broadcast_to(col, (col.shape[0], L))

    tot2 = jnp.where(rowsel, lane_bcast(cs2[:, L - 1:L]), lane_bcast(cs2[:, 0:1]))
    dec_scr[...] = jnp.exp2(tot2).reshape(nc, DT_ROWS, L)
    w_scr[...] = (dtv.reshape(nc * DT_ROWS, L) * jnp.exp2(tot2 - cs2)).reshape(nc, DT_ROWS, L)

    lane_t = lax.broadcasted_iota(jnp.int32, (L, L), 1)
    low_half = lane_t < P

    cw_halves = (cwx_ref[...], jnp.concatenate([cwb_ref[...], cwc_ref[...]], axis=1))
    cb_halves = (cbx_ref[...], jnp.concatenate([cbb_ref[...], cbc_ref[...]], axis=1))

    def rows(hf, start, size):
        if hf == 0:
            return x_ref[pl.ds(start, size), :]
        return jnp.concatenate([b_ref[pl.ds(start, size), :], c_ref[pl.ds(start, size), :]], axis=1)

    def conv_stage(c):
        base = pl.multiple_of(c * L, L)
        ws = pl.multiple_of(jnp.clip(c * L - CONV_WIN_LEAD, 0, seq - CONV_WIN), CONV_WIN_LEAD)
        variant = jnp.where(c == 0, 0, jnp.where(c == nc - 1, 2, 1))
        sh = sh_ref[variant]
        halves = []
        for hf in range(2):
            cw = cw_halves[hf]
            shifted = _dot(sh, rows(hf, ws, CONV_WIN))
            acc = cb_halves[hf] + cw[SSM_CONV // 2:SSM_CONV // 2 + 1, :] * rows(hf, base, L).astype(F32)
            for t, kk in enumerate(CONV_SHIFTED_TAPS):
                acc = acc + cw[kk:kk + 1, :] * shifted[t * L:(t + 1) * L, :]
            halves.append(acc * jax.nn.sigmoid(acc))
        xc = jnp.concatenate(halves, axis=1)
        xc_scr[c] = xc
        for j in range(npair):
            xp = xc[:, pair_cols[j]]
            xm_scr[c, j] = jnp.concatenate([jnp.where(low_half, xp, 0.0), jnp.where(low_half, 0.0, xp)],
                                           axis=0).astype(BF16)

    def state_loads(c):
        return xc_scr[c, :, b_cols], [xm_scr[c, j] for j in range(npair)]

    def state_stage(c, loaded):
        b_tok, xm = loaded
        bt = b_tok.T
        for d, ns_scr in ((0, nsf_scr), (1, nsb_scr)):
            w = w_scr[c, d * hpg:(d + 1) * hpg, :]
            ns_scr[c] = jnp.concatenate(
                [_dot(jnp.concatenate([(bt * w[2 * j:2 * j + 1, :]).astype(BF16),
                                       (bt * w[2 * j + 1:2 * j + 2, :]).astype(BF16)], axis=1), xm[j])
                 for j in range(npair)], axis=1)

    for c in range(CONV_AHEAD):
        conv_stage(c)

    def pass_a(c, carry):
        loaded = state_loads(c)
        conv_stage(c + CONV_AHEAD)
        state_stage(c, loaded)
        return carry
    lax.fori_loop(0, nc - CONV_AHEAD, pass_a, 0, unroll=PASS_A_UNROLL)
    for c in range(nc - CONV_AHEAD, nc):
        state_stage(c, state_loads(c))

    lane_r = lax.broadcasted_iota(jnp.int32, (1, L), 1)

    def decay_row(c, d):
        dec = dec_scr[c, d * hpg:(d + 1) * hpg, :]
        return jnp.concatenate([jnp.where(lane_r < P, dec[2 * j:2 * j + 1, :], dec[2 * j + 1:2 * j + 2, :])
                                for j in range(npair)], axis=1)

    def pass_b(d, ns_scr):
        def body(i, st):
            c = i if d == 0 else nc - 1 - i
            new = ns_scr[c]
            ns_scr[c] = st
            return st * decay_row(c, d) + new
        lax.fori_loop(0, nc, body, jnp.zeros((SSM_STATE, GROUP_INNER), F32))
    pass_b(0, nsf_scr)
    pass_b(1, nsb_scr)

    mask_f = ci <= ri
    mask_b = ci >= ri
    neg_inf = jnp.float32(-jnp.inf)
    sub16 = lax.broadcasted_iota(jnp.int32, (BF16_ROWS, L), 0)

    def pass_c(c, carry):
        base = pl.multiple_of(c * L, L)
        xc = xc_scr[c]
        bm = xc[:, b_cols].astype(BF16)
        cm = xc[:, c_cols].astype(BF16)
        cb = _dot_nt(cm, bm)
        off_f = _dot(cm, nsf_scr[c].astype(BF16))
        off_b = _dot(cm, nsb_scr[c].astype(BF16))
        slot = c % SSD_UNROLL
        for hd in range(DT_ROWS):
            r0 = (hd % 2) * SEG_K_PER_ROW
            blk = jnp.where((sub16 >= r0) & (sub16 < r0 + 3), 1.0, 0.0)
            for i in range(3):
                blk = jnp.where(sub16 == r0 + 3 + i, rpart_scr[i, c, hd:hd + 1, :], blk)
            rhs_scr[slot, (hd // 2) * BF16_ROWS:(hd // 2 + 1) * BF16_ROWS, hd * L:(hd + 1) * L] = blk.astype(BF16)
        seg = _dot(lhs_scr[c], rhs_scr[slot])
        dt_f_r = dtv_scr[c, 0:hpg, :]
        dt_b_r = dtv_scr[c, hpg:DT_ROWS, :]
        ys = []
        for j in range(npair):
            ws_ = []
            for h in (2 * j, 2 * j + 1):
                ef = jnp.exp2(jnp.where(mask_f, seg[:, h * L:(h + 1) * L], neg_inf))
                eb = jnp.exp2(jnp.where(mask_b, seg[:, (hpg + h) * L:(hpg + h + 1) * L], neg_inf))
                ws_.append((cb * (ef * dt_f_r[h:h + 1, :] + eb * dt_b_r[h:h + 1, :])).astype(BF16))
            scale_f = jnp.exp2(seg[:, (DT_ROWS + j) * L:(DT_ROWS + j + 1) * L])
            scale_b = jnp.exp2(seg[:, (DT_ROWS + npair + j) * L:(DT_ROWS + npair + j + 1) * L])
            ys.append(_dot(jnp.concatenate(ws_, axis=1), xm_scr[c, j])
                      + scale_f * off_f[:, pair_cols[j]] + scale_b * off_b[:, pair_cols[j]])
        y = jnp.concatenate(ys, axis=1) + dskip_ref[...] * xc[:, x_cols]
        o_ref[pl.ds(base, L), :] = y
        return carry
    lax.fori_loop(0, nc, pass_c, 0, unroll=SSD_UNROLL)


def _conv_shift_matrices():
    out = np.zeros((3, len(CONV_SHIFTED_TAPS) * SSM_CHUNK, CONV_WIN), np.float32)
    for v, lead in enumerate((0, CONV_WIN_LEAD, CONV_WIN - SSM_CHUNK)):
        for i, kk in enumerate(CONV_SHIFTED_TAPS):
            for t in range(SSM_CHUNK):
                j = t + kk - SSM_CONV // 2 + lead
                if 0 <= j < CONV_WIN:
                    out[v, i * SSM_CHUNK + t, j] = 1.0
    return out


def _seg_matmul_constants(nc):
    L, K, half = SSM_CHUNK, SEG_K_PER_ROW, SSM_HEAD_DIM
    place = np.zeros((3 * L, nc * L), np.float32)
    lhs_ones = np.zeros((1, nc * L), np.float32)
    for c in range(nc):
        for hd in range(DT_ROWS):
            for i in range(3):
                place[i * L + c * DT_ROWS + hd, c * L + hd * K + i] = 1.0
                lhs_ones[0, c * L + hd * K + 3 + i] = 1.0
    rhs = np.zeros((SEG_K, SEG_BLOCKS * L), np.float32)
    for hd in range(DT_ROWS):
        d, h = divmod(hd, HEADS_PER_GROUP)
        pair_block = DT_ROWS + d * (HEADS_PER_GROUP // 2) + h // 2
        lanes = slice(0, half) if h % 2 == 0 else slice(half, L)
        rhs[hd * K:hd * K + 3, hd * L:(hd + 1) * L] = 1.0
        rhs[hd * K:hd * K + 3, pair_block * L:(pair_block + 1) * L][:, lanes] = 1.0
    return place, lhs_ones, rhs


def _ssd_call(xbc, dtc, cw, cbias, alog, dtbias, dskip, batch, seq):
    nc = seq // SSM_CHUNK
    L = SSM_CHUNK
    assert nc >= 3 and seq >= CONV_WIN and nc * DT_ROWS == L
    sh = jnp.asarray(_conv_shift_matrices(), BF16)
    place, lhs_ones, rhs_const = _seg_matmul_constants(nc)
    place = jnp.asarray(place, BF16)
    lhs_ones = jnp.asarray(lhs_ones, F32)
    rhs_const = jnp.asarray(rhs_const, BF16)
    const = lambda a: pl.BlockSpec(a.shape, lambda b, g: (0,) * a.ndim)
    b_blk0 = SSM_INNER // SSM_STATE
    c_blk0 = b_blk0 + SSM_GROUPS

    def xbc_views(rows_, batched):
        lead = (lambda b: b) if batched else (lambda b: 0)
        return [pl.BlockSpec((rows_, GROUP_INNER), lambda b, g: (lead(b), g)),
                pl.BlockSpec((rows_, SSM_STATE), lambda b, g: (lead(b), b_blk0 + g)),
                pl.BlockSpec((rows_, SSM_STATE), lambda b, g: (lead(b), c_blk0 + g))]
    return pl.pallas_call(
        _ssd_kernel,
        grid=(batch, SSM_GROUPS),
        in_specs=xbc_views(seq, True) + [
                  pl.BlockSpec((nc, DT_ROWS, L), lambda b, g: (b, g, 0)),
                  const(sh), const(place), const(lhs_ones), const(rhs_const)]
                 + xbc_views(SUBLANES, False) + xbc_views(1, False) + [
                  pl.BlockSpec((DT_ROWS, L), lambda b, g: (g, 0)),
                  pl.BlockSpec((DT_ROWS, L), lambda b, g: (g, 0)),
                  pl.BlockSpec((1, GROUP_INNER), lambda b, g: (0, g))],
        out_specs=pl.BlockSpec((seq, GROUP_INNER), lambda b, g: (b, g)),
        out_shape=jax.ShapeDtypeStruct((batch * seq, SSM_INNER), F32),
        scratch_shapes=[pltpu.VMEM((nc, L, GROUP_COLS), F32),
                        pltpu.VMEM((nc, HEADS_PER_GROUP // 2, 2 * L, L), BF16),
                        pltpu.VMEM((nc, SSM_STATE, GROUP_INNER), F32),
                        pltpu.VMEM((nc, SSM_STATE, GROUP_INNER), F32),
                        pltpu.VMEM((nc, DT_ROWS, L), F32),
                        pltpu.VMEM((nc, DT_ROWS, L), F32),
                        pltpu.VMEM((nc, DT_ROWS, L), F32),
                        pltpu.VMEM((nc, L, SEG_K), BF16),
                        pltpu.VMEM((SSD_UNROLL, SEG_K, SEG_BLOCKS * L), BF16),
                        pltpu.VMEM((3, nc, DT_ROWS, L), F32)],
        compiler_params=pltpu.CompilerParams(dimension_semantics=("arbitrary", "arbitrary"),
                                             vmem_limit_bytes=VMEM_LIMIT),
        name="ssd",
    )(xbc, xbc, xbc, dtc, sh, place, lhs_ones, rhs_const, cw, cw, cw, cbias, cbias, cbias,
      alog, dtbias, dskip)


def _mlp_kernel(x_ref, attn_ref, ssm_ref, z_ref, gat_ref, gn_ref, wo_ref, gm_ref, wup_ref, wdn_ref, o_ref):
    for sub in range(x_ref.shape[0] // MLP_SUBTILE):
        rs = slice(sub * MLP_SUBTILE, (sub + 1) * MLP_SUBTILE)
        a = attn_ref[rs, :]
        an = (a * _inv_rms(a) * gat_ref[...]).astype(BF16)
        zc = z_ref[rs, :].astype(F32)
        y = ssm_ref[rs, :] * (zc * jax.nn.sigmoid(zc))
        yn = jnp.concatenate([y[:, g * GROUP_INNER:(g + 1) * GROUP_INNER]
                              * _inv_rms(y[:, g * GROUP_INNER:(g + 1) * GROUP_INNER]) for g in range(SSM_GROUPS)],
                             axis=1) * gn_ref[...]
        mix = _dot(an, wo_ref[0:ATTN_WIDTH, :]) + _dot(yn.astype(BF16), wo_ref[ATTN_WIDTH:D_MIX, :])
        x1 = x_ref[rs, :] + mix
        hm = (x1 * _inv_rms(x1) * gm_ref[...]).astype(BF16)
        acc = jnp.zeros_like(x1)
        for c in range(D_FF // FF_CHUNK):
            cols = slice(c * FF_CHUNK, (c + 1) * FF_CHUNK)
            u = _dot(hm, wup_ref[:, cols])
            acc = acc + _dot(jnp.square(jnp.maximum(u, 0.0)).astype(BF16), wdn_ref[cols, :])
        o_ref[rs, :] = x1 + acc


def _mlp_call(x2, attn, ssm, z, gat, gn, wo, gm, wup, wdn):
    t = x2.shape[0]
    row = lambda w: pl.BlockSpec((TM_MLP, w), lambda i: (i, 0))
    full = lambda a: pl.BlockSpec(a.shape, lambda i: (0, 0), pipeline_mode=pl.Buffered(1))
    return pl.pallas_call(
        _mlp_kernel,
        grid=(t // TM_MLP,),
        in_specs=[row(D_MODEL), row(ATTN_WIDTH), row(SSM_INNER), row(SSM_INNER), full(gat), full(gn), full(wo),
                  full(gm), full(wup), full(wdn)],
        out_specs=row(D_MODEL),
        out_shape=jax.ShapeDtypeStruct((t, D_MODEL), F32),
        compiler_params=pltpu.CompilerParams(dimension_semantics=("arbitrary",),
                                             vmem_limit_bytes=VMEM_LIMIT),
        name="outproj_mlp",
    )(x2, attn, ssm, z, gat, gn, wo, gm, wup, wdn)


def _group_major(fwd, bwd):
    lead = fwd.shape[:-1]
    both = jnp.stack([fwd.reshape(lead + (SSM_GROUPS, HEADS_PER_GROUP)),
                      bwd.reshape(lead + (SSM_GROUPS, HEADS_PER_GROUP))], axis=-2)
    return both.reshape(lead + (2 * SSM_HEADS,))


def _rotate_half(w):
    half = QK_ROPE_DIM // 2
    return jnp.concatenate([-w[..., half:], w[..., :half]], axis=-1)


def _swap_halves(g):
    half = QK_ROPE_DIM // 2
    return jnp.concatenate([g[..., half:], g[..., :half]], axis=-1)


def _lane_bcast(v):
    return jnp.broadcast_to(v[..., None], v.shape + (LANES,)).astype(F32)


def _layer(x2, cs, batch, seq, ln_mix_g, w_in, q_a_norm_g, w_uq, kv_a_norm_g, w_ukv, q_norm_g,
           k_norm_g, attn_out_norm_g, conv_w, conv_b, a_log_fwd, a_log_bwd, dt_bias_fwd, dt_bias_bwd,
           d_skip, ssm_norm_g, w_out, ln_mlp_g, w_mlp_up, w_mlp_down):
    win = _win_prep_call(*w_in)
    wq = w_uq.reshape(Q_LORA_RANK, ATTN_HEADS, QK_HEAD_DIM)
    wuq = jnp.concatenate([wq, _rotate_half(wq[..., QK_NOPE_DIM:])], axis=-1)
    wuq = wuq.reshape(Q_LORA_RANK, ATTN_HEADS * HEAD_PAD).astype(BF16)
    wukv = w_ukv.reshape(KV_LORA_RANK, ATTN_HEADS, QK_NOPE_DIM + V_HEAD_DIM)
    wuk = jnp.pad(wukv[:, :, :QK_NOPE_DIM],
                  ((0, 0), (0, 0), (0, HEAD_PAD - QK_NOPE_DIM))).reshape(KV_LORA_RANK, -1).astype(BF16)
    wuv = wukv[:, :, QK_NOPE_DIM:].reshape(KV_LORA_RANK, -1).astype(BF16)
    scale = QK_HEAD_DIM ** -0.5 * np.log2(np.e)
    gq = (jnp.concatenate([q_norm_g, _swap_halves(q_norm_g[QK_NOPE_DIM:])]) * scale)[None, :]
    zeros_nope = jnp.zeros((QK_NOPE_DIM,), k_norm_g.dtype)
    gkn = jnp.concatenate([k_norm_g[:QK_NOPE_DIM], zeros_nope])[None, :]
    gkr = jnp.concatenate([zeros_nope, k_norm_g[QK_NOPE_DIM:], _swap_halves(k_norm_g[QK_NOPE_DIM:])])[None, :]

    q, k, v, z, xbc, dtc = _inproj_call(
        x2, cs, ln_mix_g[None, :], win, q_a_norm_g[None, :], wuq, kv_a_norm_g[None, :], wuk, wuv,
        gq, gkn, gkr)

    attn, (wo_b, wup_b, wdn_b) = _attn_call(q, k, v, (w_out, w_mlp_up, w_mlp_down), batch, seq)

    cw = jnp.pad(conv_w[:, 0, :], ((0, SUBLANES - SSM_CONV), (0, 0)))
    cbias = conv_b[None, :]
    alog = _lane_bcast(_group_major(a_log_fwd, a_log_bwd))
    dtbias = _lane_bcast(_group_major(dt_bias_fwd, dt_bias_bwd))
    dskip = jnp.repeat(d_skip, SSM_HEAD_DIM)[None, :]
    ssm = _ssd_call(xbc, dtc, cw, cbias, alog, dtbias, dskip, batch, seq)

    return _mlp_call(x2, attn, ssm, z, attn_out_norm_g[None, :], ssm_norm_g[None, :], wo_b,
                     ln_mlp_g[None, :], wup_b, wdn_b)


def _rope_table(positions):
    inv_freq = 1.0 / (ROPE_THETA ** (jnp.arange(0, QK_ROPE_DIM, 2, dtype=F32) / QK_ROPE_DIM))
    ang = inv_freq[:, None] * positions.astype(F32).reshape(1, -1)
    return jnp.concatenate([jnp.cos(ang), jnp.sin(ang)], axis=0)


def kernel(x, positions, ln_mix_g, w_in, q_a_norm_g, w_uq, kv_a_norm_g, w_ukv, q_norm_g, k_norm_g,
           attn_out_norm_g, conv_w, conv_b, a_log_fwd, a_log_bwd, dt_bias_fwd, dt_bias_bwd, d_skip,
           ssm_norm_g, w_out, ln_mlp_g, w_mlp_up, w_mlp_down):
    batch, seq, d = x.shape
    assert d == D_MODEL and seq % TQ == 0 and (batch * seq) % TM_IN == 0 and (batch * seq) % TM_MLP == 0
    cs = _rope_table(positions)
    x2 = x.reshape(batch * seq, d)
    for l in range(ln_mix_g.shape[0]):
        x2 = _layer(x2, cs, batch, seq, ln_mix_g[l], (w_in, l), q_a_norm_g[l], w_uq[l], kv_a_norm_g[l],
                    w_ukv[l], q_norm_g[l], k_norm_g[l], attn_out_norm_g[l], conv_w[l], conv_b[l],
                    a_log_fwd[l], a_log_bwd[l], dt_bias_fwd[l], dt_bias_bwd[l], d_skip[l], ssm_norm_g[l],
                    w_out[l], ln_mlp_g[l], w_mlp_up[l], w_mlp_down[l])
    return x2.reshape(batch, seq, d)
```

```python
import numpy as np
import jax
import jax.numpy as jnp
from jax import lax
from jax.experimental import pallas as pl
from jax.experimental.pallas import tpu as pltpu

F32 = jnp.float32
BF16 = jnp.bfloat16

D_MODEL = 1024
ATTN_HEADS = 8
QK_NOPE_DIM = 64
QK_ROPE_DIM = 32
QK_HEAD_DIM = QK_NOPE_DIM + QK_ROPE_DIM
V_HEAD_DIM = 64
Q_LORA_RANK = D_MODEL // 4
KV_LORA_RANK = D_MODEL // 8
ROPE_THETA = 10000.0
ATTN_WIDTH = ATTN_HEADS * V_HEAD_DIM
SSM_HEADS = 8
SSM_HEAD_DIM = 64
SSM_INNER = SSM_HEADS * SSM_HEAD_DIM
SSM_GROUPS = 2
SSM_STATE = 128
SSM_CONV = 5
SSM_CHUNK = 128
SSM_CONV_CH = SSM_INNER + 2 * SSM_GROUPS * SSM_STATE
D_MIX = ATTN_WIDTH + SSM_INNER
D_FF = 4 * D_MODEL
EPS = 1e-6

LANES = 128
SUBLANES = 8
HEAD_PAD = LANES

HEADS_PER_GROUP = SSM_HEADS // SSM_GROUPS
GROUP_INNER = SSM_INNER // SSM_GROUPS
GROUP_COLS = GROUP_INNER + 2 * SSM_STATE
DT_ROWS = 2 * HEADS_PER_GROUP
BF16_ROWS = 16
SEG_K_PER_ROW = 8
SEG_K = DT_ROWS * SEG_K_PER_ROW
SEG_BLOCKS = DT_ROWS + HEADS_PER_GROUP
LOG2E = float(np.log2(np.e))
CONV_WIN = 2 * SSM_CHUNK
CONV_WIN_LEAD = SSM_CHUNK // 2
CONV_SHIFTED_TAPS = tuple(k for k in range(SSM_CONV) if k != SSM_CONV // 2)

COL_CKV = Q_LORA_RANK
COL_MISC = COL_CKV + KV_LORA_RANK
COL_Z = COL_MISC + LANES
COL_XBC = COL_Z + SSM_INNER
IN_COLS = COL_XBC + SSM_CONV_CH

TM_IN = 1024
IN_SUBTILE = 512
TQ = 1024
ATTN_HEADS_PER_STEP = 8
TM_MLP = 1024
MLP_SUBTILE = 512
FF_CHUNK = 1024
SSD_UNROLL = 8
CONV_AHEAD = 4
PASS_A_UNROLL = 4
WIN_PREP_STEPS = 4
VMEM_LIMIT = 56 * 1024 * 1024


def _inv_rms(x):
    n = x.shape[-1]
    x2 = x * x
    acc = x2[:, 0:LANES]
    for i in range(1, n // LANES):
        acc = acc + x2[:, i * LANES:(i + 1) * LANES]
    return lax.rsqrt(jnp.sum(acc, axis=-1, keepdims=True) * (1.0 / n) + EPS)


def _dot(a, b):
    return jnp.dot(a, b, preferred_element_type=F32)


def _dot_nt(a, b):
    return lax.dot_general(a, b, (((1,), (1,)), ((), ())), preferred_element_type=F32)


def _inproj_kernel(x_ref, cs_ref, g_ref, win_ref, gqa_ref, wuq_ref, gkva_ref, wuk_ref, wuv_ref, vone_ref,
                   gq_ref, gkn_ref, gkr_ref,
                   q_ref, k_ref, v_ref, z_ref, xbc_ref, dt_ref):
    for sub in range(x_ref.shape[0] // IN_SUBTILE):
        _inproj_rows(sub * IN_SUBTILE, x_ref, cs_ref, g_ref, win_ref, gqa_ref, wuq_ref, gkva_ref, wuk_ref,
                     wuv_ref, vone_ref, gq_ref, gkn_ref, gkr_ref, q_ref, k_ref, v_ref, z_ref, xbc_ref, dt_ref)


def _inproj_rows(r0, x_ref, cs_ref, g_ref, win_ref, gqa_ref, wuq_ref, gkva_ref, wuk_ref, wuv_ref, vone_ref,
                 gq_ref, gkn_ref, gkr_ref, q_ref, k_ref, v_ref, z_ref, xbc_ref, dt_ref):
    tm = IN_SUBTILE
    rs = slice(r0, r0 + tm)
    x = x_ref[rs, :]
    h = (x * _inv_rms(x) * g_ref[...]).astype(BF16)
    big = _dot(h, win_ref[...])
    z_ref[rs, :] = big[:, COL_Z:COL_XBC].astype(BF16)
    xbc_ref[rs, :] = big[:, COL_XBC:IN_COLS].astype(BF16)
    misc = big[:, COL_MISC:COL_Z]
    for c in range(tm // SSM_CHUNK):
        dt_ref[r0 // SSM_CHUNK + c] = misc[c * SSM_CHUNK:(c + 1) * SSM_CHUNK, :].T[0:2 * SSM_HEADS, :]

    cq = big[:, 0:COL_CKV]
    ckv = big[:, COL_CKV:COL_MISC]
    cqn = (cq * _inv_rms(cq) * gqa_ref[...]).astype(BF16)
    ckvn = (ckv * _inv_rms(ckv) * gkva_ref[...]).astype(BF16)
    q_pre = _dot(cqn, wuq_ref[...])
    k_pre = _dot(ckvn, wuk_ref[...])
    v_ref[rs, :] = (_dot(ckvn, wuv_ref[...]) + vone_ref[...]).astype(BF16)

    half = QK_ROPE_DIM // 2
    zpad = jnp.zeros((HEAD_PAD - QK_ROPE_DIM, SSM_CHUNK), F32)
    tt = jnp.concatenate(
        [jnp.concatenate([cs_ref[:, r0 + c * SSM_CHUNK:r0 + (c + 1) * SSM_CHUNK], zpad], axis=0).T
         for c in range(tm // SSM_CHUNK)], axis=0)
    lane_t = lax.broadcasted_iota(jnp.int32, (tm, HEAD_PAD), 1)
    cs = jnp.where(lane_t < QK_NOPE_DIM, 1.0,
                   jnp.where(lane_t < QK_NOPE_DIM + half, pltpu.roll(tt, QK_NOPE_DIM, 1),
                             jnp.where(lane_t < QK_HEAD_DIM + half, pltpu.roll(tt, QK_NOPE_DIM + half, 1),
                                       pltpu.roll(tt, QK_HEAD_DIM, 1))))
    lane = lax.broadcasted_iota(jnp.int32, (1, HEAD_PAD), 1)
    in_head = (lane < QK_HEAD_DIM).astype(F32)
    is_rope = ((lane >= QK_NOPE_DIM) & (lane < QK_HEAD_DIM)).astype(F32)
    inv_d = 1.0 / QK_HEAD_DIM

    gcq = gq_ref[...] * cs
    for hh in range(ATTN_HEADS):
        sl = slice(hh * HEAD_PAD, (hh + 1) * HEAD_PAD)
        qh = q_pre[:, sl]
        ssq = jnp.sum(qh * qh * in_head, axis=-1, keepdims=True)
        q_ref[rs, sl] = (qh * lax.rsqrt(ssq * inv_d + EPS) * gcq).astype(BF16)

    ab = misc * (gkr_ref[...] * cs)
    lane2 = lax.broadcasted_iota(jnp.int32, (tm, HEAD_PAD), 1)
    swapped = jnp.where(lane2 < QK_HEAD_DIM, pltpu.roll(ab, HEAD_PAD - QK_ROPE_DIM, 1),
                        pltpu.roll(ab, QK_ROPE_DIM, 1))
    s_both = jnp.where(lane2 >= QK_NOPE_DIM, ab + swapped, 0.0)
    ssq_pe = jnp.sum(misc * misc * is_rope, axis=-1, keepdims=True)
    gkn = gkn_ref[...]
    for hh in range(ATTN_HEADS):
        sl = slice(hh * HEAD_PAD, (hh + 1) * HEAD_PAD)
        kh = k_pre[:, sl]
        ssq = jnp.sum(kh * kh, axis=-1, keepdims=True) + ssq_pe
        k_ref[rs, sl] = ((kh * gkn + s_both) * lax.rsqrt(ssq * inv_d + EPS)).astype(BF16)


def _win_prep_kernel(w_ref, o_ref):
    w = w_ref[...].astype(F32)
    o_kpe = Q_LORA_RANK + KV_LORA_RANK
    o_z = o_kpe + QK_ROPE_DIM
    o_dt = o_z + SSM_INNER + SSM_CONV_CH
    half = QK_ROPE_DIM // 2
    hpg = HEADS_PER_GROUP
    dt_f = w[:, o_dt:o_dt + SSM_HEADS]
    dt_b = w[:, o_dt + SSM_HEADS:o_dt + 2 * SSM_HEADS]
    kpe = w[:, o_kpe:o_z]
    pieces = [w[:, 0:o_kpe]]
    for g in range(SSM_GROUPS):
        pieces += [dt_f[:, g * hpg:(g + 1) * hpg], dt_b[:, g * hpg:(g + 1) * hpg]]
    pieces += [jnp.zeros((w.shape[0], QK_NOPE_DIM - 2 * SSM_HEADS), F32), kpe, -kpe[:, half:], kpe[:, :half],
               w[:, o_z:o_dt]]
    o_ref[...] = jnp.concatenate(pieces, axis=1).astype(BF16)


def _win_prep_call(w_in3, layer):
    depth, d, n = w_in3.shape
    rows = d // WIN_PREP_STEPS
    return pl.pallas_call(
        _win_prep_kernel,
        grid=(WIN_PREP_STEPS,),
        in_specs=[pl.BlockSpec((rows, n), lambda i: (layer * WIN_PREP_STEPS + i, 0))],
        out_specs=pl.BlockSpec((rows, IN_COLS), lambda i: (i, 0)),
        out_shape=jax.ShapeDtypeStruct((d, IN_COLS), BF16),
        compiler_params=pltpu.CompilerParams(dimension_semantics=("arbitrary",), vmem_limit_bytes=VMEM_LIMIT),
        name="win_prep",
    )(w_in3.reshape(depth * d, n).astype(BF16))


def _inproj_call(x2, cs, g, win, gqa, wuq, gkva, wuk, wuv, vone, gq, gkn, gkr):
    t = x2.shape[0]
    cpt = TM_IN // SSM_CHUNK
    full = lambda a: pl.BlockSpec(a.shape, lambda i: (0,) * a.ndim, pipeline_mode=pl.Buffered(1))
    row = lambda w: pl.BlockSpec((TM_IN, w), lambda i: (i, 0))
    return pl.pallas_call(
        _inproj_kernel,
        grid=(t // TM_IN,),
        in_specs=[row(D_MODEL), pl.BlockSpec((QK_ROPE_DIM, TM_IN), lambda i: (0, i)), full(g), full(win), full(gqa), full(wuq), full(gkva),
                  full(wuk), full(wuv), full(vone), full(gq), full(gkn), full(gkr)],
        out_specs=[row(ATTN_HEADS * HEAD_PAD), row(ATTN_HEADS * HEAD_PAD), row(ATTN_HEADS * HEAD_PAD),
                   row(SSM_INNER), row(SSM_CONV_CH),
                   pl.BlockSpec((cpt, 2 * SSM_HEADS, SSM_CHUNK), lambda i: (i, 0, 0))],
        out_shape=[jax.ShapeDtypeStruct((t, ATTN_HEADS * HEAD_PAD), BF16),
                   jax.ShapeDtypeStruct((t, ATTN_HEADS * HEAD_PAD), BF16),
                   jax.ShapeDtypeStruct((t, ATTN_HEADS * HEAD_PAD), BF16),
                   jax.ShapeDtypeStruct((t, SSM_INNER), BF16),
                   jax.ShapeDtypeStruct((t, SSM_CONV_CH), BF16),
                   jax.ShapeDtypeStruct((t // SSM_CHUNK, 2 * SSM_HEADS, SSM_CHUNK), F32)],
        compiler_params=pltpu.CompilerParams(dimension_semantics=("arbitrary",),
                                             vmem_limit_bytes=VMEM_LIMIT),
        name="inproj",
    )(x2, cs, g, win, gqa, wuq, gkva, wuk, wuv, vone, gq, gkn, gkr)


def _attn_kernel(q_ref, k_ref, v_ref, *refs):
    n_w = (len(refs) - 1) // 2
    w_in_refs, o_ref, w_out_refs = refs[:n_w], refs[n_w], refs[n_w + 1:]
    for w_in_ref, w_out_ref in zip(w_in_refs, w_out_refs):
        w_out_ref[...] = w_in_ref[...].astype(BF16)

    lane = lax.broadcasted_iota(jnp.int32, (q_ref.shape[0], HEAD_PAD), 1)
    for jp in range(ATTN_HEADS_PER_STEP // 2):
        sls = [slice(j * HEAD_PAD, (j + 1) * HEAD_PAD) for j in (2 * jp, 2 * jp + 1)]
        ss = [_dot_nt(q_ref[:, sl], k_ref[:, sl]) for sl in sls]
        ps = [jnp.exp2(s - jnp.max(s, axis=-1, keepdims=True)).astype(BF16) for s in ss]
        accs = [_dot(p, v_ref[:, sl]) for p, sl in zip(ps, sls)]
        res = [acc / pltpu.roll(acc, V_HEAD_DIM, 1) for acc in accs]
        o_ref[:, jp * HEAD_PAD:(jp + 1) * HEAD_PAD] = jnp.where(lane < V_HEAD_DIM, res[0], res[1])


def _attn_call(q, k, v, later_weights, batch, seq):
    n_q = seq // TQ
    hps = ATTN_HEADS_PER_STEP
    assert ATTN_HEADS == hps
    n_steps = batch * n_q

    def rows_spec(w):
        assert w.shape[0] % (n_steps * BF16_ROWS) == 0
        return pl.BlockSpec((w.shape[0] // n_steps, w.shape[1]), lambda b, hp, i: (b * n_q + i, 0))
    w_specs = [rows_spec(w) for w in later_weights]
    outs = pl.pallas_call(
        _attn_kernel,
        grid=(batch, ATTN_HEADS // hps, n_q),
        in_specs=[pl.BlockSpec((TQ, hps * HEAD_PAD), lambda b, hp, i: (b * n_q + i, hp)),
                  pl.BlockSpec((seq, hps * HEAD_PAD), lambda b, hp, i: (b, hp)),
                  pl.BlockSpec((seq, hps * HEAD_PAD), lambda b, hp, i: (b, hp))] + w_specs,
        out_specs=[pl.BlockSpec((TQ, hps * V_HEAD_DIM), lambda b, hp, i: (b * n_q + i, hp))] + w_specs,
        out_shape=[jax.ShapeDtypeStruct((batch * seq, ATTN_WIDTH), F32)]
                  + [jax.ShapeDtypeStruct(w.shape, BF16) for w in later_weights],
        compiler_params=pltpu.CompilerParams(
            dimension_semantics=("arbitrary", "arbitrary", "arbitrary"),
            vmem_limit_bytes=VMEM_LIMIT),
        name="attention",
    )(q, k, v, *later_weights)
    return outs[0], outs[1:]


def _split3(x):
    hi = x.astype(BF16)
    r1 = x - hi.astype(F32)
    mid = r1.astype(BF16)
    lo = (r1 - mid.astype(F32)).astype(BF16)
    return hi, mid, lo


def _ssd_kernel(x_ref, b_ref, c_ref, dt_ref, sh_ref, place_ref, lhs_ones_ref, rhs_const_ref,
                cwx_ref, cwb_ref, cwc_ref, cbx_ref, cbb_ref, cbc_ref, alog_ref, dtbias_ref, dskip_ref,
                o_ref, xc_scr, xm_scr, nsf_scr, nsb_scr, dtv_scr, w_scr, dec_scr, lhs_scr, rhs_scr, rpart_scr):
    nc = dt_ref.shape[0]
    seq = x_ref.shape[0]
    L = SSM_CHUNK
    hpg = HEADS_PER_GROUP
    npair = hpg // 2
    P = SSM_HEAD_DIM
    x_cols = slice(0, GROUP_INNER)
    b_cols = slice(GROUP_INNER, GROUP_INNER + SSM_STATE)
    c_cols = slice(GROUP_INNER + SSM_STATE, GROUP_COLS)
    pair_cols = [slice(j * L, (j + 1) * L) for j in range(npair)]

    @pl.when((pl.program_id(0) == 0) & (pl.program_id(1) == 0))
    def _init_rhs():
        for u in range(SSD_UNROLL):
            rhs_scr[u] = rhs_const_ref[...]

    a_neg = -jnp.exp(alog_ref[...]) * LOG2E
    dtv = jax.nn.softplus(dt_ref[...] + dtbias_ref[...][None])
    dtv_scr[...] = dtv
    da2 = (dtv * a_neg[None]).reshape(nc * DT_ROWS, L)
    ri = lax.broadcasted_iota(jnp.int32, (L, L), 0)
    ci = lax.broadcasted_iota(jnp.int32, (L, L), 1)
    upper = (ri <= ci).astype(BF16)
    lower = (ri >= ci).astype(BF16)
    tri = jnp.concatenate([upper, lower], axis=1)
    cs_fb = sum(_dot(p, tri) for p in _split3(da2))
    rowsel = (ri & hpg) == 0
    cs2 = jnp.where(rowsel, cs_fb[:, 0:L], cs_fb[:, L:2 * L])
    colcs = cs2.T

    col_parts = jnp.concatenate(_split3(colcs), axis=1)
    half_n = place_ref.shape[1] // 2
    for hf in range(2):
        cols = slice(hf * half_n, (hf + 1) * half_n)
        lhs_half = _dot(col_parts, place_ref[:, cols]) + lhs_ones_ref[:, cols]
        for c in range(nc // 2):
            lhs_scr[hf * (nc // 2) + c] = lhs_half[:, c * L:c * L + SEG_K].astype(BF16)
    for i, part in enumerate(_split3(-cs2)):
        rpart_scr[i] = part.astype(F32).reshape(nc, DT_ROWS, L)

    def lane_bcast(col):
        return jnp.broadcast_to(col, (col.shape[0], L))

    tot2 = jnp.where(rowsel, lane_bcast(cs2[:, L - 1:L]), lane_bcast(cs2[:, 0:1]))
    dec_scr[...] = jnp.exp2(tot2).reshape(nc, DT_ROWS, L)
    w_scr[...] = (dtv.reshape(nc * DT_ROWS, L) * jnp.exp2(tot2 - cs2)).reshape(nc, DT_ROWS, L)

    lane_t = lax.broadcasted_iota(jnp.int32, (L, L), 1)
    low_half = lane_t < P

    cw_halves = (cwx_ref[...], jnp.concatenate([cwb_ref[...], cwc_ref[...]], axis=1))
    cb_halves = (cbx_ref[...], jnp.concatenate([cbb_ref[...], cbc_ref[...]], axis=1))

    def rows(hf, start, size):
        if hf == 0:
            return x_ref[pl.ds(start, size), :]
        return jnp.concatenate([b_ref[pl.ds(start, size), :], c_ref[pl.ds(start, size), :]], axis=1)

    def conv_stage(c):
        base = pl.multiple_of(c * L, L)
        ws = pl.multiple_of(jnp.clip(c * L - CONV_WIN_LEAD, 0, seq - CONV_WIN), CONV_WIN_LEAD)
        variant = jnp.where(c == 0, 0, jnp.where(c == nc - 1, 2, 1))
        sh = sh_ref[variant]
        halves = []
        for hf in range(2):
            cw = cw_halves[hf]
            shifted = _dot(sh, rows(hf, ws, CONV_WIN))
            acc = cb_halves[hf] + cw[SSM_CONV // 2:SSM_CONV // 2 + 1, :] * rows(hf, base, L).astype(F32)
            for t, kk in enumerate(CONV_SHIFTED_TAPS):
                acc = acc + cw[kk:kk + 1, :] * shifted[t * L:(t + 1) * L, :]
            halves.append(acc * jax.nn.sigmoid(acc))
        xc = jnp.concatenate(halves, axis=1)
        xc_scr[c] = xc
        for j in range(npair):
            xp = xc[:, pair_cols[j]]
            xm_scr[c, j] = jnp.concatenate([jnp.where(low_half, xp, 0.0), jnp.where(low_half, 0.0, xp)],
                                           axis=0).astype(BF16)

    def state_loads(c):
        return xc_scr[c, :, b_cols], [xm_scr[c, j] for j in range(npair)]

    def state_stage(c, loaded):
        b_tok, xm = loaded
        bt = b_tok.T
        for d, ns_scr in ((0, nsf_scr), (1, nsb_scr)):
            w = w_scr[c, d * hpg:(d + 1) * hpg, :]
            ns_scr[c] = jnp.concatenate(
                [_dot(jnp.concatenate([(bt * w[2 * j:2 * j + 1, :]).astype(BF16),
                                       (bt * w[2 * j + 1:2 * j + 2, :]).astype(BF16)], axis=1), xm[j])
                 for j in range(npair)], axis=1)

    for c in range(CONV_AHEAD):
        conv_stage(c)

    def pass_a(c, carry):
        loaded = state_loads(c)
        conv_stage(c + CONV_AHEAD)
        state_stage(c, loaded)
        return carry
    lax.fori_loop(0, nc - CONV_AHEAD, pass_a, 0, unroll=PASS_A_UNROLL)
    for c in range(nc - CONV_AHEAD, nc):
        state_stage(c, state_loads(c))

    lane_r = lax.broadcasted_iota(jnp.int32, (1, L), 1)

    def decay_row(c, d):
        dec = dec_scr[c, d * hpg:(d + 1) * hpg, :]
        return jnp.concatenate([jnp.where(lane_r < P, dec[2 * j:2 * j + 1, :], dec[2 * j + 1:2 * j + 2, :])
                                for j in range(npair)], axis=1)

    def pass_b(d, ns_scr):
        def body(i, st):
            c = i if d == 0 else nc - 1 - i
            new = ns_scr[c]
            ns_scr[c] = st
            return st * decay_row(c, d) + new
        lax.fori_loop(0, nc, body, jnp.zeros((SSM_STATE, GROUP_INNER), F32))
    pass_b(0, nsf_scr)
    pass_b(1, nsb_scr)

    mask_f = ci <= ri
    mask_b = ci >= ri
    neg_inf = jnp.float32(-jnp.inf)
    sub16 = lax.broadcasted_iota(jnp.int32, (BF16_ROWS, L), 0)

    def pass_c(c, carry):
        base = pl.multiple_of(c * L, L)
        xc = xc_scr[c]
        bm = xc[:, b_cols].astype(BF16)
        cm = xc[:, c_cols].astype(BF16)
        cb = _dot_nt(cm, bm)
        off_f = _dot(cm, nsf_scr[c].astype(BF16))
        off_b = _dot(cm, nsb_scr[c].astype(BF16))
        slot = c % SSD_UNROLL
        for hd in range(DT_ROWS):
            r0 = (hd % 2) * SEG_K_PER_ROW
            blk = jnp.where((sub16 >= r0) & (sub16 < r0 + 3), 1.0, 0.0)
            for i in range(3):
                blk = jnp.where(sub16 == r0 + 3 + i, rpart_scr[i, c, hd:hd + 1, :], blk)
            rhs_scr[slot, (hd // 2) * BF16_ROWS:(hd // 2 + 1) * BF16_ROWS, hd * L:(hd + 1) * L] = blk.astype(BF16)
        seg = _dot(lhs_scr[c], rhs_scr[slot])
        dt_f_r = dtv_scr[c, 0:hpg, :]
        dt_b_r = dtv_scr[c, hpg:DT_ROWS, :]
        ys = []
        for j in range(npair):
            ws_ = []
            for h in (2 * j, 2 * j + 1):
                ef = jnp.exp2(jnp.where(mask_f, seg[:, h * L:(h + 1) * L], neg_inf))
                eb = jnp.exp2(jnp.where(mask_b, seg[:, (hpg + h) * L:(hpg + h + 1) * L], neg_inf))
                ws_.append((cb * (ef * dt_f_r[h:h + 1, :] + eb * dt_b_r[h:h + 1, :])).astype(BF16))
            scale_f = jnp.exp2(seg[:, (DT_ROWS + j) * L:(DT_ROWS + j + 1) * L])
            scale_b = jnp.exp2(seg[:, (DT_ROWS + npair + j) * L:(DT_ROWS + npair + j + 1) * L])
            ys.append(_dot(jnp.concatenate(ws_, axis=1), xm_scr[c, j])
                      + scale_f * off_f[:, pair_cols[j]] + scale_b * off_b[:, pair_cols[j]])
        y = jnp.concatenate(ys, axis=1) + dskip_ref[...] * xc[:, x_cols]
        o_ref[pl.ds(base, L), :] = y
        return carry
    lax.fori_loop(0, nc, pass_c, 0, unroll=SSD_UNROLL)


def _conv_shift_matrices():
    out = np.zeros((3, len(CONV_SHIFTED_TAPS) * SSM_CHUNK, CONV_WIN), np.float32)
    for v, lead in enumerate((0, CONV_WIN_LEAD, CONV_WIN - SSM_CHUNK)):
        for i, kk in enumerate(CONV_SHIFTED_TAPS):
            for t in range(SSM_CHUNK):
                j = t + kk - SSM_CONV // 2 + lead
                if 0 <= j < CONV_WIN:
                    out[v, i * SSM_CHUNK + t, j] = 1.0
    return out


def _seg_matmul_constants(nc):
    L, K, half = SSM_CHUNK, SEG_K_PER_ROW, SSM_HEAD_DIM
    place = np.zeros((3 * L, nc * L), np.float32)
    lhs_ones = np.zeros((1, nc * L), np.float32)
    for c in range(nc):
        for hd in range(DT_ROWS):
            for i in range(3):
                place[i * L + c * DT_ROWS + hd, c * L + hd * K + i] = 1.0
                lhs_ones[0, c * L + hd * K + 3 + i] = 1.0
    rhs = np.zeros((SEG_K, SEG_BLOCKS * L), np.float32)
    for hd in range(DT_ROWS):
        d, h = divmod(hd, HEADS_PER_GROUP)
        pair_block = DT_ROWS + d * (HEADS_PER_GROUP // 2) + h // 2
        lanes = slice(0, half) if h % 2 == 0 else slice(half, L)
        rhs[hd * K:hd * K + 3, hd * L:(hd + 1) * L] = 1.0
        rhs[hd * K:hd * K + 3, pair_block * L:(pair_block + 1) * L][:, lanes] = 1.0
    return place, lhs_ones, rhs


def _ssd_call(xbc, dtc, cw, cbias, alog, dtbias, dskip, batch, seq):
    nc = seq // SSM_CHUNK
    L = SSM_CHUNK
    assert nc >= 3 and seq >= CONV_WIN and nc * DT_ROWS == L
    sh = jnp.asarray(_conv_shift_matrices(), BF16)
    place, lhs_ones, rhs_const = _seg_matmul_constants(nc)
    place = jnp.asarray(place, BF16)
    lhs_ones = jnp.asarray(lhs_ones, F32)
    rhs_const = jnp.asarray(rhs_const, BF16)
    const = lambda a: pl.BlockSpec(a.shape, lambda b, g: (0,) * a.ndim)
    b_blk0 = SSM_INNER // SSM_STATE
    c_blk0 = b_blk0 + SSM_GROUPS

    def xbc_views(rows_, batched):
        lead = (lambda b: b) if batched else (lambda b: 0)
        return [pl.BlockSpec((rows_, GROUP_INNER), lambda b, g: (lead(b), g)),
                pl.BlockSpec((rows_, SSM_STATE), lambda b, g: (lead(b), b_blk0 + g)),
                pl.BlockSpec((rows_, SSM_STATE), lambda b, g: (lead(b), c_blk0 + g))]
    return pl.pallas_call(
        _ssd_kernel,
        grid=(batch, SSM_GROUPS),
        in_specs=xbc_views(seq, True) + [
                  pl.BlockSpec((nc, DT_ROWS, L), lambda b, g: (b, g, 0)),
                  const(sh), const(place), const(lhs_ones), const(rhs_const)]
                 + xbc_views(SUBLANES, False) + xbc_views(1, False) + [
                  pl.BlockSpec((DT_ROWS, L), lambda b, g: (g, 0)),
                  pl.BlockSpec((DT_ROWS, L), lambda b, g: (g, 0)),
                  pl.BlockSpec((1, GROUP_INNER), lambda b, g: (0, g))],
        out_specs=pl.BlockSpec((seq, GROUP_INNER), lambda b, g: (b, g)),
        out_shape=jax.ShapeDtypeStruct((batch * seq, SSM_INNER), F32),
        scratch_shapes=[pltpu.VMEM((nc, L, GROUP_COLS), F32),
                        pltpu.VMEM((nc, HEADS_PER_GROUP // 2, 2 * L, L), BF16),
                        pltpu.VMEM((nc, SSM_STATE, GROUP_INNER), F32),
                        pltpu.VMEM((nc, SSM_STATE, GROUP_INNER), F32),
                        pltpu.VMEM((nc, DT_ROWS, L), F32),
                        pltpu.VMEM((nc, DT_ROWS, L), F32),
                        pltpu.VMEM((nc, DT_ROWS, L), F32),
                        pltpu.VMEM((nc, L, SEG_K), BF16),
                        pltpu.VMEM((SSD_UNROLL, SEG_K, SEG_BLOCKS * L), BF16),
                        pltpu.VMEM((3, nc, DT_ROWS, L), F32)],
        compiler_params=pltpu.CompilerParams(dimension_semantics=("arbitrary", "arbitrary"),
                                             vmem_limit_bytes=VMEM_LIMIT),
        name="ssd",
    )(xbc, xbc, xbc, dtc, sh, place, lhs_ones, rhs_const, cw, cw, cw, cbias, cbias, cbias,
      alog, dtbias, dskip)


def _mlp_kernel(x_ref, attn_ref, ssm_ref, z_ref, gat_ref, gn_ref, wo_ref, gm_ref, wup_ref, wdn_ref, o_ref):
    for sub in range(x_ref.shape[0] // MLP_SUBTILE):
        rs = slice(sub * MLP_SUBTILE, (sub + 1) * MLP_SUBTILE)
        a = attn_ref[rs, :]
        an = (a * _inv_rms(a) * gat_ref[...]).astype(BF16)
        zc = z_ref[rs, :].astype(F32)
        y = ssm_ref[rs, :] * (zc * jax.nn.sigmoid(zc))
        yn = jnp.concatenate([y[:, g * GROUP_INNER:(g + 1) * GROUP_INNER]
                              * _inv_rms(y[:, g * GROUP_INNER:(g + 1) * GROUP_INNER]) for g in range(SSM_GROUPS)],
                             axis=1) * gn_ref[...]
        mix = _dot(an, wo_ref[0:ATTN_WIDTH, :]) + _dot(yn.astype(BF16), wo_ref[ATTN_WIDTH:D_MIX, :])
        x1 = x_ref[rs, :] + mix
        hm = (x1 * _inv_rms(x1) * gm_ref[...]).astype(BF16)
        acc = jnp.zeros_like(x1)
        for c in range(D_FF // FF_CHUNK):
            cols = slice(c * FF_CHUNK, (c + 1) * FF_CHUNK)
            u = _dot(hm, wup_ref[:, cols])
            acc = acc + _dot(jnp.square(jnp.maximum(u, 0.0)).astype(BF16), wdn_ref[cols, :])
        o_ref[rs, :] = x1 + acc


def _mlp_call(x2, attn, ssm, z, gat, gn, wo, gm, wup, wdn):
    t = x2.shape[0]
    row = lambda w: pl.BlockSpec((TM_MLP, w), lambda i: (i, 0))
    full = lambda a: pl.BlockSpec(a.shape, lambda i: (0, 0), pipeline_mode=pl.Buffered(1))
    return pl.pallas_call(
        _mlp_kernel,
        grid=(t // TM_MLP,),
        in_specs=[row(D_MODEL), row(ATTN_WIDTH), row(SSM_INNER), row(SSM_INNER), full(gat), full(gn), full(wo),
                  full(gm), full(wup), full(wdn)],
        out_specs=row(D_MODEL),
        out_shape=jax.ShapeDtypeStruct((t, D_MODEL), F32),
        compiler_params=pltpu.CompilerParams(dimension_semantics=("arbitrary",),
                                             vmem_limit_bytes=VMEM_LIMIT),
        name="outproj_mlp",
    )(x2, attn, ssm, z, gat, gn, wo, gm, wup, wdn)


def _group_major(fwd, bwd):
    lead = fwd.shape[:-1]
    both = jnp.stack([fwd.reshape(lead + (SSM_GROUPS, HEADS_PER_GROUP)),
                      bwd.reshape(lead + (SSM_GROUPS, HEADS_PER_GROUP))], axis=-2)
    return both.reshape(lead + (2 * SSM_HEADS,))


def _rotate_half(w):
    half = QK_ROPE_DIM // 2
    return jnp.concatenate([-w[..., half:], w[..., :half]], axis=-1)


def _swap_halves(g):
    half = QK_ROPE_DIM // 2
    return jnp.concatenate([g[..., half:], g[..., :half]], axis=-1)


def _lane_bcast(v):
    return jnp.broadcast_to(v[..., None], v.shape + (LANES,)).astype(F32)


def _layer(x2, cs, batch, seq, ln_mix_g, w_in, q_a_norm_g, w_uq, kv_a_norm_g, w_ukv, q_norm_g,
           k_norm_g, attn_out_norm_g, conv_w, conv_b, a_log_fwd, a_log_bwd, dt_bias_fwd, dt_bias_bwd,
           d_skip, ssm_norm_g, w_out, ln_mlp_g, w_mlp_up, w_mlp_down):
    win = _win_prep_call(*w_in)
    wq = w_uq.reshape(Q_LORA_RANK, ATTN_HEADS, QK_HEAD_DIM)
    wuq = jnp.concatenate([wq, _rotate_half(wq[..., QK_NOPE_DIM:])], axis=-1)
    wuq = wuq.reshape(Q_LORA_RANK, ATTN_HEADS * HEAD_PAD).astype(BF16)
    wukv = w_ukv.reshape(KV_LORA_RANK, ATTN_HEADS, QK_NOPE_DIM + V_HEAD_DIM)
    wuk = jnp.pad(wukv[:, :, :QK_NOPE_DIM],
                  ((0, 0), (0, 0), (0, HEAD_PAD - QK_NOPE_DIM))).reshape(KV_LORA_RANK, -1).astype(BF16)
    wv = wukv[:, :, QK_NOPE_DIM:].reshape(KV_LORA_RANK, ATTN_HEADS // 2, 2, V_HEAD_DIM)
    zv = jnp.zeros_like(wv[:, :, 0])
    wuv = jnp.stack([jnp.concatenate([wv[:, :, 0], zv], axis=-1),
                     jnp.concatenate([zv, wv[:, :, 1]], axis=-1)], axis=2).reshape(KV_LORA_RANK, -1).astype(BF16)
    ones_v = np.ones((ATTN_HEADS // 2, V_HEAD_DIM), np.float32)
    vone = jnp.asarray(np.stack([np.concatenate([0 * ones_v, ones_v], axis=-1),
                                 np.concatenate([ones_v, 0 * ones_v], axis=-1)], axis=1).reshape(1, -1))
    scale = QK_HEAD_DIM ** -0.5 * np.log2(np.e)
    gq = (jnp.concatenate([q_norm_g, _swap_halves(q_norm_g[QK_NOPE_DIM:])]) * scale)[None, :]
    zeros_nope = jnp.zeros((QK_NOPE_DIM,), k_norm_g.dtype)
    gkn = jnp.concatenate([k_norm_g[:QK_NOPE_DIM], zeros_nope])[None, :]
    gkr = jnp.concatenate([zeros_nope, k_norm_g[QK_NOPE_DIM:], _swap_halves(k_norm_g[QK_NOPE_DIM:])])[None, :]

    q, k, v, z, xbc, dtc = _inproj_call(
        x2, cs, ln_mix_g[None, :], win, q_a_norm_g[None, :], wuq, kv_a_norm_g[None, :], wuk, wuv, vone,
        gq, gkn, gkr)

    attn, (wo_b, wup_b, wdn_b) = _attn_call(q, k, v, (w_out, w_mlp_up, w_mlp_down), batch, seq)

    cw = jnp.pad(conv_w[:, 0, :], ((0, SUBLANES - SSM_CONV), (0, 0)))
    cbias = conv_b[None, :]
    alog = _lane_bcast(_group_major(a_log_fwd, a_log_bwd))
    dtbias = _lane_bcast(_group_major(dt_bias_fwd, dt_bias_bwd))
    dskip = jnp.repeat(d_skip, SSM_HEAD_DIM)[None, :]
    ssm = _ssd_call(xbc, dtc, cw, cbias, alog, dtbias, dskip, batch, seq)

    return _mlp_call(x2, attn, ssm, z, attn_out_norm_g[None, :], ssm_norm_g[None, :], wo_b,
                     ln_mlp_g[None, :], wup_b, wdn_b)


def _rope_table(positions):
    inv_freq = 1.0 / (ROPE_THETA ** (jnp.arange(0, QK_ROPE_DIM, 2, dtype=F32) / QK_ROPE_DIM))
    ang = inv_freq[:, None] * positions.astype(F32).reshape(1, -1)
    return jnp.concatenate([jnp.cos(ang), jnp.sin(ang)], axis=0)


def kernel(x, positions, ln_mix_g, w_in, q_a_norm_g, w_uq, kv_a_norm_g, w_ukv, q_norm_g, k_norm_g,
           attn_out_norm_g, conv_w, conv_b, a_log_fwd, a_log_bwd, dt_bias_fwd, dt_bias_bwd, d_skip,
           ssm_norm_g, w_out, ln_mlp_g, w_mlp_up, w_mlp_down):
    batch, seq, d = x.shape
    assert d == D_MODEL and seq % TQ == 0 and (batch * seq) % TM_IN == 0 and (batch * seq) % TM_MLP == 0
    cs = _rope_table(positions)
    x2 = x.reshape(batch * seq, d)
    for l in range(ln_mix_g.shape[0]):
        x2 = _layer(x2, cs, batch, seq, ln_mix_g[l], (w_in, l), q_a_norm_g[l], w_uq[l], kv_a_norm_g[l],
                    w_ukv[l], q_norm_g[l], k_norm_g[l], attn_out_norm_g[l], conv_w[l], conv_b[l],
                    a_log_fwd[l], a_log_bwd[l], dt_bias_fwd[l], dt_bias_bwd[l], d_skip[l], ssm_norm_g[l],
                    w_out[l], ln_mlp_g[l], w_mlp_up[l], w_mlp_down[l])
    return x2.reshape(batch, seq, d)
```

```python
import numpy as np
import jax
import jax.numpy as jnp
from jax import lax
from jax.experimental import pallas as pl
from jax.experimental.pallas import tpu as pltpu

F32 = jnp.float32
BF16 = jnp.bfloat16

D_MODEL = 1024
ATTN_HEADS = 8
QK_NOPE_DIM = 64
QK_ROPE_DIM = 32
QK_HEAD_DIM = QK_NOPE_DIM + QK_ROPE_DIM
V_HEAD_DIM = 64
Q_LORA_RANK = D_MODEL // 4
KV_LORA_RANK = D_MODEL // 8
ROPE_THETA = 10000.0
ATTN_WIDTH = ATTN_HEADS * V_HEAD_DIM
SSM_HEADS = 8
SSM_HEAD_DIM = 64
SSM_INNER = SSM_HEADS * SSM_HEAD_DIM
SSM_GROUPS = 2
SSM_STATE = 128
SSM_CONV = 5
SSM_CHUNK = 128
SSM_CONV_CH = SSM_INNER + 2 * SSM_GROUPS * SSM_STATE
D_MIX = ATTN_WIDTH + SSM_INNER
D_FF = 4 * D_MODEL
EPS = 1e-6

LANES = 128
SUBLANES = 8
HEAD_PAD = LANES

HEADS_PER_GROUP = SSM_HEADS // SSM_GROUPS
GROUP_INNER = SSM_INNER // SSM_GROUPS
GROUP_COLS = GROUP_INNER + 2 * SSM_STATE
DT_ROWS = 2 * HEADS_PER_GROUP
BF16_ROWS = 16
SEG_K_PER_ROW = 8
SEG_K = DT_ROWS * SEG_K_PER_ROW
SEG_BLOCKS = DT_ROWS + HEADS_PER_GROUP
LOG2E = float(np.log2(np.e))
CONV_WIN = 2 * SSM_CHUNK
CONV_WIN_LEAD = SSM_CHUNK // 2
CONV_SHIFTED_TAPS = tuple(k for k in range(SSM_CONV) if k != SSM_CONV // 2)

COL_CKV = Q_LORA_RANK
COL_MISC = COL_CKV + KV_LORA_RANK
COL_Z = COL_MISC + LANES
COL_XBC = COL_Z + SSM_INNER
IN_COLS = COL_XBC + SSM_CONV_CH

TM_IN = 1024
IN_SUBTILE = 512
TQ = 1024
ATTN_HEADS_PER_STEP = 8
TM_MLP = 1024
MLP_SUBTILE = 512
FF_CHUNK = 1024
SSD_UNROLL = 8
CONV_AHEAD = 4
PASS_A_UNROLL = 4
WIN_PREP_STEPS = 4
VMEM_LIMIT = 56 * 1024 * 1024


def _inv_rms(x):
    n = x.shape[-1]
    x2 = x * x
    acc = x2[:, 0:LANES]
    for i in range(1, n // LANES):
        acc = acc + x2[:, i * LANES:(i + 1) * LANES]
    return lax.rsqrt(jnp.sum(acc, axis=-1, keepdims=True) * (1.0 / n) + EPS)


def _dot(a, b):
    return jnp.dot(a, b, preferred_element_type=F32)


def _dot_nt(a, b):
    return lax.dot_general(a, b, (((1,), (1,)), ((), ())), preferred_element_type=F32)


def _inproj_kernel(x_ref, cs_ref, g_ref, win_ref, gqa_ref, wuq_ref, gkva_ref, wuk_ref, wuv_ref, vone_ref,
                   hg_ref,
                   q_ref, k_ref, v_ref, z_ref, xbc_ref, dt_ref):
    for sub in range(x_ref.shape[0] // IN_SUBTILE):
        _inproj_rows(sub * IN_SUBTILE, x_ref, cs_ref, g_ref, win_ref, gqa_ref, wuq_ref, gkva_ref, wuk_ref,
                     wuv_ref, vone_ref, hg_ref, q_ref, k_ref, v_ref, z_ref, xbc_ref, dt_ref)


def _inproj_rows(r0, x_ref, cs_ref, g_ref, win_ref, gqa_ref, wuq_ref, gkva_ref, wuk_ref, wuv_ref, vone_ref,
                 hg_ref, q_ref, k_ref, v_ref, z_ref, xbc_ref, dt_ref):
    tm = IN_SUBTILE
    rs = slice(r0, r0 + tm)
    x = x_ref[rs, :]
    h = (x * _inv_rms(x) * g_ref[...]).astype(BF16)
    big = _dot(h, win_ref[...])
    z_ref[rs, :] = big[:, COL_Z:COL_XBC].astype(BF16)
    xbc_ref[rs, :] = big[:, COL_XBC:IN_COLS].astype(BF16)
    misc = big[:, COL_MISC:COL_Z]
    for c in range(tm // SSM_CHUNK):
        dt_ref[r0 // SSM_CHUNK + c] = misc[c * SSM_CHUNK:(c + 1) * SSM_CHUNK, :].T[0:2 * SSM_HEADS, :]

    cq = big[:, 0:COL_CKV]
    ckv = big[:, COL_CKV:COL_MISC]
    cqn = (cq * _inv_rms(cq) * gqa_ref[...]).astype(BF16)
    ckvn = (ckv * _inv_rms(ckv) * gkva_ref[...]).astype(BF16)
    q_pre = _dot(cqn, wuq_ref[...])
    k_pre = _dot(ckvn, wuk_ref[...])
    v_ref[rs, :] = (_dot(ckvn, wuv_ref[...]) + vone_ref[...]).astype(BF16)

    half = QK_ROPE_DIM // 2
    zpad = jnp.zeros((HEAD_PAD - QK_ROPE_DIM, SSM_CHUNK), F32)
    tt = jnp.concatenate(
        [jnp.concatenate([cs_ref[:, r0 + c * SSM_CHUNK:r0 + (c + 1) * SSM_CHUNK], zpad], axis=0).T
         for c in range(tm // SSM_CHUNK)], axis=0)
    lane_t = lax.broadcasted_iota(jnp.int32, (tm, HEAD_PAD), 1)
    cs = jnp.where(lane_t < QK_NOPE_DIM, 1.0,
                   jnp.where(lane_t < QK_NOPE_DIM + half, pltpu.roll(tt, QK_NOPE_DIM, 1),
                             jnp.where(lane_t < QK_HEAD_DIM + half, pltpu.roll(tt, QK_NOPE_DIM + half, 1),
                                       pltpu.roll(tt, QK_HEAD_DIM, 1))))
    lane = lax.broadcasted_iota(jnp.int32, (1, HEAD_PAD), 1)
    in_head = (lane < QK_HEAD_DIM).astype(F32)
    is_rope = ((lane >= QK_NOPE_DIM) & (lane < QK_HEAD_DIM)).astype(F32)
    inv_d = 1.0 / QK_HEAD_DIM

    gcq = hg_ref[0:1, :] * cs
    for hh in range(ATTN_HEADS):
        sl = slice(hh * HEAD_PAD, (hh + 1) * HEAD_PAD)
        qh = q_pre[:, sl]
        ssq = jnp.sum(qh * qh * in_head, axis=-1, keepdims=True)
        q_ref[rs, sl] = (qh * lax.rsqrt(ssq * inv_d + EPS) * gcq).astype(BF16)

    ab = misc * (hg_ref[2:3, :] * cs)
    lane2 = lax.broadcasted_iota(jnp.int32, (tm, HEAD_PAD), 1)
    swapped = jnp.where(lane2 < QK_HEAD_DIM, pltpu.roll(ab, HEAD_PAD - QK_ROPE_DIM, 1),
                        pltpu.roll(ab, QK_ROPE_DIM, 1))
    s_both = jnp.where(lane2 >= QK_NOPE_DIM, ab + swapped, 0.0)
    ssq_pe = jnp.sum(misc * misc * is_rope, axis=-1, keepdims=True)
    gkn = hg_ref[1:2, :]
    for hh in range(ATTN_HEADS):
        sl = slice(hh * HEAD_PAD, (hh + 1) * HEAD_PAD)
        kh = k_pre[:, sl]
        ssq = jnp.sum(kh * kh, axis=-1, keepdims=True) + ssq_pe
        k_ref[rs, sl] = ((kh * gkn + s_both) * lax.rsqrt(ssq * inv_d + EPS)).astype(BF16)


def _win_prep_kernel(w_ref, o_ref):
    w = w_ref[...].astype(F32)
    o_kpe = Q_LORA_RANK + KV_LORA_RANK
    o_z = o_kpe + QK_ROPE_DIM
    o_dt = o_z + SSM_INNER + SSM_CONV_CH
    half = QK_ROPE_DIM // 2
    hpg = HEADS_PER_GROUP
    dt_f = w[:, o_dt:o_dt + SSM_HEADS]
    dt_b = w[:, o_dt + SSM_HEADS:o_dt + 2 * SSM_HEADS]
    kpe = w[:, o_kpe:o_z]
    pieces = [w[:, 0:o_kpe]]
    for g in range(SSM_GROUPS):
        pieces += [dt_f[:, g * hpg:(g + 1) * hpg], dt_b[:, g * hpg:(g + 1) * hpg]]
    pieces += [jnp.zeros((w.shape[0], QK_NOPE_DIM - 2 * SSM_HEADS), F32), kpe, -kpe[:, half:], kpe[:, :half],
               w[:, o_z:o_dt]]
    o_ref[...] = jnp.concatenate(pieces, axis=1).astype(BF16)


def _win_prep_call(w_in3, layer):
    depth, d, n = w_in3.shape
    rows = d // WIN_PREP_STEPS
    return pl.pallas_call(
        _win_prep_kernel,
        grid=(WIN_PREP_STEPS,),
        in_specs=[pl.BlockSpec((rows, n), lambda i: (layer * WIN_PREP_STEPS + i, 0))],
        out_specs=pl.BlockSpec((rows, IN_COLS), lambda i: (i, 0)),
        out_shape=jax.ShapeDtypeStruct((d, IN_COLS), BF16),
        compiler_params=pltpu.CompilerParams(dimension_semantics=("arbitrary",), vmem_limit_bytes=VMEM_LIMIT),
        name="win_prep",
    )(w_in3.reshape(depth * d, n).astype(BF16))


def _inproj_call(x2, cs, g, win, gqa, wuq, gkva, wuk, wuv, vone, hg):
    t = x2.shape[0]
    cpt = TM_IN // SSM_CHUNK
    full = lambda a: pl.BlockSpec(a.shape, lambda i: (0,) * a.ndim, pipeline_mode=pl.Buffered(1))
    row = lambda w: pl.BlockSpec((TM_IN, w), lambda i: (i, 0))
    return pl.pallas_call(
        _inproj_kernel,
        grid=(t // TM_IN,),
        in_specs=[row(D_MODEL), pl.BlockSpec((QK_ROPE_DIM, TM_IN), lambda i: (0, i)), full(g), full(win), full(gqa), full(wuq), full(gkva),
                  full(wuk), full(wuv), full(vone), full(hg)],
        out_specs=[row(ATTN_HEADS * HEAD_PAD), row(ATTN_HEADS * HEAD_PAD), row(ATTN_HEADS * HEAD_PAD),
                   row(SSM_INNER), row(SSM_CONV_CH),
                   pl.BlockSpec((cpt, 2 * SSM_HEADS, SSM_CHUNK), lambda i: (i, 0, 0))],
        out_shape=[jax.ShapeDtypeStruct((t, ATTN_HEADS * HEAD_PAD), BF16),
                   jax.ShapeDtypeStruct((t, ATTN_HEADS * HEAD_PAD), BF16),
                   jax.ShapeDtypeStruct((t, ATTN_HEADS * HEAD_PAD), BF16),
                   jax.ShapeDtypeStruct((t, SSM_INNER), BF16),
                   jax.ShapeDtypeStruct((t, SSM_CONV_CH), BF16),
                   jax.ShapeDtypeStruct((t // SSM_CHUNK, 2 * SSM_HEADS, SSM_CHUNK), F32)],
        compiler_params=pltpu.CompilerParams(dimension_semantics=("arbitrary",),
                                             vmem_limit_bytes=VMEM_LIMIT),
        name="inproj",
    )(x2, cs, g, win, gqa, wuq, gkva, wuk, wuv, vone, hg)


def _attn_kernel(q_ref, k_ref, v_ref, *refs):
    n_w = (len(refs) - 1) // 2
    w_in_refs, o_ref, w_out_refs = refs[:n_w], refs[n_w], refs[n_w + 1:]
    for w_in_ref, w_out_ref in zip(w_in_refs, w_out_refs):
        w_out_ref[...] = w_in_ref[...].astype(BF16)

    lane = lax.broadcasted_iota(jnp.int32, (q_ref.shape[0], HEAD_PAD), 1)
    for jp in range(ATTN_HEADS_PER_STEP // 2):
        sls = [slice(j * HEAD_PAD, (j + 1) * HEAD_PAD) for j in (2 * jp, 2 * jp + 1)]
        ss = [_dot_nt(q_ref[:, sl], k_ref[:, sl]) for sl in sls]
        ps = [jnp.exp2(s - jnp.max(s, axis=-1, keepdims=True)).astype(BF16) for s in ss]
        accs = [_dot(p, v_ref[:, sl]) for p, sl in zip(ps, sls)]
        res = [acc / pltpu.roll(acc, V_HEAD_DIM, 1) for acc in accs]
        o_ref[:, jp * HEAD_PAD:(jp + 1) * HEAD_PAD] = jnp.where(lane < V_HEAD_DIM, res[0], res[1])


def _attn_call(q, k, v, later_weights, batch, seq):
    n_q = seq // TQ
    hps = ATTN_HEADS_PER_STEP
    assert ATTN_HEADS == hps
    n_steps = batch * n_q

    def rows_spec(w):
        assert w.shape[0] % (n_steps * BF16_ROWS) == 0
        return pl.BlockSpec((w.shape[0] // n_steps, w.shape[1]), lambda b, hp, i: (b * n_q + i, 0))
    w_specs = [rows_spec(w) for w in later_weights]
    outs = pl.pallas_call(
        _attn_kernel,
        grid=(batch, ATTN_HEADS // hps, n_q),
        in_specs=[pl.BlockSpec((TQ, hps * HEAD_PAD), lambda b, hp, i: (b * n_q + i, hp)),
                  pl.BlockSpec((seq, hps * HEAD_PAD), lambda b, hp, i: (b, hp)),
                  pl.BlockSpec((seq, hps * HEAD_PAD), lambda b, hp, i: (b, hp))] + w_specs,
        out_specs=[pl.BlockSpec((TQ, hps * V_HEAD_DIM), lambda b, hp, i: (b * n_q + i, hp))] + w_specs,
        out_shape=[jax.ShapeDtypeStruct((batch * seq, ATTN_WIDTH), F32)]
                  + [jax.ShapeDtypeStruct(w.shape, BF16) for w in later_weights],
        compiler_params=pltpu.CompilerParams(
            dimension_semantics=("arbitrary", "arbitrary", "arbitrary"),
            vmem_limit_bytes=VMEM_LIMIT),
        name="attention",
    )(q, k, v, *later_weights)
    return outs[0], outs[1:]


def _split3(x):
    hi = x.astype(BF16)
    r1 = x - hi.astype(F32)
    mid = r1.astype(BF16)
    lo = (r1 - mid.astype(F32)).astype(BF16)
    return hi, mid, lo


def _ssd_kernel(x_ref, b_ref, c_ref, dt_ref, sh_ref, place_ref, lhs_ones_ref, rhs_const_ref,
                cwx_ref, cwb_ref, cwc_ref, cbx_ref, cbb_ref, cbc_ref, adt_ref, dskip_ref,
                o_ref, xc_scr, xm_scr, nsf_scr, nsb_scr, dtv_scr, w_scr, dec_scr, lhs_scr, rhs_scr, rpart_scr):
    nc = dt_ref.shape[0]
    seq = x_ref.shape[0]
    L = SSM_CHUNK
    hpg = HEADS_PER_GROUP
    npair = hpg // 2
    P = SSM_HEAD_DIM
    x_cols = slice(0, GROUP_INNER)
    b_cols = slice(GROUP_INNER, GROUP_INNER + SSM_STATE)
    c_cols = slice(GROUP_INNER + SSM_STATE, GROUP_COLS)
    pair_cols = [slice(j * L, (j + 1) * L) for j in range(npair)]

    @pl.when((pl.program_id(0) == 0) & (pl.program_id(1) == 0))
    def _init_rhs():
        for u in range(SSD_UNROLL):
            rhs_scr[u] = rhs_const_ref[...]

    a_neg = -jnp.exp(adt_ref[0]) * LOG2E
    dtv = jax.nn.softplus(dt_ref[...] + adt_ref[1][None])
    dtv_scr[...] = dtv
    da2 = (dtv * a_neg[None]).reshape(nc * DT_ROWS, L)
    ri = lax.broadcasted_iota(jnp.int32, (L, L), 0)
    ci = lax.broadcasted_iota(jnp.int32, (L, L), 1)
    upper = (ri <= ci).astype(BF16)
    lower = (ri >= ci).astype(BF16)
    tri = jnp.concatenate([upper, lower], axis=1)
    cs_fb = sum(_dot(p, tri) for p in _split3(da2))
    rowsel = (ri & hpg) == 0
    cs2 = jnp.where(rowsel, cs_fb[:, 0:L], cs_fb[:, L:2 * L])
    colcs = cs2.T

    col_parts = jnp.concatenate(_split3(colcs), axis=1)
    half_n = place_ref.shape[1] // 2
    for hf in range(2):
        cols = slice(hf * half_n, (hf + 1) * half_n)
        lhs_half = _dot(col_parts, place_ref[:, cols]) + lhs_ones_ref[:, cols]
        for c in range(nc // 2):
            lhs_scr[hf * (nc // 2) + c] = lhs_half[:, c * L:c * L + SEG_K].astype(BF16)
    for i, part in enumerate(_split3(-cs2)):
        rpart_scr[i] = part.astype(F32).reshape(nc, DT_ROWS, L)

    def lane_bcast(col):
        return jnp.broadcast_to(col, (col.shape[0], L))

    tot2 = jnp.where(rowsel, lane_bcast(cs2[:, L - 1:L]), lane_bcast(cs2[:, 0:1]))
    dec_scr[...] = jnp.exp2(tot2).reshape(nc, DT_ROWS, L)
    w_scr[...] = (dtv.reshape(nc * DT_ROWS, L) * jnp.exp2(tot2 - cs2)).reshape(nc, DT_ROWS, L)

    lane_t = lax.broadcasted_iota(jnp.int32, (L, L), 1)
    low_half = lane_t < P

    cw_halves = (cwx_ref[...], jnp.concatenate([cwb_ref[...], cwc_ref[...]], axis=1))
    cb_halves = (cbx_ref[...], jnp.concatenate([cbb_ref[...], cbc_ref[...]], axis=1))

    def rows(hf, start, size):
        if hf == 0:
            return x_ref[pl.ds(start, size), :]
        return jnp.concatenate([b_ref[pl.ds(start, size), :], c_ref[pl.ds(start, size), :]], axis=1)

    def conv_stage(c):
        base = pl.multiple_of(c * L, L)
        ws = pl.multiple_of(jnp.clip(c * L - CONV_WIN_LEAD, 0, seq - CONV_WIN), CONV_WIN_LEAD)
        variant = jnp.where(c == 0, 0, jnp.where(c == nc - 1, 2, 1))
        sh = sh_ref[variant]
        halves = []
        for hf in range(2):
            cw = cw_halves[hf]
            shifted = _dot(sh, rows(hf, ws, CONV_WIN))
            acc = cb_halves[hf] + cw[SSM_CONV // 2:SSM_CONV // 2 + 1, :] * rows(hf, base, L).astype(F32)
            for t, kk in enumerate(CONV_SHIFTED_TAPS):
                acc = acc + cw[kk:kk + 1, :] * shifted[t * L:(t + 1) * L, :]
            halves.append(acc * jax.nn.sigmoid(acc))
        xc = jnp.concatenate(halves, axis=1)
        xc_scr[c] = xc
        for j in range(npair):
            xp = xc[:, pair_cols[j]]
            xm_scr[c, j] = jnp.concatenate([jnp.where(low_half, xp, 0.0), jnp.where(low_half, 0.0, xp)],
                                           axis=0).astype(BF16)

    def state_loads(c):
        return xc_scr[c, :, b_cols], [xm_scr[c, j] for j in range(npair)]

    def state_stage(c, loaded):
        b_tok, xm = loaded
        bt = b_tok.T
        for d, ns_scr in ((0, nsf_scr), (1, nsb_scr)):
            w = w_scr[c, d * hpg:(d + 1) * hpg, :]
            ns_scr[c] = jnp.concatenate(
                [_dot(jnp.concatenate([(bt * w[2 * j:2 * j + 1, :]).astype(BF16),
                                       (bt * w[2 * j + 1:2 * j + 2, :]).astype(BF16)], axis=1), xm[j])
                 for j in range(npair)], axis=1)

    for c in range(CONV_AHEAD):
        conv_stage(c)

    def pass_a(c, carry):
        loaded = state_loads(c)
        conv_stage(c + CONV_AHEAD)
        state_stage(c, loaded)
        return carry
    lax.fori_loop(0, nc - CONV_AHEAD, pass_a, 0, unroll=PASS_A_UNROLL)
    for c in range(nc - CONV_AHEAD, nc):
        state_stage(c, state_loads(c))

    lane_r = lax.broadcasted_iota(jnp.int32, (1, L), 1)

    def decay_row(c, d):
        dec = dec_scr[c, d * hpg:(d + 1) * hpg, :]
        return jnp.concatenate([jnp.where(lane_r < P, dec[2 * j:2 * j + 1, :], dec[2 * j + 1:2 * j + 2, :])
                                for j in range(npair)], axis=1)

    def pass_b(d, ns_scr):
        def body(i, st):
            c = i if d == 0 else nc - 1 - i
            new = ns_scr[c]
            ns_scr[c] = st
            return st * decay_row(c, d) + new
        lax.fori_loop(0, nc, body, jnp.zeros((SSM_STATE, GROUP_INNER), F32))
    pass_b(0, nsf_scr)
    pass_b(1, nsb_scr)

    mask_f = ci <= ri
    mask_b = ci >= ri
    neg_inf = jnp.float32(-jnp.inf)
    sub16 = lax.broadcasted_iota(jnp.int32, (BF16_ROWS, L), 0)

    def pass_c(c, carry):
        base = pl.multiple_of(c * L, L)
        xc = xc_scr[c]
        bm = xc[:, b_cols].astype(BF16)
        cm = xc[:, c_cols].astype(BF16)
        cb = _dot_nt(cm, bm)
        off_f = _dot(cm, nsf_scr[c].astype(BF16))
        off_b = _dot(cm, nsb_scr[c].astype(BF16))
        slot = c % SSD_UNROLL
        for hd in range(DT_ROWS):
            r0 = (hd % 2) * SEG_K_PER_ROW
            blk = jnp.where((sub16 >= r0) & (sub16 < r0 + 3), 1.0, 0.0)
            for i in range(3):
                blk = jnp.where(sub16 == r0 + 3 + i, rpart_scr[i, c, hd:hd + 1, :], blk)
            rhs_scr[slot, (hd // 2) * BF16_ROWS:(hd // 2 + 1) * BF16_ROWS, hd * L:(hd + 1) * L] = blk.astype(BF16)
        seg = _dot(lhs_scr[c], rhs_scr[slot])
        dt_f_r = dtv_scr[c, 0:hpg, :]
        dt_b_r = dtv_scr[c, hpg:DT_ROWS, :]
        ys = []
        for j in range(npair):
            ws_ = []
            for h in (2 * j, 2 * j + 1):
                ef = jnp.exp2(jnp.where(mask_f, seg[:, h * L:(h + 1) * L], neg_inf))
                eb = jnp.exp2(jnp.where(mask_b, seg[:, (hpg + h) * L:(hpg + h + 1) * L], neg_inf))
                ws_.append((cb * (ef * dt_f_r[h:h + 1, :] + eb * dt_b_r[h:h + 1, :])).astype(BF16))
            scale_f = jnp.exp2(seg[:, (DT_ROWS + j) * L:(DT_ROWS + j + 1) * L])
            scale_b = jnp.exp2(seg[:, (DT_ROWS + npair + j) * L:(DT_ROWS + npair + j + 1) * L])
            ys.append(_dot(jnp.concatenate(ws_, axis=1), xm_scr[c, j])
                      + scale_f * off_f[:, pair_cols[j]] + scale_b * off_b[:, pair_cols[j]])
        y = jnp.concatenate(ys, axis=1) + dskip_ref[...] * xc[:, x_cols]
        o_ref[pl.ds(base, L), :] = y
        return carry
    lax.fori_loop(0, nc, pass_c, 0, unroll=SSD_UNROLL)


def _conv_shift_matrices():
    out = np.zeros((3, len(CONV_SHIFTED_TAPS) * SSM_CHUNK, CONV_WIN), np.float32)
    for v, lead in enumerate((0, CONV_WIN_LEAD, CONV_WIN - SSM_CHUNK)):
        for i, kk in enumerate(CONV_SHIFTED_TAPS):
            for t in range(SSM_CHUNK):
                j = t + kk - SSM_CONV // 2 + lead
                if 0 <= j < CONV_WIN:
                    out[v, i * SSM_CHUNK + t, j] = 1.0
    return out


def _seg_matmul_constants(nc):
    L, K, half = SSM_CHUNK, SEG_K_PER_ROW, SSM_HEAD_DIM
    place = np.zeros((3 * L, nc * L), np.float32)
    lhs_ones = np.zeros((1, nc * L), np.float32)
    for c in range(nc):
        for hd in range(DT_ROWS):
            for i in range(3):
                place[i * L + c * DT_ROWS + hd, c * L + hd * K + i] = 1.0
                lhs_ones[0, c * L + hd * K + 3 + i] = 1.0
    rhs = np.zeros((SEG_K, SEG_BLOCKS * L), np.float32)
    for hd in range(DT_ROWS):
        d, h = divmod(hd, HEADS_PER_GROUP)
        pair_block = DT_ROWS + d * (HEADS_PER_GROUP // 2) + h // 2
        lanes = slice(0, half) if h % 2 == 0 else slice(half, L)
        rhs[hd * K:hd * K + 3, hd * L:(hd + 1) * L] = 1.0
        rhs[hd * K:hd * K + 3, pair_block * L:(pair_block + 1) * L][:, lanes] = 1.0
    return place, lhs_ones, rhs


def _ssd_call(xbc, dtc, cw, cbias, adt, dskip, batch, seq):
    nc = seq // SSM_CHUNK
    L = SSM_CHUNK
    assert nc >= 3 and seq >= CONV_WIN and nc * DT_ROWS == L
    sh = jnp.asarray(_conv_shift_matrices(), BF16)
    place, lhs_ones, rhs_const = _seg_matmul_constants(nc)
    place = jnp.asarray(place, BF16)
    lhs_ones = jnp.asarray(lhs_ones, F32)
    rhs_const = jnp.asarray(rhs_const, BF16)
    const = lambda a: pl.BlockSpec(a.shape, lambda b, g: (0,) * a.ndim)
    b_blk0 = SSM_INNER // SSM_STATE
    c_blk0 = b_blk0 + SSM_GROUPS

    def xbc_views(rows_, batched):
        lead = (lambda b: b) if batched else (lambda b: 0)
        return [pl.BlockSpec((rows_, GROUP_INNER), lambda b, g: (lead(b), g)),
                pl.BlockSpec((rows_, SSM_STATE), lambda b, g: (lead(b), b_blk0 + g)),
                pl.BlockSpec((rows_, SSM_STATE), lambda b, g: (lead(b), c_blk0 + g))]
    return pl.pallas_call(
        _ssd_kernel,
        grid=(batch, SSM_GROUPS),
        in_specs=xbc_views(seq, True) + [
                  pl.BlockSpec((nc, DT_ROWS, L), lambda b, g: (b, g, 0)),
                  const(sh), const(place), const(lhs_ones), const(rhs_const)]
                 + xbc_views(SUBLANES, False) + xbc_views(1, False) + [
                  pl.BlockSpec((2, DT_ROWS, L), lambda b, g: (0, g, 0)),
                  pl.BlockSpec((1, GROUP_INNER), lambda b, g: (0, g))],
        out_specs=pl.BlockSpec((seq, GROUP_INNER), lambda b, g: (b, g)),
        out_shape=jax.ShapeDtypeStruct((batch * seq, SSM_INNER), F32),
        scratch_shapes=[pltpu.VMEM((nc, L, GROUP_COLS), F32),
                        pltpu.VMEM((nc, HEADS_PER_GROUP // 2, 2 * L, L), BF16),
                        pltpu.VMEM((nc, SSM_STATE, GROUP_INNER), F32),
                        pltpu.VMEM((nc, SSM_STATE, GROUP_INNER), F32),
                        pltpu.VMEM((nc, DT_ROWS, L), F32),
                        pltpu.VMEM((nc, DT_ROWS, L), F32),
                        pltpu.VMEM((nc, DT_ROWS, L), F32),
                        pltpu.VMEM((nc, L, SEG_K), BF16),
                        pltpu.VMEM((SSD_UNROLL, SEG_K, SEG_BLOCKS * L), BF16),
                        pltpu.VMEM((3, nc, DT_ROWS, L), F32)],
        compiler_params=pltpu.CompilerParams(dimension_semantics=("arbitrary", "arbitrary"),
                                             vmem_limit_bytes=VMEM_LIMIT),
        name="ssd",
    )(xbc, xbc, xbc, dtc, sh, place, lhs_ones, rhs_const, cw, cw, cw, cbias, cbias, cbias,
      adt, dskip)


def _mlp_kernel(x_ref, attn_ref, ssm_ref, z_ref, gat_ref, gn_ref, wo_ref, gm_ref, wup_ref, wdn_ref, o_ref):
    for sub in range(x_ref.shape[0] // MLP_SUBTILE):
        rs = slice(sub * MLP_SUBTILE, (sub + 1) * MLP_SUBTILE)
        a = attn_ref[rs, :]
        an = (a * _inv_rms(a) * gat_ref[...]).astype(BF16)
        zc = z_ref[rs, :].astype(F32)
        y = ssm_ref[rs, :] * (zc * jax.nn.sigmoid(zc))
        yn = jnp.concatenate([y[:, g * GROUP_INNER:(g + 1) * GROUP_INNER]
                              * _inv_rms(y[:, g * GROUP_INNER:(g + 1) * GROUP_INNER]) for g in range(SSM_GROUPS)],
                             axis=1) * gn_ref[...]
        mix = _dot(an, wo_ref[0:ATTN_WIDTH, :]) + _dot(yn.astype(BF16), wo_ref[ATTN_WIDTH:D_MIX, :])
        x1 = x_ref[rs, :] + mix
        hm = (x1 * _inv_rms(x1) * gm_ref[...]).astype(BF16)
        acc = jnp.zeros_like(x1)
        for c in range(D_FF // FF_CHUNK):
            cols = slice(c * FF_CHUNK, (c + 1) * FF_CHUNK)
            u = _dot(hm, wup_ref[:, cols])
            acc = acc + _dot(jnp.square(jnp.maximum(u, 0.0)).astype(BF16), wdn_ref[cols, :])
        o_ref[rs, :] = x1 + acc


def _mlp_call(x2, attn, ssm, z, gat, gn, wo, gm, wup, wdn):
    t = x2.shape[0]
    row = lambda w: pl.BlockSpec((TM_MLP, w), lambda i: (i, 0))
    full = lambda a: pl.BlockSpec(a.shape, lambda i: (0, 0), pipeline_mode=pl.Buffered(1))
    return pl.pallas_call(
        _mlp_kernel,
        grid=(t // TM_MLP,),
        in_specs=[row(D_MODEL), row(ATTN_WIDTH), row(SSM_INNER), row(SSM_INNER), full(gat), full(gn), full(wo),
                  full(gm), full(wup), full(wdn)],
        out_specs=row(D_MODEL),
        out_shape=jax.ShapeDtypeStruct((t, D_MODEL), F32),
        compiler_params=pltpu.CompilerParams(dimension_semantics=("arbitrary",),
                                             vmem_limit_bytes=VMEM_LIMIT),
        name="outproj_mlp",
    )(x2, attn, ssm, z, gat, gn, wo, gm, wup, wdn)


def _rotate_half(w):
    half = QK_ROPE_DIM // 2
    return jnp.concatenate([-w[..., half:], w[..., :half]], axis=-1)


def _swap_halves(g):
    half = QK_ROPE_DIM // 2
    return jnp.concatenate([g[..., half:], g[..., :half]], axis=-1)


def _lane_bcast(v):
    return jnp.broadcast_to(v[..., None], v.shape + (LANES,)).astype(F32)


def _layer(x2, cs, batch, seq, ln_mix_g, w_in, q_a_norm_g, w_uq, kv_a_norm_g, w_ukv, q_norm_g,
           k_norm_g, attn_out_norm_g, conv_w, conv_b, a_log_fwd, a_log_bwd, dt_bias_fwd, dt_bias_bwd,
           d_skip, ssm_norm_g, w_out, ln_mlp_g, w_mlp_up, w_mlp_down):
    win = _win_prep_call(*w_in)
    wq = w_uq.reshape(Q_LORA_RANK, ATTN_HEADS, QK_HEAD_DIM)
    wuq = jnp.concatenate([wq, _rotate_half(wq[..., QK_NOPE_DIM:])], axis=-1)
    wuq = wuq.reshape(Q_LORA_RANK, ATTN_HEADS * HEAD_PAD).astype(BF16)
    wukv = w_ukv.reshape(KV_LORA_RANK, ATTN_HEADS, QK_NOPE_DIM + V_HEAD_DIM)
    wuk = jnp.pad(wukv[:, :, :QK_NOPE_DIM],
                  ((0, 0), (0, 0), (0, HEAD_PAD - QK_NOPE_DIM))).reshape(KV_LORA_RANK, -1).astype(BF16)
    wv = wukv[:, :, QK_NOPE_DIM:].reshape(KV_LORA_RANK, ATTN_HEADS // 2, 2, V_HEAD_DIM)
    zv = jnp.zeros_like(wv[:, :, 0])
    wuv = jnp.stack([jnp.concatenate([wv[:, :, 0], zv], axis=-1),
                     jnp.concatenate([zv, wv[:, :, 1]], axis=-1)], axis=2).reshape(KV_LORA_RANK, -1).astype(BF16)
    ones_v = np.ones((ATTN_HEADS // 2, V_HEAD_DIM), np.float32)
    vone = jnp.asarray(np.stack([np.concatenate([0 * ones_v, ones_v], axis=-1),
                                 np.concatenate([ones_v, 0 * ones_v], axis=-1)], axis=1).reshape(1, -1))
    scale = QK_HEAD_DIM ** -0.5 * np.log2(np.e)
    zeros_nope = jnp.zeros((QK_NOPE_DIM,), k_norm_g.dtype)
    k_rope_g = k_norm_g[QK_NOPE_DIM:]
    hg = jnp.concatenate([q_norm_g * scale, _swap_halves(q_norm_g[QK_NOPE_DIM:]) * scale,
                          k_norm_g[:QK_NOPE_DIM], zeros_nope,
                          zeros_nope, k_rope_g, _swap_halves(k_rope_g),
                          jnp.zeros(((SUBLANES - 3) * HEAD_PAD,), k_norm_g.dtype)]).reshape(SUBLANES, HEAD_PAD)

    q, k, v, z, xbc, dtc = _inproj_call(
        x2, cs, ln_mix_g[None, :], win, q_a_norm_g[None, :], wuq, kv_a_norm_g[None, :], wuk, wuv, vone,
        hg)

    attn, (wo_b, wup_b, wdn_b) = _attn_call(q, k, v, (w_out, w_mlp_up, w_mlp_down), batch, seq)

    cw = jnp.pad(conv_w[:, 0, :], ((0, SUBLANES - SSM_CONV), (0, 0)))
    cbias = conv_b[None, :]
    adt = jnp.stack([a_log_fwd, a_log_bwd, dt_bias_fwd, dt_bias_bwd]).reshape(2, 2, SSM_GROUPS, HEADS_PER_GROUP)
    adt = _lane_bcast(adt.transpose(0, 2, 1, 3).reshape(2, 2 * SSM_HEADS))
    dskip = jnp.repeat(d_skip, SSM_HEAD_DIM)[None, :]
    ssm = _ssd_call(xbc, dtc, cw, cbias, adt, dskip, batch, seq)

    return _mlp_call(x2, attn, ssm, z, attn_out_norm_g[None, :], ssm_norm_g[None, :], wo_b,
                     ln_mlp_g[None, :], wup_b, wdn_b)


def _rope_table(positions):
    inv_freq = 1.0 / (ROPE_THETA ** (jnp.arange(0, QK_ROPE_DIM, 2, dtype=F32) / QK_ROPE_DIM))
    ang = inv_freq[:, None] * positions.astype(F32).reshape(1, -1)
    return jnp.concatenate([jnp.cos(ang), jnp.sin(ang)], axis=0)


def kernel(x, positions, ln_mix_g, w_in, q_a_norm_g, w_uq, kv_a_norm_g, w_ukv, q_norm_g, k_norm_g,
           attn_out_norm_g, conv_w, conv_b, a_log_fwd, a_log_bwd, dt_bias_fwd, dt_bias_bwd, d_skip,
           ssm_norm_g, w_out, ln_mlp_g, w_mlp_up, w_mlp_down):
    batch, seq, d = x.shape
    assert d == D_MODEL and seq % TQ == 0 and (batch * seq) % TM_IN == 0 and (batch * seq) % TM_MLP == 0
    cs = _rope_table(positions)
    x2 = x.reshape(batch * seq, d)
    for l in range(ln_mix_g.shape[0]):
        x2 = _layer(x2, cs, batch, seq, ln_mix_g[l], (w_in, l), q_a_norm_g[l], w_uq[l], kv_a_norm_g[l],
                    w_ukv[l], q_norm_g[l], k_norm_g[l], attn_out_norm_g[l], conv_w[l], conv_b[l],
                    a_log_fwd[l], a_log_bwd[l], dt_bias_fwd[l], dt_bias_bwd[l], d_skip[l], ssm_norm_g[l],
                    w_out[l], ln_mlp_g[l], w_mlp_up[l], w_mlp_down[l])
    return x2.reshape(batch, seq, d)
```

```python
import numpy as np
import jax
import jax.numpy as jnp
from jax import lax
from jax.experimental import pallas as pl
from jax.experimental.pallas import tpu as pltpu

F32 = jnp.float32
BF16 = jnp.bfloat16

D_MODEL = 1024
ATTN_HEADS = 8
QK_NOPE_DIM = 64
QK_ROPE_DIM = 32
QK_HEAD_DIM = QK_NOPE_DIM + QK_ROPE_DIM
V_HEAD_DIM = 64
Q_LORA_RANK = D_MODEL // 4
KV_LORA_RANK = D_MODEL // 8
ROPE_THETA = 10000.0
ATTN_WIDTH = ATTN_HEADS * V_HEAD_DIM
SSM_HEADS = 8
SSM_HEAD_DIM = 64
SSM_INNER = SSM_HEADS * SSM_HEAD_DIM
SSM_GROUPS = 2
SSM_STATE = 128
SSM_CONV = 5
SSM_CHUNK = 128
SSM_CONV_CH = SSM_INNER + 2 * SSM_GROUPS * SSM_STATE
D_MIX = ATTN_WIDTH + SSM_INNER
D_FF = 4 * D_MODEL
EPS = 1e-6

LANES = 128
SUBLANES = 8
HEAD_PAD = LANES

HEADS_PER_GROUP = SSM_HEADS // SSM_GROUPS
GROUP_INNER = SSM_INNER // SSM_GROUPS
GROUP_COLS = GROUP_INNER + 2 * SSM_STATE
DT_ROWS = 2 * HEADS_PER_GROUP
BF16_ROWS = 16
SEG_K_PER_ROW = 8
SEG_K = DT_ROWS * SEG_K_PER_ROW
SEG_BLOCKS = DT_ROWS + HEADS_PER_GROUP
LOG2E = float(np.log2(np.e))
CONV_WIN = 2 * SSM_CHUNK
CONV_WIN_LEAD = SSM_CHUNK // 2
CONV_SHIFTED_TAPS = tuple(k for k in range(SSM_CONV) if k != SSM_CONV // 2)

COL_CKV = Q_LORA_RANK
COL_MISC = COL_CKV + KV_LORA_RANK
COL_Z = COL_MISC + LANES
COL_XBC = COL_Z + SSM_INNER
IN_COLS = COL_XBC + SSM_CONV_CH

TM_IN = 1024
IN_SUBTILE = 512
TQ = 1024
ATTN_HEADS_PER_STEP = 8
TM_MLP = 1024
MLP_SUBTILE = 512
FF_CHUNK = 1024
SSD_UNROLL = 8
CONV_AHEAD = 4
PASS_A_UNROLL = 4
WIN_PREP_STEPS = 4
VMEM_LIMIT = 56 * 1024 * 1024


def _inv_rms(x):
    n = x.shape[-1]
    x2 = x * x
    acc = x2[:, 0:LANES]
    for i in range(1, n // LANES):
        acc = acc + x2[:, i * LANES:(i + 1) * LANES]
    return lax.rsqrt(jnp.sum(acc, axis=-1, keepdims=True) * (1.0 / n) + EPS)


def _dot(a, b):
    return jnp.dot(a, b, preferred_element_type=F32)


def _dot_nt(a, b):
    return lax.dot_general(a, b, (((1,), (1,)), ((), ())), preferred_element_type=F32)


def _inproj_kernel(x_ref, cs_ref, g_ref, win_ref, gqa_ref, wuq_ref, gkva_ref, wuk_ref, wuv_ref, vone_ref,
                   hg_ref,
                   q_ref, k_ref, v_ref, z_ref, xbc_ref, dt_ref):
    for sub in range(x_ref.shape[0] // IN_SUBTILE):
        _inproj_rows(sub * IN_SUBTILE, x_ref, cs_ref, g_ref, win_ref, gqa_ref, wuq_ref, gkva_ref, wuk_ref,
                     wuv_ref, vone_ref, hg_ref, q_ref, k_ref, v_ref, z_ref, xbc_ref, dt_ref)


def _inproj_rows(r0, x_ref, cs_ref, g_ref, win_ref, gqa_ref, wuq_ref, gkva_ref, wuk_ref, wuv_ref, vone_ref,
                 hg_ref, q_ref, k_ref, v_ref, z_ref, xbc_ref, dt_ref):
    tm = IN_SUBTILE
    rs = slice(r0, r0 + tm)
    x = x_ref[rs, :]
    h = (x * _inv_rms(x) * g_ref[...]).astype(BF16)
    big = _dot(h, win_ref[...])
    z_ref[rs, :] = big[:, COL_Z:COL_XBC].astype(BF16)
    xbc_ref[rs, :] = big[:, COL_XBC:IN_COLS].astype(BF16)
    misc = big[:, COL_MISC:COL_Z]
    for c in range(tm // SSM_CHUNK):
        dt_ref[r0 // SSM_CHUNK + c] = misc[c * SSM_CHUNK:(c + 1) * SSM_CHUNK, :].T[0:2 * SSM_HEADS, :]

    cq = big[:, 0:COL_CKV]
    ckv = big[:, COL_CKV:COL_MISC]
    cqn = (cq * _inv_rms(cq) * gqa_ref[...]).astype(BF16)
    ckvn = (ckv * _inv_rms(ckv) * gkva_ref[...]).astype(BF16)
    q_pre = _dot(cqn, wuq_ref[...])
    k_pre = _dot(ckvn, wuk_ref[...])
    v_ref[rs, :] = (_dot(ckvn, wuv_ref[...]) + vone_ref[...]).astype(BF16)

    half = QK_ROPE_DIM // 2
    zpad = jnp.zeros((HEAD_PAD - QK_ROPE_DIM, SSM_CHUNK), F32)
    tt = jnp.concatenate(
        [jnp.concatenate([cs_ref[:, r0 + c * SSM_CHUNK:r0 + (c + 1) * SSM_CHUNK], zpad], axis=0).T
         for c in range(tm // SSM_CHUNK)], axis=0)
    lane_t = lax.broadcasted_iota(jnp.int32, (tm, HEAD_PAD), 1)
    cs = jnp.where(lane_t < QK_NOPE_DIM, 1.0,
                   jnp.where(lane_t < QK_NOPE_DIM + half, pltpu.roll(tt, QK_NOPE_DIM, 1),
                             jnp.where(lane_t < QK_HEAD_DIM + half, pltpu.roll(tt, QK_NOPE_DIM + half, 1),
                                       pltpu.roll(tt, QK_HEAD_DIM, 1))))
    lane = lax.broadcasted_iota(jnp.int32, (1, HEAD_PAD), 1)
    in_head = (lane < QK_HEAD_DIM).astype(F32)
    is_rope = ((lane >= QK_NOPE_DIM) & (lane < QK_HEAD_DIM)).astype(F32)
    inv_d = 1.0 / QK_HEAD_DIM

    gcq = hg_ref[0:1, :] * cs
    for hh in range(ATTN_HEADS):
        sl = slice(hh * HEAD_PAD, (hh + 1) * HEAD_PAD)
        qh = q_pre[:, sl]
        ssq = jnp.sum(qh * qh * in_head, axis=-1, keepdims=True)
        q_ref[rs, sl] = (qh * lax.rsqrt(ssq * inv_d + EPS) * gcq).astype(BF16)

    ab = misc * (hg_ref[2:3, :] * cs)
    lane2 = lax.broadcasted_iota(jnp.int32, (tm, HEAD_PAD), 1)
    swapped = jnp.where(lane2 < QK_HEAD_DIM, pltpu.roll(ab, HEAD_PAD - QK_ROPE_DIM, 1),
                        pltpu.roll(ab, QK_ROPE_DIM, 1))
    s_both = jnp.where(lane2 >= QK_NOPE_DIM, ab + swapped, 0.0)
    ssq_pe = jnp.sum(misc * misc * is_rope, axis=-1, keepdims=True)
    gkn = hg_ref[1:2, :]
    for hh in range(ATTN_HEADS):
        sl = slice(hh * HEAD_PAD, (hh + 1) * HEAD_PAD)
        kh = k_pre[:, sl]
        ssq = jnp.sum(kh * kh, axis=-1, keepdims=True) + ssq_pe
        k_ref[rs, sl] = ((kh * gkn + s_both) * lax.rsqrt(ssq * inv_d + EPS)).astype(BF16)


def _win_prep_kernel(w_ref, wq_ref, wkv_ref, o_ref, wuq_ref, wuk_ref, wuv_ref):
    half = QK_ROPE_DIM // 2

    @pl.when(pl.program_id(0) == 0)
    def _up_projection_tiles():
        wq = wq_ref[...]
        q_pieces = []
        for hh in range(ATTN_HEADS):
            head = wq[:, hh * QK_HEAD_DIM:(hh + 1) * QK_HEAD_DIM]
            rope = head[:, QK_NOPE_DIM:]
            q_pieces += [head, -rope[:, half:], rope[:, :half]]
        wuq_ref[...] = jnp.concatenate(q_pieces, axis=1).astype(BF16)
        wkv = wkv_ref[...]
        zeros = jnp.zeros((wkv.shape[0], V_HEAD_DIM), F32)
        k_pieces, v_pieces = [], []
        for hh in range(ATTN_HEADS):
            nope = wkv[:, hh * HEAD_PAD:hh * HEAD_PAD + QK_NOPE_DIM]
            val = wkv[:, hh * HEAD_PAD + QK_NOPE_DIM:(hh + 1) * HEAD_PAD]
            k_pieces += [nope, zeros]
            v_pieces += [val, zeros] if hh % 2 == 0 else [zeros, val]
        wuk_ref[...] = jnp.concatenate(k_pieces, axis=1).astype(BF16)
        wuv_ref[...] = jnp.concatenate(v_pieces, axis=1).astype(BF16)

    w = w_ref[...].astype(F32)
    o_kpe = Q_LORA_RANK + KV_LORA_RANK
    o_z = o_kpe + QK_ROPE_DIM
    o_dt = o_z + SSM_INNER + SSM_CONV_CH
    hpg = HEADS_PER_GROUP
    dt_f = w[:, o_dt:o_dt + SSM_HEADS]
    dt_b = w[:, o_dt + SSM_HEADS:o_dt + 2 * SSM_HEADS]
    kpe = w[:, o_kpe:o_z]
    pieces = [w[:, 0:o_kpe]]
    for g in range(SSM_GROUPS):
        pieces += [dt_f[:, g * hpg:(g + 1) * hpg], dt_b[:, g * hpg:(g + 1) * hpg]]
    pieces += [jnp.zeros((w.shape[0], QK_NOPE_DIM - 2 * SSM_HEADS), F32), kpe, -kpe[:, half:], kpe[:, :half],
               w[:, o_z:o_dt]]
    o_ref[...] = jnp.concatenate(pieces, axis=1).astype(BF16)


def _win_prep_call(w_in3, w_uq3, w_ukv3, layer):
    depth, d, n = w_in3.shape
    rows = d // WIN_PREP_STEPS
    whole = lambda a: pl.BlockSpec(a.shape[1:], lambda i: (layer, 0))
    head_cols = ATTN_HEADS * HEAD_PAD
    return pl.pallas_call(
        _win_prep_kernel,
        grid=(WIN_PREP_STEPS,),
        in_specs=[pl.BlockSpec((rows, n), lambda i: (layer * WIN_PREP_STEPS + i, 0)), whole(w_uq3), whole(w_ukv3)],
        out_specs=[pl.BlockSpec((rows, IN_COLS), lambda i: (i, 0)),
                   pl.BlockSpec((Q_LORA_RANK, head_cols), lambda i: (0, 0)),
                   pl.BlockSpec((KV_LORA_RANK, head_cols), lambda i: (0, 0)),
                   pl.BlockSpec((KV_LORA_RANK, head_cols), lambda i: (0, 0))],
        out_shape=[jax.ShapeDtypeStruct((d, IN_COLS), BF16),
                   jax.ShapeDtypeStruct((Q_LORA_RANK, head_cols), BF16),
                   jax.ShapeDtypeStruct((KV_LORA_RANK, head_cols), BF16),
                   jax.ShapeDtypeStruct((KV_LORA_RANK, head_cols), BF16)],
        compiler_params=pltpu.CompilerParams(dimension_semantics=("arbitrary",), vmem_limit_bytes=VMEM_LIMIT),
        name="win_prep",
    )(w_in3.reshape(depth * d, n).astype(BF16),
      w_uq3.reshape(depth * w_uq3.shape[1], w_uq3.shape[2]),
      w_ukv3.reshape(depth * w_ukv3.shape[1], w_ukv3.shape[2]))


def _inproj_call(x2, cs, g, win, gqa, wuq, gkva, wuk, wuv, vone, hg):
    t = x2.shape[0]
    cpt = TM_IN // SSM_CHUNK
    full = lambda a: pl.BlockSpec(a.shape, lambda i: (0,) * a.ndim, pipeline_mode=pl.Buffered(1))
    row = lambda w: pl.BlockSpec((TM_IN, w), lambda i: (i, 0))
    return pl.pallas_call(
        _inproj_kernel,
        grid=(t // TM_IN,),
        in_specs=[row(D_MODEL), pl.BlockSpec((QK_ROPE_DIM, TM_IN), lambda i: (0, i)), full(g), full(win), full(gqa), full(wuq), full(gkva),
                  full(wuk), full(wuv), full(vone), full(hg)],
        out_specs=[row(ATTN_HEADS * HEAD_PAD), row(ATTN_HEADS * HEAD_PAD), row(ATTN_HEADS * HEAD_PAD),
                   row(SSM_INNER), row(SSM_CONV_CH),
                   pl.BlockSpec((cpt, 2 * SSM_HEADS, SSM_CHUNK), lambda i: (i, 0, 0))],
        out_shape=[jax.ShapeDtypeStruct((t, ATTN_HEADS * HEAD_PAD), BF16),
                   jax.ShapeDtypeStruct((t, ATTN_HEADS * HEAD_PAD), BF16),
                   jax.ShapeDtypeStruct((t, ATTN_HEADS * HEAD_PAD), BF16),
                   jax.ShapeDtypeStruct((t, SSM_INNER), BF16),
                   jax.ShapeDtypeStruct((t, SSM_CONV_CH), BF16),
                   jax.ShapeDtypeStruct((t // SSM_CHUNK, 2 * SSM_HEADS, SSM_CHUNK), F32)],
        compiler_params=pltpu.CompilerParams(dimension_semantics=("arbitrary",),
                                             vmem_limit_bytes=VMEM_LIMIT),
        name="inproj",
    )(x2, cs, g, win, gqa, wuq, gkva, wuk, wuv, vone, hg)


def _attn_kernel(q_ref, k_ref, v_ref, *refs):
    n_w = (len(refs) - 1) // 2
    w_in_refs, o_ref, w_out_refs = refs[:n_w], refs[n_w], refs[n_w + 1:]
    for w_in_ref, w_out_ref in zip(w_in_refs, w_out_refs):
        w_out_ref[...] = w_in_ref[...].astype(BF16)

    lane = lax.broadcasted_iota(jnp.int32, (q_ref.shape[0], HEAD_PAD), 1)
    for jp in range(ATTN_HEADS_PER_STEP // 2):
        sls = [slice(j * HEAD_PAD, (j + 1) * HEAD_PAD) for j in (2 * jp, 2 * jp + 1)]
        ss = [_dot_nt(q_ref[:, sl], k_ref[:, sl]) for sl in sls]
        ps = [jnp.exp2(s - jnp.max(s, axis=-1, keepdims=True)).astype(BF16) for s in ss]
        accs = [_dot(p, v_ref[:, sl]) for p, sl in zip(ps, sls)]
        res = [acc / pltpu.roll(acc, V_HEAD_DIM, 1) for acc in accs]
        o_ref[:, jp * HEAD_PAD:(jp + 1) * HEAD_PAD] = jnp.where(lane < V_HEAD_DIM, res[0], res[1])


def _attn_call(q, k, v, later_weights, batch, seq):
    n_q = seq // TQ
    hps = ATTN_HEADS_PER_STEP
    assert ATTN_HEADS == hps
    n_steps = batch * n_q

    def rows_spec(w):
        assert w.shape[0] % (n_steps * BF16_ROWS) == 0
        return pl.BlockSpec((w.shape[0] // n_steps, w.shape[1]), lambda b, hp, i: (b * n_q + i, 0))
    w_specs = [rows_spec(w) for w in later_weights]
    outs = pl.pallas_call(
        _attn_kernel,
        grid=(batch, ATTN_HEADS // hps, n_q),
        in_specs=[pl.BlockSpec((TQ, hps * HEAD_PAD), lambda b, hp, i: (b * n_q + i, hp)),
                  pl.BlockSpec((seq, hps * HEAD_PAD), lambda b, hp, i: (b, hp)),
                  pl.BlockSpec((seq, hps * HEAD_PAD), lambda b, hp, i: (b, hp))] + w_specs,
        out_specs=[pl.BlockSpec((TQ, hps * V_HEAD_DIM), lambda b, hp, i: (b * n_q + i, hp))] + w_specs,
        out_shape=[jax.ShapeDtypeStruct((batch * seq, ATTN_WIDTH), F32)]
                  + [jax.ShapeDtypeStruct(w.shape, BF16) for w in later_weights],
        compiler_params=pltpu.CompilerParams(
            dimension_semantics=("arbitrary", "arbitrary", "arbitrary"),
            vmem_limit_bytes=VMEM_LIMIT),
        name="attention",
    )(q, k, v, *later_weights)
    return outs[0], outs[1:]


def _split3(x):
    hi = x.astype(BF16)
    r1 = x - hi.astype(F32)
    mid = r1.astype(BF16)
    lo = (r1 - mid.astype(F32)).astype(BF16)
    return hi, mid, lo


def _ssd_kernel(x_ref, b_ref, c_ref, dt_ref, sh_ref, place_ref, lhs_ones_ref, rhs_const_ref,
                cwx_ref, cwb_ref, cwc_ref, cbx_ref, cbb_ref, cbc_ref, adt_ref, dskip_ref,
                o_ref, xc_scr, xm_scr, nsf_scr, nsb_scr, dtv_scr, w_scr, dec_scr, lhs_scr, rhs_scr, rpart_scr):
    nc = dt_ref.shape[0]
    seq = x_ref.shape[0]
    L = SSM_CHUNK
    hpg = HEADS_PER_GROUP
    npair = hpg // 2
    P = SSM_HEAD_DIM
    x_cols = slice(0, GROUP_INNER)
    b_cols = slice(GROUP_INNER, GROUP_INNER + SSM_STATE)
    c_cols = slice(GROUP_INNER + SSM_STATE, GROUP_COLS)
    pair_cols = [slice(j * L, (j + 1) * L) for j in range(npair)]

    @pl.when((pl.program_id(0) == 0) & (pl.program_id(1) == 0))
    def _init_rhs():
        for u in range(SSD_UNROLL):
            rhs_scr[u] = rhs_const_ref[...]

    a_neg = -jnp.exp(adt_ref[0]) * LOG2E
    dtv = jax.nn.softplus(dt_ref[...] + adt_ref[1][None])
    dtv_scr[...] = dtv
    da2 = (dtv * a_neg[None]).reshape(nc * DT_ROWS, L)
    ri = lax.broadcasted_iota(jnp.int32, (L, L), 0)
    ci = lax.broadcasted_iota(jnp.int32, (L, L), 1)
    upper = (ri <= ci).astype(BF16)
    lower = (ri >= ci).astype(BF16)
    tri = jnp.concatenate([upper, lower], axis=1)
    cs_fb = sum(_dot(p, tri) for p in _split3(da2))
    rowsel = (ri & hpg) == 0
    cs2 = jnp.where(rowsel, cs_fb[:, 0:L], cs_fb[:, L:2 * L])
    colcs = cs2.T

    col_parts = jnp.concatenate(_split3(colcs), axis=1)
    half_n = place_ref.shape[1] // 2
    for hf in range(2):
        cols = slice(hf * half_n, (hf + 1) * half_n)
        lhs_half = _dot(col_parts, place_ref[:, cols]) + lhs_ones_ref[:, cols]
        for c in range(nc // 2):
            lhs_scr[hf * (nc // 2) + c] = lhs_half[:, c * L:c * L + SEG_K].astype(BF16)
    for i, part in enumerate(_split3(-cs2)):
        rpart_scr[i] = part.astype(F32).reshape(nc, DT_ROWS, L)

    def lane_bcast(col):
        return jnp.broadcast_to(col, (col.shape[0], L))

    tot2 = jnp.where(rowsel, lane_bcast(cs2[:, L - 1:L]), lane_bcast(cs2[:, 0:1]))
    dec_scr[...] = jnp.exp2(tot2).reshape(nc, DT_ROWS, L)
    w_scr[...] = (dtv.reshape(nc * DT_ROWS, L) * jnp.exp2(tot2 - cs2)).reshape(nc, DT_ROWS, L)

    lane_t = lax.broadcasted_iota(jnp.int32, (L, L), 1)
    low_half = lane_t < P

    cw_halves = (cwx_ref[...], jnp.concatenate([cwb_ref[...], cwc_ref[...]], axis=1))
    cb_halves = (cbx_ref[...], jnp.concatenate([cbb_ref[...], cbc_ref[...]], axis=1))

    def rows(hf, start, size):
        if hf == 0:
            return x_ref[pl.ds(start, size), :]
        return jnp.concatenate([b_ref[pl.ds(start, size), :], c_ref[pl.ds(start, size), :]], axis=1)

    def conv_stage(c):
        base = pl.multiple_of(c * L, L)
        ws = pl.multiple_of(jnp.clip(c * L - CONV_WIN_LEAD, 0, seq - CONV_WIN), CONV_WIN_LEAD)
        variant = jnp.where(c == 0, 0, jnp.where(c == nc - 1, 2, 1))
        sh = sh_ref[variant]
        halves = []
        for hf in range(2):
            cw = cw_halves[hf]
            shifted = _dot(sh, rows(hf, ws, CONV_WIN))
            acc = cb_halves[hf] + cw[SSM_CONV // 2:SSM_CONV // 2 + 1, :] * rows(hf, base, L).astype(F32)
            for t, kk in enumerate(CONV_SHIFTED_TAPS):
                acc = acc + cw[kk:kk + 1, :] * shifted[t * L:(t + 1) * L, :]
            halves.append(acc * jax.nn.sigmoid(acc))
        xc = jnp.concatenate(halves, axis=1)
        xc_scr[c] = xc
        for j in range(npair):
            xp = xc[:, pair_cols[j]]
            xm_scr[c, j] = jnp.concatenate([jnp.where(low_half, xp, 0.0), jnp.where(low_half, 0.0, xp)],
                                           axis=0).astype(BF16)

    def state_loads(c):
        return xc_scr[c, :, b_cols], [xm_scr[c, j] for j in range(npair)]

    def state_stage(c, loaded):
        b_tok, xm = loaded
        bt = b_tok.T
        for d, ns_scr in ((0, nsf_scr), (1, nsb_scr)):
            w = w_scr[c, d * hpg:(d + 1) * hpg, :]
            ns_scr[c] = jnp.concatenate(
                [_dot(jnp.concatenate([(bt * w[2 * j:2 * j + 1, :]).astype(BF16),
                                       (bt * w[2 * j + 1:2 * j + 2, :]).astype(BF16)], axis=1), xm[j])
                 for j in range(npair)], axis=1)

    for c in range(CONV_AHEAD):
        conv_stage(c)

    def pass_a(c, carry):
        loaded = state_loads(c)
        conv_stage(c + CONV_AHEAD)
        state_stage(c, loaded)
        return carry
    lax.fori_loop(0, nc - CONV_AHEAD, pass_a, 0, unroll=PASS_A_UNROLL)
    for c in range(nc - CONV_AHEAD, nc):
        state_stage(c, state_loads(c))

    lane_r = lax.broadcasted_iota(jnp.int32, (1, L), 1)

    def decay_row(c, d):
        dec = dec_scr[c, d * hpg:(d + 1) * hpg, :]
        return jnp.concatenate([jnp.where(lane_r < P, dec[2 * j:2 * j + 1, :], dec[2 * j + 1:2 * j + 2, :])
                                for j in range(npair)], axis=1)

    def pass_b(d, ns_scr):
        def body(i, st):
            c = i if d == 0 else nc - 1 - i
            new = ns_scr[c]
            ns_scr[c] = st
            return st * decay_row(c, d) + new
        lax.fori_loop(0, nc, body, jnp.zeros((SSM_STATE, GROUP_INNER), F32))
    pass_b(0, nsf_scr)
    pass_b(1, nsb_scr)

    mask_f = ci <= ri
    mask_b = ci >= ri
    neg_inf = jnp.float32(-jnp.inf)
    sub16 = lax.broadcasted_iota(jnp.int32, (BF16_ROWS, L), 0)

    def pass_c(c, carry):
        base = pl.multiple_of(c * L, L)
        xc = xc_scr[c]
        bm = xc[:, b_cols].astype(BF16)
        cm = xc[:, c_cols].astype(BF16)
        cb = _dot_nt(cm, bm)
        off_f = _dot(cm, nsf_scr[c].astype(BF16))
        off_b = _dot(cm, nsb_scr[c].astype(BF16))
        slot = c % SSD_UNROLL
        for hd in range(DT_ROWS):
            r0 = (hd % 2) * SEG_K_PER_ROW
            blk = jnp.where((sub16 >= r0) & (sub16 < r0 + 3), 1.0, 0.0)
            for i in range(3):
                blk = jnp.where(sub16 == r0 + 3 + i, rpart_scr[i, c, hd:hd + 1, :], blk)
            rhs_scr[slot, (hd // 2) * BF16_ROWS:(hd // 2 + 1) * BF16_ROWS, hd * L:(hd + 1) * L] = blk.astype(BF16)
        seg = _dot(lhs_scr[c], rhs_scr[slot])
        dt_f_r = dtv_scr[c, 0:hpg, :]
        dt_b_r = dtv_scr[c, hpg:DT_ROWS, :]
        ys = []
        for j in range(npair):
            ws_ = []
            for h in (2 * j, 2 * j + 1):
                ef = jnp.exp2(jnp.where(mask_f, seg[:, h * L:(h + 1) * L], neg_inf))
                eb = jnp.exp2(jnp.where(mask_b, seg[:, (hpg + h) * L:(hpg + h + 1) * L], neg_inf))
                ws_.append((cb * (ef * dt_f_r[h:h + 1, :] + eb * dt_b_r[h:h + 1, :])).astype(BF16))
            scale_f = jnp.exp2(seg[:, (DT_ROWS + j) * L:(DT_ROWS + j + 1) * L])
            scale_b = jnp.exp2(seg[:, (DT_ROWS + npair + j) * L:(DT_ROWS + npair + j + 1) * L])
            ys.append(_dot(jnp.concatenate(ws_, axis=1), xm_scr[c, j])
                      + scale_f * off_f[:, pair_cols[j]] + scale_b * off_b[:, pair_cols[j]])
        y = jnp.concatenate(ys, axis=1) + dskip_ref[...] * xc[:, x_cols]
        o_ref[pl.ds(base, L), :] = y
        return carry
    lax.fori_loop(0, nc, pass_c, 0, unroll=SSD_UNROLL)


def _conv_shift_matrices():
    out = np.zeros((3, len(CONV_SHIFTED_TAPS) * SSM_CHUNK, CONV_WIN), np.float32)
    for v, lead in enumerate((0, CONV_WIN_LEAD, CONV_WIN - SSM_CHUNK)):
        for i, kk in enumerate(CONV_SHIFTED_TAPS):
            for t in range(SSM_CHUNK):
                j = t + kk - SSM_CONV // 2 + lead
                if 0 <= j < CONV_WIN:
                    out[v, i * SSM_CHUNK + t, j] = 1.0
    return out


def _seg_matmul_constants(nc):
    L, K, half = SSM_CHUNK, SEG_K_PER_ROW, SSM_HEAD_DIM
    place = np.zeros((3 * L, nc * L), np.float32)
    lhs_ones = np.zeros((1, nc * L), np.float32)
    for c in range(nc):
        for hd in range(DT_ROWS):
            for i in range(3):
                place[i * L + c * DT_ROWS + hd, c * L + hd * K + i] = 1.0
                lhs_ones[0, c * L + hd * K + 3 + i] = 1.0
    rhs = np.zeros((SEG_K, SEG_BLOCKS * L), np.float32)
    for hd in range(DT_ROWS):
        d, h = divmod(hd, HEADS_PER_GROUP)
        pair_block = DT_ROWS + d * (HEADS_PER_GROUP // 2) + h // 2
        lanes = slice(0, half) if h % 2 == 0 else slice(half, L)
        rhs[hd * K:hd * K + 3, hd * L:(hd + 1) * L] = 1.0
        rhs[hd * K:hd * K + 3, pair_block * L:(pair_block + 1) * L][:, lanes] = 1.0
    return place, lhs_ones, rhs


def _ssd_call(xbc, dtc, cw, cbias, adt, dskip, batch, seq):
    nc = seq // SSM_CHUNK
    L = SSM_CHUNK
    assert nc >= 3 and seq >= CONV_WIN and nc * DT_ROWS == L
    sh = jnp.asarray(_conv_shift_matrices(), BF16)
    place, lhs_ones, rhs_const = _seg_matmul_constants(nc)
    place = jnp.asarray(place, BF16)
    lhs_ones = jnp.asarray(lhs_ones, F32)
    rhs_const = jnp.asarray(rhs_const, BF16)
    const = lambda a: pl.BlockSpec(a.shape, lambda b, g: (0,) * a.ndim)
    b_blk0 = SSM_INNER // SSM_STATE
    c_blk0 = b_blk0 + SSM_GROUPS

    def xbc_views(rows_, batched):
        lead = (lambda b: b) if batched else (lambda b: 0)
        return [pl.BlockSpec((rows_, GROUP_INNER), lambda b, g: (lead(b), g)),
                pl.BlockSpec((rows_, SSM_STATE), lambda b, g: (lead(b), b_blk0 + g)),
                pl.BlockSpec((rows_, SSM_STATE), lambda b, g: (lead(b), c_blk0 + g))]
    return pl.pallas_call(
        _ssd_kernel,
        grid=(batch, SSM_GROUPS),
        in_specs=xbc_views(seq, True) + [
                  pl.BlockSpec((nc, DT_ROWS, L), lambda b, g: (b, g, 0)),
                  const(sh), const(place), const(lhs_ones), const(rhs_const)]
                 + xbc_views(SUBLANES, False) + xbc_views(1, False) + [
                  pl.BlockSpec((2, DT_ROWS, L), lambda b, g: (0, g, 0)),
                  pl.BlockSpec((1, GROUP_INNER), lambda b, g: (0, g))],
        out_specs=pl.BlockSpec((seq, GROUP_INNER), lambda b, g: (b, g)),
        out_shape=jax.ShapeDtypeStruct((batch * seq, SSM_INNER), F32),
        scratch_shapes=[pltpu.VMEM((nc, L, GROUP_COLS), F32),
                        pltpu.VMEM((nc, HEADS_PER_GROUP // 2, 2 * L, L), BF16),
                        pltpu.VMEM((nc, SSM_STATE, GROUP_INNER), F32),
                        pltpu.VMEM((nc, SSM_STATE, GROUP_INNER), F32),
                        pltpu.VMEM((nc, DT_ROWS, L), F32),
                        pltpu.VMEM((nc, DT_ROWS, L), F32),
                        pltpu.VMEM((nc, DT_ROWS, L), F32),
                        pltpu.VMEM((nc, L, SEG_K), BF16),
                        pltpu.VMEM((SSD_UNROLL, SEG_K, SEG_BLOCKS * L), BF16),
                        pltpu.VMEM((3, nc, DT_ROWS, L), F32)],
        compiler_params=pltpu.CompilerParams(dimension_semantics=("arbitrary", "arbitrary"),
                                             vmem_limit_bytes=VMEM_LIMIT),
        name="ssd",
    )(xbc, xbc, xbc, dtc, sh, place, lhs_ones, rhs_const, cw, cw, cw, cbias, cbias, cbias,
      adt, dskip)


def _mlp_kernel(x_ref, attn_ref, ssm_ref, z_ref, gat_ref, gn_ref, wo_ref, gm_ref, wup_ref, wdn_ref, o_ref):
    for sub in range(x_ref.shape[0] // MLP_SUBTILE):
        rs = slice(sub * MLP_SUBTILE, (sub + 1) * MLP_SUBTILE)
        a = attn_ref[rs, :]
        an = (a * _inv_rms(a) * gat_ref[...]).astype(BF16)
        zc = z_ref[rs, :].astype(F32)
        y = ssm_ref[rs, :] * (zc * jax.nn.sigmoid(zc))
        yn = jnp.concatenate([y[:, g * GROUP_INNER:(g + 1) * GROUP_INNER]
                              * _inv_rms(y[:, g * GROUP_INNER:(g + 1) * GROUP_INNER]) for g in range(SSM_GROUPS)],
                             axis=1) * gn_ref[...]
        mix = _dot(an, wo_ref[0:ATTN_WIDTH, :]) + _dot(yn.astype(BF16), wo_ref[ATTN_WIDTH:D_MIX, :])
        x1 = x_ref[rs, :] + mix
        hm = (x1 * _inv_rms(x1) * gm_ref[...]).astype(BF16)
        acc = jnp.zeros_like(x1)
        for c in range(D_FF // FF_CHUNK):
            cols = slice(c * FF_CHUNK, (c + 1) * FF_CHUNK)
            u = _dot(hm, wup_ref[:, cols])
            acc = acc + _dot(jnp.square(jnp.maximum(u, 0.0)).astype(BF16), wdn_ref[cols, :])
        o_ref[rs, :] = x1 + acc


def _mlp_call(x2, attn, ssm, z, gat, gn, wo, gm, wup, wdn):
    t = x2.shape[0]
    row = lambda w: pl.BlockSpec((TM_MLP, w), lambda i: (i, 0))
    full = lambda a: pl.BlockSpec(a.shape, lambda i: (0, 0), pipeline_mode=pl.Buffered(1))
    return pl.pallas_call(
        _mlp_kernel,
        grid=(t // TM_MLP,),
        in_specs=[row(D_MODEL), row(ATTN_WIDTH), row(SSM_INNER), row(SSM_INNER), full(gat), full(gn), full(wo),
                  full(gm), full(wup), full(wdn)],
        out_specs=row(D_MODEL),
        out_shape=jax.ShapeDtypeStruct((t, D_MODEL), F32),
        compiler_params=pltpu.CompilerParams(dimension_semantics=("arbitrary",),
                                             vmem_limit_bytes=VMEM_LIMIT),
        name="outproj_mlp",
    )(x2, attn, ssm, z, gat, gn, wo, gm, wup, wdn)


def _swap_halves(g):
    half = QK_ROPE_DIM // 2
    return jnp.concatenate([g[..., half:], g[..., :half]], axis=-1)


def _lane_bcast(v):
    return jnp.broadcast_to(v[..., None], v.shape + (LANES,)).astype(F32)


def _layer(x2, cs, batch, seq, ln_mix_g, stacked_proj_weights, q_a_norm_g, kv_a_norm_g, q_norm_g,
           k_norm_g, attn_out_norm_g, conv_w, conv_b, a_log_fwd, a_log_bwd, dt_bias_fwd, dt_bias_bwd,
           d_skip, ssm_norm_g, w_out, ln_mlp_g, w_mlp_up, w_mlp_down):
    win, wuq, wuk, wuv = _win_prep_call(*stacked_proj_weights)
    ones_v = np.ones((ATTN_HEADS // 2, V_HEAD_DIM), np.float32)
    vone = jnp.asarray(np.stack([np.concatenate([0 * ones_v, ones_v], axis=-1),
                                 np.concatenate([ones_v, 0 * ones_v], axis=-1)], axis=1).reshape(1, -1))
    scale = QK_HEAD_DIM ** -0.5 * np.log2(np.e)
    zeros_nope = jnp.zeros((QK_NOPE_DIM,), k_norm_g.dtype)
    k_rope_g = k_norm_g[QK_NOPE_DIM:]
    hg = jnp.concatenate([q_norm_g * scale, _swap_halves(q_norm_g[QK_NOPE_DIM:]) * scale,
                          k_norm_g[:QK_NOPE_DIM], zeros_nope,
                          zeros_nope, k_rope_g, _swap_halves(k_rope_g),
                          jnp.zeros(((SUBLANES - 3) * HEAD_PAD,), k_norm_g.dtype)]).reshape(SUBLANES, HEAD_PAD)

    q, k, v, z, xbc, dtc = _inproj_call(
        x2, cs, ln_mix_g[None, :], win, q_a_norm_g[None, :], wuq, kv_a_norm_g[None, :], wuk, wuv, vone,
        hg)

    attn, (wo_b, wup_b, wdn_b) = _attn_call(q, k, v, (w_out, w_mlp_up, w_mlp_down), batch, seq)

    cw = jnp.pad(conv_w[:, 0, :], ((0, SUBLANES - SSM_CONV), (0, 0)))
    cbias = conv_b[None, :]
    adt = jnp.stack([a_log_fwd, a_log_bwd, dt_bias_fwd, dt_bias_bwd]).reshape(2, 2, SSM_GROUPS, HEADS_PER_GROUP)
    adt = _lane_bcast(adt.transpose(0, 2, 1, 3).reshape(2, 2 * SSM_HEADS))
    dskip = jnp.repeat(d_skip, SSM_HEAD_DIM)[None, :]
    ssm = _ssd_call(xbc, dtc, cw, cbias, adt, dskip, batch, seq)

    return _mlp_call(x2, attn, ssm, z, attn_out_norm_g[None, :], ssm_norm_g[None, :], wo_b,
                     ln_mlp_g[None, :], wup_b, wdn_b)


def _rope_table(positions):
    inv_freq = 1.0 / (ROPE_THETA ** (jnp.arange(0, QK_ROPE_DIM, 2, dtype=F32) / QK_ROPE_DIM))
    ang = inv_freq[:, None] * positions.astype(F32).reshape(1, -1)
    return jnp.concatenate([jnp.cos(ang), jnp.sin(ang)], axis=0)


def kernel(x, positions, ln_mix_g, w_in, q_a_norm_g, w_uq, kv_a_norm_g, w_ukv, q_norm_g, k_norm_g,
           attn_out_norm_g, conv_w, conv_b, a_log_fwd, a_log_bwd, dt_bias_fwd, dt_bias_bwd, d_skip,
           ssm_norm_g, w_out, ln_mlp_g, w_mlp_up, w_mlp_down):
    batch, seq, d = x.shape
    assert d == D_MODEL and seq % TQ == 0 and (batch * seq) % TM_IN == 0 and (batch * seq) % TM_MLP == 0
    cs = _rope_table(positions)
    x2 = x.reshape(batch * seq, d)
    for l in range(ln_mix_g.shape[0]):
        x2 = _layer(x2, cs, batch, seq, ln_mix_g[l], (w_in, w_uq, w_ukv, l), q_a_norm_g[l], kv_a_norm_g[l],
                    q_norm_g[l], k_norm_g[l], attn_out_norm_g[l], conv_w[l], conv_b[l],
                    a_log_fwd[l], a_log_bwd[l], dt_bias_fwd[l], dt_bias_bwd[l], d_skip[l], ssm_norm_g[l],
                    w_out[l], ln_mlp_g[l], w_mlp_up[l], w_mlp_down[l])
    return x2.reshape(batch, seq, d)
```

```python
import numpy as np
import jax
import jax.numpy as jnp
from jax import lax
from jax.experimental import pallas as pl
from jax.experimental.pallas import tpu as pltpu

F32 = jnp.float32
BF16 = jnp.bfloat16

D_MODEL = 1024
ATTN_HEADS = 8
QK_NOPE_DIM = 64
QK_ROPE_DIM = 32
QK_HEAD_DIM = QK_NOPE_DIM + QK_ROPE_DIM
V_HEAD_DIM = 64
Q_LORA_RANK = D_MODEL // 4
KV_LORA_RANK = D_MODEL // 8
ROPE_THETA = 10000.0
ATTN_WIDTH = ATTN_HEADS * V_HEAD_DIM
SSM_HEADS = 8
SSM_HEAD_DIM = 64
SSM_INNER = SSM_HEADS * SSM_HEAD_DIM
SSM_GROUPS = 2
SSM_STATE = 128
SSM_CONV = 5
SSM_CHUNK = 128
SSM_CONV_CH = SSM_INNER + 2 * SSM_GROUPS * SSM_STATE
D_MIX = ATTN_WIDTH + SSM_INNER
D_FF = 4 * D_MODEL
EPS = 1e-6

LANES = 128
SUBLANES = 8
HEAD_PAD = LANES

HEADS_PER_GROUP = SSM_HEADS // SSM_GROUPS
GROUP_INNER = SSM_INNER // SSM_GROUPS
GROUP_COLS = GROUP_INNER + 2 * SSM_STATE
DT_ROWS = 2 * HEADS_PER_GROUP
BF16_ROWS = 16
SEG_K_PER_ROW = 8
SEG_K = DT_ROWS * SEG_K_PER_ROW
SEG_BLOCKS = DT_ROWS + HEADS_PER_GROUP
LOG2E = float(np.log2(np.e))
CONV_WIN = 2 * SSM_CHUNK
CONV_WIN_LEAD = SSM_CHUNK // 2
CONV_SHIFTED_TAPS = tuple(k for k in range(SSM_CONV) if k != SSM_CONV // 2)

COL_CKV = Q_LORA_RANK
COL_MISC = COL_CKV + KV_LORA_RANK
COL_Z = COL_MISC + LANES
COL_XBC = COL_Z + SSM_INNER
IN_COLS = COL_XBC + SSM_CONV_CH

TM_IN = 1024
IN_SUBTILE = 512
TQ = 1024
ATTN_HEADS_PER_STEP = 8
TM_MLP = 1024
MLP_SUBTILE = 512
FF_CHUNK = 1024
SSD_UNROLL = 8
CONV_AHEAD = 4
PASS_A_UNROLL = 4
WIN_PREP_STEPS = 4
VMEM_LIMIT = 56 * 1024 * 1024


def _inv_rms(x):
    n = x.shape[-1]
    x2 = x * x
    acc = x2[:, 0:LANES]
    for i in range(1, n // LANES):
        acc = acc + x2[:, i * LANES:(i + 1) * LANES]
    return lax.rsqrt(jnp.sum(acc, axis=-1, keepdims=True) * (1.0 / n) + EPS)


def _dot(a, b):
    return jnp.dot(a, b, preferred_element_type=F32)


def _dot_nt(a, b):
    return lax.dot_general(a, b, (((1,), (1,)), ((), ())), preferred_element_type=F32)


def _inproj_kernel(x_ref, cs_ref, g_ref, win_ref, gqa_ref, wuq_ref, gkva_ref, wuk_ref, wuv_ref, vone_ref,
                   hg_ref,
                   q_ref, k_ref, v_ref, z_ref, xbc_ref, dt_ref):
    for sub in range(x_ref.shape[0] // IN_SUBTILE):
        _inproj_rows(sub * IN_SUBTILE, x_ref, cs_ref, g_ref, win_ref, gqa_ref, wuq_ref, gkva_ref, wuk_ref,
                     wuv_ref, vone_ref, hg_ref, q_ref, k_ref, v_ref, z_ref, xbc_ref, dt_ref)


def _inproj_rows(r0, x_ref, cs_ref, g_ref, win_ref, gqa_ref, wuq_ref, gkva_ref, wuk_ref, wuv_ref, vone_ref,
                 hg_ref, q_ref, k_ref, v_ref, z_ref, xbc_ref, dt_ref):
    tm = IN_SUBTILE
    rs = slice(r0, r0 + tm)
    x = x_ref[rs, :]
    h = (x * _inv_rms(x) * g_ref[...]).astype(BF16)
    big = _dot(h, win_ref[...])
    z_ref[rs, :] = big[:, COL_Z:COL_XBC].astype(BF16)
    xbc_ref[rs, :] = big[:, COL_XBC:IN_COLS].astype(BF16)
    misc = big[:, COL_MISC:COL_Z]
    for c in range(tm // SSM_CHUNK):
        dt_ref[r0 // SSM_CHUNK + c] = misc[c * SSM_CHUNK:(c + 1) * SSM_CHUNK, :].T[0:2 * SSM_HEADS, :]

    cq = big[:, 0:COL_CKV]
    ckv = big[:, COL_CKV:COL_MISC]
    cqn = (cq * _inv_rms(cq) * gqa_ref[...]).astype(BF16)
    ckvn = (ckv * _inv_rms(ckv) * gkva_ref[...]).astype(BF16)
    q_pre = _dot(cqn, wuq_ref[...])
    k_pre = _dot(ckvn, wuk_ref[...])
    v_ref[rs, :] = (_dot(ckvn, wuv_ref[...]) + vone_ref[...]).astype(BF16)

    half = QK_ROPE_DIM // 2
    zpad = jnp.zeros((HEAD_PAD - QK_ROPE_DIM, SSM_CHUNK), F32)
    tt = jnp.concatenate(
        [jnp.concatenate([cs_ref[:, r0 + c * SSM_CHUNK:r0 + (c + 1) * SSM_CHUNK], zpad], axis=0).T
         for c in range(tm // SSM_CHUNK)], axis=0)
    lane_t = lax.broadcasted_iota(jnp.int32, (tm, HEAD_PAD), 1)
    cs = jnp.where(lane_t < QK_NOPE_DIM, 1.0,
                   jnp.where(lane_t < QK_NOPE_DIM + half, pltpu.roll(tt, QK_NOPE_DIM, 1),
                             jnp.where(lane_t < QK_HEAD_DIM + half, pltpu.roll(tt, QK_NOPE_DIM + half, 1),
                                       pltpu.roll(tt, QK_HEAD_DIM, 1))))
    lane = lax.broadcasted_iota(jnp.int32, (1, HEAD_PAD), 1)
    in_head = (lane < QK_HEAD_DIM).astype(F32)
    is_rope = ((lane >= QK_NOPE_DIM) & (lane < QK_HEAD_DIM)).astype(F32)
    inv_d = 1.0 / QK_HEAD_DIM

    gcq = hg_ref[0:1, :] * cs
    for hh in range(ATTN_HEADS):
        sl = slice(hh * HEAD_PAD, (hh + 1) * HEAD_PAD)
        qh = q_pre[:, sl]
        ssq = jnp.sum(qh * qh * in_head, axis=-1, keepdims=True)
        q_ref[rs, sl] = (qh * lax.rsqrt(ssq * inv_d + EPS) * gcq).astype(BF16)

    ab = misc * (hg_ref[2:3, :] * cs)
    lane2 = lax.broadcasted_iota(jnp.int32, (tm, HEAD_PAD), 1)
    swapped = jnp.where(lane2 < QK_HEAD_DIM, pltpu.roll(ab, HEAD_PAD - QK_ROPE_DIM, 1),
                        pltpu.roll(ab, QK_ROPE_DIM, 1))
    s_both = jnp.where(lane2 >= QK_NOPE_DIM, ab + swapped, 0.0)
    ssq_pe = jnp.sum(misc * misc * is_rope, axis=-1, keepdims=True)
    gkn = hg_ref[1:2, :]
    for hh in range(ATTN_HEADS):
        sl = slice(hh * HEAD_PAD, (hh + 1) * HEAD_PAD)
        kh = k_pre[:, sl]
        ssq = jnp.sum(kh * kh, axis=-1, keepdims=True) + ssq_pe
        k_ref[rs, sl] = ((kh * gkn + s_both) * lax.rsqrt(ssq * inv_d + EPS)).astype(BF16)


def _win_prep_kernel(w_ref, wq_ref, wkv_ref, o_ref, wuq_ref, wuk_ref, wuv_ref):
    half = QK_ROPE_DIM // 2

    @pl.when(pl.program_id(0) == 0)
    def _up_projection_tiles():
        wq = wq_ref[...]
        q_pieces = []
        for hh in range(ATTN_HEADS):
            head = wq[:, hh * QK_HEAD_DIM:(hh + 1) * QK_HEAD_DIM]
            rope = head[:, QK_NOPE_DIM:]
            q_pieces += [head, -rope[:, half:], rope[:, :half]]
        wuq_ref[...] = jnp.concatenate(q_pieces, axis=1).astype(BF16)
        wkv = wkv_ref[...]
        zeros = jnp.zeros((wkv.shape[0], V_HEAD_DIM), F32)
        k_pieces, v_pieces = [], []
        for hh in range(ATTN_HEADS):
            nope = wkv[:, hh * HEAD_PAD:hh * HEAD_PAD + QK_NOPE_DIM]
            val = wkv[:, hh * HEAD_PAD + QK_NOPE_DIM:(hh + 1) * HEAD_PAD]
            k_pieces += [nope, zeros]
            v_pieces += [val, zeros] if hh % 2 == 0 else [zeros, val]
        wuk_ref[...] = jnp.concatenate(k_pieces, axis=1).astype(BF16)
        wuv_ref[...] = jnp.concatenate(v_pieces, axis=1).astype(BF16)

    w = w_ref[...].astype(F32)
    o_kpe = Q_LORA_RANK + KV_LORA_RANK
    o_z = o_kpe + QK_ROPE_DIM
    o_dt = o_z + SSM_INNER + SSM_CONV_CH
    hpg = HEADS_PER_GROUP
    dt_f = w[:, o_dt:o_dt + SSM_HEADS]
    dt_b = w[:, o_dt + SSM_HEADS:o_dt + 2 * SSM_HEADS]
    kpe = w[:, o_kpe:o_z]
    pieces = [w[:, 0:o_kpe]]
    for g in range(SSM_GROUPS):
        pieces += [dt_f[:, g * hpg:(g + 1) * hpg], dt_b[:, g * hpg:(g + 1) * hpg]]
    pieces += [jnp.zeros((w.shape[0], QK_NOPE_DIM - 2 * SSM_HEADS), F32), kpe, -kpe[:, half:], kpe[:, :half],
               w[:, o_z:o_dt]]
    o_ref[...] = jnp.concatenate(pieces, axis=1).astype(BF16)


def _win_prep_call(w_in3, w_uq3, w_ukv3, layer):
    depth, d, n = w_in3.shape
    rows = d // WIN_PREP_STEPS
    whole = lambda a: pl.BlockSpec(a.shape[1:], lambda i: (layer, 0))
    head_cols = ATTN_HEADS * HEAD_PAD
    return pl.pallas_call(
        _win_prep_kernel,
        grid=(WIN_PREP_STEPS,),
        in_specs=[pl.BlockSpec((rows, n), lambda i: (layer * WIN_PREP_STEPS + i, 0)), whole(w_uq3), whole(w_ukv3)],
        out_specs=[pl.BlockSpec((rows, IN_COLS), lambda i: (i, 0)),
                   pl.BlockSpec((Q_LORA_RANK, head_cols), lambda i: (0, 0)),
                   pl.BlockSpec((KV_LORA_RANK, head_cols), lambda i: (0, 0)),
                   pl.BlockSpec((KV_LORA_RANK, head_cols), lambda i: (0, 0))],
        out_shape=[jax.ShapeDtypeStruct((d, IN_COLS), BF16),
                   jax.ShapeDtypeStruct((Q_LORA_RANK, head_cols), BF16),
                   jax.ShapeDtypeStruct((KV_LORA_RANK, head_cols), BF16),
                   jax.ShapeDtypeStruct((KV_LORA_RANK, head_cols), BF16)],
        compiler_params=pltpu.CompilerParams(dimension_semantics=("arbitrary",), vmem_limit_bytes=VMEM_LIMIT),
        name="win_prep",
    )(w_in3.reshape(depth * d, n).astype(BF16),
      w_uq3.reshape(depth * w_uq3.shape[1], w_uq3.shape[2]),
      w_ukv3.reshape(depth * w_ukv3.shape[1], w_ukv3.shape[2]))


def _inproj_call(x2, cs, g, win, gqa, wuq, gkva, wuk, wuv, vone, hg):
    t = x2.shape[0]
    cpt = TM_IN // SSM_CHUNK
    full = lambda a: pl.BlockSpec(a.shape, lambda i: (0,) * a.ndim, pipeline_mode=pl.Buffered(1))
    row = lambda w: pl.BlockSpec((TM_IN, w), lambda i: (i, 0))
    return pl.pallas_call(
        _inproj_kernel,
        grid=(t // TM_IN,),
        in_specs=[row(D_MODEL), pl.BlockSpec((QK_ROPE_DIM, TM_IN), lambda i: (0, i)), full(g), full(win), full(gqa), full(wuq), full(gkva),
                  full(wuk), full(wuv), full(vone), full(hg)],
        out_specs=[row(ATTN_HEADS * HEAD_PAD), row(ATTN_HEADS * HEAD_PAD), row(ATTN_HEADS * HEAD_PAD),
                   row(SSM_INNER), row(SSM_CONV_CH),
                   pl.BlockSpec((cpt, 2 * SSM_HEADS, SSM_CHUNK), lambda i: (i, 0, 0))],
        out_shape=[jax.ShapeDtypeStruct((t, ATTN_HEADS * HEAD_PAD), BF16),
                   jax.ShapeDtypeStruct((t, ATTN_HEADS * HEAD_PAD), BF16),
                   jax.ShapeDtypeStruct((t, ATTN_HEADS * HEAD_PAD), BF16),
                   jax.ShapeDtypeStruct((t, SSM_INNER), BF16),
                   jax.ShapeDtypeStruct((t, SSM_CONV_CH), BF16),
                   jax.ShapeDtypeStruct((t // SSM_CHUNK, 2 * SSM_HEADS, SSM_CHUNK), F32)],
        compiler_params=pltpu.CompilerParams(dimension_semantics=("arbitrary",),
                                             vmem_limit_bytes=VMEM_LIMIT),
        name="inproj",
    )(x2, cs, g, win, gqa, wuq, gkva, wuk, wuv, vone, hg)


def _attn_kernel(q_ref, k_ref, v_ref, y_ref, z_ref, gat_ref, gn_ref, *refs):
    n_w = (len(refs) - 1) // 2
    w_in_refs, mix_ref, w_out_refs = refs[:n_w], refs[n_w], refs[n_w + 1:]
    for w_in_ref, w_out_ref in zip(w_in_refs, w_out_refs):
        w_out_ref[...] = w_in_ref[...].astype(BF16)

    zc = z_ref[...].astype(F32)
    y = y_ref[...] * (zc * jax.nn.sigmoid(zc))
    yn = jnp.concatenate([y[:, g * GROUP_INNER:(g + 1) * GROUP_INNER]
                          * _inv_rms(y[:, g * GROUP_INNER:(g + 1) * GROUP_INNER]) for g in range(SSM_GROUPS)],
                         axis=1) * gn_ref[...]
    mix_ref[:, ATTN_WIDTH:D_MIX] = yn.astype(BF16)

    lane = lax.broadcasted_iota(jnp.int32, (q_ref.shape[0], HEAD_PAD), 1)
    outs = []
    for jp in range(ATTN_HEADS_PER_STEP // 2):
        sls = [slice(j * HEAD_PAD, (j + 1) * HEAD_PAD) for j in (2 * jp, 2 * jp + 1)]
        ss = [_dot_nt(q_ref[:, sl], k_ref[:, sl]) for sl in sls]
        ps = [jnp.exp2(s - jnp.max(s, axis=-1, keepdims=True)).astype(BF16) for s in ss]
        accs = [_dot(p, v_ref[:, sl]) for p, sl in zip(ps, sls)]
        res = [acc / pltpu.roll(acc, V_HEAD_DIM, 1) for acc in accs]
        outs.append(jnp.where(lane < V_HEAD_DIM, res[0], res[1]))
    attn = jnp.concatenate(outs, axis=1)
    mix_ref[:, 0:ATTN_WIDTH] = (attn * _inv_rms(attn) * gat_ref[...]).astype(BF16)


def _attn_call(q, k, v, y, z, gat, gn, later_weights, batch, seq):
    n_q = seq // TQ
    hps = ATTN_HEADS_PER_STEP
    assert ATTN_HEADS == hps
    n_steps = batch * n_q

    def rows_spec(w):
        assert w.shape[0] % (n_steps * BF16_ROWS) == 0
        return pl.BlockSpec((w.shape[0] // n_steps, w.shape[1]), lambda b, hp, i: (b * n_q + i, 0))
    w_specs = [rows_spec(w) for w in later_weights]
    tok = lambda width: pl.BlockSpec((TQ, width), lambda b, hp, i: (b * n_q + i, 0))
    gain = lambda a: pl.BlockSpec(a.shape, lambda b, hp, i: (0, 0))
    outs = pl.pallas_call(
        _attn_kernel,
        grid=(batch, ATTN_HEADS // hps, n_q),
        in_specs=[pl.BlockSpec((TQ, hps * HEAD_PAD), lambda b, hp, i: (b * n_q + i, hp)),
                  pl.BlockSpec((seq, hps * HEAD_PAD), lambda b, hp, i: (b, hp)),
                  pl.BlockSpec((seq, hps * HEAD_PAD), lambda b, hp, i: (b, hp)),
                  tok(SSM_INNER), tok(SSM_INNER), gain(gat), gain(gn)] + w_specs,
        out_specs=[tok(D_MIX)] + w_specs,
        out_shape=[jax.ShapeDtypeStruct((batch * seq, D_MIX), BF16)]
                  + [jax.ShapeDtypeStruct(w.shape, BF16) for w in later_weights],
        compiler_params=pltpu.CompilerParams(
            dimension_semantics=("arbitrary", "arbitrary", "arbitrary"),
            vmem_limit_bytes=VMEM_LIMIT),
        name="attention",
    )(q, k, v, y, z, gat, gn, *later_weights)
    return outs[0], outs[1:]


def _split3(x):
    hi = x.astype(BF16)
    r1 = x - hi.astype(F32)
    mid = r1.astype(BF16)
    lo = (r1 - mid.astype(F32)).astype(BF16)
    return hi, mid, lo


def _ssd_kernel(x_ref, b_ref, c_ref, dt_ref, sh_ref, place_ref, lhs_ones_ref, rhs_const_ref,
                cwx_ref, cwb_ref, cwc_ref, cbx_ref, cbb_ref, cbc_ref, adt_ref, dskip_ref,
                o_ref, xc_scr, xm_scr, nsf_scr, nsb_scr, dtv_scr, w_scr, dec_scr, lhs_scr, rhs_scr, rpart_scr):
    nc = dt_ref.shape[0]
    seq = x_ref.shape[0]
    L = SSM_CHUNK
    hpg = HEADS_PER_GROUP
    npair = hpg // 2
    P = SSM_HEAD_DIM
    x_cols = slice(0, GROUP_INNER)
    b_cols = slice(GROUP_INNER, GROUP_INNER + SSM_STATE)
    c_cols = slice(GROUP_INNER + SSM_STATE, GROUP_COLS)
    pair_cols = [slice(j * L, (j + 1) * L) for j in range(npair)]

    @pl.when((pl.program_id(0) == 0) & (pl.program_id(1) == 0))
    def _init_rhs():
        for u in range(SSD_UNROLL):
            rhs_scr[u] = rhs_const_ref[...]

    a_neg = -jnp.exp(adt_ref[0]) * LOG2E
    dtv = jax.nn.softplus(dt_ref[...] + adt_ref[1][None])
    dtv_scr[...] = dtv
    da2 = (dtv * a_neg[None]).reshape(nc * DT_ROWS, L)
    ri = lax.broadcasted_iota(jnp.int32, (L, L), 0)
    ci = lax.broadcasted_iota(jnp.int32, (L, L), 1)
    upper = (ri <= ci).astype(BF16)
    lower = (ri >= ci).astype(BF16)
    tri = jnp.concatenate([upper, lower], axis=1)
    cs_fb = sum(_dot(p, tri) for p in _split3(da2))
    rowsel = (ri & hpg) == 0
    cs2 = jnp.where(rowsel, cs_fb[:, 0:L], cs_fb[:, L:2 * L])
    colcs = cs2.T

    col_parts = jnp.concatenate(_split3(colcs), axis=1)
    half_n = place_ref.shape[1] // 2
    for hf in range(2):
        cols = slice(hf * half_n, (hf + 1) * half_n)
        lhs_half = _dot(col_parts, place_ref[:, cols]) + lhs_ones_ref[:, cols]
        for c in range(nc // 2):
            lhs_scr[hf * (nc // 2) + c] = lhs_half[:, c * L:c * L + SEG_K].astype(BF16)
    for i, part in enumerate(_split3(-cs2)):
        rpart_scr[i] = part.astype(F32).reshape(nc, DT_ROWS, L)

    def lane_bcast(col):
        return jnp.broadcast_to(col, (col.shape[0], L))

    tot2 = jnp.where(rowsel, lane_bcast(cs2[:, L - 1:L]), lane_bcast(cs2[:, 0:1]))
    dec_scr[...] = jnp.exp2(tot2).reshape(nc, DT_ROWS, L)
    w_scr[...] = (dtv.reshape(nc * DT_ROWS, L) * jnp.exp2(tot2 - cs2)).reshape(nc, DT_ROWS, L)

    lane_t = lax.broadcasted_iota(jnp.int32, (L, L), 1)
    low_half = lane_t < P

    cw_halves = (cwx_ref[...], jnp.concatenate([cwb_ref[...], cwc_ref[...]], axis=1))
    cb_halves = (cbx_ref[...], jnp.concatenate([cbb_ref[...], cbc_ref[...]], axis=1))

    def rows(hf, start, size):
        if hf == 0:
            return x_ref[pl.ds(start, size), :]
        return jnp.concatenate([b_ref[pl.ds(start, size), :], c_ref[pl.ds(start, size), :]], axis=1)

    def conv_stage(c):
        base = pl.multiple_of(c * L, L)
        ws = pl.multiple_of(jnp.clip(c * L - CONV_WIN_LEAD, 0, seq - CONV_WIN), CONV_WIN_LEAD)
        variant = jnp.where(c == 0, 0, jnp.where(c == nc - 1, 2, 1))
        sh = sh_ref[variant]
        halves = []
        for hf in range(2):
            cw = cw_halves[hf]
            shifted = _dot(sh, rows(hf, ws, CONV_WIN))
            acc = cb_halves[hf] + cw[SSM_CONV // 2:SSM_CONV // 2 + 1, :] * rows(hf, base, L).astype(F32)
            for t, kk in enumerate(CONV_SHIFTED_TAPS):
                acc = acc + cw[kk:kk + 1, :] * shifted[t * L:(t + 1) * L, :]
            halves.append(acc * jax.nn.sigmoid(acc))
        xc = jnp.concatenate(halves, axis=1)
        xc_scr[c] = xc
        for j in range(npair):
            xp = xc[:, pair_cols[j]]
            xm_scr[c, j] = jnp.concatenate([jnp.where(low_half, xp, 0.0), jnp.where(low_half, 0.0, xp)],
                                           axis=0).astype(BF16)

    def state_loads(c):
        return xc_scr[c, :, b_cols], [xm_scr[c, j] for j in range(npair)]

    def state_stage(c, loaded):
        b_tok, xm = loaded
        bt = b_tok.T
        for d, ns_scr in ((0, nsf_scr), (1, nsb_scr)):
            w = w_scr[c, d * hpg:(d + 1) * hpg, :]
            ns_scr[c] = jnp.concatenate(
                [_dot(jnp.concatenate([(bt * w[2 * j:2 * j + 1, :]).astype(BF16),
                                       (bt * w[2 * j + 1:2 * j + 2, :]).astype(BF16)], axis=1), xm[j])
                 for j in range(npair)], axis=1)

    for c in range(CONV_AHEAD):
        conv_stage(c)

    def pass_a(c, carry):
        loaded = state_loads(c)
        conv_stage(c + CONV_AHEAD)
        state_stage(c, loaded)
        return carry
    lax.fori_loop(0, nc - CONV_AHEAD, pass_a, 0, unroll=PASS_A_UNROLL)
    for c in range(nc - CONV_AHEAD, nc):
        state_stage(c, state_loads(c))

    lane_r = lax.broadcasted_iota(jnp.int32, (1, L), 1)

    def decay_row(c, d):
        dec = dec_scr[c, d * hpg:(d + 1) * hpg, :]
        return jnp.concatenate([jnp.where(lane_r < P, dec[2 * j:2 * j + 1, :], dec[2 * j + 1:2 * j + 2, :])
                                for j in range(npair)], axis=1)

    def pass_b(d, ns_scr):
        def body(i, st):
            c = i if d == 0 else nc - 1 - i
            new = ns_scr[c]
            ns_scr[c] = st
            return st * decay_row(c, d) + new
        lax.fori_loop(0, nc, body, jnp.zeros((SSM_STATE, GROUP_INNER), F32))
    pass_b(0, nsf_scr)
    pass_b(1, nsb_scr)

    mask_f = ci <= ri
    mask_b = ci >= ri
    neg_inf = jnp.float32(-jnp.inf)
    sub16 = lax.broadcasted_iota(jnp.int32, (BF16_ROWS, L), 0)

    def pass_c(c, carry):
        base = pl.multiple_of(c * L, L)
        xc = xc_scr[c]
        bm = xc[:, b_cols].astype(BF16)
        cm = xc[:, c_cols].astype(BF16)
        cb = _dot_nt(cm, bm)
        off_f = _dot(cm, nsf_scr[c].astype(BF16))
        off_b = _dot(cm, nsb_scr[c].astype(BF16))
        slot = c % SSD_UNROLL
        for hd in range(DT_ROWS):
            r0 = (hd % 2) * SEG_K_PER_ROW
            blk = jnp.where((sub16 >= r0) & (sub16 < r0 + 3), 1.0, 0.0)
            for i in range(3):
                blk = jnp.where(sub16 == r0 + 3 + i, rpart_scr[i, c, hd:hd + 1, :], blk)
            rhs_scr[slot, (hd // 2) * BF16_ROWS:(hd // 2 + 1) * BF16_ROWS, hd * L:(hd + 1) * L] = blk.astype(BF16)
        seg = _dot(lhs_scr[c], rhs_scr[slot])
        dt_f_r = dtv_scr[c, 0:hpg, :]
        dt_b_r = dtv_scr[c, hpg:DT_ROWS, :]
        ys = []
        for j in range(npair):
            ws_ = []
            for h in (2 * j, 2 * j + 1):
                ef = jnp.exp2(jnp.where(mask_f, seg[:, h * L:(h + 1) * L], neg_inf))
                eb = jnp.exp2(jnp.where(mask_b, seg[:, (hpg + h) * L:(hpg + h + 1) * L], neg_inf))
                ws_.append((cb * (ef * dt_f_r[h:h + 1, :] + eb * dt_b_r[h:h + 1, :])).astype(BF16))
            scale_f = jnp.exp2(seg[:, (DT_ROWS + j) * L:(DT_ROWS + j + 1) * L])
            scale_b = jnp.exp2(seg[:, (DT_ROWS + npair + j) * L:(DT_ROWS + npair + j + 1) * L])
            ys.append(_dot(jnp.concatenate(ws_, axis=1), xm_scr[c, j])
                      + scale_f * off_f[:, pair_cols[j]] + scale_b * off_b[:, pair_cols[j]])
        y = jnp.concatenate(ys, axis=1) + dskip_ref[...] * xc[:, x_cols]
        o_ref[pl.ds(base, L), :] = y
        return carry
    lax.fori_loop(0, nc, pass_c, 0, unroll=SSD_UNROLL)


def _conv_shift_matrices():
    out = np.zeros((3, len(CONV_SHIFTED_TAPS) * SSM_CHUNK, CONV_WIN), np.float32)
    for v, lead in enumerate((0, CONV_WIN_LEAD, CONV_WIN - SSM_CHUNK)):
        for i, kk in enumerate(CONV_SHIFTED_TAPS):
            for t in range(SSM_CHUNK):
                j = t + kk - SSM_CONV // 2 + lead
                if 0 <= j < CONV_WIN:
                    out[v, i * SSM_CHUNK + t, j] = 1.0
    return out


def _seg_matmul_constants(nc):
    L, K, half = SSM_CHUNK, SEG_K_PER_ROW, SSM_HEAD_DIM
    place = np.zeros((3 * L, nc * L), np.float32)
    lhs_ones = np.zeros((1, nc * L), np.float32)
    for c in range(nc):
        for hd in range(DT_ROWS):
            for i in range(3):
                place[i * L + c * DT_ROWS + hd, c * L + hd * K + i] = 1.0
                lhs_ones[0, c * L + hd * K + 3 + i] = 1.0
    rhs = np.zeros((SEG_K, SEG_BLOCKS * L), np.float32)
    for hd in range(DT_ROWS):
        d, h = divmod(hd, HEADS_PER_GROUP)
        pair_block = DT_ROWS + d * (HEADS_PER_GROUP // 2) + h // 2
        lanes = slice(0, half) if h % 2 == 0 else slice(half, L)
        rhs[hd * K:hd * K + 3, hd * L:(hd + 1) * L] = 1.0
        rhs[hd * K:hd * K + 3, pair_block * L:(pair_block + 1) * L][:, lanes] = 1.0
    return place, lhs_ones, rhs


def _ssd_call(xbc, dtc, cw, cbias, adt, dskip, batch, seq):
    nc = seq // SSM_CHUNK
    L = SSM_CHUNK
    assert nc >= 3 and seq >= CONV_WIN and nc * DT_ROWS == L
    sh = jnp.asarray(_conv_shift_matrices(), BF16)
    place, lhs_ones, rhs_const = _seg_matmul_constants(nc)
    place = jnp.asarray(place, BF16)
    lhs_ones = jnp.asarray(lhs_ones, F32)
    rhs_const = jnp.asarray(rhs_const, BF16)
    const = lambda a: pl.BlockSpec(a.shape, lambda b, g: (0,) * a.ndim)
    b_blk0 = SSM_INNER // SSM_STATE
    c_blk0 = b_blk0 + SSM_GROUPS

    def xbc_views(rows_, batched):
        lead = (lambda b: b) if batched else (lambda b: 0)
        return [pl.BlockSpec((rows_, GROUP_INNER), lambda b, g: (lead(b), g)),
                pl.BlockSpec((rows_, SSM_STATE), lambda b, g: (lead(b), b_blk0 + g)),
                pl.BlockSpec((rows_, SSM_STATE), lambda b, g: (lead(b), c_blk0 + g))]
    return pl.pallas_call(
        _ssd_kernel,
        grid=(batch, SSM_GROUPS),
        in_specs=xbc_views(seq, True) + [
                  pl.BlockSpec((nc, DT_ROWS, L), lambda b, g: (b, g, 0)),
                  const(sh), const(place), const(lhs_ones), const(rhs_const)]
                 + xbc_views(SUBLANES, False) + xbc_views(1, False) + [
                  pl.BlockSpec((2, DT_ROWS, L), lambda b, g: (0, g, 0)),
                  pl.BlockSpec((1, GROUP_INNER), lambda b, g: (0, g))],
        out_specs=pl.BlockSpec((seq, GROUP_INNER), lambda b, g: (b, g)),
        out_shape=jax.ShapeDtypeStruct((batch * seq, SSM_INNER), F32),
        scratch_shapes=[pltpu.VMEM((nc, L, GROUP_COLS), F32),
                        pltpu.VMEM((nc, HEADS_PER_GROUP // 2, 2 * L, L), BF16),
                        pltpu.VMEM((nc, SSM_STATE, GROUP_INNER), F32),
                        pltpu.VMEM((nc, SSM_STATE, GROUP_INNER), F32),
                        pltpu.VMEM((nc, DT_ROWS, L), F32),
                        pltpu.VMEM((nc, DT_ROWS, L), F32),
                        pltpu.VMEM((nc, DT_ROWS, L), F32),
                        pltpu.VMEM((nc, L, SEG_K), BF16),
                        pltpu.VMEM((SSD_UNROLL, SEG_K, SEG_BLOCKS * L), BF16),
                        pltpu.VMEM((3, nc, DT_ROWS, L), F32)],
        compiler_params=pltpu.CompilerParams(dimension_semantics=("arbitrary", "arbitrary"),
                                             vmem_limit_bytes=VMEM_LIMIT),
        name="ssd",
    )(xbc, xbc, xbc, dtc, sh, place, lhs_ones, rhs_const, cw, cw, cw, cbias, cbias, cbias,
      adt, dskip)


def _mlp_kernel(x_ref, mix_ref, wo_ref, gm_ref, wup_ref, wdn_ref, o_ref):
    for sub in range(x_ref.shape[0] // MLP_SUBTILE):
        rs = slice(sub * MLP_SUBTILE, (sub + 1) * MLP_SUBTILE)
        x1 = x_ref[rs, :] + _dot(mix_ref[rs, :], wo_ref[...])
        hm = (x1 * _inv_rms(x1) * gm_ref[...]).astype(BF16)
        acc = jnp.zeros_like(x1)
        for c in range(D_FF // FF_CHUNK):
            cols = slice(c * FF_CHUNK, (c + 1) * FF_CHUNK)
            u = _dot(hm, wup_ref[:, cols])
            acc = acc + _dot(jnp.square(jnp.maximum(u, 0.0)).astype(BF16), wdn_ref[cols, :])
        o_ref[rs, :] = x1 + acc


def _mlp_call(x2, mix, wo, gm, wup, wdn):
    t = x2.shape[0]
    row = lambda w: pl.BlockSpec((TM_MLP, w), lambda i: (i, 0))
    full = lambda a: pl.BlockSpec(a.shape, lambda i: (0, 0), pipeline_mode=pl.Buffered(1))
    return pl.pallas_call(
        _mlp_kernel,
        grid=(t // TM_MLP,),
        in_specs=[row(D_MODEL), row(D_MIX), full(wo), full(gm), full(wup), full(wdn)],
        out_specs=row(D_MODEL),
        out_shape=jax.ShapeDtypeStruct((t, D_MODEL), F32),
        compiler_params=pltpu.CompilerParams(dimension_semantics=("arbitrary",),
                                             vmem_limit_bytes=VMEM_LIMIT),
        name="outproj_mlp",
    )(x2, mix, wo, gm, wup, wdn)


def _swap_halves(g):
    half = QK_ROPE_DIM // 2
    return jnp.concatenate([g[..., half:], g[..., :half]], axis=-1)


def _lane_bcast(v):
    return jnp.broadcast_to(v[..., None], v.shape + (LANES,)).astype(F32)


def _layer(x2, cs, batch, seq, ln_mix_g, stacked_proj_weights, q_a_norm_g, kv_a_norm_g, q_norm_g,
           k_norm_g, attn_out_norm_g, conv_w, conv_b, a_log_fwd, a_log_bwd, dt_bias_fwd, dt_bias_bwd,
           d_skip, ssm_norm_g, w_out, ln_mlp_g, w_mlp_up, w_mlp_down):
    win, wuq, wuk, wuv = _win_prep_call(*stacked_proj_weights)
    ones_v = np.ones((ATTN_HEADS // 2, V_HEAD_DIM), np.float32)
    vone = jnp.asarray(np.stack([np.concatenate([0 * ones_v, ones_v], axis=-1),
                                 np.concatenate([ones_v, 0 * ones_v], axis=-1)], axis=1).reshape(1, -1))
    scale = QK_HEAD_DIM ** -0.5 * np.log2(np.e)
    zeros_nope = jnp.zeros((QK_NOPE_DIM,), k_norm_g.dtype)
    k_rope_g = k_norm_g[QK_NOPE_DIM:]
    hg = jnp.concatenate([q_norm_g * scale, _swap_halves(q_norm_g[QK_NOPE_DIM:]) * scale,
                          k_norm_g[:QK_NOPE_DIM], zeros_nope,
                          zeros_nope, k_rope_g, _swap_halves(k_rope_g),
                          jnp.zeros(((SUBLANES - 3) * HEAD_PAD,), k_norm_g.dtype)]).reshape(SUBLANES, HEAD_PAD)

    q, k, v, z, xbc, dtc = _inproj_call(
        x2, cs, ln_mix_g[None, :], win, q_a_norm_g[None, :], wuq, kv_a_norm_g[None, :], wuk, wuv, vone,
        hg)

    cw = jnp.pad(conv_w[:, 0, :], ((0, SUBLANES - SSM_CONV), (0, 0)))
    cbias = conv_b[None, :]
    adt = jnp.stack([a_log_fwd, a_log_bwd, dt_bias_fwd, dt_bias_bwd]).reshape(2, 2, SSM_GROUPS, HEADS_PER_GROUP)
    adt = _lane_bcast(adt.transpose(0, 2, 1, 3).reshape(2, 2 * SSM_HEADS))
    dskip = jnp.repeat(d_skip, SSM_HEAD_DIM)[None, :]
    ssm = _ssd_call(xbc, dtc, cw, cbias, adt, dskip, batch, seq)

    mix, (wo_b, wup_b, wdn_b) = _attn_call(q, k, v, ssm, z, attn_out_norm_g[None, :], ssm_norm_g[None, :],
                                           (w_out, w_mlp_up, w_mlp_down), batch, seq)

    return _mlp_call(x2, mix, wo_b, ln_mlp_g[None, :], wup_b, wdn_b)


def _rope_table(positions):
    inv_freq = 1.0 / (ROPE_THETA ** (jnp.arange(0, QK_ROPE_DIM, 2, dtype=F32) / QK_ROPE_DIM))
    ang = inv_freq[:, None] * positions.astype(F32).reshape(1, -1)
    return jnp.concatenate([jnp.cos(ang), jnp.sin(ang)], axis=0)


def kernel(x, positions, ln_mix_g, w_in, q_a_norm_g, w_uq, kv_a_norm_g, w_ukv, q_norm_g, k_norm_g,
           attn_out_norm_g, conv_w, conv_b, a_log_fwd, a_log_bwd, dt_bias_fwd, dt_bias_bwd, d_skip,
           ssm_norm_g, w_out, ln_mlp_g, w_mlp_up, w_mlp_down):
    batch, seq, d = x.shape
    assert d == D_MODEL and seq % TQ == 0 and (batch * seq) % TM_IN == 0 and (batch * seq) % TM_MLP == 0
    cs = _rope_table(positions)
    x2 = x.reshape(batch * seq, d)
    for l in range(ln_mix_g.shape[0]):
        x2 = _layer(x2, cs, batch, seq, ln_mix_g[l], (w_in, w_uq, w_ukv, l), q_a_norm_g[l], kv_a_norm_g[l],
                    q_norm_g[l], k_norm_g[l], attn_out_norm_g[l], conv_w[l], conv_b[l],
                    a_log_fwd[l], a_log_bwd[l], dt_bias_fwd[l], dt_bias_bwd[l], d_skip[l], ssm_norm_g[l],
                    w_out[l], ln_mlp_g[l], w_mlp_up[l], w_mlp_down[l])
    return x2.reshape(batch, seq, d)
```

```python
import numpy as np
import jax
import jax.numpy as jnp
from jax import lax
from jax.experimental import pallas as pl
from jax.experimental.pallas import tpu as pltpu

F32 = jnp.float32
BF16 = jnp.bfloat16

D_MODEL = 1024
ATTN_HEADS = 8
QK_NOPE_DIM = 64
QK_ROPE_DIM = 32
QK_HEAD_DIM = QK_NOPE_DIM + QK_ROPE_DIM
V_HEAD_DIM = 64
Q_LORA_RANK = D_MODEL // 4
KV_LORA_RANK = D_MODEL // 8
ROPE_THETA = 10000.0
ATTN_WIDTH = ATTN_HEADS * V_HEAD_DIM
SSM_HEADS = 8
SSM_HEAD_DIM = 64
SSM_INNER = SSM_HEADS * SSM_HEAD_DIM
SSM_GROUPS = 2
SSM_STATE = 128
SSM_CONV = 5
SSM_CHUNK = 128
SSM_CONV_CH = SSM_INNER + 2 * SSM_GROUPS * SSM_STATE
D_MIX = ATTN_WIDTH + SSM_INNER
D_FF = 4 * D_MODEL
EPS = 1e-6

LANES = 128
SUBLANES = 8
HEAD_PAD = LANES

HEADS_PER_GROUP = SSM_HEADS // SSM_GROUPS
GROUP_INNER = SSM_INNER // SSM_GROUPS
GROUP_COLS = GROUP_INNER + 2 * SSM_STATE
DT_ROWS = 2 * HEADS_PER_GROUP
BF16_ROWS = 16
SEG_K_PER_ROW = 8
SEG_K = DT_ROWS * SEG_K_PER_ROW
SEG_BLOCKS = DT_ROWS + HEADS_PER_GROUP
LOG2E = float(np.log2(np.e))
CONV_WIN = 2 * SSM_CHUNK
CONV_WIN_LEAD = SSM_CHUNK // 2
CONV_SHIFTED_TAPS = tuple(k for k in range(SSM_CONV) if k != SSM_CONV // 2)

COL_CKV = Q_LORA_RANK
COL_MISC = COL_CKV + KV_LORA_RANK
COL_Z = COL_MISC + LANES
COL_XBC = COL_Z + SSM_INNER
IN_COLS = COL_XBC + SSM_CONV_CH

TM_IN = 1024
IN_SUBTILE = 512
TQ = 1024
ATTN_HEADS_PER_STEP = 8
TM_MLP = 1024
MLP_SUBTILE = 512
FF_CHUNK = 1024
SSD_UNROLL = 16
CONV_AHEAD = 4
PASS_A_UNROLL = 6
WIN_PREP_STEPS = 4
VMEM_LIMIT = 56 * 1024 * 1024


def _inv_rms(x):
    n = x.shape[-1]
    x2 = x * x
    acc = x2[:, 0:LANES]
    for i in range(1, n // LANES):
        acc = acc + x2[:, i * LANES:(i + 1) * LANES]
    return lax.rsqrt(jnp.sum(acc, axis=-1, keepdims=True) * (1.0 / n) + EPS)


def _dot(a, b):
    return jnp.dot(a, b, preferred_element_type=F32)


def _dot_nt(a, b):
    return lax.dot_general(a, b, (((1,), (1,)), ((), ())), preferred_element_type=F32)


def _inproj_kernel(x_ref, cs_ref, g_ref, win_ref, gqa_ref, wuq_ref, gkva_ref, wuk_ref, wuv_ref, vone_ref,
                   hg_ref,
                   q_ref, k_ref, v_ref, z_ref, xbc_ref, dt_ref):
    for sub in range(x_ref.shape[0] // IN_SUBTILE):
        _inproj_rows(sub * IN_SUBTILE, x_ref, cs_ref, g_ref, win_ref, gqa_ref, wuq_ref, gkva_ref, wuk_ref,
                     wuv_ref, vone_ref, hg_ref, q_ref, k_ref, v_ref, z_ref, xbc_ref, dt_ref)


def _inproj_rows(r0, x_ref, cs_ref, g_ref, win_ref, gqa_ref, wuq_ref, gkva_ref, wuk_ref, wuv_ref, vone_ref,
                 hg_ref, q_ref, k_ref, v_ref, z_ref, xbc_ref, dt_ref):
    tm = IN_SUBTILE
    rs = slice(r0, r0 + tm)
    x = x_ref[rs, :]
    h = (x * _inv_rms(x) * g_ref[...]).astype(BF16)
    big = _dot(h, win_ref[...])
    z_ref[rs, :] = big[:, COL_Z:COL_XBC].astype(BF16)
    xbc_ref[rs, :] = big[:, COL_XBC:IN_COLS].astype(BF16)
    misc = big[:, COL_MISC:COL_Z]
    for c in range(tm // SSM_CHUNK):
        dt_ref[r0 // SSM_CHUNK + c] = misc[c * SSM_CHUNK:(c + 1) * SSM_CHUNK, :].T[0:2 * SSM_HEADS, :]

    cq = big[:, 0:COL_CKV]
    ckv = big[:, COL_CKV:COL_MISC]
    cqn = (cq * _inv_rms(cq) * gqa_ref[...]).astype(BF16)
    ckvn = (ckv * _inv_rms(ckv) * gkva_ref[...]).astype(BF16)
    q_pre = _dot(cqn, wuq_ref[...])
    k_pre = _dot(ckvn, wuk_ref[...])
    v_ref[rs, :] = (_dot(ckvn, wuv_ref[...]) + vone_ref[...]).astype(BF16)

    half = QK_ROPE_DIM // 2
    zpad = jnp.zeros((HEAD_PAD - QK_ROPE_DIM, SSM_CHUNK), F32)
    tt = jnp.concatenate(
        [jnp.concatenate([cs_ref[:, r0 + c * SSM_CHUNK:r0 + (c + 1) * SSM_CHUNK], zpad], axis=0).T
         for c in range(tm // SSM_CHUNK)], axis=0)
    lane_t = lax.broadcasted_iota(jnp.int32, (tm, HEAD_PAD), 1)
    cs = jnp.where(lane_t < QK_NOPE_DIM, 1.0,
                   jnp.where(lane_t < QK_NOPE_DIM + half, pltpu.roll(tt, QK_NOPE_DIM, 1),
                             jnp.where(lane_t < QK_HEAD_DIM + half, pltpu.roll(tt, QK_NOPE_DIM + half, 1),
                                       pltpu.roll(tt, QK_HEAD_DIM, 1))))
    lane = lax.broadcasted_iota(jnp.int32, (1, HEAD_PAD), 1)
    in_head = (lane < QK_HEAD_DIM).astype(F32)
    is_rope = ((lane >= QK_NOPE_DIM) & (lane < QK_HEAD_DIM)).astype(F32)
    inv_d = 1.0 / QK_HEAD_DIM

    gcq = hg_ref[0:1, :] * cs
    for hh in range(ATTN_HEADS):
        sl = slice(hh * HEAD_PAD, (hh + 1) * HEAD_PAD)
        qh = q_pre[:, sl]
        ssq = jnp.sum(qh * qh * in_head, axis=-1, keepdims=True)
        q_ref[rs, sl] = (qh * lax.rsqrt(ssq * inv_d + EPS) * gcq).astype(BF16)

    ab = misc * (hg_ref[2:3, :] * cs)
    lane2 = lax.broadcasted_iota(jnp.int32, (tm, HEAD_PAD), 1)
    swapped = jnp.where(lane2 < QK_HEAD_DIM, pltpu.roll(ab, HEAD_PAD - QK_ROPE_DIM, 1),
                        pltpu.roll(ab, QK_ROPE_DIM, 1))
    s_both = jnp.where(lane2 >= QK_NOPE_DIM, ab + swapped, 0.0)
    ssq_pe = jnp.sum(misc * misc * is_rope, axis=-1, keepdims=True)
    gkn = hg_ref[1:2, :]
    for hh in range(ATTN_HEADS):
        sl = slice(hh * HEAD_PAD, (hh + 1) * HEAD_PAD)
        kh = k_pre[:, sl]
        ssq = jnp.sum(kh * kh, axis=-1, keepdims=True) + ssq_pe
        k_ref[rs, sl] = ((kh * gkn + s_both) * lax.rsqrt(ssq * inv_d + EPS)).astype(BF16)


def _win_prep_kernel(w_ref, wq_ref, wkv_ref, o_ref, wuq_ref, wuk_ref, wuv_ref):
    half = QK_ROPE_DIM // 2

    @pl.when(pl.program_id(0) == 0)
    def _up_projection_tiles():
        wq = wq_ref[...]
        q_pieces = []
        for hh in range(ATTN_HEADS):
            head = wq[:, hh * QK_HEAD_DIM:(hh + 1) * QK_HEAD_DIM]
            rope = head[:, QK_NOPE_DIM:]
            q_pieces += [head, -rope[:, half:], rope[:, :half]]
        wuq_ref[...] = jnp.concatenate(q_pieces, axis=1).astype(BF16)
        wkv = wkv_ref[...]
        zeros = jnp.zeros((wkv.shape[0], V_HEAD_DIM), F32)
        k_pieces, v_pieces = [], []
        for hh in range(ATTN_HEADS):
            nope = wkv[:, hh * HEAD_PAD:hh * HEAD_PAD + QK_NOPE_DIM]
            val = wkv[:, hh * HEAD_PAD + QK_NOPE_DIM:(hh + 1) * HEAD_PAD]
            k_pieces += [nope, zeros]
            v_pieces += [val, zeros] if hh % 2 == 0 else [zeros, val]
        wuk_ref[...] = jnp.concatenate(k_pieces, axis=1).astype(BF16)
        wuv_ref[...] = jnp.concatenate(v_pieces, axis=1).astype(BF16)

    w = w_ref[...].astype(F32)
    o_kpe = Q_LORA_RANK + KV_LORA_RANK
    o_z = o_kpe + QK_ROPE_DIM
    o_dt = o_z + SSM_INNER + SSM_CONV_CH
    hpg = HEADS_PER_GROUP
    dt_f = w[:, o_dt:o_dt + SSM_HEADS]
    dt_b = w[:, o_dt + SSM_HEADS:o_dt + 2 * SSM_HEADS]
    kpe = w[:, o_kpe:o_z]
    pieces = [w[:, 0:o_kpe]]
    for g in range(SSM_GROUPS):
        pieces += [dt_f[:, g * hpg:(g + 1) * hpg], dt_b[:, g * hpg:(g + 1) * hpg]]
    pieces += [jnp.zeros((w.shape[0], QK_NOPE_DIM - 2 * SSM_HEADS), F32), kpe, -kpe[:, half:], kpe[:, :half],
               w[:, o_z:o_dt]]
    o_ref[...] = jnp.concatenate(pieces, axis=1).astype(BF16)


def _win_prep_call(w_in3, w_uq3, w_ukv3, layer):
    depth, d, n = w_in3.shape
    rows = d // WIN_PREP_STEPS
    whole = lambda a: pl.BlockSpec(a.shape[1:], lambda i: (layer, 0))
    head_cols = ATTN_HEADS * HEAD_PAD
    return pl.pallas_call(
        _win_prep_kernel,
        grid=(WIN_PREP_STEPS,),
        in_specs=[pl.BlockSpec((rows, n), lambda i: (layer * WIN_PREP_STEPS + i, 0)), whole(w_uq3), whole(w_ukv3)],
        out_specs=[pl.BlockSpec((rows, IN_COLS), lambda i: (i, 0)),
                   pl.BlockSpec((Q_LORA_RANK, head_cols), lambda i: (0, 0)),
                   pl.BlockSpec((KV_LORA_RANK, head_cols), lambda i: (0, 0)),
                   pl.BlockSpec((KV_LORA_RANK, head_cols), lambda i: (0, 0))],
        out_shape=[jax.ShapeDtypeStruct((d, IN_COLS), BF16),
                   jax.ShapeDtypeStruct((Q_LORA_RANK, head_cols), BF16),
                   jax.ShapeDtypeStruct((KV_LORA_RANK, head_cols), BF16),
                   jax.ShapeDtypeStruct((KV_LORA_RANK, head_cols), BF16)],
        compiler_params=pltpu.CompilerParams(dimension_semantics=("arbitrary",), vmem_limit_bytes=VMEM_LIMIT),
        name="win_prep",
    )(w_in3.reshape(depth * d, n).astype(BF16),
      w_uq3.reshape(depth * w_uq3.shape[1], w_uq3.shape[2]),
      w_ukv3.reshape(depth * w_ukv3.shape[1], w_ukv3.shape[2]))


def _inproj_call(x2, cs, g, win, gqa, wuq, gkva, wuk, wuv, vone, hg):
    t = x2.shape[0]
    cpt = TM_IN // SSM_CHUNK
    full = lambda a: pl.BlockSpec(a.shape, lambda i: (0,) * a.ndim, pipeline_mode=pl.Buffered(1))
    row = lambda w: pl.BlockSpec((TM_IN, w), lambda i: (i, 0))
    return pl.pallas_call(
        _inproj_kernel,
        grid=(t // TM_IN,),
        in_specs=[row(D_MODEL), pl.BlockSpec((QK_ROPE_DIM, TM_IN), lambda i: (0, i)), full(g), full(win), full(gqa), full(wuq), full(gkva),
                  full(wuk), full(wuv), full(vone), full(hg)],
        out_specs=[row(ATTN_HEADS * HEAD_PAD), row(ATTN_HEADS * HEAD_PAD), row(ATTN_HEADS * HEAD_PAD),
                   row(SSM_INNER), row(SSM_CONV_CH),
                   pl.BlockSpec((cpt, 2 * SSM_HEADS, SSM_CHUNK), lambda i: (i, 0, 0))],
        out_shape=[jax.ShapeDtypeStruct((t, ATTN_HEADS * HEAD_PAD), BF16),
                   jax.ShapeDtypeStruct((t, ATTN_HEADS * HEAD_PAD), BF16),
                   jax.ShapeDtypeStruct((t, ATTN_HEADS * HEAD_PAD), BF16),
                   jax.ShapeDtypeStruct((t, SSM_INNER), BF16),
                   jax.ShapeDtypeStruct((t, SSM_CONV_CH), BF16),
                   jax.ShapeDtypeStruct((t // SSM_CHUNK, 2 * SSM_HEADS, SSM_CHUNK), F32)],
        compiler_params=pltpu.CompilerParams(dimension_semantics=("arbitrary",),
                                             vmem_limit_bytes=VMEM_LIMIT),
        name="inproj",
    )(x2, cs, g, win, gqa, wuq, gkva, wuk, wuv, vone, hg)


def _attn_kernel(q_ref, k_ref, v_ref, *refs):
    n_w = (len(refs) - 1) // 2
    w_in_refs, o_ref, w_out_refs = refs[:n_w], refs[n_w], refs[n_w + 1:]
    for w_in_ref, w_out_ref in zip(w_in_refs, w_out_refs):
        w_out_ref[...] = w_in_ref[...].astype(BF16)

    lane = lax.broadcasted_iota(jnp.int32, (q_ref.shape[0], HEAD_PAD), 1)
    for jp in range(ATTN_HEADS_PER_STEP // 2):
        sls = [slice(j * HEAD_PAD, (j + 1) * HEAD_PAD) for j in (2 * jp, 2 * jp + 1)]
        ss = [_dot_nt(q_ref[:, sl], k_ref[:, sl]) for sl in sls]
        ps = [jnp.exp2(s - jnp.max(s, axis=-1, keepdims=True)).astype(BF16) for s in ss]
        accs = [_dot(p, v_ref[:, sl]) for p, sl in zip(ps, sls)]
        res = [acc / pltpu.roll(acc, V_HEAD_DIM, 1) for acc in accs]
        o_ref[:, jp * HEAD_PAD:(jp + 1) * HEAD_PAD] = jnp.where(lane < V_HEAD_DIM, res[0], res[1])


def _attn_call(q, k, v, later_weights, batch, seq):
    n_q = seq // TQ
    hps = ATTN_HEADS_PER_STEP
    assert ATTN_HEADS == hps
    n_steps = batch * n_q

    def rows_spec(w):
        assert w.shape[0] % (n_steps * BF16_ROWS) == 0
        return pl.BlockSpec((w.shape[0] // n_steps, w.shape[1]), lambda b, hp, i: (b * n_q + i, 0))
    w_specs = [rows_spec(w) for w in later_weights]
    outs = pl.pallas_call(
        _attn_kernel,
        grid=(batch, ATTN_HEADS // hps, n_q),
        in_specs=[pl.BlockSpec((TQ, hps * HEAD_PAD), lambda b, hp, i: (b * n_q + i, hp)),
                  pl.BlockSpec((seq, hps * HEAD_PAD), lambda b, hp, i: (b, hp)),
                  pl.BlockSpec((seq, hps * HEAD_PAD), lambda b, hp, i: (b, hp))] + w_specs,
        out_specs=[pl.BlockSpec((TQ, hps * V_HEAD_DIM), lambda b, hp, i: (b * n_q + i, hp))] + w_specs,
        out_shape=[jax.ShapeDtypeStruct((batch * seq, ATTN_WIDTH), F32)]
                  + [jax.ShapeDtypeStruct(w.shape, BF16) for w in later_weights],
        compiler_params=pltpu.CompilerParams(
            dimension_semantics=("arbitrary", "arbitrary", "arbitrary"),
            vmem_limit_bytes=VMEM_LIMIT),
        name="attention",
    )(q, k, v, *later_weights)
    return outs[0], outs[1:]


def _split3(x):
    hi = x.astype(BF16)
    r1 = x - hi.astype(F32)
    mid = r1.astype(BF16)
    lo = (r1 - mid.astype(F32)).astype(BF16)
    return hi, mid, lo


def _ssd_kernel(x_ref, b_ref, c_ref, dt_ref, sh_ref, place_ref, lhs_ones_ref, rhs_const_ref,
                cwx_ref, cwb_ref, cwc_ref, cbx_ref, cbb_ref, cbc_ref, adt_ref, dskip_ref,
                o_ref, xc_scr, xm_scr, nsf_scr, nsb_scr, dtv_scr, w_scr, dec_scr, lhs_scr, rhs_scr, rpart_scr):
    nc = dt_ref.shape[0]
    seq = x_ref.shape[0]
    L = SSM_CHUNK
    hpg = HEADS_PER_GROUP
    npair = hpg // 2
    P = SSM_HEAD_DIM
    x_cols = slice(0, GROUP_INNER)
    b_cols = slice(GROUP_INNER, GROUP_INNER + SSM_STATE)
    c_cols = slice(GROUP_INNER + SSM_STATE, GROUP_COLS)
    pair_cols = [slice(j * L, (j + 1) * L) for j in range(npair)]

    @pl.when((pl.program_id(0) == 0) & (pl.program_id(1) == 0))
    def _init_rhs():
        for u in range(SSD_UNROLL):
            rhs_scr[u] = rhs_const_ref[...]

    a_neg = -jnp.exp(adt_ref[0]) * LOG2E
    dtv = jax.nn.softplus(dt_ref[...] + adt_ref[1][None])
    dtv_scr[...] = dtv
    da2 = (dtv * a_neg[None]).reshape(nc * DT_ROWS, L)
    ri = lax.broadcasted_iota(jnp.int32, (L, L), 0)
    ci = lax.broadcasted_iota(jnp.int32, (L, L), 1)
    upper = (ri <= ci).astype(BF16)
    lower = (ri >= ci).astype(BF16)
    tri = jnp.concatenate([upper, lower], axis=1)
    cs_fb = sum(_dot(p, tri) for p in _split3(da2))
    rowsel = (ri & hpg) == 0
    cs2 = jnp.where(rowsel, cs_fb[:, 0:L], cs_fb[:, L:2 * L])
    colcs = cs2.T

    col_parts = jnp.concatenate(_split3(colcs), axis=1)
    half_n = place_ref.shape[1] // 2
    for hf in range(2):
        cols = slice(hf * half_n, (hf + 1) * half_n)
        lhs_half = _dot(col_parts, place_ref[:, cols]) + lhs_ones_ref[:, cols]
        for c in range(nc // 2):
            lhs_scr[hf * (nc // 2) + c] = lhs_half[:, c * L:c * L + SEG_K].astype(BF16)
    for i, part in enumerate(_split3(-cs2)):
        rpart_scr[i] = part.astype(F32).reshape(nc, DT_ROWS, L)

    def lane_bcast(col):
        return jnp.broadcast_to(col, (col.shape[0], L))

    tot2 = jnp.where(rowsel, lane_bcast(cs2[:, L - 1:L]), lane_bcast(cs2[:, 0:1]))
    dec_scr[...] = jnp.exp2(tot2).reshape(nc, DT_ROWS, L)
    w_scr[...] = (dtv.reshape(nc * DT_ROWS, L) * jnp.exp2(tot2 - cs2)).reshape(nc, DT_ROWS, L)

    lane_t = lax.broadcasted_iota(jnp.int32, (L, L), 1)
    low_half = lane_t < P

    cw_halves = (cwx_ref[...], jnp.concatenate([cwb_ref[...], cwc_ref[...]], axis=1))
    cb_halves = (cbx_ref[...], jnp.concatenate([cbb_ref[...], cbc_ref[...]], axis=1))

    def rows(hf, start, size):
        if hf == 0:
            return x_ref[pl.ds(start, size), :]
        return jnp.concatenate([b_ref[pl.ds(start, size), :], c_ref[pl.ds(start, size), :]], axis=1)

    def conv_stage(c):
        base = pl.multiple_of(c * L, L)
        ws = pl.multiple_of(jnp.clip(c * L - CONV_WIN_LEAD, 0, seq - CONV_WIN), CONV_WIN_LEAD)
        variant = jnp.where(c == 0, 0, jnp.where(c == nc - 1, 2, 1))
        sh = sh_ref[variant]
        halves = []
        for hf in range(2):
            cw = cw_halves[hf]
            shifted = _dot(sh, rows(hf, ws, CONV_WIN))
            acc = cb_halves[hf] + cw[SSM_CONV // 2:SSM_CONV // 2 + 1, :] * rows(hf, base, L).astype(F32)
            for t, kk in enumerate(CONV_SHIFTED_TAPS):
                acc = acc + cw[kk:kk + 1, :] * shifted[t * L:(t + 1) * L, :]
            halves.append(acc * jax.nn.sigmoid(acc))
        xc = jnp.concatenate(halves, axis=1)
        xc_scr[c] = xc
        for j in range(npair):
            xp = xc[:, pair_cols[j]]
            xm_scr[c, j] = jnp.concatenate([jnp.where(low_half, xp, 0.0), jnp.where(low_half, 0.0, xp)],
                                           axis=0).astype(BF16)

    def state_loads(c):
        return xc_scr[c, :, b_cols], [xm_scr[c, j] for j in range(npair)]

    def state_stage(c, loaded):
        b_tok, xm = loaded
        bt = b_tok.T
        for d, ns_scr in ((0, nsf_scr), (1, nsb_scr)):
            w = w_scr[c, d * hpg:(d + 1) * hpg, :]
            ns_scr[c] = jnp.concatenate(
                [_dot(jnp.concatenate([(bt * w[2 * j:2 * j + 1, :]).astype(BF16),
                                       (bt * w[2 * j + 1:2 * j + 2, :]).astype(BF16)], axis=1), xm[j])
                 for j in range(npair)], axis=1)

    for c in range(CONV_AHEAD):
        conv_stage(c)

    def pass_a(c, carry):
        loaded = state_loads(c)
        conv_stage(c + CONV_AHEAD)
        state_stage(c, loaded)
        return carry
    lax.fori_loop(0, nc - CONV_AHEAD, pass_a, 0, unroll=PASS_A_UNROLL)
    for c in range(nc - CONV_AHEAD, nc):
        state_stage(c, state_loads(c))

    lane_r = lax.broadcasted_iota(jnp.int32, (1, L), 1)

    def decay_row(c, d):
        dec = dec_scr[c, d * hpg:(d + 1) * hpg, :]
        return jnp.concatenate([jnp.where(lane_r < P, dec[2 * j:2 * j + 1, :], dec[2 * j + 1:2 * j + 2, :])
                                for j in range(npair)], axis=1)

    def pass_b(d, ns_scr):
        def body(i, st):
            c = i if d == 0 else nc - 1 - i
            new = ns_scr[c]
            ns_scr[c] = st
            return st * decay_row(c, d) + new
        lax.fori_loop(0, nc, body, jnp.zeros((SSM_STATE, GROUP_INNER), F32))
    pass_b(0, nsf_scr)
    pass_b(1, nsb_scr)

    mask_f = ci <= ri
    mask_b = ci >= ri
    neg_inf = jnp.float32(-jnp.inf)
    sub16 = lax.broadcasted_iota(jnp.int32, (BF16_ROWS, L), 0)

    def pass_c(c, carry):
        base = pl.multiple_of(c * L, L)
        xc = xc_scr[c]
        bm = xc[:, b_cols].astype(BF16)
        cm = xc[:, c_cols].astype(BF16)
        cb = _dot_nt(cm, bm)
        off_f = _dot(cm, nsf_scr[c].astype(BF16))
        off_b = _dot(cm, nsb_scr[c].astype(BF16))
        slot = c % SSD_UNROLL
        for hd in range(DT_ROWS):
            r0 = (hd % 2) * SEG_K_PER_ROW
            blk = jnp.where((sub16 >= r0) & (sub16 < r0 + 3), 1.0, 0.0)
            for i in range(3):
                blk = jnp.where(sub16 == r0 + 3 + i, rpart_scr[i, c, hd:hd + 1, :], blk)
            rhs_scr[slot, (hd // 2) * BF16_ROWS:(hd // 2 + 1) * BF16_ROWS, hd * L:(hd + 1) * L] = blk.astype(BF16)
        seg = _dot(lhs_scr[c], rhs_scr[slot])
        dt_f_r = dtv_scr[c, 0:hpg, :]
        dt_b_r = dtv_scr[c, hpg:DT_ROWS, :]
        ys = []
        for j in range(npair):
            ws_ = []
            for h in (2 * j, 2 * j + 1):
                ef = jnp.exp2(jnp.where(mask_f, seg[:, h * L:(h + 1) * L], neg_inf))
                eb = jnp.exp2(jnp.where(mask_b, seg[:, (hpg + h) * L:(hpg + h + 1) * L], neg_inf))
                ws_.append((cb * (ef * dt_f_r[h:h + 1, :] + eb * dt_b_r[h:h + 1, :])).astype(BF16))
            scale_f = jnp.exp2(seg[:, (DT_ROWS + j) * L:(DT_ROWS + j + 1) * L])
            scale_b = jnp.exp2(seg[:, (DT_ROWS + npair + j) * L:(DT_ROWS + npair + j + 1) * L])
            ys.append(_dot(jnp.concatenate(ws_, axis=1), xm_scr[c, j])
                      + scale_f * off_f[:, pair_cols[j]] + scale_b * off_b[:, pair_cols[j]])
        y = jnp.concatenate(ys, axis=1) + dskip_ref[...] * xc[:, x_cols]
        o_ref[pl.ds(base, L), :] = y
        return carry
    lax.fori_loop(0, nc, pass_c, 0, unroll=SSD_UNROLL)


def _conv_shift_matrices():
    out = np.zeros((3, len(CONV_SHIFTED_TAPS) * SSM_CHUNK, CONV_WIN), np.float32)
    for v, lead in enumerate((0, CONV_WIN_LEAD, CONV_WIN - SSM_CHUNK)):
        for i, kk in enumerate(CONV_SHIFTED_TAPS):
            for t in range(SSM_CHUNK):
                j = t + kk - SSM_CONV // 2 + lead
                if 0 <= j < CONV_WIN:
                    out[v, i * SSM_CHUNK + t, j] = 1.0
    return out


def _seg_matmul_constants(nc):
    L, K, half = SSM_CHUNK, SEG_K_PER_ROW, SSM_HEAD_DIM
    place = np.zeros((3 * L, nc * L), np.float32)
    lhs_ones = np.zeros((1, nc * L), np.float32)
    for c in range(nc):
        for hd in range(DT_ROWS):
            for i in range(3):
                place[i * L + c * DT_ROWS + hd, c * L + hd * K + i] = 1.0
                lhs_ones[0, c * L + hd * K + 3 + i] = 1.0
    rhs = np.zeros((SEG_K, SEG_BLOCKS * L), np.float32)
    for hd in range(DT_ROWS):
        d, h = divmod(hd, HEADS_PER_GROUP)
        pair_block = DT_ROWS + d * (HEADS_PER_GROUP // 2) + h // 2
        lanes = slice(0, half) if h % 2 == 0 else slice(half, L)
        rhs[hd * K:hd * K + 3, hd * L:(hd + 1) * L] = 1.0
        rhs[hd * K:hd * K + 3, pair_block * L:(pair_block + 1) * L][:, lanes] = 1.0
    return place, lhs_ones, rhs


def _ssd_call(xbc, dtc, cw, cbias, adt, dskip, batch, seq):
    nc = seq // SSM_CHUNK
    L = SSM_CHUNK
    assert nc >= 3 and seq >= CONV_WIN and nc * DT_ROWS == L
    sh = jnp.asarray(_conv_shift_matrices(), BF16)
    place, lhs_ones, rhs_const = _seg_matmul_constants(nc)
    place = jnp.asarray(place, BF16)
    lhs_ones = jnp.asarray(lhs_ones, F32)
    rhs_const = jnp.asarray(rhs_const, BF16)
    const = lambda a: pl.BlockSpec(a.shape, lambda b, g: (0,) * a.ndim)
    b_blk0 = SSM_INNER // SSM_STATE
    c_blk0 = b_blk0 + SSM_GROUPS

    def xbc_views(rows_, batched):
        lead = (lambda b: b) if batched else (lambda b: 0)
        return [pl.BlockSpec((rows_, GROUP_INNER), lambda b, g: (lead(b), g)),
                pl.BlockSpec((rows_, SSM_STATE), lambda b, g: (lead(b), b_blk0 + g)),
                pl.BlockSpec((rows_, SSM_STATE), lambda b, g: (lead(b), c_blk0 + g))]
    return pl.pallas_call(
        _ssd_kernel,
        grid=(batch, SSM_GROUPS),
        in_specs=xbc_views(seq, True) + [
                  pl.BlockSpec((nc, DT_ROWS, L), lambda b, g: (b, g, 0)),
                  const(sh), const(place), const(lhs_ones), const(rhs_const)]
                 + xbc_views(SUBLANES, False) + xbc_views(1, False) + [
                  pl.BlockSpec((2, DT_ROWS, L), lambda b, g: (0, g, 0)),
                  pl.BlockSpec((1, GROUP_INNER), lambda b, g: (0, g))],
        out_specs=pl.BlockSpec((seq, GROUP_INNER), lambda b, g: (b, g)),
        out_shape=jax.ShapeDtypeStruct((batch * seq, SSM_INNER), F32),
        scratch_shapes=[pltpu.VMEM((nc, L, GROUP_COLS), F32),
                        pltpu.VMEM((nc, HEADS_PER_GROUP // 2, 2 * L, L), BF16),
                        pltpu.VMEM((nc, SSM_STATE, GROUP_INNER), F32),
                        pltpu.VMEM((nc, SSM_STATE, GROUP_INNER), F32),
                        pltpu.VMEM((nc, DT_ROWS, L), F32),
                        pltpu.VMEM((nc, DT_ROWS, L), F32),
                        pltpu.VMEM((nc, DT_ROWS, L), F32),
                        pltpu.VMEM((nc, L, SEG_K), BF16),
                        pltpu.VMEM((SSD_UNROLL, SEG_K, SEG_BLOCKS * L), BF16),
                        pltpu.VMEM((3, nc, DT_ROWS, L), F32)],
        compiler_params=pltpu.CompilerParams(dimension_semantics=("arbitrary", "arbitrary"),
                                             vmem_limit_bytes=VMEM_LIMIT),
        name="ssd",
    )(xbc, xbc, xbc, dtc, sh, place, lhs_ones, rhs_const, cw, cw, cw, cbias, cbias, cbias,
      adt, dskip)


def _mlp_kernel(x_ref, attn_ref, ssm_ref, z_ref, gat_ref, gn_ref, wo_ref, gm_ref, wup_ref, wdn_ref, o_ref):
    for sub in range(x_ref.shape[0] // MLP_SUBTILE):
        rs = slice(sub * MLP_SUBTILE, (sub + 1) * MLP_SUBTILE)
        a = attn_ref[rs, :]
        an = (a * _inv_rms(a) * gat_ref[...]).astype(BF16)
        zc = z_ref[rs, :].astype(F32)
        y = ssm_ref[rs, :] * (zc * jax.nn.sigmoid(zc))
        yn = jnp.concatenate([y[:, g * GROUP_INNER:(g + 1) * GROUP_INNER]
                              * _inv_rms(y[:, g * GROUP_INNER:(g + 1) * GROUP_INNER]) for g in range(SSM_GROUPS)],
                             axis=1) * gn_ref[...]
        mix = _dot(an, wo_ref[0:ATTN_WIDTH, :]) + _dot(yn.astype(BF16), wo_ref[ATTN_WIDTH:D_MIX, :])
        x1 = x_ref[rs, :] + mix
        hm = (x1 * _inv_rms(x1) * gm_ref[...]).astype(BF16)
        acc = jnp.zeros_like(x1)
        for c in range(D_FF // FF_CHUNK):
            cols = slice(c * FF_CHUNK, (c + 1) * FF_CHUNK)
            u = _dot(hm, wup_ref[:, cols])
            acc = acc + _dot(jnp.square(jnp.maximum(u, 0.0)).astype(BF16), wdn_ref[cols, :])
        o_ref[rs, :] = x1 + acc


def _mlp_call(x2, attn, ssm, z, gat, gn, wo, gm, wup, wdn):
    t = x2.shape[0]
    row = lambda w: pl.BlockSpec((TM_MLP, w), lambda i: (i, 0))
    full = lambda a: pl.BlockSpec(a.shape, lambda i: (0, 0), pipeline_mode=pl.Buffered(1))
    return pl.pallas_call(
        _mlp_kernel,
        grid=(t // TM_MLP,),
        in_specs=[row(D_MODEL), row(ATTN_WIDTH), row(SSM_INNER), row(SSM_INNER), full(gat), full(gn), full(wo),
                  full(gm), full(wup), full(wdn)],
        out_specs=row(D_MODEL),
        out_shape=jax.ShapeDtypeStruct((t, D_MODEL), F32),
        compiler_params=pltpu.CompilerParams(dimension_semantics=("arbitrary",),
                                             vmem_limit_bytes=VMEM_LIMIT),
        name="outproj_mlp",
    )(x2, attn, ssm, z, gat, gn, wo, gm, wup, wdn)


def _swap_halves(g):
    half = QK_ROPE_DIM // 2
    return jnp.concatenate([g[..., half:], g[..., :half]], axis=-1)


def _lane_bcast(v):
    return jnp.broadcast_to(v[..., None], v.shape + (LANES,)).astype(F32)


def _layer(x2, cs, batch, seq, ln_mix_g, stacked_proj_weights, q_a_norm_g, kv_a_norm_g, q_norm_g,
           k_norm_g, attn_out_norm_g, conv_w, conv_b, a_log_fwd, a_log_bwd, dt_bias_fwd, dt_bias_bwd,
           d_skip, ssm_norm_g, w_out, ln_mlp_g, w_mlp_up, w_mlp_down):
    win, wuq, wuk, wuv = _win_prep_call(*stacked_proj_weights)
    ones_v = np.ones((ATTN_HEADS // 2, V_HEAD_DIM), np.float32)
    vone = jnp.asarray(np.stack([np.concatenate([0 * ones_v, ones_v], axis=-1),
                                 np.concatenate([ones_v, 0 * ones_v], axis=-1)], axis=1).reshape(1, -1))
    scale = QK_HEAD_DIM ** -0.5 * np.log2(np.e)
    zeros_nope = jnp.zeros((QK_NOPE_DIM,), k_norm_g.dtype)
    k_rope_g = k_norm_g[QK_NOPE_DIM:]
    hg = jnp.concatenate([q_norm_g * scale, _swap_halves(q_norm_g[QK_NOPE_DIM:]) * scale,
                          k_norm_g[:QK_NOPE_DIM], zeros_nope,
                          zeros_nope, k_rope_g, _swap_halves(k_rope_g),
                          jnp.zeros(((SUBLANES - 3) * HEAD_PAD,), k_norm_g.dtype)]).reshape(SUBLANES, HEAD_PAD)

    q, k, v, z, xbc, dtc = _inproj_call(
        x2, cs, ln_mix_g[None, :], win, q_a_norm_g[None, :], wuq, kv_a_norm_g[None, :], wuk, wuv, vone,
        hg)

    attn, (wo_b, wup_b, wdn_b) = _attn_call(q, k, v, (w_out, w_mlp_up, w_mlp_down), batch, seq)

    cw = jnp.pad(conv_w[:, 0, :], ((0, SUBLANES - SSM_CONV), (0, 0)))
    cbias = conv_b[None, :]
    adt = jnp.stack([a_log_fwd, a_log_bwd, dt_bias_fwd, dt_bias_bwd]).reshape(2, 2, SSM_GROUPS, HEADS_PER_GROUP)
    adt = _lane_bcast(adt.transpose(0, 2, 1, 3).reshape(2, 2 * SSM_HEADS))
    dskip = jnp.repeat(d_skip, SSM_HEAD_DIM)[None, :]
    ssm = _ssd_call(xbc, dtc, cw, cbias, adt, dskip, batch, seq)

    return _mlp_call(x2, attn, ssm, z, attn_out_norm_g[None, :], ssm_norm_g[None, :], wo_b,
                     ln_mlp_g[None, :], wup_b, wdn_b)


def _rope_table(positions):
    inv_freq = 1.0 / (ROPE_THETA ** (jnp.arange(0, QK_ROPE_DIM, 2, dtype=F32) / QK_ROPE_DIM))
    ang = inv_freq[:, None] * positions.astype(F32).reshape(1, -1)
    return jnp.concatenate([jnp.cos(ang), jnp.sin(ang)], axis=0)


def kernel(x, positions, ln_mix_g, w_in, q_a_norm_g, w_uq, kv_a_norm_g, w_ukv, q_norm_g, k_norm_g,
           attn_out_norm_g, conv_w, conv_b, a_log_fwd, a_log_bwd, dt_bias_fwd, dt_bias_bwd, d_skip,
           ssm_norm_g, w_out, ln_mlp_g, w_mlp_up, w_mlp_down):
    batch, seq, d = x.shape
    assert d == D_MODEL and seq % TQ == 0 and (batch * seq) % TM_IN == 0 and (batch * seq) % TM_MLP == 0
    cs = _rope_table(positions)
    x2 = x.reshape(batch * seq, d)
    for l in range(ln_mix_g.shape[0]):
        x2 = _layer(x2, cs, batch, seq, ln_mix_g[l], (w_in, w_uq, w_ukv, l), q_a_norm_g[l], kv_a_norm_g[l],
                    q_norm_g[l], k_norm_g[l], attn_out_norm_g[l], conv_w[l], conv_b[l],
                    a_log_fwd[l], a_log_bwd[l], dt_bias_fwd[l], dt_bias_bwd[l], d_skip[l], ssm_norm_g[l],
                    w_out[l], ln_mlp_g[l], w_mlp_up[l], w_mlp_down[l])
    return x2.reshape(batch, seq, d)
```

```python
import numpy as np
import jax
import jax.numpy as jnp
from jax import lax
from jax.experimental import pallas as pl
from jax.experimental.pallas import tpu as pltpu

F32 = jnp.float32
BF16 = jnp.bfloat16

D_MODEL = 1024
ATTN_HEADS = 8
QK_NOPE_DIM = 64
QK_ROPE_DIM = 32
QK_HEAD_DIM = QK_NOPE_DIM + QK_ROPE_DIM
V_HEAD_DIM = 64
Q_LORA_RANK = D_MODEL // 4
KV_LORA_RANK = D_MODEL // 8
ROPE_THETA = 10000.0
ATTN_WIDTH = ATTN_HEADS * V_HEAD_DIM
SSM_HEADS = 8
SSM_HEAD_DIM = 64
SSM_INNER = SSM_HEADS * SSM_HEAD_DIM
SSM_GROUPS = 2
SSM_STATE = 128
SSM_CONV = 5
SSM_CHUNK = 128
SSM_CONV_CH = SSM_INNER + 2 * SSM_GROUPS * SSM_STATE
D_MIX = ATTN_WIDTH + SSM_INNER
D_FF = 4 * D_MODEL
EPS = 1e-6

LANES = 128
SUBLANES = 8
HEAD_PAD = LANES

HEADS_PER_GROUP = SSM_HEADS // SSM_GROUPS
GROUP_INNER = SSM_INNER // SSM_GROUPS
GROUP_COLS = GROUP_INNER + 2 * SSM_STATE
DT_ROWS = 2 * HEADS_PER_GROUP
BF16_ROWS = 16
SEG_K_PER_ROW = 8
SEG_K = DT_ROWS * SEG_K_PER_ROW
SEG_BLOCKS = DT_ROWS + HEADS_PER_GROUP
LOG2E = float(np.log2(np.e))
CONV_WIN = 2 * SSM_CHUNK
CONV_WIN_LEAD = SSM_CHUNK // 2
CONV_SHIFTED_TAPS = tuple(k for k in range(SSM_CONV) if k != SSM_CONV // 2)

COL_CKV = Q_LORA_RANK
COL_MISC = COL_CKV + KV_LORA_RANK
COL_Z = COL_MISC + LANES
COL_XBC = COL_Z + SSM_INNER
IN_COLS = COL_XBC + SSM_CONV_CH

TM_IN = 1024
IN_SUBTILE = 512
TQ = 1024
ATTN_HEADS_PER_STEP = 8
TM_MLP = 1024
MLP_SUBTILE = 512
FF_CHUNK = 1024
SSD_UNROLL = 16
CONV_AHEAD = 4
PASS_A_UNROLL = 12
WIN_PREP_STEPS = 4
VMEM_LIMIT = 56 * 1024 * 1024


def _inv_rms(x):
    n = x.shape[-1]
    x2 = x * x
    acc = x2[:, 0:LANES]
    for i in range(1, n // LANES):
        acc = acc + x2[:, i * LANES:(i + 1) * LANES]
    return lax.rsqrt(jnp.sum(acc, axis=-1, keepdims=True) * (1.0 / n) + EPS)


def _dot(a, b):
    return jnp.dot(a, b, preferred_element_type=F32)


def _dot_nt(a, b):
    return lax.dot_general(a, b, (((1,), (1,)), ((), ())), preferred_element_type=F32)


def _inproj_kernel(x_ref, cs_ref, g_ref, win_ref, gqa_ref, wuq_ref, gkva_ref, wuk_ref, wuv_ref, vone_ref,
                   hg_ref,
                   q_ref, k_ref, v_ref, z_ref, xbc_ref, dt_ref):
    for sub in range(x_ref.shape[0] // IN_SUBTILE):
        _inproj_rows(sub * IN_SUBTILE, x_ref, cs_ref, g_ref, win_ref, gqa_ref, wuq_ref, gkva_ref, wuk_ref,
                     wuv_ref, vone_ref, hg_ref, q_ref, k_ref, v_ref, z_ref, xbc_ref, dt_ref)


def _inproj_rows(r0, x_ref, cs_ref, g_ref, win_ref, gqa_ref, wuq_ref, gkva_ref, wuk_ref, wuv_ref, vone_ref,
                 hg_ref, q_ref, k_ref, v_ref, z_ref, xbc_ref, dt_ref):
    tm = IN_SUBTILE
    rs = slice(r0, r0 + tm)
    x = x_ref[rs, :]
    h = (x * _inv_rms(x) * g_ref[...]).astype(BF16)
    big = _dot(h, win_ref[...])
    z_ref[rs, :] = big[:, COL_Z:COL_XBC].astype(BF16)
    xbc_ref[rs, :] = big[:, COL_XBC:IN_COLS].astype(BF16)
    misc = big[:, COL_MISC:COL_Z]
    for c in range(tm // SSM_CHUNK):
        dt_ref[r0 // SSM_CHUNK + c] = misc[c * SSM_CHUNK:(c + 1) * SSM_CHUNK, :].T[0:2 * SSM_HEADS, :]

    cq = big[:, 0:COL_CKV]
    ckv = big[:, COL_CKV:COL_MISC]
    cqn = (cq * _inv_rms(cq) * gqa_ref[...]).astype(BF16)
    ckvn = (ckv * _inv_rms(ckv) * gkva_ref[...]).astype(BF16)
    q_pre = _dot(cqn, wuq_ref[...])
    k_pre = _dot(ckvn, wuk_ref[...])
    v_ref[rs, :] = (_dot(ckvn, wuv_ref[...]) + vone_ref[...]).astype(BF16)

    half = QK_ROPE_DIM // 2
    zpad = jnp.zeros((HEAD_PAD - QK_ROPE_DIM, SSM_CHUNK), F32)
    tt = jnp.concatenate(
        [jnp.concatenate([cs_ref[:, r0 + c * SSM_CHUNK:r0 + (c + 1) * SSM_CHUNK], zpad], axis=0).T
         for c in range(tm // SSM_CHUNK)], axis=0)
    lane_t = lax.broadcasted_iota(jnp.int32, (tm, HEAD_PAD), 1)
    cs = jnp.where(lane_t < QK_NOPE_DIM, 1.0,
                   jnp.where(lane_t < QK_NOPE_DIM + half, pltpu.roll(tt, QK_NOPE_DIM, 1),
                             jnp.where(lane_t < QK_HEAD_DIM + half, pltpu.roll(tt, QK_NOPE_DIM + half, 1),
                                       pltpu.roll(tt, QK_HEAD_DIM, 1))))
    lane = lax.broadcasted_iota(jnp.int32, (1, HEAD_PAD), 1)
    in_head = (lane < QK_HEAD_DIM).astype(F32)
    is_rope = ((lane >= QK_NOPE_DIM) & (lane < QK_HEAD_DIM)).astype(F32)
    inv_d = 1.0 / QK_HEAD_DIM

    gcq = hg_ref[0:1, :] * cs
    for hh in range(ATTN_HEADS):
        sl = slice(hh * HEAD_PAD, (hh + 1) * HEAD_PAD)
        qh = q_pre[:, sl]
        ssq = jnp.sum(qh * qh * in_head, axis=-1, keepdims=True)
        q_ref[rs, sl] = (qh * lax.rsqrt(ssq * inv_d + EPS) * gcq).astype(BF16)

    ab = misc * (hg_ref[2:3, :] * cs)
    lane2 = lax.broadcasted_iota(jnp.int32, (tm, HEAD_PAD), 1)
    swapped = jnp.where(lane2 < QK_HEAD_DIM, pltpu.roll(ab, HEAD_PAD - QK_ROPE_DIM, 1),
                        pltpu.roll(ab, QK_ROPE_DIM, 1))
    s_both = jnp.where(lane2 >= QK_NOPE_DIM, ab + swapped, 0.0)
    ssq_pe = jnp.sum(misc * misc * is_rope, axis=-1, keepdims=True)
    gkn = hg_ref[1:2, :]
    for hh in range(ATTN_HEADS):
        sl = slice(hh * HEAD_PAD, (hh + 1) * HEAD_PAD)
        kh = k_pre[:, sl]
        ssq = jnp.sum(kh * kh, axis=-1, keepdims=True) + ssq_pe
        k_ref[rs, sl] = ((kh * gkn + s_both) * lax.rsqrt(ssq * inv_d + EPS)).astype(BF16)


def _win_prep_kernel(w_ref, wq_ref, wkv_ref, o_ref, wuq_ref, wuk_ref, wuv_ref):
    half = QK_ROPE_DIM // 2

    @pl.when(pl.program_id(0) == 0)
    def _up_projection_tiles():
        wq = wq_ref[...]
        q_pieces = []
        for hh in range(ATTN_HEADS):
            head = wq[:, hh * QK_HEAD_DIM:(hh + 1) * QK_HEAD_DIM]
            rope = head[:, QK_NOPE_DIM:]
            q_pieces += [head, -rope[:, half:], rope[:, :half]]
        wuq_ref[...] = jnp.concatenate(q_pieces, axis=1).astype(BF16)
        wkv = wkv_ref[...]
        zeros = jnp.zeros((wkv.shape[0], V_HEAD_DIM), F32)
        k_pieces, v_pieces = [], []
        for hh in range(ATTN_HEADS):
            nope = wkv[:, hh * HEAD_PAD:hh * HEAD_PAD + QK_NOPE_DIM]
            val = wkv[:, hh * HEAD_PAD + QK_NOPE_DIM:(hh + 1) * HEAD_PAD]
            k_pieces += [nope, zeros]
            v_pieces += [val, zeros] if hh % 2 == 0 else [zeros, val]
        wuk_ref[...] = jnp.concatenate(k_pieces, axis=1).astype(BF16)
        wuv_ref[...] = jnp.concatenate(v_pieces, axis=1).astype(BF16)

    w = w_ref[...].astype(F32)
    o_kpe = Q_LORA_RANK + KV_LORA_RANK
    o_z = o_kpe + QK_ROPE_DIM
    o_dt = o_z + SSM_INNER + SSM_CONV_CH
    hpg = HEADS_PER_GROUP
    dt_f = w[:, o_dt:o_dt + SSM_HEADS]
    dt_b = w[:, o_dt + SSM_HEADS:o_dt + 2 * SSM_HEADS]
    kpe = w[:, o_kpe:o_z]
    pieces = [w[:, 0:o_kpe]]
    for g in range(SSM_GROUPS):
        pieces += [dt_f[:, g * hpg:(g + 1) * hpg], dt_b[:, g * hpg:(g + 1) * hpg]]
    pieces += [jnp.zeros((w.shape[0], QK_NOPE_DIM - 2 * SSM_HEADS), F32), kpe, -kpe[:, half:], kpe[:, :half],
               w[:, o_z:o_dt]]
    o_ref[...] = jnp.concatenate(pieces, axis=1).astype(BF16)


def _win_prep_call(w_in3, w_uq3, w_ukv3, layer):
    depth, d, n = w_in3.shape
    rows = d // WIN_PREP_STEPS
    whole = lambda a: pl.BlockSpec(a.shape[1:], lambda i: (layer, 0))
    head_cols = ATTN_HEADS * HEAD_PAD
    return pl.pallas_call(
        _win_prep_kernel,
        grid=(WIN_PREP_STEPS,),
        in_specs=[pl.BlockSpec((rows, n), lambda i: (layer * WIN_PREP_STEPS + i, 0)), whole(w_uq3), whole(w_ukv3)],
        out_specs=[pl.BlockSpec((rows, IN_COLS), lambda i: (i, 0)),
                   pl.BlockSpec((Q_LORA_RANK, head_cols), lambda i: (0, 0)),
                   pl.BlockSpec((KV_LORA_RANK, head_cols), lambda i: (0, 0)),
                   pl.BlockSpec((KV_LORA_RANK, head_cols), lambda i: (0, 0))],
        out_shape=[jax.ShapeDtypeStruct((d, IN_COLS), BF16),
                   jax.ShapeDtypeStruct((Q_LORA_RANK, head_cols), BF16),
                   jax.ShapeDtypeStruct((KV_LORA_RANK, head_cols), BF16),
                   jax.ShapeDtypeStruct((KV_LORA_RANK, head_cols), BF16)],
        compiler_params=pltpu.CompilerParams(dimension_semantics=("arbitrary",), vmem_limit_bytes=VMEM_LIMIT),
        name="win_prep",
    )(w_in3.reshape(depth * d, n).astype(BF16),
      w_uq3.reshape(depth * w_uq3.shape[1], w_uq3.shape[2]),
      w_ukv3.reshape(depth * w_ukv3.shape[1], w_ukv3.shape[2]))


def _inproj_call(x2, cs, g, win, gqa, wuq, gkva, wuk, wuv, vone, hg):
    t = x2.shape[0]
    cpt = TM_IN // SSM_CHUNK
    full = lambda a: pl.BlockSpec(a.shape, lambda i: (0,) * a.ndim, pipeline_mode=pl.Buffered(1))
    row = lambda w: pl.BlockSpec((TM_IN, w), lambda i: (i, 0))
    return pl.pallas_call(
        _inproj_kernel,
        grid=(t // TM_IN,),
        in_specs=[row(D_MODEL), pl.BlockSpec((QK_ROPE_DIM, TM_IN), lambda i: (0, i)), full(g), full(win), full(gqa), full(wuq), full(gkva),
                  full(wuk), full(wuv), full(vone), full(hg)],
        out_specs=[row(ATTN_HEADS * HEAD_PAD), row(ATTN_HEADS * HEAD_PAD), row(ATTN_HEADS * HEAD_PAD),
                   row(SSM_INNER), row(SSM_CONV_CH),
                   pl.BlockSpec((cpt, 2 * SSM_HEADS, SSM_CHUNK), lambda i: (i, 0, 0))],
        out_shape=[jax.ShapeDtypeStruct((t, ATTN_HEADS * HEAD_PAD), BF16),
                   jax.ShapeDtypeStruct((t, ATTN_HEADS * HEAD_PAD), BF16),
                   jax.ShapeDtypeStruct((t, ATTN_HEADS * HEAD_PAD), BF16),
                   jax.ShapeDtypeStruct((t, SSM_INNER), BF16),
                   jax.ShapeDtypeStruct((t, SSM_CONV_CH), BF16),
                   jax.ShapeDtypeStruct((t // SSM_CHUNK, 2 * SSM_HEADS, SSM_CHUNK), F32)],
        compiler_params=pltpu.CompilerParams(dimension_semantics=("arbitrary",),
                                             vmem_limit_bytes=VMEM_LIMIT),
        name="inproj",
    )(x2, cs, g, win, gqa, wuq, gkva, wuk, wuv, vone, hg)


def _attn_kernel(q_ref, k_ref, v_ref, *refs):
    n_w = (len(refs) - 1) // 2
    w_in_refs, o_ref, w_out_refs = refs[:n_w], refs[n_w], refs[n_w + 1:]
    for w_in_ref, w_out_ref in zip(w_in_refs, w_out_refs):
        w_out_ref[...] = w_in_ref[...].astype(BF16)

    lane = lax.broadcasted_iota(jnp.int32, (q_ref.shape[0], HEAD_PAD), 1)
    for jp in range(ATTN_HEADS_PER_STEP // 2):
        sls = [slice(j * HEAD_PAD, (j + 1) * HEAD_PAD) for j in (2 * jp, 2 * jp + 1)]
        ss = [_dot_nt(q_ref[:, sl], k_ref[:, sl]) for sl in sls]
        ps = [jnp.exp2(s - jnp.max(s, axis=-1, keepdims=True)).astype(BF16) for s in ss]
        accs = [_dot(p, v_ref[:, sl]) for p, sl in zip(ps, sls)]
        res = [acc / pltpu.roll(acc, V_HEAD_DIM, 1) for acc in accs]
        o_ref[:, jp * HEAD_PAD:(jp + 1) * HEAD_PAD] = jnp.where(lane < V_HEAD_DIM, res[0], res[1])


def _attn_call(q, k, v, later_weights, batch, seq):
    n_q = seq // TQ
    hps = ATTN_HEADS_PER_STEP
    assert ATTN_HEADS == hps
    n_steps = batch * n_q

    def rows_spec(w):
        assert w.shape[0] % (n_steps * BF16_ROWS) == 0
        return pl.BlockSpec((w.shape[0] // n_steps, w.shape[1]), lambda b, hp, i: (b * n_q + i, 0))
    w_specs = [rows_spec(w) for w in later_weights]
    outs = pl.pallas_call(
        _attn_kernel,
        grid=(batch, ATTN_HEADS // hps, n_q),
        in_specs=[pl.BlockSpec((TQ, hps * HEAD_PAD), lambda b, hp, i: (b * n_q + i, hp)),
                  pl.BlockSpec((seq, hps * HEAD_PAD), lambda b, hp, i: (b, hp)),
                  pl.BlockSpec((seq, hps * HEAD_PAD), lambda b, hp, i: (b, hp))] + w_specs,
        out_specs=[pl.BlockSpec((TQ, hps * V_HEAD_DIM), lambda b, hp, i: (b * n_q + i, hp))] + w_specs,
        out_shape=[jax.ShapeDtypeStruct((batch * seq, ATTN_WIDTH), F32)]
                  + [jax.ShapeDtypeStruct(w.shape, BF16) for w in later_weights],
        compiler_params=pltpu.CompilerParams(
            dimension_semantics=("arbitrary", "arbitrary", "arbitrary"),
            vmem_limit_bytes=VMEM_LIMIT),
        name="attention",
    )(q, k, v, *later_weights)
    return outs[0], outs[1:]


def _split3(x):
    hi = x.astype(BF16)
    r1 = x - hi.astype(F32)
    mid = r1.astype(BF16)
    lo = (r1 - mid.astype(F32)).astype(BF16)
    return hi, mid, lo


def _ssd_kernel(x_ref, b_ref, c_ref, dt_ref, sh_ref, place_ref, lhs_ones_ref, rhs_const_ref,
                cwx_ref, cwb_ref, cwc_ref, cbx_ref, cbb_ref, cbc_ref, adt_ref, dskip_ref,
                o_ref, xc_scr, xm_scr, nsf_scr, nsb_scr, dtv_scr, w_scr, dec_scr, lhs_scr, rhs_scr, rpart_scr):
    nc = dt_ref.shape[0]
    seq = x_ref.shape[0]
    L = SSM_CHUNK
    hpg = HEADS_PER_GROUP
    npair = hpg // 2
    P = SSM_HEAD_DIM
    x_cols = slice(0, GROUP_INNER)
    b_cols = slice(GROUP_INNER, GROUP_INNER + SSM_STATE)
    c_cols = slice(GROUP_INNER + SSM_STATE, GROUP_COLS)
    pair_cols = [slice(j * L, (j + 1) * L) for j in range(npair)]

    @pl.when((pl.program_id(0) == 0) & (pl.program_id(1) == 0))
    def _init_rhs():
        for u in range(SSD_UNROLL):
            rhs_scr[u] = rhs_const_ref[...]

    a_neg = -jnp.exp(adt_ref[0]) * LOG2E
    dtv = jax.nn.softplus(dt_ref[...] + adt_ref[1][None])
    dtv_scr[...] = dtv
    da2 = (dtv * a_neg[None]).reshape(nc * DT_ROWS, L)
    ri = lax.broadcasted_iota(jnp.int32, (L, L), 0)
    ci = lax.broadcasted_iota(jnp.int32, (L, L), 1)
    upper = (ri <= ci).astype(BF16)
    lower = (ri >= ci).astype(BF16)
    tri = jnp.concatenate([upper, lower], axis=1)
    cs_fb = sum(_dot(p, tri) for p in _split3(da2))
    rowsel = (ri & hpg) == 0
    cs2 = jnp.where(rowsel, cs_fb[:, 0:L], cs_fb[:, L:2 * L])
    colcs = cs2.T

    col_parts = jnp.concatenate(_split3(colcs), axis=1)
    half_n = place_ref.shape[1] // 2
    for hf in range(2):
        cols = slice(hf * half_n, (hf + 1) * half_n)
        lhs_half = _dot(col_parts, place_ref[:, cols]) + lhs_ones_ref[:, cols]
        for c in range(nc // 2):
            lhs_scr[hf * (nc // 2) + c] = lhs_half[:, c * L:c * L + SEG_K].astype(BF16)
    for i, part in enumerate(_split3(-cs2)):
        rpart_scr[i] = part.astype(F32).reshape(nc, DT_ROWS, L)

    def lane_bcast(col):
        return jnp.broadcast_to(col, (col.shape[0], L))

    tot2 = jnp.where(rowsel, lane_bcast(cs2[:, L - 1:L]), lane_bcast(cs2[:, 0:1]))
    dec_scr[...] = jnp.exp2(tot2).reshape(nc, DT_ROWS, L)
    w_scr[...] = (dtv.reshape(nc * DT_ROWS, L) * jnp.exp2(tot2 - cs2)).reshape(nc, DT_ROWS, L)

    lane_t = lax.broadcasted_iota(jnp.int32, (L, L), 1)
    low_half = lane_t < P

    cw_halves = (cwx_ref[...], jnp.concatenate([cwb_ref[...], cwc_ref[...]], axis=1))
    cb_halves = (cbx_ref[...], jnp.concatenate([cbb_ref[...], cbc_ref[...]], axis=1))

    def rows(hf, start, size):
        if hf == 0:
            return x_ref[pl.ds(start, size), :]
        return jnp.concatenate([b_ref[pl.ds(start, size), :], c_ref[pl.ds(start, size), :]], axis=1)

    def conv_stage(c):
        base = pl.multiple_of(c * L, L)
        ws = pl.multiple_of(jnp.clip(c * L - CONV_WIN_LEAD, 0, seq - CONV_WIN), CONV_WIN_LEAD)
        variant = jnp.where(c == 0, 0, jnp.where(c == nc - 1, 2, 1))
        sh = sh_ref[variant]
        halves = []
        for hf in range(2):
            cw = cw_halves[hf]
            shifted = _dot(sh, rows(hf, ws, CONV_WIN))
            acc = cb_halves[hf] + cw[SSM_CONV // 2:SSM_CONV // 2 + 1, :] * rows(hf, base, L).astype(F32)
            for t, kk in enumerate(CONV_SHIFTED_TAPS):
                acc = acc + cw[kk:kk + 1, :] * shifted[t * L:(t + 1) * L, :]
            halves.append(acc * jax.nn.sigmoid(acc))
        xc = jnp.concatenate(halves, axis=1)
        xc_scr[c] = xc
        for j in range(npair):
            xp = xc[:, pair_cols[j]]
            xm_scr[c, j] = jnp.concatenate([jnp.where(low_half, xp, 0.0), jnp.where(low_half, 0.0, xp)],
                                           axis=0).astype(BF16)

    def state_loads(c):
        return xc_scr[c, :, b_cols], [xm_scr[c, j] for j in range(npair)]

    def state_stage(c, loaded):
        b_tok, xm = loaded
        bt = b_tok.T
        for d, ns_scr in ((0, nsf_scr), (1, nsb_scr)):
            w = w_scr[c, d * hpg:(d + 1) * hpg, :]
            ns_scr[c] = jnp.concatenate(
                [_dot(jnp.concatenate([(bt * w[2 * j:2 * j + 1, :]).astype(BF16),
                                       (bt * w[2 * j + 1:2 * j + 2, :]).astype(BF16)], axis=1), xm[j])
                 for j in range(npair)], axis=1)

    for c in range(CONV_AHEAD):
        conv_stage(c)

    def pass_a(c, carry):
        loaded = state_loads(c)
        conv_stage(c + CONV_AHEAD)
        state_stage(c, loaded)
        return carry
    lax.fori_loop(0, nc - CONV_AHEAD, pass_a, 0, unroll=PASS_A_UNROLL)
    for c in range(nc - CONV_AHEAD, nc):
        state_stage(c, state_loads(c))

    lane_r = lax.broadcasted_iota(jnp.int32, (1, L), 1)

    def decay_row(c, d):
        dec = dec_scr[c, d * hpg:(d + 1) * hpg, :]
        return jnp.concatenate([jnp.where(lane_r < P, dec[2 * j:2 * j + 1, :], dec[2 * j + 1:2 * j + 2, :])
                                for j in range(npair)], axis=1)

    def pass_b(d, ns_scr):
        def body(i, st):
            c = i if d == 0 else nc - 1 - i
            new = ns_scr[c]
            ns_scr[c] = st
            return st * decay_row(c, d) + new
        lax.fori_loop(0, nc, body, jnp.zeros((SSM_STATE, GROUP_INNER), F32))
    pass_b(0, nsf_scr)
    pass_b(1, nsb_scr)

    mask_f = ci <= ri
    mask_b = ci >= ri
    neg_inf = jnp.float32(-jnp.inf)
    sub16 = lax.broadcasted_iota(jnp.int32, (BF16_ROWS, L), 0)

    def pass_c(c, carry):
        base = pl.multiple_of(c * L, L)
        xc = xc_scr[c]
        bm = xc[:, b_cols].astype(BF16)
        cm = xc[:, c_cols].astype(BF16)
        cb = _dot_nt(cm, bm)
        off_f = _dot(cm, nsf_scr[c].astype(BF16))
        off_b = _dot(cm, nsb_scr[c].astype(BF16))
        slot = c % SSD_UNROLL
        for hd in range(DT_ROWS):
            r0 = (hd % 2) * SEG_K_PER_ROW
            blk = jnp.where((sub16 >= r0) & (sub16 < r0 + 3), 1.0, 0.0)
            for i in range(3):
                blk = jnp.where(sub16 == r0 + 3 + i, rpart_scr[i, c, hd:hd + 1, :], blk)
            rhs_scr[slot, (hd // 2) * BF16_ROWS:(hd // 2 + 1) * BF16_ROWS, hd * L:(hd + 1) * L] = blk.astype(BF16)
        seg = _dot(lhs_scr[c], rhs_scr[slot])
        dt_f_r = dtv_scr[c, 0:hpg, :]
        dt_b_r = dtv_scr[c, hpg:DT_ROWS, :]
        ys = []
        for j in range(npair):
            ws_ = []
            for h in (2 * j, 2 * j + 1):
                ef = jnp.exp2(jnp.where(mask_f, seg[:, h * L:(h + 1) * L], neg_inf))
                eb = jnp.exp2(jnp.where(mask_b, seg[:, (hpg + h) * L:(hpg + h + 1) * L], neg_inf))
                ws_.append((cb * (ef * dt_f_r[h:h + 1, :] + eb * dt_b_r[h:h + 1, :])).astype(BF16))
            scale_f = jnp.exp2(seg[:, (DT_ROWS + j) * L:(DT_ROWS + j + 1) * L])
            scale_b = jnp.exp2(seg[:, (DT_ROWS + npair + j) * L:(DT_ROWS + npair + j + 1) * L])
            ys.append(_dot(jnp.concatenate(ws_, axis=1), xm_scr[c, j])
                      + scale_f * off_f[:, pair_cols[j]] + scale_b * off_b[:, pair_cols[j]])
        y = jnp.concatenate(ys, axis=1) + dskip_ref[...] * xc[:, x_cols]
        o_ref[pl.ds(base, L), :] = y
        return carry
    lax.fori_loop(0, nc, pass_c, 0, unroll=SSD_UNROLL)


def _conv_shift_matrices():
    out = np.zeros((3, len(CONV_SHIFTED_TAPS) * SSM_CHUNK, CONV_WIN), np.float32)
    for v, lead in enumerate((0, CONV_WIN_LEAD, CONV_WIN - SSM_CHUNK)):
        for i, kk in enumerate(CONV_SHIFTED_TAPS):
            for t in range(SSM_CHUNK):
                j = t + kk - SSM_CONV // 2 + lead
                if 0 <= j < CONV_WIN:
                    out[v, i * SSM_CHUNK + t, j] = 1.0
    return out


def _seg_matmul_constants(nc):
    L, K, half = SSM_CHUNK, SEG_K_PER_ROW, SSM_HEAD_DIM
    place = np.zeros((3 * L, nc * L), np.float32)
    lhs_ones = np.zeros((1, nc * L), np.float32)
    for c in range(nc):
        for hd in range(DT_ROWS):
            for i in range(3):
                place[i * L + c * DT_ROWS + hd, c * L + hd * K + i] = 1.0
                lhs_ones[0, c * L + hd * K + 3 + i] = 1.0
    rhs = np.zeros((SEG_K, SEG_BLOCKS * L), np.float32)
    for hd in range(DT_ROWS):
        d, h = divmod(hd, HEADS_PER_GROUP)
        pair_block = DT_ROWS + d * (HEADS_PER_GROUP // 2) + h // 2
        lanes = slice(0, half) if h % 2 == 0 else slice(half, L)
        rhs[hd * K:hd * K + 3, hd * L:(hd + 1) * L] = 1.0
        rhs[hd * K:hd * K + 3, pair_block * L:(pair_block + 1) * L][:, lanes] = 1.0
    return place, lhs_ones, rhs


def _ssd_call(xbc, dtc, cw, cbias, adt, dskip, batch, seq):
    nc = seq // SSM_CHUNK
    L = SSM_CHUNK
    assert nc >= 3 and seq >= CONV_WIN and nc * DT_ROWS == L
    sh = jnp.asarray(_conv_shift_matrices(), BF16)
    place, lhs_ones, rhs_const = _seg_matmul_constants(nc)
    place = jnp.asarray(place, BF16)
    lhs_ones = jnp.asarray(lhs_ones, F32)
    rhs_const = jnp.asarray(rhs_const, BF16)
    const = lambda a: pl.BlockSpec(a.shape, lambda b, g: (0,) * a.ndim)
    b_blk0 = SSM_INNER // SSM_STATE
    c_blk0 = b_blk0 + SSM_GROUPS

    def xbc_views(rows_, batched):
        lead = (lambda b: b) if batched else (lambda b: 0)
        return [pl.BlockSpec((rows_, GROUP_INNER), lambda b, g: (lead(b), g)),
                pl.BlockSpec((rows_, SSM_STATE), lambda b, g: (lead(b), b_blk0 + g)),
                pl.BlockSpec((rows_, SSM_STATE), lambda b, g: (lead(b), c_blk0 + g))]
    return pl.pallas_call(
        _ssd_kernel,
        grid=(batch, SSM_GROUPS),
        in_specs=xbc_views(seq, True) + [
                  pl.BlockSpec((nc, DT_ROWS, L), lambda b, g: (b, g, 0)),
                  const(sh), const(place), const(lhs_ones), const(rhs_const)]
                 + xbc_views(SUBLANES, False) + xbc_views(1, False) + [
                  pl.BlockSpec((2, DT_ROWS, L), lambda b, g: (0, g, 0)),
                  pl.BlockSpec((1, GROUP_INNER), lambda b, g: (0, g))],
        out_specs=pl.BlockSpec((seq, GROUP_INNER), lambda b, g: (b, g)),
        out_shape=jax.ShapeDtypeStruct((batch * seq, SSM_INNER), F32),
        scratch_shapes=[pltpu.VMEM((nc, L, GROUP_COLS), F32),
                        pltpu.VMEM((nc, HEADS_PER_GROUP // 2, 2 * L, L), BF16),
                        pltpu.VMEM((nc, SSM_STATE, GROUP_INNER), F32),
                        pltpu.VMEM((nc, SSM_STATE, GROUP_INNER), F32),
                        pltpu.VMEM((nc, DT_ROWS, L), F32),
                        pltpu.VMEM((nc, DT_ROWS, L), F32),
                        pltpu.VMEM((nc, DT_ROWS, L), F32),
                        pltpu.VMEM((nc, L, SEG_K), BF16),
                        pltpu.VMEM((SSD_UNROLL, SEG_K, SEG_BLOCKS * L), BF16),
                        pltpu.VMEM((3, nc, DT_ROWS, L), F32)],
        compiler_params=pltpu.CompilerParams(dimension_semantics=("arbitrary", "arbitrary"),
                                             vmem_limit_bytes=VMEM_LIMIT),
        name="ssd",
    )(xbc, xbc, xbc, dtc, sh, place, lhs_ones, rhs_const, cw, cw, cw, cbias, cbias, cbias,
      adt, dskip)


def _mlp_kernel(x_ref, attn_ref, ssm_ref, z_ref, gat_ref, gn_ref, wo_ref, gm_ref, wup_ref, wdn_ref, o_ref):
    for sub in range(x_ref.shape[0] // MLP_SUBTILE):
        rs = slice(sub * MLP_SUBTILE, (sub + 1) * MLP_SUBTILE)
        a = attn_ref[rs, :]
        an = (a * _inv_rms(a) * gat_ref[...]).astype(BF16)
        zc = z_ref[rs, :].astype(F32)
        y = ssm_ref[rs, :] * (zc * jax.nn.sigmoid(zc))
        yn = jnp.concatenate([y[:, g * GROUP_INNER:(g + 1) * GROUP_INNER]
                              * _inv_rms(y[:, g * GROUP_INNER:(g + 1) * GROUP_INNER]) for g in range(SSM_GROUPS)],
                             axis=1) * gn_ref[...]
        mix = _dot(an, wo_ref[0:ATTN_WIDTH, :]) + _dot(yn.astype(BF16), wo_ref[ATTN_WIDTH:D_MIX, :])
        x1 = x_ref[rs, :] + mix
        hm = (x1 * _inv_rms(x1) * gm_ref[...]).astype(BF16)
        acc = jnp.zeros_like(x1)
        for c in range(D_FF // FF_CHUNK):
            cols = slice(c * FF_CHUNK, (c + 1) * FF_CHUNK)
            u = _dot(hm, wup_ref[:, cols])
            acc = acc + _dot(jnp.square(jnp.maximum(u, 0.0)).astype(BF16), wdn_ref[cols, :])
        o_ref[rs, :] = x1 + acc


def _mlp_call(x2, attn, ssm, z, gat, gn, wo, gm, wup, wdn):
    t = x2.shape[0]
    row = lambda w: pl.BlockSpec((TM_MLP, w), lambda i: (i, 0))
    full = lambda a: pl.BlockSpec(a.shape, lambda i: (0, 0), pipeline_mode=pl.Buffered(1))
    return pl.pallas_call(
        _mlp_kernel,
        grid=(t // TM_MLP,),
        in_specs=[row(D_MODEL), row(ATTN_WIDTH), row(SSM_INNER), row(SSM_INNER), full(gat), full(gn), full(wo),
                  full(gm), full(wup), full(wdn)],
        out_specs=row(D_MODEL),
        out_shape=jax.ShapeDtypeStruct((t, D_MODEL), F32),
        compiler_params=pltpu.CompilerParams(dimension_semantics=("arbitrary",),
                                             vmem_limit_bytes=VMEM_LIMIT),
        name="outproj_mlp",
    )(x2, attn, ssm, z, gat, gn, wo, gm, wup, wdn)


def _swap_halves(g):
    half = QK_ROPE_DIM // 2
    return jnp.concatenate([g[..., half:], g[..., :half]], axis=-1)


def _lane_bcast(v):
    return jnp.broadcast_to(v[..., None], v.shape + (LANES,)).astype(F32)


def _layer(x2, cs, batch, seq, ln_mix_g, stacked_proj_weights, q_a_norm_g, kv_a_norm_g, q_norm_g,
           k_norm_g, attn_out_norm_g, conv_w, conv_b, a_log_fwd, a_log_bwd, dt_bias_fwd, dt_bias_bwd,
           d_skip, ssm_norm_g, w_out, ln_mlp_g, w_mlp_up, w_mlp_down):
    win, wuq, wuk, wuv = _win_prep_call(*stacked_proj_weights)
    ones_v = np.ones((ATTN_HEADS // 2, V_HEAD_DIM), np.float32)
    vone = jnp.asarray(np.stack([np.concatenate([0 * ones_v, ones_v], axis=-1),
                                 np.concatenate([ones_v, 0 * ones_v], axis=-1)], axis=1).reshape(1, -1))
    scale = QK_HEAD_DIM ** -0.5 * np.log2(np.e)
    zeros_nope = jnp.zeros((QK_NOPE_DIM,), k_norm_g.dtype)
    k_rope_g = k_norm_g[QK_NOPE_DIM:]
    hg = jnp.concatenate([q_norm_g * scale, _swap_halves(q_norm_g[QK_NOPE_DIM:]) * scale,
                          k_norm_g[:QK_NOPE_DIM], zeros_nope,
                          zeros_nope, k_rope_g, _swap_halves(k_rope_g),
                          jnp.zeros(((SUBLANES - 3) * HEAD_PAD,), k_norm_g.dtype)]).reshape(SUBLANES, HEAD_PAD)

    q, k, v, z, xbc, dtc = _inproj_call(
        x2, cs, ln_mix_g[None, :], win, q_a_norm_g[None, :], wuq, kv_a_norm_g[None, :], wuk, wuv, vone,
        hg)

    attn, (wo_b, wup_b, wdn_b) = _attn_call(q, k, v, (w_out, w_mlp_up, w_mlp_down), batch, seq)

    cw = jnp.pad(conv_w[:, 0, :], ((0, SUBLANES - SSM_CONV), (0, 0)))
    cbias = conv_b[None, :]
    adt = jnp.stack([a_log_fwd, a_log_bwd, dt_bias_fwd, dt_bias_bwd]).reshape(2, 2, SSM_GROUPS, HEADS_PER_GROUP)
    adt = _lane_bcast(adt.transpose(0, 2, 1, 3).reshape(2, 2 * SSM_HEADS))
    dskip = jnp.repeat(d_skip, SSM_HEAD_DIM)[None, :]
    ssm = _ssd_call(xbc, dtc, cw, cbias, adt, dskip, batch, seq)

    return _mlp_call(x2, attn, ssm, z, attn_out_norm_g[None, :], ssm_norm_g[None, :], wo_b,
                     ln_mlp_g[None, :], wup_b, wdn_b)


def _rope_table(positions):
    inv_freq = 1.0 / (ROPE_THETA ** (jnp.arange(0, QK_ROPE_DIM, 2, dtype=F32) / QK_ROPE_DIM))
    ang = inv_freq[:, None] * positions.astype(F32).reshape(1, -1)
    return jnp.concatenate([jnp.cos(ang), jnp.sin(ang)], axis=0)


def kernel(x, positions, ln_mix_g, w_in, q_a_norm_g, w_uq, kv_a_norm_g, w_ukv, q_norm_g, k_norm_g,
           attn_out_norm_g, conv_w, conv_b, a_log_fwd, a_log_bwd, dt_bias_fwd, dt_bias_bwd, d_skip,
           ssm_norm_g, w_out, ln_mlp_g, w_mlp_up, w_mlp_down):
    batch, seq, d = x.shape
    assert d == D_MODEL and seq % TQ == 0 and (batch * seq) % TM_IN == 0 and (batch * seq) % TM_MLP == 0
    cs = _rope_table(positions)
    x2 = x.reshape(batch * seq, d)
    for l in range(ln_mix_g.shape[0]):
        x2 = _layer(x2, cs, batch, seq, ln_mix_g[l], (w_in, w_uq, w_ukv, l), q_a_norm_g[l], kv_a_norm_g[l],
                    q_norm_g[l], k_norm_g[l], attn_out_norm_g[l], conv_w[l], conv_b[l],
                    a_log_fwd[l], a_log_bwd[l], dt_bias_fwd[l], dt_bias_bwd[l], d_skip[l], ssm_norm_g[l],
                    w_out[l], ln_mlp_g[l], w_mlp_up[l], w_mlp_down[l])
    return x2.reshape(batch, seq, d)
```

```python
import numpy as np
import jax
import jax.numpy as jnp
from jax import lax
from jax.experimental import pallas as pl
from jax.experimental.pallas import tpu as pltpu

F32 = jnp.float32
BF16 = jnp.bfloat16

D_MODEL = 1024
ATTN_HEADS = 8
QK_NOPE_DIM = 64
QK_ROPE_DIM = 32
QK_HEAD_DIM = QK_NOPE_DIM + QK_ROPE_DIM
V_HEAD_DIM = 64
Q_LORA_RANK = D_MODEL // 4
KV_LORA_RANK = D_MODEL // 8
ROPE_THETA = 10000.0
ATTN_WIDTH = ATTN_HEADS * V_HEAD_DIM
SSM_HEADS = 8
SSM_HEAD_DIM = 64
SSM_INNER = SSM_HEADS * SSM_HEAD_DIM
SSM_GROUPS = 2
SSM_STATE = 128
SSM_CONV = 5
SSM_CHUNK = 128
SSM_CONV_CH = SSM_INNER + 2 * SSM_GROUPS * SSM_STATE
D_MIX = ATTN_WIDTH + SSM_INNER
D_FF = 4 * D_MODEL
EPS = 1e-6

LANES = 128
SUBLANES = 8
HEAD_PAD = LANES

HEADS_PER_GROUP = SSM_HEADS // SSM_GROUPS
GROUP_INNER = SSM_INNER // SSM_GROUPS
GROUP_COLS = GROUP_INNER + 2 * SSM_STATE
DT_ROWS = 2 * HEADS_PER_GROUP
BF16_ROWS = 16
SEG_K_PER_ROW = 8
SEG_K = DT_ROWS * SEG_K_PER_ROW
SEG_BLOCKS = DT_ROWS + HEADS_PER_GROUP
LOG2E = float(np.log2(np.e))
CONV_WIN = 2 * SSM_CHUNK
CONV_WIN_LEAD = SSM_CHUNK // 2
CONV_SHIFTED_TAPS = tuple(k for k in range(SSM_CONV) if k != SSM_CONV // 2)

COL_CKV = Q_LORA_RANK
COL_MISC = COL_CKV + KV_LORA_RANK
COL_Z = COL_MISC + LANES
COL_XBC = COL_Z + SSM_INNER
IN_COLS = COL_XBC + SSM_CONV_CH

TM_IN = 1024
IN_SUBTILE = 512
TQ = 1024
ATTN_HEADS_PER_STEP = 8
TM_MLP = 1024
MLP_SUBTILE = 512
FF_CHUNK = 1024
SSD_UNROLL = 16
CONV_AHEAD = 4
PASS_A_UNROLL = 12
WIN_PREP_STEPS = 4
VMEM_LIMIT = 56 * 1024 * 1024


def _inv_rms(x):
    n = x.shape[-1]
    x2 = x * x
    acc = x2[:, 0:LANES]
    for i in range(1, n // LANES):
        acc = acc + x2[:, i * LANES:(i + 1) * LANES]
    return lax.rsqrt(jnp.sum(acc, axis=-1, keepdims=True) * (1.0 / n) + EPS)


def _dot(a, b):
    return jnp.dot(a, b, preferred_element_type=F32)


def _dot_nt(a, b):
    return lax.dot_general(a, b, (((1,), (1,)), ((), ())), preferred_element_type=F32)


def _inproj_kernel(x_ref, cs_ref, g_ref, win_ref, gqa_ref, wuq_ref, gkva_ref, wuk_ref, wuv_ref, vone_ref,
                   hg_ref,
                   q_ref, k_ref, v_ref, z_ref, xbc_ref, dt_ref):
    for sub in range(x_ref.shape[0] // IN_SUBTILE):
        _inproj_rows(sub * IN_SUBTILE, x_ref, cs_ref, g_ref, win_ref, gqa_ref, wuq_ref, gkva_ref, wuk_ref,
                     wuv_ref, vone_ref, hg_ref, q_ref, k_ref, v_ref, z_ref, xbc_ref, dt_ref)


def _inproj_rows(r0, x_ref, cs_ref, g_ref, win_ref, gqa_ref, wuq_ref, gkva_ref, wuk_ref, wuv_ref, vone_ref,
                 hg_ref, q_ref, k_ref, v_ref, z_ref, xbc_ref, dt_ref):
    tm = IN_SUBTILE
    rs = slice(r0, r0 + tm)
    x = x_ref[rs, :]
    h = (x * _inv_rms(x) * g_ref[...]).astype(BF16)
    big = _dot(h, win_ref[...])
    z_ref[rs, :] = big[:, COL_Z:COL_XBC].astype(BF16)
    xbc_ref[rs, :] = big[:, COL_XBC:IN_COLS].astype(BF16)
    misc = big[:, COL_MISC:COL_Z]
    for c in range(tm // SSM_CHUNK):
        dt_ref[r0 // SSM_CHUNK + c] = misc[c * SSM_CHUNK:(c + 1) * SSM_CHUNK, :].T[0:2 * SSM_HEADS, :]

    cq = big[:, 0:COL_CKV]
    ckv = big[:, COL_CKV:COL_MISC]
    cqn = (cq * _inv_rms(cq) * gqa_ref[...]).astype(BF16)
    ckvn = (ckv * _inv_rms(ckv) * gkva_ref[...]).astype(BF16)
    q_pre = _dot(cqn, wuq_ref[...])
    k_pre = _dot(ckvn, wuk_ref[...])
    v_ref[rs, :] = (_dot(ckvn, wuv_ref[...]) + vone_ref[...]).astype(BF16)

    half = QK_ROPE_DIM // 2
    zpad = jnp.zeros((HEAD_PAD - QK_ROPE_DIM, SSM_CHUNK), F32)
    tt = jnp.concatenate(
        [jnp.concatenate([cs_ref[:, r0 + c * SSM_CHUNK:r0 + (c + 1) * SSM_CHUNK], zpad], axis=0).T
         for c in range(tm // SSM_CHUNK)], axis=0)
    lane_t = lax.broadcasted_iota(jnp.int32, (tm, HEAD_PAD), 1)
    cs = jnp.where(lane_t < QK_NOPE_DIM, 1.0,
                   jnp.where(lane_t < QK_NOPE_DIM + half, pltpu.roll(tt, QK_NOPE_DIM, 1),
                             jnp.where(lane_t < QK_HEAD_DIM + half, pltpu.roll(tt, QK_NOPE_DIM + half, 1),
                                       pltpu.roll(tt, QK_HEAD_DIM, 1))))
    lane = lax.broadcasted_iota(jnp.int32, (1, HEAD_PAD), 1)
    in_head = (lane < QK_HEAD_DIM).astype(F32)
    is_rope = ((lane >= QK_NOPE_DIM) & (lane < QK_HEAD_DIM)).astype(F32)
    inv_d = 1.0 / QK_HEAD_DIM

    gcq = hg_ref[0:1, :] * cs
    for hh in range(ATTN_HEADS):
        sl = slice(hh * HEAD_PAD, (hh + 1) * HEAD_PAD)
        qh = q_pre[:, sl]
        ssq = jnp.sum(qh * qh * in_head, axis=-1, keepdims=True)
        q_ref[rs, sl] = (qh * lax.rsqrt(ssq * inv_d + EPS) * gcq).astype(BF16)

    ab = misc * (hg_ref[2:3, :] * cs)
    lane2 = lax.broadcasted_iota(jnp.int32, (tm, HEAD_PAD), 1)
    swapped = jnp.where(lane2 < QK_HEAD_DIM, pltpu.roll(ab, HEAD_PAD - QK_ROPE_DIM, 1),
                        pltpu.roll(ab, QK_ROPE_DIM, 1))
    s_both = jnp.where(lane2 >= QK_NOPE_DIM, ab + swapped, 0.0)
    ssq_pe = jnp.sum(misc * misc * is_rope, axis=-1, keepdims=True)
    gkn = hg_ref[1:2, :]
    for hh in range(ATTN_HEADS):
        sl = slice(hh * HEAD_PAD, (hh + 1) * HEAD_PAD)
        kh = k_pre[:, sl]
        ssq = jnp.sum(kh * kh, axis=-1, keepdims=True) + ssq_pe
        k_ref[rs, sl] = ((kh * gkn + s_both) * lax.rsqrt(ssq * inv_d + EPS)).astype(BF16)


def _win_prep_kernel(w_ref, wq_ref, wkv_ref, o_ref, wuq_ref, wuk_ref, wuv_ref):
    half = QK_ROPE_DIM // 2

    @pl.when(pl.program_id(0) == 0)
    def _up_projection_tiles():
        wq = wq_ref[...]
        q_pieces = []
        for hh in range(ATTN_HEADS):
            head = wq[:, hh * QK_HEAD_DIM:(hh + 1) * QK_HEAD_DIM]
            rope = head[:, QK_NOPE_DIM:]
            q_pieces += [head, -rope[:, half:], rope[:, :half]]
        wuq_ref[...] = jnp.concatenate(q_pieces, axis=1).astype(BF16)
        wkv = wkv_ref[...]
        zeros = jnp.zeros((wkv.shape[0], V_HEAD_DIM), F32)
        k_pieces, v_pieces = [], []
        for hh in range(ATTN_HEADS):
            nope = wkv[:, hh * HEAD_PAD:hh * HEAD_PAD + QK_NOPE_DIM]
            val = wkv[:, hh * HEAD_PAD + QK_NOPE_DIM:(hh + 1) * HEAD_PAD]
            k_pieces += [nope, zeros]
            v_pieces += [val, zeros] if hh % 2 == 0 else [zeros, val]
        wuk_ref[...] = jnp.concatenate(k_pieces, axis=1).astype(BF16)
        wuv_ref[...] = jnp.concatenate(v_pieces, axis=1).astype(BF16)

    w = w_ref[...].astype(F32)
    o_kpe = Q_LORA_RANK + KV_LORA_RANK
    o_z = o_kpe + QK_ROPE_DIM
    o_dt = o_z + SSM_INNER + SSM_CONV_CH
    hpg = HEADS_PER_GROUP
    dt_f = w[:, o_dt:o_dt + SSM_HEADS]
    dt_b = w[:, o_dt + SSM_HEADS:o_dt + 2 * SSM_HEADS]
    kpe = w[:, o_kpe:o_z]
    pieces = [w[:, 0:o_kpe]]
    for g in range(SSM_GROUPS):
        pieces += [dt_f[:, g * hpg:(g + 1) * hpg], dt_b[:, g * hpg:(g + 1) * hpg]]
    pieces += [jnp.zeros((w.shape[0], QK_NOPE_DIM - 2 * SSM_HEADS), F32), kpe, -kpe[:, half:], kpe[:, :half],
               w[:, o_z:o_dt]]
    o_ref[...] = jnp.concatenate(pieces, axis=1).astype(BF16)


def _win_prep_call(w_in3, w_uq3, w_ukv3, layer):
    depth, d, n = w_in3.shape
    rows = d // WIN_PREP_STEPS
    whole = lambda a: pl.BlockSpec(a.shape[1:], lambda i: (layer, 0))
    head_cols = ATTN_HEADS * HEAD_PAD
    return pl.pallas_call(
        _win_prep_kernel,
        grid=(WIN_PREP_STEPS,),
        in_specs=[pl.BlockSpec((rows, n), lambda i: (layer * WIN_PREP_STEPS + i, 0)), whole(w_uq3), whole(w_ukv3)],
        out_specs=[pl.BlockSpec((rows, IN_COLS), lambda i: (i, 0)),
                   pl.BlockSpec((Q_LORA_RANK, head_cols), lambda i: (0, 0)),
                   pl.BlockSpec((KV_LORA_RANK, head_cols), lambda i: (0, 0)),
                   pl.BlockSpec((KV_LORA_RANK, head_cols), lambda i: (0, 0))],
        out_shape=[jax.ShapeDtypeStruct((d, IN_COLS), BF16),
                   jax.ShapeDtypeStruct((Q_LORA_RANK, head_cols), BF16),
                   jax.ShapeDtypeStruct((KV_LORA_RANK, head_cols), BF16),
                   jax.ShapeDtypeStruct((KV_LORA_RANK, head_cols), BF16)],
        compiler_params=pltpu.CompilerParams(dimension_semantics=("arbitrary",), vmem_limit_bytes=VMEM_LIMIT),
        name="win_prep",
    )(w_in3.reshape(depth * d, n).astype(BF16),
      w_uq3.reshape(depth * w_uq3.shape[1], w_uq3.shape[2]),
      w_ukv3.reshape(depth * w_ukv3.shape[1], w_ukv3.shape[2]))


def _inproj_call(x2, cs, g, win, gqa, wuq, gkva, wuk, wuv, vone, hg):
    t = x2.shape[0]
    cpt = TM_IN // SSM_CHUNK
    full = lambda a: pl.BlockSpec(a.shape, lambda i: (0,) * a.ndim, pipeline_mode=pl.Buffered(1))
    row = lambda w: pl.BlockSpec((TM_IN, w), lambda i: (i, 0))
    return pl.pallas_call(
        _inproj_kernel,
        grid=(t // TM_IN,),
        in_specs=[row(D_MODEL), pl.BlockSpec((QK_ROPE_DIM, TM_IN), lambda i: (0, i)), full(g), full(win), full(gqa), full(wuq), full(gkva),
                  full(wuk), full(wuv), full(vone), full(hg)],
        out_specs=[row(ATTN_HEADS * HEAD_PAD), row(ATTN_HEADS * HEAD_PAD), row(ATTN_HEADS * HEAD_PAD),
                   row(SSM_INNER), row(SSM_CONV_CH),
                   pl.BlockSpec((cpt, 2 * SSM_HEADS, SSM_CHUNK), lambda i: (i, 0, 0))],
        out_shape=[jax.ShapeDtypeStruct((t, ATTN_HEADS * HEAD_PAD), BF16),
                   jax.ShapeDtypeStruct((t, ATTN_HEADS * HEAD_PAD), BF16),
                   jax.ShapeDtypeStruct((t, ATTN_HEADS * HEAD_PAD), BF16),
                   jax.ShapeDtypeStruct((t, SSM_INNER), BF16),
                   jax.ShapeDtypeStruct((t, SSM_CONV_CH), BF16),
                   jax.ShapeDtypeStruct((t // SSM_CHUNK, 2 * SSM_HEADS, SSM_CHUNK), F32)],
        compiler_params=pltpu.CompilerParams(dimension_semantics=("arbitrary",),
                                             vmem_limit_bytes=VMEM_LIMIT),
        name="inproj",
    )(x2, cs, g, win, gqa, wuq, gkva, wuk, wuv, vone, hg)


def _attn_kernel(q_ref, k_ref, v_ref, *refs):
    n_w = (len(refs) - 1) // 2
    w_in_refs, o_ref, w_out_refs = refs[:n_w], refs[n_w], refs[n_w + 1:]
    for w_in_ref, w_out_ref in zip(w_in_refs, w_out_refs):
        w_out_ref[...] = w_in_ref[...].astype(BF16)

    lane = lax.broadcasted_iota(jnp.int32, (q_ref.shape[0], HEAD_PAD), 1)
    for jp in range(ATTN_HEADS_PER_STEP // 2):
        sls = [slice(j * HEAD_PAD, (j + 1) * HEAD_PAD) for j in (2 * jp, 2 * jp + 1)]
        ss = [_dot_nt(q_ref[:, sl], k_ref[:, sl]) for sl in sls]
        ps = [jnp.exp2(s - jnp.max(s, axis=-1, keepdims=True)).astype(BF16) for s in ss]
        accs = [_dot(p, v_ref[:, sl]) for p, sl in zip(ps, sls)]
        res = [acc / pltpu.roll(acc, V_HEAD_DIM, 1) for acc in accs]
        o_ref[:, jp * HEAD_PAD:(jp + 1) * HEAD_PAD] = jnp.where(lane < V_HEAD_DIM, res[0], res[1])


def _attn_call(q, k, v, later_weights, batch, seq):
    n_q = seq // TQ
    hps = ATTN_HEADS_PER_STEP
    assert ATTN_HEADS == hps
    n_steps = batch * n_q

    def rows_spec(w):
        assert w.shape[0] % (n_steps * BF16_ROWS) == 0
        return pl.BlockSpec((w.shape[0] // n_steps, w.shape[1]), lambda b, hp, i: (b * n_q + i, 0))
    w_specs = [rows_spec(w) for w in later_weights]
    outs = pl.pallas_call(
        _attn_kernel,
        grid=(batch, ATTN_HEADS // hps, n_q),
        in_specs=[pl.BlockSpec((TQ, hps * HEAD_PAD), lambda b, hp, i: (b * n_q + i, hp)),
                  pl.BlockSpec((seq, hps * HEAD_PAD), lambda b, hp, i: (b, hp)),
                  pl.BlockSpec((seq, hps * HEAD_PAD), lambda b, hp, i: (b, hp))] + w_specs,
        out_specs=[pl.BlockSpec((TQ, hps * V_HEAD_DIM), lambda b, hp, i: (b * n_q + i, hp))] + w_specs,
        out_shape=[jax.ShapeDtypeStruct((batch * seq, ATTN_WIDTH), F32)]
                  + [jax.ShapeDtypeStruct(w.shape, BF16) for w in later_weights],
        compiler_params=pltpu.CompilerParams(
            dimension_semantics=("arbitrary", "arbitrary", "arbitrary"),
            vmem_limit_bytes=VMEM_LIMIT),
        name="attention",
    )(q, k, v, *later_weights)
    return outs[0], outs[1:]


def _split3(x):
    hi = x.astype(BF16)
    r1 = x - hi.astype(F32)
    mid = r1.astype(BF16)
    lo = (r1 - mid.astype(F32)).astype(BF16)
    return hi, mid, lo


def _ssd_kernel(x_ref, b_ref, c_ref, dt_ref, sh_ref, place_ref, lhs_ones_ref, rhs_const_ref,
                cwx_ref, cwb_ref, cwc_ref, cbx_ref, cbb_ref, cbc_ref, adt_ref, dskip_ref,
                o_ref, xc_scr, xm_scr, nsf_scr, nsb_scr, dtv_scr, w_scr, dec_scr, lhs_scr, rhs_scr, rpart_scr):
    nc = dt_ref.shape[0]
    seq = x_ref.shape[0]
    L = SSM_CHUNK
    hpg = HEADS_PER_GROUP
    npair = hpg // 2
    P = SSM_HEAD_DIM
    x_cols = slice(0, GROUP_INNER)
    b_cols = slice(GROUP_INNER, GROUP_INNER + SSM_STATE)
    c_cols = slice(GROUP_INNER + SSM_STATE, GROUP_COLS)
    pair_cols = [slice(j * L, (j + 1) * L) for j in range(npair)]

    @pl.when((pl.program_id(0) == 0) & (pl.program_id(1) == 0))
    def _init_rhs():
        for u in range(SSD_UNROLL):
            rhs_scr[u] = rhs_const_ref[...]

    a_neg = -jnp.exp(adt_ref[0]) * LOG2E
    dtv = jax.nn.softplus(dt_ref[...] + adt_ref[1][None])
    dtv_scr[...] = dtv
    da2 = (dtv * a_neg[None]).reshape(nc * DT_ROWS, L)
    ri = lax.broadcasted_iota(jnp.int32, (L, L), 0)
    ci = lax.broadcasted_iota(jnp.int32, (L, L), 1)
    upper = (ri <= ci).astype(BF16)
    lower = (ri >= ci).astype(BF16)
    tri = jnp.concatenate([upper, lower], axis=1)
    cs_fb = sum(_dot(p, tri) for p in _split3(da2))
    rowsel = (ri & hpg) == 0
    cs2 = jnp.where(rowsel, cs_fb[:, 0:L], cs_fb[:, L:2 * L])
    colcs = cs2.T

    col_parts = jnp.concatenate(_split3(colcs), axis=1)
    half_n = place_ref.shape[1] // 2
    for hf in range(2):
        cols = slice(hf * half_n, (hf + 1) * half_n)
        lhs_half = _dot(col_parts, place_ref[:, cols]) + lhs_ones_ref[:, cols]
        for c in range(nc // 2):
            lhs_scr[hf * (nc // 2) + c] = lhs_half[:, c * L:c * L + SEG_K].astype(BF16)
    for i, part in enumerate(_split3(-cs2)):
        rpart_scr[i] = part.astype(F32).reshape(nc, DT_ROWS, L)

    def lane_bcast(col):
        return jnp.broadcast_to(col, (col.shape[0], L))

    tot2 = jnp.where(rowsel, lane_bcast(cs2[:, L - 1:L]), lane_bcast(cs2[:, 0:1]))
    dec_scr[...] = jnp.exp2(tot2).reshape(nc, DT_ROWS, L)
    w_scr[...] = (dtv.reshape(nc * DT_ROWS, L) * jnp.exp2(tot2 - cs2)).reshape(nc, DT_ROWS, L)

    lane_t = lax.broadcasted_iota(jnp.int32, (L, L), 1)
    low_half = lane_t < P

    cw_halves = (cwx_ref[...], jnp.concatenate([cwb_ref[...], cwc_ref[...]], axis=1))
    cb_halves = (cbx_ref[...], jnp.concatenate([cbb_ref[...], cbc_ref[...]], axis=1))

    def rows(hf, start, size):
        if hf == 0:
            return x_ref[pl.ds(start, size), :]
        return jnp.concatenate([b_ref[pl.ds(start, size), :], c_ref[pl.ds(start, size), :]], axis=1)

    def conv_stage(c):
        base = pl.multiple_of(c * L, L)
        ws = pl.multiple_of(jnp.clip(c * L - CONV_WIN_LEAD, 0, seq - CONV_WIN), CONV_WIN_LEAD)
        variant = jnp.where(c == 0, 0, jnp.where(c == nc - 1, 2, 1))
        sh = sh_ref[variant]
        halves = []
        for hf in range(2):
            cw = cw_halves[hf]
            shifted = _dot(sh, rows(hf, ws, CONV_WIN))
            acc = cb_halves[hf] + cw[SSM_CONV // 2:SSM_CONV // 2 + 1, :] * rows(hf, base, L).astype(F32)
            for t, kk in enumerate(CONV_SHIFTED_TAPS):
                acc = acc + cw[kk:kk + 1, :] * shifted[t * L:(t + 1) * L, :]
            halves.append(acc * jax.nn.sigmoid(acc))
        xc = jnp.concatenate(halves, axis=1)
        xc_scr[c] = xc
        for j in range(npair):
            xp = xc[:, pair_cols[j]]
            xm_scr[c, j] = jnp.concatenate([jnp.where(low_half, xp, 0.0), jnp.where(low_half, 0.0, xp)],
                                           axis=0).astype(BF16)

    def state_loads(c):
        return xc_scr[c, :, b_cols], [xm_scr[c, j] for j in range(npair)]

    def state_stage(c, loaded):
        b_tok, xm = loaded
        bt = b_tok.T
        for d, ns_scr in ((0, nsf_scr), (1, nsb_scr)):
            w = w_scr[c, d * hpg:(d + 1) * hpg, :]
            ns_scr[c] = jnp.concatenate(
                [_dot(jnp.concatenate([(bt * w[2 * j:2 * j + 1, :]).astype(BF16),
                                       (bt * w[2 * j + 1:2 * j + 2, :]).astype(BF16)], axis=1), xm[j])
                 for j in range(npair)], axis=1)

    for c in range(CONV_AHEAD):
        conv_stage(c)

    def pass_a(c, carry):
        loaded = state_loads(c)
        conv_stage(c + CONV_AHEAD)
        state_stage(c, loaded)
        return carry
    lax.fori_loop(0, nc - CONV_AHEAD, pass_a, 0, unroll=PASS_A_UNROLL)
    for c in range(nc - CONV_AHEAD, nc):
        state_stage(c, state_loads(c))

    lane_r = lax.broadcasted_iota(jnp.int32, (1, L), 1)

    def decay_row(c, d):
        dec = dec_scr[c, d * hpg:(d + 1) * hpg, :]
        return jnp.concatenate([jnp.where(lane_r < P, dec[2 * j:2 * j + 1, :], dec[2 * j + 1:2 * j + 2, :])
                                for j in range(npair)], axis=1)

    def pass_b(d, ns_scr):
        def body(i, st):
            c = i if d == 0 else nc - 1 - i
            new = ns_scr[c]
            ns_scr[c] = st
            return st * decay_row(c, d) + new
        lax.fori_loop(0, nc, body, jnp.zeros((SSM_STATE, GROUP_INNER), F32), unroll=True)
    pass_b(0, nsf_scr)
    pass_b(1, nsb_scr)

    mask_f = ci <= ri
    mask_b = ci >= ri
    neg_inf = jnp.float32(-jnp.inf)
    sub16 = lax.broadcasted_iota(jnp.int32, (BF16_ROWS, L), 0)

    def pass_c(c, carry):
        base = pl.multiple_of(c * L, L)
        xc = xc_scr[c]
        bm = xc[:, b_cols].astype(BF16)
        cm = xc[:, c_cols].astype(BF16)
        cb = _dot_nt(cm, bm)
        off_f = _dot(cm, nsf_scr[c].astype(BF16))
        off_b = _dot(cm, nsb_scr[c].astype(BF16))
        slot = c % SSD_UNROLL
        for hd in range(DT_ROWS):
            r0 = (hd % 2) * SEG_K_PER_ROW
            blk = jnp.where((sub16 >= r0) & (sub16 < r0 + 3), 1.0, 0.0)
            for i in range(3):
                blk = jnp.where(sub16 == r0 + 3 + i, rpart_scr[i, c, hd:hd + 1, :], blk)
            rhs_scr[slot, (hd // 2) * BF16_ROWS:(hd // 2 + 1) * BF16_ROWS, hd * L:(hd + 1) * L] = blk.astype(BF16)
        seg = _dot(lhs_scr[c], rhs_scr[slot])
        dt_f_r = dtv_scr[c, 0:hpg, :]
        dt_b_r = dtv_scr[c, hpg:DT_ROWS, :]
        ys = []
        for j in range(npair):
            ws_ = []
            for h in (2 * j, 2 * j + 1):
                ef = jnp.exp2(jnp.where(mask_f, seg[:, h * L:(h + 1) * L], neg_inf))
                eb = jnp.exp2(jnp.where(mask_b, seg[:, (hpg + h) * L:(hpg + h + 1) * L], neg_inf))
                ws_.append((cb * (ef * dt_f_r[h:h + 1, :] + eb * dt_b_r[h:h + 1, :])).astype(BF16))
            scale_f = jnp.exp2(seg[:, (DT_ROWS + j) * L:(DT_ROWS + j + 1) * L])
            scale_b = jnp.exp2(seg[:, (DT_ROWS + npair + j) * L:(DT_ROWS + npair + j + 1) * L])
            ys.append(_dot(jnp.concatenate(ws_, axis=1), xm_scr[c, j])
                      + scale_f * off_f[:, pair_cols[j]] + scale_b * off_b[:, pair_cols[j]])
        y = jnp.concatenate(ys, axis=1) + dskip_ref[...] * xc[:, x_cols]
        o_ref[pl.ds(base, L), :] = y
        return carry
    lax.fori_loop(0, nc, pass_c, 0, unroll=SSD_UNROLL)


def _conv_shift_matrices():
    out = np.zeros((3, len(CONV_SHIFTED_TAPS) * SSM_CHUNK, CONV_WIN), np.float32)
    for v, lead in enumerate((0, CONV_WIN_LEAD, CONV_WIN - SSM_CHUNK)):
        for i, kk in enumerate(CONV_SHIFTED_TAPS):
            for t in range(SSM_CHUNK):
                j = t + kk - SSM_CONV // 2 + lead
                if 0 <= j < CONV_WIN:
                    out[v, i * SSM_CHUNK + t, j] = 1.0
    return out


def _seg_matmul_constants(nc):
    L, K, half = SSM_CHUNK, SEG_K_PER_ROW, SSM_HEAD_DIM
    place = np.zeros((3 * L, nc * L), np.float32)
    lhs_ones = np.zeros((1, nc * L), np.float32)
    for c in range(nc):
        for hd in range(DT_ROWS):
            for i in range(3):
                place[i * L + c * DT_ROWS + hd, c * L + hd * K + i] = 1.0
                lhs_ones[0, c * L + hd * K + 3 + i] = 1.0
    rhs = np.zeros((SEG_K, SEG_BLOCKS * L), np.float32)
    for hd in range(DT_ROWS):
        d, h = divmod(hd, HEADS_PER_GROUP)
        pair_block = DT_ROWS + d * (HEADS_PER_GROUP // 2) + h // 2
        lanes = slice(0, half) if h % 2 == 0 else slice(half, L)
        rhs[hd * K:hd * K + 3, hd * L:(hd + 1) * L] = 1.0
        rhs[hd * K:hd * K + 3, pair_block * L:(pair_block + 1) * L][:, lanes] = 1.0
    return place, lhs_ones, rhs


def _ssd_call(xbc, dtc, cw, cbias, adt, dskip, batch, seq):
    nc = seq // SSM_CHUNK
    L = SSM_CHUNK
    assert nc >= 3 and seq >= CONV_WIN and nc * DT_ROWS == L
    sh = jnp.asarray(_conv_shift_matrices(), BF16)
    place, lhs_ones, rhs_const = _seg_matmul_constants(nc)
    place = jnp.asarray(place, BF16)
    lhs_ones = jnp.asarray(lhs_ones, F32)
    rhs_const = jnp.asarray(rhs_const, BF16)
    const = lambda a: pl.BlockSpec(a.shape, lambda b, g: (0,) * a.ndim)
    b_blk0 = SSM_INNER // SSM_STATE
    c_blk0 = b_blk0 + SSM_GROUPS

    def xbc_views(rows_, batched):
        lead = (lambda b: b) if batched else (lambda b: 0)
        return [pl.BlockSpec((rows_, GROUP_INNER), lambda b, g: (lead(b), g)),
                pl.BlockSpec((rows_, SSM_STATE), lambda b, g: (lead(b), b_blk0 + g)),
                pl.BlockSpec((rows_, SSM_STATE), lambda b, g: (lead(b), c_blk0 + g))]
    return pl.pallas_call(
        _ssd_kernel,
        grid=(batch, SSM_GROUPS),
        in_specs=xbc_views(seq, True) + [
                  pl.BlockSpec((nc, DT_ROWS, L), lambda b, g: (b, g, 0)),
                  const(sh), const(place), const(lhs_ones), const(rhs_const)]
                 + xbc_views(SUBLANES, False) + xbc_views(1, False) + [
                  pl.BlockSpec((2, DT_ROWS, L), lambda b, g: (0, g, 0)),
                  pl.BlockSpec((1, GROUP_INNER), lambda b, g: (0, g))],
        out_specs=pl.BlockSpec((seq, GROUP_INNER), lambda b, g: (b, g)),
        out_shape=jax.ShapeDtypeStruct((batch * seq, SSM_INNER), F32),
        scratch_shapes=[pltpu.VMEM((nc, L, GROUP_COLS), F32),
                        pltpu.VMEM((nc, HEADS_PER_GROUP // 2, 2 * L, L), BF16),
                        pltpu.VMEM((nc, SSM_STATE, GROUP_INNER), F32),
                        pltpu.VMEM((nc, SSM_STATE, GROUP_INNER), F32),
                        pltpu.VMEM((nc, DT_ROWS, L), F32),
                        pltpu.VMEM((nc, DT_ROWS, L), F32),
                        pltpu.VMEM((nc, DT_ROWS, L), F32),
                        pltpu.VMEM((nc, L, SEG_K), BF16),
                        pltpu.VMEM((SSD_UNROLL, SEG_K, SEG_BLOCKS * L), BF16),
                        pltpu.VMEM((3, nc, DT_ROWS, L), F32)],
        compiler_params=pltpu.CompilerParams(dimension_semantics=("arbitrary", "arbitrary"),
                                             vmem_limit_bytes=VMEM_LIMIT),
        name="ssd",
    )(xbc, xbc, xbc, dtc, sh, place, lhs_ones, rhs_const, cw, cw, cw, cbias, cbias, cbias,
      adt, dskip)


def _mlp_kernel(x_ref, attn_ref, ssm_ref, z_ref, gat_ref, gn_ref, wo_ref, gm_ref, wup_ref, wdn_ref, o_ref):
    for sub in range(x_ref.shape[0] // MLP_SUBTILE):
        rs = slice(sub * MLP_SUBTILE, (sub + 1) * MLP_SUBTILE)
        a = attn_ref[rs, :]
        an = (a * _inv_rms(a) * gat_ref[...]).astype(BF16)
        zc = z_ref[rs, :].astype(F32)
        y = ssm_ref[rs, :] * (zc * jax.nn.sigmoid(zc))
        yn = jnp.concatenate([y[:, g * GROUP_INNER:(g + 1) * GROUP_INNER]
                              * _inv_rms(y[:, g * GROUP_INNER:(g + 1) * GROUP_INNER]) for g in range(SSM_GROUPS)],
                             axis=1) * gn_ref[...]
        mix = _dot(an, wo_ref[0:ATTN_WIDTH, :]) + _dot(yn.astype(BF16), wo_ref[ATTN_WIDTH:D_MIX, :])
        x1 = x_ref[rs, :] + mix
        hm = (x1 * _inv_rms(x1) * gm_ref[...]).astype(BF16)
        acc = jnp.zeros_like(x1)
        for c in range(D_FF // FF_CHUNK):
            cols = slice(c * FF_CHUNK, (c + 1) * FF_CHUNK)
            u = _dot(hm, wup_ref[:, cols])
            acc = acc + _dot(jnp.square(jnp.maximum(u, 0.0)).astype(BF16), wdn_ref[cols, :])
        o_ref[rs, :] = x1 + acc


def _mlp_call(x2, attn, ssm, z, gat, gn, wo, gm, wup, wdn):
    t = x2.shape[0]
    row = lambda w: pl.BlockSpec((TM_MLP, w), lambda i: (i, 0))
    full = lambda a: pl.BlockSpec(a.shape, lambda i: (0, 0), pipeline_mode=pl.Buffered(1))
    return pl.pallas_call(
        _mlp_kernel,
        grid=(t // TM_MLP,),
        in_specs=[row(D_MODEL), row(ATTN_WIDTH), row(SSM_INNER), row(SSM_INNER), full(gat), full(gn), full(wo),
                  full(gm), full(wup), full(wdn)],
        out_specs=row(D_MODEL),
        out_shape=jax.ShapeDtypeStruct((t, D_MODEL), F32),
        compiler_params=pltpu.CompilerParams(dimension_semantics=("arbitrary",),
                                             vmem_limit_bytes=VMEM_LIMIT),
        name="outproj_mlp",
    )(x2, attn, ssm, z, gat, gn, wo, gm, wup, wdn)


def _swap_halves(g):
    half = QK_ROPE_DIM // 2
    return jnp.concatenate([g[..., half:], g[..., :half]], axis=-1)


def _lane_bcast(v):
    return jnp.broadcast_to(v[..., None], v.shape + (LANES,)).astype(F32)


def _layer(x2, cs, batch, seq, ln_mix_g, stacked_proj_weights, q_a_norm_g, kv_a_norm_g, q_norm_g,
           k_norm_g, attn_out_norm_g, conv_w, conv_b, a_log_fwd, a_log_bwd, dt_bias_fwd, dt_bias_bwd,
           d_skip, ssm_norm_g, w_out, ln_mlp_g, w_mlp_up, w_mlp_down):
    win, wuq, wuk, wuv = _win_prep_call(*stacked_proj_weights)
    ones_v = np.ones((ATTN_HEADS // 2, V_HEAD_DIM), np.float32)
    vone = jnp.asarray(np.stack([np.concatenate([0 * ones_v, ones_v], axis=-1),
                                 np.concatenate([ones_v, 0 * ones_v], axis=-1)], axis=1).reshape(1, -1))
    scale = QK_HEAD_DIM ** -0.5 * np.log2(np.e)
    zeros_nope = jnp.zeros((QK_NOPE_DIM,), k_norm_g.dtype)
    k_rope_g = k_norm_g[QK_NOPE_DIM:]
    hg = jnp.concatenate([q_norm_g * scale, _swap_halves(q_norm_g[QK_NOPE_DIM:]) * scale,
                          k_norm_g[:QK_NOPE_DIM], zeros_nope,
                          zeros_nope, k_rope_g, _swap_halves(k_rope_g),
                          jnp.zeros(((SUBLANES - 3) * HEAD_PAD,), k_norm_g.dtype)]).reshape(SUBLANES, HEAD_PAD)

    q, k, v, z, xbc, dtc = _inproj_call(
        x2, cs, ln_mix_g[None, :], win, q_a_norm_g[None, :], wuq, kv_a_norm_g[None, :], wuk, wuv, vone,
        hg)

    attn, (wo_b, wup_b, wdn_b) = _attn_call(q, k, v, (w_out, w_mlp_up, w_mlp_down), batch, seq)

    cw = jnp.pad(conv_w[:, 0, :], ((0, SUBLANES - SSM_CONV), (0, 0)))
    cbias = conv_b[None, :]
    adt = jnp.stack([a_log_fwd, a_log_bwd, dt_bias_fwd, dt_bias_bwd]).reshape(2, 2, SSM_GROUPS, HEADS_PER_GROUP)
    adt = _lane_bcast(adt.transpose(0, 2, 1, 3).reshape(2, 2 * SSM_HEADS))
    dskip = jnp.repeat(d_skip, SSM_HEAD_DIM)[None, :]
    ssm = _ssd_call(xbc, dtc, cw, cbias, adt, dskip, batch, seq)

    return _mlp_call(x2, attn, ssm, z, attn_out_norm_g[None, :], ssm_norm_g[None, :], wo_b,
                     ln_mlp_g[None, :], wup_b, wdn_b)


def _rope_table(positions):
    inv_freq = 1.0 / (ROPE_THETA ** (jnp.arange(0, QK_ROPE_DIM, 2, dtype=F32) / QK_ROPE_DIM))
    ang = inv_freq[:, None] * positions.astype(F32).reshape(1, -1)
    return jnp.concatenate([jnp.cos(ang), jnp.sin(ang)], axis=0)


def kernel(x, positions, ln_mix_g, w_in, q_a_norm_g, w_uq, kv_a_norm_g, w_ukv, q_norm_g, k_norm_g,
           attn_out_norm_g, conv_w, conv_b, a_log_fwd, a_log_bwd, dt_bias_fwd, dt_bias_bwd, d_skip,
           ssm_norm_g, w_out, ln_mlp_g, w_mlp_up, w_mlp_down):
    batch, seq, d = x.shape
    assert d == D_MODEL and seq % TQ == 0 and (batch * seq) % TM_IN == 0 and (batch * seq) % TM_MLP == 0
    cs = _rope_table(positions)
    x2 = x.reshape(batch * seq, d)
    for l in range(ln_mix_g.shape[0]):
        x2 = _layer(x2, cs, batch, seq, ln_mix_g[l], (w_in, w_uq, w_ukv, l), q_a_norm_g[l], kv_a_norm_g[l],
                    q_norm_g[l], k_norm_g[l], attn_out_norm_g[l], conv_w[l], conv_b[l],
                    a_log_fwd[l], a_log_bwd[l], dt_bias_fwd[l], dt_bias_bwd[l], d_skip[l], ssm_norm_g[l],
                    w_out[l], ln_mlp_g[l], w_mlp_up[l], w_mlp_down[l])
    return x2.reshape(batch, seq, d)
```

```python
import types

import numpy as np
import jax
import jax.numpy as jnp
from jax import lax
from jax.experimental import pallas as pl
from jax.experimental.pallas import tpu as pltpu

F32 = jnp.float32
BF16 = jnp.bfloat16

D_MODEL = 1024
ATTN_HEADS = 8
QK_NOPE_DIM = 64
QK_ROPE_DIM = 32
QK_HEAD_DIM = QK_NOPE_DIM + QK_ROPE_DIM
V_HEAD_DIM = 64
Q_LORA_RANK = D_MODEL // 4
KV_LORA_RANK = D_MODEL // 8
ROPE_THETA = 10000.0
ATTN_WIDTH = ATTN_HEADS * V_HEAD_DIM
SSM_HEADS = 8
SSM_HEAD_DIM = 64
SSM_INNER = SSM_HEADS * SSM_HEAD_DIM
SSM_GROUPS = 2
SSM_STATE = 128
SSM_CONV = 5
SSM_CHUNK = 128
SSM_CONV_CH = SSM_INNER + 2 * SSM_GROUPS * SSM_STATE
D_MIX = ATTN_WIDTH + SSM_INNER
D_FF = 4 * D_MODEL
EPS = 1e-6

LANES = 128
SUBLANES = 8
HEAD_PAD = LANES

HEADS_PER_GROUP = SSM_HEADS // SSM_GROUPS
GROUP_INNER = SSM_INNER // SSM_GROUPS
GROUP_COLS = GROUP_INNER + 2 * SSM_STATE
DT_ROWS = 2 * HEADS_PER_GROUP
BF16_ROWS = 16
SEG_K_PER_ROW = 8
SEG_K = DT_ROWS * SEG_K_PER_ROW
SEG_BLOCKS = DT_ROWS + HEADS_PER_GROUP
LOG2E = float(np.log2(np.e))
CONV_WIN = 2 * SSM_CHUNK
CONV_WIN_LEAD = SSM_CHUNK // 2
CONV_SHIFTED_TAPS = tuple(k for k in range(SSM_CONV) if k != SSM_CONV // 2)

COL_CKV = Q_LORA_RANK
COL_MISC = COL_CKV + KV_LORA_RANK
COL_Z = COL_MISC + LANES
COL_XBC = COL_Z + SSM_INNER
IN_COLS = COL_XBC + SSM_CONV_CH

TM_IN = 1024
IN_SUBTILE = 512
TQ = 1024
ATTN_HEADS_PER_STEP = 8
TM_MLP = 1024
MLP_SUBTILE = 512
FF_CHUNK = 1024
CONV_AHEAD = 4
WIN_PREP_STEPS = 4
VMEM_LIMIT = 56 * 1024 * 1024


def _inv_rms(x):
    n = x.shape[-1]
    x2 = x * x
    acc = x2[:, 0:LANES]
    for i in range(1, n // LANES):
        acc = acc + x2[:, i * LANES:(i + 1) * LANES]
    return lax.rsqrt(jnp.sum(acc, axis=-1, keepdims=True) * (1.0 / n) + EPS)


def _dot(a, b):
    return jnp.dot(a, b, preferred_element_type=F32)


def _dot_nt(a, b):
    return lax.dot_general(a, b, (((1,), (1,)), ((), ())), preferred_element_type=F32)


def _inproj_kernel(x_ref, cs_ref, g_ref, win_ref, gqa_ref, wuq_ref, gkva_ref, wuk_ref, wuv_ref, vone_ref,
                   hg_ref,
                   q_ref, k_ref, v_ref, z_ref, xbc_ref, dt_ref):
    for sub in range(x_ref.shape[0] // IN_SUBTILE):
        _inproj_rows(sub * IN_SUBTILE, x_ref, cs_ref, g_ref, win_ref, gqa_ref, wuq_ref, gkva_ref, wuk_ref,
                     wuv_ref, vone_ref, hg_ref, q_ref, k_ref, v_ref, z_ref, xbc_ref, dt_ref)


def _inproj_rows(r0, x_ref, cs_ref, g_ref, win_ref, gqa_ref, wuq_ref, gkva_ref, wuk_ref, wuv_ref, vone_ref,
                 hg_ref, q_ref, k_ref, v_ref, z_ref, xbc_ref, dt_ref):
    tm = IN_SUBTILE
    rs = slice(r0, r0 + tm)
    x = x_ref[rs, :]
    h = (x * _inv_rms(x) * g_ref[...]).astype(BF16)
    big = _dot(h, win_ref[...])
    z_ref[rs, :] = big[:, COL_Z:COL_XBC].astype(BF16)
    xbc_ref[rs, :] = big[:, COL_XBC:IN_COLS].astype(BF16)
    misc = big[:, COL_MISC:COL_Z]
    for c in range(tm // SSM_CHUNK):
        dt_ref[r0 // SSM_CHUNK + c] = misc[c * SSM_CHUNK:(c + 1) * SSM_CHUNK, :].T[0:2 * SSM_HEADS, :]

    cq = big[:, 0:COL_CKV]
    ckv = big[:, COL_CKV:COL_MISC]
    cqn = (cq * _inv_rms(cq) * gqa_ref[...]).astype(BF16)
    ckvn = (ckv * _inv_rms(ckv) * gkva_ref[...]).astype(BF16)
    q_pre = _dot(cqn, wuq_ref[...])
    k_pre = _dot(ckvn, wuk_ref[...])
    v_ref[rs, :] = (_dot(ckvn, wuv_ref[...]) + vone_ref[...]).astype(BF16)

    half = QK_ROPE_DIM // 2
    zpad = jnp.zeros((HEAD_PAD - QK_ROPE_DIM, SSM_CHUNK), F32)
    tt = jnp.concatenate(
        [jnp.concatenate([cs_ref[:, r0 + c * SSM_CHUNK:r0 + (c + 1) * SSM_CHUNK], zpad], axis=0).T
         for c in range(tm // SSM_CHUNK)], axis=0)
    lane_t = lax.broadcasted_iota(jnp.int32, (tm, HEAD_PAD), 1)
    cs = jnp.where(lane_t < QK_NOPE_DIM, 1.0,
                   jnp.where(lane_t < QK_NOPE_DIM + half, pltpu.roll(tt, QK_NOPE_DIM, 1),
                             jnp.where(lane_t < QK_HEAD_DIM + half, pltpu.roll(tt, QK_NOPE_DIM + half, 1),
                                       pltpu.roll(tt, QK_HEAD_DIM, 1))))
    lane = lax.broadcasted_iota(jnp.int32, (1, HEAD_PAD), 1)
    in_head = (lane < QK_HEAD_DIM).astype(F32)
    is_rope = ((lane >= QK_NOPE_DIM) & (lane < QK_HEAD_DIM)).astype(F32)
    inv_d = 1.0 / QK_HEAD_DIM

    gcq = hg_ref[0:1, :] * cs
    for hh in range(ATTN_HEADS):
        sl = slice(hh * HEAD_PAD, (hh + 1) * HEAD_PAD)
        qh = q_pre[:, sl]
        ssq = jnp.sum(qh * qh * in_head, axis=-1, keepdims=True)
        q_ref[rs, sl] = (qh * lax.rsqrt(ssq * inv_d + EPS) * gcq).astype(BF16)

    ab = misc * (hg_ref[2:3, :] * cs)
    lane2 = lax.broadcasted_iota(jnp.int32, (tm, HEAD_PAD), 1)
    swapped = jnp.where(lane2 < QK_HEAD_DIM, pltpu.roll(ab, HEAD_PAD - QK_ROPE_DIM, 1),
                        pltpu.roll(ab, QK_ROPE_DIM, 1))
    s_both = jnp.where(lane2 >= QK_NOPE_DIM, ab + swapped, 0.0)
    ssq_pe = jnp.sum(misc * misc * is_rope, axis=-1, keepdims=True)
    gkn = hg_ref[1:2, :]
    for hh in range(ATTN_HEADS):
        sl = slice(hh * HEAD_PAD, (hh + 1) * HEAD_PAD)
        kh = k_pre[:, sl]
        ssq = jnp.sum(kh * kh, axis=-1, keepdims=True) + ssq_pe
        k_ref[rs, sl] = ((kh * gkn + s_both) * lax.rsqrt(ssq * inv_d + EPS)).astype(BF16)


def _win_prep_kernel(w_ref, wq_ref, wkv_ref, o_ref, wuq_ref, wuk_ref, wuv_ref):
    half = QK_ROPE_DIM // 2

    @pl.when(pl.program_id(0) == 0)
    def _up_projection_tiles():
        wq = wq_ref[...]
        q_pieces = []
        for hh in range(ATTN_HEADS):
            head = wq[:, hh * QK_HEAD_DIM:(hh + 1) * QK_HEAD_DIM]
            rope = head[:, QK_NOPE_DIM:]
            q_pieces += [head, -rope[:, half:], rope[:, :half]]
        wuq_ref[...] = jnp.concatenate(q_pieces, axis=1).astype(BF16)
        wkv = wkv_ref[...]
        zeros = jnp.zeros((wkv.shape[0], V_HEAD_DIM), F32)
        k_pieces, v_pieces = [], []
        for hh in range(ATTN_HEADS):
            nope = wkv[:, hh * HEAD_PAD:hh * HEAD_PAD + QK_NOPE_DIM]
            val = wkv[:, hh * HEAD_PAD + QK_NOPE_DIM:(hh + 1) * HEAD_PAD]
            k_pieces += [nope, zeros]
            v_pieces += [val, zeros] if hh % 2 == 0 else [zeros, val]
        wuk_ref[...] = jnp.concatenate(k_pieces, axis=1).astype(BF16)
        wuv_ref[...] = jnp.concatenate(v_pieces, axis=1).astype(BF16)

    w = w_ref[...].astype(F32)
    o_kpe = Q_LORA_RANK + KV_LORA_RANK
    o_z = o_kpe + QK_ROPE_DIM
    o_dt = o_z + SSM_INNER + SSM_CONV_CH
    hpg = HEADS_PER_GROUP
    dt_f = w[:, o_dt:o_dt + SSM_HEADS]
    dt_b = w[:, o_dt + SSM_HEADS:o_dt + 2 * SSM_HEADS]
    kpe = w[:, o_kpe:o_z]
    pieces = [w[:, 0:o_kpe]]
    for g in range(SSM_GROUPS):
        pieces += [dt_f[:, g * hpg:(g + 1) * hpg], dt_b[:, g * hpg:(g + 1) * hpg]]
    pieces += [jnp.zeros((w.shape[0], QK_NOPE_DIM - 2 * SSM_HEADS), F32), kpe, -kpe[:, half:], kpe[:, :half],
               w[:, o_z:o_dt]]
    o_ref[...] = jnp.concatenate(pieces, axis=1).astype(BF16)


def _win_prep_call(w_in3, w_uq3, w_ukv3, layer):
    depth, d, n = w_in3.shape
    rows = d // WIN_PREP_STEPS
    whole = lambda a: pl.BlockSpec(a.shape[1:], lambda i: (layer, 0))
    head_cols = ATTN_HEADS * HEAD_PAD
    return pl.pallas_call(
        _win_prep_kernel,
        grid=(WIN_PREP_STEPS,),
        in_specs=[pl.BlockSpec((rows, n), lambda i: (layer * WIN_PREP_STEPS + i, 0)), whole(w_uq3), whole(w_ukv3)],
        out_specs=[pl.BlockSpec((rows, IN_COLS), lambda i: (i, 0)),
                   pl.BlockSpec((Q_LORA_RANK, head_cols), lambda i: (0, 0)),
                   pl.BlockSpec((KV_LORA_RANK, head_cols), lambda i: (0, 0)),
                   pl.BlockSpec((KV_LORA_RANK, head_cols), lambda i: (0, 0))],
        out_shape=[jax.ShapeDtypeStruct((d, IN_COLS), BF16),
                   jax.ShapeDtypeStruct((Q_LORA_RANK, head_cols), BF16),
                   jax.ShapeDtypeStruct((KV_LORA_RANK, head_cols), BF16),
                   jax.ShapeDtypeStruct((KV_LORA_RANK, head_cols), BF16)],
        compiler_params=pltpu.CompilerParams(dimension_semantics=("arbitrary",), vmem_limit_bytes=VMEM_LIMIT),
        name="win_prep",
    )(w_in3.reshape(depth * d, n).astype(BF16),
      w_uq3.reshape(depth * w_uq3.shape[1], w_uq3.shape[2]),
      w_ukv3.reshape(depth * w_ukv3.shape[1], w_ukv3.shape[2]))


def _inproj_call(x2, cs, g, win, gqa, wuq, gkva, wuk, wuv, vone, hg):
    t = x2.shape[0]
    cpt = TM_IN // SSM_CHUNK
    full = lambda a: pl.BlockSpec(a.shape, lambda i: (0,) * a.ndim, pipeline_mode=pl.Buffered(1))
    row = lambda w: pl.BlockSpec((TM_IN, w), lambda i: (i, 0))
    return pl.pallas_call(
        _inproj_kernel,
        grid=(t // TM_IN,),
        in_specs=[row(D_MODEL), pl.BlockSpec((QK_ROPE_DIM, TM_IN), lambda i: (0, i)), full(g), full(win), full(gqa), full(wuq), full(gkva),
                  full(wuk), full(wuv), full(vone), full(hg)],
        out_specs=[row(ATTN_HEADS * HEAD_PAD), row(ATTN_HEADS * HEAD_PAD), row(ATTN_HEADS * HEAD_PAD),
                   row(SSM_INNER), row(SSM_CONV_CH),
                   pl.BlockSpec((cpt, 2 * SSM_HEADS, SSM_CHUNK), lambda i: (i, 0, 0))],
        out_shape=[jax.ShapeDtypeStruct((t, ATTN_HEADS * HEAD_PAD), BF16),
                   jax.ShapeDtypeStruct((t, ATTN_HEADS * HEAD_PAD), BF16),
                   jax.ShapeDtypeStruct((t, ATTN_HEADS * HEAD_PAD), BF16),
                   jax.ShapeDtypeStruct((t, SSM_INNER), BF16),
                   jax.ShapeDtypeStruct((t, SSM_CONV_CH), BF16),
                   jax.ShapeDtypeStruct((t // SSM_CHUNK, 2 * SSM_HEADS, SSM_CHUNK), F32)],
        compiler_params=pltpu.CompilerParams(dimension_semantics=("arbitrary",),
                                             vmem_limit_bytes=VMEM_LIMIT),
        name="inproj",
    )(x2, cs, g, win, gqa, wuq, gkva, wuk, wuv, vone, hg)


def _attn_kernel(q_ref, k_ref, v_ref, *refs):
    n_w = (len(refs) - 1) // 2
    w_in_refs, o_ref, w_out_refs = refs[:n_w], refs[n_w], refs[n_w + 1:]
    for w_in_ref, w_out_ref in zip(w_in_refs, w_out_refs):
        w_out_ref[...] = w_in_ref[...].astype(BF16)

    lane = lax.broadcasted_iota(jnp.int32, (q_ref.shape[0], HEAD_PAD), 1)
    for jp in range(ATTN_HEADS_PER_STEP // 2):
        sls = [slice(j * HEAD_PAD, (j + 1) * HEAD_PAD) for j in (2 * jp, 2 * jp + 1)]
        ss = [_dot_nt(q_ref[:, sl], k_ref[:, sl]) for sl in sls]
        ps = [jnp.exp2(s - jnp.max(s, axis=-1, keepdims=True)).astype(BF16) for s in ss]
        accs = [_dot(p, v_ref[:, sl]) for p, sl in zip(ps, sls)]
        res = [acc / pltpu.roll(acc, V_HEAD_DIM, 1) for acc in accs]
        o_ref[:, jp * HEAD_PAD:(jp + 1) * HEAD_PAD] = jnp.where(lane < V_HEAD_DIM, res[0], res[1])


def _attn_call(q, k, v, later_weights, batch, seq):
    n_q = seq // TQ
    hps = ATTN_HEADS_PER_STEP
    assert ATTN_HEADS == hps
    n_steps = batch * n_q

    def rows_spec(w):
        assert w.shape[0] % (n_steps * BF16_ROWS) == 0
        return pl.BlockSpec((w.shape[0] // n_steps, w.shape[1]), lambda b, hp, i: (b * n_q + i, 0))
    w_specs = [rows_spec(w) for w in later_weights]
    outs = pl.pallas_call(
        _attn_kernel,
        grid=(batch, ATTN_HEADS // hps, n_q),
        in_specs=[pl.BlockSpec((TQ, hps * HEAD_PAD), lambda b, hp, i: (b * n_q + i, hp)),
                  pl.BlockSpec((seq, hps * HEAD_PAD), lambda b, hp, i: (b, hp)),
                  pl.BlockSpec((seq, hps * HEAD_PAD), lambda b, hp, i: (b, hp))] + w_specs,
        out_specs=[pl.BlockSpec((TQ, hps * V_HEAD_DIM), lambda b, hp, i: (b * n_q + i, hp))] + w_specs,
        out_shape=[jax.ShapeDtypeStruct((batch * seq, ATTN_WIDTH), F32)]
                  + [jax.ShapeDtypeStruct(w.shape, BF16) for w in later_weights],
        compiler_params=pltpu.CompilerParams(
            dimension_semantics=("arbitrary", "arbitrary", "arbitrary"),
            vmem_limit_bytes=VMEM_LIMIT),
        name="attention",
    )(q, k, v, *later_weights)
    return outs[0], outs[1:]


def _split3(x):
    hi = x.astype(BF16)
    r1 = x - hi.astype(F32)
    mid = r1.astype(BF16)
    lo = (r1 - mid.astype(F32)).astype(BF16)
    return hi, mid, lo


def _ssd_kernel(x_ref, b_ref, c_ref, dt_ref, sh_ref, place_ref, lhs_ones_ref, rhs_const_ref,
                cwx_ref, cwb_ref, cwc_ref, cbx_ref, cbb_ref, cbc_ref, adt_ref, dskip_ref,
                o_ref, xc_scr, xm_scr, nsf_scr, nsb_scr, dtv_scr, w_scr, dec_scr, lhs_scr, rhs_scr, rpart_scr):
    nc = dt_ref.shape[0]

    @pl.when(pl.program_id(0) == 0)
    def _init_rhs():
        for g in range(SSM_GROUPS):
            for c in range(nc):
                rhs_scr[g, c] = rhs_const_ref[...]

    groups = []
    for g in range(SSM_GROUPS):
        inner = slice(g * GROUP_INNER, (g + 1) * GROUP_INNER)
        state = slice(g * SSM_STATE, (g + 1) * SSM_STATE)
        rows = slice(g * DT_ROWS, (g + 1) * DT_ROWS)
        groups.append(_ssd_group_stages(
            x_ref.at[:, inner], b_ref.at[:, state], c_ref.at[:, state], dt_ref.at[:, rows, :],
            sh_ref, place_ref, lhs_ones_ref,
            cwx_ref.at[:, inner], cwb_ref.at[:, state], cwc_ref.at[:, state],
            cbx_ref.at[:, inner], cbb_ref.at[:, state], cbc_ref.at[:, state],
            adt_ref.at[:, rows, :], dskip_ref.at[:, inner], o_ref.at[:, inner],
            *[scr.at[g] for scr in (xc_scr, xm_scr, nsf_scr, nsb_scr, dtv_scr, w_scr, dec_scr, lhs_scr,
                                    rhs_scr, rpart_scr)]))
    first, second = groups

    def pass_a(grp, c):
        loaded = grp.state_loads(c)
        if c + CONV_AHEAD < nc:
            grp.conv_stage(c + CONV_AHEAD)
        grp.state_stage(c, loaded)

    zero_state = jnp.zeros((SSM_STATE, GROUP_INNER), F32)
    first.dt_phase()
    for c in range(CONV_AHEAD):
        first.conv_stage(c)
    for c in range(nc):
        pass_a(first, c)
        if c == 0:
            second.dt_phase()
        if c >= nc - CONV_AHEAD:
            second.conv_stage(c - (nc - CONV_AHEAD))
    st_f = st_b = zero_state
    for c in range(nc):
        pass_a(second, c)
        st_f = first.pass_b_step(0, c, st_f)
        st_b = first.pass_b_step(1, c, st_b)
    st_f = st_b = zero_state
    for c in range(nc):
        first.pass_c(c)
        st_f = second.pass_b_step(0, c, st_f)
        st_b = second.pass_b_step(1, c, st_b)
    for c in range(nc):
        second.pass_c(c)


def _ssd_group_stages(x_ref, b_ref, c_ref, dt_ref, sh_ref, place_ref, lhs_ones_ref,
                      cwx_ref, cwb_ref, cwc_ref, cbx_ref, cbb_ref, cbc_ref, adt_ref, dskip_ref,
                      o_ref, xc_scr, xm_scr, nsf_scr, nsb_scr, dtv_scr, w_scr, dec_scr, lhs_scr, rhs_scr, rpart_scr):
    nc = dt_ref.shape[0]
    seq = x_ref.shape[0]
    L = SSM_CHUNK
    hpg = HEADS_PER_GROUP
    npair = hpg // 2
    P = SSM_HEAD_DIM
    x_cols = slice(0, GROUP_INNER)
    b_cols = slice(GROUP_INNER, GROUP_INNER + SSM_STATE)
    c_cols = slice(GROUP_INNER + SSM_STATE, GROUP_COLS)
    pair_cols = [slice(j * L, (j + 1) * L) for j in range(npair)]

    ri = lax.broadcasted_iota(jnp.int32, (L, L), 0)
    ci = lax.broadcasted_iota(jnp.int32, (L, L), 1)

    def lane_bcast(col):
        return jnp.broadcast_to(col, (col.shape[0], L))

    def dt_phase():
        a_neg = -jnp.exp(adt_ref[0]) * LOG2E
        dtv = jax.nn.softplus(dt_ref[...] + adt_ref[1][None])
        dtv_scr[...] = dtv
        da2 = (dtv * a_neg[None]).reshape(nc * DT_ROWS, L)
        upper = (ri <= ci).astype(BF16)
        lower = (ri >= ci).astype(BF16)
        tri = jnp.concatenate([upper, lower], axis=1)
        cs_fb = sum(_dot(p, tri) for p in _split3(da2))
        rowsel = (ri & hpg) == 0
        cs2 = jnp.where(rowsel, cs_fb[:, 0:L], cs_fb[:, L:2 * L])
        colcs = cs2.T

        col_parts = jnp.concatenate(_split3(colcs), axis=1)
        half_n = place_ref.shape[1] // 2
        for hf in range(2):
            cols = slice(hf * half_n, (hf + 1) * half_n)
            lhs_half = _dot(col_parts, place_ref[:, cols]) + lhs_ones_ref[:, cols]
            for c in range(nc // 2):
                lhs_scr[hf * (nc // 2) + c] = lhs_half[:, c * L:c * L + SEG_K].astype(BF16)
        for i, part in enumerate(_split3(-cs2)):
            rpart_scr[i] = part.astype(F32).reshape(nc, DT_ROWS, L)

        tot2 = jnp.where(rowsel, lane_bcast(cs2[:, L - 1:L]), lane_bcast(cs2[:, 0:1]))
        dec_scr[...] = jnp.exp2(tot2).reshape(nc, DT_ROWS, L)
        w_scr[...] = (dtv.reshape(nc * DT_ROWS, L) * jnp.exp2(tot2 - cs2)).reshape(nc, DT_ROWS, L)

    low_half = ci < P

    def rows(hf, start, size):
        if hf == 0:
            return x_ref[pl.ds(start, size), :]
        return jnp.concatenate([b_ref[pl.ds(start, size), :], c_ref[pl.ds(start, size), :]], axis=1)

    def conv_stage(c):
        base = c * L
        ws = min(max(c * L - CONV_WIN_LEAD, 0), seq - CONV_WIN)
        variant = 0 if c == 0 else (2 if c == nc - 1 else 1)
        sh = sh_ref[variant]
        cw_halves = (cwx_ref[...], jnp.concatenate([cwb_ref[...], cwc_ref[...]], axis=1))
        cb_halves = (cbx_ref[...], jnp.concatenate([cbb_ref[...], cbc_ref[...]], axis=1))
        halves = []
        for hf in range(2):
            cw = cw_halves[hf]
            shifted = _dot(sh, rows(hf, ws, CONV_WIN))
            acc = cb_halves[hf] + cw[SSM_CONV // 2:SSM_CONV // 2 + 1, :] * rows(hf, base, L).astype(F32)
            for t, kk in enumerate(CONV_SHIFTED_TAPS):
                acc = acc + cw[kk:kk + 1, :] * shifted[t * L:(t + 1) * L, :]
            halves.append(acc * jax.nn.sigmoid(acc))
        xc = jnp.concatenate(halves, axis=1)
        xc_scr[c] = xc
        for j in range(npair):
            xp = xc[:, pair_cols[j]]
            xm_scr[c, j] = jnp.concatenate([jnp.where(low_half, xp, 0.0), jnp.where(low_half, 0.0, xp)],
                                           axis=0).astype(BF16)

    def state_loads(c):
        return xc_scr[c, :, b_cols], [xm_scr[c, j] for j in range(npair)]

    def state_stage(c, loaded):
        b_tok, xm = loaded
        bt = b_tok.T
        for d, ns_scr in ((0, nsf_scr), (1, nsb_scr)):
            w = w_scr[c, d * hpg:(d + 1) * hpg, :]
            ns_scr[c] = jnp.concatenate(
                [_dot(jnp.concatenate([(bt * w[2 * j:2 * j + 1, :]).astype(BF16),
                                       (bt * w[2 * j + 1:2 * j + 2, :]).astype(BF16)], axis=1), xm[j])
                 for j in range(npair)], axis=1)

    lane_r = lax.broadcasted_iota(jnp.int32, (1, L), 1)

    def decay_row(c, d):
        dec = dec_scr[c, d * hpg:(d + 1) * hpg, :]
        return jnp.concatenate([jnp.where(lane_r < P, dec[2 * j:2 * j + 1, :], dec[2 * j + 1:2 * j + 2, :])
                                for j in range(npair)], axis=1)

    def pass_b_step(d, i, st):
        ns_scr = nsf_scr if d == 0 else nsb_scr
        c = i if d == 0 else nc - 1 - i
        new = ns_scr[c]
        ns_scr[c] = st
        return st * decay_row(c, d) + new

    mask_f = ci <= ri
    mask_b = ci >= ri
    neg_inf = jnp.float32(-jnp.inf)
    sub16 = lax.broadcasted_iota(jnp.int32, (BF16_ROWS, L), 0)

    def pass_c(c):
        base = c * L
        xc = xc_scr[c]
        bm = xc[:, b_cols].astype(BF16)
        cm = xc[:, c_cols].astype(BF16)
        cb = _dot_nt(cm, bm)
        off_f = _dot(cm, nsf_scr[c].astype(BF16))
        off_b = _dot(cm, nsb_scr[c].astype(BF16))
        for hd in range(DT_ROWS):
            r0 = (hd % 2) * SEG_K_PER_ROW
            blk = jnp.where((sub16 >= r0) & (sub16 < r0 + 3), 1.0, 0.0)
            for i in range(3):
                blk = jnp.where(sub16 == r0 + 3 + i, rpart_scr[i, c, hd:hd + 1, :], blk)
            rhs_scr[c, (hd // 2) * BF16_ROWS:(hd // 2 + 1) * BF16_ROWS, hd * L:(hd + 1) * L] = blk.astype(BF16)
        seg = _dot(lhs_scr[c], rhs_scr[c])
        dt_f_r = dtv_scr[c, 0:hpg, :]
        dt_b_r = dtv_scr[c, hpg:DT_ROWS, :]
        ys = []
        for j in range(npair):
            ws_ = []
            for h in (2 * j, 2 * j + 1):
                ef = jnp.exp2(jnp.where(mask_f, seg[:, h * L:(h + 1) * L], neg_inf))
                eb = jnp.exp2(jnp.where(mask_b, seg[:, (hpg + h) * L:(hpg + h + 1) * L], neg_inf))
                ws_.append((cb * (ef * dt_f_r[h:h + 1, :] + eb * dt_b_r[h:h + 1, :])).astype(BF16))
            scale_f = jnp.exp2(seg[:, (DT_ROWS + j) * L:(DT_ROWS + j + 1) * L])
            scale_b = jnp.exp2(seg[:, (DT_ROWS + npair + j) * L:(DT_ROWS + npair + j + 1) * L])
            ys.append(_dot(jnp.concatenate(ws_, axis=1), xm_scr[c, j])
                      + scale_f * off_f[:, pair_cols[j]] + scale_b * off_b[:, pair_cols[j]])
        y = jnp.concatenate(ys, axis=1) + dskip_ref[...] * xc[:, x_cols]
        o_ref[pl.ds(base, L), :] = y

    return types.SimpleNamespace(dt_phase=dt_phase, conv_stage=conv_stage, state_loads=state_loads,
                                 state_stage=state_stage, pass_b_step=pass_b_step, pass_c=pass_c)


def _conv_shift_matrices():
    out = np.zeros((3, len(CONV_SHIFTED_TAPS) * SSM_CHUNK, CONV_WIN), np.float32)
    for v, lead in enumerate((0, CONV_WIN_LEAD, CONV_WIN - SSM_CHUNK)):
        for i, kk in enumerate(CONV_SHIFTED_TAPS):
            for t in range(SSM_CHUNK):
                j = t + kk - SSM_CONV // 2 + lead
                if 0 <= j < CONV_WIN:
                    out[v, i * SSM_CHUNK + t, j] = 1.0
    return out


def _seg_matmul_constants(nc):
    L, K, half = SSM_CHUNK, SEG_K_PER_ROW, SSM_HEAD_DIM
    place = np.zeros((3 * L, nc * L), np.float32)
    lhs_ones = np.zeros((1, nc * L), np.float32)
    for c in range(nc):
        for hd in range(DT_ROWS):
            for i in range(3):
                place[i * L + c * DT_ROWS + hd, c * L + hd * K + i] = 1.0
                lhs_ones[0, c * L + hd * K + 3 + i] = 1.0
    rhs = np.zeros((SEG_K, SEG_BLOCKS * L), np.float32)
    for hd in range(DT_ROWS):
        d, h = divmod(hd, HEADS_PER_GROUP)
        pair_block = DT_ROWS + d * (HEADS_PER_GROUP // 2) + h // 2
        lanes = slice(0, half) if h % 2 == 0 else slice(half, L)
        rhs[hd * K:hd * K + 3, hd * L:(hd + 1) * L] = 1.0
        rhs[hd * K:hd * K + 3, pair_block * L:(pair_block + 1) * L][:, lanes] = 1.0
    return place, lhs_ones, rhs


def _ssd_call(xbc, dtc, cw, cbias, adt, dskip, batch, seq):
    nc = seq // SSM_CHUNK
    L = SSM_CHUNK
    assert nc >= 3 and seq >= CONV_WIN and nc * DT_ROWS == L
    sh = jnp.asarray(_conv_shift_matrices(), BF16)
    place, lhs_ones, rhs_const = _seg_matmul_constants(nc)
    place = jnp.asarray(place, BF16)
    lhs_ones = jnp.asarray(lhs_ones, F32)
    rhs_const = jnp.asarray(rhs_const, BF16)
    const = lambda a: pl.BlockSpec(a.shape, lambda b: (0,) * a.ndim)
    G = SSM_GROUPS
    bc_cols = G * SSM_STATE
    b_blk = SSM_INNER // bc_cols
    c_blk = b_blk + 1

    def xbc_views(rows_, batched):
        lead = (lambda b: b) if batched else (lambda b: 0)
        return [pl.BlockSpec((rows_, SSM_INNER), lambda b: (lead(b), 0)),
                pl.BlockSpec((rows_, bc_cols), lambda b: (lead(b), b_blk)),
                pl.BlockSpec((rows_, bc_cols), lambda b: (lead(b), c_blk))]
    return pl.pallas_call(
        _ssd_kernel,
        grid=(batch,),
        in_specs=xbc_views(seq, True) + [
                  pl.BlockSpec((nc, G * DT_ROWS, L), lambda b: (b, 0, 0)),
                  const(sh), const(place), const(lhs_ones), const(rhs_const)]
                 + xbc_views(SUBLANES, False) + xbc_views(1, False) + [const(adt), const(dskip)],
        out_specs=pl.BlockSpec((seq, SSM_INNER), lambda b: (b, 0)),
        out_shape=jax.ShapeDtypeStruct((batch * seq, SSM_INNER), F32),
        scratch_shapes=[pltpu.VMEM((G, nc, L, GROUP_COLS), F32),
                        pltpu.VMEM((G, nc, HEADS_PER_GROUP // 2, 2 * L, L), BF16),
                        pltpu.VMEM((G, nc, SSM_STATE, GROUP_INNER), F32),
                        pltpu.VMEM((G, nc, SSM_STATE, GROUP_INNER), F32),
                        pltpu.VMEM((G, nc, DT_ROWS, L), F32),
                        pltpu.VMEM((G, nc, DT_ROWS, L), F32),
                        pltpu.VMEM((G, nc, DT_ROWS, L), F32),
                        pltpu.VMEM((G, nc, L, SEG_K), BF16),
                        pltpu.VMEM((G, nc, SEG_K, SEG_BLOCKS * L), BF16),
                        pltpu.VMEM((G, 3, nc, DT_ROWS, L), F32)],
        compiler_params=pltpu.CompilerParams(dimension_semantics=("arbitrary",),
                                             vmem_limit_bytes=VMEM_LIMIT),
        name="ssd",
    )(xbc, xbc, xbc, dtc, sh, place, lhs_ones, rhs_const, cw, cw, cw, cbias, cbias, cbias,
      adt, dskip)


def _mlp_kernel(x_ref, attn_ref, ssm_ref, z_ref, gat_ref, gn_ref, wo_ref, gm_ref, wup_ref, wdn_ref, o_ref):
    for sub in range(x_ref.shape[0] // MLP_SUBTILE):
        rs = slice(sub * MLP_SUBTILE, (sub + 1) * MLP_SUBTILE)
        a = attn_ref[rs, :]
        an = (a * _inv_rms(a) * gat_ref[...]).astype(BF16)
        zc = z_ref[rs, :].astype(F32)
        y = ssm_ref[rs, :] * (zc * jax.nn.sigmoid(zc))
        yn = jnp.concatenate([y[:, g * GROUP_INNER:(g + 1) * GROUP_INNER]
                              * _inv_rms(y[:, g * GROUP_INNER:(g + 1) * GROUP_INNER]) for g in range(SSM_GROUPS)],
                             axis=1) * gn_ref[...]
        mix = _dot(an, wo_ref[0:ATTN_WIDTH, :]) + _dot(yn.astype(BF16), wo_ref[ATTN_WIDTH:D_MIX, :])
        x1 = x_ref[rs, :] + mix
        hm = (x1 * _inv_rms(x1) * gm_ref[...]).astype(BF16)
        acc = jnp.zeros_like(x1)
        for c in range(D_FF // FF_CHUNK):
            cols = slice(c * FF_CHUNK, (c + 1) * FF_CHUNK)
            u = _dot(hm, wup_ref[:, cols])
            acc = acc + _dot(jnp.square(jnp.maximum(u, 0.0)).astype(BF16), wdn_ref[cols, :])
        o_ref[rs, :] = x1 + acc


def _mlp_call(x2, attn, ssm, z, gat, gn, wo, gm, wup, wdn):
    t = x2.shape[0]
    row = lambda w: pl.BlockSpec((TM_MLP, w), lambda i: (i, 0))
    full = lambda a: pl.BlockSpec(a.shape, lambda i: (0, 0), pipeline_mode=pl.Buffered(1))
    return pl.pallas_call(
        _mlp_kernel,
        grid=(t // TM_MLP,),
        in_specs=[row(D_MODEL), row(ATTN_WIDTH), row(SSM_INNER), row(SSM_INNER), full(gat), full(gn), full(wo),
                  full(gm), full(wup), full(wdn)],
        out_specs=row(D_MODEL),
        out_shape=jax.ShapeDtypeStruct((t, D_MODEL), F32),
        compiler_params=pltpu.CompilerParams(dimension_semantics=("arbitrary",),
                                             vmem_limit_bytes=VMEM_LIMIT),
        name="outproj_mlp",
    )(x2, attn, ssm, z, gat, gn, wo, gm, wup, wdn)


def _swap_halves(g):
    half = QK_ROPE_DIM // 2
    return jnp.concatenate([g[..., half:], g[..., :half]], axis=-1)


def _lane_bcast(v):
    return jnp.broadcast_to(v[..., None], v.shape + (LANES,)).astype(F32)


def _layer(x2, cs, batch, seq, ln_mix_g, stacked_proj_weights, q_a_norm_g, kv_a_norm_g, q_norm_g,
           k_norm_g, attn_out_norm_g, conv_w, conv_b, a_log_fwd, a_log_bwd, dt_bias_fwd, dt_bias_bwd,
           d_skip, ssm_norm_g, w_out, ln_mlp_g, w_mlp_up, w_mlp_down):
    win, wuq, wuk, wuv = _win_prep_call(*stacked_proj_weights)
    ones_v = np.ones((ATTN_HEADS // 2, V_HEAD_DIM), np.float32)
    vone = jnp.asarray(np.stack([np.concatenate([0 * ones_v, ones_v], axis=-1),
                                 np.concatenate([ones_v, 0 * ones_v], axis=-1)], axis=1).reshape(1, -1))
    scale = QK_HEAD_DIM ** -0.5 * np.log2(np.e)
    zeros_nope = jnp.zeros((QK_NOPE_DIM,), k_norm_g.dtype)
    k_rope_g = k_norm_g[QK_NOPE_DIM:]
    hg = jnp.concatenate([q_norm_g * scale, _swap_halves(q_norm_g[QK_NOPE_DIM:]) * scale,
                          k_norm_g[:QK_NOPE_DIM], zeros_nope,
                          zeros_nope, k_rope_g, _swap_halves(k_rope_g),
                          jnp.zeros(((SUBLANES - 3) * HEAD_PAD,), k_norm_g.dtype)]).reshape(SUBLANES, HEAD_PAD)

    q, k, v, z, xbc, dtc = _inproj_call(
        x2, cs, ln_mix_g[None, :], win, q_a_norm_g[None, :], wuq, kv_a_norm_g[None, :], wuk, wuv, vone,
        hg)

    attn, (wo_b, wup_b, wdn_b) = _attn_call(q, k, v, (w_out, w_mlp_up, w_mlp_down), batch, seq)

    cw = jnp.pad(conv_w[:, 0, :], ((0, SUBLANES - SSM_CONV), (0, 0)))
    cbias = conv_b[None, :]
    adt = jnp.stack([a_log_fwd, a_log_bwd, dt_bias_fwd, dt_bias_bwd]).reshape(2, 2, SSM_GROUPS, HEADS_PER_GROUP)
    adt = _lane_bcast(adt.transpose(0, 2, 1, 3).reshape(2, 2 * SSM_HEADS))
    dskip = jnp.repeat(d_skip, SSM_HEAD_DIM)[None, :]
    ssm = _ssd_call(xbc, dtc, cw, cbias, adt, dskip, batch, seq)

    return _mlp_call(x2, attn, ssm, z, attn_out_norm_g[None, :], ssm_norm_g[None, :], wo_b,
                     ln_mlp_g[None, :], wup_b, wdn_b)


def _rope_table(positions):
    inv_freq = 1.0 / (ROPE_THETA ** (jnp.arange(0, QK_ROPE_DIM, 2, dtype=F32) / QK_ROPE_DIM))
    ang = inv_freq[:, None] * positions.astype(F32).reshape(1, -1)
    return jnp.concatenate([jnp.cos(ang), jnp.sin(ang)], axis=0)


def kernel(x, positions, ln_mix_g, w_in, q_a_norm_g, w_uq, kv_a_norm_g, w_ukv, q_norm_g, k_norm_g,
           attn_out_norm_g, conv_w, conv_b, a_log_fwd, a_log_bwd, dt_bias_fwd, dt_bias_bwd, d_skip,
           ssm_norm_g, w_out, ln_mlp_g, w_mlp_up, w_mlp_down):
    batch, seq, d = x.shape
    assert d == D_MODEL and seq % TQ == 0 and (batch * seq) % TM_IN == 0 and (batch * seq) % TM_MLP == 0
    cs = _rope_table(positions)
    x2 = x.reshape(batch * seq, d)
    for l in range(ln_mix_g.shape[0]):
        x2 = _layer(x2, cs, batch, seq, ln_mix_g[l], (w_in, w_uq, w_ukv, l), q_a_norm_g[l], kv_a_norm_g[l],
                    q_norm_g[l], k_norm_g[l], attn_out_norm_g[l], conv_w[l], conv_b[l],
                    a_log_fwd[l], a_log_bwd[l], dt_bias_fwd[l], dt_bias_bwd[l], d_skip[l], ssm_norm_g[l],
                    w_out[l], ln_mlp_g[l], w_mlp_up[l], w_mlp_down[l])
    return x2.reshape(batch, seq, d)
```

```python
import types

import numpy as np
import jax
import jax.numpy as jnp
from jax import lax
from jax.experimental import pallas as pl
from jax.experimental.pallas import tpu as pltpu

F32 = jnp.float32
BF16 = jnp.bfloat16

D_MODEL = 1024
ATTN_HEADS = 8
QK_NOPE_DIM = 64
QK_ROPE_DIM = 32
QK_HEAD_DIM = QK_NOPE_DIM + QK_ROPE_DIM
V_HEAD_DIM = 64
Q_LORA_RANK = D_MODEL // 4
KV_LORA_RANK = D_MODEL // 8
ROPE_THETA = 10000.0
ATTN_WIDTH = ATTN_HEADS * V_HEAD_DIM
SSM_HEADS = 8
SSM_HEAD_DIM = 64
SSM_INNER = SSM_HEADS * SSM_HEAD_DIM
SSM_GROUPS = 2
SSM_STATE = 128
SSM_CONV = 5
SSM_CHUNK = 128
SSM_CONV_CH = SSM_INNER + 2 * SSM_GROUPS * SSM_STATE
D_MIX = ATTN_WIDTH + SSM_INNER
D_FF = 4 * D_MODEL
EPS = 1e-6

LANES = 128
SUBLANES = 8
HEAD_PAD = LANES

HEADS_PER_GROUP = SSM_HEADS // SSM_GROUPS
GROUP_INNER = SSM_INNER // SSM_GROUPS
GROUP_COLS = GROUP_INNER + 2 * SSM_STATE
DT_ROWS = 2 * HEADS_PER_GROUP
BF16_ROWS = 16
SEG_K_PER_ROW = 8
SEG_K = DT_ROWS * SEG_K_PER_ROW
SEG_BLOCKS = DT_ROWS + HEADS_PER_GROUP
LOG2E = float(np.log2(np.e))
CONV_WIN = 2 * SSM_CHUNK
CONV_WIN_LEAD = SSM_CHUNK // 2
CONV_SHIFTED_TAPS = tuple(k for k in range(SSM_CONV) if k != SSM_CONV // 2)

COL_CKV = Q_LORA_RANK
COL_MISC = COL_CKV + KV_LORA_RANK
COL_Z = COL_MISC + LANES
COL_XBC = COL_Z + SSM_INNER
IN_COLS = COL_XBC + SSM_CONV_CH

TM_IN = 1024
IN_SUBTILE = 512
TQ = 1024
ATTN_HEADS_PER_STEP = 8
TM_MLP = 1024
MLP_SUBTILE = 512
FF_CHUNK = 1024
CONV_AHEAD = 4
WIN_PREP_STEPS = 4
VMEM_LIMIT = 56 * 1024 * 1024


def _inv_rms(x):
    n = x.shape[-1]
    x2 = x * x
    acc = x2[:, 0:LANES]
    for i in range(1, n // LANES):
        acc = acc + x2[:, i * LANES:(i + 1) * LANES]
    return lax.rsqrt(jnp.sum(acc, axis=-1, keepdims=True) * (1.0 / n) + EPS)


def _dot(a, b):
    return jnp.dot(a, b, preferred_element_type=F32)


def _dot_nt(a, b):
    return lax.dot_general(a, b, (((1,), (1,)), ((), ())), preferred_element_type=F32)


def _inproj_kernel(x_ref, cs_ref, g_ref, win_ref, gqa_ref, wuq_ref, gkva_ref, wuk_ref, wuv_ref, vone_ref,
                   hg_ref,
                   q_ref, k_ref, v_ref, z_ref, xbc_ref, dt_ref):
    for sub in range(x_ref.shape[0] // IN_SUBTILE):
        _inproj_rows(sub * IN_SUBTILE, x_ref, cs_ref, g_ref, win_ref, gqa_ref, wuq_ref, gkva_ref, wuk_ref,
                     wuv_ref, vone_ref, hg_ref, q_ref, k_ref, v_ref, z_ref, xbc_ref, dt_ref)


def _inproj_rows(r0, x_ref, cs_ref, g_ref, win_ref, gqa_ref, wuq_ref, gkva_ref, wuk_ref, wuv_ref, vone_ref,
                 hg_ref, q_ref, k_ref, v_ref, z_ref, xbc_ref, dt_ref):
    tm = IN_SUBTILE
    rs = slice(r0, r0 + tm)
    x = x_ref[rs, :]
    h = (x * _inv_rms(x) * g_ref[...]).astype(BF16)
    big = _dot(h, win_ref[...])
    z_ref[rs, :] = big[:, COL_Z:COL_XBC].astype(BF16)
    xbc_ref[rs, :] = big[:, COL_XBC:IN_COLS].astype(BF16)
    misc = big[:, COL_MISC:COL_Z]
    for c in range(tm // SSM_CHUNK):
        dt_ref[r0 // SSM_CHUNK + c] = misc[c * SSM_CHUNK:(c + 1) * SSM_CHUNK, :].T[0:2 * SSM_HEADS, :]

    cq = big[:, 0:COL_CKV]
    ckv = big[:, COL_CKV:COL_MISC]
    cqn = (cq * _inv_rms(cq) * gqa_ref[...]).astype(BF16)
    ckvn = (ckv * _inv_rms(ckv) * gkva_ref[...]).astype(BF16)
    q_pre = _dot(cqn, wuq_ref[...])
    k_pre = _dot(ckvn, wuk_ref[...])
    v_ref[rs, :] = (_dot(ckvn, wuv_ref[...]) + vone_ref[...]).astype(BF16)

    half = QK_ROPE_DIM // 2
    zpad = jnp.zeros((HEAD_PAD - QK_ROPE_DIM, SSM_CHUNK), F32)
    tt = jnp.concatenate(
        [jnp.concatenate([cs_ref[:, r0 + c * SSM_CHUNK:r0 + (c + 1) * SSM_CHUNK], zpad], axis=0).T
         for c in range(tm // SSM_CHUNK)], axis=0)
    lane_t = lax.broadcasted_iota(jnp.int32, (tm, HEAD_PAD), 1)
    cs = jnp.where(lane_t < QK_NOPE_DIM, 1.0,
                   jnp.where(lane_t < QK_NOPE_DIM + half, pltpu.roll(tt, QK_NOPE_DIM, 1),
                             jnp.where(lane_t < QK_HEAD_DIM + half, pltpu.roll(tt, QK_NOPE_DIM + half, 1),
                                       pltpu.roll(tt, QK_HEAD_DIM, 1))))
    lane = lax.broadcasted_iota(jnp.int32, (1, HEAD_PAD), 1)
    in_head = (lane < QK_HEAD_DIM).astype(F32)
    is_rope = ((lane >= QK_NOPE_DIM) & (lane < QK_HEAD_DIM)).astype(F32)
    inv_d = 1.0 / QK_HEAD_DIM

    gcq = hg_ref[0:1, :] * cs
    for hh in range(ATTN_HEADS):
        sl = slice(hh * HEAD_PAD, (hh + 1) * HEAD_PAD)
        qh = q_pre[:, sl]
        ssq = jnp.sum(qh * qh * in_head, axis=-1, keepdims=True)
        q_ref[rs, sl] = (qh * lax.rsqrt(ssq * inv_d + EPS) * gcq).astype(BF16)

    ab = misc * (hg_ref[2:3, :] * cs)
    lane2 = lax.broadcasted_iota(jnp.int32, (tm, HEAD_PAD), 1)
    swapped = jnp.where(lane2 < QK_HEAD_DIM, pltpu.roll(ab, HEAD_PAD - QK_ROPE_DIM, 1),
                        pltpu.roll(ab, QK_ROPE_DIM, 1))
    s_both = jnp.where(lane2 >= QK_NOPE_DIM, ab + swapped, 0.0)
    ssq_pe = jnp.sum(misc * misc * is_rope, axis=-1, keepdims=True)
    gkn = hg_ref[1:2, :]
    for hh in range(ATTN_HEADS):
        sl = slice(hh * HEAD_PAD, (hh + 1) * HEAD_PAD)
        kh = k_pre[:, sl]
        ssq = jnp.sum(kh * kh, axis=-1, keepdims=True) + ssq_pe
        k_ref[rs, sl] = ((kh * gkn + s_both) * lax.rsqrt(ssq * inv_d + EPS)).astype(BF16)


def _win_prep_kernel(w_ref, wq_ref, wkv_ref, o_ref, wuq_ref, wuk_ref, wuv_ref):
    half = QK_ROPE_DIM // 2

    @pl.when(pl.program_id(0) == 0)
    def _up_projection_tiles():
        wq = wq_ref[...]
        q_pieces = []
        for hh in range(ATTN_HEADS):
            head = wq[:, hh * QK_HEAD_DIM:(hh + 1) * QK_HEAD_DIM]
            rope = head[:, QK_NOPE_DIM:]
            q_pieces += [head, -rope[:, half:], rope[:, :half]]
        wuq_ref[...] = jnp.concatenate(q_pieces, axis=1).astype(BF16)
        wkv = wkv_ref[...]
        zeros = jnp.zeros((wkv.shape[0], V_HEAD_DIM), F32)
        k_pieces, v_pieces = [], []
        for hh in range(ATTN_HEADS):
            nope = wkv[:, hh * HEAD_PAD:hh * HEAD_PAD + QK_NOPE_DIM]
            val = wkv[:, hh * HEAD_PAD + QK_NOPE_DIM:(hh + 1) * HEAD_PAD]
            k_pieces += [nope, zeros]
            v_pieces += [val, zeros] if hh % 2 == 0 else [zeros, val]
        wuk_ref[...] = jnp.concatenate(k_pieces, axis=1).astype(BF16)
        wuv_ref[...] = jnp.concatenate(v_pieces, axis=1).astype(BF16)

    w = w_ref[...]
    o_kpe = Q_LORA_RANK + KV_LORA_RANK
    o_z = o_kpe + QK_ROPE_DIM
    o_dt = o_z + SSM_INNER + SSM_CONV_CH
    hpg = HEADS_PER_GROUP
    dt_f = w[:, o_dt:o_dt + SSM_HEADS]
    dt_b = w[:, o_dt + SSM_HEADS:o_dt + 2 * SSM_HEADS]
    kpe = w[:, o_kpe:o_z]
    pieces = [w[:, 0:o_kpe]]
    for g in range(SSM_GROUPS):
        pieces += [dt_f[:, g * hpg:(g + 1) * hpg], dt_b[:, g * hpg:(g + 1) * hpg]]
    pieces += [jnp.zeros((w.shape[0], QK_NOPE_DIM - 2 * SSM_HEADS), F32), kpe, -kpe[:, half:], kpe[:, :half],
               w[:, o_z:o_dt]]
    o_ref[...] = jnp.concatenate(pieces, axis=1).astype(BF16)


def _win_prep_call(w_in3, w_uq3, w_ukv3, layer):
    depth, d, n = w_in3.shape
    rows = d // WIN_PREP_STEPS
    whole = lambda a: pl.BlockSpec(a.shape[1:], lambda i: (layer, 0))
    head_cols = ATTN_HEADS * HEAD_PAD
    return pl.pallas_call(
        _win_prep_kernel,
        grid=(WIN_PREP_STEPS,),
        in_specs=[pl.BlockSpec((rows, n), lambda i: (layer * WIN_PREP_STEPS + i, 0)), whole(w_uq3), whole(w_ukv3)],
        out_specs=[pl.BlockSpec((rows, IN_COLS), lambda i: (i, 0)),
                   pl.BlockSpec((Q_LORA_RANK, head_cols), lambda i: (0, 0)),
                   pl.BlockSpec((KV_LORA_RANK, head_cols), lambda i: (0, 0)),
                   pl.BlockSpec((KV_LORA_RANK, head_cols), lambda i: (0, 0))],
        out_shape=[jax.ShapeDtypeStruct((d, IN_COLS), BF16),
                   jax.ShapeDtypeStruct((Q_LORA_RANK, head_cols), BF16),
                   jax.ShapeDtypeStruct((KV_LORA_RANK, head_cols), BF16),
                   jax.ShapeDtypeStruct((KV_LORA_RANK, head_cols), BF16)],
        compiler_params=pltpu.CompilerParams(dimension_semantics=("arbitrary",), vmem_limit_bytes=VMEM_LIMIT),
        name="win_prep",
    )(w_in3.reshape(depth * d, n),
      w_uq3.reshape(depth * w_uq3.shape[1], w_uq3.shape[2]),
      w_ukv3.reshape(depth * w_ukv3.shape[1], w_ukv3.shape[2]))


def _inproj_call(x2, cs, g, win, gqa, wuq, gkva, wuk, wuv, vone, hg):
    t = x2.shape[0]
    cpt = TM_IN // SSM_CHUNK
    full = lambda a: pl.BlockSpec(a.shape, lambda i: (0,) * a.ndim, pipeline_mode=pl.Buffered(1))
    row = lambda w: pl.BlockSpec((TM_IN, w), lambda i: (i, 0))
    return pl.pallas_call(
        _inproj_kernel,
        grid=(t // TM_IN,),
        in_specs=[row(D_MODEL), pl.BlockSpec((QK_ROPE_DIM, TM_IN), lambda i: (0, i)), full(g), full(win), full(gqa), full(wuq), full(gkva),
                  full(wuk), full(wuv), full(vone), full(hg)],
        out_specs=[row(ATTN_HEADS * HEAD_PAD), row(ATTN_HEADS * HEAD_PAD), row(ATTN_HEADS * HEAD_PAD),
                   row(SSM_INNER), row(SSM_CONV_CH),
                   pl.BlockSpec((cpt, 2 * SSM_HEADS, SSM_CHUNK), lambda i: (i, 0, 0))],
        out_shape=[jax.ShapeDtypeStruct((t, ATTN_HEADS * HEAD_PAD), BF16),
                   jax.ShapeDtypeStruct((t, ATTN_HEADS * HEAD_PAD), BF16),
                   jax.ShapeDtypeStruct((t, ATTN_HEADS * HEAD_PAD), BF16),
                   jax.ShapeDtypeStruct((t, SSM_INNER), BF16),
                   jax.ShapeDtypeStruct((t, SSM_CONV_CH), BF16),
                   jax.ShapeDtypeStruct((t // SSM_CHUNK, 2 * SSM_HEADS, SSM_CHUNK), F32)],
        compiler_params=pltpu.CompilerParams(dimension_semantics=("arbitrary",),
                                             vmem_limit_bytes=VMEM_LIMIT),
        name="inproj",
    )(x2, cs, g, win, gqa, wuq, gkva, wuk, wuv, vone, hg)


def _attn_kernel(q_ref, k_ref, v_ref, *refs):
    n_w = (len(refs) - 1) // 2
    w_in_refs, o_ref, w_out_refs = refs[:n_w], refs[n_w], refs[n_w + 1:]
    for w_in_ref, w_out_ref in zip(w_in_refs, w_out_refs):
        w_out_ref[...] = w_in_ref[...].astype(BF16)

    lane = lax.broadcasted_iota(jnp.int32, (q_ref.shape[0], HEAD_PAD), 1)
    for jp in range(ATTN_HEADS_PER_STEP // 2):
        sls = [slice(j * HEAD_PAD, (j + 1) * HEAD_PAD) for j in (2 * jp, 2 * jp + 1)]
        ss = [_dot_nt(q_ref[:, sl], k_ref[:, sl]) for sl in sls]
        ps = [jnp.exp2(s - jnp.max(s, axis=-1, keepdims=True)).astype(BF16) for s in ss]
        accs = [_dot(p, v_ref[:, sl]) for p, sl in zip(ps, sls)]
        res = [acc / pltpu.roll(acc, V_HEAD_DIM, 1) for acc in accs]
        o_ref[:, jp * HEAD_PAD:(jp + 1) * HEAD_PAD] = jnp.where(lane < V_HEAD_DIM, res[0], res[1])


def _attn_call(q, k, v, later_weights, batch, seq):
    n_q = seq // TQ
    hps = ATTN_HEADS_PER_STEP
    assert ATTN_HEADS == hps
    n_steps = batch * n_q

    def rows_spec(w):
        assert w.shape[0] % (n_steps * BF16_ROWS) == 0
        return pl.BlockSpec((w.shape[0] // n_steps, w.shape[1]), lambda b, hp, i: (b * n_q + i, 0))
    w_specs = [rows_spec(w) for w in later_weights]
    outs = pl.pallas_call(
        _attn_kernel,
        grid=(batch, ATTN_HEADS // hps, n_q),
        in_specs=[pl.BlockSpec((TQ, hps * HEAD_PAD), lambda b, hp, i: (b * n_q + i, hp)),
                  pl.BlockSpec((seq, hps * HEAD_PAD), lambda b, hp, i: (b, hp)),
                  pl.BlockSpec((seq, hps * HEAD_PAD), lambda b, hp, i: (b, hp))] + w_specs,
        out_specs=[pl.BlockSpec((TQ, hps * V_HEAD_DIM), lambda b, hp, i: (b * n_q + i, hp))] + w_specs,
        out_shape=[jax.ShapeDtypeStruct((batch * seq, ATTN_WIDTH), F32)]
                  + [jax.ShapeDtypeStruct(w.shape, BF16) for w in later_weights],
        compiler_params=pltpu.CompilerParams(
            dimension_semantics=("arbitrary", "arbitrary", "arbitrary"),
            vmem_limit_bytes=VMEM_LIMIT),
        name="attention",
    )(q, k, v, *later_weights)
    return outs[0], outs[1:]


def _split3(x):
    hi = x.astype(BF16)
    r1 = x - hi.astype(F32)
    mid = r1.astype(BF16)
    lo = (r1 - mid.astype(F32)).astype(BF16)
    return hi, mid, lo


def _ssd_kernel(x_ref, b_ref, c_ref, dt_ref, sh_ref, place_ref, lhs_ones_ref, rhs_const_ref,
                cwx_ref, cwb_ref, cwc_ref, cbx_ref, cbb_ref, cbc_ref, adt_ref, dskip_ref,
                o_ref, xc_scr, xm_scr, nsf_scr, nsb_scr, dtv_scr, w_scr, dec_scr, lhs_scr, rhs_scr, rpart_scr):
    nc = dt_ref.shape[0]

    @pl.when(pl.program_id(0) == 0)
    def _init_rhs():
        for g in range(SSM_GROUPS):
            for c in range(nc):
                rhs_scr[g, c] = rhs_const_ref[...]

    groups = []
    for g in range(SSM_GROUPS):
        inner = slice(g * GROUP_INNER, (g + 1) * GROUP_INNER)
        state = slice(g * SSM_STATE, (g + 1) * SSM_STATE)
        rows = slice(g * DT_ROWS, (g + 1) * DT_ROWS)
        groups.append(_ssd_group_stages(
            x_ref.at[:, inner], b_ref.at[:, state], c_ref.at[:, state], dt_ref.at[:, rows, :],
            sh_ref, place_ref, lhs_ones_ref,
            cwx_ref.at[:, inner], cwb_ref.at[:, state], cwc_ref.at[:, state],
            cbx_ref.at[:, inner], cbb_ref.at[:, state], cbc_ref.at[:, state],
            adt_ref.at[:, rows, :], dskip_ref.at[:, inner], o_ref.at[:, inner],
            *[scr.at[g] for scr in (xc_scr, xm_scr, nsf_scr, nsb_scr, dtv_scr, w_scr, dec_scr, lhs_scr,
                                    rhs_scr, rpart_scr)]))
    first, second = groups

    def pass_a(grp, c):
        loaded = grp.state_loads(c)
        if c + CONV_AHEAD < nc:
            grp.conv_stage(c + CONV_AHEAD)
        grp.state_stage(c, loaded)

    zero_state = jnp.zeros((SSM_STATE, GROUP_INNER), F32)
    first.dt_phase()
    for c in range(CONV_AHEAD):
        first.conv_stage(c)
    for c in range(nc):
        pass_a(first, c)
        if c == 0:
            second.dt_phase()
        if c >= nc - CONV_AHEAD:
            second.conv_stage(c - (nc - CONV_AHEAD))
    st_f = st_b = zero_state
    for c in range(nc):
        pass_a(second, c)
        st_f = first.pass_b_step(0, c, st_f)
        st_b = first.pass_b_step(1, c, st_b)
    st_f = st_b = zero_state
    for c in range(nc):
        first.pass_c(c)
        st_f = second.pass_b_step(0, c, st_f)
        st_b = second.pass_b_step(1, c, st_b)
    for c in range(nc):
        second.pass_c(c)


def _ssd_group_stages(x_ref, b_ref, c_ref, dt_ref, sh_ref, place_ref, lhs_ones_ref,
                      cwx_ref, cwb_ref, cwc_ref, cbx_ref, cbb_ref, cbc_ref, adt_ref, dskip_ref,
                      o_ref, xc_scr, xm_scr, nsf_scr, nsb_scr, dtv_scr, w_scr, dec_scr, lhs_scr, rhs_scr, rpart_scr):
    nc = dt_ref.shape[0]
    seq = x_ref.shape[0]
    L = SSM_CHUNK
    hpg = HEADS_PER_GROUP
    npair = hpg // 2
    P = SSM_HEAD_DIM
    x_cols = slice(0, GROUP_INNER)
    b_cols = slice(GROUP_INNER, GROUP_INNER + SSM_STATE)
    c_cols = slice(GROUP_INNER + SSM_STATE, GROUP_COLS)
    pair_cols = [slice(j * L, (j + 1) * L) for j in range(npair)]

    ri = lax.broadcasted_iota(jnp.int32, (L, L), 0)
    ci = lax.broadcasted_iota(jnp.int32, (L, L), 1)

    def lane_bcast(col):
        return jnp.broadcast_to(col, (col.shape[0], L))

    def dt_phase():
        a_neg = -jnp.exp(adt_ref[0]) * LOG2E
        dtv = jax.nn.softplus(dt_ref[...] + adt_ref[1][None])
        dtv_scr[...] = dtv
        da2 = (dtv * a_neg[None]).reshape(nc * DT_ROWS, L)
        upper = (ri <= ci).astype(BF16)
        lower = (ri >= ci).astype(BF16)
        tri = jnp.concatenate([upper, lower], axis=1)
        cs_fb = sum(_dot(p, tri) for p in _split3(da2))
        rowsel = (ri & hpg) == 0
        cs2 = jnp.where(rowsel, cs_fb[:, 0:L], cs_fb[:, L:2 * L])
        colcs = cs2.T

        col_parts = jnp.concatenate(_split3(colcs), axis=1)
        half_n = place_ref.shape[1] // 2
        for hf in range(2):
            cols = slice(hf * half_n, (hf + 1) * half_n)
            lhs_half = _dot(col_parts, place_ref[:, cols]) + lhs_ones_ref[:, cols]
            for c in range(nc // 2):
                lhs_scr[hf * (nc // 2) + c] = lhs_half[:, c * L:c * L + SEG_K].astype(BF16)
        for i, part in enumerate(_split3(-cs2)):
            rpart_scr[i] = part.astype(F32).reshape(nc, DT_ROWS, L)

        tot2 = jnp.where(rowsel, lane_bcast(cs2[:, L - 1:L]), lane_bcast(cs2[:, 0:1]))
        dec_scr[...] = jnp.exp2(tot2).reshape(nc, DT_ROWS, L)
        w_scr[...] = (dtv.reshape(nc * DT_ROWS, L) * jnp.exp2(tot2 - cs2)).reshape(nc, DT_ROWS, L)

    low_half = ci < P

    def rows(hf, start, size):
        if hf == 0:
            return x_ref[pl.ds(start, size), :]
        return jnp.concatenate([b_ref[pl.ds(start, size), :], c_ref[pl.ds(start, size), :]], axis=1)

    def conv_stage(c):
        base = c * L
        ws = min(max(c * L - CONV_WIN_LEAD, 0), seq - CONV_WIN)
        variant = 0 if c == 0 else (2 if c == nc - 1 else 1)
        sh = sh_ref[variant]
        cw_halves = (cwx_ref[...], jnp.concatenate([cwb_ref[...], cwc_ref[...]], axis=1))
        cb_halves = (cbx_ref[...], jnp.concatenate([cbb_ref[...], cbc_ref[...]], axis=1))
        halves = []
        for hf in range(2):
            cw = cw_halves[hf]
            shifted = _dot(sh, rows(hf, ws, CONV_WIN))
            acc = cb_halves[hf] + cw[SSM_CONV // 2:SSM_CONV // 2 + 1, :] * rows(hf, base, L).astype(F32)
            for t, kk in enumerate(CONV_SHIFTED_TAPS):
                acc = acc + cw[kk:kk + 1, :] * shifted[t * L:(t + 1) * L, :]
            halves.append(acc * jax.nn.sigmoid(acc))
        xc = jnp.concatenate(halves, axis=1)
        xc_scr[c] = xc
        for j in range(npair):
            xp = xc[:, pair_cols[j]]
            xm_scr[c, j] = jnp.concatenate([jnp.where(low_half, xp, 0.0), jnp.where(low_half, 0.0, xp)],
                                           axis=0).astype(BF16)

    def state_loads(c):
        return xc_scr[c, :, b_cols], [xm_scr[c, j] for j in range(npair)]

    def state_stage(c, loaded):
        b_tok, xm = loaded
        bt = b_tok.T
        for d, ns_scr in ((0, nsf_scr), (1, nsb_scr)):
            w = w_scr[c, d * hpg:(d + 1) * hpg, :]
            ns_scr[c] = jnp.concatenate(
                [_dot(jnp.concatenate([(bt * w[2 * j:2 * j + 1, :]).astype(BF16),
                                       (bt * w[2 * j + 1:2 * j + 2, :]).astype(BF16)], axis=1), xm[j])
                 for j in range(npair)], axis=1)

    lane_r = lax.broadcasted_iota(jnp.int32, (1, L), 1)

    def decay_row(c, d):
        dec = dec_scr[c, d * hpg:(d + 1) * hpg, :]
        return jnp.concatenate([jnp.where(lane_r < P, dec[2 * j:2 * j + 1, :], dec[2 * j + 1:2 * j + 2, :])
                                for j in range(npair)], axis=1)

    def pass_b_step(d, i, st):
        ns_scr = nsf_scr if d == 0 else nsb_scr
        c = i if d == 0 else nc - 1 - i
        new = ns_scr[c]
        ns_scr[c] = st
        return st * decay_row(c, d) + new

    mask_f = ci <= ri
    mask_b = ci >= ri
    neg_inf = jnp.float32(-jnp.inf)
    sub16 = lax.broadcasted_iota(jnp.int32, (BF16_ROWS, L), 0)

    def pass_c(c):
        base = c * L
        xc = xc_scr[c]
        bm = xc[:, b_cols].astype(BF16)
        cm = xc[:, c_cols].astype(BF16)
        cb = _dot_nt(cm, bm)
        off_f = _dot(cm, nsf_scr[c].astype(BF16))
        off_b = _dot(cm, nsb_scr[c].astype(BF16))
        for hd in range(DT_ROWS):
            r0 = (hd % 2) * SEG_K_PER_ROW
            blk = jnp.where((sub16 >= r0) & (sub16 < r0 + 3), 1.0, 0.0)
            for i in range(3):
                blk = jnp.where(sub16 == r0 + 3 + i, rpart_scr[i, c, hd:hd + 1, :], blk)
            rhs_scr[c, (hd // 2) * BF16_ROWS:(hd // 2 + 1) * BF16_ROWS, hd * L:(hd + 1) * L] = blk.astype(BF16)
        seg = _dot(lhs_scr[c], rhs_scr[c])
        dt_f_r = dtv_scr[c, 0:hpg, :]
        dt_b_r = dtv_scr[c, hpg:DT_ROWS, :]
        ys = []
        for j in range(npair):
            ws_ = []
            for h in (2 * j, 2 * j + 1):
                ef = jnp.exp2(jnp.where(mask_f, seg[:, h * L:(h + 1) * L], neg_inf))
                eb = jnp.exp2(jnp.where(mask_b, seg[:, (hpg + h) * L:(hpg + h + 1) * L], neg_inf))
                ws_.append((cb * (ef * dt_f_r[h:h + 1, :] + eb * dt_b_r[h:h + 1, :])).astype(BF16))
            scale_f = jnp.exp2(seg[:, (DT_ROWS + j) * L:(DT_ROWS + j + 1) * L])
            scale_b = jnp.exp2(seg[:, (DT_ROWS + npair + j) * L:(DT_ROWS + npair + j + 1) * L])
            ys.append(_dot(jnp.concatenate(ws_, axis=1), xm_scr[c, j])
                      + scale_f * off_f[:, pair_cols[j]] + scale_b * off_b[:, pair_cols[j]])
        y = jnp.concatenate(ys, axis=1) + dskip_ref[...] * xc[:, x_cols]
        o_ref[pl.ds(base, L), :] = y

    return types.SimpleNamespace(dt_phase=dt_phase, conv_stage=conv_stage, state_loads=state_loads,
                                 state_stage=state_stage, pass_b_step=pass_b_step, pass_c=pass_c)


def _conv_shift_matrices():
    out = np.zeros((3, len(CONV_SHIFTED_TAPS) * SSM_CHUNK, CONV_WIN), np.float32)
    for v, lead in enumerate((0, CONV_WIN_LEAD, CONV_WIN - SSM_CHUNK)):
        for i, kk in enumerate(CONV_SHIFTED_TAPS):
            for t in range(SSM_CHUNK):
                j = t + kk - SSM_CONV // 2 + lead
                if 0 <= j < CONV_WIN:
                    out[v, i * SSM_CHUNK + t, j] = 1.0
    return out


def _seg_matmul_constants(nc):
    L, K, half = SSM_CHUNK, SEG_K_PER_ROW, SSM_HEAD_DIM
    place = np.zeros((3 * L, nc * L), np.float32)
    lhs_ones = np.zeros((1, nc * L), np.float32)
    for c in range(nc):
        for hd in range(DT_ROWS):
            for i in range(3):
                place[i * L + c * DT_ROWS + hd, c * L + hd * K + i] = 1.0
                lhs_ones[0, c * L + hd * K + 3 + i] = 1.0
    rhs = np.zeros((SEG_K, SEG_BLOCKS * L), np.float32)
    for hd in range(DT_ROWS):
        d, h = divmod(hd, HEADS_PER_GROUP)
        pair_block = DT_ROWS + d * (HEADS_PER_GROUP // 2) + h // 2
        lanes = slice(0, half) if h % 2 == 0 else slice(half, L)
        rhs[hd * K:hd * K + 3, hd * L:(hd + 1) * L] = 1.0
        rhs[hd * K:hd * K + 3, pair_block * L:(pair_block + 1) * L][:, lanes] = 1.0
    return place, lhs_ones, rhs


def _ssd_call(xbc, dtc, cw, cbias, adt, dskip, batch, seq):
    nc = seq // SSM_CHUNK
    L = SSM_CHUNK
    assert nc >= 3 and seq >= CONV_WIN and nc * DT_ROWS == L
    sh = jnp.asarray(_conv_shift_matrices(), BF16)
    place, lhs_ones, rhs_const = _seg_matmul_constants(nc)
    place = jnp.asarray(place, BF16)
    lhs_ones = jnp.asarray(lhs_ones, F32)
    rhs_const = jnp.asarray(rhs_const, BF16)
    const = lambda a: pl.BlockSpec(a.shape, lambda b: (0,) * a.ndim)
    G = SSM_GROUPS
    bc_cols = G * SSM_STATE
    b_blk = SSM_INNER // bc_cols
    c_blk = b_blk + 1

    def xbc_views(rows_, batched):
        lead = (lambda b: b) if batched else (lambda b: 0)
        return [pl.BlockSpec((rows_, SSM_INNER), lambda b: (lead(b), 0)),
                pl.BlockSpec((rows_, bc_cols), lambda b: (lead(b), b_blk)),
                pl.BlockSpec((rows_, bc_cols), lambda b: (lead(b), c_blk))]
    return pl.pallas_call(
        _ssd_kernel,
        grid=(batch,),
        in_specs=xbc_views(seq, True) + [
                  pl.BlockSpec((nc, G * DT_ROWS, L), lambda b: (b, 0, 0)),
                  const(sh), const(place), const(lhs_ones), const(rhs_const)]
                 + xbc_views(SUBLANES, False) + xbc_views(1, False) + [const(adt), const(dskip)],
        out_specs=pl.BlockSpec((seq, SSM_INNER), lambda b: (b, 0)),
        out_shape=jax.ShapeDtypeStruct((batch * seq, SSM_INNER), F32),
        scratch_shapes=[pltpu.VMEM((G, nc, L, GROUP_COLS), F32),
                        pltpu.VMEM((G, nc, HEADS_PER_GROUP // 2, 2 * L, L), BF16),
                        pltpu.VMEM((G, nc, SSM_STATE, GROUP_INNER), F32),
                        pltpu.VMEM((G, nc, SSM_STATE, GROUP_INNER), F32),
                        pltpu.VMEM((G, nc, DT_ROWS, L), F32),
                        pltpu.VMEM((G, nc, DT_ROWS, L), F32),
                        pltpu.VMEM((G, nc, DT_ROWS, L), F32),
                        pltpu.VMEM((G, nc, L, SEG_K), BF16),
                        pltpu.VMEM((G, nc, SEG_K, SEG_BLOCKS * L), BF16),
                        pltpu.VMEM((G, 3, nc, DT_ROWS, L), F32)],
        compiler_params=pltpu.CompilerParams(dimension_semantics=("arbitrary",),
                                             vmem_limit_bytes=VMEM_LIMIT),
        name="ssd",
    )(xbc, xbc, xbc, dtc, sh, place, lhs_ones, rhs_const, cw, cw, cw, cbias, cbias, cbias,
      adt, dskip)


def _mlp_kernel(x_ref, attn_ref, ssm_ref, z_ref, gat_ref, gn_ref, wo_ref, gm_ref, wup_ref, wdn_ref, o_ref):
    for sub in range(x_ref.shape[0] // MLP_SUBTILE):
        rs = slice(sub * MLP_SUBTILE, (sub + 1) * MLP_SUBTILE)
        a = attn_ref[rs, :]
        an = (a * _inv_rms(a) * gat_ref[...]).astype(BF16)
        zc = z_ref[rs, :].astype(F32)
        y = ssm_ref[rs, :] * (zc * jax.nn.sigmoid(zc))
        yn = jnp.concatenate([y[:, g * GROUP_INNER:(g + 1) * GROUP_INNER]
                              * _inv_rms(y[:, g * GROUP_INNER:(g + 1) * GROUP_INNER]) for g in range(SSM_GROUPS)],
                             axis=1) * gn_ref[...]
        mix = _dot(an, wo_ref[0:ATTN_WIDTH, :]) + _dot(yn.astype(BF16), wo_ref[ATTN_WIDTH:D_MIX, :])
        x1 = x_ref[rs, :] + mix
        hm = (x1 * _inv_rms(x1) * gm_ref[...]).astype(BF16)
        acc = jnp.zeros_like(x1)
        for c in range(D_FF // FF_CHUNK):
            cols = slice(c * FF_CHUNK, (c + 1) * FF_CHUNK)
            u = _dot(hm, wup_ref[:, cols])
            acc = acc + _dot(jnp.square(jnp.maximum(u, 0.0)).astype(BF16), wdn_ref[cols, :])
        o_ref[rs, :] = x1 + acc


def _mlp_call(x2, attn, ssm, z, gat, gn, wo, gm, wup, wdn):
    t = x2.shape[0]
    row = lambda w: pl.BlockSpec((TM_MLP, w), lambda i: (i, 0))
    full = lambda a: pl.BlockSpec(a.shape, lambda i: (0, 0), pipeline_mode=pl.Buffered(1))
    return pl.pallas_call(
        _mlp_kernel,
        grid=(t // TM_MLP,),
        in_specs=[row(D_MODEL), row(ATTN_WIDTH), row(SSM_INNER), row(SSM_INNER), full(gat), full(gn), full(wo),
                  full(gm), full(wup), full(wdn)],
        out_specs=row(D_MODEL),
        out_shape=jax.ShapeDtypeStruct((t, D_MODEL), F32),
        compiler_params=pltpu.CompilerParams(dimension_semantics=("arbitrary",),
                                             vmem_limit_bytes=VMEM_LIMIT),
        name="outproj_mlp",
    )(x2, attn, ssm, z, gat, gn, wo, gm, wup, wdn)


def _swap_halves(g):
    half = QK_ROPE_DIM // 2
    return jnp.concatenate([g[..., half:], g[..., :half]], axis=-1)


def _lane_bcast(v):
    return jnp.broadcast_to(v[..., None], v.shape + (LANES,)).astype(F32)


def _layer(x2, cs, batch, seq, ln_mix_g, stacked_proj_weights, q_a_norm_g, kv_a_norm_g, q_norm_g,
           k_norm_g, attn_out_norm_g, conv_w, conv_b, a_log_fwd, a_log_bwd, dt_bias_fwd, dt_bias_bwd,
           d_skip, ssm_norm_g, w_out, ln_mlp_g, w_mlp_up, w_mlp_down):
    win, wuq, wuk, wuv = _win_prep_call(*stacked_proj_weights)
    ones_v = np.ones((ATTN_HEADS // 2, V_HEAD_DIM), np.float32)
    vone = jnp.asarray(np.stack([np.concatenate([0 * ones_v, ones_v], axis=-1),
                                 np.concatenate([ones_v, 0 * ones_v], axis=-1)], axis=1).reshape(1, -1))
    scale = QK_HEAD_DIM ** -0.5 * np.log2(np.e)
    zeros_nope = jnp.zeros((QK_NOPE_DIM,), k_norm_g.dtype)
    k_rope_g = k_norm_g[QK_NOPE_DIM:]
    hg = jnp.concatenate([q_norm_g * scale, _swap_halves(q_norm_g[QK_NOPE_DIM:]) * scale,
                          k_norm_g[:QK_NOPE_DIM], zeros_nope,
                          zeros_nope, k_rope_g, _swap_halves(k_rope_g),
                          jnp.zeros(((SUBLANES - 3) * HEAD_PAD,), k_norm_g.dtype)]).reshape(SUBLANES, HEAD_PAD)

    q, k, v, z, xbc, dtc = _inproj_call(
        x2, cs, ln_mix_g[None, :], win, q_a_norm_g[None, :], wuq, kv_a_norm_g[None, :], wuk, wuv, vone,
        hg)

    attn, (wo_b, wup_b, wdn_b) = _attn_call(q, k, v, (w_out, w_mlp_up, w_mlp_down), batch, seq)

    cw = jnp.pad(conv_w[:, 0, :], ((0, SUBLANES - SSM_CONV), (0, 0)))
    cbias = conv_b[None, :]
    adt = jnp.stack([a_log_fwd, a_log_bwd, dt_bias_fwd, dt_bias_bwd]).reshape(2, 2, SSM_GROUPS, HEADS_PER_GROUP)
    adt = _lane_bcast(adt.transpose(0, 2, 1, 3).reshape(2, 2 * SSM_HEADS))
    dskip = jnp.repeat(d_skip, SSM_HEAD_DIM)[None, :]
    ssm = _ssd_call(xbc, dtc, cw, cbias, adt, dskip, batch, seq)

    return _mlp_call(x2, attn, ssm, z, attn_out_norm_g[None, :], ssm_norm_g[None, :], wo_b,
                     ln_mlp_g[None, :], wup_b, wdn_b)


def _rope_table(positions):
    inv_freq = 1.0 / (ROPE_THETA ** (jnp.arange(0, QK_ROPE_DIM, 2, dtype=F32) / QK_ROPE_DIM))
    ang = inv_freq[:, None] * positions.astype(F32).reshape(1, -1)
    return jnp.concatenate([jnp.cos(ang), jnp.sin(ang)], axis=0)


def kernel(x, positions, ln_mix_g, w_in, q_a_norm_g, w_uq, kv_a_norm_g, w_ukv, q_norm_g, k_norm_g,
           attn_out_norm_g, conv_w, conv_b, a_log_fwd, a_log_bwd, dt_bias_fwd, dt_bias_bwd, d_skip,
           ssm_norm_g, w_out, ln_mlp_g, w_mlp_up, w_mlp_down):
    batch, seq, d = x.shape
    assert d == D_MODEL and seq % TQ == 0 and (batch * seq) % TM_IN == 0 and (batch * seq) % TM_MLP == 0
    cs = _rope_table(positions)
    x2 = x.reshape(batch * seq, d)
    for l in range(ln_mix_g.shape[0]):
        x2 = _layer(x2, cs, batch, seq, ln_mix_g[l], (w_in, w_uq, w_ukv, l), q_a_norm_g[l], kv_a_norm_g[l],
                    q_norm_g[l], k_norm_g[l], attn_out_norm_g[l], conv_w[l], conv_b[l],
                    a_log_fwd[l], a_log_bwd[l], dt_bias_fwd[l], dt_bias_bwd[l], d_skip[l], ssm_norm_g[l],
                    w_out[l], ln_mlp_g[l], w_mlp_up[l], w_mlp_down[l])
    return x2.reshape(batch, seq, d)
```

```python
import numpy as np
import jax
import jax.numpy as jnp
from jax import lax
from jax.experimental import pallas as pl
from jax.experimental.pallas import tpu as pltpu

F32 = jnp.float32
BF16 = jnp.bfloat16

D_MODEL = 1024
ATTN_HEADS = 8
QK_NOPE_DIM = 64
QK_ROPE_DIM = 32
QK_HEAD_DIM = QK_NOPE_DIM + QK_ROPE_DIM
V_HEAD_DIM = 64
Q_LORA_RANK = D_MODEL // 4
KV_LORA_RANK = D_MODEL // 8
ROPE_THETA = 10000.0
ATTN_WIDTH = ATTN_HEADS * V_HEAD_DIM
SSM_HEADS = 8
SSM_HEAD_DIM = 64
SSM_INNER = SSM_HEADS * SSM_HEAD_DIM
SSM_GROUPS = 2
SSM_STATE = 128
SSM_CONV = 5
SSM_CHUNK = 128
SSM_CONV_CH = SSM_INNER + 2 * SSM_GROUPS * SSM_STATE
D_MIX = ATTN_WIDTH + SSM_INNER
D_FF = 4 * D_MODEL
EPS = 1e-6

LANES = 128
SUBLANES = 8
HEAD_PAD = LANES

HEADS_PER_GROUP = SSM_HEADS // SSM_GROUPS
GROUP_INNER = SSM_INNER // SSM_GROUPS
GROUP_COLS = GROUP_INNER + 2 * SSM_STATE
DT_ROWS = 2 * HEADS_PER_GROUP
BF16_ROWS = 16
SEG_K_PER_ROW = 8
SEG_K = DT_ROWS * SEG_K_PER_ROW
SEG_BLOCKS = DT_ROWS + HEADS_PER_GROUP
LOG2E = float(np.log2(np.e))
CONV_WIN = 2 * SSM_CHUNK
CONV_WIN_LEAD = SSM_CHUNK // 2
CONV_SHIFTED_TAPS = tuple(k for k in range(SSM_CONV) if k != SSM_CONV // 2)

COL_CKV = Q_LORA_RANK
COL_MISC = COL_CKV + KV_LORA_RANK
COL_Z = COL_MISC + LANES
COL_XBC = COL_Z + SSM_INNER
IN_COLS = COL_XBC + SSM_CONV_CH

TM_IN = 1024
IN_SUBTILE = 512
TQ = 1024
ATTN_HEADS_PER_STEP = 8
TM_MLP = 1024
MLP_SUBTILE = 512
FF_CHUNK = 1024
SSD_UNROLL = 16
CONV_AHEAD = 4
PASS_A_UNROLL = 12
WIN_PREP_STEPS = 4
VMEM_LIMIT = 56 * 1024 * 1024


def _inv_rms(x):
    n = x.shape[-1]
    x2 = x * x
    acc = x2[:, 0:LANES]
    for i in range(1, n // LANES):
        acc = acc + x2[:, i * LANES:(i + 1) * LANES]
    return lax.rsqrt(jnp.sum(acc, axis=-1, keepdims=True) * (1.0 / n) + EPS)


def _dot(a, b):
    return jnp.dot(a, b, preferred_element_type=F32)


def _dot_nt(a, b):
    return lax.dot_general(a, b, (((1,), (1,)), ((), ())), preferred_element_type=F32)


def _inproj_kernel(x_ref, cs_ref, g_ref, win_ref, gqa_ref, wuq_ref, gkva_ref, wuk_ref, wuv_ref, vone_ref,
                   hg_ref,
                   q_ref, k_ref, v_ref, z_ref, xbc_ref, dt_ref):
    for sub in range(x_ref.shape[0] // IN_SUBTILE):
        _inproj_rows(sub * IN_SUBTILE, x_ref, cs_ref, g_ref, win_ref, gqa_ref, wuq_ref, gkva_ref, wuk_ref,
                     wuv_ref, vone_ref, hg_ref, q_ref, k_ref, v_ref, z_ref, xbc_ref, dt_ref)


def _inproj_rows(r0, x_ref, cs_ref, g_ref, win_ref, gqa_ref, wuq_ref, gkva_ref, wuk_ref, wuv_ref, vone_ref,
                 hg_ref, q_ref, k_ref, v_ref, z_ref, xbc_ref, dt_ref):
    tm = IN_SUBTILE
    rs = slice(r0, r0 + tm)
    x = x_ref[rs, :]
    h = (x * _inv_rms(x) * g_ref[...]).astype(BF16)
    big = _dot(h, win_ref[...])
    z_ref[rs, :] = big[:, COL_Z:COL_XBC].astype(BF16)
    xbc_ref[rs, :] = big[:, COL_XBC:IN_COLS].astype(BF16)
    misc = big[:, COL_MISC:COL_Z]
    for c in range(tm // SSM_CHUNK):
        dt_ref[r0 // SSM_CHUNK + c] = misc[c * SSM_CHUNK:(c + 1) * SSM_CHUNK, :].T[0:2 * SSM_HEADS, :]

    cq = big[:, 0:COL_CKV]
    ckv = big[:, COL_CKV:COL_MISC]
    cqn = (cq * _inv_rms(cq) * gqa_ref[...]).astype(BF16)
    ckvn = (ckv * _inv_rms(ckv) * gkva_ref[...]).astype(BF16)
    q_pre = _dot(cqn, wuq_ref[...])
    k_pre = _dot(ckvn, wuk_ref[...])
    v_ref[rs, :] = (_dot(ckvn, wuv_ref[...]) + vone_ref[...]).astype(BF16)

    half = QK_ROPE_DIM // 2
    zpad = jnp.zeros((HEAD_PAD - QK_ROPE_DIM, SSM_CHUNK), F32)
    tt = jnp.concatenate(
        [jnp.concatenate([cs_ref[:, r0 + c * SSM_CHUNK:r0 + (c + 1) * SSM_CHUNK], zpad], axis=0).T
         for c in range(tm // SSM_CHUNK)], axis=0)
    lane_t = lax.broadcasted_iota(jnp.int32, (tm, HEAD_PAD), 1)
    cs = jnp.where(lane_t < QK_NOPE_DIM, 1.0,
                   jnp.where(lane_t < QK_NOPE_DIM + half, pltpu.roll(tt, QK_NOPE_DIM, 1),
                             jnp.where(lane_t < QK_HEAD_DIM + half, pltpu.roll(tt, QK_NOPE_DIM + half, 1),
                                       pltpu.roll(tt, QK_HEAD_DIM, 1))))
    lane = lax.broadcasted_iota(jnp.int32, (1, HEAD_PAD), 1)
    in_head = (lane < QK_HEAD_DIM).astype(F32)
    is_rope = ((lane >= QK_NOPE_DIM) & (lane < QK_HEAD_DIM)).astype(F32)
    inv_d = 1.0 / QK_HEAD_DIM

    gcq = hg_ref[0:1, :] * cs
    for hh in range(ATTN_HEADS):
        sl = slice(hh * HEAD_PAD, (hh + 1) * HEAD_PAD)
        qh = q_pre[:, sl]
        ssq = jnp.sum(qh * qh * in_head, axis=-1, keepdims=True)
        q_ref[rs, sl] = (qh * lax.rsqrt(ssq * inv_d + EPS) * gcq).astype(BF16)

    ab = misc * (hg_ref[2:3, :] * cs)
    lane2 = lax.broadcasted_iota(jnp.int32, (tm, HEAD_PAD), 1)
    swapped = jnp.where(lane2 < QK_HEAD_DIM, pltpu.roll(ab, HEAD_PAD - QK_ROPE_DIM, 1),
                        pltpu.roll(ab, QK_ROPE_DIM, 1))
    s_both = jnp.where(lane2 >= QK_NOPE_DIM, ab + swapped, 0.0)
    ssq_pe = jnp.sum(misc * misc * is_rope, axis=-1, keepdims=True)
    gkn = hg_ref[1:2, :]
    for hh in range(ATTN_HEADS):
        sl = slice(hh * HEAD_PAD, (hh + 1) * HEAD_PAD)
        kh = k_pre[:, sl]
        ssq = jnp.sum(kh * kh, axis=-1, keepdims=True) + ssq_pe
        k_ref[rs, sl] = ((kh * gkn + s_both) * lax.rsqrt(ssq * inv_d + EPS)).astype(BF16)


def _win_prep_kernel(wt_ref, wq_ref, wkv_ref, o_ref, wuq_ref, wuk_ref, wuv_ref):
    half = QK_ROPE_DIM // 2

    @pl.when(pl.program_id(0) == 0)
    def _up_projection_tiles():
        wq = wq_ref[...]
        q_pieces = []
        for hh in range(ATTN_HEADS):
            head = wq[:, hh * QK_HEAD_DIM:(hh + 1) * QK_HEAD_DIM]
            rope = head[:, QK_NOPE_DIM:]
            q_pieces += [head, -rope[:, half:], rope[:, :half]]
        wuq_ref[...] = jnp.concatenate(q_pieces, axis=1).astype(BF16)
        wkv = wkv_ref[...]
        zeros = jnp.zeros((wkv.shape[0], V_HEAD_DIM), F32)
        k_pieces, v_pieces = [], []
        for hh in range(ATTN_HEADS):
            nope = wkv[:, hh * HEAD_PAD:hh * HEAD_PAD + QK_NOPE_DIM]
            val = wkv[:, hh * HEAD_PAD + QK_NOPE_DIM:(hh + 1) * HEAD_PAD]
            k_pieces += [nope, zeros]
            v_pieces += [val, zeros] if hh % 2 == 0 else [zeros, val]
        wuk_ref[...] = jnp.concatenate(k_pieces, axis=1).astype(BF16)
        wuv_ref[...] = jnp.concatenate(v_pieces, axis=1).astype(BF16)

    cols = wt_ref.shape[1]
    o_kpe = Q_LORA_RANK + KV_LORA_RANK
    o_z = o_kpe + QK_ROPE_DIM
    o_dt = o_z + SSM_INNER + SSM_CONV_CH
    hpg = HEADS_PER_GROUP
    assert SSM_HEADS == SUBLANES and SSM_GROUPS == 2
    dt_f = wt_ref[o_dt:o_dt + SSM_HEADS, :]
    dt_b = wt_ref[o_dt + SSM_HEADS:o_dt + 2 * SSM_HEADS, :]
    top = lax.broadcasted_iota(jnp.int32, (SUBLANES, cols), 0) < hpg
    dt_g0 = jnp.where(top, dt_f, pltpu.roll(dt_b, hpg, 0))
    dt_g1 = jnp.where(top, pltpu.roll(dt_f, hpg, 0), dt_b)
    kpe = wt_ref[o_kpe:o_z, :]
    rows = jnp.concatenate(
        [wt_ref[0:o_kpe, :], dt_g0, dt_g1, jnp.zeros((QK_NOPE_DIM - 2 * SSM_HEADS, cols), F32),
         kpe, -kpe[half:, :], kpe[:half, :], wt_ref[o_z:o_dt, :]], axis=0)
    for ct in range(cols // LANES):
        o_ref[ct * LANES:(ct + 1) * LANES, :] = jnp.concatenate(
            [rows[rt * LANES:(rt + 1) * LANES, ct * LANES:(ct + 1) * LANES].T for rt in range(IN_COLS // LANES)],
            axis=1).astype(BF16)


def _win_prep_call(w_in3, w_uq3, w_ukv3, layer):
    depth, d, n = w_in3.shape
    cols = d // WIN_PREP_STEPS
    whole = lambda a: pl.BlockSpec(a.shape[1:], lambda i: (layer, 0))
    head_cols = ATTN_HEADS * HEAD_PAD
    return pl.pallas_call(
        _win_prep_kernel,
        grid=(WIN_PREP_STEPS,),
        in_specs=[pl.BlockSpec((n, cols), lambda i: (layer, i)), whole(w_uq3), whole(w_ukv3)],
        out_specs=[pl.BlockSpec((cols, IN_COLS), lambda i: (i, 0)),
                   pl.BlockSpec((Q_LORA_RANK, head_cols), lambda i: (0, 0)),
                   pl.BlockSpec((KV_LORA_RANK, head_cols), lambda i: (0, 0)),
                   pl.BlockSpec((KV_LORA_RANK, head_cols), lambda i: (0, 0))],
        out_shape=[jax.ShapeDtypeStruct((d, IN_COLS), BF16),
                   jax.ShapeDtypeStruct((Q_LORA_RANK, head_cols), BF16),
                   jax.ShapeDtypeStruct((KV_LORA_RANK, head_cols), BF16),
                   jax.ShapeDtypeStruct((KV_LORA_RANK, head_cols), BF16)],
        compiler_params=pltpu.CompilerParams(dimension_semantics=("arbitrary",), vmem_limit_bytes=VMEM_LIMIT),
        name="win_prep",
    )(jnp.transpose(w_in3, (0, 2, 1)).reshape(depth * n, d),
      w_uq3.reshape(depth * w_uq3.shape[1], w_uq3.shape[2]),
      w_ukv3.reshape(depth * w_ukv3.shape[1], w_ukv3.shape[2]))


def _inproj_call(x2, cs, g, win, gqa, wuq, gkva, wuk, wuv, vone, hg):
    t = x2.shape[0]
    cpt = TM_IN // SSM_CHUNK
    full = lambda a: pl.BlockSpec(a.shape, lambda i: (0,) * a.ndim, pipeline_mode=pl.Buffered(1))
    row = lambda w: pl.BlockSpec((TM_IN, w), lambda i: (i, 0))
    return pl.pallas_call(
        _inproj_kernel,
        grid=(t // TM_IN,),
        in_specs=[row(D_MODEL), pl.BlockSpec((QK_ROPE_DIM, TM_IN), lambda i: (0, i)), full(g), full(win), full(gqa), full(wuq), full(gkva),
                  full(wuk), full(wuv), full(vone), full(hg)],
        out_specs=[row(ATTN_HEADS * HEAD_PAD), row(ATTN_HEADS * HEAD_PAD), row(ATTN_HEADS * HEAD_PAD),
                   row(SSM_INNER), row(SSM_CONV_CH),
                   pl.BlockSpec((cpt, 2 * SSM_HEADS, SSM_CHUNK), lambda i: (i, 0, 0))],
        out_shape=[jax.ShapeDtypeStruct((t, ATTN_HEADS * HEAD_PAD), BF16),
                   jax.ShapeDtypeStruct((t, ATTN_HEADS * HEAD_PAD), BF16),
                   jax.ShapeDtypeStruct((t, ATTN_HEADS * HEAD_PAD), BF16),
                   jax.ShapeDtypeStruct((t, SSM_INNER), BF16),
                   jax.ShapeDtypeStruct((t, SSM_CONV_CH), BF16),
                   jax.ShapeDtypeStruct((t // SSM_CHUNK, 2 * SSM_HEADS, SSM_CHUNK), F32)],
        compiler_params=pltpu.CompilerParams(dimension_semantics=("arbitrary",),
                                             vmem_limit_bytes=VMEM_LIMIT),
        name="inproj",
    )(x2, cs, g, win, gqa, wuq, gkva, wuk, wuv, vone, hg)


def _attn_kernel(q_ref, k_ref, v_ref, *refs):
    n_w = (len(refs) - 1) // 2
    w_in_refs, o_ref, w_out_refs = refs[:n_w], refs[n_w], refs[n_w + 1:]
    for w_in_ref, w_out_ref in zip(w_in_refs, w_out_refs):
        w_out_ref[...] = w_in_ref[...].astype(BF16)

    lane = lax.broadcasted_iota(jnp.int32, (q_ref.shape[0], HEAD_PAD), 1)
    for jp in range(ATTN_HEADS_PER_STEP // 2):
        sls = [slice(j * HEAD_PAD, (j + 1) * HEAD_PAD) for j in (2 * jp, 2 * jp + 1)]
        ss = [_dot_nt(q_ref[:, sl], k_ref[:, sl]) for sl in sls]
        ps = [jnp.exp2(s - jnp.max(s, axis=-1, keepdims=True)).astype(BF16) for s in ss]
        accs = [_dot(p, v_ref[:, sl]) for p, sl in zip(ps, sls)]
        res = [acc / pltpu.roll(acc, V_HEAD_DIM, 1) for acc in accs]
        o_ref[:, jp * HEAD_PAD:(jp + 1) * HEAD_PAD] = jnp.where(lane < V_HEAD_DIM, res[0], res[1])


def _attn_call(q, k, v, later_weights, batch, seq):
    n_q = seq // TQ
    hps = ATTN_HEADS_PER_STEP
    assert ATTN_HEADS == hps
    n_steps = batch * n_q

    def rows_spec(w):
        assert w.shape[0] % (n_steps * BF16_ROWS) == 0
        return pl.BlockSpec((w.shape[0] // n_steps, w.shape[1]), lambda b, hp, i: (b * n_q + i, 0))
    w_specs = [rows_spec(w) for w in later_weights]
    outs = pl.pallas_call(
        _attn_kernel,
        grid=(batch, ATTN_HEADS // hps, n_q),
        in_specs=[pl.BlockSpec((TQ, hps * HEAD_PAD), lambda b, hp, i: (b * n_q + i, hp)),
                  pl.BlockSpec((seq, hps * HEAD_PAD), lambda b, hp, i: (b, hp)),
                  pl.BlockSpec((seq, hps * HEAD_PAD), lambda b, hp, i: (b, hp))] + w_specs,
        out_specs=[pl.BlockSpec((TQ, hps * V_HEAD_DIM), lambda b, hp, i: (b * n_q + i, hp))] + w_specs,
        out_shape=[jax.ShapeDtypeStruct((batch * seq, ATTN_WIDTH), F32)]
                  + [jax.ShapeDtypeStruct(w.shape, BF16) for w in later_weights],
        compiler_params=pltpu.CompilerParams(
            dimension_semantics=("arbitrary", "arbitrary", "arbitrary"),
            vmem_limit_bytes=VMEM_LIMIT),
        name="attention",
    )(q, k, v, *later_weights)
    return outs[0], outs[1:]


def _split3(x):
    hi = x.astype(BF16)
    r1 = x - hi.astype(F32)
    mid = r1.astype(BF16)
    lo = (r1 - mid.astype(F32)).astype(BF16)
    return hi, mid, lo


def _ssd_kernel(x_ref, b_ref, c_ref, dt_ref, sh_ref, place_ref, lhs_ones_ref, rhs_const_ref,
                cwx_ref, cwb_ref, cwc_ref, cbx_ref, cbb_ref, cbc_ref, adt_ref, dskip_ref,
                o_ref, xc_scr, xm_scr, nsf_scr, nsb_scr, dtv_scr, w_scr, dec_scr, lhs_scr, rhs_scr, rpart_scr):
    nc = dt_ref.shape[0]
    seq = x_ref.shape[0]
    L = SSM_CHUNK
    hpg = HEADS_PER_GROUP
    npair = hpg // 2
    P = SSM_HEAD_DIM
    x_cols = slice(0, GROUP_INNER)
    b_cols = slice(GROUP_INNER, GROUP_INNER + SSM_STATE)
    c_cols = slice(GROUP_INNER + SSM_STATE, GROUP_COLS)
    pair_cols = [slice(j * L, (j + 1) * L) for j in range(npair)]

    @pl.when((pl.program_id(0) == 0) & (pl.program_id(1) == 0))
    def _init_rhs():
        for u in range(SSD_UNROLL):
            rhs_scr[u] = rhs_const_ref[...]

    a_neg = -jnp.exp(adt_ref[0]) * LOG2E
    dtv = jax.nn.softplus(dt_ref[...] + adt_ref[1][None])
    dtv_scr[...] = dtv
    da2 = (dtv * a_neg[None]).reshape(nc * DT_ROWS, L)
    ri = lax.broadcasted_iota(jnp.int32, (L, L), 0)
    ci = lax.broadcasted_iota(jnp.int32, (L, L), 1)
    upper = (ri <= ci).astype(BF16)
    lower = (ri >= ci).astype(BF16)
    tri = jnp.concatenate([upper, lower], axis=1)
    cs_fb = sum(_dot(p, tri) for p in _split3(da2))
    rowsel = (ri & hpg) == 0
    cs2 = jnp.where(rowsel, cs_fb[:, 0:L], cs_fb[:, L:2 * L])
    colcs = cs2.T

    col_parts = jnp.concatenate(_split3(colcs), axis=1)
    half_n = place_ref.shape[1] // 2
    for hf in range(2):
        cols = slice(hf * half_n, (hf + 1) * half_n)
        lhs_half = _dot(col_parts, place_ref[:, cols]) + lhs_ones_ref[:, cols]
        for c in range(nc // 2):
            lhs_scr[hf * (nc // 2) + c] = lhs_half[:, c * L:c * L + SEG_K].astype(BF16)
    for i, part in enumerate(_split3(-cs2)):
        rpart_scr[i] = part.astype(F32).reshape(nc, DT_ROWS, L)

    def lane_bcast(col):
        return jnp.broadcast_to(col, (col.shape[0], L))

    tot2 = jnp.where(rowsel, lane_bcast(cs2[:, L - 1:L]), lane_bcast(cs2[:, 0:1]))
    dec_scr[...] = jnp.exp2(tot2).reshape(nc, DT_ROWS, L)
    w_scr[...] = (dtv.reshape(nc * DT_ROWS, L) * jnp.exp2(tot2 - cs2)).reshape(nc, DT_ROWS, L)

    lane_t = lax.broadcasted_iota(jnp.int32, (L, L), 1)
    low_half = lane_t < P

    cw_halves = (cwx_ref[...], jnp.concatenate([cwb_ref[...], cwc_ref[...]], axis=1))
    cb_halves = (cbx_ref[...], jnp.concatenate([cbb_ref[...], cbc_ref[...]], axis=1))

    def rows(hf, start, size):
        if hf == 0:
            return x_ref[pl.ds(start, size), :]
        return jnp.concatenate([b_ref[pl.ds(start, size), :], c_ref[pl.ds(start, size), :]], axis=1)

    def conv_stage(c):
        base = pl.multiple_of(c * L, L)
        ws = pl.multiple_of(jnp.clip(c * L - CONV_WIN_LEAD, 0, seq - CONV_WIN), CONV_WIN_LEAD)
        variant = jnp.where(c == 0, 0, jnp.where(c == nc - 1, 2, 1))
        sh = sh_ref[variant]
        halves = []
        for hf in range(2):
            cw = cw_halves[hf]
            shifted = _dot(sh, rows(hf, ws, CONV_WIN))
            acc = cb_halves[hf] + cw[SSM_CONV // 2:SSM_CONV // 2 + 1, :] * rows(hf, base, L).astype(F32)
            for t, kk in enumerate(CONV_SHIFTED_TAPS):
                acc = acc + cw[kk:kk + 1, :] * shifted[t * L:(t + 1) * L, :]
            halves.append(acc * jax.nn.sigmoid(acc))
        xc = jnp.concatenate(halves, axis=1)
        xc_scr[c] = xc
        for j in range(npair):
            xp = xc[:, pair_cols[j]]
            xm_scr[c, j] = jnp.concatenate([jnp.where(low_half, xp, 0.0), jnp.where(low_half, 0.0, xp)],
                                           axis=0).astype(BF16)

    def state_loads(c):
        return xc_scr[c, :, b_cols], [xm_scr[c, j] for j in range(npair)]

    def state_stage(c, loaded):
        b_tok, xm = loaded
        bt = b_tok.T
        for d, ns_scr in ((0, nsf_scr), (1, nsb_scr)):
            w = w_scr[c, d * hpg:(d + 1) * hpg, :]
            ns_scr[c] = jnp.concatenate(
                [_dot(jnp.concatenate([(bt * w[2 * j:2 * j + 1, :]).astype(BF16),
                                       (bt * w[2 * j + 1:2 * j + 2, :]).astype(BF16)], axis=1), xm[j])
                 for j in range(npair)], axis=1)

    for c in range(CONV_AHEAD):
        conv_stage(c)

    def pass_a(c, carry):
        loaded = state_loads(c)
        conv_stage(c + CONV_AHEAD)
        state_stage(c, loaded)
        return carry
    lax.fori_loop(0, nc - CONV_AHEAD, pass_a, 0, unroll=PASS_A_UNROLL)
    for c in range(nc - CONV_AHEAD, nc):
        state_stage(c, state_loads(c))

    lane_r = lax.broadcasted_iota(jnp.int32, (1, L), 1)

    def decay_row(c, d):
        dec = dec_scr[c, d * hpg:(d + 1) * hpg, :]
        return jnp.concatenate([jnp.where(lane_r < P, dec[2 * j:2 * j + 1, :], dec[2 * j + 1:2 * j + 2, :])
                                for j in range(npair)], axis=1)

    def pass_b(d, ns_scr):
        def body(i, st):
            c = i if d == 0 else nc - 1 - i
            new = ns_scr[c]
            ns_scr[c] = st
            return st * decay_row(c, d) + new
        lax.fori_loop(0, nc, body, jnp.zeros((SSM_STATE, GROUP_INNER), F32), unroll=True)
    pass_b(0, nsf_scr)
    pass_b(1, nsb_scr)

    mask_f = ci <= ri
    mask_b = ci >= ri
    neg_inf = jnp.float32(-jnp.inf)
    sub16 = lax.broadcasted_iota(jnp.int32, (BF16_ROWS, L), 0)

    def pass_c(c, carry):
        base = pl.multiple_of(c * L, L)
        xc = xc_scr[c]
        bm = xc[:, b_cols].astype(BF16)
        cm = xc[:, c_cols].astype(BF16)
        cb = _dot_nt(cm, bm)
        off_f = _dot(cm, nsf_scr[c].astype(BF16))
        off_b = _dot(cm, nsb_scr[c].astype(BF16))
        slot = c % SSD_UNROLL
        for hd in range(DT_ROWS):
            r0 = (hd % 2) * SEG_K_PER_ROW
            blk = jnp.where((sub16 >= r0) & (sub16 < r0 + 3), 1.0, 0.0)
            for i in range(3):
                blk = jnp.where(sub16 == r0 + 3 + i, rpart_scr[i, c, hd:hd + 1, :], blk)
            rhs_scr[slot, (hd // 2) * BF16_ROWS:(hd // 2 + 1) * BF16_ROWS, hd * L:(hd + 1) * L] = blk.astype(BF16)
        seg = _dot(lhs_scr[c], rhs_scr[slot])
        dt_f_r = dtv_scr[c, 0:hpg, :]
        dt_b_r = dtv_scr[c, hpg:DT_ROWS, :]
        ys = []
        for j in range(npair):
            ws_ = []
            for h in (2 * j, 2 * j + 1):
                ef = jnp.exp2(jnp.where(mask_f, seg[:, h * L:(h + 1) * L], neg_inf))
                eb = jnp.exp2(jnp.where(mask_b, seg[:, (hpg + h) * L:(hpg + h + 1) * L], neg_inf))
                ws_.append((cb * (ef * dt_f_r[h:h + 1, :] + eb * dt_b_r[h:h + 1, :])).astype(BF16))
            scale_f = jnp.exp2(seg[:, (DT_ROWS + j) * L:(DT_ROWS + j + 1) * L])
            scale_b = jnp.exp2(seg[:, (DT_ROWS + npair + j) * L:(DT_ROWS + npair + j + 1) * L])
            ys.append(_dot(jnp.concatenate(ws_, axis=1), xm_scr[c, j])
                      + scale_f * off_f[:, pair_cols[j]] + scale_b * off_b[:, pair_cols[j]])
        y = jnp.concatenate(ys, axis=1) + dskip_ref[...] * xc[:, x_cols]
        o_ref[pl.ds(base, L), :] = y
        return carry
    lax.fori_loop(0, nc, pass_c, 0, unroll=SSD_UNROLL)


def _conv_shift_matrices():
    out = np.zeros((3, len(CONV_SHIFTED_TAPS) * SSM_CHUNK, CONV_WIN), np.float32)
    for v, lead in enumerate((0, CONV_WIN_LEAD, CONV_WIN - SSM_CHUNK)):
        for i, kk in enumerate(CONV_SHIFTED_TAPS):
            for t in range(SSM_CHUNK):
                j = t + kk - SSM_CONV // 2 + lead
                if 0 <= j < CONV_WIN:
                    out[v, i * SSM_CHUNK + t, j] = 1.0
    return out


def _seg_matmul_constants(nc):
    L, K, half = SSM_CHUNK, SEG_K_PER_ROW, SSM_HEAD_DIM
    place = np.zeros((3 * L, nc * L), np.float32)
    lhs_ones = np.zeros((1, nc * L), np.float32)
    for c in range(nc):
        for hd in range(DT_ROWS):
            for i in range(3):
                place[i * L + c * DT_ROWS + hd, c * L + hd * K + i] = 1.0
                lhs_ones[0, c * L + hd * K + 3 + i] = 1.0
    rhs = np.zeros((SEG_K, SEG_BLOCKS * L), np.float32)
    for hd in range(DT_ROWS):
        d, h = divmod(hd, HEADS_PER_GROUP)
        pair_block = DT_ROWS + d * (HEADS_PER_GROUP // 2) + h // 2
        lanes = slice(0, half) if h % 2 == 0 else slice(half, L)
        rhs[hd * K:hd * K + 3, hd * L:(hd + 1) * L] = 1.0
        rhs[hd * K:hd * K + 3, pair_block * L:(pair_block + 1) * L][:, lanes] = 1.0
    return place, lhs_ones, rhs


def _ssd_call(xbc, dtc, cw, cbias, adt, dskip, batch, seq):
    nc = seq // SSM_CHUNK
    L = SSM_CHUNK
    assert nc >= 3 and seq >= CONV_WIN and nc * DT_ROWS == L
    sh = jnp.asarray(_conv_shift_matrices(), BF16)
    place, lhs_ones, rhs_const = _seg_matmul_constants(nc)
    place = jnp.asarray(place, BF16)
    lhs_ones = jnp.asarray(lhs_ones, F32)
    rhs_const = jnp.asarray(rhs_const, BF16)
    const = lambda a: pl.BlockSpec(a.shape, lambda b, g: (0,) * a.ndim)
    b_blk0 = SSM_INNER // SSM_STATE
    c_blk0 = b_blk0 + SSM_GROUPS

    def xbc_views(rows_, batched):
        lead = (lambda b: b) if batched else (lambda b: 0)
        return [pl.BlockSpec((rows_, GROUP_INNER), lambda b, g: (lead(b), g)),
                pl.BlockSpec((rows_, SSM_STATE), lambda b, g: (lead(b), b_blk0 + g)),
                pl.BlockSpec((rows_, SSM_STATE), lambda b, g: (lead(b), c_blk0 + g))]
    return pl.pallas_call(
        _ssd_kernel,
        grid=(batch, SSM_GROUPS),
        in_specs=xbc_views(seq, True) + [
                  pl.BlockSpec((nc, DT_ROWS, L), lambda b, g: (b, g, 0)),
                  const(sh), const(place), const(lhs_ones), const(rhs_const)]
                 + xbc_views(SUBLANES, False) + xbc_views(1, False) + [
                  pl.BlockSpec((2, DT_ROWS, L), lambda b, g: (0, g, 0)),
                  pl.BlockSpec((1, GROUP_INNER), lambda b, g: (0, g))],
        out_specs=pl.BlockSpec((seq, GROUP_INNER), lambda b, g: (b, g)),
        out_shape=jax.ShapeDtypeStruct((batch * seq, SSM_INNER), F32),
        scratch_shapes=[pltpu.VMEM((nc, L, GROUP_COLS), F32),
                        pltpu.VMEM((nc, HEADS_PER_GROUP // 2, 2 * L, L), BF16),
                        pltpu.VMEM((nc, SSM_STATE, GROUP_INNER), F32),
                        pltpu.VMEM((nc, SSM_STATE, GROUP_INNER), F32),
                        pltpu.VMEM((nc, DT_ROWS, L), F32),
                        pltpu.VMEM((nc, DT_ROWS, L), F32),
                        pltpu.VMEM((nc, DT_ROWS, L), F32),
                        pltpu.VMEM((nc, L, SEG_K), BF16),
                        pltpu.VMEM((SSD_UNROLL, SEG_K, SEG_BLOCKS * L), BF16),
                        pltpu.VMEM((3, nc, DT_ROWS, L), F32)],
        compiler_params=pltpu.CompilerParams(dimension_semantics=("arbitrary", "arbitrary"),
                                             vmem_limit_bytes=VMEM_LIMIT),
        name="ssd",
    )(xbc, xbc, xbc, dtc, sh, place, lhs_ones, rhs_const, cw, cw, cw, cbias, cbias, cbias,
      adt, dskip)


def _mlp_kernel(x_ref, attn_ref, ssm_ref, z_ref, gat_ref, gn_ref, wo_ref, gm_ref, wup_ref, wdn_ref, o_ref):
    for sub in range(x_ref.shape[0] // MLP_SUBTILE):
        rs = slice(sub * MLP_SUBTILE, (sub + 1) * MLP_SUBTILE)
        a = attn_ref[rs, :]
        an = (a * _inv_rms(a) * gat_ref[...]).astype(BF16)
        zc = z_ref[rs, :].astype(F32)
        y = ssm_ref[rs, :] * (zc * jax.nn.sigmoid(zc))
        yn = jnp.concatenate([y[:, g * GROUP_INNER:(g + 1) * GROUP_INNER]
                              * _inv_rms(y[:, g * GROUP_INNER:(g + 1) * GROUP_INNER]) for g in range(SSM_GROUPS)],
                             axis=1) * gn_ref[...]
        mix = _dot(an, wo_ref[0:ATTN_WIDTH, :]) + _dot(yn.astype(BF16), wo_ref[ATTN_WIDTH:D_MIX, :])
        x1 = x_ref[rs, :] + mix
        hm = (x1 * _inv_rms(x1) * gm_ref[...]).astype(BF16)
        acc = jnp.zeros_like(x1)
        for c in range(D_FF // FF_CHUNK):
            cols = slice(c * FF_CHUNK, (c + 1) * FF_CHUNK)
            u = _dot(hm, wup_ref[:, cols])
            acc = acc + _dot(jnp.square(jnp.maximum(u, 0.0)).astype(BF16), wdn_ref[cols, :])
        o_ref[rs, :] = x1 + acc


def _mlp_call(x2, attn, ssm, z, gat, gn, wo, gm, wup, wdn):
    t = x2.shape[0]
    row = lambda w: pl.BlockSpec((TM_MLP, w), lambda i: (i, 0))
    full = lambda a: pl.BlockSpec(a.shape, lambda i: (0, 0), pipeline_mode=pl.Buffered(1))
    return pl.pallas_call(
        _mlp_kernel,
        grid=(t // TM_MLP,),
        in_specs=[row(D_MODEL), row(ATTN_WIDTH), row(SSM_INNER), row(SSM_INNER), full(gat), full(gn), full(wo),
                  full(gm), full(wup), full(wdn)],
        out_specs=row(D_MODEL),
        out_shape=jax.ShapeDtypeStruct((t, D_MODEL), F32),
        compiler_params=pltpu.CompilerParams(dimension_semantics=("arbitrary",),
                                             vmem_limit_bytes=VMEM_LIMIT),
        name="outproj_mlp",
    )(x2, attn, ssm, z, gat, gn, wo, gm, wup, wdn)


def _swap_halves(g):
    half = QK_ROPE_DIM // 2
    return jnp.concatenate([g[..., half:], g[..., :half]], axis=-1)


def _lane_bcast(v):
    return jnp.broadcast_to(v[..., None], v.shape + (LANES,)).astype(F32)


def _layer(x2, cs, batch, seq, ln_mix_g, stacked_proj_weights, q_a_norm_g, kv_a_norm_g, q_norm_g,
           k_norm_g, attn_out_norm_g, conv_w, conv_b, a_log_fwd, a_log_bwd, dt_bias_fwd, dt_bias_bwd,
           d_skip, ssm_norm_g, w_out, ln_mlp_g, w_mlp_up, w_mlp_down):
    win, wuq, wuk, wuv = _win_prep_call(*stacked_proj_weights)
    ones_v = np.ones((ATTN_HEADS // 2, V_HEAD_DIM), np.float32)
    vone = jnp.asarray(np.stack([np.concatenate([0 * ones_v, ones_v], axis=-1),
                                 np.concatenate([ones_v, 0 * ones_v], axis=-1)], axis=1).reshape(1, -1))
    scale = QK_HEAD_DIM ** -0.5 * np.log2(np.e)
    zeros_nope = jnp.zeros((QK_NOPE_DIM,), k_norm_g.dtype)
    k_rope_g = k_norm_g[QK_NOPE_DIM:]
    hg = jnp.concatenate([q_norm_g * scale, _swap_halves(q_norm_g[QK_NOPE_DIM:]) * scale,
                          k_norm_g[:QK_NOPE_DIM], zeros_nope,
                          zeros_nope, k_rope_g, _swap_halves(k_rope_g),
                          jnp.zeros(((SUBLANES - 3) * HEAD_PAD,), k_norm_g.dtype)]).reshape(SUBLANES, HEAD_PAD)

    q, k, v, z, xbc, dtc = _inproj_call(
        x2, cs, ln_mix_g[None, :], win, q_a_norm_g[None, :], wuq, kv_a_norm_g[None, :], wuk, wuv, vone,
        hg)

    attn, (wo_b, wup_b, wdn_b) = _attn_call(q, k, v, (w_out, w_mlp_up, w_mlp_down), batch, seq)

    cw = jnp.pad(conv_w[:, 0, :], ((0, SUBLANES - SSM_CONV), (0, 0)))
    cbias = conv_b[None, :]
    adt = jnp.stack([a_log_fwd, a_log_bwd, dt_bias_fwd, dt_bias_bwd]).reshape(2, 2, SSM_GROUPS, HEADS_PER_GROUP)
    adt = _lane_bcast(adt.transpose(0, 2, 1, 3).reshape(2, 2 * SSM_HEADS))
    dskip = jnp.repeat(d_skip, SSM_HEAD_DIM)[None, :]
    ssm = _ssd_call(xbc, dtc, cw, cbias, adt, dskip, batch, seq)

    return _mlp_call(x2, attn, ssm, z, attn_out_norm_g[None, :], ssm_norm_g[None, :], wo_b,
                     ln_mlp_g[None, :], wup_b, wdn_b)


def _rope_table(positions):
    inv_freq = 1.0 / (ROPE_THETA ** (jnp.arange(0, QK_ROPE_DIM, 2, dtype=F32) / QK_ROPE_DIM))
    ang = inv_freq[:, None] * positions.astype(F32).reshape(1, -1)
    return jnp.concatenate([jnp.cos(ang), jnp.sin(ang)], axis=0)


def kernel(x, positions, ln_mix_g, w_in, q_a_norm_g, w_uq, kv_a_norm_g, w_ukv, q_norm_g, k_norm_g,
           attn_out_norm_g, conv_w, conv_b, a_log_fwd, a_log_bwd, dt_bias_fwd, dt_bias_bwd, d_skip,
           ssm_norm_g, w_out, ln_mlp_g, w_mlp_up, w_mlp_down):
    batch, seq, d = x.shape
    assert d == D_MODEL and seq % TQ == 0 and (batch * seq) % TM_IN == 0 and (batch * seq) % TM_MLP == 0
    cs = _rope_table(positions)
    x2 = x.reshape(batch * seq, d)
    for l in range(ln_mix_g.shape[0]):
        x2 = _layer(x2, cs, batch, seq, ln_mix_g[l], (w_in, w_uq, w_ukv, l), q_a_norm_g[l], kv_a_norm_g[l],
                    q_norm_g[l], k_norm_g[l], attn_out_norm_g[l], conv_w[l], conv_b[l],
                    a_log_fwd[l], a_log_bwd[l], dt_bias_fwd[l], dt_bias_bwd[l], d_skip[l], ssm_norm_g[l],
                    w_out[l], ln_mlp_g[l], w_mlp_up[l], w_mlp_down[l])
    return x2.reshape(batch, seq, d)
```

```python
import numpy as np
import jax
import jax.numpy as jnp
from jax import lax
from jax.experimental import pallas as pl
from jax.experimental.pallas import tpu as pltpu

F32 = jnp.float32
BF16 = jnp.bfloat16

D_MODEL = 1024
ATTN_HEADS = 8
QK_NOPE_DIM = 64
QK_ROPE_DIM = 32
QK_HEAD_DIM = QK_NOPE_DIM + QK_ROPE_DIM
V_HEAD_DIM = 64
Q_LORA_RANK = D_MODEL // 4
KV_LORA_RANK = D_MODEL // 8
ROPE_THETA = 10000.0
ATTN_WIDTH = ATTN_HEADS * V_HEAD_DIM
SSM_HEADS = 8
SSM_HEAD_DIM = 64
SSM_INNER = SSM_HEADS * SSM_HEAD_DIM
SSM_GROUPS = 2
SSM_STATE = 128
SSM_CONV = 5
SSM_CHUNK = 128
SSM_CONV_CH = SSM_INNER + 2 * SSM_GROUPS * SSM_STATE
D_MIX = ATTN_WIDTH + SSM_INNER
D_FF = 4 * D_MODEL
EPS = 1e-6

LANES = 128
SUBLANES = 8
HEAD_PAD = LANES

HEADS_PER_GROUP = SSM_HEADS // SSM_GROUPS
GROUP_INNER = SSM_INNER // SSM_GROUPS
GROUP_COLS = GROUP_INNER + 2 * SSM_STATE
DT_ROWS = 2 * HEADS_PER_GROUP
BF16_ROWS = 16
SEG_K_PER_ROW = 8
SEG_K = DT_ROWS * SEG_K_PER_ROW
SEG_BLOCKS = DT_ROWS + HEADS_PER_GROUP
LOG2E = float(np.log2(np.e))
CONV_WIN = 2 * SSM_CHUNK
CONV_WIN_LEAD = SSM_CHUNK // 2
CONV_SHIFTED_TAPS = tuple(k for k in range(SSM_CONV) if k != SSM_CONV // 2)

COL_CKV = Q_LORA_RANK
COL_MISC = COL_CKV + KV_LORA_RANK
COL_Z = COL_MISC + LANES
COL_XBC = COL_Z + SSM_INNER
IN_COLS = COL_XBC + SSM_CONV_CH

TM_IN = 1024
IN_SUBTILE = 512
TQ = 1024
ATTN_HEADS_PER_STEP = 8
TM_MLP = 1024
MLP_SUBTILE = 512
FF_CHUNK = 1024
SSD_UNROLL = 16
CONV_AHEAD = 4
PASS_A_UNROLL = 12
WIN_PREP_STEPS = 4
VMEM_LIMIT = 56 * 1024 * 1024


def _inv_rms(x):
    n = x.shape[-1]
    x2 = x * x
    acc = x2[:, 0:LANES]
    for i in range(1, n // LANES):
        acc = acc + x2[:, i * LANES:(i + 1) * LANES]
    return lax.rsqrt(jnp.sum(acc, axis=-1, keepdims=True) * (1.0 / n) + EPS)


def _dot(a, b):
    return jnp.dot(a, b, preferred_element_type=F32)


def _dot_nt(a, b):
    return lax.dot_general(a, b, (((1,), (1,)), ((), ())), preferred_element_type=F32)


def _head_gains(hg_ref):
    half = QK_ROPE_DIM // 2
    g = hg_ref[...]
    lane = lax.broadcasted_iota(jnp.int32, g.shape, 1)
    ext = jnp.where(lane < QK_HEAD_DIM, g,
                    jnp.where(lane < QK_HEAD_DIM + half, pltpu.roll(g, half, 1),
                              pltpu.roll(g, QK_ROPE_DIM + half, 1)))
    is_nope = lane[0:1, :] < QK_NOPE_DIM
    gq = ext[0:1, :] * (QK_HEAD_DIM ** -0.5 * LOG2E)
    return gq, jnp.where(is_nope, g[1:2, :], 0.0), jnp.where(is_nope, 0.0, ext[1:2, :])


def _inproj_kernel(x_ref, cos_ref, sin_ref, g_ref, win_ref, gqa_ref, wuq_ref, gkva_ref, wuk_ref, wuv_ref,
                   vone_ref, hg_ref,
                   q_ref, k_ref, v_ref, z_ref, xbc_ref, dt_ref):
    gains = _head_gains(hg_ref)
    for sub in range(x_ref.shape[0] // IN_SUBTILE):
        _inproj_rows(sub * IN_SUBTILE, x_ref, cos_ref, sin_ref, g_ref, win_ref, gqa_ref, wuq_ref, gkva_ref,
                     wuk_ref, wuv_ref, vone_ref, gains, q_ref, k_ref, v_ref, z_ref, xbc_ref, dt_ref)


def _inproj_rows(r0, x_ref, cos_ref, sin_ref, g_ref, win_ref, gqa_ref, wuq_ref, gkva_ref, wuk_ref, wuv_ref,
                 vone_ref, gains, q_ref, k_ref, v_ref, z_ref, xbc_ref, dt_ref):
    gq, gk_nope, gk_rope = gains
    tm = IN_SUBTILE
    rs = slice(r0, r0 + tm)
    x = x_ref[rs, :]
    h = (x * _inv_rms(x) * g_ref[...]).astype(BF16)
    big = _dot(h, win_ref[...])
    z_ref[rs, :] = big[:, COL_Z:COL_XBC].astype(BF16)
    xbc_ref[rs, :] = big[:, COL_XBC:IN_COLS].astype(BF16)
    misc = big[:, COL_MISC:COL_Z]
    for c in range(tm // SSM_CHUNK):
        dt_ref[r0 // SSM_CHUNK + c] = misc[c * SSM_CHUNK:(c + 1) * SSM_CHUNK, :].T[0:2 * SSM_HEADS, :]

    cq = big[:, 0:COL_CKV]
    ckv = big[:, COL_CKV:COL_MISC]
    cqn = (cq * _inv_rms(cq) * gqa_ref[...]).astype(BF16)
    ckvn = (ckv * _inv_rms(ckv) * gkva_ref[...]).astype(BF16)
    q_pre = _dot(cqn, wuq_ref[...])
    k_pre = _dot(ckvn, wuk_ref[...])
    v_ref[rs, :] = (_dot(ckvn, wuv_ref[...]) + vone_ref[...]).astype(BF16)

    half = QK_ROPE_DIM // 2
    zpad = jnp.zeros((HEAD_PAD - QK_ROPE_DIM, SSM_CHUNK), F32)
    tt = jnp.concatenate(
        [jnp.concatenate([cos_ref[:, r0 + c * SSM_CHUNK:r0 + (c + 1) * SSM_CHUNK],
                          sin_ref[:, r0 + c * SSM_CHUNK:r0 + (c + 1) * SSM_CHUNK], zpad], axis=0).T
         for c in range(tm // SSM_CHUNK)], axis=0)
    lane_t = lax.broadcasted_iota(jnp.int32, (tm, HEAD_PAD), 1)
    cs = jnp.where(lane_t < QK_NOPE_DIM, 1.0,
                   jnp.where(lane_t < QK_NOPE_DIM + half, pltpu.roll(tt, QK_NOPE_DIM, 1),
                             jnp.where(lane_t < QK_HEAD_DIM + half, pltpu.roll(tt, QK_NOPE_DIM + half, 1),
                                       pltpu.roll(tt, QK_HEAD_DIM, 1))))
    lane = lax.broadcasted_iota(jnp.int32, (1, HEAD_PAD), 1)
    in_head = (lane < QK_HEAD_DIM).astype(F32)
    is_rope = ((lane >= QK_NOPE_DIM) & (lane < QK_HEAD_DIM)).astype(F32)
    inv_d = 1.0 / QK_HEAD_DIM

    gcq = gq * cs
    for hh in range(ATTN_HEADS):
        sl = slice(hh * HEAD_PAD, (hh + 1) * HEAD_PAD)
        qh = q_pre[:, sl]
        ssq = jnp.sum(qh * qh * in_head, axis=-1, keepdims=True)
        q_ref[rs, sl] = (qh * lax.rsqrt(ssq * inv_d + EPS) * gcq).astype(BF16)

    ab = misc * (gk_rope * cs)
    lane2 = lax.broadcasted_iota(jnp.int32, (tm, HEAD_PAD), 1)
    swapped = jnp.where(lane2 < QK_HEAD_DIM, pltpu.roll(ab, HEAD_PAD - QK_ROPE_DIM, 1),
                        pltpu.roll(ab, QK_ROPE_DIM, 1))
    s_both = jnp.where(lane2 >= QK_NOPE_DIM, ab + swapped, 0.0)
    ssq_pe = jnp.sum(misc * misc * is_rope, axis=-1, keepdims=True)
    gkn = gk_nope
    for hh in range(ATTN_HEADS):
        sl = slice(hh * HEAD_PAD, (hh + 1) * HEAD_PAD)
        kh = k_pre[:, sl]
        ssq = jnp.sum(kh * kh, axis=-1, keepdims=True) + ssq_pe
        k_ref[rs, sl] = ((kh * gkn + s_both) * lax.rsqrt(ssq * inv_d + EPS)).astype(BF16)


def _win_prep_kernel(wt_ref, wq_ref, wkv_ref, o_ref, wuq_ref, wuk_ref, wuv_ref):
    half = QK_ROPE_DIM // 2

    @pl.when(pl.program_id(0) == 0)
    def _up_projection_tiles():
        wq = wq_ref[...]
        q_pieces = []
        for hh in range(ATTN_HEADS):
            head = wq[:, hh * QK_HEAD_DIM:(hh + 1) * QK_HEAD_DIM]
            rope = head[:, QK_NOPE_DIM:]
            q_pieces += [head, -rope[:, half:], rope[:, :half]]
        wuq_ref[...] = jnp.concatenate(q_pieces, axis=1).astype(BF16)
        wkv = wkv_ref[...]
        zeros = jnp.zeros((wkv.shape[0], V_HEAD_DIM), F32)
        k_pieces, v_pieces = [], []
        for hh in range(ATTN_HEADS):
            nope = wkv[:, hh * HEAD_PAD:hh * HEAD_PAD + QK_NOPE_DIM]
            val = wkv[:, hh * HEAD_PAD + QK_NOPE_DIM:(hh + 1) * HEAD_PAD]
            k_pieces += [nope, zeros]
            v_pieces += [val, zeros] if hh % 2 == 0 else [zeros, val]
        wuk_ref[...] = jnp.concatenate(k_pieces, axis=1).astype(BF16)
        wuv_ref[...] = jnp.concatenate(v_pieces, axis=1).astype(BF16)

    cols = wt_ref.shape[1]
    o_kpe = Q_LORA_RANK + KV_LORA_RANK
    o_z = o_kpe + QK_ROPE_DIM
    o_dt = o_z + SSM_INNER + SSM_CONV_CH
    hpg = HEADS_PER_GROUP
    assert SSM_HEADS == SUBLANES and SSM_GROUPS == 2
    dt_f = wt_ref[o_dt:o_dt + SSM_HEADS, :]
    dt_b = wt_ref[o_dt + SSM_HEADS:o_dt + 2 * SSM_HEADS, :]
    top = lax.broadcasted_iota(jnp.int32, (SUBLANES, cols), 0) < hpg
    dt_g0 = jnp.where(top, dt_f, pltpu.roll(dt_b, hpg, 0))
    dt_g1 = jnp.where(top, pltpu.roll(dt_f, hpg, 0), dt_b)
    kpe = wt_ref[o_kpe:o_z, :]
    rows = jnp.concatenate(
        [wt_ref[0:o_kpe, :], dt_g0, dt_g1, jnp.zeros((QK_NOPE_DIM - 2 * SSM_HEADS, cols), F32),
         kpe, -kpe[half:, :], kpe[:half, :], wt_ref[o_z:o_dt, :]], axis=0)
    for ct in range(cols // LANES):
        o_ref[ct * LANES:(ct + 1) * LANES, :] = jnp.concatenate(
            [rows[rt * LANES:(rt + 1) * LANES, ct * LANES:(ct + 1) * LANES].T for rt in range(IN_COLS // LANES)],
            axis=1).astype(BF16)


def _win_prep_call(w_in3, w_uq3, w_ukv3, layer):
    depth, d, n = w_in3.shape
    cols = d // WIN_PREP_STEPS
    whole = lambda a: pl.BlockSpec(a.shape[1:], lambda i: (layer, 0))
    head_cols = ATTN_HEADS * HEAD_PAD
    return pl.pallas_call(
        _win_prep_kernel,
        grid=(WIN_PREP_STEPS,),
        in_specs=[pl.BlockSpec((n, cols), lambda i: (layer, i)), whole(w_uq3), whole(w_ukv3)],
        out_specs=[pl.BlockSpec((cols, IN_COLS), lambda i: (i, 0)),
                   pl.BlockSpec((Q_LORA_RANK, head_cols), lambda i: (0, 0)),
                   pl.BlockSpec((KV_LORA_RANK, head_cols), lambda i: (0, 0)),
                   pl.BlockSpec((KV_LORA_RANK, head_cols), lambda i: (0, 0))],
        out_shape=[jax.ShapeDtypeStruct((d, IN_COLS), BF16),
                   jax.ShapeDtypeStruct((Q_LORA_RANK, head_cols), BF16),
                   jax.ShapeDtypeStruct((KV_LORA_RANK, head_cols), BF16),
                   jax.ShapeDtypeStruct((KV_LORA_RANK, head_cols), BF16)],
        compiler_params=pltpu.CompilerParams(dimension_semantics=("arbitrary",), vmem_limit_bytes=VMEM_LIMIT),
        name="win_prep",
    )(jnp.transpose(w_in3, (0, 2, 1)).reshape(depth * n, d),
      w_uq3.reshape(depth * w_uq3.shape[1], w_uq3.shape[2]),
      w_ukv3.reshape(depth * w_ukv3.shape[1], w_ukv3.shape[2]))


def _inproj_call(x2, cos, sin, g, win, gqa, wuq, gkva, wuk, wuv, vone, hg):
    t = x2.shape[0]
    cpt = TM_IN // SSM_CHUNK
    full = lambda a: pl.BlockSpec(a.shape, lambda i: (0,) * a.ndim, pipeline_mode=pl.Buffered(1))
    row = lambda w: pl.BlockSpec((TM_IN, w), lambda i: (i, 0))
    rope = pl.BlockSpec((QK_ROPE_DIM // 2, TM_IN), lambda i: (0, i))
    return pl.pallas_call(
        _inproj_kernel,
        grid=(t // TM_IN,),
        in_specs=[row(D_MODEL), rope, rope, full(g), full(win), full(gqa), full(wuq), full(gkva),
                  full(wuk), full(wuv), full(vone), full(hg)],
        out_specs=[row(ATTN_HEADS * HEAD_PAD), row(ATTN_HEADS * HEAD_PAD), row(ATTN_HEADS * HEAD_PAD),
                   row(SSM_INNER), row(SSM_CONV_CH),
                   pl.BlockSpec((cpt, 2 * SSM_HEADS, SSM_CHUNK), lambda i: (i, 0, 0))],
        out_shape=[jax.ShapeDtypeStruct((t, ATTN_HEADS * HEAD_PAD), BF16),
                   jax.ShapeDtypeStruct((t, ATTN_HEADS * HEAD_PAD), BF16),
                   jax.ShapeDtypeStruct((t, ATTN_HEADS * HEAD_PAD), BF16),
                   jax.ShapeDtypeStruct((t, SSM_INNER), BF16),
                   jax.ShapeDtypeStruct((t, SSM_CONV_CH), BF16),
                   jax.ShapeDtypeStruct((t // SSM_CHUNK, 2 * SSM_HEADS, SSM_CHUNK), F32)],
        compiler_params=pltpu.CompilerParams(dimension_semantics=("arbitrary",),
                                             vmem_limit_bytes=VMEM_LIMIT),
        name="inproj",
    )(x2, cos, sin, g, win, gqa, wuq, gkva, wuk, wuv, vone, hg)


def _attn_kernel(q_ref, k_ref, v_ref, *refs):
    n_w = (len(refs) - 1) // 2
    w_in_refs, o_ref, w_out_refs = refs[:n_w], refs[n_w], refs[n_w + 1:]
    for w_in_ref, w_out_ref in zip(w_in_refs, w_out_refs):
        w_out_ref[...] = w_in_ref[...].astype(BF16)

    lane = lax.broadcasted_iota(jnp.int32, (q_ref.shape[0], HEAD_PAD), 1)
    for jp in range(ATTN_HEADS_PER_STEP // 2):
        sls = [slice(j * HEAD_PAD, (j + 1) * HEAD_PAD) for j in (2 * jp, 2 * jp + 1)]
        ss = [_dot_nt(q_ref[:, sl], k_ref[:, sl]) for sl in sls]
        ps = [jnp.exp2(s - jnp.max(s, axis=-1, keepdims=True)).astype(BF16) for s in ss]
        accs = [_dot(p, v_ref[:, sl]) for p, sl in zip(ps, sls)]
        res = [acc / pltpu.roll(acc, V_HEAD_DIM, 1) for acc in accs]
        o_ref[:, jp * HEAD_PAD:(jp + 1) * HEAD_PAD] = jnp.where(lane < V_HEAD_DIM, res[0], res[1])


def _attn_call(q, k, v, later_weights, batch, seq):
    n_q = seq // TQ
    hps = ATTN_HEADS_PER_STEP
    assert ATTN_HEADS == hps
    n_steps = batch * n_q

    def rows_spec(w):
        assert w.shape[0] % (n_steps * BF16_ROWS) == 0
        return pl.BlockSpec((w.shape[0] // n_steps, w.shape[1]), lambda b, hp, i: (b * n_q + i, 0))
    w_specs = [rows_spec(w) for w in later_weights]
    outs = pl.pallas_call(
        _attn_kernel,
        grid=(batch, ATTN_HEADS // hps, n_q),
        in_specs=[pl.BlockSpec((TQ, hps * HEAD_PAD), lambda b, hp, i: (b * n_q + i, hp)),
                  pl.BlockSpec((seq, hps * HEAD_PAD), lambda b, hp, i: (b, hp)),
                  pl.BlockSpec((seq, hps * HEAD_PAD), lambda b, hp, i: (b, hp))] + w_specs,
        out_specs=[pl.BlockSpec((TQ, hps * V_HEAD_DIM), lambda b, hp, i: (b * n_q + i, hp))] + w_specs,
        out_shape=[jax.ShapeDtypeStruct((batch * seq, ATTN_WIDTH), F32)]
                  + [jax.ShapeDtypeStruct(w.shape, BF16) for w in later_weights],
        compiler_params=pltpu.CompilerParams(
            dimension_semantics=("arbitrary", "arbitrary", "arbitrary"),
            vmem_limit_bytes=VMEM_LIMIT),
        name="attention",
    )(q, k, v, *later_weights)
    return outs[0], outs[1:]


def _split3(x):
    hi = x.astype(BF16)
    r1 = x - hi.astype(F32)
    mid = r1.astype(BF16)
    lo = (r1 - mid.astype(F32)).astype(BF16)
    return hi, mid, lo


def _ssd_kernel(x_ref, b_ref, c_ref, dt_ref, sh_ref, place_ref, lhs_ones_ref, rhs_const_ref,
                cwx_ref, cwb_ref, cwc_ref, cbx_ref, cbb_ref, cbc_ref, adt_ref, dskip_ref,
                o_ref, xc_scr, xm_scr, nsf_scr, nsb_scr, dtv_scr, w_scr, dec_scr, lhs_scr, rhs_scr, rpart_scr):
    nc = dt_ref.shape[0]
    seq = x_ref.shape[0]
    L = SSM_CHUNK
    hpg = HEADS_PER_GROUP
    npair = hpg // 2
    P = SSM_HEAD_DIM
    x_cols = slice(0, GROUP_INNER)
    b_cols = slice(GROUP_INNER, GROUP_INNER + SSM_STATE)
    c_cols = slice(GROUP_INNER + SSM_STATE, GROUP_COLS)
    pair_cols = [slice(j * L, (j + 1) * L) for j in range(npair)]

    @pl.when((pl.program_id(0) == 0) & (pl.program_id(1) == 0))
    def _init_rhs():
        for u in range(SSD_UNROLL):
            rhs_scr[u] = rhs_const_ref[...]

    a_neg = -jnp.exp(adt_ref[0]) * LOG2E
    dtv = jax.nn.softplus(dt_ref[...] + adt_ref[1][None])
    dtv_scr[...] = dtv
    da2 = (dtv * a_neg[None]).reshape(nc * DT_ROWS, L)
    ri = lax.broadcasted_iota(jnp.int32, (L, L), 0)
    ci = lax.broadcasted_iota(jnp.int32, (L, L), 1)
    upper = (ri <= ci).astype(BF16)
    lower = (ri >= ci).astype(BF16)
    tri = jnp.concatenate([upper, lower], axis=1)
    cs_fb = sum(_dot(p, tri) for p in _split3(da2))
    rowsel = (ri & hpg) == 0
    cs2 = jnp.where(rowsel, cs_fb[:, 0:L], cs_fb[:, L:2 * L])
    colcs = cs2.T

    col_parts = jnp.concatenate(_split3(colcs), axis=1)
    half_n = place_ref.shape[1] // 2
    for hf in range(2):
        cols = slice(hf * half_n, (hf + 1) * half_n)
        lhs_half = _dot(col_parts, place_ref[:, cols]) + lhs_ones_ref[:, cols]
        for c in range(nc // 2):
            lhs_scr[hf * (nc // 2) + c] = lhs_half[:, c * L:c * L + SEG_K].astype(BF16)
    for i, part in enumerate(_split3(-cs2)):
        rpart_scr[i] = part.astype(F32).reshape(nc, DT_ROWS, L)

    def lane_bcast(col):
        return jnp.broadcast_to(col, (col.shape[0], L))

    tot2 = jnp.where(rowsel, lane_bcast(cs2[:, L - 1:L]), lane_bcast(cs2[:, 0:1]))
    dec_scr[...] = jnp.exp2(tot2).reshape(nc, DT_ROWS, L)
    w_scr[...] = (dtv.reshape(nc * DT_ROWS, L) * jnp.exp2(tot2 - cs2)).reshape(nc, DT_ROWS, L)

    lane_t = lax.broadcasted_iota(jnp.int32, (L, L), 1)
    low_half = lane_t < P

    cw_halves = (cwx_ref[...], jnp.concatenate([cwb_ref[...], cwc_ref[...]], axis=1))
    cb_halves = (cbx_ref[...], jnp.concatenate([cbb_ref[...], cbc_ref[...]], axis=1))

    def rows(hf, start, size):
        if hf == 0:
            return x_ref[pl.ds(start, size), :]
        return jnp.concatenate([b_ref[pl.ds(start, size), :], c_ref[pl.ds(start, size), :]], axis=1)

    def conv_stage(c):
        base = pl.multiple_of(c * L, L)
        ws = pl.multiple_of(jnp.clip(c * L - CONV_WIN_LEAD, 0, seq - CONV_WIN), CONV_WIN_LEAD)
        variant = jnp.where(c == 0, 0, jnp.where(c == nc - 1, 2, 1))
        sh = sh_ref[variant]
        halves = []
        for hf in range(2):
            cw = cw_halves[hf]
            shifted = _dot(sh, rows(hf, ws, CONV_WIN))
            acc = cb_halves[hf] + cw[SSM_CONV // 2:SSM_CONV // 2 + 1, :] * rows(hf, base, L).astype(F32)
            for t, kk in enumerate(CONV_SHIFTED_TAPS):
                acc = acc + cw[kk:kk + 1, :] * shifted[t * L:(t + 1) * L, :]
            halves.append(acc * jax.nn.sigmoid(acc))
        xc = jnp.concatenate(halves, axis=1)
        xc_scr[c] = xc
        for j in range(npair):
            xp = xc[:, pair_cols[j]]
            xm_scr[c, j] = jnp.concatenate([jnp.where(low_half, xp, 0.0), jnp.where(low_half, 0.0, xp)],
                                           axis=0).astype(BF16)

    def state_loads(c):
        return xc_scr[c, :, b_cols], [xm_scr[c, j] for j in range(npair)]

    def state_stage(c, loaded):
        b_tok, xm = loaded
        bt = b_tok.T
        for d, ns_scr in ((0, nsf_scr), (1, nsb_scr)):
            w = w_scr[c, d * hpg:(d + 1) * hpg, :]
            ns_scr[c] = jnp.concatenate(
                [_dot(jnp.concatenate([(bt * w[2 * j:2 * j + 1, :]).astype(BF16),
                                       (bt * w[2 * j + 1:2 * j + 2, :]).astype(BF16)], axis=1), xm[j])
                 for j in range(npair)], axis=1)

    for c in range(CONV_AHEAD):
        conv_stage(c)

    def pass_a(c, carry):
        loaded = state_loads(c)
        conv_stage(c + CONV_AHEAD)
        state_stage(c, loaded)
        return carry
    lax.fori_loop(0, nc - CONV_AHEAD, pass_a, 0, unroll=PASS_A_UNROLL)
    for c in range(nc - CONV_AHEAD, nc):
        state_stage(c, state_loads(c))

    lane_r = lax.broadcasted_iota(jnp.int32, (1, L), 1)

    def decay_row(c, d):
        dec = dec_scr[c, d * hpg:(d + 1) * hpg, :]
        return jnp.concatenate([jnp.where(lane_r < P, dec[2 * j:2 * j + 1, :], dec[2 * j + 1:2 * j + 2, :])
                                for j in range(npair)], axis=1)

    def pass_b(d, ns_scr):
        def body(i, st):
            c = i if d == 0 else nc - 1 - i
            new = ns_scr[c]
            ns_scr[c] = st
            return st * decay_row(c, d) + new
        lax.fori_loop(0, nc, body, jnp.zeros((SSM_STATE, GROUP_INNER), F32), unroll=True)
    pass_b(0, nsf_scr)
    pass_b(1, nsb_scr)

    mask_f = ci <= ri
    mask_b = ci >= ri
    neg_inf = jnp.float32(-jnp.inf)
    sub16 = lax.broadcasted_iota(jnp.int32, (BF16_ROWS, L), 0)

    def pass_c(c, carry):
        base = pl.multiple_of(c * L, L)
        xc = xc_scr[c]
        bm = xc[:, b_cols].astype(BF16)
        cm = xc[:, c_cols].astype(BF16)
        cb = _dot_nt(cm, bm)
        off_f = _dot(cm, nsf_scr[c].astype(BF16))
        off_b = _dot(cm, nsb_scr[c].astype(BF16))
        slot = c % SSD_UNROLL
        for hd in range(DT_ROWS):
            r0 = (hd % 2) * SEG_K_PER_ROW
            blk = jnp.where((sub16 >= r0) & (sub16 < r0 + 3), 1.0, 0.0)
            for i in range(3):
                blk = jnp.where(sub16 == r0 + 3 + i, rpart_scr[i, c, hd:hd + 1, :], blk)
            rhs_scr[slot, (hd // 2) * BF16_ROWS:(hd // 2 + 1) * BF16_ROWS, hd * L:(hd + 1) * L] = blk.astype(BF16)
        seg = _dot(lhs_scr[c], rhs_scr[slot])
        dt_f_r = dtv_scr[c, 0:hpg, :]
        dt_b_r = dtv_scr[c, hpg:DT_ROWS, :]
        ys = []
        for j in range(npair):
            ws_ = []
            for h in (2 * j, 2 * j + 1):
                ef = jnp.exp2(jnp.where(mask_f, seg[:, h * L:(h + 1) * L], neg_inf))
                eb = jnp.exp2(jnp.where(mask_b, seg[:, (hpg + h) * L:(hpg + h + 1) * L], neg_inf))
                ws_.append((cb * (ef * dt_f_r[h:h + 1, :] + eb * dt_b_r[h:h + 1, :])).astype(BF16))
            scale_f = jnp.exp2(seg[:, (DT_ROWS + j) * L:(DT_ROWS + j + 1) * L])
            scale_b = jnp.exp2(seg[:, (DT_ROWS + npair + j) * L:(DT_ROWS + npair + j + 1) * L])
            ys.append(_dot(jnp.concatenate(ws_, axis=1), xm_scr[c, j])
                      + scale_f * off_f[:, pair_cols[j]] + scale_b * off_b[:, pair_cols[j]])
        y = jnp.concatenate(ys, axis=1) + dskip_ref[...] * xc[:, x_cols]
        o_ref[pl.ds(base, L), :] = y
        return carry
    lax.fori_loop(0, nc, pass_c, 0, unroll=SSD_UNROLL)


def _conv_shift_matrices():
    out = np.zeros((3, len(CONV_SHIFTED_TAPS) * SSM_CHUNK, CONV_WIN), np.float32)
    for v, lead in enumerate((0, CONV_WIN_LEAD, CONV_WIN - SSM_CHUNK)):
        for i, kk in enumerate(CONV_SHIFTED_TAPS):
            for t in range(SSM_CHUNK):
                j = t + kk - SSM_CONV // 2 + lead
                if 0 <= j < CONV_WIN:
                    out[v, i * SSM_CHUNK + t, j] = 1.0
    return out


def _seg_matmul_constants(nc):
    L, K, half = SSM_CHUNK, SEG_K_PER_ROW, SSM_HEAD_DIM
    place = np.zeros((3 * L, nc * L), np.float32)
    lhs_ones = np.zeros((1, nc * L), np.float32)
    for c in range(nc):
        for hd in range(DT_ROWS):
            for i in range(3):
                place[i * L + c * DT_ROWS + hd, c * L + hd * K + i] = 1.0
                lhs_ones[0, c * L + hd * K + 3 + i] = 1.0
    rhs = np.zeros((SEG_K, SEG_BLOCKS * L), np.float32)
    for hd in range(DT_ROWS):
        d, h = divmod(hd, HEADS_PER_GROUP)
        pair_block = DT_ROWS + d * (HEADS_PER_GROUP // 2) + h // 2
        lanes = slice(0, half) if h % 2 == 0 else slice(half, L)
        rhs[hd * K:hd * K + 3, hd * L:(hd + 1) * L] = 1.0
        rhs[hd * K:hd * K + 3, pair_block * L:(pair_block + 1) * L][:, lanes] = 1.0
    return place, lhs_ones, rhs


def _ssd_call(xbc, dtc, cw, cbias, adt, dskip, batch, seq):
    nc = seq // SSM_CHUNK
    L = SSM_CHUNK
    assert nc >= 3 and seq >= CONV_WIN and nc * DT_ROWS == L
    sh = jnp.asarray(_conv_shift_matrices(), BF16)
    place, lhs_ones, rhs_const = _seg_matmul_constants(nc)
    place = jnp.asarray(place, BF16)
    lhs_ones = jnp.asarray(lhs_ones, F32)
    rhs_const = jnp.asarray(rhs_const, BF16)
    const = lambda a: pl.BlockSpec(a.shape, lambda b, g: (0,) * a.ndim)
    b_blk0 = SSM_INNER // SSM_STATE
    c_blk0 = b_blk0 + SSM_GROUPS

    def xbc_views(rows_, batched):
        lead = (lambda b: b) if batched else (lambda b: 0)
        return [pl.BlockSpec((rows_, GROUP_INNER), lambda b, g: (lead(b), g)),
                pl.BlockSpec((rows_, SSM_STATE), lambda b, g: (lead(b), b_blk0 + g)),
                pl.BlockSpec((rows_, SSM_STATE), lambda b, g: (lead(b), c_blk0 + g))]
    return pl.pallas_call(
        _ssd_kernel,
        grid=(batch, SSM_GROUPS),
        in_specs=xbc_views(seq, True) + [
                  pl.BlockSpec((nc, DT_ROWS, L), lambda b, g: (b, g, 0)),
                  const(sh), const(place), const(lhs_ones), const(rhs_const)]
                 + xbc_views(SUBLANES, False) + xbc_views(1, False) + [
                  pl.BlockSpec((2, DT_ROWS, L), lambda b, g: (0, g, 0)),
                  pl.BlockSpec((1, GROUP_INNER), lambda b, g: (0, g))],
        out_specs=pl.BlockSpec((seq, GROUP_INNER), lambda b, g: (b, g)),
        out_shape=jax.ShapeDtypeStruct((batch * seq, SSM_INNER), F32),
        scratch_shapes=[pltpu.VMEM((nc, L, GROUP_COLS), F32),
                        pltpu.VMEM((nc, HEADS_PER_GROUP // 2, 2 * L, L), BF16),
                        pltpu.VMEM((nc, SSM_STATE, GROUP_INNER), F32),
                        pltpu.VMEM((nc, SSM_STATE, GROUP_INNER), F32),
                        pltpu.VMEM((nc, DT_ROWS, L), F32),
                        pltpu.VMEM((nc, DT_ROWS, L), F32),
                        pltpu.VMEM((nc, DT_ROWS, L), F32),
                        pltpu.VMEM((nc, L, SEG_K), BF16),
                        pltpu.VMEM((SSD_UNROLL, SEG_K, SEG_BLOCKS * L), BF16),
                        pltpu.VMEM((3, nc, DT_ROWS, L), F32)],
        compiler_params=pltpu.CompilerParams(dimension_semantics=("arbitrary", "arbitrary"),
                                             vmem_limit_bytes=VMEM_LIMIT),
        name="ssd",
    )(xbc, xbc, xbc, dtc, sh, place, lhs_ones, rhs_const, cw, cw, cw, cbias, cbias, cbias,
      adt, dskip)


def _mlp_kernel(x_ref, attn_ref, ssm_ref, z_ref, gat_ref, gn_ref, wo_ref, gm_ref, wup_ref, wdn_ref, o_ref):
    for sub in range(x_ref.shape[0] // MLP_SUBTILE):
        rs = slice(sub * MLP_SUBTILE, (sub + 1) * MLP_SUBTILE)
        a = attn_ref[rs, :]
        an = (a * _inv_rms(a) * gat_ref[...]).astype(BF16)
        zc = z_ref[rs, :].astype(F32)
        y = ssm_ref[rs, :] * (zc * jax.nn.sigmoid(zc))
        yn = jnp.concatenate([y[:, g * GROUP_INNER:(g + 1) * GROUP_INNER]
                              * _inv_rms(y[:, g * GROUP_INNER:(g + 1) * GROUP_INNER]) for g in range(SSM_GROUPS)],
                             axis=1) * gn_ref[...]
        mix = _dot(an, wo_ref[0:ATTN_WIDTH, :]) + _dot(yn.astype(BF16), wo_ref[ATTN_WIDTH:D_MIX, :])
        x1 = x_ref[rs, :] + mix
        hm = (x1 * _inv_rms(x1) * gm_ref[...]).astype(BF16)
        acc = jnp.zeros_like(x1)
        for c in range(D_FF // FF_CHUNK):
            cols = slice(c * FF_CHUNK, (c + 1) * FF_CHUNK)
            u = _dot(hm, wup_ref[:, cols])
            acc = acc + _dot(jnp.square(jnp.maximum(u, 0.0)).astype(BF16), wdn_ref[cols, :])
        o_ref[rs, :] = x1 + acc


def _mlp_call(x2, attn, ssm, z, gat, gn, wo, gm, wup, wdn):
    t = x2.shape[0]
    row = lambda w: pl.BlockSpec((TM_MLP, w), lambda i: (i, 0))
    full = lambda a: pl.BlockSpec(a.shape, lambda i: (0, 0), pipeline_mode=pl.Buffered(1))
    return pl.pallas_call(
        _mlp_kernel,
        grid=(t // TM_MLP,),
        in_specs=[row(D_MODEL), row(ATTN_WIDTH), row(SSM_INNER), row(SSM_INNER), full(gat), full(gn), full(wo),
                  full(gm), full(wup), full(wdn)],
        out_specs=row(D_MODEL),
        out_shape=jax.ShapeDtypeStruct((t, D_MODEL), F32),
        compiler_params=pltpu.CompilerParams(dimension_semantics=("arbitrary",),
                                             vmem_limit_bytes=VMEM_LIMIT),
        name="outproj_mlp",
    )(x2, attn, ssm, z, gat, gn, wo, gm, wup, wdn)


def _lane_bcast(v):
    return jnp.broadcast_to(v[..., None], v.shape + (LANES,)).astype(F32)


def _layer(x2, cs, batch, seq, ln_mix_g, stacked_proj_weights, q_a_norm_g, kv_a_norm_g, q_norm_g,
           k_norm_g, attn_out_norm_g, conv_w, conv_b, a_log_fwd, a_log_bwd, dt_bias_fwd, dt_bias_bwd,
           d_skip, ssm_norm_g, w_out, ln_mlp_g, w_mlp_up, w_mlp_down):
    win, wuq, wuk, wuv = _win_prep_call(*stacked_proj_weights)
    ones_v = np.ones((ATTN_HEADS // 2, V_HEAD_DIM), np.float32)
    vone = jnp.asarray(np.stack([np.concatenate([0 * ones_v, ones_v], axis=-1),
                                 np.concatenate([ones_v, 0 * ones_v], axis=-1)], axis=1).reshape(1, -1))
    hg = jnp.pad(jnp.stack([q_norm_g, k_norm_g]), ((0, SUBLANES - 2), (0, HEAD_PAD - QK_HEAD_DIM)))

    q, k, v, z, xbc, dtc = _inproj_call(
        x2, *cs, ln_mix_g[None, :], win, q_a_norm_g[None, :], wuq, kv_a_norm_g[None, :], wuk, wuv, vone,
        hg)

    attn, (wo_b, wup_b, wdn_b) = _attn_call(q, k, v, (w_out, w_mlp_up, w_mlp_down), batch, seq)

    cw = jnp.pad(conv_w[:, 0, :], ((0, SUBLANES - SSM_CONV), (0, 0)))
    cbias = conv_b[None, :]
    adt = jnp.stack([a_log_fwd, a_log_bwd, dt_bias_fwd, dt_bias_bwd]).reshape(2, 2, SSM_GROUPS, HEADS_PER_GROUP)
    adt = _lane_bcast(adt.transpose(0, 2, 1, 3).reshape(2, 2 * SSM_HEADS))
    dskip = jnp.repeat(d_skip, SSM_HEAD_DIM)[None, :]
    ssm = _ssd_call(xbc, dtc, cw, cbias, adt, dskip, batch, seq)

    return _mlp_call(x2, attn, ssm, z, attn_out_norm_g[None, :], ssm_norm_g[None, :], wo_b,
                     ln_mlp_g[None, :], wup_b, wdn_b)


def _rope_table(positions):
    inv_freq = 1.0 / (ROPE_THETA ** (jnp.arange(0, QK_ROPE_DIM, 2, dtype=F32) / QK_ROPE_DIM))
    ang = inv_freq[:, None] * positions.astype(F32).reshape(1, -1)
    return jnp.cos(ang), jnp.sin(ang)


def kernel(x, positions, ln_mix_g, w_in, q_a_norm_g, w_uq, kv_a_norm_g, w_ukv, q_norm_g, k_norm_g,
           attn_out_norm_g, conv_w, conv_b, a_log_fwd, a_log_bwd, dt_bias_fwd, dt_bias_bwd, d_skip,
           ssm_norm_g, w_out, ln_mlp_g, w_mlp_up, w_mlp_down):
    batch, seq, d = x.shape
    assert d == D_MODEL and seq % TQ == 0 and (batch * seq) % TM_IN == 0 and (batch * seq) % TM_MLP == 0
    cs = _rope_table(positions)
    x2 = x.reshape(batch * seq, d)
    for l in range(ln_mix_g.shape[0]):
        x2 = _layer(x2, cs, batch, seq, ln_mix_g[l], (w_in, w_uq, w_ukv, l), q_a_norm_g[l], kv_a_norm_g[l],
                    q_norm_g[l], k_norm_g[l], attn_out_norm_g[l], conv_w[l], conv_b[l],
                    a_log_fwd[l], a_log_bwd[l], dt_bias_fwd[l], dt_bias_bwd[l], d_skip[l], ssm_norm_g[l],
                    w_out[l], ln_mlp_g[l], w_mlp_up[l], w_mlp_down[l])
    return x2.reshape(batch, seq, d)
```

```python
import numpy as np
import jax
import jax.numpy as jnp
from jax import lax
from jax.experimental import pallas as pl
from jax.experimental.pallas import tpu as pltpu

F32 = jnp.float32
BF16 = jnp.bfloat16

D_MODEL = 1024
ATTN_HEADS = 8
QK_NOPE_DIM = 64
QK_ROPE_DIM = 32
QK_HEAD_DIM = QK_NOPE_DIM + QK_ROPE_DIM
V_HEAD_DIM = 64
Q_LORA_RANK = D_MODEL // 4
KV_LORA_RANK = D_MODEL // 8
ROPE_THETA = 10000.0
ATTN_WIDTH = ATTN_HEADS * V_HEAD_DIM
SSM_HEADS = 8
SSM_HEAD_DIM = 64
SSM_INNER = SSM_HEADS * SSM_HEAD_DIM
SSM_GROUPS = 2
SSM_STATE = 128
SSM_CONV = 5
SSM_CHUNK = 128
SSM_CONV_CH = SSM_INNER + 2 * SSM_GROUPS * SSM_STATE
D_MIX = ATTN_WIDTH + SSM_INNER
D_FF = 4 * D_MODEL
EPS = 1e-6

LANES = 128
SUBLANES = 8
HEAD_PAD = LANES

HEADS_PER_GROUP = SSM_HEADS // SSM_GROUPS
GROUP_INNER = SSM_INNER // SSM_GROUPS
GROUP_COLS = GROUP_INNER + 2 * SSM_STATE
DT_ROWS = 2 * HEADS_PER_GROUP
BF16_ROWS = 16
SEG_K_PER_ROW = 8
SEG_K = DT_ROWS * SEG_K_PER_ROW
SEG_BLOCKS = DT_ROWS + HEADS_PER_GROUP
LOG2E = float(np.log2(np.e))
CONV_WIN = 2 * SSM_CHUNK
CONV_WIN_LEAD = SSM_CHUNK // 2
CONV_SHIFTED_TAPS = tuple(k for k in range(SSM_CONV) if k != SSM_CONV // 2)

COL_CKV = Q_LORA_RANK
COL_MISC = COL_CKV + KV_LORA_RANK
COL_Z = COL_MISC + LANES
COL_XBC = COL_Z + SSM_INNER
IN_COLS = COL_XBC + SSM_CONV_CH

TM_IN = 1024
IN_SUBTILE = 512
TQ = 1024
ATTN_HEADS_PER_STEP = 8
TM_MLP = 1024
MLP_SUBTILE = 512
FF_CHUNK = 1024
SSD_UNROLL = 16
CONV_AHEAD = 4
PASS_A_UNROLL = 12
WIN_PREP_STEPS = 4
VMEM_LIMIT = 56 * 1024 * 1024


def _inv_rms(x):
    n = x.shape[-1]
    x2 = x * x
    acc = x2[:, 0:LANES]
    for i in range(1, n // LANES):
        acc = acc + x2[:, i * LANES:(i + 1) * LANES]
    return lax.rsqrt(jnp.sum(acc, axis=-1, keepdims=True) * (1.0 / n) + EPS)


def _dot(a, b):
    return jnp.dot(a, b, preferred_element_type=F32)


def _dot_nt(a, b):
    return lax.dot_general(a, b, (((1,), (1,)), ((), ())), preferred_element_type=F32)


def _head_gains(hg_ref):
    half = QK_ROPE_DIM // 2
    g = hg_ref[...]
    lane = lax.broadcasted_iota(jnp.int32, g.shape, 1)
    ext = jnp.where(lane < QK_HEAD_DIM, g,
                    jnp.where(lane < QK_HEAD_DIM + half, pltpu.roll(g, half, 1),
                              pltpu.roll(g, QK_ROPE_DIM + half, 1)))
    is_nope = lane[0:1, :] < QK_NOPE_DIM
    gq = ext[0:1, :] * (QK_HEAD_DIM ** -0.5 * LOG2E)
    return gq, jnp.where(is_nope, g[1:2, :], 0.0), jnp.where(is_nope, 0.0, ext[1:2, :])


def _inproj_kernel(x_ref, cs_ref, g_ref, win_ref, gqa_ref, wuq_ref, gkva_ref, wuk_ref, wuv_ref,
                   vone_ref, hg_ref,
                   q_ref, k_ref, v_ref, z_ref, xbc_ref, dt_ref):
    gains = _head_gains(hg_ref)
    for sub in range(x_ref.shape[0] // IN_SUBTILE):
        _inproj_rows(sub * IN_SUBTILE, x_ref, cs_ref, g_ref, win_ref, gqa_ref, wuq_ref, gkva_ref,
                     wuk_ref, wuv_ref, vone_ref, gains, q_ref, k_ref, v_ref, z_ref, xbc_ref, dt_ref)


def _inproj_rows(r0, x_ref, cs_ref, g_ref, win_ref, gqa_ref, wuq_ref, gkva_ref, wuk_ref, wuv_ref,
                 vone_ref, gains, q_ref, k_ref, v_ref, z_ref, xbc_ref, dt_ref):
    gq, gk_nope, gk_rope = gains
    tm = IN_SUBTILE
    rs = slice(r0, r0 + tm)
    x = x_ref[rs, :]
    h = (x * _inv_rms(x) * g_ref[...]).astype(BF16)
    big = _dot(h, win_ref[...])
    z_ref[rs, :] = big[:, COL_Z:COL_XBC].astype(BF16)
    xbc_ref[rs, :] = big[:, COL_XBC:IN_COLS].astype(BF16)
    misc = big[:, COL_MISC:COL_Z]
    for c in range(tm // SSM_CHUNK):
        dt_ref[r0 // SSM_CHUNK + c] = misc[c * SSM_CHUNK:(c + 1) * SSM_CHUNK, :].T[0:2 * SSM_HEADS, :]

    cq = big[:, 0:COL_CKV]
    ckv = big[:, COL_CKV:COL_MISC]
    cqn = (cq * _inv_rms(cq) * gqa_ref[...]).astype(BF16)
    ckvn = (ckv * _inv_rms(ckv) * gkva_ref[...]).astype(BF16)
    q_pre = _dot(cqn, wuq_ref[...])
    k_pre = _dot(ckvn, wuk_ref[...])
    v_ref[rs, :] = (_dot(ckvn, wuv_ref[...]) + vone_ref[...]).astype(BF16)

    half = QK_ROPE_DIM // 2
    zpad = jnp.zeros((HEAD_PAD - QK_ROPE_DIM, SSM_CHUNK), F32)
    tt = jnp.concatenate(
        [jnp.concatenate([cs_ref[:, r0 + c * SSM_CHUNK:r0 + (c + 1) * SSM_CHUNK], zpad], axis=0).T
         for c in range(tm // SSM_CHUNK)], axis=0)
    lane_t = lax.broadcasted_iota(jnp.int32, (tm, HEAD_PAD), 1)
    cs = jnp.where(lane_t < QK_NOPE_DIM, 1.0,
                   jnp.where(lane_t < QK_NOPE_DIM + half, pltpu.roll(tt, QK_NOPE_DIM, 1),
                             jnp.where(lane_t < QK_HEAD_DIM + half, pltpu.roll(tt, QK_NOPE_DIM + half, 1),
                                       pltpu.roll(tt, QK_HEAD_DIM, 1))))
    lane = lax.broadcasted_iota(jnp.int32, (1, HEAD_PAD), 1)
    in_head = (lane < QK_HEAD_DIM).astype(F32)
    is_rope = ((lane >= QK_NOPE_DIM) & (lane < QK_HEAD_DIM)).astype(F32)
    inv_d = 1.0 / QK_HEAD_DIM

    gcq = gq * cs
    for hh in range(ATTN_HEADS):
        sl = slice(hh * HEAD_PAD, (hh + 1) * HEAD_PAD)
        qh = q_pre[:, sl]
        ssq = jnp.sum(qh * qh * in_head, axis=-1, keepdims=True)
        q_ref[rs, sl] = (qh * lax.rsqrt(ssq * inv_d + EPS) * gcq).astype(BF16)

    ab = misc * (gk_rope * cs)
    lane2 = lax.broadcasted_iota(jnp.int32, (tm, HEAD_PAD), 1)
    swapped = jnp.where(lane2 < QK_HEAD_DIM, pltpu.roll(ab, HEAD_PAD - QK_ROPE_DIM, 1),
                        pltpu.roll(ab, QK_ROPE_DIM, 1))
    s_both = jnp.where(lane2 >= QK_NOPE_DIM, ab + swapped, 0.0)
    ssq_pe = jnp.sum(misc * misc * is_rope, axis=-1, keepdims=True)
    gkn = gk_nope
    for hh in range(ATTN_HEADS):
        sl = slice(hh * HEAD_PAD, (hh + 1) * HEAD_PAD)
        kh = k_pre[:, sl]
        ssq = jnp.sum(kh * kh, axis=-1, keepdims=True) + ssq_pe
        k_ref[rs, sl] = ((kh * gkn + s_both) * lax.rsqrt(ssq * inv_d + EPS)).astype(BF16)


def _win_prep_kernel(wt_ref, wq_ref, wkv_ref, o_ref, wuq_ref, wuk_ref, wuv_ref):
    half = QK_ROPE_DIM // 2

    @pl.when(pl.program_id(0) == 0)
    def _up_projection_tiles():
        wq = wq_ref[...]
        q_pieces = []
        for hh in range(ATTN_HEADS):
            head = wq[:, hh * QK_HEAD_DIM:(hh + 1) * QK_HEAD_DIM]
            rope = head[:, QK_NOPE_DIM:]
            q_pieces += [head, -rope[:, half:], rope[:, :half]]
        wuq_ref[...] = jnp.concatenate(q_pieces, axis=1).astype(BF16)
        wkv = wkv_ref[...]
        zeros = jnp.zeros((wkv.shape[0], V_HEAD_DIM), F32)
        k_pieces, v_pieces = [], []
        for hh in range(ATTN_HEADS):
            nope = wkv[:, hh * HEAD_PAD:hh * HEAD_PAD + QK_NOPE_DIM]
            val = wkv[:, hh * HEAD_PAD + QK_NOPE_DIM:(hh + 1) * HEAD_PAD]
            k_pieces += [nope, zeros]
            v_pieces += [val, zeros] if hh % 2 == 0 else [zeros, val]
        wuk_ref[...] = jnp.concatenate(k_pieces, axis=1).astype(BF16)
        wuv_ref[...] = jnp.concatenate(v_pieces, axis=1).astype(BF16)

    cols = wt_ref.shape[1]
    o_kpe = Q_LORA_RANK + KV_LORA_RANK
    o_z = o_kpe + QK_ROPE_DIM
    o_dt = o_z + SSM_INNER + SSM_CONV_CH
    hpg = HEADS_PER_GROUP
    assert SSM_HEADS == SUBLANES and SSM_GROUPS == 2
    dt_f = wt_ref[o_dt:o_dt + SSM_HEADS, :]
    dt_b = wt_ref[o_dt + SSM_HEADS:o_dt + 2 * SSM_HEADS, :]
    top = lax.broadcasted_iota(jnp.int32, (SUBLANES, cols), 0) < hpg
    dt_g0 = jnp.where(top, dt_f, pltpu.roll(dt_b, hpg, 0))
    dt_g1 = jnp.where(top, pltpu.roll(dt_f, hpg, 0), dt_b)
    kpe = wt_ref[o_kpe:o_z, :]
    rows = jnp.concatenate(
        [wt_ref[0:o_kpe, :], dt_g0, dt_g1, jnp.zeros((QK_NOPE_DIM - 2 * SSM_HEADS, cols), F32),
         kpe, -kpe[half:, :], kpe[:half, :], wt_ref[o_z:o_dt, :]], axis=0)
    for ct in range(cols // LANES):
        o_ref[ct * LANES:(ct + 1) * LANES, :] = jnp.concatenate(
            [rows[rt * LANES:(rt + 1) * LANES, ct * LANES:(ct + 1) * LANES].T for rt in range(IN_COLS // LANES)],
            axis=1).astype(BF16)


def _win_prep_call(w_in3, w_uq3, w_ukv3, layer):
    depth, d, n = w_in3.shape
    cols = d // WIN_PREP_STEPS
    whole = lambda a: pl.BlockSpec(a.shape[1:], lambda i: (layer, 0))
    head_cols = ATTN_HEADS * HEAD_PAD
    return pl.pallas_call(
        _win_prep_kernel,
        grid=(WIN_PREP_STEPS,),
        in_specs=[pl.BlockSpec((n, cols), lambda i: (layer, i)), whole(w_uq3), whole(w_ukv3)],
        out_specs=[pl.BlockSpec((cols, IN_COLS), lambda i: (i, 0)),
                   pl.BlockSpec((Q_LORA_RANK, head_cols), lambda i: (0, 0)),
                   pl.BlockSpec((KV_LORA_RANK, head_cols), lambda i: (0, 0)),
                   pl.BlockSpec((KV_LORA_RANK, head_cols), lambda i: (0, 0))],
        out_shape=[jax.ShapeDtypeStruct((d, IN_COLS), BF16),
                   jax.ShapeDtypeStruct((Q_LORA_RANK, head_cols), BF16),
                   jax.ShapeDtypeStruct((KV_LORA_RANK, head_cols), BF16),
                   jax.ShapeDtypeStruct((KV_LORA_RANK, head_cols), BF16)],
        compiler_params=pltpu.CompilerParams(dimension_semantics=("arbitrary",), vmem_limit_bytes=VMEM_LIMIT),
        name="win_prep",
    )(jnp.transpose(w_in3, (0, 2, 1)).reshape(depth * n, d),
      w_uq3.reshape(depth * w_uq3.shape[1], w_uq3.shape[2]),
      w_ukv3.reshape(depth * w_ukv3.shape[1], w_ukv3.shape[2]))


def _inproj_call(x2, cs, g, win, gqa, wuq, gkva, wuk, wuv, vone, hg):
    t = x2.shape[0]
    cpt = TM_IN // SSM_CHUNK
    full = lambda a: pl.BlockSpec(a.shape, lambda i: (0,) * a.ndim, pipeline_mode=pl.Buffered(1))
    row = lambda w: pl.BlockSpec((TM_IN, w), lambda i: (i, 0))
    return pl.pallas_call(
        _inproj_kernel,
        grid=(t // TM_IN,),
        in_specs=[row(D_MODEL), pl.BlockSpec((QK_ROPE_DIM, TM_IN), lambda i: (0, i)), full(g), full(win), full(gqa), full(wuq), full(gkva),
                  full(wuk), full(wuv), full(vone), full(hg)],
        out_specs=[row(ATTN_HEADS * HEAD_PAD), row(ATTN_HEADS * HEAD_PAD), row(ATTN_HEADS * HEAD_PAD),
                   row(SSM_INNER), row(SSM_CONV_CH),
                   pl.BlockSpec((cpt, 2 * SSM_HEADS, SSM_CHUNK), lambda i: (i, 0, 0))],
        out_shape=[jax.ShapeDtypeStruct((t, ATTN_HEADS * HEAD_PAD), BF16),
                   jax.ShapeDtypeStruct((t, ATTN_HEADS * HEAD_PAD), BF16),
                   jax.ShapeDtypeStruct((t, ATTN_HEADS * HEAD_PAD), BF16),
                   jax.ShapeDtypeStruct((t, SSM_INNER), BF16),
                   jax.ShapeDtypeStruct((t, SSM_CONV_CH), BF16),
                   jax.ShapeDtypeStruct((t // SSM_CHUNK, 2 * SSM_HEADS, SSM_CHUNK), F32)],
        compiler_params=pltpu.CompilerParams(dimension_semantics=("arbitrary",),
                                             vmem_limit_bytes=VMEM_LIMIT),
        name="inproj",
    )(x2, cs, g, win, gqa, wuq, gkva, wuk, wuv, vone, hg)


def _attn_kernel(q_ref, k_ref, v_ref, *refs):
    n_w = (len(refs) - 1) // 2
    w_in_refs, o_ref, w_out_refs = refs[:n_w], refs[n_w], refs[n_w + 1:]
    for w_in_ref, w_out_ref in zip(w_in_refs, w_out_refs):
        w_out_ref[...] = w_in_ref[...].astype(BF16)

    lane = lax.broadcasted_iota(jnp.int32, (q_ref.shape[0], HEAD_PAD), 1)
    for jp in range(ATTN_HEADS_PER_STEP // 2):
        sls = [slice(j * HEAD_PAD, (j + 1) * HEAD_PAD) for j in (2 * jp, 2 * jp + 1)]
        ss = [_dot_nt(q_ref[:, sl], k_ref[:, sl]) for sl in sls]
        ps = [jnp.exp2(s - jnp.max(s, axis=-1, keepdims=True)).astype(BF16) for s in ss]
        accs = [_dot(p, v_ref[:, sl]) for p, sl in zip(ps, sls)]
        res = [acc / pltpu.roll(acc, V_HEAD_DIM, 1) for acc in accs]
        o_ref[:, jp * HEAD_PAD:(jp + 1) * HEAD_PAD] = jnp.where(lane < V_HEAD_DIM, res[0], res[1])


def _attn_call(q, k, v, later_weights, batch, seq):
    n_q = seq // TQ
    hps = ATTN_HEADS_PER_STEP
    assert ATTN_HEADS == hps
    n_steps = batch * n_q

    def rows_spec(w):
        assert w.shape[0] % (n_steps * BF16_ROWS) == 0
        return pl.BlockSpec((w.shape[0] // n_steps, w.shape[1]), lambda b, hp, i: (b * n_q + i, 0))
    w_specs = [rows_spec(w) for w in later_weights]
    outs = pl.pallas_call(
        _attn_kernel,
        grid=(batch, ATTN_HEADS // hps, n_q),
        in_specs=[pl.BlockSpec((TQ, hps * HEAD_PAD), lambda b, hp, i: (b * n_q + i, hp)),
                  pl.BlockSpec((seq, hps * HEAD_PAD), lambda b, hp, i: (b, hp)),
                  pl.BlockSpec((seq, hps * HEAD_PAD), lambda b, hp, i: (b, hp))] + w_specs,
        out_specs=[pl.BlockSpec((TQ, hps * V_HEAD_DIM), lambda b, hp, i: (b * n_q + i, hp))] + w_specs,
        out_shape=[jax.ShapeDtypeStruct((batch * seq, ATTN_WIDTH), F32)]
                  + [jax.ShapeDtypeStruct(w.shape, BF16) for w in later_weights],
        compiler_params=pltpu.CompilerParams(
            dimension_semantics=("arbitrary", "arbitrary", "arbitrary"),
            vmem_limit_bytes=VMEM_LIMIT),
        name="attention",
    )(q, k, v, *later_weights)
    return outs[0], outs[1:]


def _split3(x):
    hi = x.astype(BF16)
    r1 = x - hi.astype(F32)
    mid = r1.astype(BF16)
    lo = (r1 - mid.astype(F32)).astype(BF16)
    return hi, mid, lo


def _ssd_kernel(x_ref, b_ref, c_ref, dt_ref, sh_ref, place_ref, lhs_ones_ref, rhs_const_ref,
                cwx_ref, cwb_ref, cwc_ref, cbx_ref, cbb_ref, cbc_ref, adt_ref, dskip_ref,
                o_ref, xc_scr, xm_scr, nsf_scr, nsb_scr, dtv_scr, w_scr, dec_scr, lhs_scr, rhs_scr, rpart_scr):
    nc = dt_ref.shape[0]
    seq = x_ref.shape[0]
    L = SSM_CHUNK
    hpg = HEADS_PER_GROUP
    npair = hpg // 2
    P = SSM_HEAD_DIM
    x_cols = slice(0, GROUP_INNER)
    b_cols = slice(GROUP_INNER, GROUP_INNER + SSM_STATE)
    c_cols = slice(GROUP_INNER + SSM_STATE, GROUP_COLS)
    pair_cols = [slice(j * L, (j + 1) * L) for j in range(npair)]

    @pl.when((pl.program_id(0) == 0) & (pl.program_id(1) == 0))
    def _init_rhs():
        for u in range(SSD_UNROLL):
            rhs_scr[u] = rhs_const_ref[...]

    a_neg = -jnp.exp(adt_ref[0]) * LOG2E
    dtv = jax.nn.softplus(dt_ref[...] + adt_ref[1][None])
    dtv_scr[...] = dtv
    da2 = (dtv * a_neg[None]).reshape(nc * DT_ROWS, L)
    ri = lax.broadcasted_iota(jnp.int32, (L, L), 0)
    ci = lax.broadcasted_iota(jnp.int32, (L, L), 1)
    upper = (ri <= ci).astype(BF16)
    lower = (ri >= ci).astype(BF16)
    tri = jnp.concatenate([upper, lower], axis=1)
    cs_fb = sum(_dot(p, tri) for p in _split3(da2))
    rowsel = (ri & hpg) == 0
    cs2 = jnp.where(rowsel, cs_fb[:, 0:L], cs_fb[:, L:2 * L])
    colcs = cs2.T

    col_parts = jnp.concatenate(_split3(colcs), axis=1)
    half_n = place_ref.shape[1] // 2
    for hf in range(2):
        cols = slice(hf * half_n, (hf + 1) * half_n)
        lhs_half = _dot(col_parts, place_ref[:, cols]) + lhs_ones_ref[:, cols]
        for c in range(nc // 2):
            lhs_scr[hf * (nc // 2) + c] = lhs_half[:, c * L:c * L + SEG_K].astype(BF16)
    for i, part in enumerate(_split3(-cs2)):
        rpart_scr[i] = part.astype(F32).reshape(nc, DT_ROWS, L)

    def lane_bcast(col):
        return jnp.broadcast_to(col, (col.shape[0], L))

    tot2 = jnp.where(rowsel, lane_bcast(cs2[:, L - 1:L]), lane_bcast(cs2[:, 0:1]))
    dec_scr[...] = jnp.exp2(tot2).reshape(nc, DT_ROWS, L)
    w_scr[...] = (dtv.reshape(nc * DT_ROWS, L) * jnp.exp2(tot2 - cs2)).reshape(nc, DT_ROWS, L)

    lane_t = lax.broadcasted_iota(jnp.int32, (L, L), 1)
    low_half = lane_t < P

    cw_halves = (cwx_ref[...], jnp.concatenate([cwb_ref[...], cwc_ref[...]], axis=1))
    cb_halves = (cbx_ref[...], jnp.concatenate([cbb_ref[...], cbc_ref[...]], axis=1))

    def rows(hf, start, size):
        if hf == 0:
            return x_ref[pl.ds(start, size), :]
        return jnp.concatenate([b_ref[pl.ds(start, size), :], c_ref[pl.ds(start, size), :]], axis=1)

    def conv_stage(c):
        base = pl.multiple_of(c * L, L)
        ws = pl.multiple_of(jnp.clip(c * L - CONV_WIN_LEAD, 0, seq - CONV_WIN), CONV_WIN_LEAD)
        variant = jnp.where(c == 0, 0, jnp.where(c == nc - 1, 2, 1))
        sh = sh_ref[variant]
        halves = []
        for hf in range(2):
            cw = cw_halves[hf]
            shifted = _dot(sh, rows(hf, ws, CONV_WIN))
            acc = cb_halves[hf] + cw[SSM_CONV // 2:SSM_CONV // 2 + 1, :] * rows(hf, base, L).astype(F32)
            for t, kk in enumerate(CONV_SHIFTED_TAPS):
                acc = acc + cw[kk:kk + 1, :] * shifted[t * L:(t + 1) * L, :]
            halves.append(acc * jax.nn.sigmoid(acc))
        xc = jnp.concatenate(halves, axis=1)
        xc_scr[c] = xc
        for j in range(npair):
            xp = xc[:, pair_cols[j]]
            xm_scr[c, j] = jnp.concatenate([jnp.where(low_half, xp, 0.0), jnp.where(low_half, 0.0, xp)],
                                           axis=0).astype(BF16)

    def state_loads(c):
        return xc_scr[c, :, b_cols], [xm_scr[c, j] for j in range(npair)]

    def state_stage(c, loaded):
        b_tok, xm = loaded
        bt = b_tok.T
        for d, ns_scr in ((0, nsf_scr), (1, nsb_scr)):
            w = w_scr[c, d * hpg:(d + 1) * hpg, :]
            ns_scr[c] = jnp.concatenate(
                [_dot(jnp.concatenate([(bt * w[2 * j:2 * j + 1, :]).astype(BF16),
                                       (bt * w[2 * j + 1:2 * j + 2, :]).astype(BF16)], axis=1), xm[j])
                 for j in range(npair)], axis=1)

    for c in range(CONV_AHEAD):
        conv_stage(c)

    def pass_a(c, carry):
        loaded = state_loads(c)
        conv_stage(c + CONV_AHEAD)
        state_stage(c, loaded)
        return carry
    lax.fori_loop(0, nc - CONV_AHEAD, pass_a, 0, unroll=PASS_A_UNROLL)
    for c in range(nc - CONV_AHEAD, nc):
        state_stage(c, state_loads(c))

    lane_r = lax.broadcasted_iota(jnp.int32, (1, L), 1)

    def decay_row(c, d):
        dec = dec_scr[c, d * hpg:(d + 1) * hpg, :]
        return jnp.concatenate([jnp.where(lane_r < P, dec[2 * j:2 * j + 1, :], dec[2 * j + 1:2 * j + 2, :])
                                for j in range(npair)], axis=1)

    def pass_b(d, ns_scr):
        def body(i, st):
            c = i if d == 0 else nc - 1 - i
            new = ns_scr[c]
            ns_scr[c] = st
            return st * decay_row(c, d) + new
        lax.fori_loop(0, nc, body, jnp.zeros((SSM_STATE, GROUP_INNER), F32), unroll=True)
    pass_b(0, nsf_scr)
    pass_b(1, nsb_scr)

    mask_f = ci <= ri
    mask_b = ci >= ri
    neg_inf = jnp.float32(-jnp.inf)
    sub16 = lax.broadcasted_iota(jnp.int32, (BF16_ROWS, L), 0)

    def pass_c(c, carry):
        base = pl.multiple_of(c * L, L)
        xc = xc_scr[c]
        bm = xc[:, b_cols].astype(BF16)
        cm = xc[:, c_cols].astype(BF16)
        cb = _dot_nt(cm, bm)
        off_f = _dot(cm, nsf_scr[c].astype(BF16))
        off_b = _dot(cm, nsb_scr[c].astype(BF16))
        slot = c % SSD_UNROLL
        for hd in range(DT_ROWS):
            r0 = (hd % 2) * SEG_K_PER_ROW
            blk = jnp.where((sub16 >= r0) & (sub16 < r0 + 3), 1.0, 0.0)
            for i in range(3):
                blk = jnp.where(sub16 == r0 + 3 + i, rpart_scr[i, c, hd:hd + 1, :], blk)
            rhs_scr[slot, (hd // 2) * BF16_ROWS:(hd // 2 + 1) * BF16_ROWS, hd * L:(hd + 1) * L] = blk.astype(BF16)
        seg = _dot(lhs_scr[c], rhs_scr[slot])
        dt_f_r = dtv_scr[c, 0:hpg, :]
        dt_b_r = dtv_scr[c, hpg:DT_ROWS, :]
        ys = []
        for j in range(npair):
            ws_ = []
            for h in (2 * j, 2 * j + 1):
                ef = jnp.exp2(jnp.where(mask_f, seg[:, h * L:(h + 1) * L], neg_inf))
                eb = jnp.exp2(jnp.where(mask_b, seg[:, (hpg + h) * L:(hpg + h + 1) * L], neg_inf))
                ws_.append((cb * (ef * dt_f_r[h:h + 1, :] + eb * dt_b_r[h:h + 1, :])).astype(BF16))
            scale_f = jnp.exp2(seg[:, (DT_ROWS + j) * L:(DT_ROWS + j + 1) * L])
            scale_b = jnp.exp2(seg[:, (DT_ROWS + npair + j) * L:(DT_ROWS + npair + j + 1) * L])
            ys.append(_dot(jnp.concatenate(ws_, axis=1), xm_scr[c, j])
                      + scale_f * off_f[:, pair_cols[j]] + scale_b * off_b[:, pair_cols[j]])
        y = jnp.concatenate(ys, axis=1) + dskip_ref[...] * xc[:, x_cols]
        o_ref[pl.ds(base, L), :] = y
        return carry
    lax.fori_loop(0, nc, pass_c, 0, unroll=SSD_UNROLL)


def _conv_shift_matrices():
    out = np.zeros((3, len(CONV_SHIFTED_TAPS) * SSM_CHUNK, CONV_WIN), np.float32)
    for v, lead in enumerate((0, CONV_WIN_LEAD, CONV_WIN - SSM_CHUNK)):
        for i, kk in enumerate(CONV_SHIFTED_TAPS):
            for t in range(SSM_CHUNK):
                j = t + kk - SSM_CONV // 2 + lead
                if 0 <= j < CONV_WIN:
                    out[v, i * SSM_CHUNK + t, j] = 1.0
    return out


def _seg_matmul_constants(nc):
    L, K, half = SSM_CHUNK, SEG_K_PER_ROW, SSM_HEAD_DIM
    place = np.zeros((3 * L, nc * L), np.float32)
    lhs_ones = np.zeros((1, nc * L), np.float32)
    for c in range(nc):
        for hd in range(DT_ROWS):
            for i in range(3):
                place[i * L + c * DT_ROWS + hd, c * L + hd * K + i] = 1.0
                lhs_ones[0, c * L + hd * K + 3 + i] = 1.0
    rhs = np.zeros((SEG_K, SEG_BLOCKS * L), np.float32)
    for hd in range(DT_ROWS):
        d, h = divmod(hd, HEADS_PER_GROUP)
        pair_block = DT_ROWS + d * (HEADS_PER_GROUP // 2) + h // 2
        lanes = slice(0, half) if h % 2 == 0 else slice(half, L)
        rhs[hd * K:hd * K + 3, hd * L:(hd + 1) * L] = 1.0
        rhs[hd * K:hd * K + 3, pair_block * L:(pair_block + 1) * L][:, lanes] = 1.0
    return place, lhs_ones, rhs


def _ssd_call(xbc, dtc, cw, cbias, adt, dskip, batch, seq):
    nc = seq // SSM_CHUNK
    L = SSM_CHUNK
    assert nc >= 3 and seq >= CONV_WIN and nc * DT_ROWS == L
    sh = jnp.asarray(_conv_shift_matrices(), BF16)
    place, lhs_ones, rhs_const = _seg_matmul_constants(nc)
    place = jnp.asarray(place, BF16)
    lhs_ones = jnp.asarray(lhs_ones, F32)
    rhs_const = jnp.asarray(rhs_const, BF16)
    const = lambda a: pl.BlockSpec(a.shape, lambda b, g: (0,) * a.ndim)
    b_blk0 = SSM_INNER // SSM_STATE
    c_blk0 = b_blk0 + SSM_GROUPS

    def xbc_views(rows_, batched):
        lead = (lambda b: b) if batched else (lambda b: 0)
        return [pl.BlockSpec((rows_, GROUP_INNER), lambda b, g: (lead(b), g)),
                pl.BlockSpec((rows_, SSM_STATE), lambda b, g: (lead(b), b_blk0 + g)),
                pl.BlockSpec((rows_, SSM_STATE), lambda b, g: (lead(b), c_blk0 + g))]
    return pl.pallas_call(
        _ssd_kernel,
        grid=(batch, SSM_GROUPS),
        in_specs=xbc_views(seq, True) + [
                  pl.BlockSpec((nc, DT_ROWS, L), lambda b, g: (b, g, 0)),
                  const(sh), const(place), const(lhs_ones), const(rhs_const)]
                 + xbc_views(SUBLANES, False) + xbc_views(1, False) + [
                  pl.BlockSpec((2, DT_ROWS, L), lambda b, g: (0, g, 0)),
                  pl.BlockSpec((1, GROUP_INNER), lambda b, g: (0, g))],
        out_specs=pl.BlockSpec((seq, GROUP_INNER), lambda b, g: (b, g)),
        out_shape=jax.ShapeDtypeStruct((batch * seq, SSM_INNER), F32),
        scratch_shapes=[pltpu.VMEM((nc, L, GROUP_COLS), F32),
                        pltpu.VMEM((nc, HEADS_PER_GROUP // 2, 2 * L, L), BF16),
                        pltpu.VMEM((nc, SSM_STATE, GROUP_INNER), F32),
                        pltpu.VMEM((nc, SSM_STATE, GROUP_INNER), F32),
                        pltpu.VMEM((nc, DT_ROWS, L), F32),
                        pltpu.VMEM((nc, DT_ROWS, L), F32),
                        pltpu.VMEM((nc, DT_ROWS, L), F32),
                        pltpu.VMEM((nc, L, SEG_K), BF16),
                        pltpu.VMEM((SSD_UNROLL, SEG_K, SEG_BLOCKS * L), BF16),
                        pltpu.VMEM((3, nc, DT_ROWS, L), F32)],
        compiler_params=pltpu.CompilerParams(dimension_semantics=("arbitrary", "arbitrary"),
                                             vmem_limit_bytes=VMEM_LIMIT),
        name="ssd",
    )(xbc, xbc, xbc, dtc, sh, place, lhs_ones, rhs_const, cw, cw, cw, cbias, cbias, cbias,
      adt, dskip)


def _mlp_kernel(x_ref, attn_ref, ssm_ref, z_ref, gat_ref, gn_ref, wo_ref, gm_ref, wup_ref, wdn_ref, o_ref):
    for sub in range(x_ref.shape[0] // MLP_SUBTILE):
        rs = slice(sub * MLP_SUBTILE, (sub + 1) * MLP_SUBTILE)
        a = attn_ref[rs, :]
        an = (a * _inv_rms(a) * gat_ref[...]).astype(BF16)
        zc = z_ref[rs, :].astype(F32)
        y = ssm_ref[rs, :] * (zc * jax.nn.sigmoid(zc))
        yn = jnp.concatenate([y[:, g * GROUP_INNER:(g + 1) * GROUP_INNER]
                              * _inv_rms(y[:, g * GROUP_INNER:(g + 1) * GROUP_INNER]) for g in range(SSM_GROUPS)],
                             axis=1) * gn_ref[...]
        mix = _dot(an, wo_ref[0:ATTN_WIDTH, :]) + _dot(yn.astype(BF16), wo_ref[ATTN_WIDTH:D_MIX, :])
        x1 = x_ref[rs, :] + mix
        hm = (x1 * _inv_rms(x1) * gm_ref[...]).astype(BF16)
        acc = jnp.zeros_like(x1)
        for c in range(D_FF // FF_CHUNK):
            cols = slice(c * FF_CHUNK, (c + 1) * FF_CHUNK)
            u = _dot(hm, wup_ref[:, cols])
            acc = acc + _dot(jnp.square(jnp.maximum(u, 0.0)).astype(BF16), wdn_ref[cols, :])
        o_ref[rs, :] = x1 + acc


def _mlp_call(x2, attn, ssm, z, gat, gn, wo, gm, wup, wdn):
    t = x2.shape[0]
    row = lambda w: pl.BlockSpec((TM_MLP, w), lambda i: (i, 0))
    full = lambda a: pl.BlockSpec(a.shape, lambda i: (0, 0), pipeline_mode=pl.Buffered(1))
    return pl.pallas_call(
        _mlp_kernel,
        grid=(t // TM_MLP,),
        in_specs=[row(D_MODEL), row(ATTN_WIDTH), row(SSM_INNER), row(SSM_INNER), full(gat), full(gn), full(wo),
                  full(gm), full(wup), full(wdn)],
        out_specs=row(D_MODEL),
        out_shape=jax.ShapeDtypeStruct((t, D_MODEL), F32),
        compiler_params=pltpu.CompilerParams(dimension_semantics=("arbitrary",),
                                             vmem_limit_bytes=VMEM_LIMIT),
        name="outproj_mlp",
    )(x2, attn, ssm, z, gat, gn, wo, gm, wup, wdn)


def _lane_bcast(v):
    return jnp.broadcast_to(v[..., None], v.shape + (LANES,)).astype(F32)


def _layer(x2, cs, batch, seq, ln_mix_g, stacked_proj_weights, q_a_norm_g, kv_a_norm_g, q_norm_g,
           k_norm_g, attn_out_norm_g, conv_w, conv_b, a_log_fwd, a_log_bwd, dt_bias_fwd, dt_bias_bwd,
           d_skip, ssm_norm_g, w_out, ln_mlp_g, w_mlp_up, w_mlp_down):
    win, wuq, wuk, wuv = _win_prep_call(*stacked_proj_weights)
    ones_v = np.ones((ATTN_HEADS // 2, V_HEAD_DIM), np.float32)
    vone = jnp.asarray(np.stack([np.concatenate([0 * ones_v, ones_v], axis=-1),
                                 np.concatenate([ones_v, 0 * ones_v], axis=-1)], axis=1).reshape(1, -1))
    hg = jnp.pad(jnp.stack([q_norm_g, k_norm_g]), ((0, SUBLANES - 2), (0, HEAD_PAD - QK_HEAD_DIM)))

    q, k, v, z, xbc, dtc = _inproj_call(
        x2, cs, ln_mix_g[None, :], win, q_a_norm_g[None, :], wuq, kv_a_norm_g[None, :], wuk, wuv, vone,
        hg)

    attn, (wo_b, wup_b, wdn_b) = _attn_call(q, k, v, (w_out, w_mlp_up, w_mlp_down), batch, seq)

    cw = jnp.pad(conv_w[:, 0, :], ((0, SUBLANES - SSM_CONV), (0, 0)))
    cbias = conv_b[None, :]
    adt = jnp.stack([a_log_fwd, a_log_bwd, dt_bias_fwd, dt_bias_bwd]).reshape(2, 2, SSM_GROUPS, HEADS_PER_GROUP)
    adt = _lane_bcast(adt.transpose(0, 2, 1, 3).reshape(2, 2 * SSM_HEADS))
    dskip = jnp.repeat(d_skip, SSM_HEAD_DIM)[None, :]
    ssm = _ssd_call(xbc, dtc, cw, cbias, adt, dskip, batch, seq)

    return _mlp_call(x2, attn, ssm, z, attn_out_norm_g[None, :], ssm_norm_g[None, :], wo_b,
                     ln_mlp_g[None, :], wup_b, wdn_b)


def _rope_table(positions):
    inv_freq = 1.0 / (ROPE_THETA ** (jnp.arange(0, QK_ROPE_DIM, 2, dtype=F32) / QK_ROPE_DIM))
    ang = inv_freq[:, None] * positions.astype(F32).reshape(1, -1)
    return jnp.concatenate([jnp.cos(ang), jnp.sin(ang)], axis=0)


def kernel(x, positions, ln_mix_g, w_in, q_a_norm_g, w_uq, kv_a_norm_g, w_ukv, q_norm_g, k_norm_g,
           attn_out_norm_g, conv_w, conv_b, a_log_fwd, a_log_bwd, dt_bias_fwd, dt_bias_bwd, d_skip,
           ssm_norm_g, w_out, ln_mlp_g, w_mlp_up, w_mlp_down):
    batch, seq, d = x.shape
    assert d == D_MODEL and seq % TQ == 0 and (batch * seq) % TM_IN == 0 and (batch * seq) % TM_MLP == 0
    cs = _rope_table(positions)
    x2 = x.reshape(batch * seq, d)
    for l in range(ln_mix_g.shape[0]):
        x2 = _layer(x2, cs, batch, seq, ln_mix_g[l], (w_in, w_uq, w_ukv, l), q_a_norm_g[l], kv_a_norm_g[l],
                    q_norm_g[l], k_norm_g[l], attn_out_norm_g[l], conv_w[l], conv_b[l],
                    a_log_fwd[l], a_log_bwd[l], dt_bias_fwd[l], dt_bias_bwd[l], d_skip[l], ssm_norm_g[l],
                    w_out[l], ln_mlp_g[l], w_mlp_up[l], w_mlp_down[l])
    return x2.reshape(batch, seq, d)
```

```python
import types

import numpy as np
import jax
import jax.numpy as jnp
from jax import lax
from jax.experimental import pallas as pl
from jax.experimental.pallas import tpu as pltpu

F32 = jnp.float32
BF16 = jnp.bfloat16

D_MODEL = 1024
ATTN_HEADS = 8
QK_NOPE_DIM = 64
QK_ROPE_DIM = 32
QK_HEAD_DIM = QK_NOPE_DIM + QK_ROPE_DIM
V_HEAD_DIM = 64
Q_LORA_RANK = D_MODEL // 4
KV_LORA_RANK = D_MODEL // 8
ROPE_THETA = 10000.0
ATTN_WIDTH = ATTN_HEADS * V_HEAD_DIM
SSM_HEADS = 8
SSM_HEAD_DIM = 64
SSM_INNER = SSM_HEADS * SSM_HEAD_DIM
SSM_GROUPS = 2
SSM_STATE = 128
SSM_CONV = 5
SSM_CHUNK = 128
SSM_CONV_CH = SSM_INNER + 2 * SSM_GROUPS * SSM_STATE
D_MIX = ATTN_WIDTH + SSM_INNER
D_FF = 4 * D_MODEL
EPS = 1e-6

LANES = 128
SUBLANES = 8
HEAD_PAD = LANES

HEADS_PER_GROUP = SSM_HEADS // SSM_GROUPS
GROUP_INNER = SSM_INNER // SSM_GROUPS
GROUP_COLS = GROUP_INNER + 2 * SSM_STATE
DT_ROWS = 2 * HEADS_PER_GROUP
BF16_ROWS = 16
SEG_K_PER_ROW = 8
SEG_K = DT_ROWS * SEG_K_PER_ROW
SEG_BLOCKS = DT_ROWS + HEADS_PER_GROUP
LOG2E = float(np.log2(np.e))
CONV_WIN = 2 * SSM_CHUNK
CONV_WIN_LEAD = SSM_CHUNK // 2
CONV_SHIFTED_TAPS = tuple(k for k in range(SSM_CONV) if k != SSM_CONV // 2)

COL_CKV = Q_LORA_RANK
COL_MISC = COL_CKV + KV_LORA_RANK
COL_Z = COL_MISC + LANES
COL_XBC = COL_Z + SSM_INNER
IN_COLS = COL_XBC + SSM_CONV_CH

TM_IN = 1024
IN_SUBTILE = 512
TQ = 1024
ATTN_HEADS_PER_STEP = 8
TM_MLP = 1024
MLP_SUBTILE = 512
FF_CHUNK = 1024
CONV_AHEAD = 4
WIN_PREP_STEPS = 4
VMEM_LIMIT = 56 * 1024 * 1024


def _inv_rms(x):
    n = x.shape[-1]
    x2 = x * x
    acc = x2[:, 0:LANES]
    for i in range(1, n // LANES):
        acc = acc + x2[:, i * LANES:(i + 1) * LANES]
    return lax.rsqrt(jnp.sum(acc, axis=-1, keepdims=True) * (1.0 / n) + EPS)


def _dot(a, b):
    return jnp.dot(a, b, preferred_element_type=F32)


def _dot_nt(a, b):
    return lax.dot_general(a, b, (((1,), (1,)), ((), ())), preferred_element_type=F32)


def _head_gains(hg_ref):
    half = QK_ROPE_DIM // 2
    g = hg_ref[...]
    lane = lax.broadcasted_iota(jnp.int32, g.shape, 1)
    ext = jnp.where(lane < QK_HEAD_DIM, g,
                    jnp.where(lane < QK_HEAD_DIM + half, pltpu.roll(g, half, 1),
                              pltpu.roll(g, QK_ROPE_DIM + half, 1)))
    is_nope = lane[0:1, :] < QK_NOPE_DIM
    gq = ext[0:1, :] * (QK_HEAD_DIM ** -0.5 * LOG2E)
    return gq, jnp.where(is_nope, g[1:2, :], 0.0), jnp.where(is_nope, 0.0, ext[1:2, :])


def _inproj_kernel(x_ref, cs_ref, g_ref, win_ref, gqa_ref, wuq_ref, gkva_ref, wuk_ref, wuv_ref,
                   vone_ref, hg_ref,
                   q_ref, k_ref, v_ref, z_ref, xbc_ref, dt_ref):
    gains = _head_gains(hg_ref)
    for sub in range(x_ref.shape[0] // IN_SUBTILE):
        _inproj_rows(sub * IN_SUBTILE, x_ref, cs_ref, g_ref, win_ref, gqa_ref, wuq_ref, gkva_ref,
                     wuk_ref, wuv_ref, vone_ref, gains, q_ref, k_ref, v_ref, z_ref, xbc_ref, dt_ref)


def _inproj_rows(r0, x_ref, cs_ref, g_ref, win_ref, gqa_ref, wuq_ref, gkva_ref, wuk_ref, wuv_ref,
                 vone_ref, gains, q_ref, k_ref, v_ref, z_ref, xbc_ref, dt_ref):
    gq, gk_nope, gk_rope = gains
    tm = IN_SUBTILE
    rs = slice(r0, r0 + tm)
    x = x_ref[rs, :]
    h = (x * _inv_rms(x) * g_ref[...]).astype(BF16)
    big = _dot(h, win_ref[...])
    z_ref[rs, :] = big[:, COL_Z:COL_XBC].astype(BF16)
    xbc_ref[rs, :] = big[:, COL_XBC:IN_COLS].astype(BF16)
    misc = big[:, COL_MISC:COL_Z]
    for c in range(tm // SSM_CHUNK):
        dt_ref[r0 // SSM_CHUNK + c] = misc[c * SSM_CHUNK:(c + 1) * SSM_CHUNK, :].T[0:2 * SSM_HEADS, :]

    cq = big[:, 0:COL_CKV]
    ckv = big[:, COL_CKV:COL_MISC]
    cqn = (cq * _inv_rms(cq) * gqa_ref[...]).astype(BF16)
    ckvn = (ckv * _inv_rms(ckv) * gkva_ref[...]).astype(BF16)
    q_pre = _dot(cqn, wuq_ref[...])
    k_pre = _dot(ckvn, wuk_ref[...])
    v_ref[rs, :] = (_dot(ckvn, wuv_ref[...]) + vone_ref[...]).astype(BF16)

    half = QK_ROPE_DIM // 2
    zpad = jnp.zeros((HEAD_PAD - QK_ROPE_DIM, SSM_CHUNK), F32)
    tt = jnp.concatenate(
        [jnp.concatenate([cs_ref[:, r0 + c * SSM_CHUNK:r0 + (c + 1) * SSM_CHUNK], zpad], axis=0).T
         for c in range(tm // SSM_CHUNK)], axis=0)
    lane_t = lax.broadcasted_iota(jnp.int32, (tm, HEAD_PAD), 1)
    cs = jnp.where(lane_t < QK_NOPE_DIM, 1.0,
                   jnp.where(lane_t < QK_NOPE_DIM + half, pltpu.roll(tt, QK_NOPE_DIM, 1),
                             jnp.where(lane_t < QK_HEAD_DIM + half, pltpu.roll(tt, QK_NOPE_DIM + half, 1),
                                       pltpu.roll(tt, QK_HEAD_DIM, 1))))
    lane = lax.broadcasted_iota(jnp.int32, (1, HEAD_PAD), 1)
    in_head = (lane < QK_HEAD_DIM).astype(F32)
    is_rope = ((lane >= QK_NOPE_DIM) & (lane < QK_HEAD_DIM)).astype(F32)
    inv_d = 1.0 / QK_HEAD_DIM

    gcq = gq * cs
    for hh in range(ATTN_HEADS):
        sl = slice(hh * HEAD_PAD, (hh + 1) * HEAD_PAD)
        qh = q_pre[:, sl]
        ssq = jnp.sum(qh * qh * in_head, axis=-1, keepdims=True)
        q_ref[rs, sl] = (qh * lax.rsqrt(ssq * inv_d + EPS) * gcq).astype(BF16)

    ab = misc * (gk_rope * cs)
    lane2 = lax.broadcasted_iota(jnp.int32, (tm, HEAD_PAD), 1)
    swapped = jnp.where(lane2 < QK_HEAD_DIM, pltpu.roll(ab, HEAD_PAD - QK_ROPE_DIM, 1),
                        pltpu.roll(ab, QK_ROPE_DIM, 1))
    s_both = jnp.where(lane2 >= QK_NOPE_DIM, ab + swapped, 0.0)
    ssq_pe = jnp.sum(misc * misc * is_rope, axis=-1, keepdims=True)
    gkn = gk_nope
    for hh in range(ATTN_HEADS):
        sl = slice(hh * HEAD_PAD, (hh + 1) * HEAD_PAD)
        kh = k_pre[:, sl]
        ssq = jnp.sum(kh * kh, axis=-1, keepdims=True) + ssq_pe
        k_ref[rs, sl] = ((kh * gkn + s_both) * lax.rsqrt(ssq * inv_d + EPS)).astype(BF16)


def _win_prep_kernel(wt_ref, wq_ref, wkv_ref, o_ref, wuq_ref, wuk_ref, wuv_ref):
    half = QK_ROPE_DIM // 2

    @pl.when(pl.program_id(0) == 0)
    def _up_projection_tiles():
        wq = wq_ref[...]
        q_pieces = []
        for hh in range(ATTN_HEADS):
            head = wq[:, hh * QK_HEAD_DIM:(hh + 1) * QK_HEAD_DIM]
            rope = head[:, QK_NOPE_DIM:]
            q_pieces += [head, -rope[:, half:], rope[:, :half]]
        wuq_ref[...] = jnp.concatenate(q_pieces, axis=1).astype(BF16)
        wkv = wkv_ref[...]
        zeros = jnp.zeros((wkv.shape[0], V_HEAD_DIM), F32)
        k_pieces, v_pieces = [], []
        for hh in range(ATTN_HEADS):
            nope = wkv[:, hh * HEAD_PAD:hh * HEAD_PAD + QK_NOPE_DIM]
            val = wkv[:, hh * HEAD_PAD + QK_NOPE_DIM:(hh + 1) * HEAD_PAD]
            k_pieces += [nope, zeros]
            v_pieces += [val, zeros] if hh % 2 == 0 else [zeros, val]
        wuk_ref[...] = jnp.concatenate(k_pieces, axis=1).astype(BF16)
        wuv_ref[...] = jnp.concatenate(v_pieces, axis=1).astype(BF16)

    cols = wt_ref.shape[1]
    o_kpe = Q_LORA_RANK + KV_LORA_RANK
    o_z = o_kpe + QK_ROPE_DIM
    o_dt = o_z + SSM_INNER + SSM_CONV_CH
    hpg = HEADS_PER_GROUP
    assert SSM_HEADS == SUBLANES and SSM_GROUPS == 2
    dt_f = wt_ref[o_dt:o_dt + SSM_HEADS, :]
    dt_b = wt_ref[o_dt + SSM_HEADS:o_dt + 2 * SSM_HEADS, :]
    top = lax.broadcasted_iota(jnp.int32, (SUBLANES, cols), 0) < hpg
    dt_g0 = jnp.where(top, dt_f, pltpu.roll(dt_b, hpg, 0))
    dt_g1 = jnp.where(top, pltpu.roll(dt_f, hpg, 0), dt_b)
    kpe = wt_ref[o_kpe:o_z, :]
    rows = jnp.concatenate(
        [wt_ref[0:o_kpe, :], dt_g0, dt_g1, jnp.zeros((QK_NOPE_DIM - 2 * SSM_HEADS, cols), F32),
         kpe, -kpe[half:, :], kpe[:half, :], wt_ref[o_z:o_dt, :]], axis=0)
    for ct in range(cols // LANES):
        o_ref[ct * LANES:(ct + 1) * LANES, :] = jnp.concatenate(
            [rows[rt * LANES:(rt + 1) * LANES, ct * LANES:(ct + 1) * LANES].T for rt in range(IN_COLS // LANES)],
            axis=1).astype(BF16)


def _win_prep_call(w_in3, w_uq3, w_ukv3, layer):
    depth, d, n = w_in3.shape
    cols = d // WIN_PREP_STEPS
    whole = lambda a: pl.BlockSpec(a.shape[1:], lambda i: (layer, 0))
    head_cols = ATTN_HEADS * HEAD_PAD
    return pl.pallas_call(
        _win_prep_kernel,
        grid=(WIN_PREP_STEPS,),
        in_specs=[pl.BlockSpec((n, cols), lambda i: (layer, i)), whole(w_uq3), whole(w_ukv3)],
        out_specs=[pl.BlockSpec((cols, IN_COLS), lambda i: (i, 0)),
                   pl.BlockSpec((Q_LORA_RANK, head_cols), lambda i: (0, 0)),
                   pl.BlockSpec((KV_LORA_RANK, head_cols), lambda i: (0, 0)),
                   pl.BlockSpec((KV_LORA_RANK, head_cols), lambda i: (0, 0))],
        out_shape=[jax.ShapeDtypeStruct((d, IN_COLS), BF16),
                   jax.ShapeDtypeStruct((Q_LORA_RANK, head_cols), BF16),
                   jax.ShapeDtypeStruct((KV_LORA_RANK, head_cols), BF16),
                   jax.ShapeDtypeStruct((KV_LORA_RANK, head_cols), BF16)],
        compiler_params=pltpu.CompilerParams(dimension_semantics=("arbitrary",), vmem_limit_bytes=VMEM_LIMIT),
        name="win_prep",
    )(jnp.transpose(w_in3, (0, 2, 1)).reshape(depth * n, d),
      w_uq3.reshape(depth * w_uq3.shape[1], w_uq3.shape[2]),
      w_ukv3.reshape(depth * w_ukv3.shape[1], w_ukv3.shape[2]))


def _inproj_call(x2, cs, g, win, gqa, wuq, gkva, wuk, wuv, vone, hg):
    t = x2.shape[0]
    cpt = TM_IN // SSM_CHUNK
    full = lambda a: pl.BlockSpec(a.shape, lambda i: (0,) * a.ndim, pipeline_mode=pl.Buffered(1))
    row = lambda w: pl.BlockSpec((TM_IN, w), lambda i: (i, 0))
    return pl.pallas_call(
        _inproj_kernel,
        grid=(t // TM_IN,),
        in_specs=[row(D_MODEL), pl.BlockSpec((QK_ROPE_DIM, TM_IN), lambda i: (0, i)), full(g), full(win), full(gqa), full(wuq), full(gkva),
                  full(wuk), full(wuv), full(vone), full(hg)],
        out_specs=[row(ATTN_HEADS * HEAD_PAD), row(ATTN_HEADS * HEAD_PAD), row(ATTN_HEADS * HEAD_PAD),
                   row(SSM_INNER), row(SSM_CONV_CH),
                   pl.BlockSpec((cpt, 2 * SSM_HEADS, SSM_CHUNK), lambda i: (i, 0, 0))],
        out_shape=[jax.ShapeDtypeStruct((t, ATTN_HEADS * HEAD_PAD), BF16),
                   jax.ShapeDtypeStruct((t, ATTN_HEADS * HEAD_PAD), BF16),
                   jax.ShapeDtypeStruct((t, ATTN_HEADS * HEAD_PAD), BF16),
                   jax.ShapeDtypeStruct((t, SSM_INNER), BF16),
                   jax.ShapeDtypeStruct((t, SSM_CONV_CH), BF16),
                   jax.ShapeDtypeStruct((t // SSM_CHUNK, 2 * SSM_HEADS, SSM_CHUNK), F32)],
        compiler_params=pltpu.CompilerParams(dimension_semantics=("arbitrary",),
                                             vmem_limit_bytes=VMEM_LIMIT),
        name="inproj",
    )(x2, cs, g, win, gqa, wuq, gkva, wuk, wuv, vone, hg)


def _attn_kernel(q_ref, k_ref, v_ref, *refs):
    n_w = (len(refs) - 1) // 2
    w_in_refs, o_ref, w_out_refs = refs[:n_w], refs[n_w], refs[n_w + 1:]
    for w_in_ref, w_out_ref in zip(w_in_refs, w_out_refs):
        w_out_ref[...] = w_in_ref[...].astype(BF16)

    lane = lax.broadcasted_iota(jnp.int32, (q_ref.shape[0], HEAD_PAD), 1)
    for jp in range(ATTN_HEADS_PER_STEP // 2):
        sls = [slice(j * HEAD_PAD, (j + 1) * HEAD_PAD) for j in (2 * jp, 2 * jp + 1)]
        ss = [_dot_nt(q_ref[:, sl], k_ref[:, sl]) for sl in sls]
        ps = [jnp.exp2(s - jnp.max(s, axis=-1, keepdims=True)).astype(BF16) for s in ss]
        accs = [_dot(p, v_ref[:, sl]) for p, sl in zip(ps, sls)]
        res = [acc / pltpu.roll(acc, V_HEAD_DIM, 1) for acc in accs]
        o_ref[:, jp * HEAD_PAD:(jp + 1) * HEAD_PAD] = jnp.where(lane < V_HEAD_DIM, res[0], res[1])


def _attn_call(q, k, v, later_weights, batch, seq):
    n_q = seq // TQ
    hps = ATTN_HEADS_PER_STEP
    assert ATTN_HEADS == hps
    n_steps = batch * n_q

    def rows_spec(w):
        assert w.shape[0] % (n_steps * BF16_ROWS) == 0
        return pl.BlockSpec((w.shape[0] // n_steps, w.shape[1]), lambda b, hp, i: (b * n_q + i, 0))
    w_specs = [rows_spec(w) for w in later_weights]
    outs = pl.pallas_call(
        _attn_kernel,
        grid=(batch, ATTN_HEADS // hps, n_q),
        in_specs=[pl.BlockSpec((TQ, hps * HEAD_PAD), lambda b, hp, i: (b * n_q + i, hp)),
                  pl.BlockSpec((seq, hps * HEAD_PAD), lambda b, hp, i: (b, hp)),
                  pl.BlockSpec((seq, hps * HEAD_PAD), lambda b, hp, i: (b, hp))] + w_specs,
        out_specs=[pl.BlockSpec((TQ, hps * V_HEAD_DIM), lambda b, hp, i: (b * n_q + i, hp))] + w_specs,
        out_shape=[jax.ShapeDtypeStruct((batch * seq, ATTN_WIDTH), F32)]
                  + [jax.ShapeDtypeStruct(w.shape, BF16) for w in later_weights],
        compiler_params=pltpu.CompilerParams(
            dimension_semantics=("arbitrary", "arbitrary", "arbitrary"),
            vmem_limit_bytes=VMEM_LIMIT),
        name="attention",
    )(q, k, v, *later_weights)
    return outs[0], outs[1:]


def _split3(x):
    hi = x.astype(BF16)
    r1 = x - hi.astype(F32)
    mid = r1.astype(BF16)
    lo = (r1 - mid.astype(F32)).astype(BF16)
    return hi, mid, lo


def _ssd_kernel(x_ref, b_ref, c_ref, dt_ref, sh_ref, place_ref, lhs_ones_ref, rhs_const_ref,
                cwx_ref, cwb_ref, cwc_ref, cbx_ref, cbb_ref, cbc_ref, adt_ref, dskip_ref,
                o_ref, xc_scr, xm_scr, nsf_scr, nsb_scr, dtv_scr, w_scr, dec_scr, lhs_scr, rhs_scr, rpart_scr):
    nc = dt_ref.shape[0]

    @pl.when(pl.program_id(0) == 0)
    def _init_rhs():
        for g in range(SSM_GROUPS):
            for c in range(nc):
                rhs_scr[g, c] = rhs_const_ref[...]

    groups = []
    for g in range(SSM_GROUPS):
        inner = slice(g * GROUP_INNER, (g + 1) * GROUP_INNER)
        state = slice(g * SSM_STATE, (g + 1) * SSM_STATE)
        rows = slice(g * DT_ROWS, (g + 1) * DT_ROWS)
        groups.append(_ssd_group_stages(
            x_ref.at[:, inner], b_ref.at[:, state], c_ref.at[:, state], dt_ref.at[:, rows, :],
            sh_ref, place_ref, lhs_ones_ref,
            cwx_ref.at[:, inner], cwb_ref.at[:, state], cwc_ref.at[:, state],
            cbx_ref.at[:, inner], cbb_ref.at[:, state], cbc_ref.at[:, state],
            adt_ref.at[:, rows, :], dskip_ref.at[:, inner], o_ref.at[:, inner],
            *[scr.at[g] for scr in (xc_scr, xm_scr, nsf_scr, nsb_scr, dtv_scr, w_scr, dec_scr, lhs_scr,
                                    rhs_scr, rpart_scr)]))
    first, second = groups

    def pass_a(grp, c):
        loaded = grp.state_loads(c)
        if c + CONV_AHEAD < nc:
            grp.conv_stage(c + CONV_AHEAD)
        grp.state_stage(c, loaded)

    zero_state = jnp.zeros((SSM_STATE, GROUP_INNER), F32)
    first.dt_phase()
    for c in range(CONV_AHEAD):
        first.conv_stage(c)
    for c in range(nc):
        pass_a(first, c)
        if c == 0:
            second.dt_phase()
        if c >= nc - CONV_AHEAD:
            second.conv_stage(c - (nc - CONV_AHEAD))
    st_f = st_b = zero_state
    for c in range(nc):
        pass_a(second, c)
        st_f = first.pass_b_step(0, c, st_f)
        st_b = first.pass_b_step(1, c, st_b)
    st_f = st_b = zero_state
    for c in range(nc):
        first.pass_c(c)
        st_f = second.pass_b_step(0, c, st_f)
        st_b = second.pass_b_step(1, c, st_b)
    for c in range(nc):
        second.pass_c(c)


def _ssd_group_stages(x_ref, b_ref, c_ref, dt_ref, sh_ref, place_ref, lhs_ones_ref,
                      cwx_ref, cwb_ref, cwc_ref, cbx_ref, cbb_ref, cbc_ref, adt_ref, dskip_ref,
                      o_ref, xc_scr, xm_scr, nsf_scr, nsb_scr, dtv_scr, w_scr, dec_scr, lhs_scr, rhs_scr, rpart_scr):
    nc = dt_ref.shape[0]
    seq = x_ref.shape[0]
    L = SSM_CHUNK
    hpg = HEADS_PER_GROUP
    npair = hpg // 2
    P = SSM_HEAD_DIM
    x_cols = slice(0, GROUP_INNER)
    b_cols = slice(GROUP_INNER, GROUP_INNER + SSM_STATE)
    c_cols = slice(GROUP_INNER + SSM_STATE, GROUP_COLS)
    pair_cols = [slice(j * L, (j + 1) * L) for j in range(npair)]

    ri = lax.broadcasted_iota(jnp.int32, (L, L), 0)
    ci = lax.broadcasted_iota(jnp.int32, (L, L), 1)

    def lane_bcast(col):
        return jnp.broadcast_to(col, (col.shape[0], L))

    def dt_phase():
        a_neg = -jnp.exp(adt_ref[0]) * LOG2E
        dtv = jax.nn.softplus(dt_ref[...] + adt_ref[1][None])
        dtv_scr[...] = dtv
        da2 = (dtv * a_neg[None]).reshape(nc * DT_ROWS, L)
        upper = (ri <= ci).astype(BF16)
        lower = (ri >= ci).astype(BF16)
        tri = jnp.concatenate([upper, lower], axis=1)
        cs_fb = sum(_dot(p, tri) for p in _split3(da2))
        rowsel = (ri & hpg) == 0
        cs2 = jnp.where(rowsel, cs_fb[:, 0:L], cs_fb[:, L:2 * L])
        colcs = cs2.T

        col_parts = jnp.concatenate(_split3(colcs), axis=1)
        half_n = place_ref.shape[1] // 2
        for hf in range(2):
            cols = slice(hf * half_n, (hf + 1) * half_n)
            lhs_half = _dot(col_parts, place_ref[:, cols]) + lhs_ones_ref[:, cols]
            for c in range(nc // 2):
                lhs_scr[hf * (nc // 2) + c] = lhs_half[:, c * L:c * L + SEG_K].astype(BF16)
        for i, part in enumerate(_split3(-cs2)):
            rpart_scr[i] = part.astype(F32).reshape(nc, DT_ROWS, L)

        tot2 = jnp.where(rowsel, lane_bcast(cs2[:, L - 1:L]), lane_bcast(cs2[:, 0:1]))
        dec_scr[...] = jnp.exp2(tot2).reshape(nc, DT_ROWS, L)
        w_scr[...] = (dtv.reshape(nc * DT_ROWS, L) * jnp.exp2(tot2 - cs2)).reshape(nc, DT_ROWS, L)

    low_half = ci < P

    def rows(hf, start, size):
        if hf == 0:
            return x_ref[pl.ds(start, size), :]
        return jnp.concatenate([b_ref[pl.ds(start, size), :], c_ref[pl.ds(start, size), :]], axis=1)

    def conv_stage(c):
        base = c * L
        ws = min(max(c * L - CONV_WIN_LEAD, 0), seq - CONV_WIN)
        variant = 0 if c == 0 else (2 if c == nc - 1 else 1)
        sh = sh_ref[variant]
        cw_halves = (cwx_ref[...], jnp.concatenate([cwb_ref[...], cwc_ref[...]], axis=1))
        cb_halves = (cbx_ref[...], jnp.concatenate([cbb_ref[...], cbc_ref[...]], axis=1))
        halves = []
        for hf in range(2):
            cw = cw_halves[hf]
            shifted = _dot(sh, rows(hf, ws, CONV_WIN))
            acc = cb_halves[hf] + cw[SSM_CONV // 2:SSM_CONV // 2 + 1, :] * rows(hf, base, L).astype(F32)
            for t, kk in enumerate(CONV_SHIFTED_TAPS):
                acc = acc + cw[kk:kk + 1, :] * shifted[t * L:(t + 1) * L, :]
            halves.append(acc * jax.nn.sigmoid(acc))
        xc = jnp.concatenate(halves, axis=1)
        xc_scr[c] = xc
        for j in range(npair):
            xp = xc[:, pair_cols[j]]
            xm_scr[c, j] = jnp.concatenate([jnp.where(low_half, xp, 0.0), jnp.where(low_half, 0.0, xp)],
                                           axis=0).astype(BF16)

    def state_loads(c):
        return xc_scr[c, :, b_cols], [xm_scr[c, j] for j in range(npair)]

    def state_stage(c, loaded):
        b_tok, xm = loaded
        bt = b_tok.T
        for d, ns_scr in ((0, nsf_scr), (1, nsb_scr)):
            w = w_scr[c, d * hpg:(d + 1) * hpg, :]
            ns_scr[c] = jnp.concatenate(
                [_dot(jnp.concatenate([(bt * w[2 * j:2 * j + 1, :]).astype(BF16),
                                       (bt * w[2 * j + 1:2 * j + 2, :]).astype(BF16)], axis=1), xm[j])
                 for j in range(npair)], axis=1)

    lane_r = lax.broadcasted_iota(jnp.int32, (1, L), 1)

    def decay_row(c, d):
        dec = dec_scr[c, d * hpg:(d + 1) * hpg, :]
        return jnp.concatenate([jnp.where(lane_r < P, dec[2 * j:2 * j + 1, :], dec[2 * j + 1:2 * j + 2, :])
                                for j in range(npair)], axis=1)

    def pass_b_step(d, i, st):
        ns_scr = nsf_scr if d == 0 else nsb_scr
        c = i if d == 0 else nc - 1 - i
        new = ns_scr[c]
        ns_scr[c] = st
        return st * decay_row(c, d) + new

    mask_f = ci <= ri
    mask_b = ci >= ri
    neg_inf = jnp.float32(-jnp.inf)
    sub16 = lax.broadcasted_iota(jnp.int32, (BF16_ROWS, L), 0)

    def pass_c(c):
        base = c * L
        xc = xc_scr[c]
        bm = xc[:, b_cols].astype(BF16)
        cm = xc[:, c_cols].astype(BF16)
        cb = _dot_nt(cm, bm)
        off_f = _dot(cm, nsf_scr[c].astype(BF16))
        off_b = _dot(cm, nsb_scr[c].astype(BF16))
        for hd in range(DT_ROWS):
            r0 = (hd % 2) * SEG_K_PER_ROW
            blk = jnp.where((sub16 >= r0) & (sub16 < r0 + 3), 1.0, 0.0)
            for i in range(3):
                blk = jnp.where(sub16 == r0 + 3 + i, rpart_scr[i, c, hd:hd + 1, :], blk)
            rhs_scr[c, (hd // 2) * BF16_ROWS:(hd // 2 + 1) * BF16_ROWS, hd * L:(hd + 1) * L] = blk.astype(BF16)
        seg = _dot(lhs_scr[c], rhs_scr[c])
        dt_f_r = dtv_scr[c, 0:hpg, :]
        dt_b_r = dtv_scr[c, hpg:DT_ROWS, :]
        ys = []
        for j in range(npair):
            ws_ = []
            for h in (2 * j, 2 * j + 1):
                ef = jnp.exp2(jnp.where(mask_f, seg[:, h * L:(h + 1) * L], neg_inf))
                eb = jnp.exp2(jnp.where(mask_b, seg[:, (hpg + h) * L:(hpg + h + 1) * L], neg_inf))
                ws_.append((cb * (ef * dt_f_r[h:h + 1, :] + eb * dt_b_r[h:h + 1, :])).astype(BF16))
            scale_f = jnp.exp2(seg[:, (DT_ROWS + j) * L:(DT_ROWS + j + 1) * L])
            scale_b = jnp.exp2(seg[:, (DT_ROWS + npair + j) * L:(DT_ROWS + npair + j + 1) * L])
            ys.append(_dot(jnp.concatenate(ws_, axis=1), xm_scr[c, j])
                      + scale_f * off_f[:, pair_cols[j]] + scale_b * off_b[:, pair_cols[j]])
        y = jnp.concatenate(ys, axis=1) + dskip_ref[...] * xc[:, x_cols]
        o_ref[pl.ds(base, L), :] = y

    return types.SimpleNamespace(dt_phase=dt_phase, conv_stage=conv_stage, state_loads=state_loads,
                                 state_stage=state_stage, pass_b_step=pass_b_step, pass_c=pass_c)


def _conv_shift_matrices():
    out = np.zeros((3, len(CONV_SHIFTED_TAPS) * SSM_CHUNK, CONV_WIN), np.float32)
    for v, lead in enumerate((0, CONV_WIN_LEAD, CONV_WIN - SSM_CHUNK)):
        for i, kk in enumerate(CONV_SHIFTED_TAPS):
            for t in range(SSM_CHUNK):
                j = t + kk - SSM_CONV // 2 + lead
                if 0 <= j < CONV_WIN:
                    out[v, i * SSM_CHUNK + t, j] = 1.0
    return out


def _seg_matmul_constants(nc):
    L, K, half = SSM_CHUNK, SEG_K_PER_ROW, SSM_HEAD_DIM
    place = np.zeros((3 * L, nc * L), np.float32)
    lhs_ones = np.zeros((1, nc * L), np.float32)
    for c in range(nc):
        for hd in range(DT_ROWS):
            for i in range(3):
                place[i * L + c * DT_ROWS + hd, c * L + hd * K + i] = 1.0
                lhs_ones[0, c * L + hd * K + 3 + i] = 1.0
    rhs = np.zeros((SEG_K, SEG_BLOCKS * L), np.float32)
    for hd in range(DT_ROWS):
        d, h = divmod(hd, HEADS_PER_GROUP)
        pair_block = DT_ROWS + d * (HEADS_PER_GROUP // 2) + h // 2
        lanes = slice(0, half) if h % 2 == 0 else slice(half, L)
        rhs[hd * K:hd * K + 3, hd * L:(hd + 1) * L] = 1.0
        rhs[hd * K:hd * K + 3, pair_block * L:(pair_block + 1) * L][:, lanes] = 1.0
    return place, lhs_ones, rhs


def _ssd_call(xbc, dtc, cw, cbias, adt, dskip, batch, seq):
    nc = seq // SSM_CHUNK
    L = SSM_CHUNK
    assert nc >= 3 and seq >= CONV_WIN and nc * DT_ROWS == L
    sh = jnp.asarray(_conv_shift_matrices(), BF16)
    place, lhs_ones, rhs_const = _seg_matmul_constants(nc)
    place = jnp.asarray(place, BF16)
    lhs_ones = jnp.asarray(lhs_ones, F32)
    rhs_const = jnp.asarray(rhs_const, BF16)
    const = lambda a: pl.BlockSpec(a.shape, lambda b: (0,) * a.ndim)
    G = SSM_GROUPS
    bc_cols = G * SSM_STATE
    b_blk = SSM_INNER // bc_cols
    c_blk = b_blk + 1

    def xbc_views(rows_, batched):
        lead = (lambda b: b) if batched else (lambda b: 0)
        return [pl.BlockSpec((rows_, SSM_INNER), lambda b: (lead(b), 0)),
                pl.BlockSpec((rows_, bc_cols), lambda b: (lead(b), b_blk)),
                pl.BlockSpec((rows_, bc_cols), lambda b: (lead(b), c_blk))]
    return pl.pallas_call(
        _ssd_kernel,
        grid=(batch,),
        in_specs=xbc_views(seq, True) + [
                  pl.BlockSpec((nc, G * DT_ROWS, L), lambda b: (b, 0, 0)),
                  const(sh), const(place), const(lhs_ones), const(rhs_const)]
                 + xbc_views(SUBLANES, False) + xbc_views(1, False) + [const(adt), const(dskip)],
        out_specs=pl.BlockSpec((seq, SSM_INNER), lambda b: (b, 0)),
        out_shape=jax.ShapeDtypeStruct((batch * seq, SSM_INNER), F32),
        scratch_shapes=[pltpu.VMEM((G, nc, L, GROUP_COLS), F32),
                        pltpu.VMEM((G, nc, HEADS_PER_GROUP // 2, 2 * L, L), BF16),
                        pltpu.VMEM((G, nc, SSM_STATE, GROUP_INNER), F32),
                        pltpu.VMEM((G, nc, SSM_STATE, GROUP_INNER), F32),
                        pltpu.VMEM((G, nc, DT_ROWS, L), F32),
                        pltpu.VMEM((G, nc, DT_ROWS, L), F32),
                        pltpu.VMEM((G, nc, DT_ROWS, L), F32),
                        pltpu.VMEM((G, nc, L, SEG_K), BF16),
                        pltpu.VMEM((G, nc, SEG_K, SEG_BLOCKS * L), BF16),
                        pltpu.VMEM((G, 3, nc, DT_ROWS, L), F32)],
        compiler_params=pltpu.CompilerParams(dimension_semantics=("arbitrary",),
                                             vmem_limit_bytes=VMEM_LIMIT),
        name="ssd",
    )(xbc, xbc, xbc, dtc, sh, place, lhs_ones, rhs_const, cw, cw, cw, cbias, cbias, cbias,
      adt, dskip)


def _mlp_kernel(x_ref, attn_ref, ssm_ref, z_ref, gat_ref, gn_ref, wo_ref, gm_ref, wup_ref, wdn_ref, o_ref):
    for sub in range(x_ref.shape[0] // MLP_SUBTILE):
        rs = slice(sub * MLP_SUBTILE, (sub + 1) * MLP_SUBTILE)
        a = attn_ref[rs, :]
        an = (a * _inv_rms(a) * gat_ref[...]).astype(BF16)
        zc = z_ref[rs, :].astype(F32)
        y = ssm_ref[rs, :] * (zc * jax.nn.sigmoid(zc))
        yn = jnp.concatenate([y[:, g * GROUP_INNER:(g + 1) * GROUP_INNER]
                              * _inv_rms(y[:, g * GROUP_INNER:(g + 1) * GROUP_INNER]) for g in range(SSM_GROUPS)],
                             axis=1) * gn_ref[...]
        mix = _dot(an, wo_ref[0:ATTN_WIDTH, :]) + _dot(yn.astype(BF16), wo_ref[ATTN_WIDTH:D_MIX, :])
        x1 = x_ref[rs, :] + mix
        hm = (x1 * _inv_rms(x1) * gm_ref[...]).astype(BF16)
        acc = jnp.zeros_like(x1)
        for c in range(D_FF // FF_CHUNK):
            cols = slice(c * FF_CHUNK, (c + 1) * FF_CHUNK)
            u = _dot(hm, wup_ref[:, cols])
            acc = acc + _dot(jnp.square(jnp.maximum(u, 0.0)).astype(BF16), wdn_ref[cols, :])
        o_ref[rs, :] = x1 + acc


def _mlp_call(x2, attn, ssm, z, gat, gn, wo, gm, wup, wdn):
    t = x2.shape[0]
    row = lambda w: pl.BlockSpec((TM_MLP, w), lambda i: (i, 0))
    full = lambda a: pl.BlockSpec(a.shape, lambda i: (0, 0), pipeline_mode=pl.Buffered(1))
    return pl.pallas_call(
        _mlp_kernel,
        grid=(t // TM_MLP,),
        in_specs=[row(D_MODEL), row(ATTN_WIDTH), row(SSM_INNER), row(SSM_INNER), full(gat), full(gn), full(wo),
                  full(gm), full(wup), full(wdn)],
        out_specs=row(D_MODEL),
        out_shape=jax.ShapeDtypeStruct((t, D_MODEL), F32),
        compiler_params=pltpu.CompilerParams(dimension_semantics=("arbitrary",),
                                             vmem_limit_bytes=VMEM_LIMIT),
        name="outproj_mlp",
    )(x2, attn, ssm, z, gat, gn, wo, gm, wup, wdn)


def _lane_bcast(v):
    return jnp.broadcast_to(v[..., None], v.shape + (LANES,)).astype(F32)


def _layer(x2, cs, batch, seq, ln_mix_g, stacked_proj_weights, q_a_norm_g, kv_a_norm_g, q_norm_g,
           k_norm_g, attn_out_norm_g, conv_w, conv_b, a_log_fwd, a_log_bwd, dt_bias_fwd, dt_bias_bwd,
           d_skip, ssm_norm_g, w_out, ln_mlp_g, w_mlp_up, w_mlp_down):
    win, wuq, wuk, wuv = _win_prep_call(*stacked_proj_weights)
    ones_v = np.ones((ATTN_HEADS // 2, V_HEAD_DIM), np.float32)
    vone = jnp.asarray(np.stack([np.concatenate([0 * ones_v, ones_v], axis=-1),
                                 np.concatenate([ones_v, 0 * ones_v], axis=-1)], axis=1).reshape(1, -1))
    hg = jnp.pad(jnp.stack([q_norm_g, k_norm_g]), ((0, SUBLANES - 2), (0, HEAD_PAD - QK_HEAD_DIM)))

    q, k, v, z, xbc, dtc = _inproj_call(
        x2, cs, ln_mix_g[None, :], win, q_a_norm_g[None, :], wuq, kv_a_norm_g[None, :], wuk, wuv, vone,
        hg)

    attn, (wo_b, wup_b, wdn_b) = _attn_call(q, k, v, (w_out, w_mlp_up, w_mlp_down), batch, seq)

    cw = jnp.pad(conv_w[:, 0, :], ((0, SUBLANES - SSM_CONV), (0, 0)))
    cbias = conv_b[None, :]
    adt = jnp.stack([a_log_fwd, a_log_bwd, dt_bias_fwd, dt_bias_bwd]).reshape(2, 2, SSM_GROUPS, HEADS_PER_GROUP)
    adt = _lane_bcast(adt.transpose(0, 2, 1, 3).reshape(2, 2 * SSM_HEADS))
    dskip = jnp.repeat(d_skip, SSM_HEAD_DIM)[None, :]
    ssm = _ssd_call(xbc, dtc, cw, cbias, adt, dskip, batch, seq)

    return _mlp_call(x2, attn, ssm, z, attn_out_norm_g[None, :], ssm_norm_g[None, :], wo_b,
                     ln_mlp_g[None, :], wup_b, wdn_b)


def _rope_table(positions):
    inv_freq = 1.0 / (ROPE_THETA ** (jnp.arange(0, QK_ROPE_DIM, 2, dtype=F32) / QK_ROPE_DIM))
    ang = inv_freq[:, None] * positions.astype(F32).reshape(1, -1)
    return jnp.concatenate([jnp.cos(ang), jnp.sin(ang)], axis=0)


def kernel(x, positions, ln_mix_g, w_in, q_a_norm_g, w_uq, kv_a_norm_g, w_ukv, q_norm_g, k_norm_g,
           attn_out_norm_g, conv_w, conv_b, a_log_fwd, a_log_bwd, dt_bias_fwd, dt_bias_bwd, d_skip,
           ssm_norm_g, w_out, ln_mlp_g, w_mlp_up, w_mlp_down):
    batch, seq, d = x.shape
    assert d == D_MODEL and seq % TQ == 0 and (batch * seq) % TM_IN == 0 and (batch * seq) % TM_MLP == 0
    cs = _rope_table(positions)
    x2 = x.reshape(batch * seq, d)
    for l in range(ln_mix_g.shape[0]):
        x2 = _layer(x2, cs, batch, seq, ln_mix_g[l], (w_in, w_uq, w_ukv, l), q_a_norm_g[l], kv_a_norm_g[l],
                    q_norm_g[l], k_norm_g[l], attn_out_norm_g[l], conv_w[l], conv_b[l],
                    a_log_fwd[l], a_log_bwd[l], dt_bias_fwd[l], dt_bias_bwd[l], d_skip[l], ssm_norm_g[l],
                    w_out[l], ln_mlp_g[l], w_mlp_up[l], w_mlp_down[l])
    return x2.reshape(batch, seq, d)
```

```python
import numpy as np
import jax
import jax.numpy as jnp
from jax import lax
from jax.experimental import pallas as pl
from jax.experimental.pallas import tpu as pltpu

F32 = jnp.float32
BF16 = jnp.bfloat16

D_MODEL = 1024
ATTN_HEADS = 8
QK_NOPE_DIM = 64
QK_ROPE_DIM = 32
QK_HEAD_DIM = QK_NOPE_DIM + QK_ROPE_DIM
V_HEAD_DIM = 64
Q_LORA_RANK = D_MODEL // 4
KV_LORA_RANK = D_MODEL // 8
ROPE_THETA = 10000.0
ATTN_WIDTH = ATTN_HEADS * V_HEAD_DIM
SSM_HEADS = 8
SSM_HEAD_DIM = 64
SSM_INNER = SSM_HEADS * SSM_HEAD_DIM
SSM_GROUPS = 2
SSM_STATE = 128
SSM_CONV = 5
SSM_CHUNK = 128
SSM_CONV_CH = SSM_INNER + 2 * SSM_GROUPS * SSM_STATE
D_MIX = ATTN_WIDTH + SSM_INNER
D_FF = 4 * D_MODEL
EPS = 1e-6

LANES = 128
SUBLANES = 8
HEAD_PAD = LANES

HEADS_PER_GROUP = SSM_HEADS // SSM_GROUPS
GROUP_INNER = SSM_INNER // SSM_GROUPS
GROUP_COLS = GROUP_INNER + 2 * SSM_STATE
DT_ROWS = 2 * HEADS_PER_GROUP
BF16_ROWS = 16
SEG_K_PER_ROW = 8
SEG_K = DT_ROWS * SEG_K_PER_ROW
SEG_BLOCKS = DT_ROWS + HEADS_PER_GROUP
LOG2E = float(np.log2(np.e))
CONV_WIN = 2 * SSM_CHUNK
CONV_WIN_LEAD = SSM_CHUNK // 2
CONV_SHIFTED_TAPS = tuple(k for k in range(SSM_CONV) if k != SSM_CONV // 2)

COL_CKV = Q_LORA_RANK
COL_MISC = COL_CKV + KV_LORA_RANK
COL_Z = COL_MISC + LANES
COL_XBC = COL_Z + SSM_INNER
IN_COLS = COL_XBC + SSM_CONV_CH

TM_IN = 1024
IN_SUBTILE = 512
TQ = 1024
ATTN_HEADS_PER_STEP = 8
TM_MLP = 1024
MLP_SUBTILE = 512
FF_CHUNK = 1024
SSD_UNROLL = 16
CONV_AHEAD = 4
PASS_A_UNROLL = 12
WIN_PREP_STEPS = 4
MIB = 1024 * 1024
VMEM_LIMIT = {"win_prep": 16 * MIB, "inproj": 48 * MIB, "attention": 56 * MIB, "ssd": 32 * MIB,
              "outproj_mlp": 56 * MIB}


def _inv_rms(x):
    n = x.shape[-1]
    x2 = x * x
    acc = x2[:, 0:LANES]
    for i in range(1, n // LANES):
        acc = acc + x2[:, i * LANES:(i + 1) * LANES]
    return lax.rsqrt(jnp.sum(acc, axis=-1, keepdims=True) * (1.0 / n) + EPS)


def _dot(a, b):
    return jnp.dot(a, b, preferred_element_type=F32)


def _dot_nt(a, b):
    return lax.dot_general(a, b, (((1,), (1,)), ((), ())), preferred_element_type=F32)


def _inproj_kernel(x_ref, cs_ref, g_ref, win_ref, gqa_ref, wuq_ref, gkva_ref, wuk_ref, wuv_ref, vone_ref,
                   hg_ref,
                   q_ref, k_ref, v_ref, z_ref, xbc_ref, dt_ref):
    for sub in range(x_ref.shape[0] // IN_SUBTILE):
        _inproj_rows(sub * IN_SUBTILE, x_ref, cs_ref, g_ref, win_ref, gqa_ref, wuq_ref, gkva_ref, wuk_ref,
                     wuv_ref, vone_ref, hg_ref, q_ref, k_ref, v_ref, z_ref, xbc_ref, dt_ref)


def _inproj_rows(r0, x_ref, cs_ref, g_ref, win_ref, gqa_ref, wuq_ref, gkva_ref, wuk_ref, wuv_ref, vone_ref,
                 hg_ref, q_ref, k_ref, v_ref, z_ref, xbc_ref, dt_ref):
    tm = IN_SUBTILE
    rs = slice(r0, r0 + tm)
    x = x_ref[rs, :]
    h = (x * _inv_rms(x) * g_ref[...]).astype(BF16)
    big = _dot(h, win_ref[...])
    z_ref[rs, :] = big[:, COL_Z:COL_XBC].astype(BF16)
    xbc_ref[rs, :] = big[:, COL_XBC:IN_COLS].astype(BF16)
    misc = big[:, COL_MISC:COL_Z]
    for c in range(tm // SSM_CHUNK):
        dt_ref[r0 // SSM_CHUNK + c] = misc[c * SSM_CHUNK:(c + 1) * SSM_CHUNK, :].T[0:2 * SSM_HEADS, :]

    cq = big[:, 0:COL_CKV]
    ckv = big[:, COL_CKV:COL_MISC]
    cqn = (cq * _inv_rms(cq) * gqa_ref[...]).astype(BF16)
    ckvn = (ckv * _inv_rms(ckv) * gkva_ref[...]).astype(BF16)
    q_pre = _dot(cqn, wuq_ref[...])
    k_pre = _dot(ckvn, wuk_ref[...])
    v_ref[rs, :] = (_dot(ckvn, wuv_ref[...]) + vone_ref[...]).astype(BF16)

    half = QK_ROPE_DIM // 2
    zpad = jnp.zeros((HEAD_PAD - QK_ROPE_DIM, SSM_CHUNK), F32)
    tt = jnp.concatenate(
        [jnp.concatenate([cs_ref[:, r0 + c * SSM_CHUNK:r0 + (c + 1) * SSM_CHUNK], zpad], axis=0).T
         for c in range(tm // SSM_CHUNK)], axis=0)
    lane_t = lax.broadcasted_iota(jnp.int32, (tm, HEAD_PAD), 1)
    cs = jnp.where(lane_t < QK_NOPE_DIM, 1.0,
                   jnp.where(lane_t < QK_NOPE_DIM + half, pltpu.roll(tt, QK_NOPE_DIM, 1),
                             jnp.where(lane_t < QK_HEAD_DIM + half, pltpu.roll(tt, QK_NOPE_DIM + half, 1),
                                       pltpu.roll(tt, QK_HEAD_DIM, 1))))
    lane = lax.broadcasted_iota(jnp.int32, (1, HEAD_PAD), 1)
    in_head = (lane < QK_HEAD_DIM).astype(F32)
    is_rope = ((lane >= QK_NOPE_DIM) & (lane < QK_HEAD_DIM)).astype(F32)
    inv_d = 1.0 / QK_HEAD_DIM

    gcq = hg_ref[0:1, :] * cs
    for hh in range(ATTN_HEADS):
        sl = slice(hh * HEAD_PAD, (hh + 1) * HEAD_PAD)
        qh = q_pre[:, sl]
        ssq = jnp.sum(qh * qh * in_head, axis=-1, keepdims=True)
        q_ref[rs, sl] = (qh * lax.rsqrt(ssq * inv_d + EPS) * gcq).astype(BF16)

    ab = misc * (hg_ref[2:3, :] * cs)
    lane2 = lax.broadcasted_iota(jnp.int32, (tm, HEAD_PAD), 1)
    swapped = jnp.where(lane2 < QK_HEAD_DIM, pltpu.roll(ab, HEAD_PAD - QK_ROPE_DIM, 1),
                        pltpu.roll(ab, QK_ROPE_DIM, 1))
    s_both = jnp.where(lane2 >= QK_NOPE_DIM, ab + swapped, 0.0)
    ssq_pe = jnp.sum(misc * misc * is_rope, axis=-1, keepdims=True)
    gkn = hg_ref[1:2, :]
    for hh in range(ATTN_HEADS):
        sl = slice(hh * HEAD_PAD, (hh + 1) * HEAD_PAD)
        kh = k_pre[:, sl]
        ssq = jnp.sum(kh * kh, axis=-1, keepdims=True) + ssq_pe
        k_ref[rs, sl] = ((kh * gkn + s_both) * lax.rsqrt(ssq * inv_d + EPS)).astype(BF16)


def _win_prep_kernel(wt_ref, wq_ref, wkv_ref, o_ref, wuq_ref, wuk_ref, wuv_ref):
    half = QK_ROPE_DIM // 2

    @pl.when(pl.program_id(0) == 0)
    def _up_projection_tiles():
        wq = wq_ref[...]
        q_pieces = []
        for hh in range(ATTN_HEADS):
            head = wq[:, hh * QK_HEAD_DIM:(hh + 1) * QK_HEAD_DIM]
            rope = head[:, QK_NOPE_DIM:]
            q_pieces += [head, -rope[:, half:], rope[:, :half]]
        wuq_ref[...] = jnp.concatenate(q_pieces, axis=1).astype(BF16)
        wkv = wkv_ref[...]
        zeros = jnp.zeros((wkv.shape[0], V_HEAD_DIM), F32)
        k_pieces, v_pieces = [], []
        for hh in range(ATTN_HEADS):
            nope = wkv[:, hh * HEAD_PAD:hh * HEAD_PAD + QK_NOPE_DIM]
            val = wkv[:, hh * HEAD_PAD + QK_NOPE_DIM:(hh + 1) * HEAD_PAD]
            k_pieces += [nope, zeros]
            v_pieces += [val, zeros] if hh % 2 == 0 else [zeros, val]
        wuk_ref[...] = jnp.concatenate(k_pieces, axis=1).astype(BF16)
        wuv_ref[...] = jnp.concatenate(v_pieces, axis=1).astype(BF16)

    cols = wt_ref.shape[1]
    o_kpe = Q_LORA_RANK + KV_LORA_RANK
    o_z = o_kpe + QK_ROPE_DIM
    o_dt = o_z + SSM_INNER + SSM_CONV_CH
    hpg = HEADS_PER_GROUP
    assert SSM_HEADS == SUBLANES and SSM_GROUPS == 2
    dt_f = wt_ref[o_dt:o_dt + SSM_HEADS, :]
    dt_b = wt_ref[o_dt + SSM_HEADS:o_dt + 2 * SSM_HEADS, :]
    top = lax.broadcasted_iota(jnp.int32, (SUBLANES, cols), 0) < hpg
    dt_g0 = jnp.where(top, dt_f, pltpu.roll(dt_b, hpg, 0))
    dt_g1 = jnp.where(top, pltpu.roll(dt_f, hpg, 0), dt_b)
    kpe = wt_ref[o_kpe:o_z, :]
    rows = jnp.concatenate(
        [wt_ref[0:o_kpe, :], dt_g0, dt_g1, jnp.zeros((QK_NOPE_DIM - 2 * SSM_HEADS, cols), F32),
         kpe, -kpe[half:, :], kpe[:half, :], wt_ref[o_z:o_dt, :]], axis=0)
    for ct in range(cols // LANES):
        o_ref[ct * LANES:(ct + 1) * LANES, :] = jnp.concatenate(
            [rows[rt * LANES:(rt + 1) * LANES, ct * LANES:(ct + 1) * LANES].T for rt in range(IN_COLS // LANES)],
            axis=1).astype(BF16)


def _win_prep_call(w_in3, w_uq3, w_ukv3, layer):
    depth, d, n = w_in3.shape
    cols = d // WIN_PREP_STEPS
    whole = lambda a: pl.BlockSpec(a.shape[1:], lambda i: (layer, 0))
    head_cols = ATTN_HEADS * HEAD_PAD
    return pl.pallas_call(
        _win_prep_kernel,
        grid=(WIN_PREP_STEPS,),
        in_specs=[pl.BlockSpec((n, cols), lambda i: (layer, i)), whole(w_uq3), whole(w_ukv3)],
        out_specs=[pl.BlockSpec((cols, IN_COLS), lambda i: (i, 0)),
                   pl.BlockSpec((Q_LORA_RANK, head_cols), lambda i: (0, 0)),
                   pl.BlockSpec((KV_LORA_RANK, head_cols), lambda i: (0, 0)),
                   pl.BlockSpec((KV_LORA_RANK, head_cols), lambda i: (0, 0))],
        out_shape=[jax.ShapeDtypeStruct((d, IN_COLS), BF16),
                   jax.ShapeDtypeStruct((Q_LORA_RANK, head_cols), BF16),
                   jax.ShapeDtypeStruct((KV_LORA_RANK, head_cols), BF16),
                   jax.ShapeDtypeStruct((KV_LORA_RANK, head_cols), BF16)],
        compiler_params=pltpu.CompilerParams(dimension_semantics=("arbitrary",),
                                             vmem_limit_bytes=VMEM_LIMIT["win_prep"]),
        name="win_prep",
    )(jnp.transpose(w_in3, (0, 2, 1)).reshape(depth * n, d),
      w_uq3.reshape(depth * w_uq3.shape[1], w_uq3.shape[2]),
      w_ukv3.reshape(depth * w_ukv3.shape[1], w_ukv3.shape[2]))


def _inproj_call(x2, cs, g, win, gqa, wuq, gkva, wuk, wuv, vone, hg):
    t = x2.shape[0]
    cpt = TM_IN // SSM_CHUNK
    full = lambda a: pl.BlockSpec(a.shape, lambda i: (0,) * a.ndim, pipeline_mode=pl.Buffered(1))
    row = lambda w: pl.BlockSpec((TM_IN, w), lambda i: (i, 0))
    return pl.pallas_call(
        _inproj_kernel,
        grid=(t // TM_IN,),
        in_specs=[row(D_MODEL), pl.BlockSpec((QK_ROPE_DIM, TM_IN), lambda i: (0, i)), full(g), full(win), full(gqa), full(wuq), full(gkva),
                  full(wuk), full(wuv), full(vone), full(hg)],
        out_specs=[row(ATTN_HEADS * HEAD_PAD), row(ATTN_HEADS * HEAD_PAD), row(ATTN_HEADS * HEAD_PAD),
                   row(SSM_INNER), row(SSM_CONV_CH),
                   pl.BlockSpec((cpt, 2 * SSM_HEADS, SSM_CHUNK), lambda i: (i, 0, 0))],
        out_shape=[jax.ShapeDtypeStruct((t, ATTN_HEADS * HEAD_PAD), BF16),
                   jax.ShapeDtypeStruct((t, ATTN_HEADS * HEAD_PAD), BF16),
                   jax.ShapeDtypeStruct((t, ATTN_HEADS * HEAD_PAD), BF16),
                   jax.ShapeDtypeStruct((t, SSM_INNER), BF16),
                   jax.ShapeDtypeStruct((t, SSM_CONV_CH), BF16),
                   jax.ShapeDtypeStruct((t // SSM_CHUNK, 2 * SSM_HEADS, SSM_CHUNK), F32)],
        compiler_params=pltpu.CompilerParams(dimension_semantics=("arbitrary",),
                                             vmem_limit_bytes=VMEM_LIMIT["inproj"]),
        name="inproj",
    )(x2, cs, g, win, gqa, wuq, gkva, wuk, wuv, vone, hg)


def _attn_kernel(q_ref, k_ref, v_ref, *refs):
    n_w = (len(refs) - 1) // 2
    w_in_refs, o_ref, w_out_refs = refs[:n_w], refs[n_w], refs[n_w + 1:]
    for w_in_ref, w_out_ref in zip(w_in_refs, w_out_refs):
        w_out_ref[...] = w_in_ref[...].astype(BF16)

    lane = lax.broadcasted_iota(jnp.int32, (q_ref.shape[0], HEAD_PAD), 1)
    for jp in range(ATTN_HEADS_PER_STEP // 2):
        sls = [slice(j * HEAD_PAD, (j + 1) * HEAD_PAD) for j in (2 * jp, 2 * jp + 1)]
        ss = [_dot_nt(q_ref[:, sl], k_ref[:, sl]) for sl in sls]
        ps = [jnp.exp2(s - jnp.max(s, axis=-1, keepdims=True)).astype(BF16) for s in ss]
        accs = [_dot(p, v_ref[:, sl]) for p, sl in zip(ps, sls)]
        res = [acc / pltpu.roll(acc, V_HEAD_DIM, 1) for acc in accs]
        o_ref[:, jp * HEAD_PAD:(jp + 1) * HEAD_PAD] = jnp.where(lane < V_HEAD_DIM, res[0], res[1])


def _attn_call(q, k, v, later_weights, batch, seq):
    n_q = seq // TQ
    hps = ATTN_HEADS_PER_STEP
    assert ATTN_HEADS == hps
    n_steps = batch * n_q

    def rows_spec(w):
        assert w.shape[0] % (n_steps * BF16_ROWS) == 0
        return pl.BlockSpec((w.shape[0] // n_steps, w.shape[1]), lambda b, hp, i: (b * n_q + i, 0))
    w_specs = [rows_spec(w) for w in later_weights]
    outs = pl.pallas_call(
        _attn_kernel,
        grid=(batch, ATTN_HEADS // hps, n_q),
        in_specs=[pl.BlockSpec((TQ, hps * HEAD_PAD), lambda b, hp, i: (b * n_q + i, hp)),
                  pl.BlockSpec((seq, hps * HEAD_PAD), lambda b, hp, i: (b, hp)),
                  pl.BlockSpec((seq, hps * HEAD_PAD), lambda b, hp, i: (b, hp))] + w_specs,
        out_specs=[pl.BlockSpec((TQ, hps * V_HEAD_DIM), lambda b, hp, i: (b * n_q + i, hp))] + w_specs,
        out_shape=[jax.ShapeDtypeStruct((batch * seq, ATTN_WIDTH), F32)]
                  + [jax.ShapeDtypeStruct(w.shape, BF16) for w in later_weights],
        compiler_params=pltpu.CompilerParams(
            dimension_semantics=("arbitrary", "arbitrary", "arbitrary"),
            vmem_limit_bytes=VMEM_LIMIT["attention"]),
        name="attention",
    )(q, k, v, *later_weights)
    return outs[0], outs[1:]


def _split3(x):
    hi = x.astype(BF16)
    r1 = x - hi.astype(F32)
    mid = r1.astype(BF16)
    lo = (r1 - mid.astype(F32)).astype(BF16)
    return hi, mid, lo


def _ssd_kernel(x_ref, b_ref, c_ref, dt_ref, sh_ref, place_ref, lhs_ones_ref, rhs_const_ref,
                cwx_ref, cwb_ref, cwc_ref, cbx_ref, cbb_ref, cbc_ref, adt_ref, dskip_ref,
                o_ref, xc_scr, xm_scr, nsf_scr, nsb_scr, dtv_scr, w_scr, dec_scr, lhs_scr, rhs_scr, rpart_scr):
    nc = dt_ref.shape[0]
    seq = x_ref.shape[0]
    L = SSM_CHUNK
    hpg = HEADS_PER_GROUP
    npair = hpg // 2
    P = SSM_HEAD_DIM
    x_cols = slice(0, GROUP_INNER)
    b_cols = slice(GROUP_INNER, GROUP_INNER + SSM_STATE)
    c_cols = slice(GROUP_INNER + SSM_STATE, GROUP_COLS)
    pair_cols = [slice(j * L, (j + 1) * L) for j in range(npair)]

    @pl.when((pl.program_id(0) == 0) & (pl.program_id(1) == 0))
    def _init_rhs():
        for u in range(SSD_UNROLL):
            rhs_scr[u] = rhs_const_ref[...]

    a_neg = -jnp.exp(adt_ref[0]) * LOG2E
    dtv = jax.nn.softplus(dt_ref[...] + adt_ref[1][None])
    dtv_scr[...] = dtv
    da2 = (dtv * a_neg[None]).reshape(nc * DT_ROWS, L)
    ri = lax.broadcasted_iota(jnp.int32, (L, L), 0)
    ci = lax.broadcasted_iota(jnp.int32, (L, L), 1)
    upper = (ri <= ci).astype(BF16)
    lower = (ri >= ci).astype(BF16)
    tri = jnp.concatenate([upper, lower], axis=1)
    cs_fb = sum(_dot(p, tri) for p in _split3(da2))
    rowsel = (ri & hpg) == 0
    cs2 = jnp.where(rowsel, cs_fb[:, 0:L], cs_fb[:, L:2 * L])
    colcs = cs2.T

    col_parts = jnp.concatenate(_split3(colcs), axis=1)
    half_n = place_ref.shape[1] // 2
    for hf in range(2):
        cols = slice(hf * half_n, (hf + 1) * half_n)
        lhs_half = _dot(col_parts, place_ref[:, cols]) + lhs_ones_ref[:, cols]
        for c in range(nc // 2):
            lhs_scr[hf * (nc // 2) + c] = lhs_half[:, c * L:c * L + SEG_K].astype(BF16)
    for i, part in enumerate(_split3(-cs2)):
        rpart_scr[i] = part.astype(F32).reshape(nc, DT_ROWS, L)

    def lane_bcast(col):
        return jnp.broadcast_to(col, (col.shape[0], L))

    tot2 = jnp.where(rowsel, lane_bcast(cs2[:, L - 1:L]), lane_bcast(cs2[:, 0:1]))
    dec_scr[...] = jnp.exp2(tot2).reshape(nc, DT_ROWS, L)
    w_scr[...] = (dtv.reshape(nc * DT_ROWS, L) * jnp.exp2(tot2 - cs2)).reshape(nc, DT_ROWS, L)

    lane_t = lax.broadcasted_iota(jnp.int32, (L, L), 1)
    low_half = lane_t < P

    cw_halves = (cwx_ref[...], jnp.concatenate([cwb_ref[...], cwc_ref[...]], axis=1))
    cb_halves = (cbx_ref[...], jnp.concatenate([cbb_ref[...], cbc_ref[...]], axis=1))

    def rows(hf, start, size):
        if hf == 0:
            return x_ref[pl.ds(start, size), :]
        return jnp.concatenate([b_ref[pl.ds(start, size), :], c_ref[pl.ds(start, size), :]], axis=1)

    def conv_stage(c):
        base = pl.multiple_of(c * L, L)
        ws = pl.multiple_of(jnp.clip(c * L - CONV_WIN_LEAD, 0, seq - CONV_WIN), CONV_WIN_LEAD)
        variant = jnp.where(c == 0, 0, jnp.where(c == nc - 1, 2, 1))
        sh = sh_ref[variant]
        halves = []
        for hf in range(2):
            cw = cw_halves[hf]
            shifted = _dot(sh, rows(hf, ws, CONV_WIN))
            acc = cb_halves[hf] + cw[SSM_CONV // 2:SSM_CONV // 2 + 1, :] * rows(hf, base, L).astype(F32)
            for t, kk in enumerate(CONV_SHIFTED_TAPS):
                acc = acc + cw[kk:kk + 1, :] * shifted[t * L:(t + 1) * L, :]
            halves.append(acc * jax.nn.sigmoid(acc))
        xc = jnp.concatenate(halves, axis=1)
        xc_scr[c] = xc
        for j in range(npair):
            xp = xc[:, pair_cols[j]]
            xm_scr[c, j] = jnp.concatenate([jnp.where(low_half, xp, 0.0), jnp.where(low_half, 0.0, xp)],
                                           axis=0).astype(BF16)

    def state_loads(c):
        return xc_scr[c, :, b_cols], [xm_scr[c, j] for j in range(npair)]

    def state_stage(c, loaded):
        b_tok, xm = loaded
        bt = b_tok.T
        for d, ns_scr in ((0, nsf_scr), (1, nsb_scr)):
            w = w_scr[c, d * hpg:(d + 1) * hpg, :]
            ns_scr[c] = jnp.concatenate(
                [_dot(jnp.concatenate([(bt * w[2 * j:2 * j + 1, :]).astype(BF16),
                                       (bt * w[2 * j + 1:2 * j + 2, :]).astype(BF16)], axis=1), xm[j])
                 for j in range(npair)], axis=1)

    for c in range(CONV_AHEAD):
        conv_stage(c)

    def pass_a(c, carry):
        loaded = state_loads(c)
        conv_stage(c + CONV_AHEAD)
        state_stage(c, loaded)
        return carry
    lax.fori_loop(0, nc - CONV_AHEAD, pass_a, 0, unroll=PASS_A_UNROLL)
    for c in range(nc - CONV_AHEAD, nc):
        state_stage(c, state_loads(c))

    lane_r = lax.broadcasted_iota(jnp.int32, (1, L), 1)

    def decay_row(c, d):
        dec = dec_scr[c, d * hpg:(d + 1) * hpg, :]
        return jnp.concatenate([jnp.where(lane_r < P, dec[2 * j:2 * j + 1, :], dec[2 * j + 1:2 * j + 2, :])
                                for j in range(npair)], axis=1)

    def pass_b(d, ns_scr):
        def body(i, st):
            c = i if d == 0 else nc - 1 - i
            new = ns_scr[c]
            ns_scr[c] = st
            return st * decay_row(c, d) + new
        lax.fori_loop(0, nc, body, jnp.zeros((SSM_STATE, GROUP_INNER), F32), unroll=True)
    pass_b(0, nsf_scr)
    pass_b(1, nsb_scr)

    mask_f = ci <= ri
    mask_b = ci >= ri
    neg_inf = jnp.float32(-jnp.inf)
    sub16 = lax.broadcasted_iota(jnp.int32, (BF16_ROWS, L), 0)

    def pass_c(c, carry):
        base = pl.multiple_of(c * L, L)
        xc = xc_scr[c]
        bm = xc[:, b_cols].astype(BF16)
        cm = xc[:, c_cols].astype(BF16)
        cb = _dot_nt(cm, bm)
        off_f = _dot(cm, nsf_scr[c].astype(BF16))
        off_b = _dot(cm, nsb_scr[c].astype(BF16))
        slot = c % SSD_UNROLL
        for hd in range(DT_ROWS):
            r0 = (hd % 2) * SEG_K_PER_ROW
            blk = jnp.where((sub16 >= r0) & (sub16 < r0 + 3), 1.0, 0.0)
            for i in range(3):
                blk = jnp.where(sub16 == r0 + 3 + i, rpart_scr[i, c, hd:hd + 1, :], blk)
            rhs_scr[slot, (hd // 2) * BF16_ROWS:(hd // 2 + 1) * BF16_ROWS, hd * L:(hd + 1) * L] = blk.astype(BF16)
        seg = _dot(lhs_scr[c], rhs_scr[slot])
        dt_f_r = dtv_scr[c, 0:hpg, :]
        dt_b_r = dtv_scr[c, hpg:DT_ROWS, :]
        ys = []
        for j in range(npair):
            ws_ = []
            for h in (2 * j, 2 * j + 1):
                ef = jnp.exp2(jnp.where(mask_f, seg[:, h * L:(h + 1) * L], neg_inf))
                eb = jnp.exp2(jnp.where(mask_b, seg[:, (hpg + h) * L:(hpg + h + 1) * L], neg_inf))
                ws_.append((cb * (ef * dt_f_r[h:h + 1, :] + eb * dt_b_r[h:h + 1, :])).astype(BF16))
            scale_f = jnp.exp2(seg[:, (DT_ROWS + j) * L:(DT_ROWS + j + 1) * L])
            scale_b = jnp.exp2(seg[:, (DT_ROWS + npair + j) * L:(DT_ROWS + npair + j + 1) * L])
            ys.append(_dot(jnp.concatenate(ws_, axis=1), xm_scr[c, j])
                      + scale_f * off_f[:, pair_cols[j]] + scale_b * off_b[:, pair_cols[j]])
        y = jnp.concatenate(ys, axis=1) + dskip_ref[...] * xc[:, x_cols]
        o_ref[pl.ds(base, L), :] = y
        return carry
    lax.fori_loop(0, nc, pass_c, 0, unroll=SSD_UNROLL)


def _conv_shift_matrices():
    out = np.zeros((3, len(CONV_SHIFTED_TAPS) * SSM_CHUNK, CONV_WIN), np.float32)
    for v, lead in enumerate((0, CONV_WIN_LEAD, CONV_WIN - SSM_CHUNK)):
        for i, kk in enumerate(CONV_SHIFTED_TAPS):
            for t in range(SSM_CHUNK):
                j = t + kk - SSM_CONV // 2 + lead
                if 0 <= j < CONV_WIN:
                    out[v, i * SSM_CHUNK + t, j] = 1.0
    return out


def _seg_matmul_constants(nc):
    L, K, half = SSM_CHUNK, SEG_K_PER_ROW, SSM_HEAD_DIM
    place = np.zeros((3 * L, nc * L), np.float32)
    lhs_ones = np.zeros((1, nc * L), np.float32)
    for c in range(nc):
        for hd in range(DT_ROWS):
            for i in range(3):
                place[i * L + c * DT_ROWS + hd, c * L + hd * K + i] = 1.0
                lhs_ones[0, c * L + hd * K + 3 + i] = 1.0
    rhs = np.zeros((SEG_K, SEG_BLOCKS * L), np.float32)
    for hd in range(DT_ROWS):
        d, h = divmod(hd, HEADS_PER_GROUP)
        pair_block = DT_ROWS + d * (HEADS_PER_GROUP // 2) + h // 2
        lanes = slice(0, half) if h % 2 == 0 else slice(half, L)
        rhs[hd * K:hd * K + 3, hd * L:(hd + 1) * L] = 1.0
        rhs[hd * K:hd * K + 3, pair_block * L:(pair_block + 1) * L][:, lanes] = 1.0
    return place, lhs_ones, rhs


def _ssd_call(xbc, dtc, cw, cbias, adt, dskip, batch, seq):
    nc = seq // SSM_CHUNK
    L = SSM_CHUNK
    assert nc >= 3 and seq >= CONV_WIN and nc * DT_ROWS == L
    sh = jnp.asarray(_conv_shift_matrices(), BF16)
    place, lhs_ones, rhs_const = _seg_matmul_constants(nc)
    place = jnp.asarray(place, BF16)
    lhs_ones = jnp.asarray(lhs_ones, F32)
    rhs_const = jnp.asarray(rhs_const, BF16)
    const = lambda a: pl.BlockSpec(a.shape, lambda b, g: (0,) * a.ndim)
    b_blk0 = SSM_INNER // SSM_STATE
    c_blk0 = b_blk0 + SSM_GROUPS

    def xbc_views(rows_, batched):
        lead = (lambda b: b) if batched else (lambda b: 0)
        return [pl.BlockSpec((rows_, GROUP_INNER), lambda b, g: (lead(b), g)),
                pl.BlockSpec((rows_, SSM_STATE), lambda b, g: (lead(b), b_blk0 + g)),
                pl.BlockSpec((rows_, SSM_STATE), lambda b, g: (lead(b), c_blk0 + g))]
    return pl.pallas_call(
        _ssd_kernel,
        grid=(batch, SSM_GROUPS),
        in_specs=xbc_views(seq, True) + [
                  pl.BlockSpec((nc, DT_ROWS, L), lambda b, g: (b, g, 0)),
                  const(sh), const(place), const(lhs_ones), const(rhs_const)]
                 + xbc_views(SUBLANES, False) + xbc_views(1, False) + [
                  pl.BlockSpec((2, DT_ROWS, L), lambda b, g: (0, g, 0)),
                  pl.BlockSpec((1, GROUP_INNER), lambda b, g: (0, g))],
        out_specs=pl.BlockSpec((seq, GROUP_INNER), lambda b, g: (b, g)),
        out_shape=jax.ShapeDtypeStruct((batch * seq, SSM_INNER), F32),
        scratch_shapes=[pltpu.VMEM((nc, L, GROUP_COLS), F32),
                        pltpu.VMEM((nc, HEADS_PER_GROUP // 2, 2 * L, L), BF16),
                        pltpu.VMEM((nc, SSM_STATE, GROUP_INNER), F32),
                        pltpu.VMEM((nc, SSM_STATE, GROUP_INNER), F32),
                        pltpu.VMEM((nc, DT_ROWS, L), F32),
                        pltpu.VMEM((nc, DT_ROWS, L), F32),
                        pltpu.VMEM((nc, DT_ROWS, L), F32),
                        pltpu.VMEM((nc, L, SEG_K), BF16),
                        pltpu.VMEM((SSD_UNROLL, SEG_K, SEG_BLOCKS * L), BF16),
                        pltpu.VMEM((3, nc, DT_ROWS, L), F32)],
        compiler_params=pltpu.CompilerParams(dimension_semantics=("arbitrary", "arbitrary"),
                                             vmem_limit_bytes=VMEM_LIMIT["ssd"]),
        name="ssd",
    )(xbc, xbc, xbc, dtc, sh, place, lhs_ones, rhs_const, cw, cw, cw, cbias, cbias, cbias,
      adt, dskip)


def _mlp_kernel(x_ref, attn_ref, ssm_ref, z_ref, gat_ref, gn_ref, wo_ref, gm_ref, wup_ref, wdn_ref, o_ref):
    for sub in range(x_ref.shape[0] // MLP_SUBTILE):
        rs = slice(sub * MLP_SUBTILE, (sub + 1) * MLP_SUBTILE)
        a = attn_ref[rs, :]
        an = (a * _inv_rms(a) * gat_ref[...]).astype(BF16)
        zc = z_ref[rs, :].astype(F32)
        y = ssm_ref[rs, :] * (zc * jax.nn.sigmoid(zc))
        yn = jnp.concatenate([y[:, g * GROUP_INNER:(g + 1) * GROUP_INNER]
                              * _inv_rms(y[:, g * GROUP_INNER:(g + 1) * GROUP_INNER]) for g in range(SSM_GROUPS)],
                             axis=1) * gn_ref[...]
        mix = _dot(an, wo_ref[0:ATTN_WIDTH, :]) + _dot(yn.astype(BF16), wo_ref[ATTN_WIDTH:D_MIX, :])
        x1 = x_ref[rs, :] + mix
        hm = (x1 * _inv_rms(x1) * gm_ref[...]).astype(BF16)
        acc = jnp.zeros_like(x1)
        for c in range(D_FF // FF_CHUNK):
            cols = slice(c * FF_CHUNK, (c + 1) * FF_CHUNK)
            u = _dot(hm, wup_ref[:, cols])
            acc = acc + _dot(jnp.square(jnp.maximum(u, 0.0)).astype(BF16), wdn_ref[cols, :])
        o_ref[rs, :] = x1 + acc


def _mlp_call(x2, attn, ssm, z, gat, gn, wo, gm, wup, wdn):
    t = x2.shape[0]
    row = lambda w: pl.BlockSpec((TM_MLP, w), lambda i: (i, 0))
    full = lambda a: pl.BlockSpec(a.shape, lambda i: (0, 0), pipeline_mode=pl.Buffered(1))
    return pl.pallas_call(
        _mlp_kernel,
        grid=(t // TM_MLP,),
        in_specs=[row(D_MODEL), row(ATTN_WIDTH), row(SSM_INNER), row(SSM_INNER), full(gat), full(gn), full(wo),
                  full(gm), full(wup), full(wdn)],
        out_specs=row(D_MODEL),
        out_shape=jax.ShapeDtypeStruct((t, D_MODEL), F32),
        compiler_params=pltpu.CompilerParams(dimension_semantics=("arbitrary",),
                                             vmem_limit_bytes=VMEM_LIMIT["outproj_mlp"]),
        name="outproj_mlp",
    )(x2, attn, ssm, z, gat, gn, wo, gm, wup, wdn)


def _swap_halves(g):
    half = QK_ROPE_DIM // 2
    return jnp.concatenate([g[..., half:], g[..., :half]], axis=-1)


def _lane_bcast(v):
    return jnp.broadcast_to(v[..., None], v.shape + (LANES,)).astype(F32)


def _layer(x2, cs, batch, seq, ln_mix_g, stacked_proj_weights, q_a_norm_g, kv_a_norm_g, q_norm_g,
           k_norm_g, attn_out_norm_g, conv_w, conv_b, a_log_fwd, a_log_bwd, dt_bias_fwd, dt_bias_bwd,
           d_skip, ssm_norm_g, w_out, ln_mlp_g, w_mlp_up, w_mlp_down):
    win, wuq, wuk, wuv = _win_prep_call(*stacked_proj_weights)
    ones_v = np.ones((ATTN_HEADS // 2, V_HEAD_DIM), np.float32)
    vone = jnp.asarray(np.stack([np.concatenate([0 * ones_v, ones_v], axis=-1),
                                 np.concatenate([ones_v, 0 * ones_v], axis=-1)], axis=1).reshape(1, -1))
    scale = QK_HEAD_DIM ** -0.5 * np.log2(np.e)
    zeros_nope = jnp.zeros((QK_NOPE_DIM,), k_norm_g.dtype)
    k_rope_g = k_norm_g[QK_NOPE_DIM:]
    hg = jnp.concatenate([q_norm_g * scale, _swap_halves(q_norm_g[QK_NOPE_DIM:]) * scale,
                          k_norm_g[:QK_NOPE_DIM], zeros_nope,
                          zeros_nope, k_rope_g, _swap_halves(k_rope_g),
                          jnp.zeros(((SUBLANES - 3) * HEAD_PAD,), k_norm_g.dtype)]).reshape(SUBLANES, HEAD_PAD)

    q, k, v, z, xbc, dtc = _inproj_call(
        x2, cs, ln_mix_g[None, :], win, q_a_norm_g[None, :], wuq, kv_a_norm_g[None, :], wuk, wuv, vone,
        hg)

    attn, (wo_b, wup_b, wdn_b) = _attn_call(q, k, v, (w_out, w_mlp_up, w_mlp_down), batch, seq)

    cw = jnp.pad(conv_w[:, 0, :], ((0, SUBLANES - SSM_CONV), (0, 0)))
    cbias = conv_b[None, :]
    adt = jnp.stack([a_log_fwd, a_log_bwd, dt_bias_fwd, dt_bias_bwd]).reshape(2, 2, SSM_GROUPS, HEADS_PER_GROUP)
    adt = _lane_bcast(adt.transpose(0, 2, 1, 3).reshape(2, 2 * SSM_HEADS))
    dskip = jnp.repeat(d_skip, SSM_HEAD_DIM)[None, :]
    ssm = _ssd_call(xbc, dtc, cw, cbias, adt, dskip, batch, seq)

    return _mlp_call(x2, attn, ssm, z, attn_out_norm_g[None, :], ssm_norm_g[None, :], wo_b,
                     ln_mlp_g[None, :], wup_b, wdn_b)


def _rope_table(positions):
    inv_freq = 1.0 / (ROPE_THETA ** (jnp.arange(0, QK_ROPE_DIM, 2, dtype=F32) / QK_ROPE_DIM))
    ang = inv_freq[:, None] * positions.astype(F32).reshape(1, -1)
    return jnp.concatenate([jnp.cos(ang), jnp.sin(ang)], axis=0)


def kernel(x, positions, ln_mix_g, w_in, q_a_norm_g, w_uq, kv_a_norm_g, w_ukv, q_norm_g, k_norm_g,
           attn_out_norm_g, conv_w, conv_b, a_log_fwd, a_log_bwd, dt_bias_fwd, dt_bias_bwd, d_skip,
           ssm_norm_g, w_out, ln_mlp_g, w_mlp_up, w_mlp_down):
    batch, seq, d = x.shape
    assert d == D_MODEL and seq % TQ == 0 and (batch * seq) % TM_IN == 0 and (batch * seq) % TM_MLP == 0
    cs = _rope_table(positions)
    x2 = x.reshape(batch * seq, d)
    for l in range(ln_mix_g.shape[0]):
        x2 = _layer(x2, cs, batch, seq, ln_mix_g[l], (w_in, w_uq, w_ukv, l), q_a_norm_g[l], kv_a_norm_g[l],
                    q_norm_g[l], k_norm_g[l], attn_out_norm_g[l], conv_w[l], conv_b[l],
                    a_log_fwd[l], a_log_bwd[l], dt_bias_fwd[l], dt_bias_bwd[l], d_skip[l], ssm_norm_g[l],
                    w_out[l], ln_mlp_g[l], w_mlp_up[l], w_mlp_down[l])
    return x2.reshape(batch, seq, d)
```

```python
import numpy as np
import jax
import jax.numpy as jnp
from jax import lax
from jax.experimental import pallas as pl
from jax.experimental.pallas import tpu as pltpu

F32 = jnp.float32
BF16 = jnp.bfloat16

D_MODEL = 1024
ATTN_HEADS = 8
QK_NOPE_DIM = 64
QK_ROPE_DIM = 32
QK_HEAD_DIM = QK_NOPE_DIM + QK_ROPE_DIM
V_HEAD_DIM = 64
Q_LORA_RANK = D_MODEL // 4
KV_LORA_RANK = D_MODEL // 8
ROPE_THETA = 10000.0
ATTN_WIDTH = ATTN_HEADS * V_HEAD_DIM
SSM_HEADS = 8
SSM_HEAD_DIM = 64
SSM_INNER = SSM_HEADS * SSM_HEAD_DIM
SSM_GROUPS = 2
SSM_STATE = 128
SSM_CONV = 5
SSM_CHUNK = 128
SSM_CONV_CH = SSM_INNER + 2 * SSM_GROUPS * SSM_STATE
D_MIX = ATTN_WIDTH + SSM_INNER
D_FF = 4 * D_MODEL
EPS = 1e-6

LANES = 128
SUBLANES = 8
HEAD_PAD = LANES

HEADS_PER_GROUP = SSM_HEADS // SSM_GROUPS
GROUP_INNER = SSM_INNER // SSM_GROUPS
GROUP_COLS = GROUP_INNER + 2 * SSM_STATE
DT_ROWS = 2 * HEADS_PER_GROUP
BF16_ROWS = 16
SEG_K_PER_ROW = 8
SEG_K = DT_ROWS * SEG_K_PER_ROW
SEG_BLOCKS = DT_ROWS + HEADS_PER_GROUP
LOG2E = float(np.log2(np.e))
CONV_WIN = 2 * SSM_CHUNK
CONV_WIN_LEAD = SSM_CHUNK // 2
CONV_SHIFTED_TAPS = tuple(k for k in range(SSM_CONV) if k != SSM_CONV // 2)

COL_CKV = Q_LORA_RANK
COL_MISC = COL_CKV + KV_LORA_RANK
COL_Z = COL_MISC + LANES
COL_XBC = COL_Z + SSM_INNER
IN_COLS = COL_XBC + SSM_CONV_CH

TM_IN = 1024
IN_SUBTILE = 512
TQ = 1024
ATTN_HEADS_PER_STEP = 8
TM_MLP = 1024
MLP_SUBTILE = 512
FF_CHUNK = 1024
SSD_UNROLL = 16
CONV_AHEAD = 4
PASS_A_UNROLL = 12
WIN_PREP_STEPS = 4
VMEM_LIMIT = 56 * 1024 * 1024
VMEM_LIMIT_WIN_PREP = 16 * 1024 * 1024
VMEM_LIMIT_SSD = 32 * 1024 * 1024


def _inv_rms(x):
    n = x.shape[-1]
    x2 = x * x
    acc = x2[:, 0:LANES]
    for i in range(1, n // LANES):
        acc = acc + x2[:, i * LANES:(i + 1) * LANES]
    return lax.rsqrt(jnp.sum(acc, axis=-1, keepdims=True) * (1.0 / n) + EPS)


def _dot(a, b):
    return jnp.dot(a, b, preferred_element_type=F32)


def _dot_nt(a, b):
    return lax.dot_general(a, b, (((1,), (1,)), ((), ())), preferred_element_type=F32)


def _inproj_kernel(x_ref, cs_ref, g_ref, win_ref, gqa_ref, wuq_ref, gkva_ref, wuk_ref, wuv_ref, vone_ref,
                   hg_ref,
                   q_ref, k_ref, v_ref, z_ref, xbc_ref, dt_ref):
    for sub in range(x_ref.shape[0] // IN_SUBTILE):
        _inproj_rows(sub * IN_SUBTILE, x_ref, cs_ref, g_ref, win_ref, gqa_ref, wuq_ref, gkva_ref, wuk_ref,
                     wuv_ref, vone_ref, hg_ref, q_ref, k_ref, v_ref, z_ref, xbc_ref, dt_ref)


def _inproj_rows(r0, x_ref, cs_ref, g_ref, win_ref, gqa_ref, wuq_ref, gkva_ref, wuk_ref, wuv_ref, vone_ref,
                 hg_ref, q_ref, k_ref, v_ref, z_ref, xbc_ref, dt_ref):
    tm = IN_SUBTILE
    rs = slice(r0, r0 + tm)
    x = x_ref[rs, :]
    h = (x * _inv_rms(x) * g_ref[...]).astype(BF16)
    big = _dot(h, win_ref[...])
    z_ref[rs, :] = big[:, COL_Z:COL_XBC].astype(BF16)
    xbc_ref[rs, :] = big[:, COL_XBC:IN_COLS].astype(BF16)
    misc = big[:, COL_MISC:COL_Z]
    for c in range(tm // SSM_CHUNK):
        dt_ref[r0 // SSM_CHUNK + c] = misc[c * SSM_CHUNK:(c + 1) * SSM_CHUNK, :].T[0:2 * SSM_HEADS, :]

    cq = big[:, 0:COL_CKV]
    ckv = big[:, COL_CKV:COL_MISC]
    cqn = (cq * _inv_rms(cq) * gqa_ref[...]).astype(BF16)
    ckvn = (ckv * _inv_rms(ckv) * gkva_ref[...]).astype(BF16)
    q_pre = _dot(cqn, wuq_ref[...])
    k_pre = _dot(ckvn, wuk_ref[...])
    v_ref[rs, :] = (_dot(ckvn, wuv_ref[...]) + vone_ref[...]).astype(BF16)

    half = QK_ROPE_DIM // 2
    zpad = jnp.zeros((HEAD_PAD - QK_ROPE_DIM, SSM_CHUNK), F32)
    tt = jnp.concatenate(
        [jnp.concatenate([cs_ref[:, r0 + c * SSM_CHUNK:r0 + (c + 1) * SSM_CHUNK], zpad], axis=0).T
         for c in range(tm // SSM_CHUNK)], axis=0)
    lane_t = lax.broadcasted_iota(jnp.int32, (tm, HEAD_PAD), 1)
    cs = jnp.where(lane_t < QK_NOPE_DIM, 1.0,
                   jnp.where(lane_t < QK_NOPE_DIM + half, pltpu.roll(tt, QK_NOPE_DIM, 1),
                             jnp.where(lane_t < QK_HEAD_DIM + half, pltpu.roll(tt, QK_NOPE_DIM + half, 1),
                                       pltpu.roll(tt, QK_HEAD_DIM, 1))))
    lane = lax.broadcasted_iota(jnp.int32, (1, HEAD_PAD), 1)
    in_head = (lane < QK_HEAD_DIM).astype(F32)
    is_rope = ((lane >= QK_NOPE_DIM) & (lane < QK_HEAD_DIM)).astype(F32)
    inv_d = 1.0 / QK_HEAD_DIM

    gcq = hg_ref[0:1, :] * cs
    for hh in range(ATTN_HEADS):
        sl = slice(hh * HEAD_PAD, (hh + 1) * HEAD_PAD)
        qh = q_pre[:, sl]
        ssq = jnp.sum(qh * qh * in_head, axis=-1, keepdims=True)
        q_ref[rs, sl] = (qh * lax.rsqrt(ssq * inv_d + EPS) * gcq).astype(BF16)

    ab = misc * (hg_ref[2:3, :] * cs)
    lane2 = lax.broadcasted_iota(jnp.int32, (tm, HEAD_PAD), 1)
    swapped = jnp.where(lane2 < QK_HEAD_DIM, pltpu.roll(ab, HEAD_PAD - QK_ROPE_DIM, 1),
                        pltpu.roll(ab, QK_ROPE_DIM, 1))
    s_both = jnp.where(lane2 >= QK_NOPE_DIM, ab + swapped, 0.0)
    ssq_pe = jnp.sum(misc * misc * is_rope, axis=-1, keepdims=True)
    gkn = hg_ref[1:2, :]
    for hh in range(ATTN_HEADS):
        sl = slice(hh * HEAD_PAD, (hh + 1) * HEAD_PAD)
        kh = k_pre[:, sl]
        ssq = jnp.sum(kh * kh, axis=-1, keepdims=True) + ssq_pe
        k_ref[rs, sl] = ((kh * gkn + s_both) * lax.rsqrt(ssq * inv_d + EPS)).astype(BF16)


def _win_prep_kernel(wt_ref, wq_ref, wkv_ref, o_ref, wuq_ref, wuk_ref, wuv_ref):
    half = QK_ROPE_DIM // 2

    @pl.when(pl.program_id(0) == 0)
    def _up_projection_tiles():
        wq = wq_ref[...]
        q_pieces = []
        for hh in range(ATTN_HEADS):
            head = wq[:, hh * QK_HEAD_DIM:(hh + 1) * QK_HEAD_DIM]
            rope = head[:, QK_NOPE_DIM:]
            q_pieces += [head, -rope[:, half:], rope[:, :half]]
        wuq_ref[...] = jnp.concatenate(q_pieces, axis=1).astype(BF16)
        wkv = wkv_ref[...]
        zeros = jnp.zeros((wkv.shape[0], V_HEAD_DIM), F32)
        k_pieces, v_pieces = [], []
        for hh in range(ATTN_HEADS):
            nope = wkv[:, hh * HEAD_PAD:hh * HEAD_PAD + QK_NOPE_DIM]
            val = wkv[:, hh * HEAD_PAD + QK_NOPE_DIM:(hh + 1) * HEAD_PAD]
            k_pieces += [nope, zeros]
            v_pieces += [val, zeros] if hh % 2 == 0 else [zeros, val]
        wuk_ref[...] = jnp.concatenate(k_pieces, axis=1).astype(BF16)
        wuv_ref[...] = jnp.concatenate(v_pieces, axis=1).astype(BF16)

    cols = wt_ref.shape[1]
    o_kpe = Q_LORA_RANK + KV_LORA_RANK
    o_z = o_kpe + QK_ROPE_DIM
    o_dt = o_z + SSM_INNER + SSM_CONV_CH
    hpg = HEADS_PER_GROUP
    assert SSM_HEADS == SUBLANES and SSM_GROUPS == 2
    dt_f = wt_ref[o_dt:o_dt + SSM_HEADS, :]
    dt_b = wt_ref[o_dt + SSM_HEADS:o_dt + 2 * SSM_HEADS, :]
    top = lax.broadcasted_iota(jnp.int32, (SUBLANES, cols), 0) < hpg
    dt_g0 = jnp.where(top, dt_f, pltpu.roll(dt_b, hpg, 0))
    dt_g1 = jnp.where(top, pltpu.roll(dt_f, hpg, 0), dt_b)
    kpe = wt_ref[o_kpe:o_z, :]
    rows = jnp.concatenate(
        [wt_ref[0:o_kpe, :], dt_g0, dt_g1, jnp.zeros((QK_NOPE_DIM - 2 * SSM_HEADS, cols), F32),
         kpe, -kpe[half:, :], kpe[:half, :], wt_ref[o_z:o_dt, :]], axis=0)
    for ct in range(cols // LANES):
        o_ref[ct * LANES:(ct + 1) * LANES, :] = jnp.concatenate(
            [rows[rt * LANES:(rt + 1) * LANES, ct * LANES:(ct + 1) * LANES].T for rt in range(IN_COLS // LANES)],
            axis=1).astype(BF16)


def _win_prep_call(w_in3, w_uq3, w_ukv3, layer):
    depth, d, n = w_in3.shape
    cols = d // WIN_PREP_STEPS
    whole = lambda a: pl.BlockSpec(a.shape[1:], lambda i: (layer, 0))
    head_cols = ATTN_HEADS * HEAD_PAD
    return pl.pallas_call(
        _win_prep_kernel,
        grid=(WIN_PREP_STEPS,),
        in_specs=[pl.BlockSpec((n, cols), lambda i: (layer, i)), whole(w_uq3), whole(w_ukv3)],
        out_specs=[pl.BlockSpec((cols, IN_COLS), lambda i: (i, 0)),
                   pl.BlockSpec((Q_LORA_RANK, head_cols), lambda i: (0, 0)),
                   pl.BlockSpec((KV_LORA_RANK, head_cols), lambda i: (0, 0)),
                   pl.BlockSpec((KV_LORA_RANK, head_cols), lambda i: (0, 0))],
        out_shape=[jax.ShapeDtypeStruct((d, IN_COLS), BF16),
                   jax.ShapeDtypeStruct((Q_LORA_RANK, head_cols), BF16),
                   jax.ShapeDtypeStruct((KV_LORA_RANK, head_cols), BF16),
                   jax.ShapeDtypeStruct((KV_LORA_RANK, head_cols), BF16)],
        compiler_params=pltpu.CompilerParams(dimension_semantics=("arbitrary",),
                                             vmem_limit_bytes=VMEM_LIMIT_WIN_PREP),
        name="win_prep",
    )(jnp.transpose(w_in3, (0, 2, 1)).reshape(depth * n, d),
      w_uq3.reshape(depth * w_uq3.shape[1], w_uq3.shape[2]),
      w_ukv3.reshape(depth * w_ukv3.shape[1], w_ukv3.shape[2]))


def _inproj_call(x2, cs, g, win, gqa, wuq, gkva, wuk, wuv, vone, hg):
    t = x2.shape[0]
    cpt = TM_IN // SSM_CHUNK
    full = lambda a: pl.BlockSpec(a.shape, lambda i: (0,) * a.ndim, pipeline_mode=pl.Buffered(1))
    row = lambda w: pl.BlockSpec((TM_IN, w), lambda i: (i, 0))
    return pl.pallas_call(
        _inproj_kernel,
        grid=(t // TM_IN,),
        in_specs=[row(D_MODEL), pl.BlockSpec((QK_ROPE_DIM, TM_IN), lambda i: (0, i)), full(g), full(win), full(gqa), full(wuq), full(gkva),
                  full(wuk), full(wuv), full(vone), full(hg)],
        out_specs=[row(ATTN_HEADS * HEAD_PAD), row(ATTN_HEADS * HEAD_PAD), row(ATTN_HEADS * HEAD_PAD),
                   row(SSM_INNER), row(SSM_CONV_CH),
                   pl.BlockSpec((cpt, 2 * SSM_HEADS, SSM_CHUNK), lambda i: (i, 0, 0))],
        out_shape=[jax.ShapeDtypeStruct((t, ATTN_HEADS * HEAD_PAD), BF16),
                   jax.ShapeDtypeStruct((t, ATTN_HEADS * HEAD_PAD), BF16),
                   jax.ShapeDtypeStruct((t, ATTN_HEADS * HEAD_PAD), BF16),
                   jax.ShapeDtypeStruct((t, SSM_INNER), BF16),
                   jax.ShapeDtypeStruct((t, SSM_CONV_CH), BF16),
                   jax.ShapeDtypeStruct((t // SSM_CHUNK, 2 * SSM_HEADS, SSM_CHUNK), F32)],
        compiler_params=pltpu.CompilerParams(dimension_semantics=("arbitrary",),
                                             vmem_limit_bytes=VMEM_LIMIT),
        name="inproj",
    )(x2, cs, g, win, gqa, wuq, gkva, wuk, wuv, vone, hg)


def _attn_kernel(q_ref, k_ref, v_ref, *refs):
    n_w = (len(refs) - 1) // 2
    w_in_refs, o_ref, w_out_refs = refs[:n_w], refs[n_w], refs[n_w + 1:]
    for w_in_ref, w_out_ref in zip(w_in_refs, w_out_refs):
        w_out_ref[...] = w_in_ref[...].astype(BF16)

    lane = lax.broadcasted_iota(jnp.int32, (q_ref.shape[0], HEAD_PAD), 1)
    for jp in range(ATTN_HEADS_PER_STEP // 2):
        sls = [slice(j * HEAD_PAD, (j + 1) * HEAD_PAD) for j in (2 * jp, 2 * jp + 1)]
        ss = [_dot_nt(q_ref[:, sl], k_ref[:, sl]) for sl in sls]
        ps = [jnp.exp2(s - jnp.max(s, axis=-1, keepdims=True)).astype(BF16) for s in ss]
        accs = [_dot(p, v_ref[:, sl]) for p, sl in zip(ps, sls)]
        res = [acc / pltpu.roll(acc, V_HEAD_DIM, 1) for acc in accs]
        o_ref[:, jp * HEAD_PAD:(jp + 1) * HEAD_PAD] = jnp.where(lane < V_HEAD_DIM, res[0], res[1])


def _attn_call(q, k, v, later_weights, batch, seq):
    n_q = seq // TQ
    hps = ATTN_HEADS_PER_STEP
    assert ATTN_HEADS == hps
    n_steps = batch * n_q

    def rows_spec(w):
        assert w.shape[0] % (n_steps * BF16_ROWS) == 0
        return pl.BlockSpec((w.shape[0] // n_steps, w.shape[1]), lambda b, hp, i: (b * n_q + i, 0))
    w_specs = [rows_spec(w) for w in later_weights]
    outs = pl.pallas_call(
        _attn_kernel,
        grid=(batch, ATTN_HEADS // hps, n_q),
        in_specs=[pl.BlockSpec((TQ, hps * HEAD_PAD), lambda b, hp, i: (b * n_q + i, hp)),
                  pl.BlockSpec((seq, hps * HEAD_PAD), lambda b, hp, i: (b, hp)),
                  pl.BlockSpec((seq, hps * HEAD_PAD), lambda b, hp, i: (b, hp))] + w_specs,
        out_specs=[pl.BlockSpec((TQ, hps * V_HEAD_DIM), lambda b, hp, i: (b * n_q + i, hp))] + w_specs,
        out_shape=[jax.ShapeDtypeStruct((batch * seq, ATTN_WIDTH), F32)]
                  + [jax.ShapeDtypeStruct(w.shape, BF16) for w in later_weights],
        compiler_params=pltpu.CompilerParams(
            dimension_semantics=("arbitrary", "arbitrary", "arbitrary"),
            vmem_limit_bytes=VMEM_LIMIT),
        name="attention",
    )(q, k, v, *later_weights)
    return outs[0], outs[1:]


def _split3(x):
    hi = x.astype(BF16)
    r1 = x - hi.astype(F32)
    mid = r1.astype(BF16)
    lo = (r1 - mid.astype(F32)).astype(BF16)
    return hi, mid, lo


def _ssd_kernel(x_ref, b_ref, c_ref, dt_ref, sh_ref, place_ref, lhs_ones_ref, rhs_const_ref,
                cwx_ref, cwb_ref, cwc_ref, cbx_ref, cbb_ref, cbc_ref, adt_ref, dskip_ref,
                o_ref, xc_scr, xm_scr, nsf_scr, nsb_scr, dtv_scr, w_scr, dec_scr, lhs_scr, rhs_scr, rpart_scr):
    nc = dt_ref.shape[0]
    seq = x_ref.shape[0]
    L = SSM_CHUNK
    hpg = HEADS_PER_GROUP
    npair = hpg // 2
    P = SSM_HEAD_DIM
    x_cols = slice(0, GROUP_INNER)
    b_cols = slice(GROUP_INNER, GROUP_INNER + SSM_STATE)
    c_cols = slice(GROUP_INNER + SSM_STATE, GROUP_COLS)
    pair_cols = [slice(j * L, (j + 1) * L) for j in range(npair)]

    @pl.when((pl.program_id(0) == 0) & (pl.program_id(1) == 0))
    def _init_rhs():
        for u in range(SSD_UNROLL):
            rhs_scr[u] = rhs_const_ref[...]

    a_neg = -jnp.exp(adt_ref[0]) * LOG2E
    dtv = jax.nn.softplus(dt_ref[...] + adt_ref[1][None])
    dtv_scr[...] = dtv
    da2 = (dtv * a_neg[None]).reshape(nc * DT_ROWS, L)
    ri = lax.broadcasted_iota(jnp.int32, (L, L), 0)
    ci = lax.broadcasted_iota(jnp.int32, (L, L), 1)
    upper = (ri <= ci).astype(BF16)
    lower = (ri >= ci).astype(BF16)
    tri = jnp.concatenate([upper, lower], axis=1)
    cs_fb = sum(_dot(p, tri) for p in _split3(da2))
    rowsel = (ri & hpg) == 0
    cs2 = jnp.where(rowsel, cs_fb[:, 0:L], cs_fb[:, L:2 * L])
    colcs = cs2.T

    col_parts = jnp.concatenate(_split3(colcs), axis=1)
    half_n = place_ref.shape[1] // 2
    for hf in range(2):
        cols = slice(hf * half_n, (hf + 1) * half_n)
        lhs_half = _dot(col_parts, place_ref[:, cols]) + lhs_ones_ref[:, cols]
        for c in range(nc // 2):
            lhs_scr[hf * (nc // 2) + c] = lhs_half[:, c * L:c * L + SEG_K].astype(BF16)
    for i, part in enumerate(_split3(-cs2)):
        rpart_scr[i] = part.astype(F32).reshape(nc, DT_ROWS, L)

    def lane_bcast(col):
        return jnp.broadcast_to(col, (col.shape[0], L))

    tot2 = jnp.where(rowsel, lane_bcast(cs2[:, L - 1:L]), lane_bcast(cs2[:, 0:1]))
    dec_scr[...] = jnp.exp2(tot2).reshape(nc, DT_ROWS, L)
    w_scr[...] = (dtv.reshape(nc * DT_ROWS, L) * jnp.exp2(tot2 - cs2)).reshape(nc, DT_ROWS, L)

    lane_t = lax.broadcasted_iota(jnp.int32, (L, L), 1)
    low_half = lane_t < P

    cw_halves = (cwx_ref[...], jnp.concatenate([cwb_ref[...], cwc_ref[...]], axis=1))
    cb_halves = (cbx_ref[...], jnp.concatenate([cbb_ref[...], cbc_ref[...]], axis=1))

    def rows(hf, start, size):
        if hf == 0:
            return x_ref[pl.ds(start, size), :]
        return jnp.concatenate([b_ref[pl.ds(start, size), :], c_ref[pl.ds(start, size), :]], axis=1)

    def conv_stage(c):
        base = pl.multiple_of(c * L, L)
        ws = pl.multiple_of(jnp.clip(c * L - CONV_WIN_LEAD, 0, seq - CONV_WIN), CONV_WIN_LEAD)
        variant = jnp.where(c == 0, 0, jnp.where(c == nc - 1, 2, 1))
        sh = sh_ref[variant]
        halves = []
        for hf in range(2):
            cw = cw_halves[hf]
            shifted = _dot(sh, rows(hf, ws, CONV_WIN))
            acc = cb_halves[hf] + cw[SSM_CONV // 2:SSM_CONV // 2 + 1, :] * rows(hf, base, L).astype(F32)
            for t, kk in enumerate(CONV_SHIFTED_TAPS):
                acc = acc + cw[kk:kk + 1, :] * shifted[t * L:(t + 1) * L, :]
            halves.append(acc * jax.nn.sigmoid(acc))
        xc = jnp.concatenate(halves, axis=1)
        xc_scr[c] = xc
        for j in range(npair):
            xp = xc[:, pair_cols[j]]
            xm_scr[c, j] = jnp.concatenate([jnp.where(low_half, xp, 0.0), jnp.where(low_half, 0.0, xp)],
                                           axis=0).astype(BF16)

    def state_loads(c):
        return xc_scr[c, :, b_cols], [xm_scr[c, j] for j in range(npair)]

    def state_stage(c, loaded):
        b_tok, xm = loaded
        bt = b_tok.T
        for d, ns_scr in ((0, nsf_scr), (1, nsb_scr)):
            w = w_scr[c, d * hpg:(d + 1) * hpg, :]
            ns_scr[c] = jnp.concatenate(
                [_dot(jnp.concatenate([(bt * w[2 * j:2 * j + 1, :]).astype(BF16),
                                       (bt * w[2 * j + 1:2 * j + 2, :]).astype(BF16)], axis=1), xm[j])
                 for j in range(npair)], axis=1)

    for c in range(CONV_AHEAD):
        conv_stage(c)

    def pass_a(c, carry):
        loaded = state_loads(c)
        conv_stage(c + CONV_AHEAD)
        state_stage(c, loaded)
        return carry
    lax.fori_loop(0, nc - CONV_AHEAD, pass_a, 0, unroll=PASS_A_UNROLL)
    for c in range(nc - CONV_AHEAD, nc):
        state_stage(c, state_loads(c))

    lane_r = lax.broadcasted_iota(jnp.int32, (1, L), 1)

    def decay_row(c, d):
        dec = dec_scr[c, d * hpg:(d + 1) * hpg, :]
        return jnp.concatenate([jnp.where(lane_r < P, dec[2 * j:2 * j + 1, :], dec[2 * j + 1:2 * j + 2, :])
                                for j in range(npair)], axis=1)

    def pass_b(d, ns_scr):
        def body(i, st):
            c = i if d == 0 else nc - 1 - i
            new = ns_scr[c]
            ns_scr[c] = st
            return st * decay_row(c, d) + new
        lax.fori_loop(0, nc, body, jnp.zeros((SSM_STATE, GROUP_INNER), F32), unroll=True)
    pass_b(0, nsf_scr)
    pass_b(1, nsb_scr)

    mask_f = ci <= ri
    mask_b = ci >= ri
    neg_inf = jnp.float32(-jnp.inf)
    sub16 = lax.broadcasted_iota(jnp.int32, (BF16_ROWS, L), 0)

    def pass_c(c, carry):
        base = pl.multiple_of(c * L, L)
        xc = xc_scr[c]
        bm = xc[:, b_cols].astype(BF16)
        cm = xc[:, c_cols].astype(BF16)
        cb = _dot_nt(cm, bm)
        off_f = _dot(cm, nsf_scr[c].astype(BF16))
        off_b = _dot(cm, nsb_scr[c].astype(BF16))
        slot = c % SSD_UNROLL
        for hd in range(DT_ROWS):
            r0 = (hd % 2) * SEG_K_PER_ROW
            blk = jnp.where((sub16 >= r0) & (sub16 < r0 + 3), 1.0, 0.0)
            for i in range(3):
                blk = jnp.where(sub16 == r0 + 3 + i, rpart_scr[i, c, hd:hd + 1, :], blk)
            rhs_scr[slot, (hd // 2) * BF16_ROWS:(hd // 2 + 1) * BF16_ROWS, hd * L:(hd + 1) * L] = blk.astype(BF16)
        seg = _dot(lhs_scr[c], rhs_scr[slot])
        dt_f_r = dtv_scr[c, 0:hpg, :]
        dt_b_r = dtv_scr[c, hpg:DT_ROWS, :]
        ys = []
        for j in range(npair):
            ws_ = []
            for h in (2 * j, 2 * j + 1):
                ef = jnp.exp2(jnp.where(mask_f, seg[:, h * L:(h + 1) * L], neg_inf))
                eb = jnp.exp2(jnp.where(mask_b, seg[:, (hpg + h) * L:(hpg + h + 1) * L], neg_inf))
                ws_.append((cb * (ef * dt_f_r[h:h + 1, :] + eb * dt_b_r[h:h + 1, :])).astype(BF16))
            scale_f = jnp.exp2(seg[:, (DT_ROWS + j) * L:(DT_ROWS + j + 1) * L])
            scale_b = jnp.exp2(seg[:, (DT_ROWS + npair + j) * L:(DT_ROWS + npair + j + 1) * L])
            ys.append(_dot(jnp.concatenate(ws_, axis=1), xm_scr[c, j])
                      + scale_f * off_f[:, pair_cols[j]] + scale_b * off_b[:, pair_cols[j]])
        y = jnp.concatenate(ys, axis=1) + dskip_ref[...] * xc[:, x_cols]
        o_ref[pl.ds(base, L), :] = y
        return carry
    lax.fori_loop(0, nc, pass_c, 0, unroll=SSD_UNROLL)


def _conv_shift_matrices():
    out = np.zeros((3, len(CONV_SHIFTED_TAPS) * SSM_CHUNK, CONV_WIN), np.float32)
    for v, lead in enumerate((0, CONV_WIN_LEAD, CONV_WIN - SSM_CHUNK)):
        for i, kk in enumerate(CONV_SHIFTED_TAPS):
            for t in range(SSM_CHUNK):
                j = t + kk - SSM_CONV // 2 + lead
                if 0 <= j < CONV_WIN:
                    out[v, i * SSM_CHUNK + t, j] = 1.0
    return out


def _seg_matmul_constants(nc):
    L, K, half = SSM_CHUNK, SEG_K_PER_ROW, SSM_HEAD_DIM
    place = np.zeros((3 * L, nc * L), np.float32)
    lhs_ones = np.zeros((1, nc * L), np.float32)
    for c in range(nc):
        for hd in range(DT_ROWS):
            for i in range(3):
                place[i * L + c * DT_ROWS + hd, c * L + hd * K + i] = 1.0
                lhs_ones[0, c * L + hd * K + 3 + i] = 1.0
    rhs = np.zeros((SEG_K, SEG_BLOCKS * L), np.float32)
    for hd in range(DT_ROWS):
        d, h = divmod(hd, HEADS_PER_GROUP)
        pair_block = DT_ROWS + d * (HEADS_PER_GROUP // 2) + h // 2
        lanes = slice(0, half) if h % 2 == 0 else slice(half, L)
        rhs[hd * K:hd * K + 3, hd * L:(hd + 1) * L] = 1.0
        rhs[hd * K:hd * K + 3, pair_block * L:(pair_block + 1) * L][:, lanes] = 1.0
    return place, lhs_ones, rhs


def _ssd_call(xbc, dtc, cw, cbias, adt, dskip, batch, seq):
    nc = seq // SSM_CHUNK
    L = SSM_CHUNK
    assert nc >= 3 and seq >= CONV_WIN and nc * DT_ROWS == L
    sh = jnp.asarray(_conv_shift_matrices(), BF16)
    place, lhs_ones, rhs_const = _seg_matmul_constants(nc)
    place = jnp.asarray(place, BF16)
    lhs_ones = jnp.asarray(lhs_ones, F32)
    rhs_const = jnp.asarray(rhs_const, BF16)
    const = lambda a: pl.BlockSpec(a.shape, lambda b, g: (0,) * a.ndim)
    b_blk0 = SSM_INNER // SSM_STATE
    c_blk0 = b_blk0 + SSM_GROUPS

    def xbc_views(rows_, batched):
        lead = (lambda b: b) if batched else (lambda b: 0)
        return [pl.BlockSpec((rows_, GROUP_INNER), lambda b, g: (lead(b), g)),
                pl.BlockSpec((rows_, SSM_STATE), lambda b, g: (lead(b), b_blk0 + g)),
                pl.BlockSpec((rows_, SSM_STATE), lambda b, g: (lead(b), c_blk0 + g))]
    return pl.pallas_call(
        _ssd_kernel,
        grid=(batch, SSM_GROUPS),
        in_specs=xbc_views(seq, True) + [
                  pl.BlockSpec((nc, DT_ROWS, L), lambda b, g: (b, g, 0)),
                  const(sh), const(place), const(lhs_ones), const(rhs_const)]
                 + xbc_views(SUBLANES, False) + xbc_views(1, False) + [
                  pl.BlockSpec((2, DT_ROWS, L), lambda b, g: (0, g, 0)),
                  pl.BlockSpec((1, GROUP_INNER), lambda b, g: (0, g))],
        out_specs=pl.BlockSpec((seq, GROUP_INNER), lambda b, g: (b, g)),
        out_shape=jax.ShapeDtypeStruct((batch * seq, SSM_INNER), F32),
        scratch_shapes=[pltpu.VMEM((nc, L, GROUP_COLS), F32),
                        pltpu.VMEM((nc, HEADS_PER_GROUP // 2, 2 * L, L), BF16),
                        pltpu.VMEM((nc, SSM_STATE, GROUP_INNER), F32),
                        pltpu.VMEM((nc, SSM_STATE, GROUP_INNER), F32),
                        pltpu.VMEM((nc, DT_ROWS, L), F32),
                        pltpu.VMEM((nc, DT_ROWS, L), F32),
                        pltpu.VMEM((nc, DT_ROWS, L), F32),
                        pltpu.VMEM((nc, L, SEG_K), BF16),
                        pltpu.VMEM((SSD_UNROLL, SEG_K, SEG_BLOCKS * L), BF16),
                        pltpu.VMEM((3, nc, DT_ROWS, L), F32)],
        compiler_params=pltpu.CompilerParams(dimension_semantics=("arbitrary", "arbitrary"),
                                             vmem_limit_bytes=VMEM_LIMIT_SSD),
        name="ssd",
    )(xbc, xbc, xbc, dtc, sh, place, lhs_ones, rhs_const, cw, cw, cw, cbias, cbias, cbias,
      adt, dskip)


def _mlp_kernel(x_ref, attn_ref, ssm_ref, z_ref, gat_ref, gn_ref, wo_ref, gm_ref, wup_ref, wdn_ref, o_ref):
    for sub in range(x_ref.shape[0] // MLP_SUBTILE):
        rs = slice(sub * MLP_SUBTILE, (sub + 1) * MLP_SUBTILE)
        a = attn_ref[rs, :]
        an = (a * _inv_rms(a) * gat_ref[...]).astype(BF16)
        zc = z_ref[rs, :].astype(F32)
        y = ssm_ref[rs, :] * (zc * jax.nn.sigmoid(zc))
        yn = jnp.concatenate([y[:, g * GROUP_INNER:(g + 1) * GROUP_INNER]
                              * _inv_rms(y[:, g * GROUP_INNER:(g + 1) * GROUP_INNER]) for g in range(SSM_GROUPS)],
                             axis=1) * gn_ref[...]
        mix = _dot(an, wo_ref[0:ATTN_WIDTH, :]) + _dot(yn.astype(BF16), wo_ref[ATTN_WIDTH:D_MIX, :])
        x1 = x_ref[rs, :] + mix
        hm = (x1 * _inv_rms(x1) * gm_ref[...]).astype(BF16)
        acc = jnp.zeros_like(x1)
        for c in range(D_FF // FF_CHUNK):
            cols = slice(c * FF_CHUNK, (c + 1) * FF_CHUNK)
            u = _dot(hm, wup_ref[:, cols])
            acc = acc + _dot(jnp.square(jnp.maximum(u, 0.0)).astype(BF16), wdn_ref[cols, :])
        o_ref[rs, :] = x1 + acc


def _mlp_call(x2, attn, ssm, z, gat, gn, wo, gm, wup, wdn):
    t = x2.shape[0]
    row = lambda w: pl.BlockSpec((TM_MLP, w), lambda i: (i, 0))
    full = lambda a: pl.BlockSpec(a.shape, lambda i: (0, 0), pipeline_mode=pl.Buffered(1))
    return pl.pallas_call(
        _mlp_kernel,
        grid=(t // TM_MLP,),
        in_specs=[row(D_MODEL), row(ATTN_WIDTH), row(SSM_INNER), row(SSM_INNER), full(gat), full(gn), full(wo),
                  full(gm), full(wup), full(wdn)],
        out_specs=row(D_MODEL),
        out_shape=jax.ShapeDtypeStruct((t, D_MODEL), F32),
        compiler_params=pltpu.CompilerParams(dimension_semantics=("arbitrary",),
                                             vmem_limit_bytes=VMEM_LIMIT),
        name="outproj_mlp",
    )(x2, attn, ssm, z, gat, gn, wo, gm, wup, wdn)


def _swap_halves(g):
    half = QK_ROPE_DIM // 2
    return jnp.concatenate([g[..., half:], g[..., :half]], axis=-1)


def _lane_bcast(v):
    return jnp.broadcast_to(v[..., None], v.shape + (LANES,)).astype(F32)


def _layer(x2, cs, batch, seq, ln_mix_g, stacked_proj_weights, q_a_norm_g, kv_a_norm_g, q_norm_g,
           k_norm_g, attn_out_norm_g, conv_w, conv_b, a_log_fwd, a_log_bwd, dt_bias_fwd, dt_bias_bwd,
           d_skip, ssm_norm_g, w_out, ln_mlp_g, w_mlp_up, w_mlp_down):
    win, wuq, wuk, wuv = _win_prep_call(*stacked_proj_weights)
    ones_v = np.ones((ATTN_HEADS // 2, V_HEAD_DIM), np.float32)
    vone = jnp.asarray(np.stack([np.concatenate([0 * ones_v, ones_v], axis=-1),
                                 np.concatenate([ones_v, 0 * ones_v], axis=-1)], axis=1).reshape(1, -1))
    scale = QK_HEAD_DIM ** -0.5 * np.log2(np.e)
    zeros_nope = jnp.zeros((QK_NOPE_DIM,), k_norm_g.dtype)
    k_rope_g = k_norm_g[QK_NOPE_DIM:]
    hg = jnp.concatenate([q_norm_g * scale, _swap_halves(q_norm_g[QK_NOPE_DIM:]) * scale,
                          k_norm_g[:QK_NOPE_DIM], zeros_nope,
                          zeros_nope, k_rope_g, _swap_halves(k_rope_g),
                          jnp.zeros(((SUBLANES - 3) * HEAD_PAD,), k_norm_g.dtype)]).reshape(SUBLANES, HEAD_PAD)

    q, k, v, z, xbc, dtc = _inproj_call(
        x2, cs, ln_mix_g[None, :], win, q_a_norm_g[None, :], wuq, kv_a_norm_g[None, :], wuk, wuv, vone,
        hg)

    attn, (wo_b, wup_b, wdn_b) = _attn_call(q, k, v, (w_out, w_mlp_up, w_mlp_down), batch, seq)

    cw = jnp.pad(conv_w[:, 0, :], ((0, SUBLANES - SSM_CONV), (0, 0)))
    cbias = conv_b[None, :]
    adt = jnp.stack([a_log_fwd, a_log_bwd, dt_bias_fwd, dt_bias_bwd]).reshape(2, 2, SSM_GROUPS, HEADS_PER_GROUP)
    adt = _lane_bcast(adt.transpose(0, 2, 1, 3).reshape(2, 2 * SSM_HEADS))
    dskip = jnp.repeat(d_skip, SSM_HEAD_DIM)[None, :]
    ssm = _ssd_call(xbc, dtc, cw, cbias, adt, dskip, batch, seq)

    return _mlp_call(x2, attn, ssm, z, attn_out_norm_g[None, :], ssm_norm_g[None, :], wo_b,
                     ln_mlp_g[None, :], wup_b, wdn_b)


def _rope_table(positions):
    inv_freq = 1.0 / (ROPE_THETA ** (jnp.arange(0, QK_ROPE_DIM, 2, dtype=F32) / QK_ROPE_DIM))
    ang = inv_freq[:, None] * positions.astype(F32).reshape(1, -1)
    return jnp.concatenate([jnp.cos(ang), jnp.sin(ang)], axis=0)


def kernel(x, positions, ln_mix_g, w_in, q_a_norm_g, w_uq, kv_a_norm_g, w_ukv, q_norm_g, k_norm_g,
           attn_out_norm_g, conv_w, conv_b, a_log_fwd, a_log_bwd, dt_bias_fwd, dt_bias_bwd, d_skip,
           ssm_norm_g, w_out, ln_mlp_g, w_mlp_up, w_mlp_down):
    batch, seq, d = x.shape
    assert d == D_MODEL and seq % TQ == 0 and (batch * seq) % TM_IN == 0 and (batch * seq) % TM_MLP == 0
    cs = _rope_table(positions)
    x2 = x.reshape(batch * seq, d)
    for l in range(ln_mix_g.shape[0]):
        x2 = _layer(x2, cs, batch, seq, ln_mix_g[l], (w_in, w_uq, w_ukv, l), q_a_norm_g[l], kv_a_norm_g[l],
                    q_norm_g[l], k_norm_g[l], attn_out_norm_g[l], conv_w[l], conv_b[l],
                    a_log_fwd[l], a_log_bwd[l], dt_bias_fwd[l], dt_bias_bwd[l], d_skip[l], ssm_norm_g[l],
                    w_out[l], ln_mlp_g[l], w_mlp_up[l], w_mlp_down[l])
    return x2.reshape(batch, seq, d)
```

```python
import numpy as np
import jax
import jax.numpy as jnp
from jax import lax
from jax.experimental import pallas as pl
from jax.experimental.pallas import tpu as pltpu

F32 = jnp.float32
BF16 = jnp.bfloat16

D_MODEL = 1024
ATTN_HEADS = 8
QK_NOPE_DIM = 64
QK_ROPE_DIM = 32
QK_HEAD_DIM = QK_NOPE_DIM + QK_ROPE_DIM
V_HEAD_DIM = 64
Q_LORA_RANK = D_MODEL // 4
KV_LORA_RANK = D_MODEL // 8
ROPE_THETA = 10000.0
ATTN_WIDTH = ATTN_HEADS * V_HEAD_DIM
SSM_HEADS = 8
SSM_HEAD_DIM = 64
SSM_INNER = SSM_HEADS * SSM_HEAD_DIM
SSM_GROUPS = 2
SSM_STATE = 128
SSM_CONV = 5
SSM_CHUNK = 128
SSM_CONV_CH = SSM_INNER + 2 * SSM_GROUPS * SSM_STATE
D_MIX = ATTN_WIDTH + SSM_INNER
D_FF = 4 * D_MODEL
EPS = 1e-6

LANES = 128
SUBLANES = 8
HEAD_PAD = LANES

HEADS_PER_GROUP = SSM_HEADS // SSM_GROUPS
GROUP_INNER = SSM_INNER // SSM_GROUPS
GROUP_COLS = GROUP_INNER + 2 * SSM_STATE
DT_ROWS = 2 * HEADS_PER_GROUP
BF16_ROWS = 16
SEG_K_PER_ROW = 8
SEG_K = DT_ROWS * SEG_K_PER_ROW
SEG_BLOCKS = DT_ROWS + HEADS_PER_GROUP
LOG2E = float(np.log2(np.e))
CONV_WIN = 2 * SSM_CHUNK
CONV_WIN_LEAD = SSM_CHUNK // 2
CONV_SHIFTED_TAPS = tuple(k for k in range(SSM_CONV) if k != SSM_CONV // 2)

COL_CKV = Q_LORA_RANK
COL_MISC = COL_CKV + KV_LORA_RANK
COL_Z = COL_MISC + LANES
COL_XBC = COL_Z + SSM_INNER
IN_COLS = COL_XBC + SSM_CONV_CH

TM_IN = 1024
IN_SUBTILE = 512
TQ = 1024
ATTN_HEADS_PER_STEP = 8
TM_MLP = 1024
MLP_SUBTILE = 512
FF_CHUNK = 1024
SSD_UNROLL = 16
CONV_AHEAD = 2
PASS_A_UNROLL = 14
WIN_PREP_STEPS = 4
VMEM_LIMIT = 56 * 1024 * 1024


def _inv_rms(x):
    n = x.shape[-1]
    x2 = x * x
    acc = x2[:, 0:LANES]
    for i in range(1, n // LANES):
        acc = acc + x2[:, i * LANES:(i + 1) * LANES]
    return lax.rsqrt(jnp.sum(acc, axis=-1, keepdims=True) * (1.0 / n) + EPS)


def _dot(a, b):
    return jnp.dot(a, b, preferred_element_type=F32)


def _dot_nt(a, b):
    return lax.dot_general(a, b, (((1,), (1,)), ((), ())), preferred_element_type=F32)


def _inproj_kernel(x_ref, cs_ref, g_ref, win_ref, gqa_ref, wuq_ref, gkva_ref, wuk_ref, wuv_ref, vone_ref,
                   hg_ref,
                   q_ref, k_ref, v_ref, z_ref, xbc_ref, dt_ref):
    for sub in range(x_ref.shape[0] // IN_SUBTILE):
        _inproj_rows(sub * IN_SUBTILE, x_ref, cs_ref, g_ref, win_ref, gqa_ref, wuq_ref, gkva_ref, wuk_ref,
                     wuv_ref, vone_ref, hg_ref, q_ref, k_ref, v_ref, z_ref, xbc_ref, dt_ref)


def _inproj_rows(r0, x_ref, cs_ref, g_ref, win_ref, gqa_ref, wuq_ref, gkva_ref, wuk_ref, wuv_ref, vone_ref,
                 hg_ref, q_ref, k_ref, v_ref, z_ref, xbc_ref, dt_ref):
    tm = IN_SUBTILE
    rs = slice(r0, r0 + tm)
    x = x_ref[rs, :]
    h = (x * _inv_rms(x) * g_ref[...]).astype(BF16)
    big = _dot(h, win_ref[...])
    z_ref[rs, :] = big[:, COL_Z:COL_XBC].astype(BF16)
    xbc_ref[rs, :] = big[:, COL_XBC:IN_COLS].astype(BF16)
    misc = big[:, COL_MISC:COL_Z]
    for c in range(tm // SSM_CHUNK):
        dt_ref[r0 // SSM_CHUNK + c] = misc[c * SSM_CHUNK:(c + 1) * SSM_CHUNK, :].T[0:2 * SSM_HEADS, :]

    cq = big[:, 0:COL_CKV]
    ckv = big[:, COL_CKV:COL_MISC]
    cqn = (cq * _inv_rms(cq) * gqa_ref[...]).astype(BF16)
    ckvn = (ckv * _inv_rms(ckv) * gkva_ref[...]).astype(BF16)
    q_pre = _dot(cqn, wuq_ref[...])
    k_pre = _dot(ckvn, wuk_ref[...])
    v_ref[rs, :] = (_dot(ckvn, wuv_ref[...]) + vone_ref[...]).astype(BF16)

    half = QK_ROPE_DIM // 2
    zpad = jnp.zeros((HEAD_PAD - QK_ROPE_DIM, SSM_CHUNK), F32)
    tt = jnp.concatenate(
        [jnp.concatenate([cs_ref[:, r0 + c * SSM_CHUNK:r0 + (c + 1) * SSM_CHUNK], zpad], axis=0).T
         for c in range(tm // SSM_CHUNK)], axis=0)
    lane_t = lax.broadcasted_iota(jnp.int32, (tm, HEAD_PAD), 1)
    cs = jnp.where(lane_t < QK_NOPE_DIM, 1.0,
                   jnp.where(lane_t < QK_NOPE_DIM + half, pltpu.roll(tt, QK_NOPE_DIM, 1),
                             jnp.where(lane_t < QK_HEAD_DIM + half, pltpu.roll(tt, QK_NOPE_DIM + half, 1),
                                       pltpu.roll(tt, QK_HEAD_DIM, 1))))
    lane = lax.broadcasted_iota(jnp.int32, (1, HEAD_PAD), 1)
    in_head = (lane < QK_HEAD_DIM).astype(F32)
    is_rope = ((lane >= QK_NOPE_DIM) & (lane < QK_HEAD_DIM)).astype(F32)
    inv_d = 1.0 / QK_HEAD_DIM

    gcq = hg_ref[0:1, :] * cs
    for hh in range(ATTN_HEADS):
        sl = slice(hh * HEAD_PAD, (hh + 1) * HEAD_PAD)
        qh = q_pre[:, sl]
        ssq = jnp.sum(qh * qh * in_head, axis=-1, keepdims=True)
        q_ref[rs, sl] = (qh * lax.rsqrt(ssq * inv_d + EPS) * gcq).astype(BF16)

    ab = misc * (hg_ref[2:3, :] * cs)
    lane2 = lax.broadcasted_iota(jnp.int32, (tm, HEAD_PAD), 1)
    swapped = jnp.where(lane2 < QK_HEAD_DIM, pltpu.roll(ab, HEAD_PAD - QK_ROPE_DIM, 1),
                        pltpu.roll(ab, QK_ROPE_DIM, 1))
    s_both = jnp.where(lane2 >= QK_NOPE_DIM, ab + swapped, 0.0)
    ssq_pe = jnp.sum(misc * misc * is_rope, axis=-1, keepdims=True)
    gkn = hg_ref[1:2, :]
    for hh in range(ATTN_HEADS):
        sl = slice(hh * HEAD_PAD, (hh + 1) * HEAD_PAD)
        kh = k_pre[:, sl]
        ssq = jnp.sum(kh * kh, axis=-1, keepdims=True) + ssq_pe
        k_ref[rs, sl] = ((kh * gkn + s_both) * lax.rsqrt(ssq * inv_d + EPS)).astype(BF16)


def _win_prep_kernel(wt_ref, wq_ref, wkv_ref, o_ref, wuq_ref, wuk_ref, wuv_ref):
    half = QK_ROPE_DIM // 2

    @pl.when(pl.program_id(0) == 0)
    def _up_projection_tiles():
        wq = wq_ref[...]
        q_pieces = []
        for hh in range(ATTN_HEADS):
            head = wq[:, hh * QK_HEAD_DIM:(hh + 1) * QK_HEAD_DIM]
            rope = head[:, QK_NOPE_DIM:]
            q_pieces += [head, -rope[:, half:], rope[:, :half]]
        wuq_ref[...] = jnp.concatenate(q_pieces, axis=1).astype(BF16)
        wkv = wkv_ref[...]
        zeros = jnp.zeros((wkv.shape[0], V_HEAD_DIM), F32)
        k_pieces, v_pieces = [], []
        for hh in range(ATTN_HEADS):
            nope = wkv[:, hh * HEAD_PAD:hh * HEAD_PAD + QK_NOPE_DIM]
            val = wkv[:, hh * HEAD_PAD + QK_NOPE_DIM:(hh + 1) * HEAD_PAD]
            k_pieces += [nope, zeros]
            v_pieces += [val, zeros] if hh % 2 == 0 else [zeros, val]
        wuk_ref[...] = jnp.concatenate(k_pieces, axis=1).astype(BF16)
        wuv_ref[...] = jnp.concatenate(v_pieces, axis=1).astype(BF16)

    cols = wt_ref.shape[1]
    o_kpe = Q_LORA_RANK + KV_LORA_RANK
    o_z = o_kpe + QK_ROPE_DIM
    o_dt = o_z + SSM_INNER + SSM_CONV_CH
    hpg = HEADS_PER_GROUP
    assert SSM_HEADS == SUBLANES and SSM_GROUPS == 2
    dt_f = wt_ref[o_dt:o_dt + SSM_HEADS, :]
    dt_b = wt_ref[o_dt + SSM_HEADS:o_dt + 2 * SSM_HEADS, :]
    top = lax.broadcasted_iota(jnp.int32, (SUBLANES, cols), 0) < hpg
    dt_g0 = jnp.where(top, dt_f, pltpu.roll(dt_b, hpg, 0))
    dt_g1 = jnp.where(top, pltpu.roll(dt_f, hpg, 0), dt_b)
    kpe = wt_ref[o_kpe:o_z, :]
    rows = jnp.concatenate(
        [wt_ref[0:o_kpe, :], dt_g0, dt_g1, jnp.zeros((QK_NOPE_DIM - 2 * SSM_HEADS, cols), F32),
         kpe, -kpe[half:, :], kpe[:half, :], wt_ref[o_z:o_dt, :]], axis=0)
    for ct in range(cols // LANES):
        o_ref[ct * LANES:(ct + 1) * LANES, :] = jnp.concatenate(
            [rows[rt * LANES:(rt + 1) * LANES, ct * LANES:(ct + 1) * LANES].T for rt in range(IN_COLS // LANES)],
            axis=1).astype(BF16)


def _win_prep_call(w_in3, w_uq3, w_ukv3, layer):
    depth, d, n = w_in3.shape
    cols = d // WIN_PREP_STEPS
    whole = lambda a: pl.BlockSpec(a.shape[1:], lambda i: (layer, 0))
    head_cols = ATTN_HEADS * HEAD_PAD
    return pl.pallas_call(
        _win_prep_kernel,
        grid=(WIN_PREP_STEPS,),
        in_specs=[pl.BlockSpec((n, cols), lambda i: (layer, i)), whole(w_uq3), whole(w_ukv3)],
        out_specs=[pl.BlockSpec((cols, IN_COLS), lambda i: (i, 0)),
                   pl.BlockSpec((Q_LORA_RANK, head_cols), lambda i: (0, 0)),
                   pl.BlockSpec((KV_LORA_RANK, head_cols), lambda i: (0, 0)),
                   pl.BlockSpec((KV_LORA_RANK, head_cols), lambda i: (0, 0))],
        out_shape=[jax.ShapeDtypeStruct((d, IN_COLS), BF16),
                   jax.ShapeDtypeStruct((Q_LORA_RANK, head_cols), BF16),
                   jax.ShapeDtypeStruct((KV_LORA_RANK, head_cols), BF16),
                   jax.ShapeDtypeStruct((KV_LORA_RANK, head_cols), BF16)],
        compiler_params=pltpu.CompilerParams(dimension_semantics=("arbitrary",), vmem_limit_bytes=VMEM_LIMIT),
        name="win_prep",
    )(jnp.transpose(w_in3, (0, 2, 1)).reshape(depth * n, d),
      w_uq3.reshape(depth * w_uq3.shape[1], w_uq3.shape[2]),
      w_ukv3.reshape(depth * w_ukv3.shape[1], w_ukv3.shape[2]))


def _inproj_call(x2, cs, g, win, gqa, wuq, gkva, wuk, wuv, vone, hg):
    t = x2.shape[0]
    cpt = TM_IN // SSM_CHUNK
    full = lambda a: pl.BlockSpec(a.shape, lambda i: (0,) * a.ndim, pipeline_mode=pl.Buffered(1))
    row = lambda w: pl.BlockSpec((TM_IN, w), lambda i: (i, 0))
    return pl.pallas_call(
        _inproj_kernel,
        grid=(t // TM_IN,),
        in_specs=[row(D_MODEL), pl.BlockSpec((QK_ROPE_DIM, TM_IN), lambda i: (0, i)), full(g), full(win), full(gqa), full(wuq), full(gkva),
                  full(wuk), full(wuv), full(vone), full(hg)],
        out_specs=[row(ATTN_HEADS * HEAD_PAD), row(ATTN_HEADS * HEAD_PAD), row(ATTN_HEADS * HEAD_PAD),
                   row(SSM_INNER), row(SSM_CONV_CH),
                   pl.BlockSpec((cpt, 2 * SSM_HEADS, SSM_CHUNK), lambda i: (i, 0, 0))],
        out_shape=[jax.ShapeDtypeStruct((t, ATTN_HEADS * HEAD_PAD), BF16),
                   jax.ShapeDtypeStruct((t, ATTN_HEADS * HEAD_PAD), BF16),
                   jax.ShapeDtypeStruct((t, ATTN_HEADS * HEAD_PAD), BF16),
                   jax.ShapeDtypeStruct((t, SSM_INNER), BF16),
                   jax.ShapeDtypeStruct((t, SSM_CONV_CH), BF16),
                   jax.ShapeDtypeStruct((t // SSM_CHUNK, 2 * SSM_HEADS, SSM_CHUNK), F32)],
        compiler_params=pltpu.CompilerParams(dimension_semantics=("arbitrary",),
                                             vmem_limit_bytes=VMEM_LIMIT),
        name="inproj",
    )(x2, cs, g, win, gqa, wuq, gkva, wuk, wuv, vone, hg)


def _attn_kernel(q_ref, k_ref, v_ref, *refs):
    n_w = (len(refs) - 1) // 2
    w_in_refs, o_ref, w_out_refs = refs[:n_w], refs[n_w], refs[n_w + 1:]
    for w_in_ref, w_out_ref in zip(w_in_refs, w_out_refs):
        w_out_ref[...] = w_in_ref[...].astype(BF16)

    lane = lax.broadcasted_iota(jnp.int32, (q_ref.shape[0], HEAD_PAD), 1)
    for jp in range(ATTN_HEADS_PER_STEP // 2):
        sls = [slice(j * HEAD_PAD, (j + 1) * HEAD_PAD) for j in (2 * jp, 2 * jp + 1)]
        ss = [_dot_nt(q_ref[:, sl], k_ref[:, sl]) for sl in sls]
        ps = [jnp.exp2(s - jnp.max(s, axis=-1, keepdims=True)).astype(BF16) for s in ss]
        accs = [_dot(p, v_ref[:, sl]) for p, sl in zip(ps, sls)]
        res = [acc / pltpu.roll(acc, V_HEAD_DIM, 1) for acc in accs]
        o_ref[:, jp * HEAD_PAD:(jp + 1) * HEAD_PAD] = jnp.where(lane < V_HEAD_DIM, res[0], res[1])


def _attn_call(q, k, v, later_weights, batch, seq):
    n_q = seq // TQ
    hps = ATTN_HEADS_PER_STEP
    assert ATTN_HEADS == hps
    n_steps = batch * n_q

    def rows_spec(w):
        assert w.shape[0] % (n_steps * BF16_ROWS) == 0
        return pl.BlockSpec((w.shape[0] // n_steps, w.shape[1]), lambda b, hp, i: (b * n_q + i, 0))
    w_specs = [rows_spec(w) for w in later_weights]
    outs = pl.pallas_call(
        _attn_kernel,
        grid=(batch, ATTN_HEADS // hps, n_q),
        in_specs=[pl.BlockSpec((TQ, hps * HEAD_PAD), lambda b, hp, i: (b * n_q + i, hp)),
                  pl.BlockSpec((seq, hps * HEAD_PAD), lambda b, hp, i: (b, hp)),
                  pl.BlockSpec((seq, hps * HEAD_PAD), lambda b, hp, i: (b, hp))] + w_specs,
        out_specs=[pl.BlockSpec((TQ, hps * V_HEAD_DIM), lambda b, hp, i: (b * n_q + i, hp))] + w_specs,
        out_shape=[jax.ShapeDtypeStruct((batch * seq, ATTN_WIDTH), F32)]
                  + [jax.ShapeDtypeStruct(w.shape, BF16) for w in later_weights],
        compiler_params=pltpu.CompilerParams(
            dimension_semantics=("arbitrary", "arbitrary", "arbitrary"),
            vmem_limit_bytes=VMEM_LIMIT),
        name="attention",
    )(q, k, v, *later_weights)
    return outs[0], outs[1:]


def _split3(x):
    hi = x.astype(BF16)
    r1 = x - hi.astype(F32)
    mid = r1.astype(BF16)
    lo = (r1 - mid.astype(F32)).astype(BF16)
    return hi, mid, lo


def _ssd_kernel(x_ref, b_ref, c_ref, dt_ref, sh_ref, place_ref, lhs_ones_ref, rhs_const_ref,
                cwx_ref, cwb_ref, cwc_ref, cbx_ref, cbb_ref, cbc_ref, adt_ref, dskip_ref,
                o_ref, xc_scr, xm_scr, nsf_scr, nsb_scr, dtv_scr, w_scr, dec_scr, lhs_scr, rhs_scr, rpart_scr):
    nc = dt_ref.shape[0]
    seq = x_ref.shape[0]
    L = SSM_CHUNK
    hpg = HEADS_PER_GROUP
    npair = hpg // 2
    P = SSM_HEAD_DIM
    x_cols = slice(0, GROUP_INNER)
    b_cols = slice(GROUP_INNER, GROUP_INNER + SSM_STATE)
    c_cols = slice(GROUP_INNER + SSM_STATE, GROUP_COLS)
    pair_cols = [slice(j * L, (j + 1) * L) for j in range(npair)]

    @pl.when((pl.program_id(0) == 0) & (pl.program_id(1) == 0))
    def _init_rhs():
        for u in range(SSD_UNROLL):
            rhs_scr[u] = rhs_const_ref[...]

    a_neg = -jnp.exp(adt_ref[0]) * LOG2E
    dtv = jax.nn.softplus(dt_ref[...] + adt_ref[1][None])
    dtv_scr[...] = dtv
    da2 = (dtv * a_neg[None]).reshape(nc * DT_ROWS, L)
    ri = lax.broadcasted_iota(jnp.int32, (L, L), 0)
    ci = lax.broadcasted_iota(jnp.int32, (L, L), 1)
    upper = (ri <= ci).astype(BF16)
    lower = (ri >= ci).astype(BF16)
    tri = jnp.concatenate([upper, lower], axis=1)
    cs_fb = sum(_dot(p, tri) for p in _split3(da2))
    rowsel = (ri & hpg) == 0
    cs2 = jnp.where(rowsel, cs_fb[:, 0:L], cs_fb[:, L:2 * L])
    colcs = cs2.T

    col_parts = jnp.concatenate(_split3(colcs), axis=1)
    half_n = place_ref.shape[1] // 2
    for hf in range(2):
        cols = slice(hf * half_n, (hf + 1) * half_n)
        lhs_half = _dot(col_parts, place_ref[:, cols]) + lhs_ones_ref[:, cols]
        for c in range(nc // 2):
            lhs_scr[hf * (nc // 2) + c] = lhs_half[:, c * L:c * L + SEG_K].astype(BF16)
    for i, part in enumerate(_split3(-cs2)):
        rpart_scr[i] = part.astype(F32).reshape(nc, DT_ROWS, L)

    def lane_bcast(col):
        return jnp.broadcast_to(col, (col.shape[0], L))

    tot2 = jnp.where(rowsel, lane_bcast(cs2[:, L - 1:L]), lane_bcast(cs2[:, 0:1]))
    dec_scr[...] = jnp.exp2(tot2).reshape(nc, DT_ROWS, L)
    w_scr[...] = (dtv.reshape(nc * DT_ROWS, L) * jnp.exp2(tot2 - cs2)).reshape(nc, DT_ROWS, L)

    lane_t = lax.broadcasted_iota(jnp.int32, (L, L), 1)
    low_half = lane_t < P

    cw_halves = (cwx_ref[...], jnp.concatenate([cwb_ref[...], cwc_ref[...]], axis=1))
    cb_halves = (cbx_ref[...], jnp.concatenate([cbb_ref[...], cbc_ref[...]], axis=1))

    def rows(hf, start, size):
        if hf == 0:
            return x_ref[pl.ds(start, size), :]
        return jnp.concatenate([b_ref[pl.ds(start, size), :], c_ref[pl.ds(start, size), :]], axis=1)

    def conv_stage(c):
        base = pl.multiple_of(c * L, L)
        ws = pl.multiple_of(jnp.clip(c * L - CONV_WIN_LEAD, 0, seq - CONV_WIN), CONV_WIN_LEAD)
        variant = jnp.where(c == 0, 0, jnp.where(c == nc - 1, 2, 1))
        sh = sh_ref[variant]
        halves = []
        for hf in range(2):
            cw = cw_halves[hf]
            shifted = _dot(sh, rows(hf, ws, CONV_WIN))
            acc = cb_halves[hf] + cw[SSM_CONV // 2:SSM_CONV // 2 + 1, :] * rows(hf, base, L).astype(F32)
            for t, kk in enumerate(CONV_SHIFTED_TAPS):
                acc = acc + cw[kk:kk + 1, :] * shifted[t * L:(t + 1) * L, :]
            halves.append(acc * jax.nn.sigmoid(acc))
        xc = jnp.concatenate(halves, axis=1)
        xc_scr[c] = xc
        for j in range(npair):
            xp = xc[:, pair_cols[j]]
            xm_scr[c, j] = jnp.concatenate([jnp.where(low_half, xp, 0.0), jnp.where(low_half, 0.0, xp)],
                                           axis=0).astype(BF16)

    def state_loads(c):
        return xc_scr[c, :, b_cols], [xm_scr[c, j] for j in range(npair)]

    def state_stage(c, loaded):
        b_tok, xm = loaded
        bt = b_tok.T
        for d, ns_scr in ((0, nsf_scr), (1, nsb_scr)):
            w = w_scr[c, d * hpg:(d + 1) * hpg, :]
            ns_scr[c] = jnp.concatenate(
                [_dot(jnp.concatenate([(bt * w[2 * j:2 * j + 1, :]).astype(BF16),
                                       (bt * w[2 * j + 1:2 * j + 2, :]).astype(BF16)], axis=1), xm[j])
                 for j in range(npair)], axis=1)

    for c in range(CONV_AHEAD):
        conv_stage(c)

    def pass_a(c, carry):
        loaded = state_loads(c)
        conv_stage(c + CONV_AHEAD)
        state_stage(c, loaded)
        return carry
    lax.fori_loop(0, nc - CONV_AHEAD, pass_a, 0, unroll=PASS_A_UNROLL)
    for c in range(nc - CONV_AHEAD, nc):
        state_stage(c, state_loads(c))

    lane_r = lax.broadcasted_iota(jnp.int32, (1, L), 1)

    def decay_row(c, d):
        dec = dec_scr[c, d * hpg:(d + 1) * hpg, :]
        return jnp.concatenate([jnp.where(lane_r < P, dec[2 * j:2 * j + 1, :], dec[2 * j + 1:2 * j + 2, :])
                                for j in range(npair)], axis=1)

    def pass_b(d, ns_scr):
        def body(i, st):
            c = i if d == 0 else nc - 1 - i
            new = ns_scr[c]
            ns_scr[c] = st
            return st * decay_row(c, d) + new
        lax.fori_loop(0, nc, body, jnp.zeros((SSM_STATE, GROUP_INNER), F32), unroll=True)
    pass_b(0, nsf_scr)
    pass_b(1, nsb_scr)

    mask_f = ci <= ri
    mask_b = ci >= ri
    neg_inf = jnp.float32(-jnp.inf)
    sub16 = lax.broadcasted_iota(jnp.int32, (BF16_ROWS, L), 0)

    def pass_c(c, carry):
        base = pl.multiple_of(c * L, L)
        xc = xc_scr[c]
        bm = xc[:, b_cols].astype(BF16)
        cm = xc[:, c_cols].astype(BF16)
        cb = _dot_nt(cm, bm)
        off_f = _dot(cm, nsf_scr[c].astype(BF16))
        off_b = _dot(cm, nsb_scr[c].astype(BF16))
        slot = c % SSD_UNROLL
        for hd in range(DT_ROWS):
            r0 = (hd % 2) * SEG_K_PER_ROW
            blk = jnp.where((sub16 >= r0) & (sub16 < r0 + 3), 1.0, 0.0)
            for i in range(3):
                blk = jnp.where(sub16 == r0 + 3 + i, rpart_scr[i, c, hd:hd + 1, :], blk)
            rhs_scr[slot, (hd // 2) * BF16_ROWS:(hd // 2 + 1) * BF16_ROWS, hd * L:(hd + 1) * L] = blk.astype(BF16)
        seg = _dot(lhs_scr[c], rhs_scr[slot])
        dt_f_r = dtv_scr[c, 0:hpg, :]
        dt_b_r = dtv_scr[c, hpg:DT_ROWS, :]
        ys = []
        for j in range(npair):
            ws_ = []
            for h in (2 * j, 2 * j + 1):
                ef = jnp.exp2(jnp.where(mask_f, seg[:, h * L:(h + 1) * L], neg_inf))
                eb = jnp.exp2(jnp.where(mask_b, seg[:, (hpg + h) * L:(hpg + h + 1) * L], neg_inf))
                ws_.append((cb * (ef * dt_f_r[h:h + 1, :] + eb * dt_b_r[h:h + 1, :])).astype(BF16))
            scale_f = jnp.exp2(seg[:, (DT_ROWS + j) * L:(DT_ROWS + j + 1) * L])
            scale_b = jnp.exp2(seg[:, (DT_ROWS + npair + j) * L:(DT_ROWS + npair + j + 1) * L])
            ys.append(_dot(jnp.concatenate(ws_, axis=1), xm_scr[c, j])
                      + scale_f * off_f[:, pair_cols[j]] + scale_b * off_b[:, pair_cols[j]])
        y = jnp.concatenate(ys, axis=1) + dskip_ref[...] * xc[:, x_cols]
        o_ref[pl.ds(base, L), :] = y
        return carry
    lax.fori_loop(0, nc, pass_c, 0, unroll=SSD_UNROLL)


def _conv_shift_matrices():
    out = np.zeros((3, len(CONV_SHIFTED_TAPS) * SSM_CHUNK, CONV_WIN), np.float32)
    for v, lead in enumerate((0, CONV_WIN_LEAD, CONV_WIN - SSM_CHUNK)):
        for i, kk in enumerate(CONV_SHIFTED_TAPS):
            for t in range(SSM_CHUNK):
                j = t + kk - SSM_CONV // 2 + lead
                if 0 <= j < CONV_WIN:
                    out[v, i * SSM_CHUNK + t, j] = 1.0
    return out


def _seg_matmul_constants(nc):
    L, K, half = SSM_CHUNK, SEG_K_PER_ROW, SSM_HEAD_DIM
    place = np.zeros((3 * L, nc * L), np.float32)
    lhs_ones = np.zeros((1, nc * L), np.float32)
    for c in range(nc):
        for hd in range(DT_ROWS):
            for i in range(3):
                place[i * L + c * DT_ROWS + hd, c * L + hd * K + i] = 1.0
                lhs_ones[0, c * L + hd * K + 3 + i] = 1.0
    rhs = np.zeros((SEG_K, SEG_BLOCKS * L), np.float32)
    for hd in range(DT_ROWS):
        d, h = divmod(hd, HEADS_PER_GROUP)
        pair_block = DT_ROWS + d * (HEADS_PER_GROUP // 2) + h // 2
        lanes = slice(0, half) if h % 2 == 0 else slice(half, L)
        rhs[hd * K:hd * K + 3, hd * L:(hd + 1) * L] = 1.0
        rhs[hd * K:hd * K + 3, pair_block * L:(pair_block + 1) * L][:, lanes] = 1.0
    return place, lhs_ones, rhs


def _ssd_call(xbc, dtc, cw, cbias, adt, dskip, batch, seq):
    nc = seq // SSM_CHUNK
    L = SSM_CHUNK
    assert nc >= 3 and seq >= CONV_WIN and nc * DT_ROWS == L
    sh = jnp.asarray(_conv_shift_matrices(), BF16)
    place, lhs_ones, rhs_const = _seg_matmul_constants(nc)
    place = jnp.asarray(place, BF16)
    lhs_ones = jnp.asarray(lhs_ones, F32)
    rhs_const = jnp.asarray(rhs_const, BF16)
    const = lambda a: pl.BlockSpec(a.shape, lambda b, g: (0,) * a.ndim)
    b_blk0 = SSM_INNER // SSM_STATE
    c_blk0 = b_blk0 + SSM_GROUPS

    def xbc_views(rows_, batched):
        lead = (lambda b: b) if batched else (lambda b: 0)
        return [pl.BlockSpec((rows_, GROUP_INNER), lambda b, g: (lead(b), g)),
                pl.BlockSpec((rows_, SSM_STATE), lambda b, g: (lead(b), b_blk0 + g)),
                pl.BlockSpec((rows_, SSM_STATE), lambda b, g: (lead(b), c_blk0 + g))]
    return pl.pallas_call(
        _ssd_kernel,
        grid=(batch, SSM_GROUPS),
        in_specs=xbc_views(seq, True) + [
                  pl.BlockSpec((nc, DT_ROWS, L), lambda b, g: (b, g, 0)),
                  const(sh), const(place), const(lhs_ones), const(rhs_const)]
                 + xbc_views(SUBLANES, False) + xbc_views(1, False) + [
                  pl.BlockSpec((2, DT_ROWS, L), lambda b, g: (0, g, 0)),
                  pl.BlockSpec((1, GROUP_INNER), lambda b, g: (0, g))],
        out_specs=pl.BlockSpec((seq, GROUP_INNER), lambda b, g: (b, g)),
        out_shape=jax.ShapeDtypeStruct((batch * seq, SSM_INNER), F32),
        scratch_shapes=[pltpu.VMEM((nc, L, GROUP_COLS), F32),
                        pltpu.VMEM((nc, HEADS_PER_GROUP // 2, 2 * L, L), BF16),
                        pltpu.VMEM((nc, SSM_STATE, GROUP_INNER), F32),
                        pltpu.VMEM((nc, SSM_STATE, GROUP_INNER), F32),
                        pltpu.VMEM((nc, DT_ROWS, L), F32),
                        pltpu.VMEM((nc, DT_ROWS, L), F32),
                        pltpu.VMEM((nc, DT_ROWS, L), F32),
                        pltpu.VMEM((nc, L, SEG_K), BF16),
                        pltpu.VMEM((SSD_UNROLL, SEG_K, SEG_BLOCKS * L), BF16),
                        pltpu.VMEM((3, nc, DT_ROWS, L), F32)],
        compiler_params=pltpu.CompilerParams(dimension_semantics=("arbitrary", "arbitrary"),
                                             vmem_limit_bytes=VMEM_LIMIT),
        name="ssd",
    )(xbc, xbc, xbc, dtc, sh, place, lhs_ones, rhs_const, cw, cw, cw, cbias, cbias, cbias,
      adt, dskip)


def _mlp_kernel(x_ref, attn_ref, ssm_ref, z_ref, gat_ref, gn_ref, wo_ref, gm_ref, wup_ref, wdn_ref, o_ref):
    for sub in range(x_ref.shape[0] // MLP_SUBTILE):
        rs = slice(sub * MLP_SUBTILE, (sub + 1) * MLP_SUBTILE)
        a = attn_ref[rs, :]
        an = (a * _inv_rms(a) * gat_ref[...]).astype(BF16)
        zc = z_ref[rs, :].astype(F32)
        y = ssm_ref[rs, :] * (zc * jax.nn.sigmoid(zc))
        yn = jnp.concatenate([y[:, g * GROUP_INNER:(g + 1) * GROUP_INNER]
                              * _inv_rms(y[:, g * GROUP_INNER:(g + 1) * GROUP_INNER]) for g in range(SSM_GROUPS)],
                             axis=1) * gn_ref[...]
        mix = _dot(an, wo_ref[0:ATTN_WIDTH, :]) + _dot(yn.astype(BF16), wo_ref[ATTN_WIDTH:D_MIX, :])
        x1 = x_ref[rs, :] + mix
        hm = (x1 * _inv_rms(x1) * gm_ref[...]).astype(BF16)
        acc = jnp.zeros_like(x1)
        for c in range(D_FF // FF_CHUNK):
            cols = slice(c * FF_CHUNK, (c + 1) * FF_CHUNK)
            u = _dot(hm, wup_ref[:, cols])
            acc = acc + _dot(jnp.square(jnp.maximum(u, 0.0)).astype(BF16), wdn_ref[cols, :])
        o_ref[rs, :] = x1 + acc


def _mlp_call(x2, attn, ssm, z, gat, gn, wo, gm, wup, wdn):
    t = x2.shape[0]
    row = lambda w: pl.BlockSpec((TM_MLP, w), lambda i: (i, 0))
    full = lambda a: pl.BlockSpec(a.shape, lambda i: (0, 0), pipeline_mode=pl.Buffered(1))
    return pl.pallas_call(
        _mlp_kernel,
        grid=(t // TM_MLP,),
        in_specs=[row(D_MODEL), row(ATTN_WIDTH), row(SSM_INNER), row(SSM_INNER), full(gat), full(gn), full(wo),
                  full(gm), full(wup), full(wdn)],
        out_specs=row(D_MODEL),
        out_shape=jax.ShapeDtypeStruct((t, D_MODEL), F32),
        compiler_params=pltpu.CompilerParams(dimension_semantics=("arbitrary",),
                                             vmem_limit_bytes=VMEM_LIMIT),
        name="outproj_mlp",
    )(x2, attn, ssm, z, gat, gn, wo, gm, wup, wdn)


def _swap_halves(g):
    half = QK_ROPE_DIM // 2
    return jnp.concatenate([g[..., half:], g[..., :half]], axis=-1)


def _lane_bcast(v):
    return jnp.broadcast_to(v[..., None], v.shape + (LANES,)).astype(F32)


def _layer(x2, cs, batch, seq, ln_mix_g, stacked_proj_weights, q_a_norm_g, kv_a_norm_g, q_norm_g,
           k_norm_g, attn_out_norm_g, conv_w, conv_b, a_log_fwd, a_log_bwd, dt_bias_fwd, dt_bias_bwd,
           d_skip, ssm_norm_g, w_out, ln_mlp_g, w_mlp_up, w_mlp_down):
    win, wuq, wuk, wuv = _win_prep_call(*stacked_proj_weights)
    ones_v = np.ones((ATTN_HEADS // 2, V_HEAD_DIM), np.float32)
    vone = jnp.asarray(np.stack([np.concatenate([0 * ones_v, ones_v], axis=-1),
                                 np.concatenate([ones_v, 0 * ones_v], axis=-1)], axis=1).reshape(1, -1))
    scale = QK_HEAD_DIM ** -0.5 * np.log2(np.e)
    zeros_nope = jnp.zeros((QK_NOPE_DIM,), k_norm_g.dtype)
    k_rope_g = k_norm_g[QK_NOPE_DIM:]
    hg = jnp.concatenate([q_norm_g * scale, _swap_halves(q_norm_g[QK_NOPE_DIM:]) * scale,
                          k_norm_g[:QK_NOPE_DIM], zeros_nope,
                          zeros_nope, k_rope_g, _swap_halves(k_rope_g),
                          jnp.zeros(((SUBLANES - 3) * HEAD_PAD,), k_norm_g.dtype)]).reshape(SUBLANES, HEAD_PAD)

    q, k, v, z, xbc, dtc = _inproj_call(
        x2, cs, ln_mix_g[None, :], win, q_a_norm_g[None, :], wuq, kv_a_norm_g[None, :], wuk, wuv, vone,
        hg)

    attn, (wo_b, wup_b, wdn_b) = _attn_call(q, k, v, (w_out, w_mlp_up, w_mlp_down), batch, seq)

    cw = jnp.pad(conv_w[:, 0, :], ((0, SUBLANES - SSM_CONV), (0, 0)))
    cbias = conv_b[None, :]
    adt = jnp.stack([a_log_fwd, a_log_bwd, dt_bias_fwd, dt_bias_bwd]).reshape(2, 2, SSM_GROUPS, HEADS_PER_GROUP)
    adt = _lane_bcast(adt.transpose(0, 2, 1, 3).reshape(2, 2 * SSM_HEADS))
    dskip = jnp.repeat(d_skip, SSM_HEAD_DIM)[None, :]
    ssm = _ssd_call(xbc, dtc, cw, cbias, adt, dskip, batch, seq)

    return _mlp_call(x2, attn, ssm, z, attn_out_norm_g[None, :], ssm_norm_g[None, :], wo_b,
                     ln_mlp_g[None, :], wup_b, wdn_b)


def _rope_table(positions):
    inv_freq = 1.0 / (ROPE_THETA ** (jnp.arange(0, QK_ROPE_DIM, 2, dtype=F32) / QK_ROPE_DIM))
    ang = inv_freq[:, None] * positions.astype(F32).reshape(1, -1)
    return jnp.concatenate([jnp.cos(ang), jnp.sin(ang)], axis=0)


def kernel(x, positions, ln_mix_g, w_in, q_a_norm_g, w_uq, kv_a_norm_g, w_ukv, q_norm_g, k_norm_g,
           attn_out_norm_g, conv_w, conv_b, a_log_fwd, a_log_bwd, dt_bias_fwd, dt_bias_bwd, d_skip,
           ssm_norm_g, w_out, ln_mlp_g, w_mlp_up, w_mlp_down):
    batch, seq, d = x.shape
    assert d == D_MODEL and seq % TQ == 0 and (batch * seq) % TM_IN == 0 and (batch * seq) % TM_MLP == 0
    cs = _rope_table(positions)
    x2 = x.reshape(batch * seq, d)
    for l in range(ln_mix_g.shape[0]):
        x2 = _layer(x2, cs, batch, seq, ln_mix_g[l], (w_in, w_uq, w_ukv, l), q_a_norm_g[l], kv_a_norm_g[l],
                    q_norm_g[l], k_norm_g[l], attn_out_norm_g[l], conv_w[l], conv_b[l],
                    a_log_fwd[l], a_log_bwd[l], dt_bias_fwd[l], dt_bias_bwd[l], d_skip[l], ssm_norm_g[l],
                    w_out[l], ln_mlp_g[l], w_mlp_up[l], w_mlp_down[l])
    return x2.reshape(batch, seq, d)
```
